```python
import math
import jax, jax.numpy as jnp
from jax import lax
import numpy as np

D_MODEL = 1024
BATCH = 8
SEQ = 4096
DEPTH = 4

N_A_LAYERS = DEPTH // 2
N_B_LAYERS = DEPTH - N_A_LAYERS
SSM_GROUP = 16
N_GROUPS = D_MODEL // SSM_GROUP
SSM_STATE = 64
DT_MIN = 1e-3
DT_MAX = 1e-1
N_HEADS = 16
HEAD_DIM = D_MODEL // N_HEADS
Q_BLOCK = 128
ATTN_SCALE = HEAD_DIM ** -0.5
D_FF = ((8 * D_MODEL // 3 + 127) // 128) * 128
CONV_W = 3
EPS = 1e-6

kernel_name = "yoco_s5_fox_convffn_trunk"


def rmsnorm(x, g):
    xf = x.astype(jnp.float32)
    y = xf * lax.rsqrt(jnp.mean(xf * xf, axis=-1, keepdims=True) + EPS) * g.astype(jnp.float32)
    return y.astype(x.dtype)


def causal_dwconv(h, w, b):
    L = h.shape[1]
    hp = jnp.pad(h, ((0, 0), (CONV_W - 1, 0), (0, 0)))
    y = b
    for k in range(CONV_W):
        y = y + hp[:, k:k + L, :] * w[k]
    return y


def conv_ffn(h, w_in, conv_w, conv_b, w_out):
    u = causal_dwconv(h @ w_in, conv_w, conv_b)
    gate, up = jnp.split(u, 2, axis=-1)
    return (jax.nn.silu(gate) * up) @ w_out


def _ssm_combine(e_i, e_j):
    ai_re, ai_im, bi_re, bi_im = e_i
    aj_re, aj_im, bj_re, bj_im = e_j
    a_re = aj_re * ai_re - aj_im * ai_im
    a_im = aj_re * ai_im + aj_im * ai_re
    b_re = aj_re * bi_re - aj_im * bi_im + bj_re
    b_im = aj_re * bi_im + aj_im * bi_re + bj_im
    return (a_re, a_im, b_re, b_im)


def s5_mixer(h, lam_re, lam_im, log_dt, b_re, b_im, c_re, c_im, d_skip, w_glu):
    dtype = h.dtype
    bsz, L, _ = h.shape
    f32 = jnp.float32
    u = h.astype(f32).reshape(bsz, L, N_GROUPS, SSM_GROUP)
    lr = lam_re.astype(f32)
    li = lam_im.astype(f32)
    dt = jnp.exp(log_dt.astype(f32))[:, None]
    mag = jnp.exp(lr * dt)
    lb_re = mag * jnp.cos(li * dt)
    lb_im = mag * jnp.sin(li * dt)
    den = lr * lr + li * li
    nr = lb_re - 1.0
    fr = ((nr * lr + lb_im * li) / den)[..., None]
    fi = ((lb_im * lr - nr * li) / den)[..., None]
    br = b_re.astype(f32)
    bi = b_im.astype(f32)
    bb_re = fr * br - fi * bi
    bb_im = fr * bi + fi * br
    bu_re = jnp.einsum('blgh,gph->blgp', u, bb_re)
    bu_im = jnp.einsum('blgh,gph->blgp', u, bb_im)
    a_re = jnp.broadcast_to(lb_re[None, None], (1, L, N_GROUPS, SSM_STATE))
    a_im = jnp.broadcast_to(lb_im[None, None], (1, L, N_GROUPS, SSM_STATE))
    _, _, s_re, s_im = lax.associative_scan(_ssm_combine, (a_re, a_im, bu_re, bu_im), axis=1)
    y = (jnp.einsum('blgp,ghp->blgh', s_re, c_re.astype(f32))
         - jnp.einsum('blgp,ghp->blgh', s_im, c_im.astype(f32)))
    y = y.reshape(bsz, L, D_MODEL) + d_skip.astype(f32) * u.reshape(bsz, L, D_MODEL)
    y = jax.nn.gelu(y)
    z_a, z_g = jnp.split(y @ w_glu.astype(f32), 2, axis=-1)
    return (z_a * jax.nn.sigmoid(z_g)).astype(dtype)


def fox_shared_kv(h_kv, w_kvf, b_f):
    bsz, L, _ = h_kv.shape
    z = h_kv @ w_kvf
    k = z[..., :D_MODEL].reshape(bsz, L, N_HEADS, HEAD_DIM)
    v = z[..., D_MODEL:2 * D_MODEL].reshape(bsz, L, N_HEADS, HEAD_DIM)
    f_logit = z[..., 2 * D_MODEL:].astype(jnp.float32) + b_f.astype(jnp.float32)
    cum = jnp.cumsum(jax.nn.log_sigmoid(f_logit), axis=1)
    return k, v, cum


def fox_attention(h, w_q, w_o, k, v, cum):
    dtype = h.dtype
    bsz, L, _ = h.shape
    nb = L // Q_BLOCK
    f32 = jnp.float32
    q = (h @ w_q).reshape(bsz, nb, Q_BLOCK, N_HEADS, HEAD_DIM).transpose(1, 0, 2, 3, 4)
    cq = cum.reshape(bsz, nb, Q_BLOCK, N_HEADS).transpose(1, 0, 2, 3)
    kf = k.astype(f32)
    vf = v.astype(f32)
    ck = cum.transpose(0, 2, 1)[:, :, None, :]
    kpos = jnp.arange(L, dtype=jnp.int32)
    starts = jnp.arange(nb, dtype=jnp.int32) * Q_BLOCK

    def one_block(args):
        qb, cqb, start = args
        s = jnp.einsum('bqhd,bkhd->bhqk', qb.astype(f32), kf) * ATTN_SCALE
        s = s + cqb.transpose(0, 2, 1)[..., None] - ck
        qpos = start + jnp.arange(Q_BLOCK, dtype=jnp.int32)
        mask = kpos[None, :] <= qpos[:, None]
        s = jnp.where(mask, s, -jnp.inf)
        p = jax.nn.softmax(s, axis=-1)
        return jnp.einsum('bhqk,bkhd->bqhd', p, vf)

    o = lax.map(one_block, (q, cq, starts))
    o = o.transpose(1, 0, 2, 3, 4).reshape(bsz, L, D_MODEL)
    return o.astype(dtype) @ w_o


def _fwd_setup_inputs(seed: int = 0) -> dict:
    key = jax.random.key(seed)
    ks = jax.random.split(key, 24)
    nrm = jax.random.normal
    D, G, P, H, F = D_MODEL, N_GROUPS, SSM_STATE, SSM_GROUP, D_FF
    x = nrm(ks[0], (BATCH, SEQ, D), jnp.float32)
    g_mix = 1.0 + 0.02 * nrm(ks[1], (DEPTH, D), jnp.float32)
    g_ffn = 1.0 + 0.02 * nrm(ks[2], (DEPTH, D), jnp.float32)
    lam_re = -0.5 + 0.01 * nrm(ks[3], (N_A_LAYERS, G, P), jnp.float32)
    lam_im = (math.pi * jnp.arange(P, dtype=jnp.float32))[None, None, :] + 0.01 * nrm(ks[4], (N_A_LAYERS, G, P), jnp.float32)
    log_dt = jax.random.uniform(ks[5], (N_A_LAYERS, G), jnp.float32, math.log(DT_MIN), math.log(DT_MAX))
    ssm_b_re = nrm(ks[6], (N_A_LAYERS, G, P, H), jnp.float32) * (2 * H) ** -0.5
    ssm_b_im = nrm(ks[7], (N_A_LAYERS, G, P, H), jnp.float32) * (2 * H) ** -0.5
    ssm_c_re = nrm(ks[8], (N_A_LAYERS, G, H, P), jnp.float32) * P ** -0.5
    ssm_c_im = nrm(ks[9], (N_A_LAYERS, G, H, P), jnp.float32) * P ** -0.5
    ssm_d = nrm(ks[10], (N_A_LAYERS, D), jnp.float32)
    w_glu = nrm(ks[11], (N_A_LAYERS, D, 2 * D), jnp.float32) * D ** -0.5
    g_kv = 1.0 + 0.02 * nrm(ks[12], (D,), jnp.float32)
    w_kvf = nrm(ks[13], (D, 2 * D + N_HEADS), jnp.float32) * D ** -0.5
    b_f = 2.0 + 0.5 * nrm(ks[14], (N_HEADS,), jnp.float32)
    w_q = nrm(ks[15], (N_B_LAYERS, D, D), jnp.float32) * D ** -0.5
    w_o = nrm(ks[16], (N_B_LAYERS, D, D), jnp.float32) * D ** -0.5
    w_ffn_in = nrm(ks[17], (DEPTH, D, 2 * F), jnp.float32) * D ** -0.5
    ffn_conv_w = nrm(ks[18], (DEPTH, CONV_W, 2 * F), jnp.float32) * CONV_W ** -0.5
    ffn_conv_b = 0.01 * nrm(ks[19], (DEPTH, 2 * F), jnp.float32)
    w_ffn_out = nrm(ks[20], (DEPTH, F, D), jnp.float32) * F ** -0.5
    g_final = 1.0 + 0.02 * nrm(ks[21], (D,), jnp.float32)
    return {"x": x, "g_mix": g_mix, "g_ffn": g_ffn, "lam_re": lam_re, "lam_im": lam_im,
            "log_dt": log_dt, "ssm_b_re": ssm_b_re, "ssm_b_im": ssm_b_im, "ssm_c_re": ssm_c_re,
            "ssm_c_im": ssm_c_im, "ssm_d": ssm_d, "w_glu": w_glu, "g_kv": g_kv, "w_kvf": w_kvf,
            "b_f": b_f, "w_q": w_q, "w_o": w_o, "w_ffn_in": w_ffn_in, "ffn_conv_w": ffn_conv_w,
            "ffn_conv_b": ffn_conv_b, "w_ffn_out": w_ffn_out, "g_final": g_final}


def _fwd_reference(x, g_mix, g_ffn, lam_re, lam_im, log_dt, ssm_b_re, ssm_b_im, ssm_c_re, ssm_c_im,
              ssm_d, w_glu, g_kv, w_kvf, b_f, w_q, w_o, w_ffn_in, ffn_conv_w, ffn_conv_b,
              w_ffn_out, g_final):
    h = x
    k = v = cum = None
    for layer in range(DEPTH):
        if layer < N_A_LAYERS:
            h = h + s5_mixer(rmsnorm(h, g_mix[layer]), lam_re[layer], lam_im[layer], log_dt[layer],
                             ssm_b_re[layer], ssm_b_im[layer], ssm_c_re[layer], ssm_c_im[layer],
                             ssm_d[layer], w_glu[layer])
        else:
            if layer == N_A_LAYERS:
                k, v, cum = fox_shared_kv(rmsnorm(h, g_kv), w_kvf, b_f)
            j = layer - N_A_LAYERS
            h = h + fox_attention(rmsnorm(h, g_mix[layer]), w_q[j], w_o[j], k, v, cum)
        h = h + conv_ffn(rmsnorm(h, g_ffn[layer]), w_ffn_in[layer], ffn_conv_w[layer],
                         ffn_conv_b[layer], w_ffn_out[layer])
    return rmsnorm(h, g_final)


import jax as _jax
import jax.numpy as _jnp

TWIN_FORMAT = 'train_step'
FWD_PARAMS = ['x', 'g_mix', 'g_ffn', 'lam_re', 'lam_im', 'log_dt', 'ssm_b_re', 'ssm_b_im', 'ssm_c_re', 'ssm_c_im', 'ssm_d', 'w_glu', 'g_kv', 'w_kvf', 'b_f', 'w_q', 'w_o', 'w_ffn_in', 'ffn_conv_w', 'ffn_conv_b', 'w_ffn_out', 'g_final']
TWIN_WEIGHTS = ['g_mix', 'g_ffn', 'lam_re', 'lam_im', 'log_dt', 'ssm_b_re', 'ssm_b_im', 'ssm_c_re', 'ssm_c_im', 'ssm_d', 'w_glu', 'g_kv', 'w_kvf', 'b_f', 'w_q', 'w_o', 'w_ffn_in', 'ffn_conv_w', 'ffn_conv_b', 'w_ffn_out', 'g_final']
TWIN_DIFF_INPUT = 'x'
TWIN_INPUTS = ['x', 'g_mix', 'g_ffn', 'lam_re', 'lam_im', 'log_dt', 'ssm_b_re', 'ssm_b_im', 'ssm_c_re', 'ssm_c_im', 'ssm_d', 'w_glu', 'g_kv', 'w_kvf', 'b_f', 'w_q', 'w_o', 'w_ffn_in', 'ffn_conv_w', 'ffn_conv_b', 'w_ffn_out', 'g_final', 'loss_target', 'm_g_mix', 'm_g_ffn', 'm_lam_re', 'm_lam_im', 'm_log_dt', 'm_ssm_b_re', 'm_ssm_b_im', 'm_ssm_c_re', 'm_ssm_c_im', 'm_ssm_d', 'm_w_glu', 'm_g_kv', 'm_w_kvf', 'm_b_f', 'm_w_q', 'm_w_o', 'm_w_ffn_in', 'm_ffn_conv_w', 'm_ffn_conv_b', 'm_w_ffn_out', 'm_g_final', 'v_g_mix', 'v_g_ffn', 'v_lam_re', 'v_lam_im', 'v_log_dt', 'v_ssm_b_re', 'v_ssm_b_im', 'v_ssm_c_re', 'v_ssm_c_im', 'v_ssm_d', 'v_w_glu', 'v_g_kv', 'v_w_kvf', 'v_b_f', 'v_w_q', 'v_w_o', 'v_w_ffn_in', 'v_ffn_conv_w', 'v_ffn_conv_b', 'v_w_ffn_out', 'v_g_final']
TWIN_OUTPUTS = ['loss', 'grad_x', 'grad_g_mix', 'grad_g_ffn', 'grad_lam_re', 'grad_lam_im', 'grad_log_dt', 'grad_ssm_b_re', 'grad_ssm_b_im', 'grad_ssm_c_re', 'grad_ssm_c_im', 'grad_ssm_d', 'grad_w_glu', 'grad_g_kv', 'grad_w_kvf', 'grad_b_f', 'grad_w_q', 'grad_w_o', 'grad_w_ffn_in', 'grad_ffn_conv_w', 'grad_ffn_conv_b', 'grad_w_ffn_out', 'grad_g_final', 'delta_g_mix', 'delta_g_ffn', 'delta_lam_re', 'delta_lam_im', 'delta_log_dt', 'delta_ssm_b_re', 'delta_ssm_b_im', 'delta_ssm_c_re', 'delta_ssm_c_im', 'delta_ssm_d', 'delta_w_glu', 'delta_g_kv', 'delta_w_kvf', 'delta_b_f', 'delta_w_q', 'delta_w_o', 'delta_w_ffn_in', 'delta_ffn_conv_w', 'delta_ffn_conv_b', 'delta_w_ffn_out', 'delta_g_final', 'new_m_g_mix', 'new_m_g_ffn', 'new_m_lam_re', 'new_m_lam_im', 'new_m_log_dt', 'new_m_ssm_b_re', 'new_m_ssm_b_im', 'new_m_ssm_c_re', 'new_m_ssm_c_im', 'new_m_ssm_d', 'new_m_w_glu', 'new_m_g_kv', 'new_m_w_kvf', 'new_m_b_f', 'new_m_w_q', 'new_m_w_o', 'new_m_w_ffn_in', 'new_m_ffn_conv_w', 'new_m_ffn_conv_b', 'new_m_w_ffn_out', 'new_m_g_final', 'new_v_g_mix', 'new_v_g_ffn', 'new_v_lam_re', 'new_v_lam_im', 'new_v_log_dt', 'new_v_ssm_b_re', 'new_v_ssm_b_im', 'new_v_ssm_c_re', 'new_v_ssm_c_im', 'new_v_ssm_d', 'new_v_w_glu', 'new_v_g_kv', 'new_v_w_kvf', 'new_v_b_f', 'new_v_w_q', 'new_v_w_o', 'new_v_w_ffn_in', 'new_v_ffn_conv_w', 'new_v_ffn_conv_b', 'new_v_w_ffn_out', 'new_v_g_final']
TWIN_LEAF_KINDS = {'loss': 'loss', 'grad_x': 'grad_x', 'grad_g_mix': 'grad_w', 'grad_g_ffn': 'grad_w', 'grad_lam_re': 'grad_w', 'grad_lam_im': 'grad_w', 'grad_log_dt': 'grad_w', 'grad_ssm_b_re': 'grad_w', 'grad_ssm_b_im': 'grad_w', 'grad_ssm_c_re': 'grad_w', 'grad_ssm_c_im': 'grad_w', 'grad_ssm_d': 'grad_w', 'grad_w_glu': 'grad_w', 'grad_g_kv': 'grad_w', 'grad_w_kvf': 'grad_w', 'grad_b_f': 'grad_w', 'grad_w_q': 'grad_w', 'grad_w_o': 'grad_w', 'grad_w_ffn_in': 'grad_w', 'grad_ffn_conv_w': 'grad_w', 'grad_ffn_conv_b': 'grad_w', 'grad_w_ffn_out': 'grad_w', 'grad_g_final': 'grad_w', 'delta_g_mix': 'delta_w', 'delta_g_ffn': 'delta_w', 'delta_lam_re': 'delta_w', 'delta_lam_im': 'delta_w', 'delta_log_dt': 'delta_w', 'delta_ssm_b_re': 'delta_w', 'delta_ssm_b_im': 'delta_w', 'delta_ssm_c_re': 'delta_w', 'delta_ssm_c_im': 'delta_w', 'delta_ssm_d': 'delta_w', 'delta_w_glu': 'delta_w', 'delta_g_kv': 'delta_w', 'delta_w_kvf': 'delta_w', 'delta_b_f': 'delta_w', 'delta_w_q': 'delta_w', 'delta_w_o': 'delta_w', 'delta_w_ffn_in': 'delta_w', 'delta_ffn_conv_w': 'delta_w', 'delta_ffn_conv_b': 'delta_w', 'delta_w_ffn_out': 'delta_w', 'delta_g_final': 'delta_w', 'new_m_g_mix': 'new_m', 'new_m_g_ffn': 'new_m', 'new_m_lam_re': 'new_m', 'new_m_lam_im': 'new_m', 'new_m_log_dt': 'new_m', 'new_m_ssm_b_re': 'new_m', 'new_m_ssm_b_im': 'new_m', 'new_m_ssm_c_re': 'new_m', 'new_m_ssm_c_im': 'new_m', 'new_m_ssm_d': 'new_m', 'new_m_w_glu': 'new_m', 'new_m_g_kv': 'new_m', 'new_m_w_kvf': 'new_m', 'new_m_b_f': 'new_m', 'new_m_w_q': 'new_m', 'new_m_w_o': 'new_m', 'new_m_w_ffn_in': 'new_m', 'new_m_ffn_conv_w': 'new_m', 'new_m_ffn_conv_b': 'new_m', 'new_m_w_ffn_out': 'new_m', 'new_m_g_final': 'new_m', 'new_v_g_mix': 'new_v', 'new_v_g_ffn': 'new_v', 'new_v_lam_re': 'new_v', 'new_v_lam_im': 'new_v', 'new_v_log_dt': 'new_v', 'new_v_ssm_b_re': 'new_v', 'new_v_ssm_b_im': 'new_v', 'new_v_ssm_c_re': 'new_v', 'new_v_ssm_c_im': 'new_v', 'new_v_ssm_d': 'new_v', 'new_v_w_glu': 'new_v', 'new_v_g_kv': 'new_v', 'new_v_w_kvf': 'new_v', 'new_v_b_f': 'new_v', 'new_v_w_q': 'new_v', 'new_v_w_o': 'new_v', 'new_v_w_ffn_in': 'new_v', 'new_v_ffn_conv_w': 'new_v', 'new_v_ffn_conv_b': 'new_v', 'new_v_w_ffn_out': 'new_v', 'new_v_g_final': 'new_v'}


def _forward(args):
    return _fwd_reference(*[args[k] for k in FWD_PARAMS])


def _output_shape():
    out = _jax.eval_shape(lambda: _forward(_fwd_setup_inputs(0)))
    return out.shape, out.dtype

N_MICROBATCH = 1
ADAM_LR = 0.001
ADAM_B1 = 0.9
ADAM_B2 = 0.999
ADAM_EPS = 1e-08
ADAM_WD = 0.01
ADAM_STEP = 10
PER_EXAMPLE_BATCH_AXIS = {'x': 0, 'loss_target': 0}
SHARED_INPUTS = []
_WEIGHT_DTYPES = {'g_mix': _jnp.float32, 'g_ffn': _jnp.float32, 'lam_re': _jnp.float32, 'lam_im': _jnp.float32, 'log_dt': _jnp.float32, 'ssm_b_re': _jnp.float32, 'ssm_b_im': _jnp.float32, 'ssm_c_re': _jnp.float32, 'ssm_c_im': _jnp.float32, 'ssm_d': _jnp.float32, 'w_glu': _jnp.float32, 'g_kv': _jnp.float32, 'w_kvf': _jnp.float32, 'b_f': _jnp.float32, 'w_q': _jnp.float32, 'w_o': _jnp.float32, 'w_ffn_in': _jnp.float32, 'ffn_conv_w': _jnp.float32, 'ffn_conv_b': _jnp.float32, 'w_ffn_out': _jnp.float32, 'g_final': _jnp.float32}
MOMENT_SCALE = {'g_mix': 7.180730e-02, 'g_ffn': 1.272525e-01, 'lam_re': 6.536631e-03, 'lam_im': 7.373571e-03, 'log_dt': 3.935036e+00, 'ssm_b_re': 4.171640e-03, 'ssm_b_im': 4.155116e-03, 'ssm_c_re': 5.715213e-03, 'ssm_c_im': 5.841018e-03, 'ssm_d': 8.716103e-02, 'w_glu': 6.212592e-02, 'g_kv': 1.004073e-01, 'w_kvf': 7.048385e-02, 'b_f': 3.704243e-01, 'w_q': 3.874650e-02, 'w_o': 5.734433e-02, 'w_ffn_in': 5.441515e-02, 'ffn_conv_w': 5.426270e-02, 'ffn_conv_b': 5.649718e-02, 'w_ffn_out': 8.885701e-02, 'g_final': 3.202894e+01}


def _to_microbatches(a, axis):
    t = _jnp.moveaxis(a, axis, 0)
    t = t.reshape((N_MICROBATCH, t.shape[0] // N_MICROBATCH) + t.shape[1:])
    return _jnp.moveaxis(t, 1, axis + 1)


def setup_inputs(seed: int = 0) -> dict:
    inp = _fwd_setup_inputs(seed)
    key = _jax.random.fold_in(_jax.random.key(seed), 7919)
    shape, _ = _output_shape()
    out = dict(inp)
    out["loss_target"] = _jax.random.normal(_jax.random.fold_in(key, 0), shape, _jnp.float32)
    for i, name in enumerate(TWIN_WEIGHTS):
        w = inp[name].astype(_jnp.float32)
        if MOMENT_SCALE is None:
            s = _jnp.sqrt(_jnp.mean(_jnp.square(w)) + 1e-30)
        else:
            s = MOMENT_SCALE[name]
        km, kv = _jax.random.split(_jax.random.fold_in(key, i + 1))
        out[name] = w
        out["m_" + name] = s * _jax.random.normal(km, w.shape, _jnp.float32)
        out["v_" + name] = (s * s) * _jax.random.uniform(kv, w.shape, _jnp.float32, 0.5, 1.5)
    if N_MICROBATCH > 1:
        for name, axis in PER_EXAMPLE_BATCH_AXIS.items():
            out[name] = _to_microbatches(out[name], axis)
    return {'x': out['x'], 'g_mix': out['g_mix'], 'g_ffn': out['g_ffn'], 'lam_re': out['lam_re'], 'lam_im': out['lam_im'], 'log_dt': out['log_dt'], 'ssm_b_re': out['ssm_b_re'], 'ssm_b_im': out['ssm_b_im'], 'ssm_c_re': out['ssm_c_re'], 'ssm_c_im': out['ssm_c_im'], 'ssm_d': out['ssm_d'], 'w_glu': out['w_glu'], 'g_kv': out['g_kv'], 'w_kvf': out['w_kvf'], 'b_f': out['b_f'], 'w_q': out['w_q'], 'w_o': out['w_o'], 'w_ffn_in': out['w_ffn_in'], 'ffn_conv_w': out['ffn_conv_w'], 'ffn_conv_b': out['ffn_conv_b'], 'w_ffn_out': out['w_ffn_out'], 'g_final': out['g_final'], 'loss_target': out['loss_target'], 'm_g_mix': out['m_g_mix'], 'm_g_ffn': out['m_g_ffn'], 'm_lam_re': out['m_lam_re'], 'm_lam_im': out['m_lam_im'], 'm_log_dt': out['m_log_dt'], 'm_ssm_b_re': out['m_ssm_b_re'], 'm_ssm_b_im': out['m_ssm_b_im'], 'm_ssm_c_re': out['m_ssm_c_re'], 'm_ssm_c_im': out['m_ssm_c_im'], 'm_ssm_d': out['m_ssm_d'], 'm_w_glu': out['m_w_glu'], 'm_g_kv': out['m_g_kv'], 'm_w_kvf': out['m_w_kvf'], 'm_b_f': out['m_b_f'], 'm_w_q': out['m_w_q'], 'm_w_o': out['m_w_o'], 'm_w_ffn_in': out['m_w_ffn_in'], 'm_ffn_conv_w': out['m_ffn_conv_w'], 'm_ffn_conv_b': out['m_ffn_conv_b'], 'm_w_ffn_out': out['m_w_ffn_out'], 'm_g_final': out['m_g_final'], 'v_g_mix': out['v_g_mix'], 'v_g_ffn': out['v_g_ffn'], 'v_lam_re': out['v_lam_re'], 'v_lam_im': out['v_lam_im'], 'v_log_dt': out['v_log_dt'], 'v_ssm_b_re': out['v_ssm_b_re'], 'v_ssm_b_im': out['v_ssm_b_im'], 'v_ssm_c_re': out['v_ssm_c_re'], 'v_ssm_c_im': out['v_ssm_c_im'], 'v_ssm_d': out['v_ssm_d'], 'v_w_glu': out['v_w_glu'], 'v_g_kv': out['v_g_kv'], 'v_w_kvf': out['v_w_kvf'], 'v_b_f': out['v_b_f'], 'v_w_q': out['v_w_q'], 'v_w_o': out['v_w_o'], 'v_w_ffn_in': out['v_w_ffn_in'], 'v_ffn_conv_w': out['v_ffn_conv_w'], 'v_ffn_conv_b': out['v_ffn_conv_b'], 'v_w_ffn_out': out['v_w_ffn_out'], 'v_g_final': out['v_g_final']}


def _loss(weights, diff, rest, loss_target):
    with _jax.named_scope("forward"):
        args = {**rest, TWIN_DIFF_INPUT: diff, **{k: w.astype(_WEIGHT_DTYPES[k]) for k, w in weights.items()}}
        y = _forward(args)
    with _jax.named_scope("loss_head"):
        err = _jnp.square(y.astype(_jnp.float32) - loss_target)
        return 0.5 * _jnp.sum(_jnp.mean(err, axis=-1)) if err.ndim else 0.5 * err


def _adamw(w, g, m, v):
    m = ADAM_B1 * m + (1.0 - ADAM_B1) * g
    v = ADAM_B2 * v + (1.0 - ADAM_B2) * _jnp.square(g)
    m_hat = m / (1.0 - ADAM_B1 ** ADAM_STEP)
    v_hat = v / (1.0 - ADAM_B2 ** ADAM_STEP)
    delta = -ADAM_LR * (m_hat / (_jnp.sqrt(v_hat) + ADAM_EPS) + ADAM_WD * w)
    return delta, m, v


def reference(x, g_mix, g_ffn, lam_re, lam_im, log_dt, ssm_b_re, ssm_b_im, ssm_c_re, ssm_c_im, ssm_d, w_glu, g_kv, w_kvf, b_f, w_q, w_o, w_ffn_in, ffn_conv_w, ffn_conv_b, w_ffn_out, g_final, loss_target, m_g_mix, m_g_ffn, m_lam_re, m_lam_im, m_log_dt, m_ssm_b_re, m_ssm_b_im, m_ssm_c_re, m_ssm_c_im, m_ssm_d, m_w_glu, m_g_kv, m_w_kvf, m_b_f, m_w_q, m_w_o, m_w_ffn_in, m_ffn_conv_w, m_ffn_conv_b, m_w_ffn_out, m_g_final, v_g_mix, v_g_ffn, v_lam_re, v_lam_im, v_log_dt, v_ssm_b_re, v_ssm_b_im, v_ssm_c_re, v_ssm_c_im, v_ssm_d, v_w_glu, v_g_kv, v_w_kvf, v_b_f, v_w_q, v_w_o, v_w_ffn_in, v_ffn_conv_w, v_ffn_conv_b, v_w_ffn_out, v_g_final):
    given = dict(x=x, g_mix=g_mix, g_ffn=g_ffn, lam_re=lam_re, lam_im=lam_im, log_dt=log_dt, ssm_b_re=ssm_b_re, ssm_b_im=ssm_b_im, ssm_c_re=ssm_c_re, ssm_c_im=ssm_c_im, ssm_d=ssm_d, w_glu=w_glu, g_kv=g_kv, w_kvf=w_kvf, b_f=b_f, w_q=w_q, w_o=w_o, w_ffn_in=w_ffn_in, ffn_conv_w=ffn_conv_w, ffn_conv_b=ffn_conv_b, w_ffn_out=w_ffn_out, g_final=g_final, loss_target=loss_target, m_g_mix=m_g_mix, m_g_ffn=m_g_ffn, m_lam_re=m_lam_re, m_lam_im=m_lam_im, m_log_dt=m_log_dt, m_ssm_b_re=m_ssm_b_re, m_ssm_b_im=m_ssm_b_im, m_ssm_c_re=m_ssm_c_re, m_ssm_c_im=m_ssm_c_im, m_ssm_d=m_ssm_d, m_w_glu=m_w_glu, m_g_kv=m_g_kv, m_w_kvf=m_w_kvf, m_b_f=m_b_f, m_w_q=m_w_q, m_w_o=m_w_o, m_w_ffn_in=m_w_ffn_in, m_ffn_conv_w=m_ffn_conv_w, m_ffn_conv_b=m_ffn_conv_b, m_w_ffn_out=m_w_ffn_out, m_g_final=m_g_final, v_g_mix=v_g_mix, v_g_ffn=v_g_ffn, v_lam_re=v_lam_re, v_lam_im=v_lam_im, v_log_dt=v_log_dt, v_ssm_b_re=v_ssm_b_re, v_ssm_b_im=v_ssm_b_im, v_ssm_c_re=v_ssm_c_re, v_ssm_c_im=v_ssm_c_im, v_ssm_d=v_ssm_d, v_w_glu=v_w_glu, v_g_kv=v_g_kv, v_w_kvf=v_w_kvf, v_b_f=v_b_f, v_w_q=v_w_q, v_w_o=v_w_o, v_w_ffn_in=v_w_ffn_in, v_ffn_conv_w=v_ffn_conv_w, v_ffn_conv_b=v_ffn_conv_b, v_w_ffn_out=v_w_ffn_out, v_g_final=v_g_final)
    weights = {n: given[n] for n in TWIN_WEIGHTS}
    shared = {n: given[n] for n in SHARED_INPUTS}
    per_example = {n: given[n] for n in ['x']}
    grad_fn = _jax.value_and_grad(_loss, argnums=(0, 1))

    def one_microbatch(ex, loss_target):
        ex = dict(ex)
        diff = ex.pop(TWIN_DIFF_INPUT)
        return grad_fn(weights, diff, {**shared, **ex}, loss_target)

    if N_MICROBATCH == 1:
        loss, (grad_w, grad_x) = one_microbatch(per_example, given["loss_target"])
    else:
        def body(carry, xs):
            loss_sum, grad_sum = carry
            l_k, (gw_k, gx_k) = one_microbatch(xs[0], xs[1])
            with _jax.named_scope("update"):
                return (loss_sum + l_k, _jax.tree.map(_jnp.add, grad_sum, gw_k)), gx_k

        init = (_jnp.zeros((), _jnp.float32), _jax.tree.map(_jnp.zeros_like, weights))
        (loss, grad_w), grad_x = _jax.lax.scan(body, init, (per_example, given["loss_target"]))
    with _jax.named_scope("update"):
        delta_w, new_m, new_v = {}, {}, {}
        for n in TWIN_WEIGHTS:
            delta_w[n], new_m[n], new_v[n] = _adamw(weights[n], grad_w[n], given["m_" + n], given["v_" + n])
    return (loss, grad_x, *[grad_w[n] for n in TWIN_WEIGHTS], *[delta_w[n] for n in TWIN_WEIGHTS],
            *[new_m[n] for n in TWIN_WEIGHTS], *[new_v[n] for n in TWIN_WEIGHTS])
```

```python
import functools
import math

import jax
import jax.numpy as jnp
from jax import lax
from jax.experimental import pallas as pl
from jax.experimental.pallas import tpu as pltpu

F32 = jnp.float32
BF16 = jnp.bfloat16

D_MODEL = 1024
DEPTH = 4
N_A = 2
N_GROUPS = 64
SSM_GROUP = 16
SSM_STATE = 64
N_HEADS = 16
HEAD_DIM = 64
ATTN_SCALE = HEAD_DIM ** -0.5
D_FF = 2816
EPS = 1e-6
N_DEV = 8
LANES = 128
SUBLANES = 8

ADAM_LR = 0.001
ADAM_B1 = 0.9
ADAM_B2 = 0.999
ADAM_EPS = 1e-08
ADAM_WD = 0.01
ADAM_STEP = 10

ROW_TILE = 512
S5_CHUNK = 256
ATTN_TILE = 512
CUM_TILE = 256
NEG = -1e30

MESH_AXES = ("x", "y", "c")


def _tile(n, target, align=LANES):
    t = (min(target, n) // align) * align
    while t >= align:
        if n % t == 0:
            return t
        t -= align
    return n


def _params(*sem):
    return pltpu.CompilerParams(dimension_semantics=sem, vmem_limit_bytes=56 * 1024 * 1024)


def _mm(a, b, *, ta=False, tb=False, add=None, out_dtype=F32, a_split=False, b_split=False, out_split=False,
        tm=512, tn=512, tk=2048, name):
    if a_split:
        M, K = a.shape[1], 2 * a.shape[2]
    else:
        M, K = (a.shape[1], a.shape[0]) if ta else a.shape
    if b_split:
        N = 2 * b.shape[2]
    else:
        N = b.shape[0] if tb else b.shape[1]
    tm = _tile(M, tm)
    tn = _tile(N // 2 if (b_split or out_split) else N, tn)
    tk = _tile(K // 2 if a_split else K, tk)
    nm, nn, nk = M // tm, N // tn, K // tk

    if a_split:
        hk = nk // 2
        a_spec = pl.BlockSpec((None, tm, tk), lambda i, j, k: (k // hk, i, k % hk))
    elif ta:
        a_spec = pl.BlockSpec((tk, tm), lambda i, j, k: (k, i))
    else:
        a_spec = pl.BlockSpec((tm, tk), lambda i, j, k: (i, k))
    if b_split:
        hn = nn // 2
        b_spec = pl.BlockSpec((None, tk, tn), lambda i, j, k: (j // hn, k, j % hn))
    elif tb:
        b_spec = pl.BlockSpec((tn, tk), lambda i, j, k: (j, k))
    else:
        b_spec = pl.BlockSpec((tk, tn), lambda i, j, k: (k, j))
    if out_split:
        hn = nn // 2
        o_spec = pl.BlockSpec((None, tm, tn), lambda i, j, k: (j // hn, i, j % hn))
        out_shape = jax.ShapeDtypeStruct((2, M, N // 2), out_dtype)
    else:
        o_spec = pl.BlockSpec((tm, tn), lambda i, j, k: (i, j))
        out_shape = jax.ShapeDtypeStruct((M, N), out_dtype)
    dims = (((0 if ta else 1,), (1 if tb else 0,)), ((), ()))
    has_add = add is not None

    def body(*refs):
        if has_add:
            a_ref, b_ref, add_ref, o_ref, acc = refs
        else:
            a_ref, b_ref, o_ref, acc = refs
        k = pl.program_id(2)

        @pl.when(k == 0)
        def _():
            acc[...] = jnp.zeros_like(acc)

        acc[...] += lax.dot_general(a_ref[...].astype(BF16), b_ref[...].astype(BF16), dims,
                                    preferred_element_type=F32)

        @pl.when(k == nk - 1)
        def _():
            r = acc[...]
            if has_add:
                r = r + add_ref[...]
            o_ref[...] = r.astype(out_dtype)

    in_specs = [a_spec, b_spec]
    args = [a, b]
    if has_add:
        in_specs.append(pl.BlockSpec((tm, tn), lambda i, j, k: (i, j)))
        args.append(add)
    return pl.pallas_call(
        body, grid=(nm, nn, nk), in_specs=in_specs, out_specs=o_spec, out_shape=out_shape,
        scratch_shapes=[pltpu.VMEM((tm, tn), F32)],
        compiler_params=_params("parallel", "parallel", "arbitrary"), name=name)(*args)


def _rms_fwd(h, g, *, name):
    L, D = h.shape
    tr = _tile(L, ROW_TILE, SUBLANES)

    def body(h_ref, g_ref, o_ref):
        x = h_ref[...]
        r = lax.rsqrt(jnp.mean(x * x, axis=1, keepdims=True) + EPS)
        o_ref[...] = (x * r * g_ref[...]).astype(BF16)

    return pl.pallas_call(
        body, grid=(L // tr,),
        in_specs=[pl.BlockSpec((tr, D), lambda i: (i, 0)), pl.BlockSpec((1, D), lambda i: (0, 0))],
        out_specs=pl.BlockSpec((tr, D), lambda i: (i, 0)), out_shape=jax.ShapeDtypeStruct((L, D), BF16),
        compiler_params=_params("parallel"), name=name)(h, g)


def _rms_bwd(h, g, dy, dres, *, name):
    L, D = h.shape
    tr = _tile(L, ROW_TILE, SUBLANES)

    def body(h_ref, g_ref, dy_ref, dres_ref, dh_ref, dg_ref):
        @pl.when(pl.program_id(0) == 0)
        def _():
            dg_ref[...] = jnp.zeros_like(dg_ref)

        x = h_ref[...]
        r = lax.rsqrt(jnp.mean(x * x, axis=1, keepdims=True) + EPS)
        xn = x * r
        dy = dy_ref[...].astype(F32)
        gdy = dy * g_ref[...]
        dx = r * (gdy - xn * jnp.mean(gdy * xn, axis=1, keepdims=True))
        dh_ref[...] = dres_ref[...] + dx
        dg_ref[...] += jnp.sum(dy * xn, axis=0, keepdims=True)

    row = pl.BlockSpec((tr, D), lambda i: (i, 0))
    vec = pl.BlockSpec((1, D), lambda i: (0, 0))
    return pl.pallas_call(
        body, grid=(L // tr,), in_specs=[row, vec, row, row], out_specs=[row, vec],
        out_shape=[jax.ShapeDtypeStruct((L, D), F32), jax.ShapeDtypeStruct((1, D), F32)],
        compiler_params=_params("arbitrary"), name=name)(h, g, dy, dres)


def _glu_res_rms(z, h, g, *, name):
    L, D = h.shape
    tr = _tile(L, ROW_TILE, SUBLANES)

    def body(za_ref, zg_ref, h_ref, g_ref, h1_ref, hn_ref):
        x = h_ref[...] + za_ref[...] * jax.nn.sigmoid(zg_ref[...])
        h1_ref[...] = x
        r = lax.rsqrt(jnp.mean(x * x, axis=1, keepdims=True) + EPS)
        hn_ref[...] = (x * r * g_ref[...]).astype(BF16)

    row = pl.BlockSpec((tr, D), lambda i: (i, 0))
    return pl.pallas_call(
        body, grid=(L // tr,),
        in_specs=[row, pl.BlockSpec((tr, D), lambda i: (i, 1)), row, pl.BlockSpec((1, D), lambda i: (0, 0))],
        out_specs=[row, row],
        out_shape=[jax.ShapeDtypeStruct((L, D), F32), jax.ShapeDtypeStruct((L, D), BF16)],
        compiler_params=_params("parallel"), name=name)(z, z, h, g)


def _glu_bwd(z, dout, *, name):
    L, D = dout.shape
    tr = _tile(L, ROW_TILE, SUBLANES)

    def body(za_ref, zg_ref, d_ref, o_ref):
        sg = jax.nn.sigmoid(zg_ref[...])
        d = d_ref[...]
        o_ref[:, :D] = (d * sg).astype(BF16)
        o_ref[:, D:] = (d * za_ref[...] * sg * (1.0 - sg)).astype(BF16)

    row = pl.BlockSpec((tr, D), lambda i: (i, 0))
    return pl.pallas_call(
        body, grid=(L // tr,),
        in_specs=[row, pl.BlockSpec((tr, D), lambda i: (i, 1)), row],
        out_specs=pl.BlockSpec((tr, 2 * D), lambda i: (i, 0)),
        out_shape=jax.ShapeDtypeStruct((L, 2 * D), BF16),
        compiler_params=_params("parallel"), name=name)(z, z, dout)


def _loss_head(h, g, target, *, name):
    L, D = h.shape
    tr = _tile(L, ROW_TILE, SUBLANES)

    def body(h_ref, g_ref, t_ref, loss_ref, dh_ref, dg_ref):
        @pl.when(pl.program_id(0) == 0)
        def _():
            dg_ref[...] = jnp.zeros_like(dg_ref)
            loss_ref[...] = jnp.zeros_like(loss_ref)

        x = h_ref[...]
        gg = g_ref[...]
        r = lax.rsqrt(jnp.mean(x * x, axis=1, keepdims=True) + EPS)
        xn = x * r
        err = xn * gg - t_ref[...]
        loss_ref[...] += 0.5 * jnp.sum(jnp.mean(err * err, axis=1, keepdims=True), axis=0, keepdims=True)
        dy = err * (1.0 / D)
        gdy = dy * gg
        dh_ref[...] = r * (gdy - xn * jnp.mean(gdy * xn, axis=1, keepdims=True))
        dg_ref[...] += jnp.sum(dy * xn, axis=0, keepdims=True)

    row = pl.BlockSpec((tr, D), lambda i: (i, 0))
    vec = pl.BlockSpec((1, D), lambda i: (0, 0))
    return pl.pallas_call(
        body, grid=(L // tr,), in_specs=[row, vec, row],
        out_specs=[pl.BlockSpec((1, 1), lambda i: (0, 0)), row, vec],
        out_shape=[jax.ShapeDtypeStruct((1, 1), F32), jax.ShapeDtypeStruct((L, D), F32),
                   jax.ShapeDtypeStruct((1, D), F32)],
        compiler_params=_params("arbitrary"), name=name)(h, g, target)


CONV_COL_TILE = 256


def _shift_down(x, halo, k, row):
    y = pltpu.roll(x, k, 0)
    for r in range(k):
        y = jnp.where(row == r, halo[SUBLANES - k + r:SUBLANES - k + r + 1, :], y)
    return y


def _shift_up(x, halo, k, row, n):
    y = pltpu.roll(x, n - k, 0)
    for r in range(k):
        y = jnp.where(row == n - k + r, halo[r:r + 1, :], y)
    return y


def _conv_specs(L, F, tr, tc):
    nrb = tr // SUBLANES
    main = pl.BlockSpec((2, tr, tc), lambda j, i: (0, i, j))
    prev = pl.BlockSpec((2, SUBLANES, tc), lambda j, i: (0, jnp.maximum(i * nrb - 1, 0), j))
    nxt = pl.BlockSpec((2, SUBLANES, tc), lambda j, i: (0, jnp.minimum((i + 1) * nrb, L // SUBLANES - 1), j))
    cw = pl.BlockSpec((2, 3, tc), lambda j, i: (0, 0, j))
    cb = pl.BlockSpec((2, 1, tc), lambda j, i: (0, 0, j))
    half = pl.BlockSpec((tr, tc), lambda j, i: (i, j))
    return main, prev, nxt, cw, cb, half


def _conv_tile(u_ref, p_ref, w_ref, b_ref, s, first, row):
    x = u_ref[s]
    halo = jnp.where(first, 0.0, p_ref[s])
    w = w_ref[s]
    x1 = _shift_down(x, halo, 1, row)
    x2 = _shift_down(x, halo, 2, row)
    return b_ref[s] + x2 * w[0:1] + x1 * w[1:2] + x * w[2:3], x1, x2


def _conv_act(u0, cw, cb, *, name):
    _, L, F = u0.shape
    tr, tc = _tile(L, ROW_TILE, SUBLANES), _tile(F, CONV_COL_TILE)
    main, prev, _, cws, cbs, half = _conv_specs(L, F, tr, tc)

    def body(u_ref, p_ref, w_ref, b_ref, a_ref):
        first = pl.program_id(1) == 0
        row = lax.broadcasted_iota(jnp.int32, (tr, tc), 0)
        gate, _, _ = _conv_tile(u_ref, p_ref, w_ref, b_ref, 0, first, row)
        up, _, _ = _conv_tile(u_ref, p_ref, w_ref, b_ref, 1, first, row)
        a_ref[...] = (gate * jax.nn.sigmoid(gate) * up).astype(BF16)

    return pl.pallas_call(
        body, grid=(F // tc, L // tr), in_specs=[main, prev, cws, cbs], out_specs=half,
        out_shape=jax.ShapeDtypeStruct((L, F), BF16),
        compiler_params=_params("parallel", "parallel"), name=name)(u0, u0, cw, cb)


def _conv_act_bwd(u0, cw, cb, da, *, name):
    _, L, F = u0.shape
    tr, tc = _tile(L, ROW_TILE, SUBLANES), _tile(F, CONV_COL_TILE)
    main, prev, _, cws, cbs, half = _conv_specs(L, F, tr, tc)

    def body(u_ref, p_ref, w_ref, b_ref, da_ref, a_ref, du_ref, dcb_ref):
        first = pl.program_id(1) == 0

        @pl.when(first)
        def _():
            dcb_ref[...] = jnp.zeros_like(dcb_ref)

        row = lax.broadcasted_iota(jnp.int32, (tr, tc), 0)
        gate, _, _ = _conv_tile(u_ref, p_ref, w_ref, b_ref, 0, first, row)
        up, _, _ = _conv_tile(u_ref, p_ref, w_ref, b_ref, 1, first, row)
        sg = jax.nn.sigmoid(gate)
        silu = gate * sg
        a_ref[...] = (silu * up).astype(BF16)
        da = da_ref[...]
        dgate = da * up * (sg * (1.0 + gate * (1.0 - sg)))
        dup = da * silu
        du_ref[0] = dgate
        du_ref[1] = dup
        dcb_ref[0] += jnp.sum(dgate, axis=0, keepdims=True)
        dcb_ref[1] += jnp.sum(dup, axis=0, keepdims=True)

    return pl.pallas_call(
        body, grid=(F // tc, L // tr), in_specs=[main, prev, cws, cbs, half],
        out_specs=[half, main, cbs],
        out_shape=[jax.ShapeDtypeStruct((L, F), BF16), jax.ShapeDtypeStruct((2, L, F), F32),
                   jax.ShapeDtypeStruct((2, 1, F), F32)],
        compiler_params=_params("parallel", "arbitrary"), name=name)(u0, u0, cw, cb, da)


def _conv_bwd(u0, cw, du, *, name):
    _, L, F = u0.shape
    tr, tc = _tile(L, ROW_TILE, SUBLANES), _tile(F, CONV_COL_TILE)
    main, prev, nxt, cws, _, _ = _conv_specs(L, F, tr, tc)
    nr = L // tr

    def body(u_ref, p_ref, w_ref, du_ref, n_ref, du0_ref, dcw_ref):
        i = pl.program_id(1)

        @pl.when(i == 0)
        def _():
            dcw_ref[...] = jnp.zeros_like(dcw_ref)

        row = lax.broadcasted_iota(jnp.int32, (tr, tc), 0)
        for s in range(2):
            x = u_ref[s]
            halo = jnp.where(i == 0, 0.0, p_ref[s])
            x1 = _shift_down(x, halo, 1, row)
            x2 = _shift_down(x, halo, 2, row)
            d = du_ref[s]
            nh = jnp.where(i == nr - 1, 0.0, n_ref[s])
            d1 = _shift_up(d, nh, 1, row, tr)
            d2 = _shift_up(d, nh, 2, row, tr)
            w = w_ref[s]
            du0_ref[s] = (d * w[2:3] + d1 * w[1:2] + d2 * w[0:1]).astype(BF16)
            dcw_ref[s, 0:1, :] += jnp.sum(d * x2, axis=0, keepdims=True)
            dcw_ref[s, 1:2, :] += jnp.sum(d * x1, axis=0, keepdims=True)
            dcw_ref[s, 2:3, :] += jnp.sum(d * x, axis=0, keepdims=True)

    return pl.pallas_call(
        body, grid=(F // tc, nr), in_specs=[main, prev, cws, main, nxt],
        out_specs=[main, cws],
        out_shape=[jax.ShapeDtypeStruct((2, L, F), BF16), jax.ShapeDtypeStruct((2, 3, F), F32)],
        compiler_params=_params("parallel", "arbitrary"), name=name)(u0, u0, cw, du, du)


GELU_C = math.sqrt(2.0 / math.pi)
GELU_A = 0.044715


def _gelu(x):
    return 0.5 * x * (1.0 + jnp.tanh(GELU_C * (x + GELU_A * x * x * x)))


def _gelu_grad(x):
    th = jnp.tanh(GELU_C * (x + GELU_A * x * x * x))
    return 0.5 * (1.0 + th) + 0.5 * x * (1.0 - th * th) * GELU_C * (1.0 + 3.0 * GELU_A * x * x)


def _s5_project_in(u_ref, wb_ref, s3, T, TP):
    for j in range(N_GROUPS):
        blk = (j // 8) * LANES
        s3[pl.ds(j * TP + SUBLANES, T), :] = jnp.dot(u_ref[:, blk:blk + LANES], wb_ref[j],
                                                     preferred_element_type=F32)


def _s5_scan_fwd(s3, a1, a2, s0, T, TP):
    span = (N_GROUPS - 1) * TP + 2 * SUBLANES

    def blk(i, s):
        view = s3.at[pl.ds(pl.multiple_of(i * SUBLANES, SUBLANES), span)]
        for k in range(SUBLANES):
            rows = pl.ds(SUBLANES + k, N_GROUPS, stride=TP)
            s = a1 * s + a2 * pltpu.roll(s, SSM_STATE, 1) + view[rows, :]
            view[rows, :] = s
        return s

    return lax.fori_loop(0, T // SUBLANES, blk, s0)


def _s5_fwd(hn, wb, wc, a1, a2, dvec, *, name):
    L, D = hn.shape
    T = min(S5_CHUNK, L)
    TP = T + SUBLANES
    nC = L // T

    def body(u_ref, wb_ref, wc_ref, a1_ref, a2_ref, d_ref, y_ref, yg_ref, sb_ref, s3, st):
        @pl.when(pl.program_id(0) == 0)
        def _():
            st[...] = jnp.zeros_like(st)

        sb_ref[0] = st[...]
        _s5_project_in(u_ref, wb_ref, s3, T, TP)
        st[...] = _s5_scan_fwd(s3, a1_ref[...], a2_ref[...], st[...], T, TP)
        for jb in range(D // LANES):
            acc = jnp.zeros((T, LANES), F32)
            for j in range(8 * jb, 8 * jb + 8):
                acc += jnp.dot(s3[pl.ds(j * TP + SUBLANES, T), :].astype(BF16), wc_ref[j],
                               preferred_element_type=F32)
            cols = slice(jb * LANES, (jb + 1) * LANES)
            y = acc + d_ref[:, cols] * u_ref[:, cols].astype(F32)
            y_ref[:, cols] = y
            yg_ref[:, cols] = _gelu(y).astype(BF16)

    row = pl.BlockSpec((T, D), lambda c: (c, 0))
    wspec = pl.BlockSpec((N_GROUPS, LANES, LANES), lambda c: (0, 0, 0))
    aspec = pl.BlockSpec((N_GROUPS, LANES), lambda c: (0, 0))
    return pl.pallas_call(
        body, grid=(nC,),
        in_specs=[row, wspec, wspec, aspec, aspec, pl.BlockSpec((1, D), lambda c: (0, 0))],
        out_specs=[row, row, pl.BlockSpec((1, N_GROUPS, LANES), lambda c: (c, 0, 0))],
        out_shape=[jax.ShapeDtypeStruct((L, D), F32), jax.ShapeDtypeStruct((L, D), BF16),
                   jax.ShapeDtypeStruct((nC, N_GROUPS, LANES), F32)],
        scratch_shapes=[pltpu.VMEM((N_GROUPS * TP, LANES), F32), pltpu.VMEM((N_GROUPS, LANES), F32)],
        compiler_params=_params("arbitrary"), name=name)(hn, wb, wc, a1, a2, dvec)


def _s5_bwd(hn, dyg, ypre, sbound, wb, wc, a1, a2, dvec, *, name):
    L, D = hn.shape
    T = min(S5_CHUNK, L)
    TP = T + SUBLANES
    nC = L // T
    span = (N_GROUPS - 1) * TP + 2 * SUBLANES
    NT = (((1,), (1,)), ((), ()))
    TN = (((0,), (0,)), ((), ()))

    def body(u_ref, dyg_ref, yp_ref, sb_ref, wb_ref, wc_ref, a1_ref, a2_ref, d_ref,
             du_ref, dwb_ref, dwc_ref, da1_ref, da2_ref, dd_ref, s3, g3, gst, dy_s):
        @pl.when(pl.program_id(0) == 0)
        def _():
            gst[...] = jnp.zeros_like(gst)
            dwb_ref[...] = jnp.zeros_like(dwb_ref)
            dwc_ref[...] = jnp.zeros_like(dwc_ref)
            da1_ref[...] = jnp.zeros_like(da1_ref)
            da2_ref[...] = jnp.zeros_like(da2_ref)
            dd_ref[...] = jnp.zeros_like(dd_ref)

        a1 = a1_ref[...]
        a2 = a2_ref[...]
        dy = dyg_ref[...].astype(F32) * _gelu_grad(yp_ref[...])
        dy_s[...] = dy.astype(BF16)
        dd_ref[...] += jnp.sum(dy * u_ref[...].astype(F32), axis=0, keepdims=True)
        du_ref[...] = d_ref[...] * dy

        s3[pl.ds(SUBLANES - 1, N_GROUPS, stride=TP), :] = sb_ref[0]
        _s5_project_in(u_ref, wb_ref, s3, T, TP)
        _s5_scan_fwd(s3, a1, a2, sb_ref[0], T, TP)

        for j in range(N_GROUPS):
            blk = (j // 8) * LANES
            g3[pl.ds(j * TP + SUBLANES, T), :] = lax.dot_general(dy_s[:, blk:blk + LANES], wc_ref[j], NT,
                                                                 preferred_element_type=F32)
        a2c = -a2

        def rblk(ii, carry):
            g, acc1, acc2 = carry
            t0 = pl.multiple_of((T // SUBLANES - 1 - ii) * SUBLANES, SUBLANES)
            gv = g3.at[pl.ds(t0, span)]
            sv = s3.at[pl.ds(t0, span)]
            for k in reversed(range(SUBLANES)):
                rows = pl.ds(SUBLANES + k, N_GROUPS, stride=TP)
                g = a1 * g + a2c * pltpu.roll(g, SSM_STATE, 1) + gv[rows, :]
                gv[rows, :] = g
                sp = sv[pl.ds(SUBLANES - 1 + k, N_GROUPS, stride=TP), :]
                acc1 = acc1 + g * sp
                acc2 = acc2 + g * pltpu.roll(sp, SSM_STATE, 1)
            return g, acc1, acc2

        zero = jnp.zeros((N_GROUPS, LANES), F32)
        g, acc1, acc2 = lax.fori_loop(0, T // SUBLANES, rblk, (gst[...], zero, zero))
        gst[...] = g
        da1_ref[...] += acc1
        da2_ref[...] += acc2

        for jb in range(D // LANES):
            cols = slice(jb * LANES, (jb + 1) * LANES)
            acc = jnp.zeros((T, LANES), F32)
            for j in range(8 * jb, 8 * jb + 8):
                rows = pl.ds(j * TP + SUBLANES, T)
                gj = g3[rows, :].astype(BF16)
                dwc_ref[j] += lax.dot_general(s3[rows, :].astype(BF16), dy_s[:, cols], TN,
                                              preferred_element_type=F32)
                dwb_ref[j] += lax.dot_general(u_ref[:, cols], gj, TN, preferred_element_type=F32)
                acc += lax.dot_general(gj, wb_ref[j], NT, preferred_element_type=F32)
            du_ref[:, cols] += acc

    rrow = pl.BlockSpec((T, D), lambda c: (nC - 1 - c, 0))
    wspec = pl.BlockSpec((N_GROUPS, LANES, LANES), lambda c: (0, 0, 0))
    aspec = pl.BlockSpec((N_GROUPS, LANES), lambda c: (0, 0))
    vec = pl.BlockSpec((1, D), lambda c: (0, 0))
    return pl.pallas_call(
        body, grid=(nC,),
        in_specs=[rrow, rrow, rrow, pl.BlockSpec((1, N_GROUPS, LANES), lambda c: (nC - 1 - c, 0, 0)),
                  wspec, wspec, aspec, aspec, vec],
        out_specs=[rrow, wspec, wspec, aspec, aspec, vec],
        out_shape=[jax.ShapeDtypeStruct((L, D), F32),
                   jax.ShapeDtypeStruct((N_GROUPS, LANES, LANES), F32),
                   jax.ShapeDtypeStruct((N_GROUPS, LANES, LANES), F32),
                   jax.ShapeDtypeStruct((N_GROUPS, LANES), F32), jax.ShapeDtypeStruct((N_GROUPS, LANES), F32),
                   jax.ShapeDtypeStruct((1, D), F32)],
        scratch_shapes=[pltpu.VMEM((N_GROUPS * TP, LANES), F32), pltpu.VMEM((N_GROUPS * TP, LANES), F32),
                        pltpu.VMEM((N_GROUPS, LANES), F32), pltpu.VMEM((T, D), BF16)],
        compiler_params=_params("arbitrary"), name=name)(hn, dyg, ypre, sbound, wb, wc, a1, a2, dvec)


def _s5_prep(lam_re, lam_im, log_dt, b_re, b_im, c_re, c_im):
    dt = jnp.exp(log_dt)[:, None]
    mag = jnp.exp(lam_re * dt)
    lb_re = mag * jnp.cos(lam_im * dt)
    lb_im = mag * jnp.sin(lam_im * dt)
    den = lam_re * lam_re + lam_im * lam_im
    nr = lb_re - 1.0
    fr = ((nr * lam_re + lb_im * lam_im) / den)[..., None]
    fi = ((lb_im * lam_re - nr * lam_im) / den)[..., None]
    bb_re = fr * b_re - fi * b_im
    bb_im = fr * b_im + fi * b_re
    a1 = jnp.concatenate([lb_re, lb_re], axis=1)
    a2 = jnp.concatenate([-lb_im, lb_im], axis=1)
    sel = jax.nn.one_hot(jnp.arange(N_GROUPS) % 8, 8, dtype=F32)
    blk_b = jnp.concatenate([bb_re, bb_im], axis=1).transpose(0, 2, 1)
    wb = jnp.einsum('jk,jhl->jkhl', sel, blk_b).reshape(N_GROUPS, LANES, LANES)
    blk_c = jnp.concatenate([c_re, -c_im], axis=2).transpose(0, 2, 1)
    wc = jnp.einsum('jk,jlh->jlkh', sel, blk_c).reshape(N_GROUPS, LANES, LANES)
    return a1, a2, wb, wc


def _tri(n, upper):
    r = lax.broadcasted_iota(jnp.int32, (n, n), 0)
    c = lax.broadcasted_iota(jnp.int32, (n, n), 1)
    return ((r <= c) if upper else (r >= c)).astype(F32)


def _fgate_fwd(fl, bf, *, name):
    L, W = fl.shape
    tr = _tile(L, CUM_TILE, SUBLANES)

    def body(f_ref, b_ref, o_ref, carry):
        @pl.when(pl.program_id(0) == 0)
        def _():
            carry[...] = jnp.zeros_like(carry)

        x = f_ref[...] + b_ref[...]
        ls = jnp.minimum(x, 0.0) - jnp.log(1.0 + jnp.exp(-jnp.abs(x)))
        cum = jnp.dot(_tri(tr, False), ls, preferred_element_type=F32, precision=lax.Precision.HIGHEST) + carry[...]
        o_ref[...] = cum
        carry[...] = cum[tr - 1:tr, :]

    return pl.pallas_call(
        body, grid=(L // tr,),
        in_specs=[pl.BlockSpec((tr, W), lambda i: (i, 0)), pl.BlockSpec((1, W), lambda i: (0, 0))],
        out_specs=pl.BlockSpec((tr, W), lambda i: (i, 0)), out_shape=jax.ShapeDtypeStruct((L, W), F32),
        scratch_shapes=[pltpu.VMEM((1, W), F32)], compiler_params=_params("arbitrary"), name=name)(fl, bf)


def _fgate_bwd(fl, bf, dcum, *, name):
    L, W = fl.shape
    tr = _tile(L, CUM_TILE, SUBLANES)
    n = L // tr

    def body(f_ref, b_ref, d_ref, o_ref, db_ref, carry):
        @pl.when(pl.program_id(0) == 0)
        def _():
            carry[...] = jnp.zeros_like(carry)
            db_ref[...] = jnp.zeros_like(db_ref)

        d = d_ref[...]
        rev = jnp.dot(_tri(tr, True), d, preferred_element_type=F32, precision=lax.Precision.HIGHEST) + carry[...]
        carry[...] += jnp.sum(d, axis=0, keepdims=True)
        df = rev * jax.nn.sigmoid(-(f_ref[...] + b_ref[...]))
        o_ref[...] = df
        db_ref[...] += jnp.sum(df, axis=0, keepdims=True)

    rrow = pl.BlockSpec((tr, W), lambda i: (n - 1 - i, 0))
    vec = pl.BlockSpec((1, W), lambda i: (0, 0))
    return pl.pallas_call(
        body, grid=(n,), in_specs=[rrow, vec, rrow], out_specs=[rrow, vec],
        out_shape=[jax.ShapeDtypeStruct((L, W), F32), jax.ShapeDtypeStruct((1, W), F32)],
        scratch_shapes=[pltpu.VMEM((1, W), F32)], compiler_params=_params("arbitrary"), name=name)(fl, bf, dcum)


_NT = (((1,), (1,)), ((), ()))
_TN = (((0,), (0,)), ((), ()))


def _attn_logits(q, k, ck, diag, t):
    s = lax.dot_general(q, k, _NT, preferred_element_type=F32) * ATTN_SCALE - ck
    r = lax.broadcasted_iota(jnp.int32, (t, t), 0)
    c = lax.broadcasted_iota(jnp.int32, (t, t), 1)
    return jnp.where(jnp.logical_and(diag, c > r), NEG, s)


def _attn_fwd(q, k, v, ck, *, name):
    H, L, dh = q.shape
    t = _tile(L, ATTN_TILE)
    n = L // t

    def body(q_ref, k_ref, v_ref, ck_ref, o_ref, o32_ref, lse_ref, m_s, l_s, acc):
        i, j = pl.program_id(1), pl.program_id(2)

        @pl.when(j == 0)
        def _():
            m_s[...] = jnp.full_like(m_s, NEG)
            l_s[...] = jnp.zeros_like(l_s)
            acc[...] = jnp.zeros_like(acc)

        @pl.when(j <= i)
        def _():
            s = _attn_logits(q_ref[...], k_ref[...], ck_ref[...], j == i, t)
            m_new = jnp.maximum(m_s[...], jnp.max(s, axis=1, keepdims=True))
            alpha = jnp.exp(m_s[...] - m_new)
            p = jnp.exp(s - m_new)
            l_s[...] = alpha * l_s[...] + jnp.sum(p, axis=1, keepdims=True)
            p_hi = p.astype(BF16)
            p_lo = (p - p_hi.astype(F32)).astype(BF16)
            pv = (jnp.dot(p_hi, v_ref[...], preferred_element_type=F32)
                  + jnp.dot(p_lo, v_ref[...], preferred_element_type=F32))
            acc[...] = alpha * acc[...] + pv
            m_s[...] = m_new

        @pl.when(j == n - 1)
        def _():
            o = acc[...] / l_s[...]
            o_ref[...] = o.astype(BF16)
            o32_ref[...] = o
            lse_ref[...] = m_s[...] + jnp.log(l_s[...])

    qs = pl.BlockSpec((None, t, dh), lambda h, i, j: (h, i, 0))
    ks = pl.BlockSpec((None, t, dh), lambda h, i, j: (h, jnp.minimum(i, j), 0))
    cs = pl.BlockSpec((None, 1, t), lambda h, i, j: (h, 0, jnp.minimum(i, j)))
    return pl.pallas_call(
        body, grid=(H, n, n), in_specs=[qs, ks, ks, cs],
        out_specs=[qs, qs, pl.BlockSpec((None, t, 1), lambda h, i, j: (h, i, 0))],
        out_shape=[jax.ShapeDtypeStruct((H, L, dh), BF16), jax.ShapeDtypeStruct((H, L, dh), F32),
                   jax.ShapeDtypeStruct((H, L, 1), F32)],
        scratch_shapes=[pltpu.VMEM((t, 1), F32), pltpu.VMEM((t, 1), F32), pltpu.VMEM((t, dh), F32)],
        compiler_params=_params("parallel", "parallel", "arbitrary"), name=name)(q, k, v, ck)


def _attn_ds(q_ref, k_ref, v_ref, ck_ref, o_ref, do_ref, lse_ref, diag, t):
    s = _attn_logits(q_ref[...], k_ref[...], ck_ref[...], diag, t)
    p = jnp.exp(s - lse_ref[...])
    do = do_ref[...]
    dp = lax.dot_general(do, v_ref[...], _NT, preferred_element_type=F32)
    delta = jnp.sum(do.astype(F32) * o_ref[...], axis=1, keepdims=True)
    return p, p * (dp - delta)


def _attn_bwd_kv(q, k, v, ck, o, do, lse, *, name):
    H, L, dh = q.shape
    t = _tile(L, ATTN_TILE)
    n = L // t

    def body(q_ref, k_ref, v_ref, ck_ref, o_ref, do_ref, lse_ref, dk_ref, dv_ref, dck_ref):
        j, i = pl.program_id(1), pl.program_id(2)

        @pl.when(i == 0)
        def _():
            dk_ref[...] = jnp.zeros_like(dk_ref)
            dv_ref[...] = jnp.zeros_like(dv_ref)
            dck_ref[...] = jnp.zeros_like(dck_ref)

        @pl.when(i >= j)
        def _():
            p, ds = _attn_ds(q_ref, k_ref, v_ref, ck_ref, o_ref, do_ref, lse_ref, j == i, t)
            dv_ref[...] += lax.dot_general(p.astype(BF16), do_ref[...], _TN, preferred_element_type=F32)
            dk_ref[...] += lax.dot_general(ds.astype(BF16), q_ref[...], _TN,
                                           preferred_element_type=F32) * ATTN_SCALE
            dck_ref[...] -= jnp.sum(ds, axis=0, keepdims=True)

    qs = pl.BlockSpec((None, t, dh), lambda h, j, i: (h, jnp.maximum(i, j), 0))
    ks = pl.BlockSpec((None, t, dh), lambda h, j, i: (h, j, 0))
    cs = pl.BlockSpec((None, 1, t), lambda h, j, i: (h, 0, j))
    ls = pl.BlockSpec((None, t, 1), lambda h, j, i: (h, jnp.maximum(i, j), 0))
    return pl.pallas_call(
        body, grid=(H, n, n), in_specs=[qs, ks, ks, cs, qs, qs, ls], out_specs=[ks, ks, cs],
        out_shape=[jax.ShapeDtypeStruct((H, L, dh), F32), jax.ShapeDtypeStruct((H, L, dh), F32),
                   jax.ShapeDtypeStruct((H, 1, L), F32)],
        compiler_params=_params("parallel", "parallel", "arbitrary"), name=name)(q, k, v, ck, o, do, lse)


def _attn_bwd_q(q, k, v, ck, o, do, lse, *, name):
    H, L, dh = q.shape
    t = _tile(L, ATTN_TILE)
    n = L // t

    def body(q_ref, k_ref, v_ref, ck_ref, o_ref, do_ref, lse_ref, dq_ref, acc):
        i, j = pl.program_id(1), pl.program_id(2)

        @pl.when(j == 0)
        def _():
            acc[...] = jnp.zeros_like(acc)

        @pl.when(j <= i)
        def _():
            _, ds = _attn_ds(q_ref, k_ref, v_ref, ck_ref, o_ref, do_ref, lse_ref, j == i, t)
            acc[...] += jnp.dot(ds.astype(BF16), k_ref[...], preferred_element_type=F32)

        @pl.when(j == n - 1)
        def _():
            dq_ref[...] = (acc[...] * ATTN_SCALE).astype(BF16)

    qs = pl.BlockSpec((None, t, dh), lambda h, i, j: (h, i, 0))
    ks = pl.BlockSpec((None, t, dh), lambda h, i, j: (h, jnp.minimum(i, j), 0))
    cs = pl.BlockSpec((None, 1, t), lambda h, i, j: (h, 0, jnp.minimum(i, j)))
    ls = pl.BlockSpec((None, t, 1), lambda h, i, j: (h, i, 0))
    return pl.pallas_call(
        body, grid=(H, n, n), in_specs=[qs, ks, ks, cs, qs, qs, ls], out_specs=qs,
        out_shape=jax.ShapeDtypeStruct((H, L, dh), BF16),
        scratch_shapes=[pltpu.VMEM((t, dh), F32)],
        compiler_params=_params("parallel", "parallel", "arbitrary"), name=name)(q, k, v, ck, o, do, lse)


def _heads(a):
    L = a.shape[0]
    return a.reshape(L, N_HEADS, HEAD_DIM).transpose(1, 0, 2)


def _unheads(a):
    H, L, dh = a.shape
    return a.transpose(1, 0, 2).reshape(L, H * dh)


def local_step(x, target, w):
    L = x.shape[0]
    grads = {}
    vec = lambda a: a.reshape(1, -1)

    h = x
    saved = []
    kvs = None
    for layer in range(DEPTH):
        if layer < N_A:
            prep_args = (w["lam_re"][layer], w["lam_im"][layer], w["log_dt"][layer], w["ssm_b_re"][layer],
                         w["ssm_b_im"][layer], w["ssm_c_re"][layer], w["ssm_c_im"][layer])
            (a1, a2, wb, wc), prep_vjp = jax.vjp(_s5_prep, *prep_args)
            wb16, wc16 = wb.astype(BF16), wc.astype(BF16)
            dvec = vec(w["ssm_d"][layer])
            hn = _rms_fwd(h, vec(w["g_mix"][layer]), name=f"mix_norm{layer}")
            ypre, yg, sb = _s5_fwd(hn, wb16, wc16, a1, a2, dvec, name=f"s5_fwd{layer}")
            z = _mm(yg, w["w_glu"][layer], name=f"glu_mm{layer}")
            h1, hn2 = _glu_res_rms(z, h, vec(w["g_ffn"][layer]), name=f"glu_res{layer}")
            mix_saved = (h, hn, ypre, yg, sb, z, a1, a2, wb16, wc16, dvec, prep_vjp)
        else:
            j = layer - N_A
            if layer == N_A:
                hkv = _rms_fwd(h, vec(w["g_kv"]), name="kv_norm")
                kvm = _mm(hkv, w["w_kv"], out_dtype=BF16, name="kv_mm")
                fl = _mm(hkv, w["w_f"], tn=LANES, name="f_mm")
                cum = _fgate_fwd(fl, w["b_f_pad"], name="fgate_fwd")
                kh, vh = _heads(kvm[:, :D_MODEL]), _heads(kvm[:, D_MODEL:])
                ck = cum[:, :N_HEADS].T.reshape(N_HEADS, 1, L)
                kvs = (h, hkv, fl, kh, vh, ck)
            _, _, _, kh, vh, ck = kvs
            hn = _rms_fwd(h, vec(w["g_mix"][layer]), name=f"mix_norm{layer}")
            qh = _heads(_mm(hn, w["w_q"][j], out_dtype=BF16, name=f"q_mm{layer}"))
            oh16, oh, lse = _attn_fwd(qh, kh, vh, ck, name=f"attn_fwd{layer}")
            o = _unheads(oh16)
            h1 = _mm(o, w["w_o"][j], add=h, name=f"o_mm{layer}")
            mix_saved = (h, hn, qh, oh, o, lse)
            hn2 = None
        if hn2 is None:
            hn2 = _rms_fwd(h1, vec(w["g_ffn"][layer]), name=f"ffn_norm{layer}")
        u0 = _mm(hn2, w["w_ffn_in"][layer], out_split=True, tn=256, name=f"ffn_in{layer}")
        a = _conv_act(u0, w["conv_w"][layer], w["conv_b"][layer], name=f"ffn_act{layer}")
        h2 = _mm(a, w["w_ffn_out"][layer], add=h1, tk=1408, name=f"ffn_out{layer}")
        saved.append((mix_saved, h1, hn2, u0))
        h = h2

    loss, dh, dg_final = _loss_head(h, vec(w["g_final"]), target, name="loss_head")
    grads["g_final"] = dg_final.reshape(-1)

    gl = {k: [None] * DEPTH for k in ("g_mix", "g_ffn", "w_ffn_in", "conv_w", "conv_b", "w_ffn_out")}
    ga = {k: [None] * N_A for k in ("lam_re", "lam_im", "log_dt", "ssm_b_re", "ssm_b_im", "ssm_c_re", "ssm_c_im",
                                    "ssm_d", "w_glu")}
    gb = {k: [None] * (DEPTH - N_A) for k in ("w_q", "w_o")}
    dkh = dvh = dck = None
    for layer in reversed(range(DEPTH)):
        mix_saved, h1, hn2, u0 = saved[layer]
        cw, cb = w["conv_w"][layer], w["conv_b"][layer]
        da = _mm(dh, w["w_ffn_out"][layer], tb=True, tn=256, tk=1024, name=f"ffn_da{layer}")
        a, du, dcb = _conv_act_bwd(u0, cw, cb, da, name=f"ffn_act_bwd{layer}")
        gl["w_ffn_out"][layer] = _mm(a, dh, ta=True, tm=256, name=f"ffn_dwout{layer}")
        du0, dcw = _conv_bwd(u0, cw, du, name=f"ffn_conv_bwd{layer}")
        gl["w_ffn_in"][layer] = _mm(hn2, du0, ta=True, b_split=True, tn=256, name=f"ffn_dwin{layer}")
        dhn2 = _mm(du0, w["w_ffn_in"][layer], a_split=True, tb=True, tk=1408, name=f"ffn_dhn{layer}")
        dh1, dg = _rms_bwd(h1, vec(w["g_ffn"][layer]), dhn2, dh, name=f"ffn_norm_bwd{layer}")
        gl["g_ffn"][layer], gl["conv_w"][layer], gl["conv_b"][layer] = dg.reshape(-1), dcw, dcb
        if layer < N_A:
            hin, hn, ypre, yg, sb, z, a1, a2, wb16, wc16, dvec, prep_vjp = mix_saved
            dz = _glu_bwd(z, dh1, name=f"glu_bwd{layer}")
            ga["w_glu"][layer] = _mm(yg, dz, ta=True, name=f"glu_dw{layer}")
            dyg = _mm(dz, w["w_glu"][layer], tb=True, out_dtype=BF16, name=f"glu_dy{layer}")
            du, dwb, dwc, da1, da2, dd = _s5_bwd(hn, dyg, ypre, sb, wb16, wc16, a1, a2, dvec, name=f"s5_bwd{layer}")
            dparams = prep_vjp((da1, da2, dwb, dwc))
            for nme, val in zip(("lam_re", "lam_im", "log_dt", "ssm_b_re", "ssm_b_im", "ssm_c_re", "ssm_c_im"), dparams):
                ga[nme][layer] = val
            ga["ssm_d"][layer] = dd.reshape(-1)
            dh, dg = _rms_bwd(hin, vec(w["g_mix"][layer]), du, dh1, name=f"mix_norm_bwd{layer}")
        else:
            j = layer - N_A
            hin, hn, qh, oh, o, lse = mix_saved
            _, _, _, kh, vh, ck = kvs
            gb["w_o"][j] = _mm(o, dh1, ta=True, name=f"o_dw{layer}")
            doh = _heads(_mm(dh1, w["w_o"][j], tb=True, out_dtype=BF16, name=f"o_dx{layer}"))
            dk_l, dv_l, dck_l = _attn_bwd_kv(qh, kh, vh, ck, oh, doh, lse, name=f"attn_bwd_kv{layer}")
            dqh = _attn_bwd_q(qh, kh, vh, ck, oh, doh, lse, name=f"attn_bwd_q{layer}")
            dkh = dk_l if dkh is None else dkh + dk_l
            dvh = dv_l if dvh is None else dvh + dv_l
            dck = dck_l if dck is None else dck + dck_l
            dq = _unheads(dqh)
            gb["w_q"][j] = _mm(hn, dq, ta=True, name=f"q_dw{layer}")
            dhn = _mm(dq, w["w_q"][j], tb=True, name=f"q_dx{layer}")
            dh, dg = _rms_bwd(hin, vec(w["g_mix"][layer]), dhn, dh1, name=f"mix_norm_bwd{layer}")
            if layer == N_A:
                hkv_in, hkv, fl, _, _, _ = kvs
                dcum = jnp.pad(dck.reshape(N_HEADS, L).T, ((0, 0), (0, LANES - N_HEADS)))
                dfl, dbf = _fgate_bwd(fl, w["b_f_pad"], dcum, name="fgate_bwd")
                dkv = jnp.concatenate([_unheads(dkh), _unheads(dvh)], axis=1).astype(BF16)
                dfl16 = dfl.astype(BF16)
                grads["w_kv"] = _mm(hkv, dkv, ta=True, name="kv_dw")
                grads["w_f"] = _mm(hkv, dfl16, ta=True, tn=LANES, name="f_dw")
                dhkv = _mm(dkv, w["w_kv"], tb=True, name="kv_dx")
                dhkv = _mm(dfl16, w["w_f"], tb=True, add=dhkv, tk=LANES, name="f_dx")
                grads["b_f"] = dbf[0, :N_HEADS]
                dh, dgkv = _rms_bwd(hkv_in, vec(w["g_kv"]), dhkv, dh, name="kv_norm_bwd")
                grads["g_kv"] = dgkv.reshape(-1)
        gl["g_mix"][layer] = dg.reshape(-1)

    for d in (gl, ga, gb):
        for k, v in d.items():
            grads[k] = jnp.stack(v)
    return loss, dh, grads


def _exchange(x, *, scatter, name):
    R, C = x.shape[-2:]

    def body(x_ref, o_ref, send_sems, recv_sems, local_sem):
        xi, yi, ci = lax.axis_index("x"), lax.axis_index("y"), lax.axis_index("c")
        me = 4 * xi + 2 * yi + ci

        def src(p):
            return x_ref.at[p] if scatter else x_ref

        own = pltpu.make_async_copy(src(me), o_ref.at[me], local_sem)
        own.start()
        sends, recvs = [], []
        for k in range(1, N_DEV):
            px, py, pc = xi ^ (k >> 2), yi ^ ((k >> 1) & 1), ci ^ (k & 1)
            p = 4 * px + 2 * py + pc
            sends.append(pltpu.make_async_remote_copy(
                src_ref=src(p), dst_ref=o_ref.at[me], send_sem=send_sems.at[k - 1], recv_sem=recv_sems.at[k - 1],
                device_id=(px, py, pc), device_id_type=pl.DeviceIdType.MESH))
            recvs.append(pltpu.make_async_remote_copy(
                src_ref=src(p), dst_ref=o_ref.at[p], send_sem=send_sems.at[k - 1], recv_sem=recv_sems.at[k - 1],
                device_id=(px, py, pc), device_id_type=pl.DeviceIdType.MESH))
        for cp in sends:
            cp.start()
        for cp in recvs:
            cp.wait_recv()
        for cp in sends:
            cp.wait_send()
        own.wait()

    return pl.pallas_call(
        body, out_shape=jax.ShapeDtypeStruct((N_DEV, R, C), x.dtype),
        in_specs=[pl.BlockSpec(memory_space=pl.ANY)], out_specs=pl.BlockSpec(memory_space=pl.ANY),
        scratch_shapes=[pltpu.SemaphoreType.DMA((N_DEV - 1,)), pltpu.SemaphoreType.DMA((N_DEV - 1,)),
                        pltpu.SemaphoreType.DMA],
        compiler_params=pltpu.CompilerParams(has_side_effects=True), name=name)(x)


def _reduce_adamw(slots, w, m, v, *, name):
    _, R, C = slots.shape
    tr = _tile(R, 256, 16)
    c1 = 1.0 / (1.0 - ADAM_B1 ** ADAM_STEP)
    c2 = 1.0 / (1.0 - ADAM_B2 ** ADAM_STEP)

    def body(s_ref, w_ref, m_ref, v_ref, g_ref, d_ref, nm_ref, nv_ref):
        g = s_ref[0].astype(F32)
        for d in range(1, N_DEV):
            g = g + s_ref[d].astype(F32)
        m2 = ADAM_B1 * m_ref[...] + (1.0 - ADAM_B1) * g
        v2 = ADAM_B2 * v_ref[...] + (1.0 - ADAM_B2) * (g * g)
        g_ref[...] = g
        nm_ref[...] = m2
        nv_ref[...] = v2
        d_ref[...] = -ADAM_LR * ((m2 * c1) / (jnp.sqrt(v2 * c2) + ADAM_EPS) + ADAM_WD * w_ref[...])

    row = pl.BlockSpec((tr, C), lambda i: (i, 0))
    out = jax.ShapeDtypeStruct((R, C), F32)
    return pl.pallas_call(
        body, grid=(R // tr,), in_specs=[pl.BlockSpec((N_DEV, tr, C), lambda i: (0, i, 0)), row, row, row],
        out_specs=[row, row, row, row], out_shape=[out, out, out, out],
        compiler_params=_params("parallel"), name=name)(slots, w, m, v)


def _rows(a, rows, width=D_MODEL):
    a = a.reshape(-1)
    if a.shape[0] != rows * width:
        a = jnp.pad(a, (0, rows * width - a.shape[0]))
    return a.reshape(rows, width)


_BIG = (
    ("w_glu", 512,
     lambda g: g.reshape(8, 2, 1024, 256).transpose(1, 2, 0, 3).reshape(2, 1024, 2048),
     lambda f: f.reshape(2, 1024, 8, 256).transpose(2, 0, 1, 3).reshape(8, 512, 1024)),
    ("w_kvf", 272,
     lambda g: g[:, :258].reshape(8, 1024, 258).transpose(1, 0, 2).reshape(1024, 2064),
     lambda f: jnp.pad(f.reshape(1024, 8, 258).transpose(1, 0, 2).reshape(8, 258, 1024), ((0, 0), (0, 14), (0, 0)))),
    ("w_q", 256,
     lambda g: g.reshape(8, 2, 128, 1024).transpose(1, 0, 2, 3).reshape(2, 1024, 1024),
     lambda f: f.reshape(2, 8, 128, 1024).transpose(1, 0, 2, 3).reshape(8, 256, 1024)),
    ("w_o", 256,
     lambda g: g.reshape(8, 2, 128, 1024).transpose(1, 0, 2, 3).reshape(2, 1024, 1024),
     lambda f: f.reshape(2, 8, 128, 1024).transpose(1, 0, 2, 3).reshape(8, 256, 1024)),
    ("w_ffn_in", 2816,
     lambda g: g.reshape(8, 4, 1024, 704).transpose(1, 2, 0, 3).reshape(4, 1024, 5632),
     lambda f: f.reshape(4, 1024, 8, 704).transpose(2, 0, 1, 3).reshape(8, 2816, 1024)),
    ("w_ffn_out", 1408,
     lambda g: g.reshape(8, 4, 352, 1024).transpose(1, 0, 2, 3).reshape(4, 2816, 1024),
     lambda f: f.reshape(4, 8, 352, 1024).transpose(1, 0, 2, 3).reshape(8, 1408, 1024)),
)
_SMALL_ROWS = 72
_REPL = ("g_mix", "g_ffn", "lam_re", "lam_im", "log_dt", "ssm_b_re", "ssm_b_im", "ssm_c_re", "ssm_c_im",
         "g_kv", "b_f", "ffn_conv_b", "g_final")
_REPL_ROWS = 576
_ORDER = ("g_mix", "g_ffn", "lam_re", "lam_im", "log_dt", "ssm_b_re", "ssm_b_im", "ssm_c_re", "ssm_c_im", "ssm_d",
          "w_glu", "g_kv", "w_kvf", "b_f", "w_q", "w_o", "w_ffn_in", "ffn_conv_w", "ffn_conv_b", "w_ffn_out", "g_final")


def _pack_big(shards, dtype):
    return jnp.concatenate([_rows(shards[n].astype(dtype), r) for n, r, _, _ in _BIG], axis=0)


def _unpack_big(flat, shapes):
    out, off = {}, 0
    for n, r, _, _ in _BIG:
        size = math.prod(shapes[n])
        out[n] = flat[off:off + r].reshape(-1)[:size].reshape(shapes[n])
        off += r
    return out


def _pack_small(ssm_d, conv_w):
    flat = jnp.concatenate([ssm_d.reshape(-1), conv_w.reshape(-1)])
    return _rows(flat, _SMALL_ROWS, LANES)


def _unpack_small(flat):
    flat = flat.reshape(-1)
    return flat[:256].reshape(2, 128), flat[256:256 + 8448].reshape(4, 3, 704)


def _pack_repl(d):
    return _rows(jnp.concatenate([d[n].reshape(-1) for n in _REPL]), _REPL_ROWS)


def _unpack_repl(flat, shapes):
    flat, out, off = flat.reshape(-1), {}, 0
    for n in _REPL:
        size = math.prod(shapes[n])
        out[n] = flat[off:off + size].reshape(shapes[n])
        off += size
    return out


def kernel(x, g_mix, g_ffn, lam_re, lam_im, log_dt, ssm_b_re, ssm_b_im, ssm_c_re, ssm_c_im, ssm_d, w_glu, g_kv, w_kvf, b_f, w_q, w_o, w_ffn_in, ffn_conv_w, ffn_conv_b, w_ffn_out, g_final, loss_target, m_g_mix, m_g_ffn, m_lam_re, m_lam_im, m_log_dt, m_ssm_b_re, m_ssm_b_im, m_ssm_c_re, m_ssm_c_im, m_ssm_d, m_w_glu, m_g_kv, m_w_kvf, m_b_f, m_w_q, m_w_o, m_w_ffn_in, m_ffn_conv_w, m_ffn_conv_b, m_w_ffn_out, m_g_final, v_g_mix, v_g_ffn, v_lam_re, v_lam_im, v_log_dt, v_ssm_b_re, v_ssm_b_im, v_ssm_c_re, v_ssm_c_im, v_ssm_d, v_w_glu, v_g_kv, v_w_kvf, v_b_f, v_w_q, v_w_o, v_w_ffn_in, v_ffn_conv_w, v_ffn_conv_b, v_w_ffn_out, v_g_final):
    wts = dict(g_mix=g_mix, g_ffn=g_ffn, lam_re=lam_re, lam_im=lam_im, log_dt=log_dt, ssm_b_re=ssm_b_re,
               ssm_b_im=ssm_b_im, ssm_c_re=ssm_c_re, ssm_c_im=ssm_c_im, ssm_d=ssm_d, w_glu=w_glu, g_kv=g_kv,
               w_kvf=w_kvf, b_f=b_f, w_q=w_q, w_o=w_o, w_ffn_in=w_ffn_in, ffn_conv_w=ffn_conv_w,
               ffn_conv_b=ffn_conv_b, w_ffn_out=w_ffn_out, g_final=g_final)
    mom = dict(g_mix=m_g_mix, g_ffn=m_g_ffn, lam_re=m_lam_re, lam_im=m_lam_im, log_dt=m_log_dt, ssm_b_re=m_ssm_b_re,
               ssm_b_im=m_ssm_b_im, ssm_c_re=m_ssm_c_re, ssm_c_im=m_ssm_c_im, ssm_d=m_ssm_d, w_glu=m_w_glu,
               g_kv=m_g_kv, w_kvf=m_w_kvf, b_f=m_b_f, w_q=m_w_q, w_o=m_w_o, w_ffn_in=m_w_ffn_in,
               ffn_conv_w=m_ffn_conv_w, ffn_conv_b=m_ffn_conv_b, w_ffn_out=m_w_ffn_out, g_final=m_g_final)
    var = dict(g_mix=v_g_mix, g_ffn=v_g_ffn, lam_re=v_lam_re, lam_im=v_lam_im, log_dt=v_log_dt, ssm_b_re=v_ssm_b_re,
               ssm_b_im=v_ssm_b_im, ssm_c_re=v_ssm_c_re, ssm_c_im=v_ssm_c_im, ssm_d=v_ssm_d, w_glu=v_w_glu,
               g_kv=v_g_kv, w_kvf=v_w_kvf, b_f=v_b_f, w_q=v_w_q, w_o=v_w_o, w_ffn_in=v_w_ffn_in,
               ffn_conv_w=v_ffn_conv_w, ffn_conv_b=v_ffn_conv_b, w_ffn_out=v_w_ffn_out, g_final=v_g_final)
    shapes = {n: a.shape for n, a in wts.items()}
    F = D_FF

    big = _exchange(_pack_big(wts, BF16), scatter=False, name="gather_big")
    small = _exchange(_pack_small(ssm_d, ffn_conv_w), scatter=False, name="gather_small")
    full, off = {}, 0
    for n, r, to_full, _ in _BIG:
        full[n] = to_full(big[:, off:off + r])
        off += r
    small = small.reshape(N_DEV, -1)
    ssm_d_full = small[:, :256].reshape(8, 2, 128).transpose(1, 0, 2).reshape(2, D_MODEL)
    conv_w_full = small[:, 256:256 + 8448].reshape(8, 4, 3, 704).transpose(1, 2, 0, 3).reshape(4, 3, 2, F)

    w = dict(wts)
    w.update(w_glu=full["w_glu"], w_q=full["w_q"], w_o=full["w_o"], w_ffn_in=full["w_ffn_in"],
             w_ffn_out=full["w_ffn_out"], ssm_d=ssm_d_full)
    w["w_kv"] = full["w_kvf"][:, :2 * D_MODEL]
    w["w_f"] = jnp.pad(full["w_kvf"][:, 2 * D_MODEL:], ((0, 0), (0, LANES - N_HEADS)))
    w["b_f_pad"] = jnp.pad(b_f, (0, LANES - N_HEADS)).reshape(1, LANES)
    w["conv_w"] = conv_w_full.transpose(0, 2, 1, 3)
    w["conv_b"] = ffn_conv_b.reshape(DEPTH, 2, 1, F)

    loss, dx, g = local_step(x[0], loss_target[0], w)
    loss = lax.psum(loss[0, 0], MESH_AXES)

    g["w_kvf"] = jnp.concatenate([g["w_kv"], g["w_f"][:, :N_HEADS]], axis=1)
    gbig = jnp.concatenate([to_blocks(g[n]).astype(BF16) for n, _, _, to_blocks in _BIG], axis=1)
    gbig = _exchange(gbig, scatter=True, name="scatter_big")
    g_d = g["ssm_d"].reshape(2, 8, 128).transpose(1, 0, 2).reshape(8, 256)
    g_cw = g["conv_w"].transpose(0, 2, 1, 3).reshape(4, 3, 8, 704).transpose(2, 0, 1, 3).reshape(8, 8448)
    gsmall = jnp.concatenate([g_d, g_cw, jnp.zeros((8, _SMALL_ROWS * LANES - 8704), F32)], axis=1)
    gsmall = _exchange(gsmall.reshape(8, _SMALL_ROWS, LANES), scatter=True, name="scatter_small")
    g["ffn_conv_b"] = g["conv_b"].reshape(DEPTH, 2 * F)
    grepl = _exchange(_pack_repl(g), scatter=False, name="gather_repl")

    res = {}
    outs = _reduce_adamw(gbig, _pack_big(wts, F32), _pack_big(mom, F32), _pack_big(var, F32), name="adamw_big")
    for kind, flat in zip(("grad", "delta", "m", "v"), outs):
        for n, a in _unpack_big(flat, shapes).items():
            res[kind, n] = a
    outs = _reduce_adamw(gsmall, _pack_small(ssm_d, ffn_conv_w), _pack_small(m_ssm_d, m_ffn_conv_w),
                         _pack_small(v_ssm_d, v_ffn_conv_w), name="adamw_small")
    for kind, flat in zip(("grad", "delta", "m", "v"), outs):
        res[kind, "ssm_d"], res[kind, "ffn_conv_w"] = _unpack_small(flat)
    outs = _reduce_adamw(grepl, _pack_repl(wts), _pack_repl(mom), _pack_repl(var), name="adamw_repl")
    for kind, flat in zip(("grad", "delta", "m", "v"), outs):
        for n, a in _unpack_repl(flat, shapes).items():
            res[kind, n] = a

    return (loss, dx[None], *[res[kind, n] for kind in ("grad", "delta", "m", "v") for n in _ORDER])
```

```python
import functools
import math

import jax
import jax.numpy as jnp
from jax import lax
from jax.experimental import pallas as pl
from jax.experimental.pallas import tpu as pltpu

F32 = jnp.float32
BF16 = jnp.bfloat16

D_MODEL = 1024
DEPTH = 4
N_A = 2
N_GROUPS = 64
SSM_GROUP = 16
SSM_STATE = 64
N_HEADS = 16
HEAD_DIM = 64
ATTN_SCALE = HEAD_DIM ** -0.5
D_FF = 2816
EPS = 1e-6
N_DEV = 8
LANES = 128
SUBLANES = 8

ADAM_LR = 0.001
ADAM_B1 = 0.9
ADAM_B2 = 0.999
ADAM_EPS = 1e-08
ADAM_WD = 0.01
ADAM_STEP = 10

ROW_TILE = 512
S5_CHUNK = 256
ATTN_TILE = 512
CUM_TILE = 256
NEG = -1e30

MESH_AXES = ("x", "y", "c")


def _tile(n, target, align=LANES):
    t = (min(target, n) // align) * align
    while t >= align:
        if n % t == 0:
            return t
        t -= align
    return n


def _params(*sem):
    return pltpu.CompilerParams(dimension_semantics=sem, vmem_limit_bytes=56 * 1024 * 1024)


def _mm(a, b, *, ta=False, tb=False, add=None, out_dtype=F32, a_split=False, b_split=False, out_split=False,
        tm=1024, tn=1024, tk=1024, name):
    if a_split:
        M, K = a.shape[1], 2 * a.shape[2]
    else:
        M, K = (a.shape[1], a.shape[0]) if ta else a.shape
    if b_split:
        N = 2 * b.shape[2]
    else:
        N = b.shape[0] if tb else b.shape[1]
    tm = _tile(M, tm)
    tn = _tile(N // 2 if (b_split or out_split) else N, tn)
    tk = _tile(K // 2 if a_split else K, tk)
    nm, nn, nk = M // tm, N // tn, K // tk

    if a_split:
        hk = nk // 2
        a_spec = pl.BlockSpec((None, tm, tk), lambda i, j, k: (k // hk, i, k % hk))
    elif ta:
        a_spec = pl.BlockSpec((tk, tm), lambda i, j, k: (k, i))
    else:
        a_spec = pl.BlockSpec((tm, tk), lambda i, j, k: (i, k))
    if b_split:
        hn = nn // 2
        b_spec = pl.BlockSpec((None, tk, tn), lambda i, j, k: (j // hn, k, j % hn))
    elif tb:
        b_spec = pl.BlockSpec((tn, tk), lambda i, j, k: (j, k))
    else:
        b_spec = pl.BlockSpec((tk, tn), lambda i, j, k: (k, j))
    if out_split:
        hn = nn // 2
        o_spec = pl.BlockSpec((None, tm, tn), lambda i, j, k: (j // hn, i, j % hn))
        out_shape = jax.ShapeDtypeStruct((2, M, N // 2), out_dtype)
    else:
        o_spec = pl.BlockSpec((tm, tn), lambda i, j, k: (i, j))
        out_shape = jax.ShapeDtypeStruct((M, N), out_dtype)
    dims = (((0 if ta else 1,), (1 if tb else 0,)), ((), ()))
    has_add = add is not None

    def body(*refs):
        a_ref, b_ref = refs[0], refs[1]
        add_ref = refs[2] if has_add else None
        o_ref = refs[3] if has_add else refs[2]
        part = lax.dot_general(a_ref[...].astype(BF16), b_ref[...].astype(BF16), dims, preferred_element_type=F32)

        def finish(r):
            if has_add:
                r = r + add_ref[...]
            o_ref[...] = r.astype(out_dtype)

        if nk == 1:
            finish(part)
            return
        acc = refs[-1]
        k = pl.program_id(2)

        @pl.when(k == 0)
        def _():
            acc[...] = part

        @pl.when(k > 0)
        def _():
            acc[...] += part

        @pl.when(k == nk - 1)
        def _():
            finish(acc[...])

    in_specs = [a_spec, b_spec]
    args = [a, b]
    if has_add:
        in_specs.append(pl.BlockSpec((tm, tn), lambda i, j, k: (i, j)))
        args.append(add)
    return pl.pallas_call(
        body, grid=(nm, nn, nk), in_specs=in_specs, out_specs=o_spec, out_shape=out_shape,
        scratch_shapes=[pltpu.VMEM((tm, tn), F32)] if nk > 1 else [],
        compiler_params=_params("parallel", "parallel", "arbitrary"), name=name)(*args)


def _rms_fwd(h, g, *, name):
    L, D = h.shape
    tr = _tile(L, ROW_TILE, SUBLANES)

    def body(h_ref, g_ref, o_ref):
        x = h_ref[...]
        r = lax.rsqrt(jnp.mean(x * x, axis=1, keepdims=True) + EPS)
        o_ref[...] = (x * r * g_ref[...]).astype(BF16)

    return pl.pallas_call(
        body, grid=(L // tr,),
        in_specs=[pl.BlockSpec((tr, D), lambda i: (i, 0)), pl.BlockSpec((1, D), lambda i: (0, 0))],
        out_specs=pl.BlockSpec((tr, D), lambda i: (i, 0)), out_shape=jax.ShapeDtypeStruct((L, D), BF16),
        compiler_params=_params("parallel"), name=name)(h, g)


def _rms_bwd(h, g, dy, dres, *, name):
    L, D = h.shape
    tr = _tile(L, ROW_TILE, SUBLANES)

    def body(h_ref, g_ref, dy_ref, dres_ref, dh_ref, dg_ref):
        @pl.when(pl.program_id(0) == 0)
        def _():
            dg_ref[...] = jnp.zeros_like(dg_ref)

        x = h_ref[...]
        r = lax.rsqrt(jnp.mean(x * x, axis=1, keepdims=True) + EPS)
        xn = x * r
        dy = dy_ref[...].astype(F32)
        gdy = dy * g_ref[...]
        dx = r * (gdy - xn * jnp.mean(gdy * xn, axis=1, keepdims=True))
        dh_ref[...] = dres_ref[...] + dx
        dg_ref[...] += jnp.sum(dy * xn, axis=0, keepdims=True)

    row = pl.BlockSpec((tr, D), lambda i: (i, 0))
    vec = pl.BlockSpec((1, D), lambda i: (0, 0))
    return pl.pallas_call(
        body, grid=(L // tr,), in_specs=[row, vec, row, row], out_specs=[row, vec],
        out_shape=[jax.ShapeDtypeStruct((L, D), F32), jax.ShapeDtypeStruct((1, D), F32)],
        compiler_params=_params("arbitrary"), name=name)(h, g, dy, dres)


def _glu_res_rms(z, h, g, *, name):
    L, D = h.shape
    tr = _tile(L, ROW_TILE, SUBLANES)

    def body(za_ref, zg_ref, h_ref, g_ref, h1_ref, hn_ref):
        x = h_ref[...] + za_ref[...] * jax.nn.sigmoid(zg_ref[...])
        h1_ref[...] = x
        r = lax.rsqrt(jnp.mean(x * x, axis=1, keepdims=True) + EPS)
        hn_ref[...] = (x * r * g_ref[...]).astype(BF16)

    row = pl.BlockSpec((tr, D), lambda i: (i, 0))
    return pl.pallas_call(
        body, grid=(L // tr,),
        in_specs=[row, pl.BlockSpec((tr, D), lambda i: (i, 1)), row, pl.BlockSpec((1, D), lambda i: (0, 0))],
        out_specs=[row, row],
        out_shape=[jax.ShapeDtypeStruct((L, D), F32), jax.ShapeDtypeStruct((L, D), BF16)],
        compiler_params=_params("parallel"), name=name)(z, z, h, g)


def _glu_bwd(z, dout, *, name):
    L, D = dout.shape
    tr = _tile(L, ROW_TILE, SUBLANES)

    def body(za_ref, zg_ref, d_ref, o_ref):
        sg = jax.nn.sigmoid(zg_ref[...])
        d = d_ref[...]
        o_ref[:, :D] = (d * sg).astype(BF16)
        o_ref[:, D:] = (d * za_ref[...] * sg * (1.0 - sg)).astype(BF16)

    row = pl.BlockSpec((tr, D), lambda i: (i, 0))
    return pl.pallas_call(
        body, grid=(L // tr,),
        in_specs=[row, pl.BlockSpec((tr, D), lambda i: (i, 1)), row],
        out_specs=pl.BlockSpec((tr, 2 * D), lambda i: (i, 0)),
        out_shape=jax.ShapeDtypeStruct((L, 2 * D), BF16),
        compiler_params=_params("parallel"), name=name)(z, z, dout)


def _loss_head(h, g, target, *, name):
    L, D = h.shape
    tr = _tile(L, ROW_TILE, SUBLANES)

    def body(h_ref, g_ref, t_ref, loss_ref, dh_ref, dg_ref):
        @pl.when(pl.program_id(0) == 0)
        def _():
            dg_ref[...] = jnp.zeros_like(dg_ref)
            loss_ref[...] = jnp.zeros_like(loss_ref)

        x = h_ref[...]
        gg = g_ref[...]
        r = lax.rsqrt(jnp.mean(x * x, axis=1, keepdims=True) + EPS)
        xn = x * r
        err = xn * gg - t_ref[...]
        loss_ref[...] += 0.5 * jnp.sum(jnp.mean(err * err, axis=1, keepdims=True), axis=0, keepdims=True)
        dy = err * (1.0 / D)
        gdy = dy * gg
        dh_ref[...] = r * (gdy - xn * jnp.mean(gdy * xn, axis=1, keepdims=True))
        dg_ref[...] += jnp.sum(dy * xn, axis=0, keepdims=True)

    row = pl.BlockSpec((tr, D), lambda i: (i, 0))
    vec = pl.BlockSpec((1, D), lambda i: (0, 0))
    return pl.pallas_call(
        body, grid=(L // tr,), in_specs=[row, vec, row],
        out_specs=[pl.BlockSpec((1, 1), lambda i: (0, 0)), row, vec],
        out_shape=[jax.ShapeDtypeStruct((1, 1), F32), jax.ShapeDtypeStruct((L, D), F32),
                   jax.ShapeDtypeStruct((1, D), F32)],
        compiler_params=_params("arbitrary"), name=name)(h, g, target)


CONV_COL_TILE = 256


def _shift_down(x, halo, k, row):
    y = pltpu.roll(x, k, 0)
    for r in range(k):
        y = jnp.where(row == r, halo[SUBLANES - k + r:SUBLANES - k + r + 1, :], y)
    return y


def _shift_up(x, halo, k, row, n):
    y = pltpu.roll(x, n - k, 0)
    for r in range(k):
        y = jnp.where(row == n - k + r, halo[r:r + 1, :], y)
    return y


def _conv_specs(L, F, tr, tc):
    nrb = tr // SUBLANES
    main = pl.BlockSpec((2, tr, tc), lambda j, i: (0, i, j))
    prev = pl.BlockSpec((2, SUBLANES, tc), lambda j, i: (0, jnp.maximum(i * nrb - 1, 0), j))
    nxt = pl.BlockSpec((2, SUBLANES, tc), lambda j, i: (0, jnp.minimum((i + 1) * nrb, L // SUBLANES - 1), j))
    cw = pl.BlockSpec((2, 3, tc), lambda j, i: (0, 0, j))
    cb = pl.BlockSpec((2, 1, tc), lambda j, i: (0, 0, j))
    half = pl.BlockSpec((tr, tc), lambda j, i: (i, j))
    return main, prev, nxt, cw, cb, half


def _conv_tile(u_ref, p_ref, w_ref, b_ref, s, first, row):
    x = u_ref[s]
    halo = jnp.where(first, 0.0, p_ref[s])
    w = w_ref[s]
    x1 = _shift_down(x, halo, 1, row)
    x2 = _shift_down(x, halo, 2, row)
    return b_ref[s] + x2 * w[0:1] + x1 * w[1:2] + x * w[2:3], x1, x2


def _conv_act(u0, cw, cb, *, name):
    _, L, F = u0.shape
    tr, tc = _tile(L, ROW_TILE, SUBLANES), _tile(F, CONV_COL_TILE)
    main, prev, _, cws, cbs, half = _conv_specs(L, F, tr, tc)

    def body(u_ref, p_ref, w_ref, b_ref, a_ref):
        first = pl.program_id(1) == 0
        row = lax.broadcasted_iota(jnp.int32, (tr, tc), 0)
        gate, _, _ = _conv_tile(u_ref, p_ref, w_ref, b_ref, 0, first, row)
        up, _, _ = _conv_tile(u_ref, p_ref, w_ref, b_ref, 1, first, row)
        a_ref[...] = (gate * jax.nn.sigmoid(gate) * up).astype(BF16)

    return pl.pallas_call(
        body, grid=(F // tc, L // tr), in_specs=[main, prev, cws, cbs], out_specs=half,
        out_shape=jax.ShapeDtypeStruct((L, F), BF16),
        compiler_params=_params("parallel", "parallel"), name=name)(u0, u0, cw, cb)


def _conv_act_bwd(u0, cw, cb, da, *, name):
    _, L, F = u0.shape
    tr, tc = _tile(L, ROW_TILE, SUBLANES), _tile(F, CONV_COL_TILE)
    main, prev, _, cws, cbs, half = _conv_specs(L, F, tr, tc)

    def body(u_ref, p_ref, w_ref, b_ref, da_ref, a_ref, du_ref, dcb_ref):
        first = pl.program_id(1) == 0

        @pl.when(first)
        def _():
            dcb_ref[...] = jnp.zeros_like(dcb_ref)

        row = lax.broadcasted_iota(jnp.int32, (tr, tc), 0)
        gate, _, _ = _conv_tile(u_ref, p_ref, w_ref, b_ref, 0, first, row)
        up, _, _ = _conv_tile(u_ref, p_ref, w_ref, b_ref, 1, first, row)
        sg = jax.nn.sigmoid(gate)
        silu = gate * sg
        a_ref[...] = (silu * up).astype(BF16)
        da = da_ref[...]
        dgate = da * up * (sg * (1.0 + gate * (1.0 - sg)))
        dup = da * silu
        du_ref[0] = dgate
        du_ref[1] = dup
        dcb_ref[0] += jnp.sum(dgate, axis=0, keepdims=True)
        dcb_ref[1] += jnp.sum(dup, axis=0, keepdims=True)

    return pl.pallas_call(
        body, grid=(F // tc, L // tr), in_specs=[main, prev, cws, cbs, half],
        out_specs=[half, main, cbs],
        out_shape=[jax.ShapeDtypeStruct((L, F), BF16), jax.ShapeDtypeStruct((2, L, F), F32),
                   jax.ShapeDtypeStruct((2, 1, F), F32)],
        compiler_params=_params("parallel", "arbitrary"), name=name)(u0, u0, cw, cb, da)


def _conv_bwd(u0, cw, du, *, name):
    _, L, F = u0.shape
    tr, tc = _tile(L, ROW_TILE, SUBLANES), _tile(F, CONV_COL_TILE)
    main, prev, nxt, cws, _, _ = _conv_specs(L, F, tr, tc)
    nr = L // tr

    def body(u_ref, p_ref, w_ref, du_ref, n_ref, du0_ref, dcw_ref):
        i = pl.program_id(1)

        @pl.when(i == 0)
        def _():
            dcw_ref[...] = jnp.zeros_like(dcw_ref)

        row = lax.broadcasted_iota(jnp.int32, (tr, tc), 0)
        for s in range(2):
            x = u_ref[s]
            halo = jnp.where(i == 0, 0.0, p_ref[s])
            x1 = _shift_down(x, halo, 1, row)
            x2 = _shift_down(x, halo, 2, row)
            d = du_ref[s]
            nh = jnp.where(i == nr - 1, 0.0, n_ref[s])
            d1 = _shift_up(d, nh, 1, row, tr)
            d2 = _shift_up(d, nh, 2, row, tr)
            w = w_ref[s]
            du0_ref[s] = (d * w[2:3] + d1 * w[1:2] + d2 * w[0:1]).astype(BF16)
            dcw_ref[s, 0:1, :] += jnp.sum(d * x2, axis=0, keepdims=True)
            dcw_ref[s, 1:2, :] += jnp.sum(d * x1, axis=0, keepdims=True)
            dcw_ref[s, 2:3, :] += jnp.sum(d * x, axis=0, keepdims=True)

    return pl.pallas_call(
        body, grid=(F // tc, nr), in_specs=[main, prev, cws, main, nxt],
        out_specs=[main, cws],
        out_shape=[jax.ShapeDtypeStruct((2, L, F), BF16), jax.ShapeDtypeStruct((2, 3, F), F32)],
        compiler_params=_params("parallel", "arbitrary"), name=name)(u0, u0, cw, du, du)


N_TILES = 64
HALF = N_TILES // 2


def _swap(s):
    return jnp.concatenate([s[HALF:], s[:HALF]], axis=0)


def _chan_block(j):
    return ((j % HALF) // 4) * LANES


def _tiles_of(jb):
    return [4 * jb + i for i in range(4)] + [HALF + 4 * jb + i for i in range(4)]
GELU_C = math.sqrt(2.0 / math.pi)
GELU_A = 0.044715


def _gelu(x):
    return 0.5 * x * (1.0 + jnp.tanh(GELU_C * (x + GELU_A * x * x * x)))


def _gelu_grad(x):
    th = jnp.tanh(GELU_C * (x + GELU_A * x * x * x))
    return 0.5 * (1.0 + th) + 0.5 * x * (1.0 - th * th) * GELU_C * (1.0 + 3.0 * GELU_A * x * x)


def _s5_project_in(u_ref, wb_ref, s3, T, TP):
    for j in range(N_TILES):
        blk = _chan_block(j)
        s3[pl.ds(j * TP + SUBLANES, T), :] = jnp.dot(u_ref[:, blk:blk + LANES], wb_ref[j],
                                                     preferred_element_type=F32)


def _s5_scan_fwd(s3, a1, a2, s0, T, TP):
    span = (N_GROUPS - 1) * TP + 2 * SUBLANES

    def blk(i, s):
        view = s3.at[pl.ds(pl.multiple_of(i * SUBLANES, SUBLANES), span)]
        for k in range(SUBLANES):
            rows = pl.ds(SUBLANES + k, N_GROUPS, stride=TP)
            s = a1 * s + a2 * _swap(s) + view[rows, :]
            view[rows, :] = s
        return s

    return lax.fori_loop(0, T // SUBLANES, blk, s0)


def _s5_fwd(hn, wb, wc, a1, a2, dvec, *, name):
    L, D = hn.shape
    T = min(S5_CHUNK, L)
    TP = T + SUBLANES
    nC = L // T

    def body(u_ref, wb_ref, wc_ref, a1_ref, a2_ref, d_ref, y_ref, yg_ref, sb_ref, s3, st):
        @pl.when(pl.program_id(0) == 0)
        def _():
            st[...] = jnp.zeros_like(st)

        sb_ref[0] = st[...]
        _s5_project_in(u_ref, wb_ref, s3, T, TP)
        st[...] = _s5_scan_fwd(s3, a1_ref[...], a2_ref[...], st[...], T, TP)
        for jb in range(D // LANES):
            acc = jnp.zeros((T, LANES), F32)
            for j in _tiles_of(jb):
                acc += jnp.dot(s3[pl.ds(j * TP + SUBLANES, T), :].astype(BF16), wc_ref[j],
                               preferred_element_type=F32)
            cols = slice(jb * LANES, (jb + 1) * LANES)
            y = acc + d_ref[:, cols] * u_ref[:, cols].astype(F32)
            y_ref[:, cols] = y
            yg_ref[:, cols] = _gelu(y).astype(BF16)

    row = pl.BlockSpec((T, D), lambda c: (c, 0))
    wspec = pl.BlockSpec((N_GROUPS, LANES, LANES), lambda c: (0, 0, 0))
    aspec = pl.BlockSpec((N_GROUPS, LANES), lambda c: (0, 0))
    return pl.pallas_call(
        body, grid=(nC,),
        in_specs=[row, wspec, wspec, aspec, aspec, pl.BlockSpec((1, D), lambda c: (0, 0))],
        out_specs=[row, row, pl.BlockSpec((1, N_GROUPS, LANES), lambda c: (c, 0, 0))],
        out_shape=[jax.ShapeDtypeStruct((L, D), F32), jax.ShapeDtypeStruct((L, D), BF16),
                   jax.ShapeDtypeStruct((nC, N_GROUPS, LANES), F32)],
        scratch_shapes=[pltpu.VMEM((N_GROUPS * TP, LANES), F32), pltpu.VMEM((N_GROUPS, LANES), F32)],
        compiler_params=_params("arbitrary"), name=name)(hn, wb, wc, a1, a2, dvec)


def _s5_bwd(hn, dyg, ypre, sbound, wb, wc, a1, a2, dvec, *, name):
    L, D = hn.shape
    T = min(S5_CHUNK, L)
    TP = T + SUBLANES
    nC = L // T
    span = (N_GROUPS - 1) * TP + 2 * SUBLANES
    NT = (((1,), (1,)), ((), ()))
    TN = (((0,), (0,)), ((), ()))

    def body(u_ref, dyg_ref, yp_ref, sb_ref, wb_ref, wc_ref, a1_ref, a2_ref, d_ref,
             du_ref, dwb_ref, dwc_ref, da1_ref, da2_ref, dd_ref, s3, g3, gst, dy_s):
        @pl.when(pl.program_id(0) == 0)
        def _():
            gst[...] = jnp.zeros_like(gst)
            dwb_ref[...] = jnp.zeros_like(dwb_ref)
            dwc_ref[...] = jnp.zeros_like(dwc_ref)
            da1_ref[...] = jnp.zeros_like(da1_ref)
            da2_ref[...] = jnp.zeros_like(da2_ref)
            dd_ref[...] = jnp.zeros_like(dd_ref)

        a1 = a1_ref[...]
        a2 = a2_ref[...]
        dy = dyg_ref[...].astype(F32) * _gelu_grad(yp_ref[...])
        dy_s[...] = dy.astype(BF16)
        dd_ref[...] += jnp.sum(dy * u_ref[...].astype(F32), axis=0, keepdims=True)
        du_ref[...] = d_ref[...] * dy

        s3[pl.ds(SUBLANES - 1, N_GROUPS, stride=TP), :] = sb_ref[0]
        _s5_project_in(u_ref, wb_ref, s3, T, TP)
        _s5_scan_fwd(s3, a1, a2, sb_ref[0], T, TP)

        for j in range(N_TILES):
            blk = _chan_block(j)
            g3[pl.ds(j * TP + SUBLANES, T), :] = lax.dot_general(dy_s[:, blk:blk + LANES], wc_ref[j], NT,
                                                                 preferred_element_type=F32)
        a2c = -a2

        def rblk(ii, carry):
            g, acc1, acc2 = carry
            t0 = pl.multiple_of((T // SUBLANES - 1 - ii) * SUBLANES, SUBLANES)
            gv = g3.at[pl.ds(t0, span)]
            sv = s3.at[pl.ds(t0, span)]
            for k in reversed(range(SUBLANES)):
                rows = pl.ds(SUBLANES + k, N_GROUPS, stride=TP)
                g = a1 * g + a2c * _swap(g) + gv[rows, :]
                gv[rows, :] = g
                sp = sv[pl.ds(SUBLANES - 1 + k, N_GROUPS, stride=TP), :]
                acc1 = acc1 + g * sp
                acc2 = acc2 + g * _swap(sp)
            return g, acc1, acc2

        zero = jnp.zeros((N_GROUPS, LANES), F32)
        g, acc1, acc2 = lax.fori_loop(0, T // SUBLANES, rblk, (gst[...], zero, zero))
        gst[...] = g
        da1_ref[...] += acc1
        da2_ref[...] += acc2

        for jb in range(D // LANES):
            cols = slice(jb * LANES, (jb + 1) * LANES)
            acc = jnp.zeros((T, LANES), F32)
            for j in _tiles_of(jb):
                rows = pl.ds(j * TP + SUBLANES, T)
                gj = g3[rows, :].astype(BF16)
                dwc_ref[j] += lax.dot_general(s3[rows, :].astype(BF16), dy_s[:, cols], TN,
                                              preferred_element_type=F32)
                dwb_ref[j] += lax.dot_general(u_ref[:, cols], gj, TN, preferred_element_type=F32)
                acc += lax.dot_general(gj, wb_ref[j], NT, preferred_element_type=F32)
            du_ref[:, cols] += acc

    rrow = pl.BlockSpec((T, D), lambda c: (nC - 1 - c, 0))
    wspec = pl.BlockSpec((N_GROUPS, LANES, LANES), lambda c: (0, 0, 0))
    aspec = pl.BlockSpec((N_GROUPS, LANES), lambda c: (0, 0))
    vec = pl.BlockSpec((1, D), lambda c: (0, 0))
    return pl.pallas_call(
        body, grid=(nC,),
        in_specs=[rrow, rrow, rrow, pl.BlockSpec((1, N_GROUPS, LANES), lambda c: (nC - 1 - c, 0, 0)),
                  wspec, wspec, aspec, aspec, vec],
        out_specs=[rrow, wspec, wspec, aspec, aspec, vec],
        out_shape=[jax.ShapeDtypeStruct((L, D), F32),
                   jax.ShapeDtypeStruct((N_GROUPS, LANES, LANES), F32),
                   jax.ShapeDtypeStruct((N_GROUPS, LANES, LANES), F32),
                   jax.ShapeDtypeStruct((N_GROUPS, LANES), F32), jax.ShapeDtypeStruct((N_GROUPS, LANES), F32),
                   jax.ShapeDtypeStruct((1, D), F32)],
        scratch_shapes=[pltpu.VMEM((N_GROUPS * TP, LANES), F32), pltpu.VMEM((N_GROUPS * TP, LANES), F32),
                        pltpu.VMEM((N_GROUPS, LANES), F32), pltpu.VMEM((T, D), BF16)],
        compiler_params=_params("arbitrary"), name=name)(hn, dyg, ypre, sbound, wb, wc, a1, a2, dvec)


def _s5_prep(lam_re, lam_im, log_dt, b_re, b_im, c_re, c_im):
    dt = jnp.exp(log_dt)[:, None]
    mag = jnp.exp(lam_re * dt)
    lb_re = mag * jnp.cos(lam_im * dt)
    lb_im = mag * jnp.sin(lam_im * dt)
    den = lam_re * lam_re + lam_im * lam_im
    nr = lb_re - 1.0
    fr = ((nr * lam_re + lb_im * lam_im) / den)[..., None]
    fi = ((lb_im * lam_re - nr * lam_im) / den)[..., None]
    bb_re = fr * b_re - fi * b_im
    bb_im = fr * b_im + fi * b_re
    pair = lambda a: a.reshape(HALF, 2 * SSM_STATE)
    a1 = jnp.concatenate([pair(lb_re), pair(lb_re)], axis=0)
    a2 = jnp.concatenate([-pair(lb_im), pair(lb_im)], axis=0)
    sel = jax.nn.one_hot(jnp.arange(HALF) % 4, 4, dtype=F32)
    eye = jnp.eye(2, dtype=F32)

    def w_in(bb):
        return jnp.einsum('jk,ef,jfph->jkehfp', sel, eye, bb.reshape(HALF, 2, SSM_STATE, SSM_GROUP)
                          ).reshape(HALF, LANES, LANES)

    def w_out(c):
        return jnp.einsum('jk,ef,jfhp->jepkfh', sel, eye, c.reshape(HALF, 2, SSM_GROUP, SSM_STATE)
                          ).reshape(HALF, LANES, LANES)

    wb = jnp.concatenate([w_in(bb_re), w_in(bb_im)], axis=0)
    wc = jnp.concatenate([w_out(c_re), w_out(-c_im)], axis=0)
    return a1, a2, wb, wc


def _tri(n, upper):
    r = lax.broadcasted_iota(jnp.int32, (n, n), 0)
    c = lax.broadcasted_iota(jnp.int32, (n, n), 1)
    return ((r <= c) if upper else (r >= c)).astype(F32)


def _fgate_fwd(fl, bf, *, name):
    L, W = fl.shape
    tr = _tile(L, CUM_TILE, SUBLANES)

    def body(f_ref, b_ref, o_ref, carry):
        @pl.when(pl.program_id(0) == 0)
        def _():
            carry[...] = jnp.zeros_like(carry)

        x = f_ref[...] + b_ref[...]
        ls = jnp.minimum(x, 0.0) - jnp.log(1.0 + jnp.exp(-jnp.abs(x)))
        cum = jnp.dot(_tri(tr, False), ls, preferred_element_type=F32, precision=lax.Precision.HIGHEST) + carry[...]
        o_ref[...] = cum
        carry[...] = cum[tr - 1:tr, :]

    return pl.pallas_call(
        body, grid=(L // tr,),
        in_specs=[pl.BlockSpec((tr, W), lambda i: (i, 0)), pl.BlockSpec((1, W), lambda i: (0, 0))],
        out_specs=pl.BlockSpec((tr, W), lambda i: (i, 0)), out_shape=jax.ShapeDtypeStruct((L, W), F32),
        scratch_shapes=[pltpu.VMEM((1, W), F32)], compiler_params=_params("arbitrary"), name=name)(fl, bf)


def _fgate_bwd(fl, bf, dcum, *, name):
    L, W = fl.shape
    tr = _tile(L, CUM_TILE, SUBLANES)
    n = L // tr

    def body(f_ref, b_ref, d_ref, o_ref, db_ref, carry):
        @pl.when(pl.program_id(0) == 0)
        def _():
            carry[...] = jnp.zeros_like(carry)
            db_ref[...] = jnp.zeros_like(db_ref)

        d = d_ref[...]
        rev = jnp.dot(_tri(tr, True), d, preferred_element_type=F32, precision=lax.Precision.HIGHEST) + carry[...]
        carry[...] += jnp.sum(d, axis=0, keepdims=True)
        df = rev * jax.nn.sigmoid(-(f_ref[...] + b_ref[...]))
        o_ref[...] = df
        db_ref[...] += jnp.sum(df, axis=0, keepdims=True)

    rrow = pl.BlockSpec((tr, W), lambda i: (n - 1 - i, 0))
    vec = pl.BlockSpec((1, W), lambda i: (0, 0))
    return pl.pallas_call(
        body, grid=(n,), in_specs=[rrow, vec, rrow], out_specs=[rrow, vec],
        out_shape=[jax.ShapeDtypeStruct((L, W), F32), jax.ShapeDtypeStruct((1, W), F32)],
        scratch_shapes=[pltpu.VMEM((1, W), F32)], compiler_params=_params("arbitrary"), name=name)(fl, bf, dcum)


_NT = (((1,), (1,)), ((), ()))
_TN = (((0,), (0,)), ((), ()))
HEAD_PAIRS = N_HEADS // 2


def _attn_logits(qs, k, ck, masked, t):
    s = lax.dot_general(qs, k, _NT, preferred_element_type=F32) - ck
    if masked:
        r = lax.broadcasted_iota(jnp.int32, (t, t), 0)
        c = lax.broadcasted_iota(jnp.int32, (t, t), 1)
        s = jnp.where(c > r, NEG, s)
    return s


def _attn_fwd(q, kv, ck, *, name):
    L, D = q.shape
    t = _tile(L, ATTN_TILE)
    n = L // t
    dh = HEAD_DIM

    def body(q_ref, k_ref, v_ref, ck_ref, o_ref, o32_ref, lse_ref, m_s, l_s, acc):
        i, j = pl.program_id(1), pl.program_id(2)

        @pl.when(j == 0)
        def _():
            m_s[...] = jnp.full_like(m_s, NEG)
            l_s[...] = jnp.zeros_like(l_s)
            acc[...] = jnp.zeros_like(acc)

        def tile(masked):
            for e in range(2):
                sl = slice(e * dh, (e + 1) * dh)
                v = v_ref[:, sl]
                s = _attn_logits(q_ref[:, sl] * ATTN_SCALE, k_ref[:, sl], ck_ref[e], masked, t)
                m_new = jnp.maximum(m_s[e], jnp.max(s, axis=1, keepdims=True))
                alpha = jnp.exp(m_s[e] - m_new)
                p = jnp.exp(s - m_new)
                l_s[e] = alpha * l_s[e] + jnp.sum(p, axis=1, keepdims=True)
                p_hi = p.astype(BF16)
                p_lo = (p - p_hi.astype(F32)).astype(BF16)
                pv = (jnp.dot(p_hi, v, preferred_element_type=F32) + jnp.dot(p_lo, v, preferred_element_type=F32))
                acc[e] = alpha * acc[e] + pv
                m_s[e] = m_new

        pl.when(j < i)(functools.partial(tile, False))
        pl.when(j == i)(functools.partial(tile, True))

        @pl.when(j == n - 1)
        def _():
            for e in range(2):
                sl = slice(e * dh, (e + 1) * dh)
                o = acc[e] / l_s[e]
                o_ref[:, sl] = o.astype(BF16)
                o32_ref[:, sl] = o
                lse_ref[e] = m_s[e] + jnp.log(l_s[e])

    qs = pl.BlockSpec((t, LANES), lambda h, i, j: (i, h))
    ks = pl.BlockSpec((t, LANES), lambda h, i, j: (jnp.minimum(i, j), h))
    vs = pl.BlockSpec((t, LANES), lambda h, i, j: (jnp.minimum(i, j), HEAD_PAIRS + h))
    cs = pl.BlockSpec((2, 1, t), lambda h, i, j: (h, 0, jnp.minimum(i, j)))
    return pl.pallas_call(
        body, grid=(HEAD_PAIRS, n, n), in_specs=[qs, ks, vs, cs],
        out_specs=[qs, qs, pl.BlockSpec((2, t, 1), lambda h, i, j: (h, i, 0))],
        out_shape=[jax.ShapeDtypeStruct((L, D), BF16), jax.ShapeDtypeStruct((L, D), F32),
                   jax.ShapeDtypeStruct((N_HEADS, L, 1), F32)],
        scratch_shapes=[pltpu.VMEM((2, t, 1), F32), pltpu.VMEM((2, t, 1), F32), pltpu.VMEM((2, t, dh), F32)],
        compiler_params=_params("parallel", "parallel", "arbitrary"), name=name)(q, kv, kv, ck)


def _attn_bwd(q, kv, ck, o, do, lse, *, name):
    L, D = q.shape
    t = _tile(L, ATTN_TILE)
    n = L // t
    dh = HEAD_DIM

    def body(q_ref, k_ref, v_ref, ck_ref, o_ref, do_ref, lse_ref, dq_ref, dk_ref, dv_ref, dck_ref):
        j, i = pl.program_id(1), pl.program_id(2)

        @pl.when(jnp.logical_and(i == 0, j == 0))
        def _():
            dq_ref[...] = jnp.zeros_like(dq_ref)

        @pl.when(i == 0)
        def _():
            dk_ref[...] = jnp.zeros_like(dk_ref)
            dv_ref[...] = jnp.zeros_like(dv_ref)
            dck_ref[...] = jnp.zeros_like(dck_ref)

        def tile(masked):
            rows = pl.ds(pl.multiple_of(i * t, t), t)
            for e in range(2):
                sl = slice(e * dh, (e + 1) * dh)
                qs = q_ref[:, sl] * ATTN_SCALE
                k = k_ref[:, sl]
                do = do_ref[:, sl]
                s = _attn_logits(qs, k, ck_ref[e], masked, t)
                p = jnp.exp(s - lse_ref[e])
                dp = lax.dot_general(do, v_ref[:, sl], _NT, preferred_element_type=F32)
                delta = jnp.sum(do.astype(F32) * o_ref[:, sl], axis=1, keepdims=True)
                ds = p * (dp - delta)
                ds16 = ds.astype(BF16)
                dv_ref[:, sl] += lax.dot_general(p.astype(BF16), do, _TN, preferred_element_type=F32)
                dk_ref[:, sl] += lax.dot_general(ds16, qs, _TN, preferred_element_type=F32)
                dck_ref[e] -= jnp.sum(ds, axis=0, keepdims=True)
                dq_ref[rows, sl] += jnp.dot(ds16, k, preferred_element_type=F32) * ATTN_SCALE

        pl.when(i > j)(functools.partial(tile, False))
        pl.when(i == j)(functools.partial(tile, True))

    qs = pl.BlockSpec((t, LANES), lambda h, j, i: (jnp.maximum(i, j), h))
    ks = pl.BlockSpec((t, LANES), lambda h, j, i: (j, h))
    vs = pl.BlockSpec((t, LANES), lambda h, j, i: (j, HEAD_PAIRS + h))
    cs = pl.BlockSpec((2, 1, t), lambda h, j, i: (h, 0, j))
    ls = pl.BlockSpec((2, t, 1), lambda h, j, i: (h, jnp.maximum(i, j), 0))
    full = jax.ShapeDtypeStruct((L, D), F32)
    return pl.pallas_call(
        body, grid=(HEAD_PAIRS, n, n), in_specs=[qs, ks, vs, cs, qs, qs, ls],
        out_specs=[pl.BlockSpec((L, LANES), lambda h, j, i: (0, h)), ks, ks, cs],
        out_shape=[full, full, full, jax.ShapeDtypeStruct((N_HEADS, 1, L), F32)],
        compiler_params=_params("parallel", "arbitrary", "arbitrary"), name=name)(q, kv, kv, ck, o, do, lse)


def local_step(x, target, w):
    L = x.shape[0]
    grads = {}
    vec = lambda a: a.reshape(1, -1)

    h = x
    saved = []
    kvs = None
    for layer in range(DEPTH):
        if layer < N_A:
            prep_args = (w["lam_re"][layer], w["lam_im"][layer], w["log_dt"][layer], w["ssm_b_re"][layer],
                         w["ssm_b_im"][layer], w["ssm_c_re"][layer], w["ssm_c_im"][layer])
            (a1, a2, wb, wc), prep_vjp = jax.vjp(_s5_prep, *prep_args)
            wb16, wc16 = wb.astype(BF16), wc.astype(BF16)
            dvec = vec(w["ssm_d"][layer])
            hn = _rms_fwd(h, vec(w["g_mix"][layer]), name=f"mix_norm{layer}")
            ypre, yg, sb = _s5_fwd(hn, wb16, wc16, a1, a2, dvec, name=f"s5_fwd{layer}")
            z = _mm(yg, w["w_glu"][layer], name=f"glu_mm{layer}")
            h1, hn2 = _glu_res_rms(z, h, vec(w["g_ffn"][layer]), name=f"glu_res{layer}")
            mix_saved = (h, hn, ypre, yg, sb, z, a1, a2, wb16, wc16, dvec, prep_vjp)
        else:
            j = layer - N_A
            if layer == N_A:
                hkv = _rms_fwd(h, vec(w["g_kv"]), name="kv_norm")
                kvm = _mm(hkv, w["w_kv"], out_dtype=BF16, name="kv_mm")
                fl = _mm(hkv, w["w_f"], tn=LANES, name="f_mm")
                cum = _fgate_fwd(fl, w["b_f_pad"], name="fgate_fwd")
                ck = cum[:, :N_HEADS].T.reshape(N_HEADS, 1, L)
                kvs = (h, hkv, fl, kvm, ck)
            _, _, _, kvm, ck = kvs
            hn = _rms_fwd(h, vec(w["g_mix"][layer]), name=f"mix_norm{layer}")
            q = _mm(hn, w["w_q"][j], out_dtype=BF16, name=f"q_mm{layer}")
            o, o32, lse = _attn_fwd(q, kvm, ck, name=f"attn_fwd{layer}")
            h1 = _mm(o, w["w_o"][j], add=h, name=f"o_mm{layer}")
            mix_saved = (h, hn, q, o32, o, lse)
            hn2 = None
        if hn2 is None:
            hn2 = _rms_fwd(h1, vec(w["g_ffn"][layer]), name=f"ffn_norm{layer}")
        u0 = _mm(hn2, w["w_ffn_in"][layer], out_split=True, tn=1408, name=f"ffn_in{layer}")
        a = _conv_act(u0, w["conv_w"][layer], w["conv_b"][layer], name=f"ffn_act{layer}")
        h2 = _mm(a, w["w_ffn_out"][layer], add=h1, tn=512, tk=2816, name=f"ffn_out{layer}")
        saved.append((mix_saved, h1, hn2, u0))
        h = h2

    loss, dh, dg_final = _loss_head(h, vec(w["g_final"]), target, name="loss_head")
    grads["g_final"] = dg_final.reshape(-1)

    gl = {k: [None] * DEPTH for k in ("g_mix", "g_ffn", "w_ffn_in", "conv_w", "conv_b", "w_ffn_out")}
    ga = {k: [None] * N_A for k in ("lam_re", "lam_im", "log_dt", "ssm_b_re", "ssm_b_im", "ssm_c_re", "ssm_c_im",
                                    "ssm_d", "w_glu")}
    gb = {k: [None] * (DEPTH - N_A) for k in ("w_q", "w_o")}
    dkh = dvh = dck = None
    for layer in reversed(range(DEPTH)):
        mix_saved, h1, hn2, u0 = saved[layer]
        cw, cb = w["conv_w"][layer], w["conv_b"][layer]
        da = _mm(dh, w["w_ffn_out"][layer], tb=True, tn=1408, name=f"ffn_da{layer}")
        a, du, dcb = _conv_act_bwd(u0, cw, cb, da, name=f"ffn_act_bwd{layer}")
        gl["w_ffn_out"][layer] = _mm(a, dh, ta=True, tm=1408, name=f"ffn_dwout{layer}")
        du0, dcw = _conv_bwd(u0, cw, du, name=f"ffn_conv_bwd{layer}")
        gl["w_ffn_in"][layer] = _mm(hn2, du0, ta=True, b_split=True, tn=1408, name=f"ffn_dwin{layer}")
        dhn2 = _mm(du0, w["w_ffn_in"][layer], a_split=True, tb=True, tk=1408, name=f"ffn_dhn{layer}")
        dh1, dg = _rms_bwd(h1, vec(w["g_ffn"][layer]), dhn2, dh, name=f"ffn_norm_bwd{layer}")
        gl["g_ffn"][layer], gl["conv_w"][layer], gl["conv_b"][layer] = dg.reshape(-1), dcw, dcb
        if layer < N_A:
            hin, hn, ypre, yg, sb, z, a1, a2, wb16, wc16, dvec, prep_vjp = mix_saved
            dz = _glu_bwd(z, dh1, name=f"glu_bwd{layer}")
            ga["w_glu"][layer] = _mm(yg, dz, ta=True, name=f"glu_dw{layer}")
            dyg = _mm(dz, w["w_glu"][layer], tb=True, out_dtype=BF16, name=f"glu_dy{layer}")
            du, dwb, dwc, da1, da2, dd = _s5_bwd(hn, dyg, ypre, sb, wb16, wc16, a1, a2, dvec, name=f"s5_bwd{layer}")
            dparams = prep_vjp((da1, da2, dwb, dwc))
            for nme, val in zip(("lam_re", "lam_im", "log_dt", "ssm_b_re", "ssm_b_im", "ssm_c_re", "ssm_c_im"), dparams):
                ga[nme][layer] = val
            ga["ssm_d"][layer] = dd.reshape(-1)
            dh, dg = _rms_bwd(hin, vec(w["g_mix"][layer]), du, dh1, name=f"mix_norm_bwd{layer}")
        else:
            j = layer - N_A
            hin, hn, q, o32, o, lse = mix_saved
            _, _, _, kvm, ck = kvs
            gb["w_o"][j] = _mm(o, dh1, ta=True, name=f"o_dw{layer}")
            do = _mm(dh1, w["w_o"][j], tb=True, out_dtype=BF16, name=f"o_dx{layer}")
            dq, dk_l, dv_l, dck_l = _attn_bwd(q, kvm, ck, o32, do, lse, name=f"attn_bwd{layer}")
            dkh = dk_l if dkh is None else dkh + dk_l
            dvh = dv_l if dvh is None else dvh + dv_l
            dck = dck_l if dck is None else dck + dck_l
            gb["w_q"][j] = _mm(hn, dq, ta=True, name=f"q_dw{layer}")
            dhn = _mm(dq, w["w_q"][j], tb=True, name=f"q_dx{layer}")
            dh, dg = _rms_bwd(hin, vec(w["g_mix"][layer]), dhn, dh1, name=f"mix_norm_bwd{layer}")
            if layer == N_A:
                hkv_in, hkv, fl, _, _ = kvs
                dcum = jnp.pad(dck.reshape(N_HEADS, L).T, ((0, 0), (0, LANES - N_HEADS)))
                dfl, dbf = _fgate_bwd(fl, w["b_f_pad"], dcum, name="fgate_bwd")
                dkv = jnp.concatenate([dkh, dvh], axis=1).astype(BF16)
                dfl16 = dfl.astype(BF16)
                grads["w_kv"] = _mm(hkv, dkv, ta=True, name="kv_dw")
                grads["w_f"] = _mm(hkv, dfl16, ta=True, tn=LANES, name="f_dw")
                dhkv = _mm(dkv, w["w_kv"], tb=True, name="kv_dx")
                dhkv = _mm(dfl16, w["w_f"], tb=True, add=dhkv, tk=LANES, name="f_dx")
                grads["b_f"] = dbf[0, :N_HEADS]
                dh, dgkv = _rms_bwd(hkv_in, vec(w["g_kv"]), dhkv, dh, name="kv_norm_bwd")
                grads["g_kv"] = dgkv.reshape(-1)
        gl["g_mix"][layer] = dg.reshape(-1)

    for d in (gl, ga, gb):
        for k, v in d.items():
            grads[k] = jnp.stack(v)
    return loss, dh, grads


def _exchange(x, *, scatter, name):
    R, C = x.shape[-2:]

    def body(x_ref, o_ref, send_sems, recv_sems, local_sem):
        xi, yi, ci = lax.axis_index("x"), lax.axis_index("y"), lax.axis_index("c")
        me = 4 * xi + 2 * yi + ci

        def src(p):
            return x_ref.at[p] if scatter else x_ref

        own = pltpu.make_async_copy(src(me), o_ref.at[me], local_sem)
        own.start()
        sends, recvs = [], []
        for k in range(1, N_DEV):
            px, py, pc = xi ^ (k >> 2), yi ^ ((k >> 1) & 1), ci ^ (k & 1)
            p = 4 * px + 2 * py + pc
            sends.append(pltpu.make_async_remote_copy(
                src_ref=src(p), dst_ref=o_ref.at[me], send_sem=send_sems.at[k - 1], recv_sem=recv_sems.at[k - 1],
                device_id=(px, py, pc), device_id_type=pl.DeviceIdType.MESH))
            recvs.append(pltpu.make_async_remote_copy(
                src_ref=src(p), dst_ref=o_ref.at[p], send_sem=send_sems.at[k - 1], recv_sem=recv_sems.at[k - 1],
                device_id=(px, py, pc), device_id_type=pl.DeviceIdType.MESH))
        for cp in sends:
            cp.start()
        for cp in recvs:
            cp.wait_recv()
        for cp in sends:
            cp.wait_send()
        own.wait()

    return pl.pallas_call(
        body, out_shape=jax.ShapeDtypeStruct((N_DEV, R, C), x.dtype),
        in_specs=[pl.BlockSpec(memory_space=pl.ANY)], out_specs=pl.BlockSpec(memory_space=pl.ANY),
        scratch_shapes=[pltpu.SemaphoreType.DMA((N_DEV - 1,)), pltpu.SemaphoreType.DMA((N_DEV - 1,)),
                        pltpu.SemaphoreType.DMA],
        compiler_params=pltpu.CompilerParams(has_side_effects=True), name=name)(x)


def _reduce_adamw(slots, w, m, v, *, name):
    _, R, C = slots.shape
    tr = _tile(R, 256, 16)
    c1 = 1.0 / (1.0 - ADAM_B1 ** ADAM_STEP)
    c2 = 1.0 / (1.0 - ADAM_B2 ** ADAM_STEP)

    def body(s_ref, w_ref, m_ref, v_ref, g_ref, d_ref, nm_ref, nv_ref):
        g = s_ref[0].astype(F32)
        for d in range(1, N_DEV):
            g = g + s_ref[d].astype(F32)
        m2 = ADAM_B1 * m_ref[...] + (1.0 - ADAM_B1) * g
        v2 = ADAM_B2 * v_ref[...] + (1.0 - ADAM_B2) * (g * g)
        g_ref[...] = g
        nm_ref[...] = m2
        nv_ref[...] = v2
        d_ref[...] = -ADAM_LR * ((m2 * c1) / (jnp.sqrt(v2 * c2) + ADAM_EPS) + ADAM_WD * w_ref[...])

    row = pl.BlockSpec((tr, C), lambda i: (i, 0))
    out = jax.ShapeDtypeStruct((R, C), F32)
    return pl.pallas_call(
        body, grid=(R // tr,), in_specs=[pl.BlockSpec((N_DEV, tr, C), lambda i: (0, i, 0)), row, row, row],
        out_specs=[row, row, row, row], out_shape=[out, out, out, out],
        compiler_params=_params("parallel"), name=name)(slots, w, m, v)


def _rows(a, rows, width=D_MODEL):
    a = a.reshape(-1)
    if a.shape[0] != rows * width:
        a = jnp.pad(a, (0, rows * width - a.shape[0]))
    return a.reshape(rows, width)


_BIG = (
    ("w_glu", 512,
     lambda g: g.reshape(8, 2, 1024, 256).transpose(1, 2, 0, 3).reshape(2, 1024, 2048),
     lambda f: f.reshape(2, 1024, 8, 256).transpose(2, 0, 1, 3).reshape(8, 512, 1024)),
    ("w_kvf", 272,
     lambda g: g[:, :258].reshape(8, 1024, 258).transpose(1, 0, 2).reshape(1024, 2064),
     lambda f: jnp.pad(f.reshape(1024, 8, 258).transpose(1, 0, 2).reshape(8, 258, 1024), ((0, 0), (0, 14), (0, 0)))),
    ("w_q", 256,
     lambda g: g.reshape(8, 2, 128, 1024).transpose(1, 0, 2, 3).reshape(2, 1024, 1024),
     lambda f: f.reshape(2, 8, 128, 1024).transpose(1, 0, 2, 3).reshape(8, 256, 1024)),
    ("w_o", 256,
     lambda g: g.reshape(8, 2, 128, 1024).transpose(1, 0, 2, 3).reshape(2, 1024, 1024),
     lambda f: f.reshape(2, 8, 128, 1024).transpose(1, 0, 2, 3).reshape(8, 256, 1024)),
    ("w_ffn_in", 2816,
     lambda g: g.reshape(8, 4, 1024, 704).transpose(1, 2, 0, 3).reshape(4, 1024, 5632),
     lambda f: f.reshape(4, 1024, 8, 704).transpose(2, 0, 1, 3).reshape(8, 2816, 1024)),
    ("w_ffn_out", 1408,
     lambda g: g.reshape(8, 4, 352, 1024).transpose(1, 0, 2, 3).reshape(4, 2816, 1024),
     lambda f: f.reshape(4, 8, 352, 1024).transpose(1, 0, 2, 3).reshape(8, 1408, 1024)),
)
_SMALL_ROWS = 72
_REPL = ("g_mix", "g_ffn", "lam_re", "lam_im", "log_dt", "ssm_b_re", "ssm_b_im", "ssm_c_re", "ssm_c_im",
         "g_kv", "b_f", "ffn_conv_b", "g_final")
_REPL_ROWS = 576
_ORDER = ("g_mix", "g_ffn", "lam_re", "lam_im", "log_dt", "ssm_b_re", "ssm_b_im", "ssm_c_re", "ssm_c_im", "ssm_d",
          "w_glu", "g_kv", "w_kvf", "b_f", "w_q", "w_o", "w_ffn_in", "ffn_conv_w", "ffn_conv_b", "w_ffn_out", "g_final")


def _pack_big(shards, dtype):
    return jnp.concatenate([_rows(shards[n].astype(dtype), r) for n, r, _, _ in _BIG], axis=0)


def _unpack_big(flat, shapes):
    out, off = {}, 0
    for n, r, _, _ in _BIG:
        size = math.prod(shapes[n])
        out[n] = flat[off:off + r].reshape(-1)[:size].reshape(shapes[n])
        off += r
    return out


def _pack_small(ssm_d, conv_w):
    flat = jnp.concatenate([ssm_d.reshape(-1), conv_w.reshape(-1)])
    return _rows(flat, _SMALL_ROWS, LANES)


def _unpack_small(flat):
    flat = flat.reshape(-1)
    return flat[:256].reshape(2, 128), flat[256:256 + 8448].reshape(4, 3, 704)


def _pack_repl(d):
    return _rows(jnp.concatenate([d[n].reshape(-1) for n in _REPL]), _REPL_ROWS)


def _unpack_repl(flat, shapes):
    flat, out, off = flat.reshape(-1), {}, 0
    for n in _REPL:
        size = math.prod(shapes[n])
        out[n] = flat[off:off + size].reshape(shapes[n])
        off += size
    return out


def kernel(x, g_mix, g_ffn, lam_re, lam_im, log_dt, ssm_b_re, ssm_b_im, ssm_c_re, ssm_c_im, ssm_d, w_glu, g_kv, w_kvf, b_f, w_q, w_o, w_ffn_in, ffn_conv_w, ffn_conv_b, w_ffn_out, g_final, loss_target, m_g_mix, m_g_ffn, m_lam_re, m_lam_im, m_log_dt, m_ssm_b_re, m_ssm_b_im, m_ssm_c_re, m_ssm_c_im, m_ssm_d, m_w_glu, m_g_kv, m_w_kvf, m_b_f, m_w_q, m_w_o, m_w_ffn_in, m_ffn_conv_w, m_ffn_conv_b, m_w_ffn_out, m_g_final, v_g_mix, v_g_ffn, v_lam_re, v_lam_im, v_log_dt, v_ssm_b_re, v_ssm_b_im, v_ssm_c_re, v_ssm_c_im, v_ssm_d, v_w_glu, v_g_kv, v_w_kvf, v_b_f, v_w_q, v_w_o, v_w_ffn_in, v_ffn_conv_w, v_ffn_conv_b, v_w_ffn_out, v_g_final):
    wts = dict(g_mix=g_mix, g_ffn=g_ffn, lam_re=lam_re, lam_im=lam_im, log_dt=log_dt, ssm_b_re=ssm_b_re,
               ssm_b_im=ssm_b_im, ssm_c_re=ssm_c_re, ssm_c_im=ssm_c_im, ssm_d=ssm_d, w_glu=w_glu, g_kv=g_kv,
               w_kvf=w_kvf, b_f=b_f, w_q=w_q, w_o=w_o, w_ffn_in=w_ffn_in, ffn_conv_w=ffn_conv_w,
               ffn_conv_b=ffn_conv_b, w_ffn_out=w_ffn_out, g_final=g_final)
    mom = dict(g_mix=m_g_mix, g_ffn=m_g_ffn, lam_re=m_lam_re, lam_im=m_lam_im, log_dt=m_log_dt, ssm_b_re=m_ssm_b_re,
               ssm_b_im=m_ssm_b_im, ssm_c_re=m_ssm_c_re, ssm_c_im=m_ssm_c_im, ssm_d=m_ssm_d, w_glu=m_w_glu,
               g_kv=m_g_kv, w_kvf=m_w_kvf, b_f=m_b_f, w_q=m_w_q, w_o=m_w_o, w_ffn_in=m_w_ffn_in,
               ffn_conv_w=m_ffn_conv_w, ffn_conv_b=m_ffn_conv_b, w_ffn_out=m_w_ffn_out, g_final=m_g_final)
    var = dict(g_mix=v_g_mix, g_ffn=v_g_ffn, lam_re=v_lam_re, lam_im=v_lam_im, log_dt=v_log_dt, ssm_b_re=v_ssm_b_re,
               ssm_b_im=v_ssm_b_im, ssm_c_re=v_ssm_c_re, ssm_c_im=v_ssm_c_im, ssm_d=v_ssm_d, w_glu=v_w_glu,
               g_kv=v_g_kv, w_kvf=v_w_kvf, b_f=v_b_f, w_q=v_w_q, w_o=v_w_o, w_ffn_in=v_w_ffn_in,
               ffn_conv_w=v_ffn_conv_w, ffn_conv_b=v_ffn_conv_b, w_ffn_out=v_w_ffn_out, g_final=v_g_final)
    shapes = {n: a.shape for n, a in wts.items()}
    F = D_FF

    big = _exchange(_pack_big(wts, BF16), scatter=False, name="gather_big")
    small = _exchange(_pack_small(ssm_d, ffn_conv_w), scatter=False, name="gather_small")
    full, off = {}, 0
    for n, r, to_full, _ in _BIG:
        full[n] = to_full(big[:, off:off + r])
        off += r
    small = small.reshape(N_DEV, -1)
    ssm_d_full = small[:, :256].reshape(8, 2, 128).transpose(1, 0, 2).reshape(2, D_MODEL)
    conv_w_full = small[:, 256:256 + 8448].reshape(8, 4, 3, 704).transpose(1, 2, 0, 3).reshape(4, 3, 2, F)

    w = dict(wts)
    w.update(w_glu=full["w_glu"], w_q=full["w_q"], w_o=full["w_o"], w_ffn_in=full["w_ffn_in"],
             w_ffn_out=full["w_ffn_out"], ssm_d=ssm_d_full)
    w["w_kv"] = full["w_kvf"][:, :2 * D_MODEL]
    w["w_f"] = jnp.pad(full["w_kvf"][:, 2 * D_MODEL:], ((0, 0), (0, LANES - N_HEADS)))
    w["b_f_pad"] = jnp.pad(b_f, (0, LANES - N_HEADS)).reshape(1, LANES)
    w["conv_w"] = conv_w_full.transpose(0, 2, 1, 3)
    w["conv_b"] = ffn_conv_b.reshape(DEPTH, 2, 1, F)

    loss, dx, g = local_step(x[0], loss_target[0], w)
    loss = lax.psum(loss[0, 0], MESH_AXES)

    g["w_kvf"] = jnp.concatenate([g["w_kv"], g["w_f"][:, :N_HEADS]], axis=1)
    gbig = jnp.concatenate([to_blocks(g[n]).astype(BF16) for n, _, _, to_blocks in _BIG], axis=1)
    gbig = _exchange(gbig, scatter=True, name="scatter_big")
    g_d = g["ssm_d"].reshape(2, 8, 128).transpose(1, 0, 2).reshape(8, 256)
    g_cw = g["conv_w"].transpose(0, 2, 1, 3).reshape(4, 3, 8, 704).transpose(2, 0, 1, 3).reshape(8, 8448)
    gsmall = jnp.concatenate([g_d, g_cw, jnp.zeros((8, _SMALL_ROWS * LANES - 8704), F32)], axis=1)
    gsmall = _exchange(gsmall.reshape(8, _SMALL_ROWS, LANES), scatter=True, name="scatter_small")
    g["ffn_conv_b"] = g["conv_b"].reshape(DEPTH, 2 * F)
    grepl = _exchange(_pack_repl(g), scatter=False, name="gather_repl")

    res = {}
    outs = _reduce_adamw(gbig, _pack_big(wts, F32), _pack_big(mom, F32), _pack_big(var, F32), name="adamw_big")
    for kind, flat in zip(("grad", "delta", "m", "v"), outs):
        for n, a in _unpack_big(flat, shapes).items():
            res[kind, n] = a
    outs = _reduce_adamw(gsmall, _pack_small(ssm_d, ffn_conv_w), _pack_small(m_ssm_d, m_ffn_conv_w),
                         _pack_small(v_ssm_d, v_ffn_conv_w), name="adamw_small")
    for kind, flat in zip(("grad", "delta", "m", "v"), outs):
        res[kind, "ssm_d"], res[kind, "ffn_conv_w"] = _unpack_small(flat)
    outs = _reduce_adamw(grepl, _pack_repl(wts), _pack_repl(mom), _pack_repl(var), name="adamw_repl")
    for kind, flat in zip(("grad", "delta", "m", "v"), outs):
        for n, a in _unpack_repl(flat, shapes).items():
            res[kind, n] = a

    return (loss, dx[None], *[res[kind, n] for kind in ("grad", "delta", "m", "v") for n in _ORDER])
```

```python
import functools
import math

import jax
import jax.numpy as jnp
from jax import lax
from jax.experimental import pallas as pl
from jax.experimental.pallas import tpu as pltpu

F32 = jnp.float32
BF16 = jnp.bfloat16

D_MODEL = 1024
DEPTH = 4
N_A = 2
N_GROUPS = 64
SSM_GROUP = 16
SSM_STATE = 64
N_HEADS = 16
HEAD_DIM = 64
ATTN_SCALE = HEAD_DIM ** -0.5
D_FF = 2816
EPS = 1e-6
N_DEV = 8
LANES = 128
SUBLANES = 8

ADAM_LR = 0.001
ADAM_B1 = 0.9
ADAM_B2 = 0.999
ADAM_EPS = 1e-08
ADAM_WD = 0.01
ADAM_STEP = 10

ROW_TILE = 512
S5_CHUNK = 256
ATTN_TILE = 512
CUM_TILE = 256
NEG = -1e30

MESH_AXES = ("x", "y", "c")


def _tile(n, target, align=LANES):
    t = (min(target, n) // align) * align
    while t >= align:
        if n % t == 0:
            return t
        t -= align
    return n


def _params(*sem):
    return pltpu.CompilerParams(dimension_semantics=sem, vmem_limit_bytes=56 * 1024 * 1024)


_ANY = pl.BlockSpec(memory_space=pl.ANY)
_XFER_SEMS = (pltpu.SemaphoreType.DMA((N_DEV - 1,)), pltpu.SemaphoreType.DMA((N_DEV - 1,)), pltpu.SemaphoreType.DMA)


def _xfer_copies(x_ref, o_ref, send_sems, recv_sems, local_sem, scatter):
    xi, yi, ci = lax.axis_index("x"), lax.axis_index("y"), lax.axis_index("c")
    me = 4 * xi + 2 * yi + ci

    def src(p):
        return x_ref.at[p] if scatter else x_ref

    own = pltpu.make_async_copy(src(me), o_ref.at[me], local_sem)
    sends, recvs = [], []
    for k in range(1, N_DEV):
        px, py, pc = xi ^ (k >> 2), yi ^ ((k >> 1) & 1), ci ^ (k & 1)
        p = 4 * px + 2 * py + pc
        sends.append(pltpu.make_async_remote_copy(
            src_ref=src(p), dst_ref=o_ref.at[me], send_sem=send_sems.at[k - 1], recv_sem=recv_sems.at[k - 1],
            device_id=(px, py, pc), device_id_type=pl.DeviceIdType.MESH))
        recvs.append(pltpu.make_async_remote_copy(
            src_ref=src(p), dst_ref=o_ref.at[p], send_sem=send_sems.at[k - 1], recv_sem=recv_sems.at[k - 1],
            device_id=(px, py, pc), device_id_type=pl.DeviceIdType.MESH))
    return own, sends, recvs


def _xfer_start(*refs, scatter):
    own, sends, _ = _xfer_copies(*refs, scatter)
    own.start()
    for cp in sends:
        cp.start()


def _xfer_wait(*refs, scatter):
    own, sends, recvs = _xfer_copies(*refs, scatter)
    for cp in recvs:
        cp.wait_recv()
    for cp in sends:
        cp.wait_send()
    own.wait()


def _xfer_out(x, scatter):
    return jax.ShapeDtypeStruct((N_DEV,) + (x.shape[1:] if scatter else x.shape), x.dtype)


def _pcall(body, args, *, grid, in_specs, out_specs, out_shape, scratch_shapes=(), sem, name, xfers=()):
    out_specs, out_shape = list(out_specs), list(out_shape)
    n_in, n_out, n_x, n_scr = len(in_specs), len(out_specs), len(xfers), len(scratch_shapes)
    if not xfers:
        return pl.pallas_call(body, grid=grid, in_specs=in_specs, out_specs=out_specs, out_shape=out_shape,
                              scratch_shapes=list(scratch_shapes), compiler_params=_params(*sem), name=name)(*args)
    flags = [s for _, s in xfers]

    def wrapped(*refs):
        ins, xin = refs[:n_in], refs[n_in:n_in + n_x]
        outs = refs[n_in + n_x:n_in + n_x + n_out]
        xout = refs[n_in + n_x + n_out:n_in + 2 * n_x + n_out]
        scr = refs[n_in + 2 * n_x + n_out:]
        own, sems = scr[:n_scr], scr[n_scr:]
        ids = [pl.program_id(d) for d in range(len(grid))]
        first = functools.reduce(jnp.logical_and, [i == 0 for i in ids])
        last = functools.reduce(jnp.logical_and, [i == g - 1 for i, g in zip(ids, grid)])

        @pl.when(first)
        def _():
            for t in range(n_x):
                _xfer_start(xin[t], xout[t], *sems[3 * t:3 * t + 3], scatter=flags[t])

        body(*ins, *outs, *own)

        @pl.when(last)
        def _():
            for t in range(n_x):
                _xfer_wait(xin[t], xout[t], *sems[3 * t:3 * t + 3], scatter=flags[t])

    return pl.pallas_call(
        wrapped, grid=grid, in_specs=list(in_specs) + [_ANY] * n_x, out_specs=out_specs + [_ANY] * n_x,
        out_shape=out_shape + [_xfer_out(x, s) for x, s in xfers],
        scratch_shapes=list(scratch_shapes) + list(_XFER_SEMS) * n_x,
        compiler_params=_params(*["arbitrary"] * len(grid)), name=name)(*args, *[x for x, _ in xfers])


def _exchange(xfers, *, name):
    def body():
        pass

    return _pcall(body, (), grid=(1,), in_specs=[], out_specs=[], out_shape=[], sem=("arbitrary",), name=name,
                  xfers=xfers)


def _mm(a, b, *, ak="mk", bk="kn", ok="mn", add=None, out_dtype=F32, tm=1024, tn=1024, tk=1024, name, xfers=()):
    sa, sb = a.shape, b.shape
    fm = fn = fk = None
    if ak == "mk":
        M, K, a_c = sa[0], sa[1], 1
    elif ak == "km":
        K, M, a_c = sa[0], sa[1], 0
    elif ak == "bmk":
        M, K, a_c, fk = sa[1], sa[0] * sa[2], 1, sa[2]
    else:
        K, M, a_c, fm = sa[1], sa[0] * sa[2], 0, sa[2]
    if bk == "kn":
        N, b_c = sb[1], 0
    elif bk == "nk":
        N, b_c = sb[0], 1
    elif bk == "bkn":
        N, b_c, fn = sb[0] * sb[2], 0, sb[2]
    elif bk == "bnk":
        N, b_c, fk = sb[1], 1, sb[2]
    elif bk == "kbn":
        N, b_c, fk = sb[2], 0, sb[1]
    else:
        N, b_c, fn = sb[0] * sb[1], 1, sb[1]
    tm, tn, tk = fm or _tile(M, tm), fn or _tile(N, tn), fk or _tile(K, tk)
    nm, nn, nk = M // tm, N // tn, K // tk

    a_spec = {"mk": pl.BlockSpec((tm, tk), lambda i, j, k: (i, k)),
              "km": pl.BlockSpec((tk, tm), lambda i, j, k: (k, i)),
              "bmk": pl.BlockSpec((None, tm, tk), lambda i, j, k: (k, i, 0)),
              "bkm": pl.BlockSpec((None, tk, tm), lambda i, j, k: (i, k, 0))}[ak]
    b_spec = {"kn": pl.BlockSpec((tk, tn), lambda i, j, k: (k, j)),
              "nk": pl.BlockSpec((tn, tk), lambda i, j, k: (j, k)),
              "bkn": pl.BlockSpec((None, tk, tn), lambda i, j, k: (j, k, 0)),
              "bnk": pl.BlockSpec((None, tn, tk), lambda i, j, k: (k, j, 0)),
              "kbn": pl.BlockSpec((None, tk, tn), lambda i, j, k: (k, 0, j)),
              "nbk": pl.BlockSpec((None, tn, tk), lambda i, j, k: (j, 0, k))}[bk]
    if ok == "mn":
        o_spec = pl.BlockSpec((tm, tn), lambda i, j, k: (i, j))
        out_shape = jax.ShapeDtypeStruct((M, N), out_dtype)
    elif ok == "bmn":
        o_spec = pl.BlockSpec((None, tm, tn), lambda i, j, k: (j, i, 0))
        out_shape = jax.ShapeDtypeStruct((nn, M, tn), out_dtype)
    else:
        o_spec = pl.BlockSpec((None, tm, tn), lambda i, j, k: (i, 0, j))
        out_shape = jax.ShapeDtypeStruct((nm, tm, N), out_dtype)
    dims = (((a_c,), (b_c,)), ((), ()))
    has_add = add is not None

    def body(*refs):
        a_ref, b_ref = refs[0], refs[1]
        add_ref = refs[2] if has_add else None
        o_ref = refs[3] if has_add else refs[2]
        part = lax.dot_general(a_ref[...].astype(BF16), b_ref[...].astype(BF16), dims, preferred_element_type=F32)

        def finish(r):
            if has_add:
                r = r + add_ref[...]
            o_ref[...] = r.astype(out_dtype)

        if nk == 1:
            finish(part)
            return
        acc = refs[-1]
        k = pl.program_id(2)

        @pl.when(k == 0)
        def _():
            acc[...] = part

        @pl.when(k > 0)
        def _():
            acc[...] += part

        @pl.when(k == nk - 1)
        def _():
            finish(acc[...])

    in_specs = [a_spec, b_spec]
    args = [a, b]
    if has_add:
        in_specs.append(pl.BlockSpec((tm, tn), lambda i, j, k: (i, j)))
        args.append(add)
    res = _pcall(body, args, grid=(nm, nn, nk), in_specs=in_specs, out_specs=[o_spec], out_shape=[out_shape],
                 scratch_shapes=[pltpu.VMEM((tm, tn), F32)] if nk > 1 else [],
                 sem=("parallel", "parallel", "arbitrary"), name=name, xfers=xfers)
    return res if xfers else res[0]


def _rms_fwd(h, g, *, name):
    L, D = h.shape
    tr = _tile(L, ROW_TILE, SUBLANES)

    def body(h_ref, g_ref, o_ref):
        x = h_ref[...]
        r = lax.rsqrt(jnp.mean(x * x, axis=1, keepdims=True) + EPS)
        o_ref[...] = (x * r * g_ref[...]).astype(BF16)

    return pl.pallas_call(
        body, grid=(L // tr,),
        in_specs=[pl.BlockSpec((tr, D), lambda i: (i, 0)), pl.BlockSpec((1, D), lambda i: (0, 0))],
        out_specs=pl.BlockSpec((tr, D), lambda i: (i, 0)), out_shape=jax.ShapeDtypeStruct((L, D), BF16),
        compiler_params=_params("parallel"), name=name)(h, g)


def _rms_bwd(h, g, dy, dres, *, name):
    L, D = h.shape
    tr = _tile(L, ROW_TILE, SUBLANES)

    def body(h_ref, g_ref, dy_ref, dres_ref, dh_ref, dg_ref):
        @pl.when(pl.program_id(0) == 0)
        def _():
            dg_ref[...] = jnp.zeros_like(dg_ref)

        x = h_ref[...]
        r = lax.rsqrt(jnp.mean(x * x, axis=1, keepdims=True) + EPS)
        xn = x * r
        dy = dy_ref[...].astype(F32)
        gdy = dy * g_ref[...]
        dx = r * (gdy - xn * jnp.mean(gdy * xn, axis=1, keepdims=True))
        dh_ref[...] = dres_ref[...] + dx
        dg_ref[...] += jnp.sum(dy * xn, axis=0, keepdims=True)

    row = pl.BlockSpec((tr, D), lambda i: (i, 0))
    vec = pl.BlockSpec((1, D), lambda i: (0, 0))
    return pl.pallas_call(
        body, grid=(L // tr,), in_specs=[row, vec, row, row], out_specs=[row, vec],
        out_shape=[jax.ShapeDtypeStruct((L, D), F32), jax.ShapeDtypeStruct((1, D), F32)],
        compiler_params=_params("arbitrary"), name=name)(h, g, dy, dres)


def _glu_res_rms(z, h, g, *, name):
    L, D = h.shape
    nb, cb = z.shape[1], z.shape[3]
    tr = _tile(L, ROW_TILE, SUBLANES)

    def body(z_ref, h_ref, g_ref, h1_ref, hn_ref):
        za = jnp.concatenate([z_ref[0, d] for d in range(nb)], axis=1)
        zg = jnp.concatenate([z_ref[1, d] for d in range(nb)], axis=1)
        x = h_ref[...] + za * jax.nn.sigmoid(zg)
        h1_ref[...] = x
        r = lax.rsqrt(jnp.mean(x * x, axis=1, keepdims=True) + EPS)
        hn_ref[...] = (x * r * g_ref[...]).astype(BF16)

    row = pl.BlockSpec((tr, D), lambda i: (i, 0))
    return pl.pallas_call(
        body, grid=(L // tr,),
        in_specs=[pl.BlockSpec((2, nb, tr, cb), lambda i: (0, 0, i, 0)), row, pl.BlockSpec((1, D), lambda i: (0, 0))],
        out_specs=[row, row],
        out_shape=[jax.ShapeDtypeStruct((L, D), F32), jax.ShapeDtypeStruct((L, D), BF16)],
        compiler_params=_params("parallel"), name=name)(z, h, g)


def _glu_bwd(z, dout, *, name):
    L, D = dout.shape
    nb, cb = z.shape[1], z.shape[3]
    tr = _tile(L, ROW_TILE, SUBLANES)

    def body(z_ref, d_ref, o_ref):
        for d in range(nb):
            dd = d_ref[:, d * cb:(d + 1) * cb]
            sg = jax.nn.sigmoid(z_ref[1, d])
            o_ref[0, d] = (dd * sg).astype(BF16)
            o_ref[1, d] = (dd * z_ref[0, d] * sg * (1.0 - sg)).astype(BF16)

    zs = pl.BlockSpec((2, nb, tr, cb), lambda i: (0, 0, i, 0))
    return pl.pallas_call(
        body, grid=(L // tr,), in_specs=[zs, pl.BlockSpec((tr, D), lambda i: (i, 0))], out_specs=zs,
        out_shape=jax.ShapeDtypeStruct(z.shape, BF16),
        compiler_params=_params("parallel"), name=name)(z, dout)


def _loss_head(h, g, target, *, name):
    L, D = h.shape
    tr = _tile(L, ROW_TILE, SUBLANES)

    def body(h_ref, g_ref, t_ref, loss_ref, dh_ref, dg_ref):
        @pl.when(pl.program_id(0) == 0)
        def _():
            dg_ref[...] = jnp.zeros_like(dg_ref)
            loss_ref[...] = jnp.zeros_like(loss_ref)

        x = h_ref[...]
        gg = g_ref[...]
        r = lax.rsqrt(jnp.mean(x * x, axis=1, keepdims=True) + EPS)
        xn = x * r
        err = xn * gg - t_ref[...]
        loss_ref[...] += 0.5 * jnp.sum(jnp.mean(err * err, axis=1, keepdims=True), axis=0, keepdims=True)
        dy = err * (1.0 / D)
        gdy = dy * gg
        dh_ref[...] = r * (gdy - xn * jnp.mean(gdy * xn, axis=1, keepdims=True))
        dg_ref[...] += jnp.sum(dy * xn, axis=0, keepdims=True)

    row = pl.BlockSpec((tr, D), lambda i: (i, 0))
    vec = pl.BlockSpec((1, D), lambda i: (0, 0))
    return pl.pallas_call(
        body, grid=(L // tr,), in_specs=[row, vec, row],
        out_specs=[pl.BlockSpec((1, 1), lambda i: (0, 0)), row, vec],
        out_shape=[jax.ShapeDtypeStruct((1, 1), F32), jax.ShapeDtypeStruct((L, D), F32),
                   jax.ShapeDtypeStruct((1, D), F32)],
        compiler_params=_params("arbitrary"), name=name)(h, g, target)


def _shift_down(x, halo, k, row):
    y = pltpu.roll(x, k, 0)
    for r in range(k):
        y = jnp.where(row == r, halo[SUBLANES - k + r:SUBLANES - k + r + 1, :], y)
    return y


def _shift_up(x, halo, k, row, n):
    y = pltpu.roll(x, n - k, 0)
    for r in range(k):
        y = jnp.where(row == n - k + r, halo[r:r + 1, :], y)
    return y


def _conv_specs(L, tr, tc):
    nrb = tr // SUBLANES
    main = pl.BlockSpec((2, None, tr, tc), lambda j, i: (0, j, i, 0))
    prev = pl.BlockSpec((2, None, SUBLANES, tc), lambda j, i: (0, j, jnp.maximum(i * nrb - 1, 0), 0))
    nxt = pl.BlockSpec((2, None, SUBLANES, tc),
                       lambda j, i: (0, j, jnp.minimum((i + 1) * nrb, L // SUBLANES - 1), 0))
    cw = pl.BlockSpec((2, None, 3, tc), lambda j, i: (0, j, 0, 0))
    cb = pl.BlockSpec((2, None, 1, tc), lambda j, i: (0, j, 0, 0))
    half = pl.BlockSpec((None, tr, tc), lambda j, i: (j, i, 0))
    return main, prev, nxt, cw, cb, half


def _conv_tile(u_ref, p_ref, w_ref, b_ref, s, first, row):
    x = u_ref[s]
    halo = jnp.where(first, 0.0, p_ref[s])
    w = w_ref[s]
    x1 = _shift_down(x, halo, 1, row)
    x2 = _shift_down(x, halo, 2, row)
    return b_ref[s] + x2 * w[0:1] + x1 * w[1:2] + x * w[2:3], x1, x2


def _conv_act(u0, cw, cb, *, name, xfers=()):
    _, nb, L, tc = u0.shape
    tr = _tile(L, ROW_TILE, SUBLANES)
    main, prev, _, cws, cbs, half = _conv_specs(L, tr, tc)

    def body(u_ref, p_ref, w_ref, b_ref, a_ref):
        first = pl.program_id(1) == 0
        row = lax.broadcasted_iota(jnp.int32, (tr, tc), 0)
        gate, _, _ = _conv_tile(u_ref, p_ref, w_ref, b_ref, 0, first, row)
        up, _, _ = _conv_tile(u_ref, p_ref, w_ref, b_ref, 1, first, row)
        a_ref[...] = (gate * jax.nn.sigmoid(gate) * up).astype(BF16)

    return _pcall(body, (u0, u0, cw, cb), grid=(nb, L // tr), in_specs=[main, prev, cws, cbs], out_specs=[half],
                  out_shape=[jax.ShapeDtypeStruct((nb, L, tc), BF16)], sem=("parallel", "parallel"), name=name,
                  xfers=xfers)


def _conv_act_bwd(u0, cw, cb, da, *, name, xfers=()):
    _, nb, L, tc = u0.shape
    tr = _tile(L, ROW_TILE, SUBLANES)
    main, prev, _, cws, cbs, half = _conv_specs(L, tr, tc)

    def body(u_ref, p_ref, w_ref, b_ref, da_ref, a_ref, du_ref, dcb_ref):
        first = pl.program_id(1) == 0

        @pl.when(first)
        def _():
            dcb_ref[...] = jnp.zeros_like(dcb_ref)

        row = lax.broadcasted_iota(jnp.int32, (tr, tc), 0)
        gate, _, _ = _conv_tile(u_ref, p_ref, w_ref, b_ref, 0, first, row)
        up, _, _ = _conv_tile(u_ref, p_ref, w_ref, b_ref, 1, first, row)
        sg = jax.nn.sigmoid(gate)
        silu = gate * sg
        a_ref[...] = (silu * up).astype(BF16)
        da = da_ref[...]
        dgate = da * up * (sg * (1.0 + gate * (1.0 - sg)))
        dup = da * silu
        du_ref[0] = dgate
        du_ref[1] = dup
        dcb_ref[0] += jnp.sum(dgate, axis=0, keepdims=True)
        dcb_ref[1] += jnp.sum(dup, axis=0, keepdims=True)

    return _pcall(body, (u0, u0, cw, cb, da), grid=(nb, L // tr), in_specs=[main, prev, cws, cbs, half],
                  out_specs=[half, main, cbs],
                  out_shape=[jax.ShapeDtypeStruct((nb, L, tc), BF16), jax.ShapeDtypeStruct((2, nb, L, tc), F32),
                             jax.ShapeDtypeStruct((2, nb, 1, tc), F32)],
                  sem=("parallel", "arbitrary"), name=name, xfers=xfers)


def _conv_bwd(u0, cw, du, *, name, xfers=()):
    _, nb, L, tc = u0.shape
    tr = _tile(L, ROW_TILE, SUBLANES)
    main, prev, nxt, cws, _, _ = _conv_specs(L, tr, tc)
    nr = L // tr

    def body(u_ref, p_ref, w_ref, du_ref, n_ref, du0_ref, dcw_ref):
        i = pl.program_id(1)

        @pl.when(i == 0)
        def _():
            dcw_ref[...] = jnp.zeros_like(dcw_ref)

        row = lax.broadcasted_iota(jnp.int32, (tr, tc), 0)
        for s in range(2):
            x = u_ref[s]
            halo = jnp.where(i == 0, 0.0, p_ref[s])
            x1 = _shift_down(x, halo, 1, row)
            x2 = _shift_down(x, halo, 2, row)
            d = du_ref[s]
            nh = jnp.where(i == nr - 1, 0.0, n_ref[s])
            d1 = _shift_up(d, nh, 1, row, tr)
            d2 = _shift_up(d, nh, 2, row, tr)
            w = w_ref[s]
            du0_ref[s] = (d * w[2:3] + d1 * w[1:2] + d2 * w[0:1]).astype(BF16)
            dcw_ref[s, 0:1, :] += jnp.sum(d * x2, axis=0, keepdims=True)
            dcw_ref[s, 1:2, :] += jnp.sum(d * x1, axis=0, keepdims=True)
            dcw_ref[s, 2:3, :] += jnp.sum(d * x, axis=0, keepdims=True)

    return _pcall(body, (u0, u0, cw, du, du), grid=(nb, nr), in_specs=[main, prev, cws, main, nxt],
                  out_specs=[main, cws],
                  out_shape=[jax.ShapeDtypeStruct((2, nb, L, tc), BF16), jax.ShapeDtypeStruct((2, nb, 3, tc), F32)],
                  sem=("parallel", "arbitrary"), name=name, xfers=xfers)


N_TILES = 64
HALF = N_TILES // 2


def _swap(s):
    return jnp.concatenate([s[HALF:], s[:HALF]], axis=0)


def _chan_block(j):
    return ((j % HALF) // 4) * LANES


def _tiles_of(jb):
    return [4 * jb + i for i in range(4)] + [HALF + 4 * jb + i for i in range(4)]
GELU_C = math.sqrt(2.0 / math.pi)
GELU_A = 0.044715


def _gelu(x):
    return 0.5 * x * (1.0 + jnp.tanh(GELU_C * (x + GELU_A * x * x * x)))


def _gelu_grad(x):
    th = jnp.tanh(GELU_C * (x + GELU_A * x * x * x))
    return 0.5 * (1.0 + th) + 0.5 * x * (1.0 - th * th) * GELU_C * (1.0 + 3.0 * GELU_A * x * x)


def _s5_project_in(u_ref, wb_ref, s3, T, TP):
    for j in range(N_TILES):
        blk = _chan_block(j)
        s3[pl.ds(j * TP + SUBLANES, T), :] = jnp.dot(u_ref[:, blk:blk + LANES], wb_ref[j],
                                                     preferred_element_type=F32)


def _s5_scan_fwd(s3, a1, a2, s0, T, TP):
    span = (N_GROUPS - 1) * TP + 2 * SUBLANES

    def blk(i, s):
        view = s3.at[pl.ds(pl.multiple_of(i * SUBLANES, SUBLANES), span)]
        for k in range(SUBLANES):
            rows = pl.ds(SUBLANES + k, N_GROUPS, stride=TP)
            s = a1 * s + a2 * _swap(s) + view[rows, :]
            view[rows, :] = s
        return s

    return lax.fori_loop(0, T // SUBLANES, blk, s0)


def _s5_fwd(hn, wb, wc, a1, a2, dvec, *, name, xfers=()):
    L, D = hn.shape
    T = min(S5_CHUNK, L)
    TP = T + SUBLANES
    nC = L // T

    def body(u_ref, wb_ref, wc_ref, a1_ref, a2_ref, d_ref, y_ref, yg_ref, sb_ref, s3, st):
        @pl.when(pl.program_id(0) == 0)
        def _():
            st[...] = jnp.zeros_like(st)

        sb_ref[0] = st[...]
        _s5_project_in(u_ref, wb_ref, s3, T, TP)
        st[...] = _s5_scan_fwd(s3, a1_ref[...], a2_ref[...], st[...], T, TP)
        for jb in range(D // LANES):
            acc = jnp.zeros((T, LANES), F32)
            for j in _tiles_of(jb):
                acc += jnp.dot(s3[pl.ds(j * TP + SUBLANES, T), :].astype(BF16), wc_ref[j],
                               preferred_element_type=F32)
            cols = slice(jb * LANES, (jb + 1) * LANES)
            y = acc + d_ref[:, cols] * u_ref[:, cols].astype(F32)
            y_ref[:, cols] = y
            yg_ref[:, cols] = _gelu(y).astype(BF16)

    row = pl.BlockSpec((T, D), lambda c: (c, 0))
    wspec = pl.BlockSpec((N_GROUPS, LANES, LANES), lambda c: (0, 0, 0))
    aspec = pl.BlockSpec((N_GROUPS, LANES), lambda c: (0, 0))
    return _pcall(
        body, (hn, wb, wc, a1, a2, dvec), grid=(nC,),
        in_specs=[row, wspec, wspec, aspec, aspec, pl.BlockSpec((1, D), lambda c: (0, 0))],
        out_specs=[row, row, pl.BlockSpec((1, N_GROUPS, LANES), lambda c: (c, 0, 0))],
        out_shape=[jax.ShapeDtypeStruct((L, D), F32), jax.ShapeDtypeStruct((L, D), BF16),
                   jax.ShapeDtypeStruct((nC, N_GROUPS, LANES), F32)],
        scratch_shapes=[pltpu.VMEM((N_GROUPS * TP, LANES), F32), pltpu.VMEM((N_GROUPS, LANES), F32)],
        sem=("arbitrary",), name=name, xfers=xfers)


def _s5_bwd(hn, dyg, ypre, sbound, wb, wc, a1, a2, dvec, *, name, xfers=()):
    L, D = hn.shape
    T = min(S5_CHUNK, L)
    TP = T + SUBLANES
    nC = L // T
    span = (N_GROUPS - 1) * TP + 2 * SUBLANES
    NT = (((1,), (1,)), ((), ()))
    TN = (((0,), (0,)), ((), ()))

    def body(u_ref, dyg_ref, yp_ref, sb_ref, wb_ref, wc_ref, a1_ref, a2_ref, d_ref,
             du_ref, dwb_ref, dwc_ref, da1_ref, da2_ref, dd_ref, s3, g3, gst, dy_s):
        @pl.when(pl.program_id(0) == 0)
        def _():
            gst[...] = jnp.zeros_like(gst)
            dwb_ref[...] = jnp.zeros_like(dwb_ref)
            dwc_ref[...] = jnp.zeros_like(dwc_ref)
            da1_ref[...] = jnp.zeros_like(da1_ref)
            da2_ref[...] = jnp.zeros_like(da2_ref)
            dd_ref[...] = jnp.zeros_like(dd_ref)

        a1 = a1_ref[...]
        a2 = a2_ref[...]
        dy = dyg_ref[...].astype(F32) * _gelu_grad(yp_ref[...])
        dy_s[...] = dy.astype(BF16)
        dd_ref[...] += jnp.sum(dy * u_ref[...].astype(F32), axis=0, keepdims=True)
        du_ref[...] = d_ref[...] * dy

        s3[pl.ds(SUBLANES - 1, N_GROUPS, stride=TP), :] = sb_ref[0]
        _s5_project_in(u_ref, wb_ref, s3, T, TP)
        _s5_scan_fwd(s3, a1, a2, sb_ref[0], T, TP)

        for j in range(N_TILES):
            blk = _chan_block(j)
            g3[pl.ds(j * TP + SUBLANES, T), :] = lax.dot_general(dy_s[:, blk:blk + LANES], wc_ref[j], NT,
                                                                 preferred_element_type=F32)
        a2c = -a2

        def rblk(ii, carry):
            g, acc1, acc2 = carry
            t0 = pl.multiple_of((T // SUBLANES - 1 - ii) * SUBLANES, SUBLANES)
            gv = g3.at[pl.ds(t0, span)]
            sv = s3.at[pl.ds(t0, span)]
            for k in reversed(range(SUBLANES)):
                rows = pl.ds(SUBLANES + k, N_GROUPS, stride=TP)
                g = a1 * g + a2c * _swap(g) + gv[rows, :]
                gv[rows, :] = g
                sp = sv[pl.ds(SUBLANES - 1 + k, N_GROUPS, stride=TP), :]
                acc1 = acc1 + g * sp
                acc2 = acc2 + g * _swap(sp)
            return g, acc1, acc2

        zero = jnp.zeros((N_GROUPS, LANES), F32)
        g, acc1, acc2 = lax.fori_loop(0, T // SUBLANES, rblk, (gst[...], zero, zero))
        gst[...] = g
        da1_ref[...] += acc1
        da2_ref[...] += acc2

        for jb in range(D // LANES):
            cols = slice(jb * LANES, (jb + 1) * LANES)
            acc = jnp.zeros((T, LANES), F32)
            for j in _tiles_of(jb):
                rows = pl.ds(j * TP + SUBLANES, T)
                gj = g3[rows, :].astype(BF16)
                dwc_ref[j] += lax.dot_general(s3[rows, :].astype(BF16), dy_s[:, cols], TN,
                                              preferred_element_type=F32)
                dwb_ref[j] += lax.dot_general(u_ref[:, cols], gj, TN, preferred_element_type=F32)
                acc += lax.dot_general(gj, wb_ref[j], NT, preferred_element_type=F32)
            du_ref[:, cols] += acc

    rrow = pl.BlockSpec((T, D), lambda c: (nC - 1 - c, 0))
    wspec = pl.BlockSpec((N_GROUPS, LANES, LANES), lambda c: (0, 0, 0))
    aspec = pl.BlockSpec((N_GROUPS, LANES), lambda c: (0, 0))
    vec = pl.BlockSpec((1, D), lambda c: (0, 0))
    return _pcall(
        body, (hn, dyg, ypre, sbound, wb, wc, a1, a2, dvec), grid=(nC,),
        in_specs=[rrow, rrow, rrow, pl.BlockSpec((1, N_GROUPS, LANES), lambda c: (nC - 1 - c, 0, 0)),
                  wspec, wspec, aspec, aspec, vec],
        out_specs=[rrow, wspec, wspec, aspec, aspec, vec],
        out_shape=[jax.ShapeDtypeStruct((L, D), F32),
                   jax.ShapeDtypeStruct((N_GROUPS, LANES, LANES), F32),
                   jax.ShapeDtypeStruct((N_GROUPS, LANES, LANES), F32),
                   jax.ShapeDtypeStruct((N_GROUPS, LANES), F32), jax.ShapeDtypeStruct((N_GROUPS, LANES), F32),
                   jax.ShapeDtypeStruct((1, D), F32)],
        scratch_shapes=[pltpu.VMEM((N_GROUPS * TP, LANES), F32), pltpu.VMEM((N_GROUPS * TP, LANES), F32),
                        pltpu.VMEM((N_GROUPS, LANES), F32), pltpu.VMEM((T, D), BF16)],
        sem=("arbitrary",), name=name, xfers=xfers)


def _s5_prep(lam_re, lam_im, log_dt, b_re, b_im, c_re, c_im):
    dt = jnp.exp(log_dt)[:, None]
    mag = jnp.exp(lam_re * dt)
    lb_re = mag * jnp.cos(lam_im * dt)
    lb_im = mag * jnp.sin(lam_im * dt)
    den = lam_re * lam_re + lam_im * lam_im
    nr = lb_re - 1.0
    fr = ((nr * lam_re + lb_im * lam_im) / den)[..., None]
    fi = ((lb_im * lam_re - nr * lam_im) / den)[..., None]
    bb_re = fr * b_re - fi * b_im
    bb_im = fr * b_im + fi * b_re
    pair = lambda a: a.reshape(HALF, 2 * SSM_STATE)
    a1 = jnp.concatenate([pair(lb_re), pair(lb_re)], axis=0)
    a2 = jnp.concatenate([-pair(lb_im), pair(lb_im)], axis=0)
    sel = jax.nn.one_hot(jnp.arange(HALF) % 4, 4, dtype=F32)
    eye = jnp.eye(2, dtype=F32)

    def w_in(bb):
        return jnp.einsum('jk,ef,jfph->jkehfp', sel, eye, bb.reshape(HALF, 2, SSM_STATE, SSM_GROUP)
                          ).reshape(HALF, LANES, LANES)

    def w_out(c):
        return jnp.einsum('jk,ef,jfhp->jepkfh', sel, eye, c.reshape(HALF, 2, SSM_GROUP, SSM_STATE)
                          ).reshape(HALF, LANES, LANES)

    wb = jnp.concatenate([w_in(bb_re), w_in(bb_im)], axis=0)
    wc = jnp.concatenate([w_out(c_re), w_out(-c_im)], axis=0)
    return a1, a2, wb, wc


def _tri(n, upper):
    r = lax.broadcasted_iota(jnp.int32, (n, n), 0)
    c = lax.broadcasted_iota(jnp.int32, (n, n), 1)
    return ((r <= c) if upper else (r >= c)).astype(F32)


def _fgate_fwd(fl, bf, *, name):
    L, W = fl.shape
    tr = _tile(L, CUM_TILE, SUBLANES)

    def body(f_ref, b_ref, o_ref, carry):
        @pl.when(pl.program_id(0) == 0)
        def _():
            carry[...] = jnp.zeros_like(carry)

        x = f_ref[...] + b_ref[...]
        ls = jnp.minimum(x, 0.0) - jnp.log(1.0 + jnp.exp(-jnp.abs(x)))
        cum = jnp.dot(_tri(tr, False), ls, preferred_element_type=F32, precision=lax.Precision.HIGHEST) + carry[...]
        o_ref[...] = cum
        carry[...] = cum[tr - 1:tr, :]

    return pl.pallas_call(
        body, grid=(L // tr,),
        in_specs=[pl.BlockSpec((tr, W), lambda i: (i, 0)), pl.BlockSpec((1, W), lambda i: (0, 0))],
        out_specs=pl.BlockSpec((tr, W), lambda i: (i, 0)), out_shape=jax.ShapeDtypeStruct((L, W), F32),
        scratch_shapes=[pltpu.VMEM((1, W), F32)], compiler_params=_params("arbitrary"), name=name)(fl, bf)


def _fgate_bwd(fl, bf, dcum, *, name):
    L, W = fl.shape
    tr = _tile(L, CUM_TILE, SUBLANES)
    n = L // tr

    def body(f_ref, b_ref, d_ref, o_ref, db_ref, carry):
        @pl.when(pl.program_id(0) == 0)
        def _():
            carry[...] = jnp.zeros_like(carry)
            db_ref[...] = jnp.zeros_like(db_ref)

        d = d_ref[...]
        rev = jnp.dot(_tri(tr, True), d, preferred_element_type=F32, precision=lax.Precision.HIGHEST) + carry[...]
        carry[...] += jnp.sum(d, axis=0, keepdims=True)
        df = rev * jax.nn.sigmoid(-(f_ref[...] + b_ref[...]))
        o_ref[...] = df
        db_ref[...] += jnp.sum(df, axis=0, keepdims=True)

    rrow = pl.BlockSpec((tr, W), lambda i: (n - 1 - i, 0))
    vec = pl.BlockSpec((1, W), lambda i: (0, 0))
    return pl.pallas_call(
        body, grid=(n,), in_specs=[rrow, vec, rrow], out_specs=[rrow, vec],
        out_shape=[jax.ShapeDtypeStruct((L, W), F32), jax.ShapeDtypeStruct((1, W), F32)],
        scratch_shapes=[pltpu.VMEM((1, W), F32)], compiler_params=_params("arbitrary"), name=name)(fl, bf, dcum)


_NT = (((1,), (1,)), ((), ()))
_TN = (((0,), (0,)), ((), ()))
HEAD_PAIRS = N_HEADS // 2


def _attn_logits(qs, k, ck, masked, t):
    s = lax.dot_general(qs, k, _NT, preferred_element_type=F32) - ck
    if masked:
        r = lax.broadcasted_iota(jnp.int32, (t, t), 0)
        c = lax.broadcasted_iota(jnp.int32, (t, t), 1)
        s = jnp.where(c > r, NEG, s)
    return s


def _attn_fwd(q, kv, ck, *, name, xfers=()):
    L, D = q.shape
    t = _tile(L, ATTN_TILE)
    n = L // t
    dh = HEAD_DIM

    def body(q_ref, k_ref, v_ref, ck_ref, o_ref, o32_ref, lse_ref, m_s, l_s, acc):
        i, j = pl.program_id(1), pl.program_id(2)

        @pl.when(j == 0)
        def _():
            m_s[...] = jnp.full_like(m_s, NEG)
            l_s[...] = jnp.zeros_like(l_s)
            acc[...] = jnp.zeros_like(acc)

        def tile(masked):
            for e in range(2):
                sl = slice(e * dh, (e + 1) * dh)
                v = v_ref[:, sl]
                s = _attn_logits(q_ref[:, sl] * ATTN_SCALE, k_ref[:, sl], ck_ref[e], masked, t)
                m_new = jnp.maximum(m_s[e], jnp.max(s, axis=1, keepdims=True))
                alpha = jnp.exp(m_s[e] - m_new)
                p = jnp.exp(s - m_new)
                l_s[e] = alpha * l_s[e] + jnp.sum(p, axis=1, keepdims=True)
                p_hi = p.astype(BF16)
                p_lo = (p - p_hi.astype(F32)).astype(BF16)
                pv = (jnp.dot(p_hi, v, preferred_element_type=F32) + jnp.dot(p_lo, v, preferred_element_type=F32))
                acc[e] = alpha * acc[e] + pv
                m_s[e] = m_new

        pl.when(j < i)(functools.partial(tile, False))
        pl.when(j == i)(functools.partial(tile, True))

        @pl.when(j == n - 1)
        def _():
            for e in range(2):
                sl = slice(e * dh, (e + 1) * dh)
                o = acc[e] / l_s[e]
                o_ref[:, sl] = o.astype(BF16)
                o32_ref[:, sl] = o
                lse_ref[e] = m_s[e] + jnp.log(l_s[e])

    qs = pl.BlockSpec((t, LANES), lambda h, i, j: (i, h))
    ks = pl.BlockSpec((t, LANES), lambda h, i, j: (jnp.minimum(i, j), h))
    vs = pl.BlockSpec((t, LANES), lambda h, i, j: (jnp.minimum(i, j), HEAD_PAIRS + h))
    cs = pl.BlockSpec((2, 1, t), lambda h, i, j: (h, 0, jnp.minimum(i, j)))
    return _pcall(
        body, (q, kv, kv, ck), grid=(HEAD_PAIRS, n, n), in_specs=[qs, ks, vs, cs],
        out_specs=[qs, qs, pl.BlockSpec((2, t, 1), lambda h, i, j: (h, i, 0))],
        out_shape=[jax.ShapeDtypeStruct((L, D), BF16), jax.ShapeDtypeStruct((L, D), F32),
                   jax.ShapeDtypeStruct((N_HEADS, L, 1), F32)],
        scratch_shapes=[pltpu.VMEM((2, t, 1), F32), pltpu.VMEM((2, t, 1), F32), pltpu.VMEM((2, t, dh), F32)],
        sem=("parallel", "parallel", "arbitrary"), name=name, xfers=xfers)


def _attn_bwd(q, kv, ck, o, do, lse, *, name, xfers=()):
    L, D = q.shape
    t = _tile(L, ATTN_TILE)
    n = L // t
    dh = HEAD_DIM

    def body(q_ref, k_ref, v_ref, ck_ref, o_ref, do_ref, lse_ref, dq_ref, dk_ref, dv_ref, dck_ref):
        j, i = pl.program_id(1), pl.program_id(2)

        @pl.when(jnp.logical_and(i == 0, j == 0))
        def _():
            dq_ref[...] = jnp.zeros_like(dq_ref)

        @pl.when(i == 0)
        def _():
            dk_ref[...] = jnp.zeros_like(dk_ref)
            dv_ref[...] = jnp.zeros_like(dv_ref)
            dck_ref[...] = jnp.zeros_like(dck_ref)

        def tile(masked):
            rows = pl.ds(pl.multiple_of(i * t, t), t)
            for e in range(2):
                sl = slice(e * dh, (e + 1) * dh)
                qs = q_ref[:, sl] * ATTN_SCALE
                k = k_ref[:, sl]
                do = do_ref[:, sl]
                s = _attn_logits(qs, k, ck_ref[e], masked, t)
                p = jnp.exp(s - lse_ref[e])
                dp = lax.dot_general(do, v_ref[:, sl], _NT, preferred_element_type=F32)
                delta = jnp.sum(do.astype(F32) * o_ref[:, sl], axis=1, keepdims=True)
                ds = p * (dp - delta)
                ds16 = ds.astype(BF16)
                dv_ref[:, sl] += lax.dot_general(p.astype(BF16), do, _TN, preferred_element_type=F32)
                dk_ref[:, sl] += lax.dot_general(ds16, qs, _TN, preferred_element_type=F32)
                dck_ref[e] -= jnp.sum(ds, axis=0, keepdims=True)
                dq_ref[rows, sl] += jnp.dot(ds16, k, preferred_element_type=F32) * ATTN_SCALE

        pl.when(i > j)(functools.partial(tile, False))
        pl.when(i == j)(functools.partial(tile, True))

    qs = pl.BlockSpec((t, LANES), lambda h, j, i: (jnp.maximum(i, j), h))
    ks = pl.BlockSpec((t, LANES), lambda h, j, i: (j, h))
    vs = pl.BlockSpec((t, LANES), lambda h, j, i: (j, HEAD_PAIRS + h))
    cs = pl.BlockSpec((2, 1, t), lambda h, j, i: (h, 0, j))
    ls = pl.BlockSpec((2, t, 1), lambda h, j, i: (h, jnp.maximum(i, j), 0))
    full = jax.ShapeDtypeStruct((L, D), F32)
    return _pcall(
        body, (q, kv, kv, ck, o, do, lse), grid=(HEAD_PAIRS, n, n), in_specs=[qs, ks, vs, cs, qs, qs, ls],
        out_specs=[pl.BlockSpec((L, LANES), lambda h, j, i: (0, h)), ks, ks, cs],
        out_shape=[full, full, full, jax.ShapeDtypeStruct((N_HEADS, 1, L), F32)],
        sem=("parallel", "arbitrary", "arbitrary"), name=name, xfers=xfers)


SHARD_COLS_FFN = 2 * D_FF // N_DEV
SHARD_ROWS_FFN = D_FF // N_DEV
SHARD_COLS_GLU = 2 * D_MODEL // N_DEV
SHARD_ROWS_QO = D_MODEL // N_DEV
SHARD_COLS_KVF = (2 * D_MODEL + N_HEADS) // N_DEV
S5_NAMES = ("lam_re", "lam_im", "log_dt", "ssm_b_re", "ssm_b_im", "ssm_c_re", "ssm_c_im")
REPL_LATE_ROWS = 288
REPL_EARLY_ROWS = 320


def _leaves(parts):
    out = []
    for p in parts:
        out.extend(_leaves(p) if isinstance(p, (list, tuple)) else [p.reshape(-1)])
    return out


def _pack_rows(parts, rows):
    flat = jnp.concatenate(_leaves(parts))
    return jnp.pad(flat, (0, rows * D_MODEL - flat.shape[0])).reshape(rows, D_MODEL)


def _unpack_rows(flat, like):
    flat, out, off = flat.reshape(-1), [], 0
    for p in _leaves(like):
        out.append(flat[off:off + p.shape[0]])
        off += p.shape[0]
    return out


def _repl_late(d):
    return [d["g_mix"][0], [d[n][0] for n in S5_NAMES]]


def _repl_early(d):
    return [list(d["g_mix"][1:]), list(d["g_ffn"]), [d[n][1] for n in S5_NAMES], d["g_kv"], d["b_f"],
            list(d["ffn_conv_b"]), d["g_final"]]


def _kvf_blocks(full):
    return full.reshape(D_MODEL, N_DEV, SHARD_COLS_KVF).transpose(1, 0, 2)


class _Step:
    def __init__(self, weights, send=None, plan=None):
        self.w = dict(weights)
        self.send = send or {}
        self.grad = {}
        self.bcast = {}
        self.slots = {}
        self.plan = plan or {}

    def xfers(self, host):
        src = {"w": self.send, "g": self.grad, "b": self.bcast}
        return [(src[kind][k], kind == "g") for kind, k in self.plan.get(host, ())]

    def land(self, host, gathered):
        for (kind, k), g in zip(self.plan.get(host, ()), gathered):
            if kind == "w":
                self.arrive(k, g)
            else:
                self.slots[k] = g

    def arrive(self, k, g):
        name = k[0]
        if name == "w_ffn_out":
            self.w[k] = g.reshape(4, 2 * SHARD_ROWS_FFN, D_MODEL)
        elif name in ("w_q", "w_o"):
            self.w[k] = g.reshape(D_MODEL, D_MODEL)
        elif name == "w_kvf":
            full = g.transpose(1, 0, 2).reshape(D_MODEL, N_DEV * SHARD_COLS_KVF)
            self.w["w_kv",] = full[:, :2 * D_MODEL]
            self.w["w_f",] = jnp.pad(full[:, 2 * D_MODEL:], ((0, 0), (0, LANES - N_HEADS)))
        elif name == "small":
            flat = g.reshape(N_DEV, -1)
            self.w["ssm_d",] = flat[:, :256].reshape(N_DEV, N_A, LANES).transpose(1, 0, 2).reshape(N_A, D_MODEL)
            cw = flat[:, 256:256 + DEPTH * 3 * SHARD_COLS_FFN].reshape(N_DEV, DEPTH, 3, SHARD_COLS_FFN)
            for layer in range(DEPTH):
                self.w["conv_w", layer] = cw[:, layer].reshape(2, 4, 3, SHARD_COLS_FFN)
        else:
            self.w[k] = g

    def run(self, host, fn, *args, **kw):
        xf = self.xfers(host)
        res = fn(*args, name=host, xfers=xf, **kw)
        if not xf:
            return res[0] if isinstance(res, (list, tuple)) and len(res) == 1 else res
        n_own = len(res) - len(xf)
        self.land(host, res[n_own:])
        return res[0] if n_own == 1 else res[:n_own]


def _step(x, target, S):
    L = x.shape[0]
    W = S.w
    vec = lambda a: a.reshape(1, -1)
    CF = SHARD_COLS_FFN

    h = x
    saved = []
    kvs = None
    for layer in range(DEPTH):
        t = str(layer)
        if layer < N_A:
            (a1, a2, wb, wc), prep_vjp = jax.vjp(_s5_prep, *[W[n][layer] for n in S5_NAMES])
            wb16, wc16 = wb.astype(BF16), wc.astype(BF16)
            hn = _rms_fwd(h, vec(W["g_mix"][layer]), name="mix_norm" + t)
            dvec = vec(W["ssm_d",][layer])
            ypre, yg, sb = S.run("s5_fwd" + t, _s5_fwd, hn, wb16, wc16, a1, a2, dvec)
            z = S.run("glu_mm" + t, _mm, yg, W["w_glu", layer], bk="bkn", ok="bmn")
            z = z.reshape(2, 4, L, SHARD_COLS_GLU)
            h1, hn2 = _glu_res_rms(z, h, vec(W["g_ffn"][layer]), name="glu_res" + t)
            mix_saved = (h, hn, ypre, yg, sb, z, a1, a2, wb16, wc16, dvec, prep_vjp)
        else:
            j = layer - N_A
            if layer == N_A:
                hkv = _rms_fwd(h, vec(W["g_kv"]), name="kv_norm")
                kvm = S.run("kv_mm", _mm, hkv, W["w_kv",], out_dtype=BF16)
                fl = S.run("f_mm", _mm, hkv, W["w_f",])
                cum = _fgate_fwd(fl, W["b_f_pad",], name="fgate_fwd")
                ck = cum[:, :N_HEADS].T.reshape(N_HEADS, 1, L)
                kvs = (h, hkv, fl, kvm, ck)
            _, _, _, kvm, ck = kvs
            hn = _rms_fwd(h, vec(W["g_mix"][layer]), name="mix_norm" + t)
            q = S.run("q_mm" + t, _mm, hn, W["w_q", j], out_dtype=BF16)
            o, o32, lse = S.run("attn_fwd" + t, _attn_fwd, q, kvm, ck)
            h1 = S.run("o_mm" + t, _mm, o, W["w_o", j], add=h)
            hn2 = _rms_fwd(h1, vec(W["g_ffn"][layer]), name="ffn_norm" + t)
            mix_saved = (h, hn, q, o32, o, lse)
        u0 = S.run("ffn_in" + t, _mm, hn2, W["w_ffn_in", layer], bk="bkn", ok="bmn").reshape(2, 4, L, CF)
        a = S.run("ffn_act" + t, _conv_act, u0, W["conv_w", layer], W["conv_b", layer])
        h2 = S.run("ffn_out" + t, _mm, a, W["w_ffn_out", layer], ak="bmk", bk="kbn", add=h1)
        saved.append((mix_saved, h1, hn2, u0))
        h = h2

    loss, dh, dg_final = _loss_head(h, vec(W["g_final"]), target, name="loss_head")
    g = {"g_final": dg_final.reshape(-1)}
    gl = {k: [None] * DEPTH for k in ("g_mix", "g_ffn", "conv_w", "ffn_conv_b")}
    ga = {k: [None] * N_A for k in S5_NAMES + ("ssm_d",)}
    dk = dv = dck = None
    for layer in reversed(range(DEPTH)):
        t = str(layer)
        mix_saved, h1, hn2, u0 = saved[layer]
        cw, cb = W["conv_w", layer], W["conv_b", layer]
        da = S.run("ffn_da" + t, _mm, dh, W["w_ffn_out", layer], bk="nbk", ok="bmn")
        a, du, dcb = S.run("ffn_act_bwd" + t, _conv_act_bwd, u0, cw, cb, da)
        dw_out = S.run("ffn_dwout" + t, _mm, a, dh, ak="bkm", ok="mbn", out_dtype=BF16)
        S.grad["w_ffn_out", layer] = dw_out.reshape(N_DEV, SHARD_ROWS_FFN, D_MODEL)
        du0, dcw = S.run("ffn_conv_bwd" + t, _conv_bwd, u0, cw, du)
        du0 = du0.reshape(N_DEV, L, CF)
        S.grad["w_ffn_in", layer] = S.run("ffn_dwin" + t, _mm, hn2, du0, ak="km", bk="bkn", ok="bmn", out_dtype=BF16)
        dhn2 = S.run("ffn_dhn" + t, _mm, du0, W["w_ffn_in", layer], ak="bmk", bk="bnk")
        dh1, dg = _rms_bwd(h1, vec(W["g_ffn"][layer]), dhn2, dh, name="ffn_norm_bwd" + t)
        gl["g_ffn"][layer], gl["conv_w"][layer], gl["ffn_conv_b"][layer] = dg.reshape(-1), dcw, dcb.reshape(-1)
        if layer < N_A:
            hin, hn, ypre, yg, sb, z, a1, a2, wb16, wc16, dvec, prep_vjp = mix_saved
            dz = _glu_bwd(z, dh1, name="glu_bwd" + t).reshape(N_DEV, L, SHARD_COLS_GLU)
            S.grad["w_glu", layer] = S.run("glu_dw" + t, _mm, yg, dz, ak="km", bk="bkn", ok="bmn", out_dtype=BF16)
            dyg = S.run("glu_dy" + t, _mm, dz, W["w_glu", layer], ak="bmk", bk="bnk", out_dtype=BF16)
            if layer == 0:
                S.bcast["repl_early",] = _pack_rows(_repl_early({**g, **gl, **ga}), REPL_EARLY_ROWS)
            du, dwb, dwc, da1, da2, dd = S.run("s5_bwd" + t, _s5_bwd, hn, dyg, ypre, sb, wb16, wc16, a1, a2, dvec)
            for nme, val in zip(S5_NAMES, prep_vjp((da1, da2, dwb, dwc))):
                ga[nme][layer] = val
            ga["ssm_d"][layer] = dd.reshape(-1)
            dh, dg = _rms_bwd(hin, vec(W["g_mix"][layer]), du, dh1, name="mix_norm_bwd" + t)
        else:
            j = layer - N_A
            hin, hn, q, o32, o, lse = mix_saved
            _, _, _, kvm, ck = kvs
            S.grad["w_o", j] = S.run("o_dw" + t, _mm, o, dh1, ak="km", out_dtype=BF16
                                     ).reshape(N_DEV, SHARD_ROWS_QO, D_MODEL)
            do = S.run("o_dx" + t, _mm, dh1, W["w_o", j], bk="nk", out_dtype=BF16)
            dq, dk_l, dv_l, dck_l = S.run("attn_bwd" + t, _attn_bwd, q, kvm, ck, o32, do, lse)
            dk = dk_l if dk is None else dk + dk_l
            dv = dv_l if dv is None else dv + dv_l
            dck = dck_l if dck is None else dck + dck_l
            S.grad["w_q", j] = S.run("q_dw" + t, _mm, hn, dq, ak="km", out_dtype=BF16
                                     ).reshape(N_DEV, SHARD_ROWS_QO, D_MODEL)
            dhn = S.run("q_dx" + t, _mm, dq, W["w_q", j], bk="nk")
            dh, dg = _rms_bwd(hin, vec(W["g_mix"][layer]), dhn, dh1, name="mix_norm_bwd" + t)
            if layer == N_A:
                hkv_in, hkv, fl, _, _ = kvs
                dcum = jnp.pad(dck.reshape(N_HEADS, L).T, ((0, 0), (0, LANES - N_HEADS)))
                dfl, dbf = _fgate_bwd(fl, W["b_f_pad",], dcum, name="fgate_bwd")
                dkv = jnp.concatenate([dk, dv], axis=1).astype(BF16)
                dfl16 = dfl.astype(BF16)
                dw_kv = S.run("kv_dw", _mm, hkv, dkv, ak="km")
                dw_f = S.run("f_dw", _mm, hkv, dfl16, ak="km")
                S.grad["w_kvf",] = _kvf_blocks(jnp.concatenate([dw_kv, dw_f[:, :N_HEADS]], axis=1)).astype(BF16)
                dhkv = S.run("kv_dx", _mm, dkv, W["w_kv",], bk="nk")
                dhkv = S.run("f_dx", _mm, dfl16, W["w_f",], bk="nk", add=dhkv)
                g["b_f"] = dbf[0, :N_HEADS]
                dh, dgkv = _rms_bwd(hkv_in, vec(W["g_kv"]), dhkv, dh, name="kv_norm_bwd")
                g["g_kv"] = dgkv.reshape(-1)
        gl["g_mix"][layer] = dg.reshape(-1)

    for d in (gl, ga):
        for k, v in d.items():
            g[k] = jnp.stack(v)
    return loss, dh, g


def _reduce_adamw(slots, w, m, v, layer, prev, *, name):
    nl, R, C = w.shape
    tr = _tile(R, 256, 16)
    c1 = 1.0 / (1.0 - ADAM_B1 ** ADAM_STEP)
    c2 = 1.0 / (1.0 - ADAM_B2 ** ADAM_STEP)

    def body(s_ref, w_ref, m_ref, v_ref, *rest):
        g_ref, d_ref, nm_ref, nv_ref = rest[-4:]
        g = s_ref[0].astype(F32)
        for d in range(1, N_DEV):
            g = g + s_ref[d].astype(F32)
        m2 = ADAM_B1 * m_ref[...] + (1.0 - ADAM_B1) * g
        v2 = ADAM_B2 * v_ref[...] + (1.0 - ADAM_B2) * (g * g)
        g_ref[...] = g
        nm_ref[...] = m2
        nv_ref[...] = v2
        d_ref[...] = -ADAM_LR * ((m2 * c1) / (jnp.sqrt(v2 * c2) + ADAM_EPS) + ADAM_WD * w_ref[...])

    row = pl.BlockSpec((None, tr, C), lambda i: (layer, i, 0))
    out = jax.ShapeDtypeStruct((nl, R, C), F32)
    n_prev = 0 if prev is None else 4
    return pl.pallas_call(
        body, grid=(R // tr,),
        in_specs=[pl.BlockSpec((N_DEV, tr, C), lambda i: (0, i, 0)), row, row, row] + [_ANY] * n_prev,
        out_specs=[row, row, row, row], out_shape=[out, out, out, out],
        input_output_aliases={4 + k: k for k in range(n_prev)},
        compiler_params=_params("parallel"), name=name)(slots, w, m, v, *(prev or ()))


def _adamw_layers(slots, w, m, v, *, name):
    shape = w.shape
    w3, m3, v3 = (a.reshape((len(slots),) + a.shape[-2:]) for a in (w, m, v))
    outs = None
    for layer, s in enumerate(slots):
        outs = _reduce_adamw(s, w3, m3, v3, layer, outs, name=f"{name}{layer}")
    return [o.reshape(shape) for o in outs]


_SMALL_ROWS = 72
_ORDER = ("g_mix", "g_ffn", "lam_re", "lam_im", "log_dt", "ssm_b_re", "ssm_b_im", "ssm_c_re", "ssm_c_im", "ssm_d",
          "w_glu", "g_kv", "w_kvf", "b_f", "w_q", "w_o", "w_ffn_in", "ffn_conv_w", "ffn_conv_b", "w_ffn_out", "g_final")


def _pack_small(ssm_d, conv_w):
    flat = jnp.concatenate([ssm_d.reshape(-1), conv_w.reshape(-1)])
    return jnp.pad(flat, (0, _SMALL_ROWS * LANES - flat.shape[0])).reshape(_SMALL_ROWS, LANES)


def _unpack_small(flat):
    flat = flat.reshape(-1)
    return flat[:256].reshape(2, 128), flat[256:256 + 8448].reshape(4, 3, 704)


_FWD_PLAN = {
    "start": [("small",), ("w_glu", 0)],
    "s5_fwd0": [("w_ffn_in", 0)],
    "ffn_in0": [("w_ffn_out", 0)],
    "ffn_act0": [("w_glu", 1)],
    "s5_fwd1": [("w_ffn_in", 1)],
    "ffn_in1": [("w_ffn_out", 1)],
    "ffn_act1": [("w_kvf",), ("w_q", 0)],
    "ffn_out1": [("w_o", 0)],
    "attn_fwd2": [("w_ffn_in", 2), ("w_ffn_out", 2), ("w_q", 1), ("w_o", 1), ("w_ffn_in", 3), ("w_ffn_out", 3)],
}
_BWD_PLAN = {
    "ffn_conv_bwd3": [("w_ffn_out", 3)],
    "attn_bwd3": [("w_ffn_in", 3), ("w_o", 1)],
    "ffn_act_bwd2": [("w_q", 1)],
    "ffn_conv_bwd2": [("w_ffn_out", 2)],
    "attn_bwd2": [("w_ffn_in", 2), ("w_o", 0)],
    "ffn_act_bwd1": [("w_q", 0), ("w_kvf",)],
    "ffn_conv_bwd1": [("w_ffn_out", 1)],
    "s5_bwd1": [("w_ffn_in", 1), ("w_glu", 1)],
    "ffn_conv_bwd0": [("w_ffn_out", 0)],
    "s5_bwd0": [("w_ffn_in", 0), ("w_glu", 0), ("repl_early",)],
    "end": [("small",), ("repl_late",)],
}
_PLAN = {h: [("w", k) for k in ks] for h, ks in _FWD_PLAN.items()}
_PLAN.update({h: [("b" if k[0].startswith("repl") else "g", k) for k in ks] for h, ks in _BWD_PLAN.items()})


def kernel(x, g_mix, g_ffn, lam_re, lam_im, log_dt, ssm_b_re, ssm_b_im, ssm_c_re, ssm_c_im, ssm_d, w_glu, g_kv, w_kvf, b_f, w_q, w_o, w_ffn_in, ffn_conv_w, ffn_conv_b, w_ffn_out, g_final, loss_target, m_g_mix, m_g_ffn, m_lam_re, m_lam_im, m_log_dt, m_ssm_b_re, m_ssm_b_im, m_ssm_c_re, m_ssm_c_im, m_ssm_d, m_w_glu, m_g_kv, m_w_kvf, m_b_f, m_w_q, m_w_o, m_w_ffn_in, m_ffn_conv_w, m_ffn_conv_b, m_w_ffn_out, m_g_final, v_g_mix, v_g_ffn, v_lam_re, v_lam_im, v_log_dt, v_ssm_b_re, v_ssm_b_im, v_ssm_c_re, v_ssm_c_im, v_ssm_d, v_w_glu, v_g_kv, v_w_kvf, v_b_f, v_w_q, v_w_o, v_w_ffn_in, v_ffn_conv_w, v_ffn_conv_b, v_w_ffn_out, v_g_final):
    wts = dict(g_mix=g_mix, g_ffn=g_ffn, lam_re=lam_re, lam_im=lam_im, log_dt=log_dt, ssm_b_re=ssm_b_re,
               ssm_b_im=ssm_b_im, ssm_c_re=ssm_c_re, ssm_c_im=ssm_c_im, ssm_d=ssm_d, w_glu=w_glu, g_kv=g_kv,
               w_kvf=w_kvf, b_f=b_f, w_q=w_q, w_o=w_o, w_ffn_in=w_ffn_in, ffn_conv_w=ffn_conv_w,
               ffn_conv_b=ffn_conv_b, w_ffn_out=w_ffn_out, g_final=g_final)
    mom = dict(g_mix=m_g_mix, g_ffn=m_g_ffn, lam_re=m_lam_re, lam_im=m_lam_im, log_dt=m_log_dt, ssm_b_re=m_ssm_b_re,
               ssm_b_im=m_ssm_b_im, ssm_c_re=m_ssm_c_re, ssm_c_im=m_ssm_c_im, ssm_d=m_ssm_d, w_glu=m_w_glu,
               g_kv=m_g_kv, w_kvf=m_w_kvf, b_f=m_b_f, w_q=m_w_q, w_o=m_w_o, w_ffn_in=m_w_ffn_in,
               ffn_conv_w=m_ffn_conv_w, ffn_conv_b=m_ffn_conv_b, w_ffn_out=m_w_ffn_out, g_final=m_g_final)
    var = dict(g_mix=v_g_mix, g_ffn=v_g_ffn, lam_re=v_lam_re, lam_im=v_lam_im, log_dt=v_log_dt, ssm_b_re=v_ssm_b_re,
               ssm_b_im=v_ssm_b_im, ssm_c_re=v_ssm_c_re, ssm_c_im=v_ssm_c_im, ssm_d=v_ssm_d, w_glu=v_w_glu,
               g_kv=v_g_kv, w_kvf=v_w_kvf, b_f=v_b_f, w_q=v_w_q, w_o=v_w_o, w_ffn_in=v_w_ffn_in,
               ffn_conv_w=v_ffn_conv_w, ffn_conv_b=v_ffn_conv_b, w_ffn_out=v_w_ffn_out, g_final=v_g_final)
    kinds = ("grad", "delta", "m", "v")

    ready = {n: wts[n] for n in ("g_mix", "g_ffn", "g_kv", "g_final") + S5_NAMES}
    ready["b_f_pad",] = jnp.pad(b_f, (0, LANES - N_HEADS)).reshape(1, LANES)
    send = {("small",): _pack_small(ssm_d, ffn_conv_w), ("w_kvf",): w_kvf.astype(BF16)}
    for layer in range(DEPTH):
        ready["conv_b", layer] = ffn_conv_b[layer].reshape(2, 4, 1, SHARD_COLS_FFN)
        send["w_ffn_in", layer] = w_ffn_in[layer].astype(BF16)
        send["w_ffn_out", layer] = w_ffn_out[layer].astype(BF16)
    for layer in range(N_A):
        send["w_glu", layer] = w_glu[layer].astype(BF16)
        send["w_q", layer] = w_q[layer].astype(BF16)
        send["w_o", layer] = w_o[layer].astype(BF16)

    S = _Step(ready, send, _PLAN)
    S.land("start", _exchange(S.xfers("start"), name="start"))
    loss, dx, g = _step(x[0], loss_target[0], S)
    loss = lax.psum(loss[0, 0], MESH_AXES)

    g_d = g["ssm_d"].reshape(N_A, N_DEV, LANES).transpose(1, 0, 2).reshape(N_DEV, N_A * LANES)
    g_cw = jnp.stack([g["conv_w"][layer].reshape(N_DEV, 3, SHARD_COLS_FFN) for layer in range(DEPTH)], axis=1)
    g_small = jnp.concatenate([g_d, g_cw.reshape(N_DEV, -1)], axis=1)
    g_small = jnp.pad(g_small, ((0, 0), (0, _SMALL_ROWS * LANES - g_small.shape[1])))
    S.grad["small",] = g_small.reshape(N_DEV, _SMALL_ROWS, LANES)
    S.bcast["repl_late",] = _pack_rows(_repl_late(g), REPL_LATE_ROWS)
    S.land("end", _exchange(S.xfers("end"), name="end"))

    res = {}
    for name, nl in (("w_glu", N_A), ("w_q", DEPTH - N_A), ("w_o", DEPTH - N_A), ("w_ffn_in", DEPTH),
                     ("w_ffn_out", DEPTH)):
        outs = _adamw_layers([S.slots[name, layer] for layer in range(nl)], wts[name], mom[name], var[name],
                             name="adamw_" + name)
        res.update({(kind, name): a for kind, a in zip(kinds, outs)})
    outs = _adamw_layers([S.slots["w_kvf",]], w_kvf, m_w_kvf, v_w_kvf, name="adamw_w_kvf")
    res.update({(kind, "w_kvf"): a for kind, a in zip(kinds, outs)})
    outs = _adamw_layers([S.slots["small",]], _pack_small(ssm_d, ffn_conv_w), _pack_small(m_ssm_d, m_ffn_conv_w),
                         _pack_small(v_ssm_d, v_ffn_conv_w), name="adamw_small")
    for kind, flat in zip(kinds, outs):
        res[kind, "ssm_d"], res[kind, "ffn_conv_w"] = _unpack_small(flat)

    pieces = {}
    for key, rows, sel in ((("repl_early",), REPL_EARLY_ROWS, _repl_early), (("repl_late",), REPL_LATE_ROWS, _repl_late)):
        outs = _adamw_layers([S.slots[key]], *[_pack_rows(sel(d), rows) for d in (wts, mom, var)],
                             name="adamw_" + key[0])
        for kind, flat in zip(kinds, outs):
            pieces[kind, key[0]] = _unpack_rows(flat, sel(wts))
    for kind in kinds:
        early, late = iter(pieces[kind, "repl_early"]), iter(pieces[kind, "repl_late"])
        take = lambda it, n: [next(it) for _ in range(n)]
        res[kind, "g_mix"] = jnp.stack(take(late, 1) + take(early, DEPTH - 1))
        res[kind, "g_ffn"] = jnp.stack(take(early, DEPTH))
        for n in S5_NAMES:
            res[kind, n] = jnp.stack([next(late), next(early)]).reshape(wts[n].shape)
        res[kind, "g_kv"], res[kind, "b_f"] = next(early), next(early)
        res[kind, "ffn_conv_b"] = jnp.stack(take(early, DEPTH))
        res[kind, "g_final"] = next(early)

    return (loss, dx[None], *[res[kind, n] for kind in kinds for n in _ORDER])
```

```python
import functools
import math

import jax
import jax.numpy as jnp
from jax import lax
from jax.experimental import pallas as pl
from jax.experimental.pallas import tpu as pltpu

F32 = jnp.float32
BF16 = jnp.bfloat16

D_MODEL = 1024
DEPTH = 4
N_A = 2
N_GROUPS = 64
SSM_GROUP = 16
SSM_STATE = 64
N_HEADS = 16
HEAD_DIM = 64
ATTN_SCALE = HEAD_DIM ** -0.5
D_FF = 2816
EPS = 1e-6
N_DEV = 8
LANES = 128
SUBLANES = 8

ADAM_LR = 0.001
ADAM_B1 = 0.9
ADAM_B2 = 0.999
ADAM_EPS = 1e-08
ADAM_WD = 0.01
ADAM_STEP = 10

ROW_TILE = 512
S5_CHUNK = 256
ATTN_TILE = 512
CUM_TILE = 256
NEG = -1e30

MESH_AXES = ("x", "y", "c")


def _tile(n, target, align=LANES):
    t = (min(target, n) // align) * align
    while t >= align:
        if n % t == 0:
            return t
        t -= align
    return n


def _params(*sem):
    return pltpu.CompilerParams(dimension_semantics=sem, vmem_limit_bytes=56 * 1024 * 1024)


_ANY = pl.BlockSpec(memory_space=pl.ANY)
_XFER_SEMS = (pltpu.SemaphoreType.DMA((N_DEV - 1,)), pltpu.SemaphoreType.DMA((N_DEV - 1,)), pltpu.SemaphoreType.DMA)


def _xfer_copies(x_ref, o_ref, send_sems, recv_sems, local_sem, scatter):
    xi, yi, ci = lax.axis_index("x"), lax.axis_index("y"), lax.axis_index("c")
    me = 4 * xi + 2 * yi + ci

    def src(p):
        return x_ref.at[p] if scatter else x_ref

    own = pltpu.make_async_copy(src(me), o_ref.at[me], local_sem)
    sends, recvs = [], []
    for k in range(1, N_DEV):
        px, py, pc = xi ^ (k >> 2), yi ^ ((k >> 1) & 1), ci ^ (k & 1)
        p = 4 * px + 2 * py + pc
        sends.append(pltpu.make_async_remote_copy(
            src_ref=src(p), dst_ref=o_ref.at[me], send_sem=send_sems.at[k - 1], recv_sem=recv_sems.at[k - 1],
            device_id=(px, py, pc), device_id_type=pl.DeviceIdType.MESH))
        recvs.append(pltpu.make_async_remote_copy(
            src_ref=src(p), dst_ref=o_ref.at[p], send_sem=send_sems.at[k - 1], recv_sem=recv_sems.at[k - 1],
            device_id=(px, py, pc), device_id_type=pl.DeviceIdType.MESH))
    return own, sends, recvs


def _xfer_start(*refs, scatter):
    own, sends, _ = _xfer_copies(*refs, scatter)
    own.start()
    for cp in sends:
        cp.start()


def _xfer_wait(*refs, scatter):
    own, sends, recvs = _xfer_copies(*refs, scatter)
    for cp in recvs:
        cp.wait_recv()
    for cp in sends:
        cp.wait_send()
    own.wait()


def _xfer_out(x, scatter):
    return jax.ShapeDtypeStruct((N_DEV,) + (x.shape[1:] if scatter else x.shape), x.dtype)


def _pcall(body, args, *, grid, in_specs, out_specs, out_shape, scratch_shapes=(), sem, name, xfers=(), prefetch=()):
    out_specs, out_shape = list(out_specs), list(out_shape)
    n_pre, n_in, n_out, n_x, n_scr = len(prefetch), len(in_specs), len(out_specs), len(xfers), len(scratch_shapes)
    flags = [s for _, s in xfers]

    def wrapped(*refs):
        pre, refs = refs[:n_pre], refs[n_pre:]
        ins, xin = refs[:n_in], refs[n_in:n_in + n_x]
        outs = refs[n_in + n_x:n_in + n_x + n_out]
        xout = refs[n_in + n_x + n_out:n_in + 2 * n_x + n_out]
        scr = refs[n_in + 2 * n_x + n_out:]
        own, sems = scr[:n_scr], scr[n_scr:]
        ids = [pl.program_id(d) for d in range(len(grid))]
        first = functools.reduce(jnp.logical_and, [i == 0 for i in ids])
        last = functools.reduce(jnp.logical_and, [i == g - 1 for i, g in zip(ids, grid)])

        @pl.when(first)
        def _():
            for t in range(n_x):
                _xfer_start(xin[t], xout[t], *sems[3 * t:3 * t + 3], scatter=flags[t])

        body(*pre, *ins, *outs, *own)

        @pl.when(last)
        def _():
            for t in range(n_x):
                _xfer_wait(xin[t], xout[t], *sems[3 * t:3 * t + 3], scatter=flags[t])

    grid_spec = pltpu.PrefetchScalarGridSpec(
        num_scalar_prefetch=n_pre, grid=grid, in_specs=list(in_specs) + [_ANY] * n_x,
        out_specs=out_specs + [_ANY] * n_x, scratch_shapes=list(scratch_shapes) + list(_XFER_SEMS) * n_x)
    return pl.pallas_call(
        wrapped if xfers else body, grid_spec=grid_spec, out_shape=out_shape + [_xfer_out(x, s) for x, s in xfers],
        compiler_params=_params(*(["arbitrary"] * len(grid) if xfers else sem)), name=name,
    )(*prefetch, *args, *[x for x, _ in xfers])


def _exchange(xfers, *, name):
    def body():
        pass

    return _pcall(body, (), grid=(1,), in_specs=[], out_specs=[], out_shape=[], sem=("arbitrary",), name=name,
                  xfers=xfers)


def _mm(a, b, *, ak="mk", bk="kn", ok="mn", add=None, out_dtype=F32, tm=1024, tn=1024, tk=1024, name, xfers=()):
    sa, sb = a.shape, b.shape
    fm = fn = fk = None
    if ak == "mk":
        M, K, a_c = sa[0], sa[1], 1
    elif ak == "km":
        K, M, a_c = sa[0], sa[1], 0
    elif ak == "bmk":
        M, K, a_c, fk = sa[1], sa[0] * sa[2], 1, sa[2]
    else:
        K, M, a_c, fm = sa[1], sa[0] * sa[2], 0, sa[2]
    if bk == "kn":
        N, b_c = sb[1], 0
    elif bk == "nk":
        N, b_c = sb[0], 1
    elif bk == "bkn":
        N, b_c, fn = sb[0] * sb[2], 0, sb[2]
    elif bk == "bnk":
        N, b_c, fk = sb[1], 1, sb[2]
    elif bk == "kbn":
        N, b_c, fk = sb[2], 0, sb[1]
    else:
        N, b_c, fn = sb[0] * sb[1], 1, sb[1]
    tm, tn, tk = fm or _tile(M, tm), fn or _tile(N, tn), fk or _tile(K, tk)
    nm, nn, nk = M // tm, N // tn, K // tk

    a_spec = {"mk": pl.BlockSpec((tm, tk), lambda i, j, k: (i, k)),
              "km": pl.BlockSpec((tk, tm), lambda i, j, k: (k, i)),
              "bmk": pl.BlockSpec((None, tm, tk), lambda i, j, k: (k, i, 0)),
              "bkm": pl.BlockSpec((None, tk, tm), lambda i, j, k: (i, k, 0))}[ak]
    b_spec = {"kn": pl.BlockSpec((tk, tn), lambda i, j, k: (k, j)),
              "nk": pl.BlockSpec((tn, tk), lambda i, j, k: (j, k)),
              "bkn": pl.BlockSpec((None, tk, tn), lambda i, j, k: (j, k, 0)),
              "bnk": pl.BlockSpec((None, tn, tk), lambda i, j, k: (k, j, 0)),
              "kbn": pl.BlockSpec((None, tk, tn), lambda i, j, k: (k, 0, j)),
              "nbk": pl.BlockSpec((None, tn, tk), lambda i, j, k: (j, 0, k))}[bk]
    if ok == "mn":
        o_spec = pl.BlockSpec((tm, tn), lambda i, j, k: (i, j))
        out_shape = jax.ShapeDtypeStruct((M, N), out_dtype)
    elif ok == "bmn":
        o_spec = pl.BlockSpec((None, tm, tn), lambda i, j, k: (j, i, 0))
        out_shape = jax.ShapeDtypeStruct((nn, M, tn), out_dtype)
    else:
        o_spec = pl.BlockSpec((None, tm, tn), lambda i, j, k: (i, 0, j))
        out_shape = jax.ShapeDtypeStruct((nm, tm, N), out_dtype)
    dims = (((a_c,), (b_c,)), ((), ()))
    has_add = add is not None

    def body(*refs):
        a_ref, b_ref = refs[0], refs[1]
        add_ref = refs[2] if has_add else None
        o_ref = refs[3] if has_add else refs[2]
        part = lax.dot_general(a_ref[...].astype(BF16), b_ref[...].astype(BF16), dims, preferred_element_type=F32)

        def finish(r):
            if has_add:
                r = r + add_ref[...]
            o_ref[...] = r.astype(out_dtype)

        if nk == 1:
            finish(part)
            return
        acc = refs[-1]
        k = pl.program_id(2)

        @pl.when(k == 0)
        def _():
            acc[...] = part

        @pl.when(k > 0)
        def _():
            acc[...] += part

        @pl.when(k == nk - 1)
        def _():
            finish(acc[...])

    in_specs = [a_spec, b_spec]
    args = [a, b]
    if has_add:
        in_specs.append(pl.BlockSpec((tm, tn), lambda i, j, k: (i, j)))
        args.append(add)
    res = _pcall(body, args, grid=(nm, nn, nk), in_specs=in_specs, out_specs=[o_spec], out_shape=[out_shape],
                 scratch_shapes=[pltpu.VMEM((tm, tn), F32)] if nk > 1 else [],
                 sem=("parallel", "parallel", "arbitrary"), name=name, xfers=xfers)
    return res if xfers else res[0]


def _rms_fwd(h, g, *, name):
    L, D = h.shape
    tr = _tile(L, ROW_TILE, SUBLANES)

    def body(h_ref, g_ref, o_ref):
        x = h_ref[...]
        r = lax.rsqrt(jnp.mean(x * x, axis=1, keepdims=True) + EPS)
        o_ref[...] = (x * r * g_ref[...]).astype(BF16)

    return pl.pallas_call(
        body, grid=(L // tr,),
        in_specs=[pl.BlockSpec((tr, D), lambda i: (i, 0)), pl.BlockSpec((1, D), lambda i: (0, 0))],
        out_specs=pl.BlockSpec((tr, D), lambda i: (i, 0)), out_shape=jax.ShapeDtypeStruct((L, D), BF16),
        compiler_params=_params("parallel"), name=name)(h, g)


def _rms_bwd(h, g, dy, dres, *, name):
    L, D = h.shape
    tr = _tile(L, ROW_TILE, SUBLANES)

    def body(h_ref, g_ref, dy_ref, dres_ref, dh_ref, dg_ref):
        @pl.when(pl.program_id(0) == 0)
        def _():
            dg_ref[...] = jnp.zeros_like(dg_ref)

        x = h_ref[...]
        r = lax.rsqrt(jnp.mean(x * x, axis=1, keepdims=True) + EPS)
        xn = x * r
        dy = dy_ref[...].astype(F32)
        gdy = dy * g_ref[...]
        dx = r * (gdy - xn * jnp.mean(gdy * xn, axis=1, keepdims=True))
        dh_ref[...] = dres_ref[...] + dx
        dg_ref[...] += jnp.sum(dy * xn, axis=0, keepdims=True)

    row = pl.BlockSpec((tr, D), lambda i: (i, 0))
    vec = pl.BlockSpec((1, D), lambda i: (0, 0))
    return pl.pallas_call(
        body, grid=(L // tr,), in_specs=[row, vec, row, row], out_specs=[row, vec],
        out_shape=[jax.ShapeDtypeStruct((L, D), F32), jax.ShapeDtypeStruct((1, D), F32)],
        compiler_params=_params("arbitrary"), name=name)(h, g, dy, dres)


def _glu_res_rms(z, h, g, *, name):
    L, D = h.shape
    nb, cb = z.shape[1], z.shape[3]
    tr = _tile(L, ROW_TILE, SUBLANES)

    def body(z_ref, h_ref, g_ref, h1_ref, hn_ref):
        za = jnp.concatenate([z_ref[0, d] for d in range(nb)], axis=1)
        zg = jnp.concatenate([z_ref[1, d] for d in range(nb)], axis=1)
        x = h_ref[...] + za * jax.nn.sigmoid(zg)
        h1_ref[...] = x
        r = lax.rsqrt(jnp.mean(x * x, axis=1, keepdims=True) + EPS)
        hn_ref[...] = (x * r * g_ref[...]).astype(BF16)

    row = pl.BlockSpec((tr, D), lambda i: (i, 0))
    return pl.pallas_call(
        body, grid=(L // tr,),
        in_specs=[pl.BlockSpec((2, nb, tr, cb), lambda i: (0, 0, i, 0)), row, pl.BlockSpec((1, D), lambda i: (0, 0))],
        out_specs=[row, row],
        out_shape=[jax.ShapeDtypeStruct((L, D), F32), jax.ShapeDtypeStruct((L, D), BF16)],
        compiler_params=_params("parallel"), name=name)(z, h, g)


def _glu_bwd(z, dout, *, name):
    L, D = dout.shape
    nb, cb = z.shape[1], z.shape[3]
    tr = _tile(L, ROW_TILE, SUBLANES)

    def body(z_ref, d_ref, o_ref):
        for d in range(nb):
            dd = d_ref[:, d * cb:(d + 1) * cb]
            sg = jax.nn.sigmoid(z_ref[1, d])
            o_ref[0, d] = (dd * sg).astype(BF16)
            o_ref[1, d] = (dd * z_ref[0, d] * sg * (1.0 - sg)).astype(BF16)

    zs = pl.BlockSpec((2, nb, tr, cb), lambda i: (0, 0, i, 0))
    return pl.pallas_call(
        body, grid=(L // tr,), in_specs=[zs, pl.BlockSpec((tr, D), lambda i: (i, 0))], out_specs=zs,
        out_shape=jax.ShapeDtypeStruct(z.shape, BF16),
        compiler_params=_params("parallel"), name=name)(z, dout)


def _loss_head(h, g, target, *, name):
    L, D = h.shape
    tr = _tile(L, ROW_TILE, SUBLANES)

    def body(h_ref, g_ref, t_ref, loss_ref, dh_ref, dg_ref):
        @pl.when(pl.program_id(0) == 0)
        def _():
            dg_ref[...] = jnp.zeros_like(dg_ref)
            loss_ref[...] = jnp.zeros_like(loss_ref)

        x = h_ref[...]
        gg = g_ref[...]
        r = lax.rsqrt(jnp.mean(x * x, axis=1, keepdims=True) + EPS)
        xn = x * r
        err = xn * gg - t_ref[...]
        loss_ref[...] += 0.5 * jnp.sum(jnp.mean(err * err, axis=1, keepdims=True), axis=0, keepdims=True)
        dy = err * (1.0 / D)
        gdy = dy * gg
        dh_ref[...] = r * (gdy - xn * jnp.mean(gdy * xn, axis=1, keepdims=True))
        dg_ref[...] += jnp.sum(dy * xn, axis=0, keepdims=True)

    row = pl.BlockSpec((tr, D), lambda i: (i, 0))
    vec = pl.BlockSpec((1, D), lambda i: (0, 0))
    return pl.pallas_call(
        body, grid=(L // tr,), in_specs=[row, vec, row],
        out_specs=[pl.BlockSpec((1, 1), lambda i: (0, 0)), row, vec],
        out_shape=[jax.ShapeDtypeStruct((1, 1), F32), jax.ShapeDtypeStruct((L, D), F32),
                   jax.ShapeDtypeStruct((1, D), F32)],
        compiler_params=_params("arbitrary"), name=name)(h, g, target)


CONV_ROW_TILE = 256


def _conv_specs(L, tr, tc):
    nrb = tr // SUBLANES
    before = lambda i: jnp.maximum(i * nrb - 1, 0)
    after = lambda i: jnp.minimum((i + 1) * nrb, L // SUBLANES - 1)
    main = pl.BlockSpec((2, None, tr, tc), lambda j, i: (0, j, i, 0))
    prev = pl.BlockSpec((2, None, SUBLANES, tc), lambda j, i: (0, j, before(i), 0))
    nxt = pl.BlockSpec((2, None, SUBLANES, tc), lambda j, i: (0, j, after(i), 0))
    cw = pl.BlockSpec((2, None, 3, tc), lambda j, i: (0, j, 0, 0))
    cb = pl.BlockSpec((2, None, 1, tc), lambda j, i: (0, j, 0, 0))
    half = pl.BlockSpec((None, tr, tc), lambda j, i: (j, i, 0))
    half_nxt = pl.BlockSpec((None, SUBLANES, tc), lambda j, i: (j, after(i), 0))
    return main, prev, nxt, cw, cb, half, half_nxt


def _conv_rows(xe, w, b):
    x1 = pltpu.roll(xe, 1, 0)
    x2 = pltpu.roll(xe, 2, 0)
    return b + x2 * w[0:1] + x1 * w[1:2] + xe * w[2:3], x1, x2


def _conv_act(u0, cw, cb, *, name, xfers=()):
    _, nb, L, tc = u0.shape
    tr = _tile(L, CONV_ROW_TILE, SUBLANES)
    main, prev, _, cws, cbs, half, _ = _conv_specs(L, tr, tc)

    def body(u_ref, p_ref, w_ref, b_ref, a_ref):
        first = pl.program_id(1) == 0
        y = []
        for s in range(2):
            xe = jnp.concatenate([jnp.where(first, 0.0, p_ref[s]), u_ref[s]], axis=0)
            y.append(_conv_rows(xe, w_ref[s], b_ref[s])[0][SUBLANES:])
        a_ref[...] = (y[0] * jax.nn.sigmoid(y[0]) * y[1]).astype(BF16)

    return _pcall(body, (u0, u0, cw, cb), grid=(nb, L // tr), in_specs=[main, prev, cws, cbs], out_specs=[half],
                  out_shape=[jax.ShapeDtypeStruct((nb, L, tc), BF16)], sem=("parallel", "parallel"), name=name,
                  xfers=xfers)


def _conv_ffn_bwd(u0, cw, cb, da, *, name, xfers=()):
    _, nb, L, tc = u0.shape
    tr = _tile(L, CONV_ROW_TILE, SUBLANES)
    main, prev, nxt, cws, cbs, half, half_nxt = _conv_specs(L, tr, tc)
    nr = L // tr
    H = SUBLANES

    def body(u_ref, p_ref, n_ref, w_ref, b_ref, da_ref, dan_ref, a_ref, du0_ref, dcw_ref, dcb_ref):
        i = pl.program_id(1)

        @pl.when(i == 0)
        def _():
            dcw_ref[...] = jnp.zeros_like(dcw_ref)
            dcb_ref[...] = jnp.zeros_like(dcb_ref)

        y, x1, x2 = [], [], []
        for s in range(2):
            xe = jnp.concatenate([jnp.where(i == 0, 0.0, p_ref[s]), u_ref[s], n_ref[s]], axis=0)
            ys, x1s, x2s = _conv_rows(xe, w_ref[s], b_ref[s])
            y.append(ys[H:])
            x1.append(x1s[H:H + tr])
            x2.append(x2s[H:H + tr])
        gate, up = y
        da = jnp.concatenate([da_ref[...], dan_ref[...]], axis=0)
        row = lax.broadcasted_iota(jnp.int32, (tr + H, tc), 0)
        da = jnp.where(jnp.logical_and(i == nr - 1, row >= tr), 0.0, da)
        sg = jax.nn.sigmoid(gate)
        silu = gate * sg
        a_ref[...] = (silu * up)[:tr].astype(BF16)
        d = (da * up * (sg * (1.0 + gate * (1.0 - sg))), da * silu)
        for s in range(2):
            w = w_ref[s]
            d0 = d[s][:tr]
            d1 = pltpu.roll(d[s], tr + H - 1, 0)[:tr]
            d2 = pltpu.roll(d[s], tr + H - 2, 0)[:tr]
            du0_ref[s] = (d0 * w[2:3] + d1 * w[1:2] + d2 * w[0:1]).astype(BF16)
            dcw_ref[s, 0:1, :] += jnp.sum(d0 * x2[s], axis=0, keepdims=True)
            dcw_ref[s, 1:2, :] += jnp.sum(d0 * x1[s], axis=0, keepdims=True)
            dcw_ref[s, 2:3, :] += jnp.sum(d0 * u_ref[s], axis=0, keepdims=True)
            dcb_ref[s] += jnp.sum(d0, axis=0, keepdims=True)

    return _pcall(body, (u0, u0, u0, cw, cb, da, da), grid=(nb, nr),
                  in_specs=[main, prev, nxt, cws, cbs, half, half_nxt], out_specs=[half, main, cws, cbs],
                  out_shape=[jax.ShapeDtypeStruct((nb, L, tc), BF16), jax.ShapeDtypeStruct((2, nb, L, tc), BF16),
                             jax.ShapeDtypeStruct((2, nb, 3, tc), F32), jax.ShapeDtypeStruct((2, nb, 1, tc), F32)],
                  sem=("parallel", "arbitrary"), name=name, xfers=xfers)


N_TILES = 64
HALF = N_TILES // 2


def _swap(s):
    return jnp.concatenate([s[HALF:], s[:HALF]], axis=0)


def _chan_block(j):
    return ((j % HALF) // 4) * LANES


def _tiles_of(jb):
    return [4 * jb + i for i in range(4)] + [HALF + 4 * jb + i for i in range(4)]
GELU_C = math.sqrt(2.0 / math.pi)
GELU_A = 0.044715


def _gelu(x):
    return 0.5 * x * (1.0 + jnp.tanh(GELU_C * (x + GELU_A * x * x * x)))


def _gelu_grad(x):
    th = jnp.tanh(GELU_C * (x + GELU_A * x * x * x))
    return 0.5 * (1.0 + th) + 0.5 * x * (1.0 - th * th) * GELU_C * (1.0 + 3.0 * GELU_A * x * x)


def _s5_project_in(u_ref, wb_ref, s3, T, TP):
    for j in range(N_TILES):
        blk = _chan_block(j)
        s3[pl.ds(j * TP + SUBLANES, T), :] = jnp.dot(u_ref[:, blk:blk + LANES], wb_ref[j],
                                                     preferred_element_type=F32)


def _s5_scan_fwd(s3, a1, a2, s0, T, TP):
    span = (N_GROUPS - 1) * TP + 2 * SUBLANES

    def blk(i, s):
        view = s3.at[pl.ds(pl.multiple_of(i * SUBLANES, SUBLANES), span)]
        for k in range(SUBLANES):
            rows = pl.ds(SUBLANES + k, N_GROUPS, stride=TP)
            s = a1 * s + a2 * _swap(s) + view[rows, :]
            view[rows, :] = s
        return s

    return lax.fori_loop(0, T // SUBLANES, blk, s0)


def _s5_fwd(hn, wb, wc, a1, a2, dvec, *, name, xfers=()):
    L, D = hn.shape
    T = min(S5_CHUNK, L)
    TP = T + SUBLANES
    nC = L // T

    def body(u_ref, wb_ref, wc_ref, a1_ref, a2_ref, d_ref, y_ref, yg_ref, sb_ref, s3, st):
        @pl.when(pl.program_id(0) == 0)
        def _():
            st[...] = jnp.zeros_like(st)

        sb_ref[0] = st[...]
        _s5_project_in(u_ref, wb_ref, s3, T, TP)
        st[...] = _s5_scan_fwd(s3, a1_ref[...], a2_ref[...], st[...], T, TP)
        for jb in range(D // LANES):
            acc = jnp.zeros((T, LANES), F32)
            for j in _tiles_of(jb):
                acc += jnp.dot(s3[pl.ds(j * TP + SUBLANES, T), :].astype(BF16), wc_ref[j],
                               preferred_element_type=F32)
            cols = slice(jb * LANES, (jb + 1) * LANES)
            y = acc + d_ref[:, cols] * u_ref[:, cols].astype(F32)
            y_ref[:, cols] = y
            yg_ref[:, cols] = _gelu(y).astype(BF16)

    row = pl.BlockSpec((T, D), lambda c: (c, 0))
    wspec = pl.BlockSpec((N_GROUPS, LANES, LANES), lambda c: (0, 0, 0))
    aspec = pl.BlockSpec((N_GROUPS, LANES), lambda c: (0, 0))
    return _pcall(
        body, (hn, wb, wc, a1, a2, dvec), grid=(nC,),
        in_specs=[row, wspec, wspec, aspec, aspec, pl.BlockSpec((1, D), lambda c: (0, 0))],
        out_specs=[row, row, pl.BlockSpec((1, N_GROUPS, LANES), lambda c: (c, 0, 0))],
        out_shape=[jax.ShapeDtypeStruct((L, D), F32), jax.ShapeDtypeStruct((L, D), BF16),
                   jax.ShapeDtypeStruct((nC, N_GROUPS, LANES), F32)],
        scratch_shapes=[pltpu.VMEM((N_GROUPS * TP, LANES), F32), pltpu.VMEM((N_GROUPS, LANES), F32)],
        sem=("arbitrary",), name=name, xfers=xfers)


def _s5_bwd(hn, dyg, ypre, sbound, wb, wc, a1, a2, dvec, *, name, xfers=()):
    L, D = hn.shape
    T = min(S5_CHUNK, L)
    TP = T + SUBLANES
    nC = L // T
    span = (N_GROUPS - 1) * TP + 2 * SUBLANES
    NT = (((1,), (1,)), ((), ()))
    TN = (((0,), (0,)), ((), ()))

    def body(u_ref, dyg_ref, yp_ref, sb_ref, wb_ref, wc_ref, a1_ref, a2_ref, d_ref,
             du_ref, dwb_ref, dwc_ref, da1_ref, da2_ref, dd_ref, s3, g3, gst, dy_s):
        @pl.when(pl.program_id(0) == 0)
        def _():
            gst[...] = jnp.zeros_like(gst)
            dwb_ref[...] = jnp.zeros_like(dwb_ref)
            dwc_ref[...] = jnp.zeros_like(dwc_ref)
            da1_ref[...] = jnp.zeros_like(da1_ref)
            da2_ref[...] = jnp.zeros_like(da2_ref)
            dd_ref[...] = jnp.zeros_like(dd_ref)

        a1 = a1_ref[...]
        a2 = a2_ref[...]
        dy = dyg_ref[...].astype(F32) * _gelu_grad(yp_ref[...])
        dy_s[...] = dy.astype(BF16)
        dd_ref[...] += jnp.sum(dy * u_ref[...].astype(F32), axis=0, keepdims=True)
        du_ref[...] = d_ref[...] * dy

        s3[pl.ds(SUBLANES - 1, N_GROUPS, stride=TP), :] = sb_ref[0]
        _s5_project_in(u_ref, wb_ref, s3, T, TP)
        _s5_scan_fwd(s3, a1, a2, sb_ref[0], T, TP)

        for j in range(N_TILES):
            blk = _chan_block(j)
            g3[pl.ds(j * TP + SUBLANES, T), :] = lax.dot_general(dy_s[:, blk:blk + LANES], wc_ref[j], NT,
                                                                 preferred_element_type=F32)
        a2c = -a2

        def rblk(ii, carry):
            g, acc1, acc2 = carry
            t0 = pl.multiple_of((T // SUBLANES - 1 - ii) * SUBLANES, SUBLANES)
            gv = g3.at[pl.ds(t0, span)]
            sv = s3.at[pl.ds(t0, span)]
            for k in reversed(range(SUBLANES)):
                rows = pl.ds(SUBLANES + k, N_GROUPS, stride=TP)
                g = a1 * g + a2c * _swap(g) + gv[rows, :]
                gv[rows, :] = g
                sp = sv[pl.ds(SUBLANES - 1 + k, N_GROUPS, stride=TP), :]
                acc1 = acc1 + g * sp
                acc2 = acc2 + g * _swap(sp)
            return g, acc1, acc2

        zero = jnp.zeros((N_GROUPS, LANES), F32)
        g, acc1, acc2 = lax.fori_loop(0, T // SUBLANES, rblk, (gst[...], zero, zero))
        gst[...] = g
        da1_ref[...] += acc1
        da2_ref[...] += acc2

        for jb in range(D // LANES):
            cols = slice(jb * LANES, (jb + 1) * LANES)
            acc = jnp.zeros((T, LANES), F32)
            for j in _tiles_of(jb):
                rows = pl.ds(j * TP + SUBLANES, T)
                gj = g3[rows, :].astype(BF16)
                dwc_ref[j] += lax.dot_general(s3[rows, :].astype(BF16), dy_s[:, cols], TN,
                                              preferred_element_type=F32)
                dwb_ref[j] += lax.dot_general(u_ref[:, cols], gj, TN, preferred_element_type=F32)
                acc += lax.dot_general(gj, wb_ref[j], NT, preferred_element_type=F32)
            du_ref[:, cols] += acc

    rrow = pl.BlockSpec((T, D), lambda c: (nC - 1 - c, 0))
    wspec = pl.BlockSpec((N_GROUPS, LANES, LANES), lambda c: (0, 0, 0))
    aspec = pl.BlockSpec((N_GROUPS, LANES), lambda c: (0, 0))
    vec = pl.BlockSpec((1, D), lambda c: (0, 0))
    return _pcall(
        body, (hn, dyg, ypre, sbound, wb, wc, a1, a2, dvec), grid=(nC,),
        in_specs=[rrow, rrow, rrow, pl.BlockSpec((1, N_GROUPS, LANES), lambda c: (nC - 1 - c, 0, 0)),
                  wspec, wspec, aspec, aspec, vec],
        out_specs=[rrow, wspec, wspec, aspec, aspec, vec],
        out_shape=[jax.ShapeDtypeStruct((L, D), F32),
                   jax.ShapeDtypeStruct((N_GROUPS, LANES, LANES), F32),
                   jax.ShapeDtypeStruct((N_GROUPS, LANES, LANES), F32),
                   jax.ShapeDtypeStruct((N_GROUPS, LANES), F32), jax.ShapeDtypeStruct((N_GROUPS, LANES), F32),
                   jax.ShapeDtypeStruct((1, D), F32)],
        scratch_shapes=[pltpu.VMEM((N_GROUPS * TP, LANES), F32), pltpu.VMEM((N_GROUPS * TP, LANES), F32),
                        pltpu.VMEM((N_GROUPS, LANES), F32), pltpu.VMEM((T, D), BF16)],
        sem=("arbitrary",), name=name, xfers=xfers)


def _s5_prep(lam_re, lam_im, log_dt, b_re, b_im, c_re, c_im):
    dt = jnp.exp(log_dt)[:, None]
    mag = jnp.exp(lam_re * dt)
    lb_re = mag * jnp.cos(lam_im * dt)
    lb_im = mag * jnp.sin(lam_im * dt)
    den = lam_re * lam_re + lam_im * lam_im
    nr = lb_re - 1.0
    fr = ((nr * lam_re + lb_im * lam_im) / den)[..., None]
    fi = ((lb_im * lam_re - nr * lam_im) / den)[..., None]
    bb_re = fr * b_re - fi * b_im
    bb_im = fr * b_im + fi * b_re
    pair = lambda a: a.reshape(HALF, 2 * SSM_STATE)
    a1 = jnp.concatenate([pair(lb_re), pair(lb_re)], axis=0)
    a2 = jnp.concatenate([-pair(lb_im), pair(lb_im)], axis=0)
    sel = jax.nn.one_hot(jnp.arange(HALF) % 4, 4, dtype=F32)
    eye = jnp.eye(2, dtype=F32)

    def w_in(bb):
        return jnp.einsum('jk,ef,jfph->jkehfp', sel, eye, bb.reshape(HALF, 2, SSM_STATE, SSM_GROUP)
                          ).reshape(HALF, LANES, LANES)

    def w_out(c):
        return jnp.einsum('jk,ef,jfhp->jepkfh', sel, eye, c.reshape(HALF, 2, SSM_GROUP, SSM_STATE)
                          ).reshape(HALF, LANES, LANES)

    wb = jnp.concatenate([w_in(bb_re), w_in(bb_im)], axis=0)
    wc = jnp.concatenate([w_out(c_re), w_out(-c_im)], axis=0)
    return a1, a2, wb, wc


def _tri(n, upper):
    r = lax.broadcasted_iota(jnp.int32, (n, n), 0)
    c = lax.broadcasted_iota(jnp.int32, (n, n), 1)
    return ((r <= c) if upper else (r >= c)).astype(F32)


def _fgate_fwd(fl, bf, *, name):
    L, W = fl.shape
    tr = _tile(L, CUM_TILE, SUBLANES)

    def body(f_ref, b_ref, o_ref, carry):
        @pl.when(pl.program_id(0) == 0)
        def _():
            carry[...] = jnp.zeros_like(carry)

        x = f_ref[...] + b_ref[...]
        ls = jnp.minimum(x, 0.0) - jnp.log(1.0 + jnp.exp(-jnp.abs(x)))
        cum = jnp.dot(_tri(tr, False), ls, preferred_element_type=F32, precision=lax.Precision.HIGHEST) + carry[...]
        o_ref[...] = cum
        carry[...] = cum[tr - 1:tr, :]

    return pl.pallas_call(
        body, grid=(L // tr,),
        in_specs=[pl.BlockSpec((tr, W), lambda i: (i, 0)), pl.BlockSpec((1, W), lambda i: (0, 0))],
        out_specs=pl.BlockSpec((tr, W), lambda i: (i, 0)), out_shape=jax.ShapeDtypeStruct((L, W), F32),
        scratch_shapes=[pltpu.VMEM((1, W), F32)], compiler_params=_params("arbitrary"), name=name)(fl, bf)


def _fgate_bwd(fl, bf, dcum, *, name):
    L, W = fl.shape
    tr = _tile(L, CUM_TILE, SUBLANES)
    n = L // tr

    def body(f_ref, b_ref, d_ref, o_ref, db_ref, carry):
        @pl.when(pl.program_id(0) == 0)
        def _():
            carry[...] = jnp.zeros_like(carry)
            db_ref[...] = jnp.zeros_like(db_ref)

        d = d_ref[...]
        rev = jnp.dot(_tri(tr, True), d, preferred_element_type=F32, precision=lax.Precision.HIGHEST) + carry[...]
        carry[...] += jnp.sum(d, axis=0, keepdims=True)
        df = rev * jax.nn.sigmoid(-(f_ref[...] + b_ref[...]))
        o_ref[...] = df
        db_ref[...] += jnp.sum(df, axis=0, keepdims=True)

    rrow = pl.BlockSpec((tr, W), lambda i: (n - 1 - i, 0))
    vec = pl.BlockSpec((1, W), lambda i: (0, 0))
    return pl.pallas_call(
        body, grid=(n,), in_specs=[rrow, vec, rrow], out_specs=[rrow, vec],
        out_shape=[jax.ShapeDtypeStruct((L, W), F32), jax.ShapeDtypeStruct((1, W), F32)],
        scratch_shapes=[pltpu.VMEM((1, W), F32)], compiler_params=_params("arbitrary"), name=name)(fl, bf, dcum)


_NT = (((1,), (1,)), ((), ()))
_TN = (((0,), (0,)), ((), ()))
HEAD_PAIRS = N_HEADS // 2


def _causal_tiles(n, by_row):
    pairs = ([(i, j) for i in range(n) for j in range(i + 1)] if by_row
             else [(i, j) for j in range(n) for i in range(j, n)])
    return (jnp.array([p[0] for p in pairs], jnp.int32), jnp.array([p[1] for p in pairs], jnp.int32))


def _attn_logits(qs, k, ck, masked, t):
    s = lax.dot_general(qs, k, _NT, preferred_element_type=F32) - ck
    if masked:
        r = lax.broadcasted_iota(jnp.int32, (t, t), 0)
        c = lax.broadcasted_iota(jnp.int32, (t, t), 1)
        s = jnp.where(c > r, NEG, s)
    return s


def _attn_fwd(q, kv, ck, *, name, xfers=()):
    L, D = q.shape
    t = _tile(L, ATTN_TILE)
    n = L // t
    dh = HEAD_DIM

    def body(i_tab, j_tab, q_ref, k_ref, v_ref, ck_ref, o_ref, o32_ref, lse_ref, m_s, l_s, acc):
        i, j = i_tab[pl.program_id(1)], j_tab[pl.program_id(1)]

        @pl.when(j == 0)
        def _():
            m_s[...] = jnp.full_like(m_s, NEG)
            l_s[...] = jnp.zeros_like(l_s)
            acc[...] = jnp.zeros_like(acc)

        def tile(masked):
            for e in range(2):
                sl = slice(e * dh, (e + 1) * dh)
                v = v_ref[:, sl]
                s = _attn_logits(q_ref[:, sl] * ATTN_SCALE, k_ref[:, sl], ck_ref[e], masked, t)
                m_new = jnp.maximum(m_s[e], jnp.max(s, axis=1, keepdims=True))
                alpha = jnp.exp(m_s[e] - m_new)
                p = jnp.exp(s - m_new)
                l_s[e] = alpha * l_s[e] + jnp.sum(p, axis=1, keepdims=True)
                p_hi = p.astype(BF16)
                p_lo = (p - p_hi.astype(F32)).astype(BF16)
                pv = (jnp.dot(p_hi, v, preferred_element_type=F32) + jnp.dot(p_lo, v, preferred_element_type=F32))
                acc[e] = alpha * acc[e] + pv
                m_s[e] = m_new

        pl.when(j < i)(functools.partial(tile, False))
        pl.when(j == i)(functools.partial(tile, True))

        @pl.when(j == i)
        def _():
            for e in range(2):
                sl = slice(e * dh, (e + 1) * dh)
                o = acc[e] / l_s[e]
                o_ref[:, sl] = o.astype(BF16)
                o32_ref[:, sl] = o
                lse_ref[e] = m_s[e] + jnp.log(l_s[e])

    qs = pl.BlockSpec((t, LANES), lambda h, s, it, jt: (it[s], h))
    ks = pl.BlockSpec((t, LANES), lambda h, s, it, jt: (jt[s], h))
    vs = pl.BlockSpec((t, LANES), lambda h, s, it, jt: (jt[s], HEAD_PAIRS + h))
    cs = pl.BlockSpec((2, 1, t), lambda h, s, it, jt: (h, 0, jt[s]))
    tabs = _causal_tiles(n, by_row=True)
    return _pcall(
        body, (q, kv, kv, ck), grid=(HEAD_PAIRS, tabs[0].shape[0]), in_specs=[qs, ks, vs, cs],
        out_specs=[qs, qs, pl.BlockSpec((2, t, 1), lambda h, s, it, jt: (h, it[s], 0))],
        out_shape=[jax.ShapeDtypeStruct((L, D), BF16), jax.ShapeDtypeStruct((L, D), F32),
                   jax.ShapeDtypeStruct((N_HEADS, L, 1), F32)],
        scratch_shapes=[pltpu.VMEM((2, t, 1), F32), pltpu.VMEM((2, t, 1), F32), pltpu.VMEM((2, t, dh), F32)],
        sem=("parallel", "arbitrary"), name=name, xfers=xfers, prefetch=tabs)


def _attn_bwd(q, kv, ck, o, do, lse, *, name, xfers=()):
    L, D = q.shape
    t = _tile(L, ATTN_TILE)
    n = L // t
    dh = HEAD_DIM

    def body(i_tab, j_tab, q_ref, k_ref, v_ref, ck_ref, o_ref, do_ref, lse_ref, dq_ref, dk_ref, dv_ref, dck_ref):
        i, j = i_tab[pl.program_id(1)], j_tab[pl.program_id(1)]

        @pl.when(pl.program_id(1) == 0)
        def _():
            dq_ref[...] = jnp.zeros_like(dq_ref)

        @pl.when(i == j)
        def _():
            dk_ref[...] = jnp.zeros_like(dk_ref)
            dv_ref[...] = jnp.zeros_like(dv_ref)
            dck_ref[...] = jnp.zeros_like(dck_ref)

        def tile(masked):
            rows = pl.ds(pl.multiple_of(i * t, t), t)
            for e in range(2):
                sl = slice(e * dh, (e + 1) * dh)
                qs = q_ref[:, sl] * ATTN_SCALE
                k = k_ref[:, sl]
                do = do_ref[:, sl]
                s = _attn_logits(qs, k, ck_ref[e], masked, t)
                p = jnp.exp(s - lse_ref[e])
                dp = lax.dot_general(do, v_ref[:, sl], _NT, preferred_element_type=F32)
                delta = jnp.sum(do.astype(F32) * o_ref[:, sl], axis=1, keepdims=True)
                ds = p * (dp - delta)
                ds16 = ds.astype(BF16)
                dv_ref[:, sl] += lax.dot_general(p.astype(BF16), do, _TN, preferred_element_type=F32)
                dk_ref[:, sl] += lax.dot_general(ds16, qs, _TN, preferred_element_type=F32)
                dck_ref[e] -= jnp.sum(ds, axis=0, keepdims=True)
                dq_ref[rows, sl] += jnp.dot(ds16, k, preferred_element_type=F32) * ATTN_SCALE

        pl.when(i > j)(functools.partial(tile, False))
        pl.when(i == j)(functools.partial(tile, True))

    qs = pl.BlockSpec((t, LANES), lambda h, s, it, jt: (it[s], h))
    ks = pl.BlockSpec((t, LANES), lambda h, s, it, jt: (jt[s], h))
    vs = pl.BlockSpec((t, LANES), lambda h, s, it, jt: (jt[s], HEAD_PAIRS + h))
    cs = pl.BlockSpec((2, 1, t), lambda h, s, it, jt: (h, 0, jt[s]))
    ls = pl.BlockSpec((2, t, 1), lambda h, s, it, jt: (h, it[s], 0))
    full = jax.ShapeDtypeStruct((L, D), F32)
    tabs = _causal_tiles(n, by_row=False)
    return _pcall(
        body, (q, kv, kv, ck, o, do, lse), grid=(HEAD_PAIRS, tabs[0].shape[0]), in_specs=[qs, ks, vs, cs, qs, qs, ls],
        out_specs=[pl.BlockSpec((L, LANES), lambda h, s, it, jt: (0, h)), ks, ks, cs],
        out_shape=[full, full, full, jax.ShapeDtypeStruct((N_HEADS, 1, L), F32)],
        sem=("parallel", "arbitrary"), name=name, xfers=xfers, prefetch=tabs)


SHARD_COLS_FFN = 2 * D_FF // N_DEV
SHARD_ROWS_FFN = D_FF // N_DEV
SHARD_COLS_GLU = 2 * D_MODEL // N_DEV
SHARD_ROWS_QO = D_MODEL // N_DEV
SHARD_COLS_KVF = (2 * D_MODEL + N_HEADS) // N_DEV
S5_NAMES = ("lam_re", "lam_im", "log_dt", "ssm_b_re", "ssm_b_im", "ssm_c_re", "ssm_c_im")
REPL_LATE_ROWS = 288
REPL_EARLY_ROWS = 320


def _leaves(parts):
    out = []
    for p in parts:
        out.extend(_leaves(p) if isinstance(p, (list, tuple)) else [p.reshape(-1)])
    return out


def _pack_rows(parts, rows):
    flat = jnp.concatenate(_leaves(parts))
    return jnp.pad(flat, (0, rows * D_MODEL - flat.shape[0])).reshape(rows, D_MODEL)


def _unpack_rows(flat, like):
    flat, out, off = flat.reshape(-1), [], 0
    for p in _leaves(like):
        out.append(flat[off:off + p.shape[0]])
        off += p.shape[0]
    return out


def _repl_late(d):
    return [d["g_mix"][0], [d[n][0] for n in S5_NAMES]]


def _repl_early(d):
    return [list(d["g_mix"][1:]), list(d["g_ffn"]), [d[n][1] for n in S5_NAMES], d["g_kv"], d["b_f"],
            list(d["ffn_conv_b"]), d["g_final"]]


def _kvf_blocks(full):
    return full.reshape(D_MODEL, N_DEV, SHARD_COLS_KVF).transpose(1, 0, 2)


class _Step:
    def __init__(self, weights, send=None, plan=None):
        self.w = dict(weights)
        self.send = send or {}
        self.grad = {}
        self.bcast = {}
        self.slots = {}
        self.plan = plan or {}

    def xfers(self, host):
        src = {"w": self.send, "g": self.grad, "b": self.bcast}
        return [(src[kind][k], kind == "g") for kind, k in self.plan.get(host, ())]

    def land(self, host, gathered):
        for (kind, k), g in zip(self.plan.get(host, ()), gathered):
            if kind == "w":
                self.arrive(k, g)
            else:
                self.slots[k] = g

    def arrive(self, k, g):
        name = k[0]
        if name == "w_ffn_in" and len(k) == 3:
            self.w[k] = g
            halves = [self.w.get((name, k[1], h)) for h in range(2)]
            if all(h is not None for h in halves):
                self.w[name, k[1]] = jnp.concatenate(halves, axis=1)
        elif name == "w_ffn_out":
            self.w[k] = g.reshape(4, 2 * SHARD_ROWS_FFN, D_MODEL)
        elif name in ("w_q", "w_o"):
            self.w[k] = g.reshape(D_MODEL, D_MODEL)
        elif name == "w_kvf":
            full = g.transpose(1, 0, 2).reshape(D_MODEL, N_DEV * SHARD_COLS_KVF)
            self.w["w_kv",] = full[:, :2 * D_MODEL]
            self.w["w_f",] = jnp.pad(full[:, 2 * D_MODEL:], ((0, 0), (0, LANES - N_HEADS)))
        elif name == "small":
            flat = g.reshape(N_DEV, -1)
            self.w["ssm_d",] = flat[:, :256].reshape(N_DEV, N_A, LANES).transpose(1, 0, 2).reshape(N_A, D_MODEL)
            cw = flat[:, 256:256 + DEPTH * 3 * SHARD_COLS_FFN].reshape(N_DEV, DEPTH, 3, SHARD_COLS_FFN)
            for layer in range(DEPTH):
                self.w["conv_w", layer] = cw[:, layer].reshape(2, 4, 3, SHARD_COLS_FFN)
        else:
            self.w[k] = g

    def run(self, host, fn, *args, **kw):
        xf = self.xfers(host)
        res = fn(*args, name=host, xfers=xf, **kw)
        if not xf:
            return res[0] if isinstance(res, (list, tuple)) and len(res) == 1 else res
        n_own = len(res) - len(xf)
        self.land(host, res[n_own:])
        return res[0] if n_own == 1 else res[:n_own]


def _step(x, target, S):
    L = x.shape[0]
    W = S.w
    vec = lambda a: a.reshape(1, -1)
    CF = SHARD_COLS_FFN

    h = x
    saved = []
    kvs = None
    for layer in range(DEPTH):
        t = str(layer)
        if layer < N_A:
            (a1, a2, wb, wc), prep_vjp = jax.vjp(_s5_prep, *[W[n][layer] for n in S5_NAMES])
            wb16, wc16 = wb.astype(BF16), wc.astype(BF16)
            hn = _rms_fwd(h, vec(W["g_mix"][layer]), name="mix_norm" + t)
            dvec = vec(W["ssm_d",][layer])
            ypre, yg, sb = S.run("s5_fwd" + t, _s5_fwd, hn, wb16, wc16, a1, a2, dvec)
            z = S.run("glu_mm" + t, _mm, yg, W["w_glu", layer], bk="bkn", ok="bmn")
            z = z.reshape(2, 4, L, SHARD_COLS_GLU)
            h1, hn2 = _glu_res_rms(z, h, vec(W["g_ffn"][layer]), name="glu_res" + t)
            mix_saved = (h, hn, ypre, yg, sb, z, a1, a2, wb16, wc16, dvec, prep_vjp)
        else:
            j = layer - N_A
            if layer == N_A:
                hkv = _rms_fwd(h, vec(W["g_kv"]), name="kv_norm")
                kvm = S.run("kv_mm", _mm, hkv, W["w_kv",], out_dtype=BF16)
                fl = S.run("f_mm", _mm, hkv, W["w_f",])
                cum = _fgate_fwd(fl, W["b_f_pad",], name="fgate_fwd")
                ck = cum[:, :N_HEADS].T.reshape(N_HEADS, 1, L)
                kvs = (h, hkv, fl, kvm, ck)
            _, _, _, kvm, ck = kvs
            hn = _rms_fwd(h, vec(W["g_mix"][layer]), name="mix_norm" + t)
            q = S.run("q_mm" + t, _mm, hn, W["w_q", j], out_dtype=BF16)
            o, o32, lse = S.run("attn_fwd" + t, _attn_fwd, q, kvm, ck)
            h1 = S.run("o_mm" + t, _mm, o, W["w_o", j], add=h)
            hn2 = _rms_fwd(h1, vec(W["g_ffn"][layer]), name="ffn_norm" + t)
            mix_saved = (h, hn, q, o32, o, lse)
        u0 = S.run("ffn_in" + t, _mm, hn2, W["w_ffn_in", layer], bk="bkn", ok="bmn").reshape(2, 4, L, CF)
        a = S.run("ffn_act" + t, _conv_act, u0, W["conv_w", layer], W["conv_b", layer])
        h2 = S.run("ffn_out" + t, _mm, a, W["w_ffn_out", layer], ak="bmk", bk="kbn", add=h1)
        saved.append((mix_saved, h1, hn2, u0))
        h = h2

    loss, dh, dg_final = _loss_head(h, vec(W["g_final"]), target, name="loss_head")
    g = {"g_final": dg_final.reshape(-1)}
    gl = {k: [None] * DEPTH for k in ("g_mix", "g_ffn", "conv_w", "ffn_conv_b")}
    ga = {k: [None] * N_A for k in S5_NAMES + ("ssm_d",)}
    dk = dv = dck = None
    for layer in reversed(range(DEPTH)):
        t = str(layer)
        mix_saved, h1, hn2, u0 = saved[layer]
        cw, cb = W["conv_w", layer], W["conv_b", layer]
        da = S.run("ffn_da" + t, _mm, dh, W["w_ffn_out", layer], bk="nbk", ok="bmn")
        a, du0, dcw, dcb = S.run("ffn_conv_bwd" + t, _conv_ffn_bwd, u0, cw, cb, da)
        dw_out = S.run("ffn_dwout" + t, _mm, a, dh, ak="bkm", ok="mbn", out_dtype=BF16)
        S.grad["w_ffn_out", layer] = dw_out.reshape(N_DEV, SHARD_ROWS_FFN, D_MODEL)
        du0 = du0.reshape(N_DEV, L, CF)
        S.grad["w_ffn_in", layer] = S.run("ffn_dwin" + t, _mm, hn2, du0, ak="km", bk="bkn", ok="bmn", out_dtype=BF16)
        dhn2 = S.run("ffn_dhn" + t, _mm, du0, W["w_ffn_in", layer], ak="bmk", bk="bnk")
        dh1, dg = _rms_bwd(h1, vec(W["g_ffn"][layer]), dhn2, dh, name="ffn_norm_bwd" + t)
        gl["g_ffn"][layer], gl["conv_w"][layer], gl["ffn_conv_b"][layer] = dg.reshape(-1), dcw, dcb.reshape(-1)
        if layer < N_A:
            hin, hn, ypre, yg, sb, z, a1, a2, wb16, wc16, dvec, prep_vjp = mix_saved
            dz = _glu_bwd(z, dh1, name="glu_bwd" + t).reshape(N_DEV, L, SHARD_COLS_GLU)
            S.grad["w_glu", layer] = S.run("glu_dw" + t, _mm, yg, dz, ak="km", bk="bkn", ok="bmn", out_dtype=BF16)
            dyg = S.run("glu_dy" + t, _mm, dz, W["w_glu", layer], ak="bmk", bk="bnk", out_dtype=BF16)
            if layer == 0:
                S.bcast["repl_early",] = _pack_rows(_repl_early({**g, **gl, **ga}), REPL_EARLY_ROWS)
            du, dwb, dwc, da1, da2, dd = S.run("s5_bwd" + t, _s5_bwd, hn, dyg, ypre, sb, wb16, wc16, a1, a2, dvec)
            for nme, val in zip(S5_NAMES, prep_vjp((da1, da2, dwb, dwc))):
                ga[nme][layer] = val
            ga["ssm_d"][layer] = dd.reshape(-1)
            dh, dg = _rms_bwd(hin, vec(W["g_mix"][layer]), du, dh1, name="mix_norm_bwd" + t)
        else:
            j = layer - N_A
            hin, hn, q, o32, o, lse = mix_saved
            _, _, _, kvm, ck = kvs
            S.grad["w_o", j] = S.run("o_dw" + t, _mm, o, dh1, ak="km", out_dtype=BF16
                                     ).reshape(N_DEV, SHARD_ROWS_QO, D_MODEL)
            do = S.run("o_dx" + t, _mm, dh1, W["w_o", j], bk="nk", out_dtype=BF16)
            dq, dk_l, dv_l, dck_l = S.run("attn_bwd" + t, _attn_bwd, q, kvm, ck, o32, do, lse)
            dk = dk_l if dk is None else dk + dk_l
            dv = dv_l if dv is None else dv + dv_l
            dck = dck_l if dck is None else dck + dck_l
            S.grad["w_q", j] = S.run("q_dw" + t, _mm, hn, dq, ak="km", out_dtype=BF16
                                     ).reshape(N_DEV, SHARD_ROWS_QO, D_MODEL)
            dhn = S.run("q_dx" + t, _mm, dq, W["w_q", j], bk="nk")
            dh, dg = _rms_bwd(hin, vec(W["g_mix"][layer]), dhn, dh1, name="mix_norm_bwd" + t)
            if layer == N_A:
                hkv_in, hkv, fl, _, _ = kvs
                dcum = jnp.pad(dck.reshape(N_HEADS, L).T, ((0, 0), (0, LANES - N_HEADS)))
                dfl, dbf = _fgate_bwd(fl, W["b_f_pad",], dcum, name="fgate_bwd")
                dkv = jnp.concatenate([dk, dv], axis=1).astype(BF16)
                dfl16 = dfl.astype(BF16)
                dw_kv = S.run("kv_dw", _mm, hkv, dkv, ak="km")
                dw_f = S.run("f_dw", _mm, hkv, dfl16, ak="km")
                S.grad["w_kvf",] = _kvf_blocks(jnp.concatenate([dw_kv, dw_f[:, :N_HEADS]], axis=1)).astype(BF16)
                dhkv = S.run("kv_dx", _mm, dkv, W["w_kv",], bk="nk")
                dhkv = S.run("f_dx", _mm, dfl16, W["w_f",], bk="nk", add=dhkv)
                g["b_f"] = dbf[0, :N_HEADS]
                dh, dgkv = _rms_bwd(hkv_in, vec(W["g_kv"]), dhkv, dh, name="kv_norm_bwd")
                g["g_kv"] = dgkv.reshape(-1)
        gl["g_mix"][layer] = dg.reshape(-1)

    for d in (gl, ga):
        for k, v in d.items():
            g[k] = jnp.stack(v)
    return loss, dh, g


def _reduce_adamw(slots, w, m, v, layer, prev, *, name):
    nl, R, C = w.shape
    tr = _tile(R, 256, 16)
    c1 = 1.0 / (1.0 - ADAM_B1 ** ADAM_STEP)
    c2 = 1.0 / (1.0 - ADAM_B2 ** ADAM_STEP)

    def body(s_ref, w_ref, m_ref, v_ref, *rest):
        g_ref, d_ref, nm_ref, nv_ref = rest[-4:]
        g = s_ref[0].astype(F32)
        for d in range(1, N_DEV):
            g = g + s_ref[d].astype(F32)
        m2 = ADAM_B1 * m_ref[...] + (1.0 - ADAM_B1) * g
        v2 = ADAM_B2 * v_ref[...] + (1.0 - ADAM_B2) * (g * g)
        g_ref[...] = g
        nm_ref[...] = m2
        nv_ref[...] = v2
        d_ref[...] = -ADAM_LR * ((m2 * c1) / (jnp.sqrt(v2 * c2) + ADAM_EPS) + ADAM_WD * w_ref[...])

    row = pl.BlockSpec((None, tr, C), lambda i: (layer, i, 0))
    out = jax.ShapeDtypeStruct((nl, R, C), F32)
    n_prev = 0 if prev is None else 4
    return pl.pallas_call(
        body, grid=(R // tr,),
        in_specs=[pl.BlockSpec((N_DEV, tr, C), lambda i: (0, i, 0)), row, row, row] + [_ANY] * n_prev,
        out_specs=[row, row, row, row], out_shape=[out, out, out, out],
        input_output_aliases={4 + k: k for k in range(n_prev)},
        compiler_params=_params("parallel"), name=name)(slots, w, m, v, *(prev or ()))


def _adamw_layers(slots, w, m, v, *, name):
    shape = w.shape
    w3, m3, v3 = (a.reshape((len(slots),) + a.shape[-2:]) for a in (w, m, v))
    outs = None
    for layer, s in enumerate(slots):
        outs = _reduce_adamw(s, w3, m3, v3, layer, outs, name=f"{name}{layer}")
    return [o.reshape(shape) for o in outs]


_SMALL_ROWS = 72
_ORDER = ("g_mix", "g_ffn", "lam_re", "lam_im", "log_dt", "ssm_b_re", "ssm_b_im", "ssm_c_re", "ssm_c_im", "ssm_d",
          "w_glu", "g_kv", "w_kvf", "b_f", "w_q", "w_o", "w_ffn_in", "ffn_conv_w", "ffn_conv_b", "w_ffn_out", "g_final")


def _pack_small(ssm_d, conv_w):
    flat = jnp.concatenate([ssm_d.reshape(-1), conv_w.reshape(-1)])
    return jnp.pad(flat, (0, _SMALL_ROWS * LANES - flat.shape[0])).reshape(_SMALL_ROWS, LANES)


def _unpack_small(flat):
    flat = flat.reshape(-1)
    return flat[:256].reshape(2, 128), flat[256:256 + 8448].reshape(4, 3, 704)


_FWD_PLAN = {
    "start": [("small",)],
    "s5_fwd0": [("w_glu", 0), ("w_ffn_in", 0, 0)],
    "glu_mm0": [("w_ffn_in", 0, 1)],
    "ffn_in0": [("w_ffn_out", 0)],
    "ffn_act0": [("w_glu", 1)],
    "ffn_out0": [("w_ffn_in", 1, 0)],
    "s5_fwd1": [("w_ffn_in", 1, 1)],
    "ffn_in1": [("w_ffn_out", 1)],
    "ffn_act1": [("w_kvf",)],
    "ffn_out1": [("w_q", 0), ("w_o", 0)],
    "attn_fwd2": [("w_ffn_in", 2), ("w_ffn_out", 2), ("w_q", 1), ("w_o", 1), ("w_ffn_in", 3), ("w_ffn_out", 3)],
}
_BWD_PLAN = {
    "ffn_dhn3": [("w_ffn_out", 3)],
    "attn_bwd3": [("w_ffn_in", 3), ("w_o", 1)],
    "ffn_conv_bwd2": [("w_q", 1)],
    "ffn_dhn2": [("w_ffn_out", 2)],
    "attn_bwd2": [("w_ffn_in", 2), ("w_o", 0)],
    "ffn_conv_bwd1": [("w_q", 0), ("w_kvf",)],
    "ffn_dhn1": [("w_ffn_out", 1)],
    "s5_bwd1": [("w_ffn_in", 1), ("w_glu", 1)],
    "ffn_dhn0": [("w_ffn_out", 0)],
    "glu_dy0": [("w_glu", 0)],
    "s5_bwd0": [("w_ffn_in", 0), ("repl_early",)],
    "end": [("small",), ("repl_late",)],
}
_PLAN = {h: [("w", k) for k in ks] for h, ks in _FWD_PLAN.items()}
_PLAN.update({h: [("b" if k[0].startswith("repl") else "g", k) for k in ks] for h, ks in _BWD_PLAN.items()})


def kernel(x, g_mix, g_ffn, lam_re, lam_im, log_dt, ssm_b_re, ssm_b_im, ssm_c_re, ssm_c_im, ssm_d, w_glu, g_kv, w_kvf, b_f, w_q, w_o, w_ffn_in, ffn_conv_w, ffn_conv_b, w_ffn_out, g_final, loss_target, m_g_mix, m_g_ffn, m_lam_re, m_lam_im, m_log_dt, m_ssm_b_re, m_ssm_b_im, m_ssm_c_re, m_ssm_c_im, m_ssm_d, m_w_glu, m_g_kv, m_w_kvf, m_b_f, m_w_q, m_w_o, m_w_ffn_in, m_ffn_conv_w, m_ffn_conv_b, m_w_ffn_out, m_g_final, v_g_mix, v_g_ffn, v_lam_re, v_lam_im, v_log_dt, v_ssm_b_re, v_ssm_b_im, v_ssm_c_re, v_ssm_c_im, v_ssm_d, v_w_glu, v_g_kv, v_w_kvf, v_b_f, v_w_q, v_w_o, v_w_ffn_in, v_ffn_conv_w, v_ffn_conv_b, v_w_ffn_out, v_g_final):
    wts = dict(g_mix=g_mix, g_ffn=g_ffn, lam_re=lam_re, lam_im=lam_im, log_dt=log_dt, ssm_b_re=ssm_b_re,
               ssm_b_im=ssm_b_im, ssm_c_re=ssm_c_re, ssm_c_im=ssm_c_im, ssm_d=ssm_d, w_glu=w_glu, g_kv=g_kv,
               w_kvf=w_kvf, b_f=b_f, w_q=w_q, w_o=w_o, w_ffn_in=w_ffn_in, ffn_conv_w=ffn_conv_w,
               ffn_conv_b=ffn_conv_b, w_ffn_out=w_ffn_out, g_final=g_final)
    mom = dict(g_mix=m_g_mix, g_ffn=m_g_ffn, lam_re=m_lam_re, lam_im=m_lam_im, log_dt=m_log_dt, ssm_b_re=m_ssm_b_re,
               ssm_b_im=m_ssm_b_im, ssm_c_re=m_ssm_c_re, ssm_c_im=m_ssm_c_im, ssm_d=m_ssm_d, w_glu=m_w_glu,
               g_kv=m_g_kv, w_kvf=m_w_kvf, b_f=m_b_f, w_q=m_w_q, w_o=m_w_o, w_ffn_in=m_w_ffn_in,
               ffn_conv_w=m_ffn_conv_w, ffn_conv_b=m_ffn_conv_b, w_ffn_out=m_w_ffn_out, g_final=m_g_final)
    var = dict(g_mix=v_g_mix, g_ffn=v_g_ffn, lam_re=v_lam_re, lam_im=v_lam_im, log_dt=v_log_dt, ssm_b_re=v_ssm_b_re,
               ssm_b_im=v_ssm_b_im, ssm_c_re=v_ssm_c_re, ssm_c_im=v_ssm_c_im, ssm_d=v_ssm_d, w_glu=v_w_glu,
               g_kv=v_g_kv, w_kvf=v_w_kvf, b_f=v_b_f, w_q=v_w_q, w_o=v_w_o, w_ffn_in=v_w_ffn_in,
               ffn_conv_w=v_ffn_conv_w, ffn_conv_b=v_ffn_conv_b, w_ffn_out=v_w_ffn_out, g_final=v_g_final)
    kinds = ("grad", "delta", "m", "v")

    ready = {n: wts[n] for n in ("g_mix", "g_ffn", "g_kv", "g_final") + S5_NAMES}
    ready["b_f_pad",] = jnp.pad(b_f, (0, LANES - N_HEADS)).reshape(1, LANES)
    send = {("small",): _pack_small(ssm_d, ffn_conv_w), ("w_kvf",): w_kvf.astype(BF16)}
    for layer in range(DEPTH):
        ready["conv_b", layer] = ffn_conv_b[layer].reshape(2, 4, 1, SHARD_COLS_FFN)
        if ("w_ffn_in", layer) in [k for ks in _FWD_PLAN.values() for k in ks]:
            send["w_ffn_in", layer] = w_ffn_in[layer].astype(BF16)
        else:
            half = D_MODEL // 2
            send["w_ffn_in", layer, 0] = w_ffn_in[layer, :half].astype(BF16)
            send["w_ffn_in", layer, 1] = w_ffn_in[layer, half:].astype(BF16)
        send["w_ffn_out", layer] = w_ffn_out[layer].astype(BF16)
    for layer in range(N_A):
        send["w_glu", layer] = w_glu[layer].astype(BF16)
        send["w_q", layer] = w_q[layer].astype(BF16)
        send["w_o", layer] = w_o[layer].astype(BF16)

    S = _Step(ready, send, _PLAN)
    S.land("start", _exchange(S.xfers("start"), name="start"))
    loss, dx, g = _step(x[0], loss_target[0], S)
    loss = lax.psum(loss[0, 0], MESH_AXES)

    g_d = g["ssm_d"].reshape(N_A, N_DEV, LANES).transpose(1, 0, 2).reshape(N_DEV, N_A * LANES)
    g_cw = jnp.stack([g["conv_w"][layer].reshape(N_DEV, 3, SHARD_COLS_FFN) for layer in range(DEPTH)], axis=1)
    g_small = jnp.concatenate([g_d, g_cw.reshape(N_DEV, -1)], axis=1)
    g_small = jnp.pad(g_small, ((0, 0), (0, _SMALL_ROWS * LANES - g_small.shape[1])))
    S.grad["small",] = g_small.reshape(N_DEV, _SMALL_ROWS, LANES)
    S.bcast["repl_late",] = _pack_rows(_repl_late(g), REPL_LATE_ROWS)
    S.land("end", _exchange(S.xfers("end"), name="end"))

    res = {}
    for name, nl in (("w_glu", N_A), ("w_q", DEPTH - N_A), ("w_o", DEPTH - N_A), ("w_ffn_in", DEPTH),
                     ("w_ffn_out", DEPTH)):
        outs = _adamw_layers([S.slots[name, layer] for layer in range(nl)], wts[name], mom[name], var[name],
                             name="adamw_" + name)
        res.update({(kind, name): a for kind, a in zip(kinds, outs)})
    outs = _adamw_layers([S.slots["w_kvf",]], w_kvf, m_w_kvf, v_w_kvf, name="adamw_w_kvf")
    res.update({(kind, "w_kvf"): a for kind, a in zip(kinds, outs)})
    outs = _adamw_layers([S.slots["small",]], _pack_small(ssm_d, ffn_conv_w), _pack_small(m_ssm_d, m_ffn_conv_w),
                         _pack_small(v_ssm_d, v_ffn_conv_w), name="adamw_small")
    for kind, flat in zip(kinds, outs):
        res[kind, "ssm_d"], res[kind, "ffn_conv_w"] = _unpack_small(flat)

    pieces = {}
    for key, rows, sel in ((("repl_early",), REPL_EARLY_ROWS, _repl_early), (("repl_late",), REPL_LATE_ROWS, _repl_late)):
        outs = _adamw_layers([S.slots[key]], *[_pack_rows(sel(d), rows) for d in (wts, mom, var)],
                             name="adamw_" + key[0])
        for kind, flat in zip(kinds, outs):
            pieces[kind, key[0]] = _unpack_rows(flat, sel(wts))
    for kind in kinds:
        early, late = iter(pieces[kind, "repl_early"]), iter(pieces[kind, "repl_late"])
        take = lambda it, n: [next(it) for _ in range(n)]
        res[kind, "g_mix"] = jnp.stack(take(late, 1) + take(early, DEPTH - 1))
        res[kind, "g_ffn"] = jnp.stack(take(early, DEPTH))
        for n in S5_NAMES:
            res[kind, n] = jnp.stack([next(late), next(early)]).reshape(wts[n].shape)
        res[kind, "g_kv"], res[kind, "b_f"] = next(early), next(early)
        res[kind, "ffn_conv_b"] = jnp.stack(take(early, DEPTH))
        res[kind, "g_final"] = next(early)

    return (loss, dx[None], *[res[kind, n] for kind in kinds for n in _ORDER])
```

```python
import functools
import math

import jax
import jax.numpy as jnp
from jax import lax
from jax.experimental import pallas as pl
from jax.experimental.pallas import tpu as pltpu

F32 = jnp.float32
BF16 = jnp.bfloat16

D_MODEL = 1024
DEPTH = 4
N_A = 2
N_GROUPS = 64
SSM_GROUP = 16
SSM_STATE = 64
N_HEADS = 16
HEAD_DIM = 64
ATTN_SCALE = HEAD_DIM ** -0.5
D_FF = 2816
EPS = 1e-6
N_DEV = 8
LANES = 128
SUBLANES = 8

ADAM_LR = 0.001
ADAM_B1 = 0.9
ADAM_B2 = 0.999
ADAM_EPS = 1e-08
ADAM_WD = 0.01
ADAM_STEP = 10

ROW_TILE = 512
S5_CHUNK = 256
ATTN_TILE = 512
CUM_TILE = 256
NEG = -1e30

MESH_AXES = ("x", "y", "c")


def _tile(n, target, align=LANES):
    t = (min(target, n) // align) * align
    while t >= align:
        if n % t == 0:
            return t
        t -= align
    return n


def _params(*sem):
    return pltpu.CompilerParams(dimension_semantics=sem, vmem_limit_bytes=56 * 1024 * 1024)


_ANY = pl.BlockSpec(memory_space=pl.ANY)
_XFER_SEMS = (pltpu.SemaphoreType.DMA((N_DEV - 1,)), pltpu.SemaphoreType.DMA((N_DEV - 1,)), pltpu.SemaphoreType.DMA)


def _xfer_copies(x_ref, o_ref, send_sems, recv_sems, local_sem, scatter):
    xi, yi, ci = lax.axis_index("x"), lax.axis_index("y"), lax.axis_index("c")
    me = 4 * xi + 2 * yi + ci

    def src(p):
        return x_ref.at[p] if scatter else x_ref

    own = pltpu.make_async_copy(src(me), o_ref.at[me], local_sem)
    sends, recvs = [], []
    for k in range(1, N_DEV):
        px, py, pc = xi ^ (k >> 2), yi ^ ((k >> 1) & 1), ci ^ (k & 1)
        p = 4 * px + 2 * py + pc
        sends.append(pltpu.make_async_remote_copy(
            src_ref=src(p), dst_ref=o_ref.at[me], send_sem=send_sems.at[k - 1], recv_sem=recv_sems.at[k - 1],
            device_id=(px, py, pc), device_id_type=pl.DeviceIdType.MESH))
        recvs.append(pltpu.make_async_remote_copy(
            src_ref=src(p), dst_ref=o_ref.at[p], send_sem=send_sems.at[k - 1], recv_sem=recv_sems.at[k - 1],
            device_id=(px, py, pc), device_id_type=pl.DeviceIdType.MESH))
    return own, sends, recvs


def _xfer_start(*refs, scatter):
    own, sends, _ = _xfer_copies(*refs, scatter)
    own.start()
    for cp in sends:
        cp.start()


def _xfer_wait(*refs, scatter):
    own, sends, recvs = _xfer_copies(*refs, scatter)
    for cp in recvs:
        cp.wait_recv()
    for cp in sends:
        cp.wait_send()
    own.wait()


def _xfer_out(x, scatter):
    return jax.ShapeDtypeStruct((N_DEV,) + (x.shape[1:] if scatter else x.shape), x.dtype)


def _pcall(body, args, *, grid, in_specs, out_specs, out_shape, scratch_shapes=(), sem, name, xfers=(), prefetch=()):
    out_specs, out_shape = list(out_specs), list(out_shape)
    n_pre, n_in, n_out, n_x, n_scr = len(prefetch), len(in_specs), len(out_specs), len(xfers), len(scratch_shapes)
    flags = [s for _, s in xfers]

    def wrapped(*refs):
        pre, refs = refs[:n_pre], refs[n_pre:]
        ins, xin = refs[:n_in], refs[n_in:n_in + n_x]
        outs = refs[n_in + n_x:n_in + n_x + n_out]
        xout = refs[n_in + n_x + n_out:n_in + 2 * n_x + n_out]
        scr = refs[n_in + 2 * n_x + n_out:]
        own, sems = scr[:n_scr], scr[n_scr:]
        ids = [pl.program_id(d) for d in range(len(grid))]
        first = functools.reduce(jnp.logical_and, [i == 0 for i in ids])
        last = functools.reduce(jnp.logical_and, [i == g - 1 for i, g in zip(ids, grid)])

        @pl.when(first)
        def _():
            for t in range(n_x):
                _xfer_start(xin[t], xout[t], *sems[3 * t:3 * t + 3], scatter=flags[t])

        body(*pre, *ins, *outs, *own)

        @pl.when(last)
        def _():
            for t in range(n_x):
                _xfer_wait(xin[t], xout[t], *sems[3 * t:3 * t + 3], scatter=flags[t])

    grid_spec = pltpu.PrefetchScalarGridSpec(
        num_scalar_prefetch=n_pre, grid=grid, in_specs=list(in_specs) + [_ANY] * n_x,
        out_specs=out_specs + [_ANY] * n_x, scratch_shapes=list(scratch_shapes) + list(_XFER_SEMS) * n_x)
    return pl.pallas_call(
        wrapped if xfers else body, grid_spec=grid_spec, out_shape=out_shape + [_xfer_out(x, s) for x, s in xfers],
        compiler_params=_params(*(["arbitrary"] * len(grid) if xfers else sem)), name=name,
    )(*prefetch, *args, *[x for x, _ in xfers])


def _exchange(xfers, *, name):
    def body():
        pass

    return _pcall(body, (), grid=(1,), in_specs=[], out_specs=[], out_shape=[], sem=("arbitrary",), name=name,
                  xfers=xfers)


def _mm(a, b, *, ak="mk", bk="kn", ok="mn", add=None, out_dtype=F32, tm=1024, tn=1024, tk=1024, name, xfers=()):
    sa, sb = a.shape, b.shape
    fm = fn = fk = None
    if ak == "mk":
        M, K, a_c = sa[0], sa[1], 1
    elif ak == "km":
        K, M, a_c = sa[0], sa[1], 0
    elif ak == "bmk":
        M, K, a_c, fk = sa[1], sa[0] * sa[2], 1, sa[2]
    else:
        K, M, a_c, fm = sa[1], sa[0] * sa[2], 0, sa[2]
    if bk == "kn":
        N, b_c = sb[1], 0
    elif bk == "nk":
        N, b_c = sb[0], 1
    elif bk == "bkn":
        N, b_c, fn = sb[0] * sb[2], 0, sb[2]
    elif bk == "bnk":
        N, b_c, fk = sb[1], 1, sb[2]
    elif bk == "kbn":
        N, b_c, fk = sb[2], 0, sb[1]
    else:
        N, b_c, fn = sb[0] * sb[1], 1, sb[1]
    tm, tn, tk = fm or _tile(M, tm), fn or _tile(N, tn), fk or _tile(K, tk)
    nm, nn, nk = M // tm, N // tn, K // tk

    a_spec = {"mk": pl.BlockSpec((tm, tk), lambda i, j, k: (i, k)),
              "km": pl.BlockSpec((tk, tm), lambda i, j, k: (k, i)),
              "bmk": pl.BlockSpec((None, tm, tk), lambda i, j, k: (k, i, 0)),
              "bkm": pl.BlockSpec((None, tk, tm), lambda i, j, k: (i, k, 0))}[ak]
    b_spec = {"kn": pl.BlockSpec((tk, tn), lambda i, j, k: (k, j)),
              "nk": pl.BlockSpec((tn, tk), lambda i, j, k: (j, k)),
              "bkn": pl.BlockSpec((None, tk, tn), lambda i, j, k: (j, k, 0)),
              "bnk": pl.BlockSpec((None, tn, tk), lambda i, j, k: (k, j, 0)),
              "kbn": pl.BlockSpec((None, tk, tn), lambda i, j, k: (k, 0, j)),
              "nbk": pl.BlockSpec((None, tn, tk), lambda i, j, k: (j, 0, k))}[bk]
    if ok == "mn":
        o_spec = pl.BlockSpec((tm, tn), lambda i, j, k: (i, j))
        out_shape = jax.ShapeDtypeStruct((M, N), out_dtype)
    elif ok == "bmn":
        o_spec = pl.BlockSpec((None, tm, tn), lambda i, j, k: (j, i, 0))
        out_shape = jax.ShapeDtypeStruct((nn, M, tn), out_dtype)
    else:
        o_spec = pl.BlockSpec((None, tm, tn), lambda i, j, k: (i, 0, j))
        out_shape = jax.ShapeDtypeStruct((nm, tm, N), out_dtype)
    dims = (((a_c,), (b_c,)), ((), ()))
    has_add = add is not None

    def body(*refs):
        a_ref, b_ref = refs[0], refs[1]
        add_ref = refs[2] if has_add else None
        o_ref = refs[3] if has_add else refs[2]
        part = lax.dot_general(a_ref[...].astype(BF16), b_ref[...].astype(BF16), dims, preferred_element_type=F32)

        def finish(r):
            if has_add:
                r = r + add_ref[...]
            o_ref[...] = r.astype(out_dtype)

        if nk == 1:
            finish(part)
            return
        acc = refs[-1]
        k = pl.program_id(2)

        @pl.when(k == 0)
        def _():
            acc[...] = part

        @pl.when(k > 0)
        def _():
            acc[...] += part

        @pl.when(k == nk - 1)
        def _():
            finish(acc[...])

    in_specs = [a_spec, b_spec]
    args = [a, b]
    if has_add:
        in_specs.append(pl.BlockSpec((tm, tn), lambda i, j, k: (i, j)))
        args.append(add)
    res = _pcall(body, args, grid=(nm, nn, nk), in_specs=in_specs, out_specs=[o_spec], out_shape=[out_shape],
                 scratch_shapes=[pltpu.VMEM((tm, tn), F32)] if nk > 1 else [],
                 sem=("parallel", "parallel", "arbitrary"), name=name, xfers=xfers)
    return res if xfers else res[0]


def _rms_fwd(h, g, *, name):
    L, D = h.shape
    tr = _tile(L, ROW_TILE, SUBLANES)

    def body(h_ref, g_ref, o_ref):
        x = h_ref[...]
        r = lax.rsqrt(jnp.mean(x * x, axis=1, keepdims=True) + EPS)
        o_ref[...] = (x * r * g_ref[...]).astype(BF16)

    return pl.pallas_call(
        body, grid=(L // tr,),
        in_specs=[pl.BlockSpec((tr, D), lambda i: (i, 0)), pl.BlockSpec((1, D), lambda i: (0, 0))],
        out_specs=pl.BlockSpec((tr, D), lambda i: (i, 0)), out_shape=jax.ShapeDtypeStruct((L, D), BF16),
        compiler_params=_params("parallel"), name=name)(h, g)


def _rms_bwd(h, g, dy, dres, *, name):
    L, D = h.shape
    tr = _tile(L, ROW_TILE, SUBLANES)

    def body(h_ref, g_ref, dy_ref, dres_ref, dh_ref, dg_ref):
        @pl.when(pl.program_id(0) == 0)
        def _():
            dg_ref[...] = jnp.zeros_like(dg_ref)

        x = h_ref[...]
        r = lax.rsqrt(jnp.mean(x * x, axis=1, keepdims=True) + EPS)
        xn = x * r
        dy = dy_ref[...].astype(F32)
        gdy = dy * g_ref[...]
        dx = r * (gdy - xn * jnp.mean(gdy * xn, axis=1, keepdims=True))
        dh_ref[...] = dres_ref[...] + dx
        dg_ref[...] += jnp.sum(dy * xn, axis=0, keepdims=True)

    row = pl.BlockSpec((tr, D), lambda i: (i, 0))
    vec = pl.BlockSpec((1, D), lambda i: (0, 0))
    return pl.pallas_call(
        body, grid=(L // tr,), in_specs=[row, vec, row, row], out_specs=[row, vec],
        out_shape=[jax.ShapeDtypeStruct((L, D), F32), jax.ShapeDtypeStruct((1, D), F32)],
        compiler_params=_params("arbitrary"), name=name)(h, g, dy, dres)


def _glu_res_rms(z, h, g, *, name):
    L, D = h.shape
    nb, cb = z.shape[1], z.shape[3]
    tr = _tile(L, ROW_TILE, SUBLANES)

    def body(z_ref, h_ref, g_ref, h1_ref, hn_ref):
        za = jnp.concatenate([z_ref[0, d] for d in range(nb)], axis=1)
        zg = jnp.concatenate([z_ref[1, d] for d in range(nb)], axis=1)
        x = h_ref[...] + za * jax.nn.sigmoid(zg)
        h1_ref[...] = x
        r = lax.rsqrt(jnp.mean(x * x, axis=1, keepdims=True) + EPS)
        hn_ref[...] = (x * r * g_ref[...]).astype(BF16)

    row = pl.BlockSpec((tr, D), lambda i: (i, 0))
    return pl.pallas_call(
        body, grid=(L // tr,),
        in_specs=[pl.BlockSpec((2, nb, tr, cb), lambda i: (0, 0, i, 0)), row, pl.BlockSpec((1, D), lambda i: (0, 0))],
        out_specs=[row, row],
        out_shape=[jax.ShapeDtypeStruct((L, D), F32), jax.ShapeDtypeStruct((L, D), BF16)],
        compiler_params=_params("parallel"), name=name)(z, h, g)


def _glu_bwd(z, dout, *, name):
    L, D = dout.shape
    nb, cb = z.shape[1], z.shape[3]
    tr = _tile(L, ROW_TILE, SUBLANES)

    def body(z_ref, d_ref, o_ref):
        for d in range(nb):
            dd = d_ref[:, d * cb:(d + 1) * cb]
            sg = jax.nn.sigmoid(z_ref[1, d])
            o_ref[0, d] = (dd * sg).astype(BF16)
            o_ref[1, d] = (dd * z_ref[0, d] * sg * (1.0 - sg)).astype(BF16)

    zs = pl.BlockSpec((2, nb, tr, cb), lambda i: (0, 0, i, 0))
    return pl.pallas_call(
        body, grid=(L // tr,), in_specs=[zs, pl.BlockSpec((tr, D), lambda i: (i, 0))], out_specs=zs,
        out_shape=jax.ShapeDtypeStruct(z.shape, BF16),
        compiler_params=_params("parallel"), name=name)(z, dout)


def _loss_head(h, g, target, *, name):
    L, D = h.shape
    tr = _tile(L, ROW_TILE, SUBLANES)

    def body(h_ref, g_ref, t_ref, loss_ref, dh_ref, dg_ref):
        @pl.when(pl.program_id(0) == 0)
        def _():
            dg_ref[...] = jnp.zeros_like(dg_ref)
            loss_ref[...] = jnp.zeros_like(loss_ref)

        x = h_ref[...]
        gg = g_ref[...]
        r = lax.rsqrt(jnp.mean(x * x, axis=1, keepdims=True) + EPS)
        xn = x * r
        err = xn * gg - t_ref[...]
        loss_ref[...] += 0.5 * jnp.sum(jnp.mean(err * err, axis=1, keepdims=True), axis=0, keepdims=True)
        dy = err * (1.0 / D)
        gdy = dy * gg
        dh_ref[...] = r * (gdy - xn * jnp.mean(gdy * xn, axis=1, keepdims=True))
        dg_ref[...] += jnp.sum(dy * xn, axis=0, keepdims=True)

    row = pl.BlockSpec((tr, D), lambda i: (i, 0))
    vec = pl.BlockSpec((1, D), lambda i: (0, 0))
    return pl.pallas_call(
        body, grid=(L // tr,), in_specs=[row, vec, row],
        out_specs=[pl.BlockSpec((1, 1), lambda i: (0, 0)), row, vec],
        out_shape=[jax.ShapeDtypeStruct((1, 1), F32), jax.ShapeDtypeStruct((L, D), F32),
                   jax.ShapeDtypeStruct((1, D), F32)],
        compiler_params=_params("arbitrary"), name=name)(h, g, target)


CONV_ROW_TILE = 256


def _conv_specs(L, tr, tc):
    nrb = tr // SUBLANES
    before = lambda i: jnp.maximum(i * nrb - 1, 0)
    after = lambda i: jnp.minimum((i + 1) * nrb, L // SUBLANES - 1)
    main = pl.BlockSpec((2, None, tr, tc), lambda j, i: (0, j, i, 0))
    prev = pl.BlockSpec((2, None, SUBLANES, tc), lambda j, i: (0, j, before(i), 0))
    nxt = pl.BlockSpec((2, None, SUBLANES, tc), lambda j, i: (0, j, after(i), 0))
    cw = pl.BlockSpec((2, None, 3, tc), lambda j, i: (0, j, 0, 0))
    cb = pl.BlockSpec((2, None, 1, tc), lambda j, i: (0, j, 0, 0))
    half = pl.BlockSpec((None, tr, tc), lambda j, i: (j, i, 0))
    half_nxt = pl.BlockSpec((None, SUBLANES, tc), lambda j, i: (j, after(i), 0))
    return main, prev, nxt, cw, cb, half, half_nxt


def _conv_rows(xe, w, b):
    x1 = pltpu.roll(xe, 1, 0)
    x2 = pltpu.roll(xe, 2, 0)
    return b + x2 * w[0:1] + x1 * w[1:2] + xe * w[2:3], x1, x2


def _shift_down(x, halo, k, row):
    y = pltpu.roll(x, k, 0)
    for r in range(k):
        y = jnp.where(row == r, halo[SUBLANES - k + r:SUBLANES - k + r + 1, :], y)
    return y


def _conv_act(u0, cw, cb, *, name, xfers=()):
    _, nb, L, tc = u0.shape
    tr = _tile(L, ROW_TILE, SUBLANES)
    main, prev, _, cws, cbs, half, _ = _conv_specs(L, tr, tc)

    def body(u_ref, p_ref, w_ref, b_ref, a_ref):
        first = pl.program_id(1) == 0
        row = lax.broadcasted_iota(jnp.int32, (tr, tc), 0)
        y = []
        for s in range(2):
            x, w = u_ref[s], w_ref[s]
            halo = jnp.where(first, 0.0, p_ref[s])
            x1 = _shift_down(x, halo, 1, row)
            x2 = _shift_down(x, halo, 2, row)
            y.append(b_ref[s] + x2 * w[0:1] + x1 * w[1:2] + x * w[2:3])
        a_ref[...] = (y[0] * jax.nn.sigmoid(y[0]) * y[1]).astype(BF16)

    return _pcall(body, (u0, u0, cw, cb), grid=(nb, L // tr), in_specs=[main, prev, cws, cbs], out_specs=[half],
                  out_shape=[jax.ShapeDtypeStruct((nb, L, tc), BF16)], sem=("parallel", "parallel"), name=name,
                  xfers=xfers)


def _conv_ffn_bwd(u0, cw, cb, da, *, name, xfers=()):
    _, nb, L, tc = u0.shape
    tr = _tile(L, CONV_ROW_TILE, SUBLANES)
    main, prev, nxt, cws, cbs, half, half_nxt = _conv_specs(L, tr, tc)
    nr = L // tr
    H = SUBLANES

    def body(u_ref, p_ref, n_ref, w_ref, b_ref, da_ref, dan_ref, a_ref, du0_ref, dcw_ref, dcb_ref):
        i = pl.program_id(1)

        @pl.when(i == 0)
        def _():
            dcw_ref[...] = jnp.zeros_like(dcw_ref)
            dcb_ref[...] = jnp.zeros_like(dcb_ref)

        y, x1, x2 = [], [], []
        for s in range(2):
            xe = jnp.concatenate([jnp.where(i == 0, 0.0, p_ref[s]), u_ref[s], n_ref[s]], axis=0)
            ys, x1s, x2s = _conv_rows(xe, w_ref[s], b_ref[s])
            y.append(ys[H:])
            x1.append(x1s[H:H + tr])
            x2.append(x2s[H:H + tr])
        gate, up = y
        da = jnp.concatenate([da_ref[...], dan_ref[...]], axis=0)
        row = lax.broadcasted_iota(jnp.int32, (tr + H, tc), 0)
        da = jnp.where(jnp.logical_and(i == nr - 1, row >= tr), 0.0, da)
        sg = jax.nn.sigmoid(gate)
        silu = gate * sg
        a_ref[...] = (silu * up)[:tr].astype(BF16)
        d = (da * up * (sg * (1.0 + gate * (1.0 - sg))), da * silu)
        for s in range(2):
            w = w_ref[s]
            d0 = d[s][:tr]
            d1 = pltpu.roll(d[s], tr + H - 1, 0)[:tr]
            d2 = pltpu.roll(d[s], tr + H - 2, 0)[:tr]
            du0_ref[s] = (d0 * w[2:3] + d1 * w[1:2] + d2 * w[0:1]).astype(BF16)
            dcw_ref[s, 0:1, :] += jnp.sum(d0 * x2[s], axis=0, keepdims=True)
            dcw_ref[s, 1:2, :] += jnp.sum(d0 * x1[s], axis=0, keepdims=True)
            dcw_ref[s, 2:3, :] += jnp.sum(d0 * u_ref[s], axis=0, keepdims=True)
            dcb_ref[s] += jnp.sum(d0, axis=0, keepdims=True)

    return _pcall(body, (u0, u0, u0, cw, cb, da, da), grid=(nb, nr),
                  in_specs=[main, prev, nxt, cws, cbs, half, half_nxt], out_specs=[half, main, cws, cbs],
                  out_shape=[jax.ShapeDtypeStruct((nb, L, tc), BF16), jax.ShapeDtypeStruct((2, nb, L, tc), BF16),
                             jax.ShapeDtypeStruct((2, nb, 3, tc), F32), jax.ShapeDtypeStruct((2, nb, 1, tc), F32)],
                  sem=("parallel", "arbitrary"), name=name, xfers=xfers)


N_TILES = 64
HALF = N_TILES // 2


def _swap(s):
    return jnp.concatenate([s[HALF:], s[:HALF]], axis=0)


def _chan_block(j):
    return ((j % HALF) // 4) * LANES


def _tiles_of(jb):
    return [4 * jb + i for i in range(4)] + [HALF + 4 * jb + i for i in range(4)]
GELU_C = math.sqrt(2.0 / math.pi)
GELU_A = 0.044715


def _gelu(x):
    return 0.5 * x * (1.0 + jnp.tanh(GELU_C * (x + GELU_A * x * x * x)))


def _gelu_grad(x):
    th = jnp.tanh(GELU_C * (x + GELU_A * x * x * x))
    return 0.5 * (1.0 + th) + 0.5 * x * (1.0 - th * th) * GELU_C * (1.0 + 3.0 * GELU_A * x * x)


def _s5_project_in(u_ref, wb_ref, s3, T, TP):
    for j in range(N_TILES):
        blk = _chan_block(j)
        s3[pl.ds(j * TP + SUBLANES, T), :] = jnp.dot(u_ref[:, blk:blk + LANES], wb_ref[j],
                                                     preferred_element_type=F32)


def _s5_scan_fwd(s3, a1, a2, s0, T, TP):
    span = (N_GROUPS - 1) * TP + 2 * SUBLANES

    def blk(i, s):
        view = s3.at[pl.ds(pl.multiple_of(i * SUBLANES, SUBLANES), span)]
        for k in range(SUBLANES):
            rows = pl.ds(SUBLANES + k, N_GROUPS, stride=TP)
            s = a1 * s + a2 * _swap(s) + view[rows, :]
            view[rows, :] = s
        return s

    return lax.fori_loop(0, T // SUBLANES, blk, s0)


def _s5_fwd(hn, wb, wc, a1, a2, dvec, *, name, xfers=()):
    L, D = hn.shape
    T = min(S5_CHUNK, L)
    TP = T + SUBLANES
    nC = L // T

    def body(u_ref, wb_ref, wc_ref, a1_ref, a2_ref, d_ref, y_ref, yg_ref, sb_ref, s3, st):
        @pl.when(pl.program_id(0) == 0)
        def _():
            st[...] = jnp.zeros_like(st)

        sb_ref[0] = st[...]
        _s5_project_in(u_ref, wb_ref, s3, T, TP)
        st[...] = _s5_scan_fwd(s3, a1_ref[...], a2_ref[...], st[...], T, TP)
        for jb in range(D // LANES):
            acc = jnp.zeros((T, LANES), F32)
            for j in _tiles_of(jb):
                acc += jnp.dot(s3[pl.ds(j * TP + SUBLANES, T), :].astype(BF16), wc_ref[j],
                               preferred_element_type=F32)
            cols = slice(jb * LANES, (jb + 1) * LANES)
            y = acc + d_ref[:, cols] * u_ref[:, cols].astype(F32)
            y_ref[:, cols] = y
            yg_ref[:, cols] = _gelu(y).astype(BF16)

    row = pl.BlockSpec((T, D), lambda c: (c, 0))
    wspec = pl.BlockSpec((N_GROUPS, LANES, LANES), lambda c: (0, 0, 0))
    aspec = pl.BlockSpec((N_GROUPS, LANES), lambda c: (0, 0))
    return _pcall(
        body, (hn, wb, wc, a1, a2, dvec), grid=(nC,),
        in_specs=[row, wspec, wspec, aspec, aspec, pl.BlockSpec((1, D), lambda c: (0, 0))],
        out_specs=[row, row, pl.BlockSpec((1, N_GROUPS, LANES), lambda c: (c, 0, 0))],
        out_shape=[jax.ShapeDtypeStruct((L, D), F32), jax.ShapeDtypeStruct((L, D), BF16),
                   jax.ShapeDtypeStruct((nC, N_GROUPS, LANES), F32)],
        scratch_shapes=[pltpu.VMEM((N_GROUPS * TP, LANES), F32), pltpu.VMEM((N_GROUPS, LANES), F32)],
        sem=("arbitrary",), name=name, xfers=xfers)


def _s5_bwd(hn, dyg, ypre, sbound, wb, wc, a1, a2, dvec, *, name, xfers=()):
    L, D = hn.shape
    T = min(S5_CHUNK, L)
    TP = T + SUBLANES
    nC = L // T
    span = (N_GROUPS - 1) * TP + 2 * SUBLANES
    NT = (((1,), (1,)), ((), ()))
    TN = (((0,), (0,)), ((), ()))

    def body(u_ref, dyg_ref, yp_ref, sb_ref, wb_ref, wc_ref, a1_ref, a2_ref, d_ref,
             du_ref, dwb_ref, dwc_ref, da1_ref, da2_ref, dd_ref, s3, g3, gst, dy_s):
        @pl.when(pl.program_id(0) == 0)
        def _():
            gst[...] = jnp.zeros_like(gst)
            dwb_ref[...] = jnp.zeros_like(dwb_ref)
            dwc_ref[...] = jnp.zeros_like(dwc_ref)
            da1_ref[...] = jnp.zeros_like(da1_ref)
            da2_ref[...] = jnp.zeros_like(da2_ref)
            dd_ref[...] = jnp.zeros_like(dd_ref)

        a1 = a1_ref[...]
        a2 = a2_ref[...]
        dy = dyg_ref[...].astype(F32) * _gelu_grad(yp_ref[...])
        dy_s[...] = dy.astype(BF16)
        dd_ref[...] += jnp.sum(dy * u_ref[...].astype(F32), axis=0, keepdims=True)
        du_ref[...] = d_ref[...] * dy

        s3[pl.ds(SUBLANES - 1, N_GROUPS, stride=TP), :] = sb_ref[0]
        _s5_project_in(u_ref, wb_ref, s3, T, TP)
        _s5_scan_fwd(s3, a1, a2, sb_ref[0], T, TP)

        for j in range(N_TILES):
            blk = _chan_block(j)
            g3[pl.ds(j * TP + SUBLANES, T), :] = lax.dot_general(dy_s[:, blk:blk + LANES], wc_ref[j], NT,
                                                                 preferred_element_type=F32)
        a2c = -a2

        def rblk(ii, carry):
            g, acc1, acc2 = carry
            t0 = pl.multiple_of((T // SUBLANES - 1 - ii) * SUBLANES, SUBLANES)
            gv = g3.at[pl.ds(t0, span)]
            sv = s3.at[pl.ds(t0, span)]
            for k in reversed(range(SUBLANES)):
                rows = pl.ds(SUBLANES + k, N_GROUPS, stride=TP)
                g = a1 * g + a2c * _swap(g) + gv[rows, :]
                gv[rows, :] = g
                sp = sv[pl.ds(SUBLANES - 1 + k, N_GROUPS, stride=TP), :]
                acc1 = acc1 + g * sp
                acc2 = acc2 + g * _swap(sp)
            return g, acc1, acc2

        zero = jnp.zeros((N_GROUPS, LANES), F32)
        g, acc1, acc2 = lax.fori_loop(0, T // SUBLANES, rblk, (gst[...], zero, zero))
        gst[...] = g
        da1_ref[...] += acc1
        da2_ref[...] += acc2

        for jb in range(D // LANES):
            cols = slice(jb * LANES, (jb + 1) * LANES)
            acc = jnp.zeros((T, LANES), F32)
            for j in _tiles_of(jb):
                rows = pl.ds(j * TP + SUBLANES, T)
                gj = g3[rows, :].astype(BF16)
                dwc_ref[j] += lax.dot_general(s3[rows, :].astype(BF16), dy_s[:, cols], TN,
                                              preferred_element_type=F32)
                dwb_ref[j] += lax.dot_general(u_ref[:, cols], gj, TN, preferred_element_type=F32)
                acc += lax.dot_general(gj, wb_ref[j], NT, preferred_element_type=F32)
            du_ref[:, cols] += acc

    rrow = pl.BlockSpec((T, D), lambda c: (nC - 1 - c, 0))
    wspec = pl.BlockSpec((N_GROUPS, LANES, LANES), lambda c: (0, 0, 0))
    aspec = pl.BlockSpec((N_GROUPS, LANES), lambda c: (0, 0))
    vec = pl.BlockSpec((1, D), lambda c: (0, 0))
    return _pcall(
        body, (hn, dyg, ypre, sbound, wb, wc, a1, a2, dvec), grid=(nC,),
        in_specs=[rrow, rrow, rrow, pl.BlockSpec((1, N_GROUPS, LANES), lambda c: (nC - 1 - c, 0, 0)),
                  wspec, wspec, aspec, aspec, vec],
        out_specs=[rrow, wspec, wspec, aspec, aspec, vec],
        out_shape=[jax.ShapeDtypeStruct((L, D), F32),
                   jax.ShapeDtypeStruct((N_GROUPS, LANES, LANES), F32),
                   jax.ShapeDtypeStruct((N_GROUPS, LANES, LANES), F32),
                   jax.ShapeDtypeStruct((N_GROUPS, LANES), F32), jax.ShapeDtypeStruct((N_GROUPS, LANES), F32),
                   jax.ShapeDtypeStruct((1, D), F32)],
        scratch_shapes=[pltpu.VMEM((N_GROUPS * TP, LANES), F32), pltpu.VMEM((N_GROUPS * TP, LANES), F32),
                        pltpu.VMEM((N_GROUPS, LANES), F32), pltpu.VMEM((T, D), BF16)],
        sem=("arbitrary",), name=name, xfers=xfers)


def _s5_prep(lam_re, lam_im, log_dt, b_re, b_im, c_re, c_im):
    dt = jnp.exp(log_dt)[:, None]
    mag = jnp.exp(lam_re * dt)
    lb_re = mag * jnp.cos(lam_im * dt)
    lb_im = mag * jnp.sin(lam_im * dt)
    den = lam_re * lam_re + lam_im * lam_im
    nr = lb_re - 1.0
    fr = ((nr * lam_re + lb_im * lam_im) / den)[..., None]
    fi = ((lb_im * lam_re - nr * lam_im) / den)[..., None]
    bb_re = fr * b_re - fi * b_im
    bb_im = fr * b_im + fi * b_re
    pair = lambda a: a.reshape(HALF, 2 * SSM_STATE)
    a1 = jnp.concatenate([pair(lb_re), pair(lb_re)], axis=0)
    a2 = jnp.concatenate([-pair(lb_im), pair(lb_im)], axis=0)
    sel = jax.nn.one_hot(jnp.arange(HALF) % 4, 4, dtype=F32)
    eye = jnp.eye(2, dtype=F32)

    def w_in(bb):
        return jnp.einsum('jk,ef,jfph->jkehfp', sel, eye, bb.reshape(HALF, 2, SSM_STATE, SSM_GROUP)
                          ).reshape(HALF, LANES, LANES)

    def w_out(c):
        return jnp.einsum('jk,ef,jfhp->jepkfh', sel, eye, c.reshape(HALF, 2, SSM_GROUP, SSM_STATE)
                          ).reshape(HALF, LANES, LANES)

    wb = jnp.concatenate([w_in(bb_re), w_in(bb_im)], axis=0)
    wc = jnp.concatenate([w_out(c_re), w_out(-c_im)], axis=0)
    return a1, a2, wb, wc


def _tri(n, upper):
    r = lax.broadcasted_iota(jnp.int32, (n, n), 0)
    c = lax.broadcasted_iota(jnp.int32, (n, n), 1)
    return ((r <= c) if upper else (r >= c)).astype(F32)


def _fgate_fwd(fl, bf, *, name):
    L, W = fl.shape
    tr = _tile(L, CUM_TILE, SUBLANES)

    def body(f_ref, b_ref, o_ref, carry):
        @pl.when(pl.program_id(0) == 0)
        def _():
            carry[...] = jnp.zeros_like(carry)

        x = f_ref[...] + b_ref[...]
        ls = jnp.minimum(x, 0.0) - jnp.log(1.0 + jnp.exp(-jnp.abs(x)))
        cum = jnp.dot(_tri(tr, False), ls, preferred_element_type=F32, precision=lax.Precision.HIGHEST) + carry[...]
        o_ref[...] = cum
        carry[...] = cum[tr - 1:tr, :]

    return pl.pallas_call(
        body, grid=(L // tr,),
        in_specs=[pl.BlockSpec((tr, W), lambda i: (i, 0)), pl.BlockSpec((1, W), lambda i: (0, 0))],
        out_specs=pl.BlockSpec((tr, W), lambda i: (i, 0)), out_shape=jax.ShapeDtypeStruct((L, W), F32),
        scratch_shapes=[pltpu.VMEM((1, W), F32)], compiler_params=_params("arbitrary"), name=name)(fl, bf)


def _fgate_bwd(fl, bf, dcum, *, name):
    L, W = fl.shape
    tr = _tile(L, CUM_TILE, SUBLANES)
    n = L // tr

    def body(f_ref, b_ref, d_ref, o_ref, db_ref, carry):
        @pl.when(pl.program_id(0) == 0)
        def _():
            carry[...] = jnp.zeros_like(carry)
            db_ref[...] = jnp.zeros_like(db_ref)

        d = d_ref[...]
        rev = jnp.dot(_tri(tr, True), d, preferred_element_type=F32, precision=lax.Precision.HIGHEST) + carry[...]
        carry[...] += jnp.sum(d, axis=0, keepdims=True)
        df = rev * jax.nn.sigmoid(-(f_ref[...] + b_ref[...]))
        o_ref[...] = df
        db_ref[...] += jnp.sum(df, axis=0, keepdims=True)

    rrow = pl.BlockSpec((tr, W), lambda i: (n - 1 - i, 0))
    vec = pl.BlockSpec((1, W), lambda i: (0, 0))
    return pl.pallas_call(
        body, grid=(n,), in_specs=[rrow, vec, rrow], out_specs=[rrow, vec],
        out_shape=[jax.ShapeDtypeStruct((L, W), F32), jax.ShapeDtypeStruct((1, W), F32)],
        scratch_shapes=[pltpu.VMEM((1, W), F32)], compiler_params=_params("arbitrary"), name=name)(fl, bf, dcum)


_NT = (((1,), (1,)), ((), ()))
_TN = (((0,), (0,)), ((), ()))
HEAD_PAIRS = N_HEADS // 2


def _causal_tiles(n, by_row):
    pairs = ([(i, j) for i in range(n) for j in range(i + 1)] if by_row
             else [(i, j) for j in range(n) for i in range(j, n)])
    return (jnp.array([p[0] for p in pairs], jnp.int32), jnp.array([p[1] for p in pairs], jnp.int32))


def _attn_logits(qs, k, ck, masked, t):
    s = lax.dot_general(qs, k, _NT, preferred_element_type=F32) - ck
    if masked:
        r = lax.broadcasted_iota(jnp.int32, (t, t), 0)
        c = lax.broadcasted_iota(jnp.int32, (t, t), 1)
        s = jnp.where(c > r, NEG, s)
    return s


def _attn_fwd(q, kv, ck, *, name, xfers=()):
    L, D = q.shape
    t = _tile(L, ATTN_TILE)
    n = L // t
    dh = HEAD_DIM

    def body(q_ref, k_ref, v_ref, ck_ref, o_ref, o32_ref, lse_ref, m_s, l_s, acc):
        i, j = pl.program_id(1), pl.program_id(2)

        @pl.when(j == 0)
        def _():
            m_s[...] = jnp.full_like(m_s, NEG)
            l_s[...] = jnp.zeros_like(l_s)
            acc[...] = jnp.zeros_like(acc)

        def tile(masked):
            for e in range(2):
                sl = slice(e * dh, (e + 1) * dh)
                v = v_ref[:, sl]
                s = _attn_logits(q_ref[:, sl] * ATTN_SCALE, k_ref[:, sl], ck_ref[e], masked, t)
                m_new = jnp.maximum(m_s[e], jnp.max(s, axis=1, keepdims=True))
                alpha = jnp.exp(m_s[e] - m_new)
                p = jnp.exp(s - m_new)
                l_s[e] = alpha * l_s[e] + jnp.sum(p, axis=1, keepdims=True)
                p_hi = p.astype(BF16)
                p_lo = (p - p_hi.astype(F32)).astype(BF16)
                pv = (jnp.dot(p_hi, v, preferred_element_type=F32) + jnp.dot(p_lo, v, preferred_element_type=F32))
                acc[e] = alpha * acc[e] + pv
                m_s[e] = m_new

        pl.when(j < i)(functools.partial(tile, False))
        pl.when(j == i)(functools.partial(tile, True))

        @pl.when(j == i)
        def _():
            for e in range(2):
                sl = slice(e * dh, (e + 1) * dh)
                o = acc[e] / l_s[e]
                o_ref[:, sl] = o.astype(BF16)
                o32_ref[:, sl] = o
                lse_ref[e] = m_s[e] + jnp.log(l_s[e])

    qs = pl.BlockSpec((t, LANES), lambda h, i, j: (i, h))
    ks = pl.BlockSpec((t, LANES), lambda h, i, j: (jnp.minimum(i, j), h))
    vs = pl.BlockSpec((t, LANES), lambda h, i, j: (jnp.minimum(i, j), HEAD_PAIRS + h))
    cs = pl.BlockSpec((2, 1, t), lambda h, i, j: (h, 0, jnp.minimum(i, j)))
    return _pcall(
        body, (q, kv, kv, ck), grid=(HEAD_PAIRS, n, n), in_specs=[qs, ks, vs, cs],
        out_specs=[qs, qs, pl.BlockSpec((2, t, 1), lambda h, i, j: (h, i, 0))],
        out_shape=[jax.ShapeDtypeStruct((L, D), BF16), jax.ShapeDtypeStruct((L, D), F32),
                   jax.ShapeDtypeStruct((N_HEADS, L, 1), F32)],
        scratch_shapes=[pltpu.VMEM((2, t, 1), F32), pltpu.VMEM((2, t, 1), F32), pltpu.VMEM((2, t, dh), F32)],
        sem=("parallel", "parallel", "arbitrary"), name=name, xfers=xfers)


def _attn_bwd(q, kv, ck, o, do, lse, *, name, xfers=()):
    L, D = q.shape
    t = _tile(L, ATTN_TILE)
    n = L // t
    dh = HEAD_DIM

    def body(i_tab, j_tab, q_ref, k_ref, v_ref, ck_ref, o_ref, do_ref, lse_ref, dq_ref, dk_ref, dv_ref, dck_ref):
        i, j = i_tab[pl.program_id(1)], j_tab[pl.program_id(1)]

        @pl.when(pl.program_id(1) == 0)
        def _():
            dq_ref[...] = jnp.zeros_like(dq_ref)

        @pl.when(i == j)
        def _():
            dk_ref[...] = jnp.zeros_like(dk_ref)
            dv_ref[...] = jnp.zeros_like(dv_ref)
            dck_ref[...] = jnp.zeros_like(dck_ref)

        def tile(masked):
            rows = pl.ds(pl.multiple_of(i * t, t), t)
            for e in range(2):
                sl = slice(e * dh, (e + 1) * dh)
                qs = q_ref[:, sl] * ATTN_SCALE
                k = k_ref[:, sl]
                do = do_ref[:, sl]
                s = _attn_logits(qs, k, ck_ref[e], masked, t)
                p = jnp.exp(s - lse_ref[e])
                dp = lax.dot_general(do, v_ref[:, sl], _NT, preferred_element_type=F32)
                delta = jnp.sum(do.astype(F32) * o_ref[:, sl], axis=1, keepdims=True)
                ds = p * (dp - delta)
                ds16 = ds.astype(BF16)
                dv_ref[:, sl] += lax.dot_general(p.astype(BF16), do, _TN, preferred_element_type=F32)
                dk_ref[:, sl] += lax.dot_general(ds16, qs, _TN, preferred_element_type=F32)
                dck_ref[e] -= jnp.sum(ds, axis=0, keepdims=True)
                dq_ref[rows, sl] += jnp.dot(ds16, k, preferred_element_type=F32) * ATTN_SCALE

        pl.when(i > j)(functools.partial(tile, False))
        pl.when(i == j)(functools.partial(tile, True))

    qs = pl.BlockSpec((t, LANES), lambda h, s, it, jt: (it[s], h))
    ks = pl.BlockSpec((t, LANES), lambda h, s, it, jt: (jt[s], h))
    vs = pl.BlockSpec((t, LANES), lambda h, s, it, jt: (jt[s], HEAD_PAIRS + h))
    cs = pl.BlockSpec((2, 1, t), lambda h, s, it, jt: (h, 0, jt[s]))
    ls = pl.BlockSpec((2, t, 1), lambda h, s, it, jt: (h, it[s], 0))
    full = jax.ShapeDtypeStruct((L, D), F32)
    tabs = _causal_tiles(n, by_row=False)
    return _pcall(
        body, (q, kv, kv, ck, o, do, lse), grid=(HEAD_PAIRS, tabs[0].shape[0]), in_specs=[qs, ks, vs, cs, qs, qs, ls],
        out_specs=[pl.BlockSpec((L, LANES), lambda h, s, it, jt: (0, h)), ks, ks, cs],
        out_shape=[full, full, full, jax.ShapeDtypeStruct((N_HEADS, 1, L), F32)],
        sem=("parallel", "arbitrary"), name=name, xfers=xfers, prefetch=tabs)


SHARD_COLS_FFN = 2 * D_FF // N_DEV
SHARD_ROWS_FFN = D_FF // N_DEV
SHARD_COLS_GLU = 2 * D_MODEL // N_DEV
SHARD_ROWS_QO = D_MODEL // N_DEV
SHARD_COLS_KVF = (2 * D_MODEL + N_HEADS) // N_DEV
S5_NAMES = ("lam_re", "lam_im", "log_dt", "ssm_b_re", "ssm_b_im", "ssm_c_re", "ssm_c_im")
REPL_LATE_ROWS = 288
REPL_EARLY_ROWS = 320


def _leaves(parts):
    out = []
    for p in parts:
        out.extend(_leaves(p) if isinstance(p, (list, tuple)) else [p.reshape(-1)])
    return out


def _pack_rows(parts, rows):
    flat = jnp.concatenate(_leaves(parts))
    return jnp.pad(flat, (0, rows * D_MODEL - flat.shape[0])).reshape(rows, D_MODEL)


def _unpack_rows(flat, like):
    flat, out, off = flat.reshape(-1), [], 0
    for p in _leaves(like):
        out.append(flat[off:off + p.shape[0]])
        off += p.shape[0]
    return out


def _repl_late(d):
    return [d["g_mix"][0], [d[n][0] for n in S5_NAMES]]


def _repl_early(d):
    return [list(d["g_mix"][1:]), list(d["g_ffn"]), [d[n][1] for n in S5_NAMES], d["g_kv"], d["b_f"],
            list(d["ffn_conv_b"]), d["g_final"]]


def _kvf_blocks(full):
    return full.reshape(D_MODEL, N_DEV, SHARD_COLS_KVF).transpose(1, 0, 2)


class _Step:
    def __init__(self, weights, send=None, plan=None):
        self.w = dict(weights)
        self.send = send or {}
        self.grad = {}
        self.bcast = {}
        self.slots = {}
        self.plan = plan or {}

    def xfers(self, host):
        src = {"w": self.send, "g": self.grad, "b": self.bcast}
        return [(src[kind][k], kind == "g") for kind, k in self.plan.get(host, ())]

    def land(self, host, gathered):
        for (kind, k), g in zip(self.plan.get(host, ()), gathered):
            if kind == "w":
                self.arrive(k, g)
            else:
                self.slots[k] = g

    def arrive(self, k, g):
        name = k[0]
        if name == "w_ffn_in" and len(k) == 3:
            self.w[k] = g
            halves = [self.w.get((name, k[1], h)) for h in range(2)]
            if all(h is not None for h in halves):
                self.w[name, k[1]] = jnp.concatenate(halves, axis=1)
        elif name == "w_ffn_out":
            self.w[k] = g.reshape(4, 2 * SHARD_ROWS_FFN, D_MODEL)
        elif name in ("w_q", "w_o"):
            self.w[k] = g.reshape(D_MODEL, D_MODEL)
        elif name == "w_kvf":
            full = g.transpose(1, 0, 2).reshape(D_MODEL, N_DEV * SHARD_COLS_KVF)
            self.w["w_kv",] = full[:, :2 * D_MODEL]
            self.w["w_f",] = jnp.pad(full[:, 2 * D_MODEL:], ((0, 0), (0, LANES - N_HEADS)))
        elif name == "small":
            flat = g.reshape(N_DEV, -1)
            self.w["ssm_d",] = flat[:, :256].reshape(N_DEV, N_A, LANES).transpose(1, 0, 2).reshape(N_A, D_MODEL)
            cw = flat[:, 256:256 + DEPTH * 3 * SHARD_COLS_FFN].reshape(N_DEV, DEPTH, 3, SHARD_COLS_FFN)
            for layer in range(DEPTH):
                self.w["conv_w", layer] = cw[:, layer].reshape(2, 4, 3, SHARD_COLS_FFN)
        else:
            self.w[k] = g

    def run(self, host, fn, *args, **kw):
        xf = self.xfers(host)
        res = fn(*args, name=host, xfers=xf, **kw)
        if not xf:
            return res[0] if isinstance(res, (list, tuple)) and len(res) == 1 else res
        n_own = len(res) - len(xf)
        self.land(host, res[n_own:])
        return res[0] if n_own == 1 else res[:n_own]


def _step(x, target, S):
    L = x.shape[0]
    W = S.w
    vec = lambda a: a.reshape(1, -1)
    CF = SHARD_COLS_FFN

    h = x
    saved = []
    kvs = None
    for layer in range(DEPTH):
        t = str(layer)
        if layer < N_A:
            (a1, a2, wb, wc), prep_vjp = jax.vjp(_s5_prep, *[W[n][layer] for n in S5_NAMES])
            wb16, wc16 = wb.astype(BF16), wc.astype(BF16)
            hn = _rms_fwd(h, vec(W["g_mix"][layer]), name="mix_norm" + t)
            dvec = vec(W["ssm_d",][layer])
            ypre, yg, sb = S.run("s5_fwd" + t, _s5_fwd, hn, wb16, wc16, a1, a2, dvec)
            z = S.run("glu_mm" + t, _mm, yg, W["w_glu", layer], bk="bkn", ok="bmn")
            z = z.reshape(2, 4, L, SHARD_COLS_GLU)
            h1, hn2 = _glu_res_rms(z, h, vec(W["g_ffn"][layer]), name="glu_res" + t)
            mix_saved = (h, hn, ypre, yg, sb, z, a1, a2, wb16, wc16, dvec, prep_vjp)
        else:
            j = layer - N_A
            if layer == N_A:
                hkv = _rms_fwd(h, vec(W["g_kv"]), name="kv_norm")
                kvm = S.run("kv_mm", _mm, hkv, W["w_kv",], out_dtype=BF16)
                fl = S.run("f_mm", _mm, hkv, W["w_f",])
                cum = _fgate_fwd(fl, W["b_f_pad",], name="fgate_fwd")
                ck = cum[:, :N_HEADS].T.reshape(N_HEADS, 1, L)
                kvs = (h, hkv, fl, kvm, ck)
            _, _, _, kvm, ck = kvs
            hn = _rms_fwd(h, vec(W["g_mix"][layer]), name="mix_norm" + t)
            q = S.run("q_mm" + t, _mm, hn, W["w_q", j], out_dtype=BF16)
            o, o32, lse = S.run("attn_fwd" + t, _attn_fwd, q, kvm, ck)
            h1 = S.run("o_mm" + t, _mm, o, W["w_o", j], add=h)
            hn2 = _rms_fwd(h1, vec(W["g_ffn"][layer]), name="ffn_norm" + t)
            mix_saved = (h, hn, q, o32, o, lse)
        u0 = S.run("ffn_in" + t, _mm, hn2, W["w_ffn_in", layer], bk="nbk", ok="bmn", tm=2048).reshape(2, 4, L, CF)
        a = S.run("ffn_act" + t, _conv_act, u0, W["conv_w", layer], W["conv_b", layer])
        h2 = S.run("ffn_out" + t, _mm, a, W["w_ffn_out", layer], ak="bmk", bk="kbn", add=h1)
        saved.append((mix_saved, h1, hn2, u0))
        h = h2

    loss, dh, dg_final = _loss_head(h, vec(W["g_final"]), target, name="loss_head")
    g = {"g_final": dg_final.reshape(-1)}
    gl = {k: [None] * DEPTH for k in ("g_mix", "g_ffn", "conv_w", "ffn_conv_b")}
    ga = {k: [None] * N_A for k in S5_NAMES + ("ssm_d",)}
    dk = dv = dck = None
    for layer in reversed(range(DEPTH)):
        t = str(layer)
        mix_saved, h1, hn2, u0 = saved[layer]
        cw, cb = W["conv_w", layer], W["conv_b", layer]
        da = S.run("ffn_da" + t, _mm, dh, W["w_ffn_out", layer], bk="nbk", ok="bmn", tm=2048)
        a, du0, dcw, dcb = S.run("ffn_conv_bwd" + t, _conv_ffn_bwd, u0, cw, cb, da)
        dw_out = S.run("ffn_dwout" + t, _mm, a, dh, ak="bkm", ok="mbn", out_dtype=BF16)
        S.grad["w_ffn_out", layer] = dw_out.reshape(N_DEV, SHARD_ROWS_FFN, D_MODEL)
        du0 = du0.reshape(N_DEV, L, CF)
        S.grad["w_ffn_in", layer] = S.run("ffn_dwin" + t, _mm, du0, hn2, ak="bkm", ok="mbn", out_dtype=BF16)
        dhn2 = S.run("ffn_dhn" + t, _mm, du0, W["w_ffn_in", layer], ak="bmk", bk="kbn", tm=2048)
        dh1, dg = _rms_bwd(h1, vec(W["g_ffn"][layer]), dhn2, dh, name="ffn_norm_bwd" + t)
        gl["g_ffn"][layer], gl["conv_w"][layer], gl["ffn_conv_b"][layer] = dg.reshape(-1), dcw, dcb.reshape(-1)
        if layer < N_A:
            hin, hn, ypre, yg, sb, z, a1, a2, wb16, wc16, dvec, prep_vjp = mix_saved
            dz = _glu_bwd(z, dh1, name="glu_bwd" + t).reshape(N_DEV, L, SHARD_COLS_GLU)
            S.grad["w_glu", layer] = S.run("glu_dw" + t, _mm, yg, dz, ak="km", bk="bkn", ok="bmn", out_dtype=BF16)
            dyg = S.run("glu_dy" + t, _mm, dz, W["w_glu", layer], ak="bmk", bk="bnk", out_dtype=BF16)
            if layer == 0:
                S.bcast["repl_early",] = _pack_rows(_repl_early({**g, **gl, **ga}), REPL_EARLY_ROWS)
            du, dwb, dwc, da1, da2, dd = S.run("s5_bwd" + t, _s5_bwd, hn, dyg, ypre, sb, wb16, wc16, a1, a2, dvec)
            for nme, val in zip(S5_NAMES, prep_vjp((da1, da2, dwb, dwc))):
                ga[nme][layer] = val
            ga["ssm_d"][layer] = dd.reshape(-1)
            dh, dg = _rms_bwd(hin, vec(W["g_mix"][layer]), du, dh1, name="mix_norm_bwd" + t)
        else:
            j = layer - N_A
            hin, hn, q, o32, o, lse = mix_saved
            _, _, _, kvm, ck = kvs
            S.grad["w_o", j] = S.run("o_dw" + t, _mm, o, dh1, ak="km", out_dtype=BF16
                                     ).reshape(N_DEV, SHARD_ROWS_QO, D_MODEL)
            do = S.run("o_dx" + t, _mm, dh1, W["w_o", j], bk="nk", out_dtype=BF16)
            dq, dk_l, dv_l, dck_l = S.run("attn_bwd" + t, _attn_bwd, q, kvm, ck, o32, do, lse)
            dk = dk_l if dk is None else dk + dk_l
            dv = dv_l if dv is None else dv + dv_l
            dck = dck_l if dck is None else dck + dck_l
            S.grad["w_q", j] = S.run("q_dw" + t, _mm, hn, dq, ak="km", out_dtype=BF16
                                     ).reshape(N_DEV, SHARD_ROWS_QO, D_MODEL)
            dhn = S.run("q_dx" + t, _mm, dq, W["w_q", j], bk="nk")
            dh, dg = _rms_bwd(hin, vec(W["g_mix"][layer]), dhn, dh1, name="mix_norm_bwd" + t)
            if layer == N_A:
                hkv_in, hkv, fl, _, _ = kvs
                dcum = jnp.pad(dck.reshape(N_HEADS, L).T, ((0, 0), (0, LANES - N_HEADS)))
                dfl, dbf = _fgate_bwd(fl, W["b_f_pad",], dcum, name="fgate_bwd")
                dkv = jnp.concatenate([dk, dv], axis=1).astype(BF16)
                dfl16 = dfl.astype(BF16)
                dw_kv = S.run("kv_dw", _mm, hkv, dkv, ak="km")
                dw_f = S.run("f_dw", _mm, hkv, dfl16, ak="km")
                S.grad["w_kvf",] = _kvf_blocks(jnp.concatenate([dw_kv, dw_f[:, :N_HEADS]], axis=1)).astype(BF16)
                dhkv = S.run("kv_dx", _mm, dkv, W["w_kv",], bk="nk")
                dhkv = S.run("f_dx", _mm, dfl16, W["w_f",], bk="nk", add=dhkv)
                g["b_f"] = dbf[0, :N_HEADS]
                dh, dgkv = _rms_bwd(hkv_in, vec(W["g_kv"]), dhkv, dh, name="kv_norm_bwd")
                g["g_kv"] = dgkv.reshape(-1)
        gl["g_mix"][layer] = dg.reshape(-1)

    for d in (gl, ga):
        for k, v in d.items():
            g[k] = jnp.stack(v)
    return loss, dh, g


def _adamw_layers(slots, w, m, v, *, name):
    shape = w.shape
    nl = len(slots)
    w, m, v = (a.reshape((nl,) + a.shape[-2:]) for a in (w, m, v))
    _, R, C = w.shape
    tr = _tile(R, 256, 16)
    c1 = 1.0 / (1.0 - ADAM_B1 ** ADAM_STEP)
    c2 = 1.0 / (1.0 - ADAM_B2 ** ADAM_STEP)

    def body(*refs):
        s_refs, (w_ref, m_ref, v_ref), (g_ref, d_ref, nm_ref, nv_ref) = refs[:nl], refs[nl:nl + 3], refs[nl + 3:]
        for layer in range(nl):
            @pl.when(pl.program_id(0) == layer)
            def _(s_ref=s_refs[layer]):
                g = s_ref[0].astype(F32)
                for d in range(1, N_DEV):
                    g = g + s_ref[d].astype(F32)
                m2 = ADAM_B1 * m_ref[...] + (1.0 - ADAM_B1) * g
                v2 = ADAM_B2 * v_ref[...] + (1.0 - ADAM_B2) * (g * g)
                g_ref[...] = g
                nm_ref[...] = m2
                nv_ref[...] = v2
                d_ref[...] = -ADAM_LR * ((m2 * c1) / (jnp.sqrt(v2 * c2) + ADAM_EPS) + ADAM_WD * w_ref[...])

    def slab_spec(layer):
        return pl.BlockSpec((N_DEV, tr, C), lambda l, i: (0, jnp.where(l == layer, i, 0), 0))

    row = pl.BlockSpec((None, tr, C), lambda l, i: (l, i, 0))
    out = jax.ShapeDtypeStruct((nl, R, C), F32)
    outs = pl.pallas_call(
        body, grid=(nl, R // tr), in_specs=[slab_spec(layer) for layer in range(nl)] + [row, row, row],
        out_specs=[row, row, row, row], out_shape=[out, out, out, out],
        compiler_params=_params("arbitrary", "arbitrary"), name=name)(*slots, w, m, v)
    return [o.reshape(shape) for o in outs]


_SMALL_ROWS = 72
_ORDER = ("g_mix", "g_ffn", "lam_re", "lam_im", "log_dt", "ssm_b_re", "ssm_b_im", "ssm_c_re", "ssm_c_im", "ssm_d",
          "w_glu", "g_kv", "w_kvf", "b_f", "w_q", "w_o", "w_ffn_in", "ffn_conv_w", "ffn_conv_b", "w_ffn_out", "g_final")


def _pack_small(ssm_d, conv_w):
    flat = jnp.concatenate([ssm_d.reshape(-1), conv_w.reshape(-1)])
    return jnp.pad(flat, (0, _SMALL_ROWS * LANES - flat.shape[0])).reshape(_SMALL_ROWS, LANES)


def _unpack_small(flat):
    flat = flat.reshape(-1)
    return flat[:256].reshape(2, 128), flat[256:256 + 8448].reshape(4, 3, 704)


_FWD_PLAN = {
    "start": [("small",)],
    "s5_fwd0": [("w_glu", 0), ("w_ffn_in", 0, 0)],
    "glu_mm0": [("w_ffn_in", 0, 1)],
    "ffn_in0": [("w_ffn_out", 0)],
    "ffn_act0": [("w_glu", 1)],
    "ffn_out0": [("w_ffn_in", 1, 0)],
    "s5_fwd1": [("w_ffn_in", 1, 1)],
    "ffn_in1": [("w_ffn_out", 1)],
    "ffn_act1": [("w_kvf",)],
    "ffn_out1": [("w_q", 0), ("w_o", 0)],
    "attn_fwd2": [("w_ffn_in", 2), ("w_ffn_out", 2), ("w_q", 1), ("w_o", 1), ("w_ffn_in", 3), ("w_ffn_out", 3)],
}
_BWD_PLAN = {
    "ffn_dhn3": [("w_ffn_out", 3)],
    "attn_bwd3": [("w_ffn_in", 3), ("w_o", 1)],
    "ffn_conv_bwd2": [("w_q", 1)],
    "ffn_dhn2": [("w_ffn_out", 2)],
    "attn_bwd2": [("w_ffn_in", 2), ("w_o", 0)],
    "ffn_conv_bwd1": [("w_q", 0), ("w_kvf",)],
    "ffn_dhn1": [("w_ffn_out", 1)],
    "s5_bwd1": [("w_ffn_in", 1), ("w_glu", 1)],
    "ffn_dhn0": [("w_ffn_out", 0)],
    "glu_dy0": [("w_glu", 0)],
    "s5_bwd0": [("w_ffn_in", 0), ("repl_early",)],
    "end": [("small",), ("repl_late",)],
}
_PLAN = {h: [("w", k) for k in ks] for h, ks in _FWD_PLAN.items()}
_PLAN.update({h: [("b" if k[0].startswith("repl") else "g", k) for k in ks] for h, ks in _BWD_PLAN.items()})


def kernel(x, g_mix, g_ffn, lam_re, lam_im, log_dt, ssm_b_re, ssm_b_im, ssm_c_re, ssm_c_im, ssm_d, w_glu, g_kv, w_kvf, b_f, w_q, w_o, w_ffn_in, ffn_conv_w, ffn_conv_b, w_ffn_out, g_final, loss_target, m_g_mix, m_g_ffn, m_lam_re, m_lam_im, m_log_dt, m_ssm_b_re, m_ssm_b_im, m_ssm_c_re, m_ssm_c_im, m_ssm_d, m_w_glu, m_g_kv, m_w_kvf, m_b_f, m_w_q, m_w_o, m_w_ffn_in, m_ffn_conv_w, m_ffn_conv_b, m_w_ffn_out, m_g_final, v_g_mix, v_g_ffn, v_lam_re, v_lam_im, v_log_dt, v_ssm_b_re, v_ssm_b_im, v_ssm_c_re, v_ssm_c_im, v_ssm_d, v_w_glu, v_g_kv, v_w_kvf, v_b_f, v_w_q, v_w_o, v_w_ffn_in, v_ffn_conv_w, v_ffn_conv_b, v_w_ffn_out, v_g_final):
    wts = dict(g_mix=g_mix, g_ffn=g_ffn, lam_re=lam_re, lam_im=lam_im, log_dt=log_dt, ssm_b_re=ssm_b_re,
               ssm_b_im=ssm_b_im, ssm_c_re=ssm_c_re, ssm_c_im=ssm_c_im, ssm_d=ssm_d, w_glu=w_glu, g_kv=g_kv,
               w_kvf=w_kvf, b_f=b_f, w_q=w_q, w_o=w_o, w_ffn_in=w_ffn_in, ffn_conv_w=ffn_conv_w,
               ffn_conv_b=ffn_conv_b, w_ffn_out=w_ffn_out, g_final=g_final)
    mom = dict(g_mix=m_g_mix, g_ffn=m_g_ffn, lam_re=m_lam_re, lam_im=m_lam_im, log_dt=m_log_dt, ssm_b_re=m_ssm_b_re,
               ssm_b_im=m_ssm_b_im, ssm_c_re=m_ssm_c_re, ssm_c_im=m_ssm_c_im, ssm_d=m_ssm_d, w_glu=m_w_glu,
               g_kv=m_g_kv, w_kvf=m_w_kvf, b_f=m_b_f, w_q=m_w_q, w_o=m_w_o, w_ffn_in=m_w_ffn_in,
               ffn_conv_w=m_ffn_conv_w, ffn_conv_b=m_ffn_conv_b, w_ffn_out=m_w_ffn_out, g_final=m_g_final)
    var = dict(g_mix=v_g_mix, g_ffn=v_g_ffn, lam_re=v_lam_re, lam_im=v_lam_im, log_dt=v_log_dt, ssm_b_re=v_ssm_b_re,
               ssm_b_im=v_ssm_b_im, ssm_c_re=v_ssm_c_re, ssm_c_im=v_ssm_c_im, ssm_d=v_ssm_d, w_glu=v_w_glu,
               g_kv=v_g_kv, w_kvf=v_w_kvf, b_f=v_b_f, w_q=v_w_q, w_o=v_w_o, w_ffn_in=v_w_ffn_in,
               ffn_conv_w=v_ffn_conv_w, ffn_conv_b=v_ffn_conv_b, w_ffn_out=v_w_ffn_out, g_final=v_g_final)
    kinds = ("grad", "delta", "m", "v")

    ready = {n: wts[n] for n in ("g_mix", "g_ffn", "g_kv", "g_final") + S5_NAMES}
    ready["b_f_pad",] = jnp.pad(b_f, (0, LANES - N_HEADS)).reshape(1, LANES)
    send = {("small",): _pack_small(ssm_d, ffn_conv_w), ("w_kvf",): w_kvf.astype(BF16)}
    for layer in range(DEPTH):
        ready["conv_b", layer] = ffn_conv_b[layer].reshape(2, 4, 1, SHARD_COLS_FFN)
        w_in_t = jnp.swapaxes(w_ffn_in[layer], 0, 1).astype(BF16)
        if ("w_ffn_in", layer) in [k for ks in _FWD_PLAN.values() for k in ks]:
            send["w_ffn_in", layer] = w_in_t
        else:
            half = SHARD_COLS_FFN // 2
            send["w_ffn_in", layer, 0] = w_in_t[:half]
            send["w_ffn_in", layer, 1] = w_in_t[half:]
        send["w_ffn_out", layer] = w_ffn_out[layer].astype(BF16)
    for layer in range(N_A):
        send["w_glu", layer] = w_glu[layer].astype(BF16)
        send["w_q", layer] = w_q[layer].astype(BF16)
        send["w_o", layer] = w_o[layer].astype(BF16)

    S = _Step(ready, send, _PLAN)
    S.land("start", _exchange(S.xfers("start"), name="start"))
    loss, dx, g = _step(x[0], loss_target[0], S)
    loss = lax.psum(loss[0, 0], MESH_AXES)

    g_d = g["ssm_d"].reshape(N_A, N_DEV, LANES).transpose(1, 0, 2).reshape(N_DEV, N_A * LANES)
    g_cw = jnp.stack([g["conv_w"][layer].reshape(N_DEV, 3, SHARD_COLS_FFN) for layer in range(DEPTH)], axis=1)
    g_small = jnp.concatenate([g_d, g_cw.reshape(N_DEV, -1)], axis=1)
    g_small = jnp.pad(g_small, ((0, 0), (0, _SMALL_ROWS * LANES - g_small.shape[1])))
    S.grad["small",] = g_small.reshape(N_DEV, _SMALL_ROWS, LANES)
    S.bcast["repl_late",] = _pack_rows(_repl_late(g), REPL_LATE_ROWS)
    S.land("end", _exchange(S.xfers("end"), name="end"))

    res = {}
    for name, nl in (("w_glu", N_A), ("w_q", DEPTH - N_A), ("w_o", DEPTH - N_A), ("w_ffn_in", DEPTH),
                     ("w_ffn_out", DEPTH)):
        view = (lambda a: jnp.swapaxes(a, 1, 2)) if name == "w_ffn_in" else (lambda a: a)
        outs = _adamw_layers([S.slots[name, layer] for layer in range(nl)], view(wts[name]), view(mom[name]),
                             view(var[name]), name="adamw_" + name)
        res.update({(kind, name): view(a) for kind, a in zip(kinds, outs)})
    outs = _adamw_layers([S.slots["w_kvf",]], w_kvf, m_w_kvf, v_w_kvf, name="adamw_w_kvf")
    res.update({(kind, "w_kvf"): a for kind, a in zip(kinds, outs)})
    outs = _adamw_layers([S.slots["small",]], _pack_small(ssm_d, ffn_conv_w), _pack_small(m_ssm_d, m_ffn_conv_w),
                         _pack_small(v_ssm_d, v_ffn_conv_w), name="adamw_small")
    for kind, flat in zip(kinds, outs):
        res[kind, "ssm_d"], res[kind, "ffn_conv_w"] = _unpack_small(flat)

    pieces = {}
    for key, rows, sel in ((("repl_early",), REPL_EARLY_ROWS, _repl_early), (("repl_late",), REPL_LATE_ROWS, _repl_late)):
        outs = _adamw_layers([S.slots[key]], *[_pack_rows(sel(d), rows) for d in (wts, mom, var)],
                             name="adamw_" + key[0])
        for kind, flat in zip(kinds, outs):
            pieces[kind, key[0]] = _unpack_rows(flat, sel(wts))
    for kind in kinds:
        early, late = iter(pieces[kind, "repl_early"]), iter(pieces[kind, "repl_late"])
        take = lambda it, n: [next(it) for _ in range(n)]
        res[kind, "g_mix"] = jnp.stack(take(late, 1) + take(early, DEPTH - 1))
        res[kind, "g_ffn"] = jnp.stack(take(early, DEPTH))
        for n in S5_NAMES:
            res[kind, n] = jnp.stack([next(late), next(early)]).reshape(wts[n].shape)
        res[kind, "g_kv"], res[kind, "b_f"] = next(early), next(early)
        res[kind, "ffn_conv_b"] = jnp.stack(take(early, DEPTH))
        res[kind, "g_final"] = next(early)

    return (loss, dx[None], *[res[kind, n] for kind in kinds for n in _ORDER])
```

```python
import functools
import math

import jax
import jax.numpy as jnp
from jax import lax
from jax.experimental import pallas as pl
from jax.experimental.pallas import tpu as pltpu

F32 = jnp.float32
BF16 = jnp.bfloat16

D_MODEL = 1024
DEPTH = 4
N_A = 2
N_GROUPS = 64
SSM_GROUP = 16
SSM_STATE = 64
N_HEADS = 16
HEAD_DIM = 64
ATTN_SCALE = HEAD_DIM ** -0.5
D_FF = 2816
EPS = 1e-6
N_DEV = 8
LANES = 128
SUBLANES = 8

ADAM_LR = 0.001
ADAM_B1 = 0.9
ADAM_B2 = 0.999
ADAM_EPS = 1e-08
ADAM_WD = 0.01
ADAM_STEP = 10

ROW_TILE = 512
S5_CHUNK = 256
ATTN_TILE = 512
CUM_TILE = 256
NEG = -1e30

MESH_AXES = ("x", "y", "c")


def _tile(n, target, align=LANES):
    t = (min(target, n) // align) * align
    while t >= align:
        if n % t == 0:
            return t
        t -= align
    return n


def _params(*sem):
    return pltpu.CompilerParams(dimension_semantics=sem, vmem_limit_bytes=56 * 1024 * 1024)


_ANY = pl.BlockSpec(memory_space=pl.ANY)
_XFER_SEMS = (pltpu.SemaphoreType.DMA((N_DEV - 1,)), pltpu.SemaphoreType.DMA((N_DEV - 1,)), pltpu.SemaphoreType.DMA)


def _xfer_copies(x_ref, o_ref, send_sems, recv_sems, local_sem, scatter):
    xi, yi, ci = lax.axis_index("x"), lax.axis_index("y"), lax.axis_index("c")
    me = 4 * xi + 2 * yi + ci

    def src(p):
        return x_ref.at[p] if scatter else x_ref

    own = pltpu.make_async_copy(src(me), o_ref.at[me], local_sem)
    sends, recvs = [], []
    for k in range(1, N_DEV):
        px, py, pc = xi ^ (k >> 2), yi ^ ((k >> 1) & 1), ci ^ (k & 1)
        p = 4 * px + 2 * py + pc
        sends.append(pltpu.make_async_remote_copy(
            src_ref=src(p), dst_ref=o_ref.at[me], send_sem=send_sems.at[k - 1], recv_sem=recv_sems.at[k - 1],
            device_id=(px, py, pc), device_id_type=pl.DeviceIdType.MESH))
        recvs.append(pltpu.make_async_remote_copy(
            src_ref=src(p), dst_ref=o_ref.at[p], send_sem=send_sems.at[k - 1], recv_sem=recv_sems.at[k - 1],
            device_id=(px, py, pc), device_id_type=pl.DeviceIdType.MESH))
    return own, sends, recvs


def _xfer_start(*refs, scatter):
    own, sends, _ = _xfer_copies(*refs, scatter)
    own.start()
    for cp in sends:
        cp.start()


def _xfer_wait(*refs, scatter):
    own, sends, recvs = _xfer_copies(*refs, scatter)
    for cp in recvs:
        cp.wait_recv()
    for cp in sends:
        cp.wait_send()
    own.wait()


def _xfer_out(x, scatter):
    return jax.ShapeDtypeStruct((N_DEV,) + (x.shape[1:] if scatter else x.shape), x.dtype)


def _pcall(body, args, *, grid, in_specs, out_specs, out_shape, scratch_shapes=(), sem, name, xfers=(), prefetch=()):
    out_specs, out_shape = list(out_specs), list(out_shape)
    n_pre, n_in, n_out, n_x, n_scr = len(prefetch), len(in_specs), len(out_specs), len(xfers), len(scratch_shapes)
    flags = [s for _, s in xfers]

    def wrapped(*refs):
        pre, refs = refs[:n_pre], refs[n_pre:]
        ins, xin = refs[:n_in], refs[n_in:n_in + n_x]
        outs = refs[n_in + n_x:n_in + n_x + n_out]
        xout = refs[n_in + n_x + n_out:n_in + 2 * n_x + n_out]
        scr = refs[n_in + 2 * n_x + n_out:]
        own, sems = scr[:n_scr], scr[n_scr:]
        ids = [pl.program_id(d) for d in range(len(grid))]
        first = functools.reduce(jnp.logical_and, [i == 0 for i in ids])
        last = functools.reduce(jnp.logical_and, [i == g - 1 for i, g in zip(ids, grid)])

        @pl.when(first)
        def _():
            for t in range(n_x):
                _xfer_start(xin[t], xout[t], *sems[3 * t:3 * t + 3], scatter=flags[t])

        body(*pre, *ins, *outs, *own)

        @pl.when(last)
        def _():
            for t in range(n_x):
                _xfer_wait(xin[t], xout[t], *sems[3 * t:3 * t + 3], scatter=flags[t])

    grid_spec = pltpu.PrefetchScalarGridSpec(
        num_scalar_prefetch=n_pre, grid=grid, in_specs=list(in_specs) + [_ANY] * n_x,
        out_specs=out_specs + [_ANY] * n_x, scratch_shapes=list(scratch_shapes) + list(_XFER_SEMS) * n_x)
    return pl.pallas_call(
        wrapped if xfers else body, grid_spec=grid_spec, out_shape=out_shape + [_xfer_out(x, s) for x, s in xfers],
        compiler_params=_params(*(["arbitrary"] * len(grid) if xfers else sem)), name=name,
    )(*prefetch, *args, *[x for x, _ in xfers])


def _exchange(xfers, *, name):
    def body():
        pass

    return _pcall(body, (), grid=(1,), in_specs=[], out_specs=[], out_shape=[], sem=("arbitrary",), name=name,
                  xfers=xfers)


def _mm(a, b, *, ak="mk", bk="kn", ok="mn", add=None, out_dtype=F32, tm=1024, tn=1024, tk=1024, name, xfers=()):
    sa, sb = a.shape, b.shape
    fm = fn = fk = None
    if ak == "mk":
        M, K, a_c = sa[0], sa[1], 1
    elif ak == "km":
        K, M, a_c = sa[0], sa[1], 0
    elif ak == "bmk":
        M, K, a_c, fk = sa[1], sa[0] * sa[2], 1, sa[2]
    else:
        K, M, a_c, fm = sa[1], sa[0] * sa[2], 0, sa[2]
    if bk == "kn":
        N, b_c = sb[1], 0
    elif bk == "nk":
        N, b_c = sb[0], 1
    elif bk == "bkn":
        N, b_c, fn = sb[0] * sb[2], 0, sb[2]
    elif bk == "bnk":
        N, b_c, fk = sb[1], 1, sb[2]
    elif bk == "kbn":
        N, b_c, fk = sb[2], 0, sb[1]
    else:
        N, b_c, fn = sb[0] * sb[1], 1, sb[1]
    tm, tn, tk = fm or _tile(M, tm), fn or _tile(N, tn), fk or _tile(K, tk)
    nm, nn, nk = M // tm, N // tn, K // tk

    a_spec = {"mk": pl.BlockSpec((tm, tk), lambda i, j, k: (i, k)),
              "km": pl.BlockSpec((tk, tm), lambda i, j, k: (k, i)),
              "bmk": pl.BlockSpec((None, tm, tk), lambda i, j, k: (k, i, 0)),
              "bkm": pl.BlockSpec((None, tk, tm), lambda i, j, k: (i, k, 0))}[ak]
    b_spec = {"kn": pl.BlockSpec((tk, tn), lambda i, j, k: (k, j)),
              "nk": pl.BlockSpec((tn, tk), lambda i, j, k: (j, k)),
              "bkn": pl.BlockSpec((None, tk, tn), lambda i, j, k: (j, k, 0)),
              "bnk": pl.BlockSpec((None, tn, tk), lambda i, j, k: (k, j, 0)),
              "kbn": pl.BlockSpec((None, tk, tn), lambda i, j, k: (k, 0, j)),
              "nbk": pl.BlockSpec((None, tn, tk), lambda i, j, k: (j, 0, k))}[bk]
    if ok == "mn":
        o_spec = pl.BlockSpec((tm, tn), lambda i, j, k: (i, j))
        out_shape = jax.ShapeDtypeStruct((M, N), out_dtype)
    elif ok == "bmn":
        o_spec = pl.BlockSpec((None, tm, tn), lambda i, j, k: (j, i, 0))
        out_shape = jax.ShapeDtypeStruct((nn, M, tn), out_dtype)
    else:
        o_spec = pl.BlockSpec((None, tm, tn), lambda i, j, k: (i, 0, j))
        out_shape = jax.ShapeDtypeStruct((nm, tm, N), out_dtype)
    dims = (((a_c,), (b_c,)), ((), ()))
    has_add = add is not None

    def body(*refs):
        a_ref, b_ref = refs[0], refs[1]
        add_ref = refs[2] if has_add else None
        o_ref = refs[3] if has_add else refs[2]
        part = lax.dot_general(a_ref[...].astype(BF16), b_ref[...].astype(BF16), dims, preferred_element_type=F32)

        def finish(r):
            if has_add:
                r = r + add_ref[...]
            o_ref[...] = r.astype(out_dtype)

        if nk == 1:
            finish(part)
            return
        acc = refs[-1]
        k = pl.program_id(2)

        @pl.when(k == 0)
        def _():
            acc[...] = part

        @pl.when(k > 0)
        def _():
            acc[...] += part

        @pl.when(k == nk - 1)
        def _():
            finish(acc[...])

    in_specs = [a_spec, b_spec]
    args = [a, b]
    if has_add:
        in_specs.append(pl.BlockSpec((tm, tn), lambda i, j, k: (i, j)))
        args.append(add)
    res = _pcall(body, args, grid=(nm, nn, nk), in_specs=in_specs, out_specs=[o_spec], out_shape=[out_shape],
                 scratch_shapes=[pltpu.VMEM((tm, tn), F32)] if nk > 1 else [],
                 sem=("parallel", "parallel", "arbitrary"), name=name, xfers=xfers)
    return res if xfers else res[0]


def _rms_fwd(h, g, *, name):
    L, D = h.shape
    tr = _tile(L, ROW_TILE, SUBLANES)

    def body(h_ref, g_ref, o_ref):
        x = h_ref[...]
        r = lax.rsqrt(jnp.mean(x * x, axis=1, keepdims=True) + EPS)
        o_ref[...] = (x * r * g_ref[...]).astype(BF16)

    return pl.pallas_call(
        body, grid=(L // tr,),
        in_specs=[pl.BlockSpec((tr, D), lambda i: (i, 0)), pl.BlockSpec((1, D), lambda i: (0, 0))],
        out_specs=pl.BlockSpec((tr, D), lambda i: (i, 0)), out_shape=jax.ShapeDtypeStruct((L, D), BF16),
        compiler_params=_params("parallel"), name=name)(h, g)


def _rms_bwd(h, g, dy, dres, *, name):
    L, D = h.shape
    tr = _tile(L, ROW_TILE, SUBLANES)

    def body(h_ref, g_ref, dy_ref, dres_ref, dh_ref, dg_ref):
        @pl.when(pl.program_id(0) == 0)
        def _():
            dg_ref[...] = jnp.zeros_like(dg_ref)

        x = h_ref[...]
        r = lax.rsqrt(jnp.mean(x * x, axis=1, keepdims=True) + EPS)
        xn = x * r
        dy = dy_ref[...].astype(F32)
        gdy = dy * g_ref[...]
        dx = r * (gdy - xn * jnp.mean(gdy * xn, axis=1, keepdims=True))
        dh_ref[...] = dres_ref[...] + dx
        dg_ref[...] += jnp.sum(dy * xn, axis=0, keepdims=True)

    row = pl.BlockSpec((tr, D), lambda i: (i, 0))
    vec = pl.BlockSpec((1, D), lambda i: (0, 0))
    return pl.pallas_call(
        body, grid=(L // tr,), in_specs=[row, vec, row, row], out_specs=[row, vec],
        out_shape=[jax.ShapeDtypeStruct((L, D), F32), jax.ShapeDtypeStruct((1, D), F32)],
        compiler_params=_params("arbitrary"), name=name)(h, g, dy, dres)


def _glu_res_rms(z, h, g, *, name):
    L, D = h.shape
    nb, cb = z.shape[1], z.shape[3]
    tr = _tile(L, ROW_TILE, SUBLANES)

    def body(z_ref, h_ref, g_ref, h1_ref, hn_ref):
        za = jnp.concatenate([z_ref[0, d] for d in range(nb)], axis=1)
        zg = jnp.concatenate([z_ref[1, d] for d in range(nb)], axis=1)
        x = h_ref[...] + za * jax.nn.sigmoid(zg)
        h1_ref[...] = x
        r = lax.rsqrt(jnp.mean(x * x, axis=1, keepdims=True) + EPS)
        hn_ref[...] = (x * r * g_ref[...]).astype(BF16)

    row = pl.BlockSpec((tr, D), lambda i: (i, 0))
    return pl.pallas_call(
        body, grid=(L // tr,),
        in_specs=[pl.BlockSpec((2, nb, tr, cb), lambda i: (0, 0, i, 0)), row, pl.BlockSpec((1, D), lambda i: (0, 0))],
        out_specs=[row, row],
        out_shape=[jax.ShapeDtypeStruct((L, D), F32), jax.ShapeDtypeStruct((L, D), BF16)],
        compiler_params=_params("parallel"), name=name)(z, h, g)


def _glu_bwd(z, dout, *, name):
    L, D = dout.shape
    nb, cb = z.shape[1], z.shape[3]
    tr = _tile(L, ROW_TILE, SUBLANES)

    def body(z_ref, d_ref, o_ref):
        for d in range(nb):
            dd = d_ref[:, d * cb:(d + 1) * cb]
            sg = jax.nn.sigmoid(z_ref[1, d])
            o_ref[0, d] = (dd * sg).astype(BF16)
            o_ref[1, d] = (dd * z_ref[0, d] * sg * (1.0 - sg)).astype(BF16)

    zs = pl.BlockSpec((2, nb, tr, cb), lambda i: (0, 0, i, 0))
    return pl.pallas_call(
        body, grid=(L // tr,), in_specs=[zs, pl.BlockSpec((tr, D), lambda i: (i, 0))], out_specs=zs,
        out_shape=jax.ShapeDtypeStruct(z.shape, BF16),
        compiler_params=_params("parallel"), name=name)(z, dout)


def _loss_head(h, g, target, *, name):
    L, D = h.shape
    tr = _tile(L, ROW_TILE, SUBLANES)

    def body(h_ref, g_ref, t_ref, loss_ref, dh_ref, dg_ref):
        @pl.when(pl.program_id(0) == 0)
        def _():
            dg_ref[...] = jnp.zeros_like(dg_ref)
            loss_ref[...] = jnp.zeros_like(loss_ref)

        x = h_ref[...]
        gg = g_ref[...]
        r = lax.rsqrt(jnp.mean(x * x, axis=1, keepdims=True) + EPS)
        xn = x * r
        err = xn * gg - t_ref[...]
        loss_ref[...] += 0.5 * jnp.sum(jnp.mean(err * err, axis=1, keepdims=True), axis=0, keepdims=True)
        dy = err * (1.0 / D)
        gdy = dy * gg
        dh_ref[...] = r * (gdy - xn * jnp.mean(gdy * xn, axis=1, keepdims=True))
        dg_ref[...] += jnp.sum(dy * xn, axis=0, keepdims=True)

    row = pl.BlockSpec((tr, D), lambda i: (i, 0))
    vec = pl.BlockSpec((1, D), lambda i: (0, 0))
    return pl.pallas_call(
        body, grid=(L // tr,), in_specs=[row, vec, row],
        out_specs=[pl.BlockSpec((1, 1), lambda i: (0, 0)), row, vec],
        out_shape=[jax.ShapeDtypeStruct((1, 1), F32), jax.ShapeDtypeStruct((L, D), F32),
                   jax.ShapeDtypeStruct((1, D), F32)],
        compiler_params=_params("arbitrary"), name=name)(h, g, target)


CONV_ROW_TILE = 256


def _sigmoid(x):
    return pl.reciprocal(1.0 + jnp.exp(-x), approx=True)


def _conv_specs(L, tr, tc):
    nrb = tr // SUBLANES
    before = lambda i: jnp.maximum(i * nrb - 1, 0)
    after = lambda i: jnp.minimum((i + 1) * nrb, L // SUBLANES - 1)
    main = pl.BlockSpec((2, None, tr, tc), lambda j, i: (0, j, i, 0))
    prev = pl.BlockSpec((2, None, SUBLANES, tc), lambda j, i: (0, j, before(i), 0))
    nxt = pl.BlockSpec((2, None, SUBLANES, tc), lambda j, i: (0, j, after(i), 0))
    cw = pl.BlockSpec((2, None, 3, tc), lambda j, i: (0, j, 0, 0))
    cb = pl.BlockSpec((2, None, 1, tc), lambda j, i: (0, j, 0, 0))
    half = pl.BlockSpec((None, tr, tc), lambda j, i: (j, i, 0))
    half_nxt = pl.BlockSpec((None, SUBLANES, tc), lambda j, i: (j, after(i), 0))
    return main, prev, nxt, cw, cb, half, half_nxt


def _conv_rows(xe, w, b):
    x1 = pltpu.roll(xe, 1, 0)
    x2 = pltpu.roll(xe, 2, 0)
    return b + x2 * w[0:1] + x1 * w[1:2] + xe * w[2:3], x1, x2


def _shift_down(x, halo, k, row):
    y = pltpu.roll(x, k, 0)
    for r in range(k):
        y = jnp.where(row == r, halo[SUBLANES - k + r:SUBLANES - k + r + 1, :], y)
    return y


def _conv_act(u0, cw, cb, *, name, xfers=()):
    _, nb, L, tc = u0.shape
    tr = _tile(L, ROW_TILE, SUBLANES)
    main, prev, _, cws, cbs, half, _ = _conv_specs(L, tr, tc)

    def body(u_ref, p_ref, w_ref, b_ref, a_ref):
        first = pl.program_id(1) == 0
        row = lax.broadcasted_iota(jnp.int32, (tr, tc), 0)
        y = []
        for s in range(2):
            x, w = u_ref[s], w_ref[s]
            halo = jnp.where(first, 0.0, p_ref[s])
            x1 = _shift_down(x, halo, 1, row)
            x2 = _shift_down(x, halo, 2, row)
            y.append(b_ref[s] + x2 * w[0:1] + x1 * w[1:2] + x * w[2:3])
        a_ref[...] = (y[0] * _sigmoid(y[0]) * y[1]).astype(BF16)

    return _pcall(body, (u0, u0, cw, cb), grid=(nb, L // tr), in_specs=[main, prev, cws, cbs], out_specs=[half],
                  out_shape=[jax.ShapeDtypeStruct((nb, L, tc), BF16)], sem=("parallel", "parallel"), name=name,
                  xfers=xfers)


def _conv_ffn_bwd(u0, cw, cb, da, *, name, xfers=()):
    _, nb, L, tc = u0.shape
    tr = _tile(L, CONV_ROW_TILE, SUBLANES)
    main, prev, nxt, cws, cbs, half, half_nxt = _conv_specs(L, tr, tc)
    nr = L // tr
    H = SUBLANES

    def body(u_ref, p_ref, n_ref, w_ref, b_ref, da_ref, dan_ref, a_ref, du0_ref, dcw_ref, dcb_ref):
        i = pl.program_id(1)

        @pl.when(i == 0)
        def _():
            dcw_ref[...] = jnp.zeros_like(dcw_ref)
            dcb_ref[...] = jnp.zeros_like(dcb_ref)

        y, x1, x2 = [], [], []
        for s in range(2):
            xe = jnp.concatenate([jnp.where(i == 0, 0.0, p_ref[s]), u_ref[s], n_ref[s]], axis=0)
            ys, x1s, x2s = _conv_rows(xe, w_ref[s], b_ref[s])
            y.append(ys[H:])
            x1.append(x1s[H:H + tr])
            x2.append(x2s[H:H + tr])
        gate, up = y
        da = jnp.concatenate([da_ref[...], dan_ref[...]], axis=0)
        row = lax.broadcasted_iota(jnp.int32, (tr + H, tc), 0)
        da = jnp.where(jnp.logical_and(i == nr - 1, row >= tr), 0.0, da)
        sg = _sigmoid(gate)
        silu = gate * sg
        a_ref[...] = (silu * up)[:tr].astype(BF16)
        d = (da * up * (sg * (1.0 + gate * (1.0 - sg))), da * silu)
        for s in range(2):
            w = w_ref[s]
            d0 = d[s][:tr]
            d1 = pltpu.roll(d[s], tr + H - 1, 0)[:tr]
            d2 = pltpu.roll(d[s], tr + H - 2, 0)[:tr]
            du0_ref[s] = (d0 * w[2:3] + d1 * w[1:2] + d2 * w[0:1]).astype(BF16)
            dcw_ref[s, 0:1, :] += jnp.sum(d0 * x2[s], axis=0, keepdims=True)
            dcw_ref[s, 1:2, :] += jnp.sum(d0 * x1[s], axis=0, keepdims=True)
            dcw_ref[s, 2:3, :] += jnp.sum(d0 * u_ref[s], axis=0, keepdims=True)
            dcb_ref[s] += jnp.sum(d0, axis=0, keepdims=True)

    return _pcall(body, (u0, u0, u0, cw, cb, da, da), grid=(nb, nr),
                  in_specs=[main, prev, nxt, cws, cbs, half, half_nxt], out_specs=[half, main, cws, cbs],
                  out_shape=[jax.ShapeDtypeStruct((nb, L, tc), BF16), jax.ShapeDtypeStruct((2, nb, L, tc), BF16),
                             jax.ShapeDtypeStruct((2, nb, 3, tc), F32), jax.ShapeDtypeStruct((2, nb, 1, tc), F32)],
                  sem=("parallel", "arbitrary"), name=name, xfers=xfers)


N_TILES = 64
HALF = N_TILES // 2


def _swap(s):
    return jnp.concatenate([s[HALF:], s[:HALF]], axis=0)


def _chan_block(j):
    return ((j % HALF) // 4) * LANES


def _pairs_of(jb):
    return [2 * jb, 2 * jb + 1, HALF // 2 + 2 * jb, HALF // 2 + 2 * jb + 1]


_WB_SPEC = pl.BlockSpec((HALF, LANES, 2 * LANES), lambda c: (0, 0, 0))
_WC_SPEC = pl.BlockSpec((HALF, 2 * LANES, LANES), lambda c: (0, 0, 0))
GELU_C = math.sqrt(2.0 / math.pi)
GELU_A = 0.044715


def _gelu(x):
    return 0.5 * x * (1.0 + jnp.tanh(GELU_C * (x + GELU_A * x * x * x)))


def _gelu_grad(x):
    th = jnp.tanh(GELU_C * (x + GELU_A * x * x * x))
    return 0.5 * (1.0 + th) + 0.5 * x * (1.0 - th * th) * GELU_C * (1.0 + 3.0 * GELU_A * x * x)


def _tile_rows(j, T, TP):
    return pl.ds(j * TP + SUBLANES, T)


def _pair(ref, jp, T, TP):
    return jnp.concatenate([ref[_tile_rows(2 * jp, T, TP), :], ref[_tile_rows(2 * jp + 1, T, TP), :]],
                           axis=1).astype(BF16)


def _unpair(ref, jp, val, T, TP):
    ref[_tile_rows(2 * jp, T, TP), :] = val[:, :LANES]
    ref[_tile_rows(2 * jp + 1, T, TP), :] = val[:, LANES:]


def _s5_project_in(u_ref, wb_ref, s3, T, TP):
    for jp in range(HALF):
        blk = _chan_block(2 * jp)
        _unpair(s3, jp, jnp.dot(u_ref[:, blk:blk + LANES], wb_ref[jp], preferred_element_type=F32), T, TP)


def _s5_scan_fwd(s3, a1, a2, s0, T, TP):
    span = (N_GROUPS - 1) * TP + 2 * SUBLANES

    def blk(i, s):
        view = s3.at[pl.ds(pl.multiple_of(i * SUBLANES, SUBLANES), span)]
        for k in range(SUBLANES):
            rows = pl.ds(SUBLANES + k, N_GROUPS, stride=TP)
            s = a1 * s + a2 * _swap(s) + view[rows, :]
            view[rows, :] = s
        return s

    return lax.fori_loop(0, T // SUBLANES, blk, s0)


def _s5_fwd(hn, wb, wc, a1, a2, dvec, *, name, xfers=()):
    L, D = hn.shape
    T = min(S5_CHUNK, L)
    TP = T + SUBLANES
    nC = L // T

    def body(u_ref, wb_ref, wc_ref, a1_ref, a2_ref, d_ref, y_ref, yg_ref, sb_ref, s3, st):
        @pl.when(pl.program_id(0) == 0)
        def _():
            st[...] = jnp.zeros_like(st)

        sb_ref[0] = st[...]
        _s5_project_in(u_ref, wb_ref, s3, T, TP)
        st[...] = _s5_scan_fwd(s3, a1_ref[...], a2_ref[...], st[...], T, TP)
        for jb in range(D // LANES):
            acc = jnp.zeros((T, LANES), F32)
            for jp in _pairs_of(jb):
                acc += jnp.dot(_pair(s3, jp, T, TP), wc_ref[jp], preferred_element_type=F32)
            cols = slice(jb * LANES, (jb + 1) * LANES)
            y = acc + d_ref[:, cols] * u_ref[:, cols].astype(F32)
            y_ref[:, cols] = y
            yg_ref[:, cols] = _gelu(y).astype(BF16)

    row = pl.BlockSpec((T, D), lambda c: (c, 0))
    aspec = pl.BlockSpec((N_GROUPS, LANES), lambda c: (0, 0))
    return _pcall(
        body, (hn, wb, wc, a1, a2, dvec), grid=(nC,),
        in_specs=[row, _WB_SPEC, _WC_SPEC, aspec, aspec, pl.BlockSpec((1, D), lambda c: (0, 0))],
        out_specs=[row, row, pl.BlockSpec((1, N_GROUPS, LANES), lambda c: (c, 0, 0))],
        out_shape=[jax.ShapeDtypeStruct((L, D), F32), jax.ShapeDtypeStruct((L, D), BF16),
                   jax.ShapeDtypeStruct((nC, N_GROUPS, LANES), F32)],
        scratch_shapes=[pltpu.VMEM((N_GROUPS * TP, LANES), F32), pltpu.VMEM((N_GROUPS, LANES), F32)],
        sem=("arbitrary",), name=name, xfers=xfers)


def _s5_bwd(hn, dyg, ypre, sbound, wb, wc, a1, a2, dvec, *, name, xfers=()):
    L, D = hn.shape
    T = min(S5_CHUNK, L)
    TP = T + SUBLANES
    nC = L // T
    span = (N_GROUPS - 1) * TP + 2 * SUBLANES
    NT = (((1,), (1,)), ((), ()))
    TN = (((0,), (0,)), ((), ()))

    def body(u_ref, dyg_ref, yp_ref, sb_ref, wb_ref, wc_ref, a1_ref, a2_ref, d_ref,
             du_ref, dwb_ref, dwc_ref, da1_ref, da2_ref, dd_ref, s3, g3, gst, dy_s):
        @pl.when(pl.program_id(0) == 0)
        def _():
            gst[...] = jnp.zeros_like(gst)
            dwb_ref[...] = jnp.zeros_like(dwb_ref)
            dwc_ref[...] = jnp.zeros_like(dwc_ref)
            da1_ref[...] = jnp.zeros_like(da1_ref)
            da2_ref[...] = jnp.zeros_like(da2_ref)
            dd_ref[...] = jnp.zeros_like(dd_ref)

        a1 = a1_ref[...]
        a2 = a2_ref[...]
        dy = dyg_ref[...].astype(F32) * _gelu_grad(yp_ref[...])
        dy_s[...] = dy.astype(BF16)
        dd_ref[...] += jnp.sum(dy * u_ref[...].astype(F32), axis=0, keepdims=True)
        du_ref[...] = d_ref[...] * dy

        s3[pl.ds(SUBLANES - 1, N_GROUPS, stride=TP), :] = sb_ref[0]
        _s5_project_in(u_ref, wb_ref, s3, T, TP)
        _s5_scan_fwd(s3, a1, a2, sb_ref[0], T, TP)

        for jp in range(HALF):
            blk = _chan_block(2 * jp)
            _unpair(g3, jp, lax.dot_general(dy_s[:, blk:blk + LANES], wc_ref[jp], NT, preferred_element_type=F32),
                    T, TP)
        a2c = -a2

        def rblk(ii, carry):
            g, acc1, acc2 = carry
            t0 = pl.multiple_of((T // SUBLANES - 1 - ii) * SUBLANES, SUBLANES)
            gv = g3.at[pl.ds(t0, span)]
            sv = s3.at[pl.ds(t0, span)]
            for k in reversed(range(SUBLANES)):
                rows = pl.ds(SUBLANES + k, N_GROUPS, stride=TP)
                g = a1 * g + a2c * _swap(g) + gv[rows, :]
                gv[rows, :] = g
                sp = sv[pl.ds(SUBLANES - 1 + k, N_GROUPS, stride=TP), :]
                acc1 = acc1 + g * sp
                acc2 = acc2 + g * _swap(sp)
            return g, acc1, acc2

        zero = jnp.zeros((N_GROUPS, LANES), F32)
        g, acc1, acc2 = lax.fori_loop(0, T // SUBLANES, rblk, (gst[...], zero, zero))
        gst[...] = g
        da1_ref[...] += acc1
        da2_ref[...] += acc2

        for jb in range(D // LANES):
            cols = slice(jb * LANES, (jb + 1) * LANES)
            acc = jnp.zeros((T, LANES), F32)
            for jp in _pairs_of(jb):
                gp = _pair(g3, jp, T, TP)
                dwc_ref[jp] += lax.dot_general(_pair(s3, jp, T, TP), dy_s[:, cols], TN, preferred_element_type=F32)
                dwb_ref[jp] += lax.dot_general(u_ref[:, cols], gp, TN, preferred_element_type=F32)
                acc += lax.dot_general(gp, wb_ref[jp], NT, preferred_element_type=F32)
            du_ref[:, cols] += acc

    rrow = pl.BlockSpec((T, D), lambda c: (nC - 1 - c, 0))
    aspec = pl.BlockSpec((N_GROUPS, LANES), lambda c: (0, 0))
    vec = pl.BlockSpec((1, D), lambda c: (0, 0))
    return _pcall(
        body, (hn, dyg, ypre, sbound, wb, wc, a1, a2, dvec), grid=(nC,),
        in_specs=[rrow, rrow, rrow, pl.BlockSpec((1, N_GROUPS, LANES), lambda c: (nC - 1 - c, 0, 0)),
                  _WB_SPEC, _WC_SPEC, aspec, aspec, vec],
        out_specs=[rrow, _WB_SPEC, _WC_SPEC, aspec, aspec, vec],
        out_shape=[jax.ShapeDtypeStruct((L, D), F32),
                   jax.ShapeDtypeStruct((HALF, LANES, 2 * LANES), F32),
                   jax.ShapeDtypeStruct((HALF, 2 * LANES, LANES), F32),
                   jax.ShapeDtypeStruct((N_GROUPS, LANES), F32), jax.ShapeDtypeStruct((N_GROUPS, LANES), F32),
                   jax.ShapeDtypeStruct((1, D), F32)],
        scratch_shapes=[pltpu.VMEM((N_GROUPS * TP, LANES), F32), pltpu.VMEM((N_GROUPS * TP, LANES), F32),
                        pltpu.VMEM((N_GROUPS, LANES), F32), pltpu.VMEM((T, D), BF16)],
        sem=("arbitrary",), name=name, xfers=xfers)


def _s5_prep(lam_re, lam_im, log_dt, b_re, b_im, c_re, c_im):
    dt = jnp.exp(log_dt)[:, None]
    mag = jnp.exp(lam_re * dt)
    lb_re = mag * jnp.cos(lam_im * dt)
    lb_im = mag * jnp.sin(lam_im * dt)
    den = lam_re * lam_re + lam_im * lam_im
    nr = lb_re - 1.0
    fr = ((nr * lam_re + lb_im * lam_im) / den)[..., None]
    fi = ((lb_im * lam_re - nr * lam_im) / den)[..., None]
    bb_re = fr * b_re - fi * b_im
    bb_im = fr * b_im + fi * b_re
    pair = lambda a: a.reshape(HALF, 2 * SSM_STATE)
    a1 = jnp.concatenate([pair(lb_re), pair(lb_re)], axis=0)
    a2 = jnp.concatenate([-pair(lb_im), pair(lb_im)], axis=0)
    sel = jax.nn.one_hot(jnp.arange(HALF) % 4, 4, dtype=F32)
    eye = jnp.eye(2, dtype=F32)

    def w_in(bb):
        return jnp.einsum('jk,ef,jfph->jkehfp', sel, eye, bb.reshape(HALF, 2, SSM_STATE, SSM_GROUP)
                          ).reshape(HALF, LANES, LANES)

    def w_out(c):
        return jnp.einsum('jk,ef,jfhp->jepkfh', sel, eye, c.reshape(HALF, 2, SSM_GROUP, SSM_STATE)
                          ).reshape(HALF, LANES, LANES)

    wb = jnp.concatenate([w_in(bb_re), w_in(bb_im)], axis=0)
    wc = jnp.concatenate([w_out(c_re), w_out(-c_im)], axis=0)
    wb = wb.reshape(HALF, 2, LANES, LANES).transpose(0, 2, 1, 3).reshape(HALF, LANES, 2 * LANES)
    wc = wc.reshape(HALF, 2 * LANES, LANES)
    return a1, a2, wb, wc


def _tri(n, upper):
    r = lax.broadcasted_iota(jnp.int32, (n, n), 0)
    c = lax.broadcasted_iota(jnp.int32, (n, n), 1)
    return ((r <= c) if upper else (r >= c)).astype(F32)


def _fgate_fwd(fl, bf, *, name):
    L, W = fl.shape
    tr = _tile(L, CUM_TILE, SUBLANES)

    def body(f_ref, b_ref, o_ref, carry):
        @pl.when(pl.program_id(0) == 0)
        def _():
            carry[...] = jnp.zeros_like(carry)

        x = f_ref[...] + b_ref[...]
        ls = jnp.minimum(x, 0.0) - jnp.log(1.0 + jnp.exp(-jnp.abs(x)))
        cum = jnp.dot(_tri(tr, False), ls, preferred_element_type=F32, precision=lax.Precision.HIGHEST) + carry[...]
        o_ref[...] = cum
        carry[...] = cum[tr - 1:tr, :]

    return pl.pallas_call(
        body, grid=(L // tr,),
        in_specs=[pl.BlockSpec((tr, W), lambda i: (i, 0)), pl.BlockSpec((1, W), lambda i: (0, 0))],
        out_specs=pl.BlockSpec((tr, W), lambda i: (i, 0)), out_shape=jax.ShapeDtypeStruct((L, W), F32),
        scratch_shapes=[pltpu.VMEM((1, W), F32)], compiler_params=_params("arbitrary"), name=name)(fl, bf)


def _fgate_bwd(fl, bf, dcum, *, name):
    L, W = fl.shape
    tr = _tile(L, CUM_TILE, SUBLANES)
    n = L // tr

    def body(f_ref, b_ref, d_ref, o_ref, db_ref, carry):
        @pl.when(pl.program_id(0) == 0)
        def _():
            carry[...] = jnp.zeros_like(carry)
            db_ref[...] = jnp.zeros_like(db_ref)

        d = d_ref[...]
        rev = jnp.dot(_tri(tr, True), d, preferred_element_type=F32, precision=lax.Precision.HIGHEST) + carry[...]
        carry[...] += jnp.sum(d, axis=0, keepdims=True)
        df = rev * jax.nn.sigmoid(-(f_ref[...] + b_ref[...]))
        o_ref[...] = df
        db_ref[...] += jnp.sum(df, axis=0, keepdims=True)

    rrow = pl.BlockSpec((tr, W), lambda i: (n - 1 - i, 0))
    vec = pl.BlockSpec((1, W), lambda i: (0, 0))
    return pl.pallas_call(
        body, grid=(n,), in_specs=[rrow, vec, rrow], out_specs=[rrow, vec],
        out_shape=[jax.ShapeDtypeStruct((L, W), F32), jax.ShapeDtypeStruct((1, W), F32)],
        scratch_shapes=[pltpu.VMEM((1, W), F32)], compiler_params=_params("arbitrary"), name=name)(fl, bf, dcum)


_NT = (((1,), (1,)), ((), ()))
_TN = (((0,), (0,)), ((), ()))
HEAD_PAIRS = N_HEADS // 2


def _causal_tiles(n, by_row):
    pairs = ([(i, j) for i in range(n) for j in range(i + 1)] if by_row
             else [(i, j) for j in range(n) for i in range(j, n)])
    return (jnp.array([p[0] for p in pairs], jnp.int32), jnp.array([p[1] for p in pairs], jnp.int32))


def _attn_logits(qs, k, ck, masked, t):
    s = lax.dot_general(qs, k, _NT, preferred_element_type=F32) - ck
    if masked:
        r = lax.broadcasted_iota(jnp.int32, (t, t), 0)
        c = lax.broadcasted_iota(jnp.int32, (t, t), 1)
        s = jnp.where(c > r, NEG, s)
    return s


def _attn_fwd(q, kv, ck, *, name, xfers=()):
    L, D = q.shape
    t = _tile(L, ATTN_TILE)
    n = L // t
    dh = HEAD_DIM

    def body(q_ref, k_ref, v_ref, ck_ref, o_ref, o32_ref, lse_ref, m_s, l_s, acc):
        i, j = pl.program_id(1), pl.program_id(2)

        @pl.when(j == 0)
        def _():
            m_s[...] = jnp.full_like(m_s, NEG)
            l_s[...] = jnp.zeros_like(l_s)
            acc[...] = jnp.zeros_like(acc)

        def tile(masked):
            for e in range(2):
                sl = slice(e * dh, (e + 1) * dh)
                v = v_ref[:, sl]
                s = _attn_logits(q_ref[:, sl] * ATTN_SCALE, k_ref[:, sl], ck_ref[e], masked, t)
                m_new = jnp.maximum(m_s[e], jnp.max(s, axis=1, keepdims=True))
                alpha = jnp.exp(m_s[e] - m_new)
                p = jnp.exp(s - m_new)
                l_s[e] = alpha * l_s[e] + jnp.sum(p, axis=1, keepdims=True)
                p_hi = p.astype(BF16)
                p_lo = (p - p_hi.astype(F32)).astype(BF16)
                pv = (jnp.dot(p_hi, v, preferred_element_type=F32) + jnp.dot(p_lo, v, preferred_element_type=F32))
                acc[e] = alpha * acc[e] + pv
                m_s[e] = m_new

        pl.when(j < i)(functools.partial(tile, False))
        pl.when(j == i)(functools.partial(tile, True))

        @pl.when(j == n - 1)
        def _():
            for e in range(2):
                sl = slice(e * dh, (e + 1) * dh)
                o = acc[e] / l_s[e]
                o_ref[:, sl] = o.astype(BF16)
                o32_ref[:, sl] = o
                lse_ref[e] = m_s[e] + jnp.log(l_s[e])

    qs = pl.BlockSpec((t, LANES), lambda h, i, j: (i, h))
    ks = pl.BlockSpec((t, LANES), lambda h, i, j: (jnp.minimum(i, j), h))
    vs = pl.BlockSpec((t, LANES), lambda h, i, j: (jnp.minimum(i, j), HEAD_PAIRS + h))
    cs = pl.BlockSpec((2, 1, t), lambda h, i, j: (h, 0, jnp.minimum(i, j)))
    return _pcall(
        body, (q, kv, kv, ck), grid=(HEAD_PAIRS, n, n), in_specs=[qs, ks, vs, cs],
        out_specs=[qs, qs, pl.BlockSpec((2, t, 1), lambda h, i, j: (h, i, 0))],
        out_shape=[jax.ShapeDtypeStruct((L, D), BF16), jax.ShapeDtypeStruct((L, D), F32),
                   jax.ShapeDtypeStruct((N_HEADS, L, 1), F32)],
        scratch_shapes=[pltpu.VMEM((2, t, 1), F32), pltpu.VMEM((2, t, 1), F32), pltpu.VMEM((2, t, dh), F32)],
        sem=("parallel", "parallel", "arbitrary"), name=name, xfers=xfers)


def _attn_bwd(q, kv, ck, o, do, lse, *, name, xfers=()):
    L, D = q.shape
    t = _tile(L, ATTN_TILE)
    n = L // t
    dh = HEAD_DIM

    def body(i_tab, j_tab, q_ref, k_ref, v_ref, ck_ref, o_ref, do_ref, lse_ref, dq_ref, dk_ref, dv_ref, dck_ref):
        i, j = i_tab[pl.program_id(1)], j_tab[pl.program_id(1)]

        @pl.when(pl.program_id(1) == 0)
        def _():
            dq_ref[...] = jnp.zeros_like(dq_ref)

        @pl.when(i == j)
        def _():
            dk_ref[...] = jnp.zeros_like(dk_ref)
            dv_ref[...] = jnp.zeros_like(dv_ref)
            dck_ref[...] = jnp.zeros_like(dck_ref)

        def tile(masked):
            rows = pl.ds(pl.multiple_of(i * t, t), t)
            for e in range(2):
                sl = slice(e * dh, (e + 1) * dh)
                qs = q_ref[:, sl] * ATTN_SCALE
                k = k_ref[:, sl]
                do = do_ref[:, sl]
                s = _attn_logits(qs, k, ck_ref[e], masked, t)
                p = jnp.exp(s - lse_ref[e])
                dp = lax.dot_general(do, v_ref[:, sl], _NT, preferred_element_type=F32)
                delta = jnp.sum(do.astype(F32) * o_ref[:, sl], axis=1, keepdims=True)
                ds = p * (dp - delta)
                ds16 = ds.astype(BF16)
                dv_ref[:, sl] += lax.dot_general(p.astype(BF16), do, _TN, preferred_element_type=F32)
                dk_ref[:, sl] += lax.dot_general(ds16, qs, _TN, preferred_element_type=F32)
                dck_ref[e] -= jnp.sum(ds, axis=0, keepdims=True)
                dq_ref[rows, sl] += jnp.dot(ds16, k, preferred_element_type=F32) * ATTN_SCALE

        pl.when(i > j)(functools.partial(tile, False))
        pl.when(i == j)(functools.partial(tile, True))

    qs = pl.BlockSpec((t, LANES), lambda h, s, it, jt: (it[s], h))
    ks = pl.BlockSpec((t, LANES), lambda h, s, it, jt: (jt[s], h))
    vs = pl.BlockSpec((t, LANES), lambda h, s, it, jt: (jt[s], HEAD_PAIRS + h))
    cs = pl.BlockSpec((2, 1, t), lambda h, s, it, jt: (h, 0, jt[s]))
    ls = pl.BlockSpec((2, t, 1), lambda h, s, it, jt: (h, it[s], 0))
    full = jax.ShapeDtypeStruct((L, D), F32)
    tabs = _causal_tiles(n, by_row=False)
    return _pcall(
        body, (q, kv, kv, ck, o, do, lse), grid=(HEAD_PAIRS, tabs[0].shape[0]), in_specs=[qs, ks, vs, cs, qs, qs, ls],
        out_specs=[pl.BlockSpec((L, LANES), lambda h, s, it, jt: (0, h)), ks, ks, cs],
        out_shape=[full, full, full, jax.ShapeDtypeStruct((N_HEADS, 1, L), F32)],
        sem=("parallel", "arbitrary"), name=name, xfers=xfers, prefetch=tabs)


SHARD_COLS_FFN = 2 * D_FF // N_DEV
SHARD_ROWS_FFN = D_FF // N_DEV
SHARD_COLS_GLU = 2 * D_MODEL // N_DEV
SHARD_ROWS_QO = D_MODEL // N_DEV
SHARD_COLS_KVF = (2 * D_MODEL + N_HEADS) // N_DEV
S5_NAMES = ("lam_re", "lam_im", "log_dt", "ssm_b_re", "ssm_b_im", "ssm_c_re", "ssm_c_im")
REPL_LATE_ROWS = 288
REPL_EARLY_ROWS = 320


def _leaves(parts):
    out = []
    for p in parts:
        out.extend(_leaves(p) if isinstance(p, (list, tuple)) else [p.reshape(-1)])
    return out


def _pack_rows(parts, rows):
    flat = jnp.concatenate(_leaves(parts))
    return jnp.pad(flat, (0, rows * D_MODEL - flat.shape[0])).reshape(rows, D_MODEL)


def _unpack_rows(flat, like):
    flat, out, off = flat.reshape(-1), [], 0
    for p in _leaves(like):
        out.append(flat[off:off + p.shape[0]])
        off += p.shape[0]
    return out


def _repl_late(d):
    return [d["g_mix"][0], [d[n][0] for n in S5_NAMES]]


def _repl_early(d):
    return [list(d["g_mix"][1:]), list(d["g_ffn"]), [d[n][1] for n in S5_NAMES], d["g_kv"], d["b_f"],
            list(d["ffn_conv_b"]), d["g_final"]]


def _kvf_blocks(full):
    return full.reshape(D_MODEL, N_DEV, SHARD_COLS_KVF).transpose(1, 0, 2)


class _Step:
    def __init__(self, weights, send=None, plan=None):
        self.w = dict(weights)
        self.send = send or {}
        self.grad = {}
        self.bcast = {}
        self.slots = {}
        self.plan = plan or {}

    def xfers(self, host):
        src = {"w": self.send, "g": self.grad, "b": self.bcast}
        return [(src[kind][k], kind == "g") for kind, k in self.plan.get(host, ())]

    def land(self, host, gathered):
        for (kind, k), g in zip(self.plan.get(host, ()), gathered):
            if kind == "w":
                self.arrive(k, g)
            else:
                self.slots[k] = g

    def arrive(self, k, g):
        name = k[0]
        if name == "w_ffn_in" and len(k) == 3:
            self.w[k] = g
            halves = [self.w.get((name, k[1], h)) for h in range(2)]
            if all(h is not None for h in halves):
                self.w[name, k[1]] = jnp.concatenate(halves, axis=1)
        elif name == "w_ffn_out":
            self.w[k] = g.reshape(4, 2 * SHARD_ROWS_FFN, D_MODEL)
        elif name in ("w_q", "w_o"):
            self.w[k] = g.reshape(D_MODEL, D_MODEL)
        elif name == "w_kvf":
            full = g.transpose(1, 0, 2).reshape(D_MODEL, N_DEV * SHARD_COLS_KVF)
            self.w["w_kv",] = full[:, :2 * D_MODEL]
            self.w["w_f",] = jnp.pad(full[:, 2 * D_MODEL:], ((0, 0), (0, LANES - N_HEADS)))
        elif name == "small":
            flat = g.reshape(N_DEV, -1)
            self.w["ssm_d",] = flat[:, :256].reshape(N_DEV, N_A, LANES).transpose(1, 0, 2).reshape(N_A, D_MODEL)
            cw = flat[:, 256:256 + DEPTH * 3 * SHARD_COLS_FFN].reshape(N_DEV, DEPTH, 3, SHARD_COLS_FFN)
            for layer in range(DEPTH):
                self.w["conv_w", layer] = cw[:, layer].reshape(2, 4, 3, SHARD_COLS_FFN)
        else:
            self.w[k] = g

    def run(self, host, fn, *args, **kw):
        xf = self.xfers(host)
        res = fn(*args, name=host, xfers=xf, **kw)
        if not xf:
            return res[0] if isinstance(res, (list, tuple)) and len(res) == 1 else res
        n_own = len(res) - len(xf)
        self.land(host, res[n_own:])
        return res[0] if n_own == 1 else res[:n_own]


def _step(x, target, S):
    L = x.shape[0]
    W = S.w
    vec = lambda a: a.reshape(1, -1)
    CF = SHARD_COLS_FFN

    h = x
    saved = []
    kvs = None
    for layer in range(DEPTH):
        t = str(layer)
        if layer < N_A:
            (a1, a2, wb, wc), prep_vjp = jax.vjp(_s5_prep, *[W[n][layer] for n in S5_NAMES])
            wb16, wc16 = wb.astype(BF16), wc.astype(BF16)
            hn = _rms_fwd(h, vec(W["g_mix"][layer]), name="mix_norm" + t)
            dvec = vec(W["ssm_d",][layer])
            ypre, yg, sb = S.run("s5_fwd" + t, _s5_fwd, hn, wb16, wc16, a1, a2, dvec)
            z = S.run("glu_mm" + t, _mm, yg, W["w_glu", layer], bk="bkn", ok="bmn")
            z = z.reshape(2, 4, L, SHARD_COLS_GLU)
            h1, hn2 = _glu_res_rms(z, h, vec(W["g_ffn"][layer]), name="glu_res" + t)
            mix_saved = (h, hn, ypre, yg, sb, z, a1, a2, wb16, wc16, dvec, prep_vjp)
        else:
            j = layer - N_A
            if layer == N_A:
                hkv = _rms_fwd(h, vec(W["g_kv"]), name="kv_norm")
                kvm = S.run("kv_mm", _mm, hkv, W["w_kv",], out_dtype=BF16)
                fl = S.run("f_mm", _mm, hkv, W["w_f",])
                cum = _fgate_fwd(fl, W["b_f_pad",], name="fgate_fwd")
                ck = cum[:, :N_HEADS].T.reshape(N_HEADS, 1, L)
                kvs = (h, hkv, fl, kvm, ck)
            _, _, _, kvm, ck = kvs
            hn = _rms_fwd(h, vec(W["g_mix"][layer]), name="mix_norm" + t)
            q = S.run("q_mm" + t, _mm, hn, W["w_q", j], out_dtype=BF16)
            o, o32, lse = S.run("attn_fwd" + t, _attn_fwd, q, kvm, ck)
            h1 = S.run("o_mm" + t, _mm, o, W["w_o", j], add=h)
            hn2 = _rms_fwd(h1, vec(W["g_ffn"][layer]), name="ffn_norm" + t)
            mix_saved = (h, hn, q, o32, o, lse)
        u0 = S.run("ffn_in" + t, _mm, hn2, W["w_ffn_in", layer], bk="nbk", ok="bmn", tm=2048).reshape(2, 4, L, CF)
        a = S.run("ffn_act" + t, _conv_act, u0, W["conv_w", layer], W["conv_b", layer])
        h2 = S.run("ffn_out" + t, _mm, a, W["w_ffn_out", layer], ak="bmk", bk="kbn", add=h1)
        saved.append((mix_saved, h1, hn2, u0))
        h = h2

    loss, dh, dg_final = _loss_head(h, vec(W["g_final"]), target, name="loss_head")
    g = {"g_final": dg_final.reshape(-1)}
    gl = {k: [None] * DEPTH for k in ("g_mix", "g_ffn", "conv_w", "ffn_conv_b")}
    ga = {k: [None] * N_A for k in S5_NAMES + ("ssm_d",)}
    dk = dv = dck = None
    for layer in reversed(range(DEPTH)):
        t = str(layer)
        mix_saved, h1, hn2, u0 = saved[layer]
        cw, cb = W["conv_w", layer], W["conv_b", layer]
        da = S.run("ffn_da" + t, _mm, dh, W["w_ffn_out", layer], bk="nbk", ok="bmn", tm=2048)
        a, du0, dcw, dcb = S.run("ffn_conv_bwd" + t, _conv_ffn_bwd, u0, cw, cb, da)
        dw_out = S.run("ffn_dwout" + t, _mm, a, dh, ak="bkm", ok="mbn", out_dtype=BF16)
        S.grad["w_ffn_out", layer] = dw_out.reshape(N_DEV, SHARD_ROWS_FFN, D_MODEL)
        du0 = du0.reshape(N_DEV, L, CF)
        S.grad["w_ffn_in", layer] = S.run("ffn_dwin" + t, _mm, du0, hn2, ak="bkm", ok="mbn", out_dtype=BF16)
        dhn2 = S.run("ffn_dhn" + t, _mm, du0, W["w_ffn_in", layer], ak="bmk", bk="kbn", tm=2048)
        dh1, dg = _rms_bwd(h1, vec(W["g_ffn"][layer]), dhn2, dh, name="ffn_norm_bwd" + t)
        gl["g_ffn"][layer], gl["conv_w"][layer], gl["ffn_conv_b"][layer] = dg.reshape(-1), dcw, dcb.reshape(-1)
        if layer < N_A:
            hin, hn, ypre, yg, sb, z, a1, a2, wb16, wc16, dvec, prep_vjp = mix_saved
            dz = _glu_bwd(z, dh1, name="glu_bwd" + t).reshape(N_DEV, L, SHARD_COLS_GLU)
            S.grad["w_glu", layer] = S.run("glu_dw" + t, _mm, yg, dz, ak="km", bk="bkn", ok="bmn", out_dtype=BF16)
            dyg = S.run("glu_dy" + t, _mm, dz, W["w_glu", layer], ak="bmk", bk="bnk", out_dtype=BF16)
            if layer == 0:
                S.bcast["repl_early",] = _pack_rows(_repl_early({**g, **gl, **ga}), REPL_EARLY_ROWS)
            du, dwb, dwc, da1, da2, dd = S.run("s5_bwd" + t, _s5_bwd, hn, dyg, ypre, sb, wb16, wc16, a1, a2, dvec)
            for nme, val in zip(S5_NAMES, prep_vjp((da1, da2, dwb, dwc))):
                ga[nme][layer] = val
            ga["ssm_d"][layer] = dd.reshape(-1)
            dh, dg = _rms_bwd(hin, vec(W["g_mix"][layer]), du, dh1, name="mix_norm_bwd" + t)
        else:
            j = layer - N_A
            hin, hn, q, o32, o, lse = mix_saved
            _, _, _, kvm, ck = kvs
            S.grad["w_o", j] = S.run("o_dw" + t, _mm, o, dh1, ak="km", out_dtype=BF16
                                     ).reshape(N_DEV, SHARD_ROWS_QO, D_MODEL)
            do = S.run("o_dx" + t, _mm, dh1, W["w_o", j], bk="nk", out_dtype=BF16)
            dq, dk_l, dv_l, dck_l = S.run("attn_bwd" + t, _attn_bwd, q, kvm, ck, o32, do, lse)
            dk = dk_l if dk is None else dk + dk_l
            dv = dv_l if dv is None else dv + dv_l
            dck = dck_l if dck is None else dck + dck_l
            S.grad["w_q", j] = S.run("q_dw" + t, _mm, hn, dq, ak="km", out_dtype=BF16
                                     ).reshape(N_DEV, SHARD_ROWS_QO, D_MODEL)
            dhn = S.run("q_dx" + t, _mm, dq, W["w_q", j], bk="nk")
            dh, dg = _rms_bwd(hin, vec(W["g_mix"][layer]), dhn, dh1, name="mix_norm_bwd" + t)
            if layer == N_A:
                hkv_in, hkv, fl, _, _ = kvs
                dcum = jnp.pad(dck.reshape(N_HEADS, L).T, ((0, 0), (0, LANES - N_HEADS)))
                dfl, dbf = _fgate_bwd(fl, W["b_f_pad",], dcum, name="fgate_bwd")
                dkv = jnp.concatenate([dk, dv], axis=1).astype(BF16)
                dfl16 = dfl.astype(BF16)
                dw_kv = S.run("kv_dw", _mm, hkv, dkv, ak="km")
                dw_f = S.run("f_dw", _mm, hkv, dfl16, ak="km")
                S.grad["w_kvf",] = _kvf_blocks(jnp.concatenate([dw_kv, dw_f[:, :N_HEADS]], axis=1)).astype(BF16)
                dhkv = S.run("kv_dx", _mm, dkv, W["w_kv",], bk="nk")
                dhkv = S.run("f_dx", _mm, dfl16, W["w_f",], bk="nk", add=dhkv)
                g["b_f"] = dbf[0, :N_HEADS]
                dh, dgkv = _rms_bwd(hkv_in, vec(W["g_kv"]), dhkv, dh, name="kv_norm_bwd")
                g["g_kv"] = dgkv.reshape(-1)
        gl["g_mix"][layer] = dg.reshape(-1)

    for d in (gl, ga):
        for k, v in d.items():
            g[k] = jnp.stack(v)
    return loss, dh, g


def _adamw_layers(slots, w, m, v, *, name):
    shape = w.shape
    nl = len(slots)
    w, m, v = (a.reshape((nl,) + a.shape[-2:]) for a in (w, m, v))
    _, R, C = w.shape
    tr = _tile(R, 256, 16)
    c1 = 1.0 / (1.0 - ADAM_B1 ** ADAM_STEP)
    c2 = 1.0 / (1.0 - ADAM_B2 ** ADAM_STEP)

    def body(*refs):
        s_refs, (w_ref, m_ref, v_ref), (g_ref, d_ref, nm_ref, nv_ref) = refs[:nl], refs[nl:nl + 3], refs[nl + 3:]
        for layer in range(nl):
            @pl.when(pl.program_id(0) == layer)
            def _(s_ref=s_refs[layer]):
                g = s_ref[0].astype(F32)
                for d in range(1, N_DEV):
                    g = g + s_ref[d].astype(F32)
                m2 = ADAM_B1 * m_ref[...] + (1.0 - ADAM_B1) * g
                v2 = ADAM_B2 * v_ref[...] + (1.0 - ADAM_B2) * (g * g)
                g_ref[...] = g
                nm_ref[...] = m2
                nv_ref[...] = v2
                d_ref[...] = -ADAM_LR * ((m2 * c1) / (jnp.sqrt(v2 * c2) + ADAM_EPS) + ADAM_WD * w_ref[...])

    def slab_spec(layer):
        return pl.BlockSpec((N_DEV, tr, C), lambda l, i: (0, jnp.where(l == layer, i, 0), 0))

    row = pl.BlockSpec((None, tr, C), lambda l, i: (l, i, 0))
    out = jax.ShapeDtypeStruct((nl, R, C), F32)
    outs = pl.pallas_call(
        body, grid=(nl, R // tr), in_specs=[slab_spec(layer) for layer in range(nl)] + [row, row, row],
        out_specs=[row, row, row, row], out_shape=[out, out, out, out],
        compiler_params=_params("arbitrary", "arbitrary"), name=name)(*slots, w, m, v)
    return [o.reshape(shape) for o in outs]


_SMALL_ROWS = 72
_ORDER = ("g_mix", "g_ffn", "lam_re", "lam_im", "log_dt", "ssm_b_re", "ssm_b_im", "ssm_c_re", "ssm_c_im", "ssm_d",
          "w_glu", "g_kv", "w_kvf", "b_f", "w_q", "w_o", "w_ffn_in", "ffn_conv_w", "ffn_conv_b", "w_ffn_out", "g_final")


def _pack_small(ssm_d, conv_w):
    flat = jnp.concatenate([ssm_d.reshape(-1), conv_w.reshape(-1)])
    return jnp.pad(flat, (0, _SMALL_ROWS * LANES - flat.shape[0])).reshape(_SMALL_ROWS, LANES)


def _unpack_small(flat):
    flat = flat.reshape(-1)
    return flat[:256].reshape(2, 128), flat[256:256 + 8448].reshape(4, 3, 704)


_FWD_PLAN = {
    "start": [("small",)],
    "s5_fwd0": [("w_glu", 0), ("w_ffn_in", 0, 0)],
    "glu_mm0": [("w_ffn_in", 0, 1)],
    "ffn_in0": [("w_ffn_out", 0)],
    "ffn_act0": [("w_glu", 1)],
    "ffn_out0": [("w_ffn_in", 1, 0)],
    "s5_fwd1": [("w_ffn_in", 1, 1)],
    "ffn_in1": [("w_ffn_out", 1)],
    "ffn_act1": [("w_kvf",)],
    "ffn_out1": [("w_q", 0), ("w_o", 0)],
    "attn_fwd2": [("w_ffn_in", 2), ("w_ffn_out", 2), ("w_q", 1), ("w_o", 1), ("w_ffn_in", 3), ("w_ffn_out", 3)],
}
_BWD_PLAN = {
    "ffn_dhn3": [("w_ffn_out", 3)],
    "attn_bwd3": [("w_ffn_in", 3), ("w_o", 1)],
    "ffn_conv_bwd2": [("w_q", 1)],
    "ffn_dhn2": [("w_ffn_out", 2)],
    "attn_bwd2": [("w_ffn_in", 2), ("w_o", 0)],
    "ffn_conv_bwd1": [("w_q", 0), ("w_kvf",)],
    "ffn_dhn1": [("w_ffn_out", 1)],
    "s5_bwd1": [("w_ffn_in", 1), ("w_glu", 1)],
    "ffn_dhn0": [("w_ffn_out", 0)],
    "glu_dy0": [("w_glu", 0)],
    "s5_bwd0": [("w_ffn_in", 0), ("repl_early",)],
    "end": [("small",), ("repl_late",)],
}
_PLAN = {h: [("w", k) for k in ks] for h, ks in _FWD_PLAN.items()}
_PLAN.update({h: [("b" if k[0].startswith("repl") else "g", k) for k in ks] for h, ks in _BWD_PLAN.items()})


def kernel(x, g_mix, g_ffn, lam_re, lam_im, log_dt, ssm_b_re, ssm_b_im, ssm_c_re, ssm_c_im, ssm_d, w_glu, g_kv, w_kvf, b_f, w_q, w_o, w_ffn_in, ffn_conv_w, ffn_conv_b, w_ffn_out, g_final, loss_target, m_g_mix, m_g_ffn, m_lam_re, m_lam_im, m_log_dt, m_ssm_b_re, m_ssm_b_im, m_ssm_c_re, m_ssm_c_im, m_ssm_d, m_w_glu, m_g_kv, m_w_kvf, m_b_f, m_w_q, m_w_o, m_w_ffn_in, m_ffn_conv_w, m_ffn_conv_b, m_w_ffn_out, m_g_final, v_g_mix, v_g_ffn, v_lam_re, v_lam_im, v_log_dt, v_ssm_b_re, v_ssm_b_im, v_ssm_c_re, v_ssm_c_im, v_ssm_d, v_w_glu, v_g_kv, v_w_kvf, v_b_f, v_w_q, v_w_o, v_w_ffn_in, v_ffn_conv_w, v_ffn_conv_b, v_w_ffn_out, v_g_final):
    wts = dict(g_mix=g_mix, g_ffn=g_ffn, lam_re=lam_re, lam_im=lam_im, log_dt=log_dt, ssm_b_re=ssm_b_re,
               ssm_b_im=ssm_b_im, ssm_c_re=ssm_c_re, ssm_c_im=ssm_c_im, ssm_d=ssm_d, w_glu=w_glu, g_kv=g_kv,
               w_kvf=w_kvf, b_f=b_f, w_q=w_q, w_o=w_o, w_ffn_in=w_ffn_in, ffn_conv_w=ffn_conv_w,
               ffn_conv_b=ffn_conv_b, w_ffn_out=w_ffn_out, g_final=g_final)
    mom = dict(g_mix=m_g_mix, g_ffn=m_g_ffn, lam_re=m_lam_re, lam_im=m_lam_im, log_dt=m_log_dt, ssm_b_re=m_ssm_b_re,
               ssm_b_im=m_ssm_b_im, ssm_c_re=m_ssm_c_re, ssm_c_im=m_ssm_c_im, ssm_d=m_ssm_d, w_glu=m_w_glu,
               g_kv=m_g_kv, w_kvf=m_w_kvf, b_f=m_b_f, w_q=m_w_q, w_o=m_w_o, w_ffn_in=m_w_ffn_in,
               ffn_conv_w=m_ffn_conv_w, ffn_conv_b=m_ffn_conv_b, w_ffn_out=m_w_ffn_out, g_final=m_g_final)
    var = dict(g_mix=v_g_mix, g_ffn=v_g_ffn, lam_re=v_lam_re, lam_im=v_lam_im, log_dt=v_log_dt, ssm_b_re=v_ssm_b_re,
               ssm_b_im=v_ssm_b_im, ssm_c_re=v_ssm_c_re, ssm_c_im=v_ssm_c_im, ssm_d=v_ssm_d, w_glu=v_w_glu,
               g_kv=v_g_kv, w_kvf=v_w_kvf, b_f=v_b_f, w_q=v_w_q, w_o=v_w_o, w_ffn_in=v_w_ffn_in,
               ffn_conv_w=v_ffn_conv_w, ffn_conv_b=v_ffn_conv_b, w_ffn_out=v_w_ffn_out, g_final=v_g_final)
    kinds = ("grad", "delta", "m", "v")

    ready = {n: wts[n] for n in ("g_mix", "g_ffn", "g_kv", "g_final") + S5_NAMES}
    ready["b_f_pad",] = jnp.pad(b_f, (0, LANES - N_HEADS)).reshape(1, LANES)
    send = {("small",): _pack_small(ssm_d, ffn_conv_w), ("w_kvf",): w_kvf.astype(BF16)}
    for layer in range(DEPTH):
        ready["conv_b", layer] = ffn_conv_b[layer].reshape(2, 4, 1, SHARD_COLS_FFN)
        w_in_t = jnp.swapaxes(w_ffn_in[layer], 0, 1).astype(BF16)
        if ("w_ffn_in", layer) in [k for ks in _FWD_PLAN.values() for k in ks]:
            send["w_ffn_in", layer] = w_in_t
        else:
            half = SHARD_COLS_FFN // 2
            send["w_ffn_in", layer, 0] = w_in_t[:half]
            send["w_ffn_in", layer, 1] = w_in_t[half:]
        send["w_ffn_out", layer] = w_ffn_out[layer].astype(BF16)
    for layer in range(N_A):
        send["w_glu", layer] = w_glu[layer].astype(BF16)
        send["w_q", layer] = w_q[layer].astype(BF16)
        send["w_o", layer] = w_o[layer].astype(BF16)

    S = _Step(ready, send, _PLAN)
    S.land("start", _exchange(S.xfers("start"), name="start"))
    loss, dx, g = _step(x[0], loss_target[0], S)
    loss = lax.psum(loss[0, 0], MESH_AXES)

    g_d = g["ssm_d"].reshape(N_A, N_DEV, LANES).transpose(1, 0, 2).reshape(N_DEV, N_A * LANES)
    g_cw = jnp.stack([g["conv_w"][layer].reshape(N_DEV, 3, SHARD_COLS_FFN) for layer in range(DEPTH)], axis=1)
    g_small = jnp.concatenate([g_d, g_cw.reshape(N_DEV, -1)], axis=1)
    g_small = jnp.pad(g_small, ((0, 0), (0, _SMALL_ROWS * LANES - g_small.shape[1])))
    S.grad["small",] = g_small.reshape(N_DEV, _SMALL_ROWS, LANES)
    S.bcast["repl_late",] = _pack_rows(_repl_late(g), REPL_LATE_ROWS)
    S.land("end", _exchange(S.xfers("end"), name="end"))

    res = {}
    for name, nl in (("w_glu", N_A), ("w_q", DEPTH - N_A), ("w_o", DEPTH - N_A), ("w_ffn_in", DEPTH),
                     ("w_ffn_out", DEPTH)):
        view = (lambda a: jnp.swapaxes(a, 1, 2)) if name == "w_ffn_in" else (lambda a: a)
        outs = _adamw_layers([S.slots[name, layer] for layer in range(nl)], view(wts[name]), view(mom[name]),
                             view(var[name]), name="adamw_" + name)
        res.update({(kind, name): view(a) for kind, a in zip(kinds, outs)})
    outs = _adamw_layers([S.slots["w_kvf",]], w_kvf, m_w_kvf, v_w_kvf, name="adamw_w_kvf")
    res.update({(kind, "w_kvf"): a for kind, a in zip(kinds, outs)})
    outs = _adamw_layers([S.slots["small",]], _pack_small(ssm_d, ffn_conv_w), _pack_small(m_ssm_d, m_ffn_conv_w),
                         _pack_small(v_ssm_d, v_ffn_conv_w), name="adamw_small")
    for kind, flat in zip(kinds, outs):
        res[kind, "ssm_d"], res[kind, "ffn_conv_w"] = _unpack_small(flat)

    pieces = {}
    for key, rows, sel in ((("repl_early",), REPL_EARLY_ROWS, _repl_early), (("repl_late",), REPL_LATE_ROWS, _repl_late)):
        outs = _adamw_layers([S.slots[key]], *[_pack_rows(sel(d), rows) for d in (wts, mom, var)],
                             name="adamw_" + key[0])
        for kind, flat in zip(kinds, outs):
            pieces[kind, key[0]] = _unpack_rows(flat, sel(wts))
    for kind in kinds:
        early, late = iter(pieces[kind, "repl_early"]), iter(pieces[kind, "repl_late"])
        take = lambda it, n: [next(it) for _ in range(n)]
        res[kind, "g_mix"] = jnp.stack(take(late, 1) + take(early, DEPTH - 1))
        res[kind, "g_ffn"] = jnp.stack(take(early, DEPTH))
        for n in S5_NAMES:
            res[kind, n] = jnp.stack([next(late), next(early)]).reshape(wts[n].shape)
        res[kind, "g_kv"], res[kind, "b_f"] = next(early), next(early)
        res[kind, "ffn_conv_b"] = jnp.stack(take(early, DEPTH))
        res[kind, "g_final"] = next(early)

    return (loss, dx[None], *[res[kind, n] for kind in kinds for n in _ORDER])
```

```python
import functools
import math

import jax
import jax.numpy as jnp
from jax import lax
from jax.experimental import pallas as pl
from jax.experimental.pallas import tpu as pltpu

F32 = jnp.float32
BF16 = jnp.bfloat16

D_MODEL = 1024
DEPTH = 4
N_A = 2
N_GROUPS = 64
SSM_GROUP = 16
SSM_STATE = 64
N_HEADS = 16
HEAD_DIM = 64
ATTN_SCALE = HEAD_DIM ** -0.5
D_FF = 2816
EPS = 1e-6
N_DEV = 8
LANES = 128
SUBLANES = 8

ADAM_LR = 0.001
ADAM_B1 = 0.9
ADAM_B2 = 0.999
ADAM_EPS = 1e-08
ADAM_WD = 0.01
ADAM_STEP = 10

ROW_TILE = 512
S5_CHUNK = 256
ATTN_TILE = 512
CUM_TILE = 256
NEG = -1e30

MESH_AXES = ("x", "y", "c")


def _tile(n, target, align=LANES):
    t = (min(target, n) // align) * align
    while t >= align:
        if n % t == 0:
            return t
        t -= align
    return n


def _params(*sem):
    return pltpu.CompilerParams(dimension_semantics=sem, vmem_limit_bytes=56 * 1024 * 1024)


_ANY = pl.BlockSpec(memory_space=pl.ANY)
_XFER_SEMS = (pltpu.SemaphoreType.DMA((N_DEV - 1,)), pltpu.SemaphoreType.DMA((N_DEV - 1,)), pltpu.SemaphoreType.DMA)


def _xfer_copies(x_ref, o_ref, send_sems, recv_sems, local_sem, scatter):
    xi, yi, ci = lax.axis_index("x"), lax.axis_index("y"), lax.axis_index("c")
    me = 4 * xi + 2 * yi + ci

    def src(p):
        return x_ref.at[p] if scatter else x_ref

    own = pltpu.make_async_copy(src(me), o_ref.at[me], local_sem)
    sends, recvs = [], []
    for k in range(1, N_DEV):
        px, py, pc = xi ^ (k >> 2), yi ^ ((k >> 1) & 1), ci ^ (k & 1)
        p = 4 * px + 2 * py + pc
        sends.append(pltpu.make_async_remote_copy(
            src_ref=src(p), dst_ref=o_ref.at[me], send_sem=send_sems.at[k - 1], recv_sem=recv_sems.at[k - 1],
            device_id=(px, py, pc), device_id_type=pl.DeviceIdType.MESH))
        recvs.append(pltpu.make_async_remote_copy(
            src_ref=src(p), dst_ref=o_ref.at[p], send_sem=send_sems.at[k - 1], recv_sem=recv_sems.at[k - 1],
            device_id=(px, py, pc), device_id_type=pl.DeviceIdType.MESH))
    return own, sends, recvs


def _xfer_start(*refs, scatter):
    own, sends, _ = _xfer_copies(*refs, scatter)
    own.start()
    for cp in sends:
        cp.start()


def _xfer_wait(*refs, scatter):
    own, sends, recvs = _xfer_copies(*refs, scatter)
    for cp in recvs:
        cp.wait_recv()
    for cp in sends:
        cp.wait_send()
    own.wait()


def _xfer_out(x, scatter):
    return jax.ShapeDtypeStruct((N_DEV,) + (x.shape[1:] if scatter else x.shape), x.dtype)


def _pcall(body, args, *, grid, in_specs, out_specs, out_shape, scratch_shapes=(), sem, name, xfers=(), prefetch=()):
    out_specs, out_shape = list(out_specs), list(out_shape)
    n_pre, n_in, n_out, n_x, n_scr = len(prefetch), len(in_specs), len(out_specs), len(xfers), len(scratch_shapes)
    flags = [s for _, s in xfers]

    def wrapped(*refs):
        pre, refs = refs[:n_pre], refs[n_pre:]
        ins, xin = refs[:n_in], refs[n_in:n_in + n_x]
        outs = refs[n_in + n_x:n_in + n_x + n_out]
        xout = refs[n_in + n_x + n_out:n_in + 2 * n_x + n_out]
        scr = refs[n_in + 2 * n_x + n_out:]
        own, sems = scr[:n_scr], scr[n_scr:]
        ids = [pl.program_id(d) for d in range(len(grid))]
        first = functools.reduce(jnp.logical_and, [i == 0 for i in ids])
        last = functools.reduce(jnp.logical_and, [i == g - 1 for i, g in zip(ids, grid)])

        @pl.when(first)
        def _():
            for t in range(n_x):
                _xfer_start(xin[t], xout[t], *sems[3 * t:3 * t + 3], scatter=flags[t])

        body(*pre, *ins, *outs, *own)

        @pl.when(last)
        def _():
            for t in range(n_x):
                _xfer_wait(xin[t], xout[t], *sems[3 * t:3 * t + 3], scatter=flags[t])

    grid_spec = pltpu.PrefetchScalarGridSpec(
        num_scalar_prefetch=n_pre, grid=grid, in_specs=list(in_specs) + [_ANY] * n_x,
        out_specs=out_specs + [_ANY] * n_x, scratch_shapes=list(scratch_shapes) + list(_XFER_SEMS) * n_x)
    return pl.pallas_call(
        wrapped if xfers else body, grid_spec=grid_spec, out_shape=out_shape + [_xfer_out(x, s) for x, s in xfers],
        compiler_params=_params(*(["arbitrary"] * len(grid) if xfers else sem)), name=name,
    )(*prefetch, *args, *[x for x, _ in xfers])


def _exchange(xfers, *, name):
    def body():
        pass

    return _pcall(body, (), grid=(1,), in_specs=[], out_specs=[], out_shape=[], sem=("arbitrary",), name=name,
                  xfers=xfers)


def _mm(a, b, *, ak="mk", bk="kn", ok="mn", add=None, out_dtype=F32, tm=1024, tn=1024, tk=1024, kg=1, name,
        xfers=()):
    sa, sb = a.shape, b.shape
    fm = fn = fk = None
    if ak == "mk":
        M, K, a_c = sa[0], sa[1], 1
    elif ak == "km":
        K, M, a_c = sa[0], sa[1], 0
    elif ak == "bmk":
        M, K, a_c, fk = sa[1], sa[0] * sa[2], 1, sa[2]
    else:
        K, M, a_c, fm = sa[1], sa[0] * sa[2], 0, sa[2]
    if bk == "kn":
        N, b_c = sb[1], 0
    elif bk == "nk":
        N, b_c = sb[0], 1
    elif bk == "bkn":
        N, b_c, fn = sb[0] * sb[2], 0, sb[2]
    elif bk == "bnk":
        N, b_c, fk = sb[1], 1, sb[2]
    elif bk == "kbn":
        N, b_c, fk = sb[2], 0, sb[1]
    else:
        N, b_c, fn = sb[0] * sb[1], 1, sb[1]
    tm, tn, tk = fm or _tile(M, tm), fn or _tile(N, tn), fk or _tile(K, tk)
    kblk = None if kg == 1 else kg
    nm, nn, nk = M // tm, N // tn, K // (tk * kg)

    a_spec = {"mk": pl.BlockSpec((tm, tk), lambda i, j, k: (i, k)),
              "km": pl.BlockSpec((tk, tm), lambda i, j, k: (k, i)),
              "bmk": pl.BlockSpec((kblk, tm, tk), lambda i, j, k: (k, i, 0)),
              "bkm": pl.BlockSpec((None, tk, tm), lambda i, j, k: (i, k, 0))}[ak]
    b_spec = {"kn": pl.BlockSpec((tk, tn), lambda i, j, k: (k, j)),
              "nk": pl.BlockSpec((tn, tk), lambda i, j, k: (j, k)),
              "bkn": pl.BlockSpec((None, tk, tn), lambda i, j, k: (j, k, 0)),
              "bnk": pl.BlockSpec((kblk, tn, tk), lambda i, j, k: (k, j, 0)),
              "kbn": pl.BlockSpec((kblk, tk, tn), lambda i, j, k: (k, 0, j)),
              "nbk": pl.BlockSpec((None, tn, tk), lambda i, j, k: (j, 0, k))}[bk]
    if ok == "mn":
        o_spec = pl.BlockSpec((tm, tn), lambda i, j, k: (i, j))
        out_shape = jax.ShapeDtypeStruct((M, N), out_dtype)
    elif ok == "bmn":
        o_spec = pl.BlockSpec((None, tm, tn), lambda i, j, k: (j, i, 0))
        out_shape = jax.ShapeDtypeStruct((nn, M, tn), out_dtype)
    else:
        o_spec = pl.BlockSpec((None, tm, tn), lambda i, j, k: (i, 0, j))
        out_shape = jax.ShapeDtypeStruct((nm, tm, N), out_dtype)
    dims = (((a_c,), (b_c,)), ((), ()))
    has_add = add is not None

    def body(*refs):
        a_ref, b_ref = refs[0], refs[1]
        add_ref = refs[2] if has_add else None
        o_ref = refs[3] if has_add else refs[2]
        if kg == 1:
            part = lax.dot_general(a_ref[...].astype(BF16), b_ref[...].astype(BF16), dims, preferred_element_type=F32)
        else:
            part = sum(lax.dot_general(a_ref[g].astype(BF16), b_ref[g].astype(BF16), dims,
                                       preferred_element_type=F32) for g in range(kg))

        def finish(r):
            if has_add:
                r = r + add_ref[...]
            o_ref[...] = r.astype(out_dtype)

        if nk == 1:
            finish(part)
            return
        acc = refs[-1]
        k = pl.program_id(2)

        @pl.when(k == 0)
        def _():
            acc[...] = part

        @pl.when(k > 0)
        def _():
            acc[...] += part

        @pl.when(k == nk - 1)
        def _():
            finish(acc[...])

    in_specs = [a_spec, b_spec]
    args = [a, b]
    if has_add:
        in_specs.append(pl.BlockSpec((tm, tn), lambda i, j, k: (i, j)))
        args.append(add)
    res = _pcall(body, args, grid=(nm, nn, nk), in_specs=in_specs, out_specs=[o_spec], out_shape=[out_shape],
                 scratch_shapes=[pltpu.VMEM((tm, tn), F32)] if nk > 1 else [],
                 sem=("parallel", "parallel", "arbitrary"), name=name, xfers=xfers)
    return res if xfers else res[0]


def _rms_fwd(h, g, *, name):
    L, D = h.shape
    tr = _tile(L, ROW_TILE, SUBLANES)

    def body(h_ref, g_ref, o_ref):
        x = h_ref[...]
        r = lax.rsqrt(jnp.mean(x * x, axis=1, keepdims=True) + EPS)
        o_ref[...] = (x * r * g_ref[...]).astype(BF16)

    return pl.pallas_call(
        body, grid=(L // tr,),
        in_specs=[pl.BlockSpec((tr, D), lambda i: (i, 0)), pl.BlockSpec((1, D), lambda i: (0, 0))],
        out_specs=pl.BlockSpec((tr, D), lambda i: (i, 0)), out_shape=jax.ShapeDtypeStruct((L, D), BF16),
        compiler_params=_params("parallel"), name=name)(h, g)


def _rms_bwd(h, g, dy, dres, *, name):
    L, D = h.shape
    tr = _tile(L, ROW_TILE, SUBLANES)

    def body(h_ref, g_ref, dy_ref, dres_ref, dh_ref, dg_ref):
        @pl.when(pl.program_id(0) == 0)
        def _():
            dg_ref[...] = jnp.zeros_like(dg_ref)

        x = h_ref[...]
        r = lax.rsqrt(jnp.mean(x * x, axis=1, keepdims=True) + EPS)
        xn = x * r
        dy = dy_ref[...].astype(F32)
        gdy = dy * g_ref[...]
        dx = r * (gdy - xn * jnp.mean(gdy * xn, axis=1, keepdims=True))
        dh_ref[...] = dres_ref[...] + dx
        dg_ref[...] += jnp.sum(dy * xn, axis=0, keepdims=True)

    row = pl.BlockSpec((tr, D), lambda i: (i, 0))
    vec = pl.BlockSpec((1, D), lambda i: (0, 0))
    return pl.pallas_call(
        body, grid=(L // tr,), in_specs=[row, vec, row, row], out_specs=[row, vec],
        out_shape=[jax.ShapeDtypeStruct((L, D), F32), jax.ShapeDtypeStruct((1, D), F32)],
        compiler_params=_params("arbitrary"), name=name)(h, g, dy, dres)


def _glu_res_rms(z, h, g, *, name):
    L, D = h.shape
    nb, cb = z.shape[1], z.shape[3]
    tr = _tile(L, ROW_TILE, SUBLANES)

    def body(z_ref, h_ref, g_ref, h1_ref, hn_ref):
        za = jnp.concatenate([z_ref[0, d] for d in range(nb)], axis=1)
        zg = jnp.concatenate([z_ref[1, d] for d in range(nb)], axis=1)
        x = h_ref[...] + za * jax.nn.sigmoid(zg)
        h1_ref[...] = x
        r = lax.rsqrt(jnp.mean(x * x, axis=1, keepdims=True) + EPS)
        hn_ref[...] = (x * r * g_ref[...]).astype(BF16)

    row = pl.BlockSpec((tr, D), lambda i: (i, 0))
    return pl.pallas_call(
        body, grid=(L // tr,),
        in_specs=[pl.BlockSpec((2, nb, tr, cb), lambda i: (0, 0, i, 0)), row, pl.BlockSpec((1, D), lambda i: (0, 0))],
        out_specs=[row, row],
        out_shape=[jax.ShapeDtypeStruct((L, D), F32), jax.ShapeDtypeStruct((L, D), BF16)],
        compiler_params=_params("parallel"), name=name)(z, h, g)


def _glu_bwd(z, dout, *, name):
    L, D = dout.shape
    nb, cb = z.shape[1], z.shape[3]
    tr = _tile(L, ROW_TILE, SUBLANES)

    def body(z_ref, d_ref, o_ref):
        for d in range(nb):
            dd = d_ref[:, d * cb:(d + 1) * cb]
            sg = jax.nn.sigmoid(z_ref[1, d])
            o_ref[0, d] = (dd * sg).astype(BF16)
            o_ref[1, d] = (dd * z_ref[0, d] * sg * (1.0 - sg)).astype(BF16)

    zs = pl.BlockSpec((2, nb, tr, cb), lambda i: (0, 0, i, 0))
    return pl.pallas_call(
        body, grid=(L // tr,), in_specs=[zs, pl.BlockSpec((tr, D), lambda i: (i, 0))], out_specs=zs,
        out_shape=jax.ShapeDtypeStruct(z.shape, BF16),
        compiler_params=_params("parallel"), name=name)(z, dout)


def _loss_head(h, g, target, *, name):
    L, D = h.shape
    tr = _tile(L, ROW_TILE, SUBLANES)

    def body(h_ref, g_ref, t_ref, loss_ref, dh_ref, dg_ref):
        @pl.when(pl.program_id(0) == 0)
        def _():
            dg_ref[...] = jnp.zeros_like(dg_ref)
            loss_ref[...] = jnp.zeros_like(loss_ref)

        x = h_ref[...]
        gg = g_ref[...]
        r = lax.rsqrt(jnp.mean(x * x, axis=1, keepdims=True) + EPS)
        xn = x * r
        err = xn * gg - t_ref[...]
        loss_ref[...] += 0.5 * jnp.sum(jnp.mean(err * err, axis=1, keepdims=True), axis=0, keepdims=True)
        dy = err * (1.0 / D)
        gdy = dy * gg
        dh_ref[...] = r * (gdy - xn * jnp.mean(gdy * xn, axis=1, keepdims=True))
        dg_ref[...] += jnp.sum(dy * xn, axis=0, keepdims=True)

    row = pl.BlockSpec((tr, D), lambda i: (i, 0))
    vec = pl.BlockSpec((1, D), lambda i: (0, 0))
    return pl.pallas_call(
        body, grid=(L // tr,), in_specs=[row, vec, row],
        out_specs=[pl.BlockSpec((1, 1), lambda i: (0, 0)), row, vec],
        out_shape=[jax.ShapeDtypeStruct((1, 1), F32), jax.ShapeDtypeStruct((L, D), F32),
                   jax.ShapeDtypeStruct((1, D), F32)],
        compiler_params=_params("arbitrary"), name=name)(h, g, target)


CONV_ROW_TILE = 256


def _sigmoid(x):
    return pl.reciprocal(1.0 + jnp.exp(-x), approx=True)


def _conv_specs(L, tr, tc):
    nrb = tr // SUBLANES
    before = lambda i: jnp.maximum(i * nrb - 1, 0)
    after = lambda i: jnp.minimum((i + 1) * nrb, L // SUBLANES - 1)
    main = pl.BlockSpec((2, None, tr, tc), lambda j, i: (0, j, i, 0))
    prev = pl.BlockSpec((2, None, SUBLANES, tc), lambda j, i: (0, j, before(i), 0))
    nxt = pl.BlockSpec((2, None, SUBLANES, tc), lambda j, i: (0, j, after(i), 0))
    cw = pl.BlockSpec((2, None, 3, tc), lambda j, i: (0, j, 0, 0))
    cb = pl.BlockSpec((2, None, 1, tc), lambda j, i: (0, j, 0, 0))
    half = pl.BlockSpec((None, tr, tc), lambda j, i: (j, i, 0))
    half_nxt = pl.BlockSpec((None, SUBLANES, tc), lambda j, i: (j, after(i), 0))
    return main, prev, nxt, cw, cb, half, half_nxt


def _conv_rows(xe, w, b):
    x1 = pltpu.roll(xe, 1, 0)
    x2 = pltpu.roll(xe, 2, 0)
    return b + x2 * w[0:1] + x1 * w[1:2] + xe * w[2:3], x1, x2


def _shift_down(x, halo, k, row):
    y = pltpu.roll(x, k, 0)
    for r in range(k):
        y = jnp.where(row == r, halo[SUBLANES - k + r:SUBLANES - k + r + 1, :], y)
    return y


def _conv_act(u0, cw, cb, *, name, xfers=()):
    _, nb, L, tc = u0.shape
    tr = _tile(L, ROW_TILE, SUBLANES)
    main, prev, _, cws, cbs, half, _ = _conv_specs(L, tr, tc)

    def body(u_ref, p_ref, w_ref, b_ref, a_ref):
        first = pl.program_id(1) == 0
        row = lax.broadcasted_iota(jnp.int32, (tr, tc), 0)
        y = []
        for s in range(2):
            x, w = u_ref[s], w_ref[s]
            halo = jnp.where(first, 0.0, p_ref[s])
            x1 = _shift_down(x, halo, 1, row)
            x2 = _shift_down(x, halo, 2, row)
            y.append(b_ref[s] + x2 * w[0:1] + x1 * w[1:2] + x * w[2:3])
        a_ref[...] = (y[0] * _sigmoid(y[0]) * y[1]).astype(BF16)

    return _pcall(body, (u0, u0, cw, cb), grid=(nb, L // tr), in_specs=[main, prev, cws, cbs], out_specs=[half],
                  out_shape=[jax.ShapeDtypeStruct((nb, L, tc), BF16)], sem=("parallel", "parallel"), name=name,
                  xfers=xfers)


def _conv_ffn_bwd(u0, cw, cb, da, *, name, xfers=()):
    _, nb, L, tc = u0.shape
    tr = _tile(L, CONV_ROW_TILE, SUBLANES)
    main, prev, nxt, cws, cbs, half, half_nxt = _conv_specs(L, tr, tc)
    nr = L // tr
    H = SUBLANES

    def body(u_ref, p_ref, n_ref, w_ref, b_ref, da_ref, dan_ref, a_ref, du0_ref, dcw_ref, dcb_ref):
        i = pl.program_id(1)

        @pl.when(i == 0)
        def _():
            dcw_ref[...] = jnp.zeros_like(dcw_ref)
            dcb_ref[...] = jnp.zeros_like(dcb_ref)

        y, x1, x2 = [], [], []
        for s in range(2):
            xe = jnp.concatenate([jnp.where(i == 0, 0.0, p_ref[s]), u_ref[s], n_ref[s]], axis=0)
            ys, x1s, x2s = _conv_rows(xe, w_ref[s], b_ref[s])
            y.append(ys[H:])
            x1.append(x1s[H:H + tr])
            x2.append(x2s[H:H + tr])
        gate, up = y
        da = jnp.concatenate([da_ref[...], dan_ref[...]], axis=0)
        row = lax.broadcasted_iota(jnp.int32, (tr + H, tc), 0)
        da = jnp.where(jnp.logical_and(i == nr - 1, row >= tr), 0.0, da)
        sg = _sigmoid(gate)
        silu = gate * sg
        a_ref[...] = (silu * up)[:tr].astype(BF16)
        d = (da * up * (sg * (1.0 + gate * (1.0 - sg))), da * silu)
        for s in range(2):
            w = w_ref[s]
            d0 = d[s][:tr]
            d1 = pltpu.roll(d[s], tr + H - 1, 0)[:tr]
            d2 = pltpu.roll(d[s], tr + H - 2, 0)[:tr]
            du0_ref[s] = (d0 * w[2:3] + d1 * w[1:2] + d2 * w[0:1]).astype(BF16)
            dcw_ref[s, 0:1, :] += jnp.sum(d0 * x2[s], axis=0, keepdims=True)
            dcw_ref[s, 1:2, :] += jnp.sum(d0 * x1[s], axis=0, keepdims=True)
            dcw_ref[s, 2:3, :] += jnp.sum(d0 * u_ref[s], axis=0, keepdims=True)
            dcb_ref[s] += jnp.sum(d0, axis=0, keepdims=True)

    return _pcall(body, (u0, u0, u0, cw, cb, da, da), grid=(nb, nr),
                  in_specs=[main, prev, nxt, cws, cbs, half, half_nxt], out_specs=[half, main, cws, cbs],
                  out_shape=[jax.ShapeDtypeStruct((nb, L, tc), BF16), jax.ShapeDtypeStruct((2, nb, L, tc), BF16),
                             jax.ShapeDtypeStruct((2, nb, 3, tc), F32), jax.ShapeDtypeStruct((2, nb, 1, tc), F32)],
                  sem=("parallel", "arbitrary"), name=name, xfers=xfers)


N_TILES = 64
HALF = N_TILES // 2


def _swap(s):
    return jnp.concatenate([s[HALF:], s[:HALF]], axis=0)


def _chan_block(j):
    return ((j % HALF) // 4) * LANES


def _pairs_of(jb):
    return [2 * jb, 2 * jb + 1, HALF // 2 + 2 * jb, HALF // 2 + 2 * jb + 1]


_WB_SPEC = pl.BlockSpec((HALF, LANES, 2 * LANES), lambda c: (0, 0, 0))
_WC_SPEC = pl.BlockSpec((HALF, 2 * LANES, LANES), lambda c: (0, 0, 0))
GELU_C = math.sqrt(2.0 / math.pi)
GELU_A = 0.044715


def _gelu(x):
    return 0.5 * x * (1.0 + jnp.tanh(GELU_C * (x + GELU_A * x * x * x)))


def _gelu_grad(x):
    th = jnp.tanh(GELU_C * (x + GELU_A * x * x * x))
    return 0.5 * (1.0 + th) + 0.5 * x * (1.0 - th * th) * GELU_C * (1.0 + 3.0 * GELU_A * x * x)


def _tile_rows(j, T, TP):
    return pl.ds(j * TP + SUBLANES, T)


def _pair(ref, jp, T, TP):
    return jnp.concatenate([ref[_tile_rows(2 * jp, T, TP), :], ref[_tile_rows(2 * jp + 1, T, TP), :]],
                           axis=1).astype(BF16)


def _unpair(ref, jp, val, T, TP):
    ref[_tile_rows(2 * jp, T, TP), :] = val[:, :LANES]
    ref[_tile_rows(2 * jp + 1, T, TP), :] = val[:, LANES:]


def _s5_project_in(u_ref, wb_ref, s3, T, TP):
    for jp in range(HALF):
        blk = _chan_block(2 * jp)
        _unpair(s3, jp, jnp.dot(u_ref[:, blk:blk + LANES], wb_ref[jp], preferred_element_type=F32), T, TP)


def _s5_scan_fwd(s3, a1, a2, s0, T, TP):
    span = (N_GROUPS - 1) * TP + 2 * SUBLANES

    def blk(i, s):
        view = s3.at[pl.ds(pl.multiple_of(i * SUBLANES, SUBLANES), span)]
        for k in range(SUBLANES):
            rows = pl.ds(SUBLANES + k, N_GROUPS, stride=TP)
            s = a1 * s + a2 * _swap(s) + view[rows, :]
            view[rows, :] = s
        return s

    return lax.fori_loop(0, T // SUBLANES, blk, s0)


def _s5_fwd(hn, wb, wc, a1, a2, dvec, *, name, xfers=()):
    L, D = hn.shape
    T = min(S5_CHUNK, L)
    TP = T + SUBLANES
    nC = L // T

    def body(u_ref, wb_ref, wc_ref, a1_ref, a2_ref, d_ref, y_ref, yg_ref, sb_ref, s3, st):
        @pl.when(pl.program_id(0) == 0)
        def _():
            st[...] = jnp.zeros_like(st)

        sb_ref[0] = st[...]
        _s5_project_in(u_ref, wb_ref, s3, T, TP)
        st[...] = _s5_scan_fwd(s3, a1_ref[...], a2_ref[...], st[...], T, TP)
        for jb in range(D // LANES):
            acc = jnp.zeros((T, LANES), F32)
            for jp in _pairs_of(jb):
                acc += jnp.dot(_pair(s3, jp, T, TP), wc_ref[jp], preferred_element_type=F32)
            cols = slice(jb * LANES, (jb + 1) * LANES)
            y = acc + d_ref[:, cols] * u_ref[:, cols].astype(F32)
            y_ref[:, cols] = y
            yg_ref[:, cols] = _gelu(y).astype(BF16)

    row = pl.BlockSpec((T, D), lambda c: (c, 0))
    aspec = pl.BlockSpec((N_GROUPS, LANES), lambda c: (0, 0))
    return _pcall(
        body, (hn, wb, wc, a1, a2, dvec), grid=(nC,),
        in_specs=[row, _WB_SPEC, _WC_SPEC, aspec, aspec, pl.BlockSpec((1, D), lambda c: (0, 0))],
        out_specs=[row, row, pl.BlockSpec((1, N_GROUPS, LANES), lambda c: (c, 0, 0))],
        out_shape=[jax.ShapeDtypeStruct((L, D), F32), jax.ShapeDtypeStruct((L, D), BF16),
                   jax.ShapeDtypeStruct((nC, N_GROUPS, LANES), F32)],
        scratch_shapes=[pltpu.VMEM((N_GROUPS * TP, LANES), F32), pltpu.VMEM((N_GROUPS, LANES), F32)],
        sem=("arbitrary",), name=name, xfers=xfers)


def _s5_bwd(hn, dyg, ypre, sbound, wb, wc, a1, a2, dvec, *, name, xfers=()):
    L, D = hn.shape
    T = min(S5_CHUNK, L)
    TP = T + SUBLANES
    nC = L // T
    span = (N_GROUPS - 1) * TP + 2 * SUBLANES
    NT = (((1,), (1,)), ((), ()))
    TN = (((0,), (0,)), ((), ()))

    def body(u_ref, dyg_ref, yp_ref, sb_ref, wb_ref, wc_ref, a1_ref, a2_ref, d_ref,
             du_ref, dwb_ref, dwc_ref, da1_ref, da2_ref, dd_ref, s3, g3, gst, dy_s):
        @pl.when(pl.program_id(0) == 0)
        def _():
            gst[...] = jnp.zeros_like(gst)
            dwb_ref[...] = jnp.zeros_like(dwb_ref)
            dwc_ref[...] = jnp.zeros_like(dwc_ref)
            da1_ref[...] = jnp.zeros_like(da1_ref)
            da2_ref[...] = jnp.zeros_like(da2_ref)
            dd_ref[...] = jnp.zeros_like(dd_ref)

        a1 = a1_ref[...]
        a2 = a2_ref[...]
        dy = dyg_ref[...].astype(F32) * _gelu_grad(yp_ref[...])
        dy_s[...] = dy.astype(BF16)
        dd_ref[...] += jnp.sum(dy * u_ref[...].astype(F32), axis=0, keepdims=True)
        du_ref[...] = d_ref[...] * dy

        s3[pl.ds(SUBLANES - 1, N_GROUPS, stride=TP), :] = sb_ref[0]
        _s5_project_in(u_ref, wb_ref, s3, T, TP)
        _s5_scan_fwd(s3, a1, a2, sb_ref[0], T, TP)

        for jp in range(HALF):
            blk = _chan_block(2 * jp)
            _unpair(g3, jp, lax.dot_general(dy_s[:, blk:blk + LANES], wc_ref[jp], NT, preferred_element_type=F32),
                    T, TP)
        a2c = -a2

        def rblk(ii, carry):
            g, acc1, acc2 = carry
            t0 = pl.multiple_of((T // SUBLANES - 1 - ii) * SUBLANES, SUBLANES)
            gv = g3.at[pl.ds(t0, span)]
            sv = s3.at[pl.ds(t0, span)]
            for k in reversed(range(SUBLANES)):
                rows = pl.ds(SUBLANES + k, N_GROUPS, stride=TP)
                g = a1 * g + a2c * _swap(g) + gv[rows, :]
                gv[rows, :] = g
                sp = sv[pl.ds(SUBLANES - 1 + k, N_GROUPS, stride=TP), :]
                acc1 = acc1 + g * sp
                acc2 = acc2 + g * _swap(sp)
            return g, acc1, acc2

        zero = jnp.zeros((N_GROUPS, LANES), F32)
        g, acc1, acc2 = lax.fori_loop(0, T // SUBLANES, rblk, (gst[...], zero, zero))
        gst[...] = g
        da1_ref[...] += acc1
        da2_ref[...] += acc2

        for jb in range(D // LANES):
            cols = slice(jb * LANES, (jb + 1) * LANES)
            acc = jnp.zeros((T, LANES), F32)
            for jp in _pairs_of(jb):
                gp = _pair(g3, jp, T, TP)
                dwc_ref[jp] += lax.dot_general(_pair(s3, jp, T, TP), dy_s[:, cols], TN, preferred_element_type=F32)
                dwb_ref[jp] += lax.dot_general(u_ref[:, cols], gp, TN, preferred_element_type=F32)
                acc += lax.dot_general(gp, wb_ref[jp], NT, preferred_element_type=F32)
            du_ref[:, cols] += acc

    rrow = pl.BlockSpec((T, D), lambda c: (nC - 1 - c, 0))
    aspec = pl.BlockSpec((N_GROUPS, LANES), lambda c: (0, 0))
    vec = pl.BlockSpec((1, D), lambda c: (0, 0))
    return _pcall(
        body, (hn, dyg, ypre, sbound, wb, wc, a1, a2, dvec), grid=(nC,),
        in_specs=[rrow, rrow, rrow, pl.BlockSpec((1, N_GROUPS, LANES), lambda c: (nC - 1 - c, 0, 0)),
                  _WB_SPEC, _WC_SPEC, aspec, aspec, vec],
        out_specs=[rrow, _WB_SPEC, _WC_SPEC, aspec, aspec, vec],
        out_shape=[jax.ShapeDtypeStruct((L, D), F32),
                   jax.ShapeDtypeStruct((HALF, LANES, 2 * LANES), F32),
                   jax.ShapeDtypeStruct((HALF, 2 * LANES, LANES), F32),
                   jax.ShapeDtypeStruct((N_GROUPS, LANES), F32), jax.ShapeDtypeStruct((N_GROUPS, LANES), F32),
                   jax.ShapeDtypeStruct((1, D), F32)],
        scratch_shapes=[pltpu.VMEM((N_GROUPS * TP, LANES), F32), pltpu.VMEM((N_GROUPS * TP, LANES), F32),
                        pltpu.VMEM((N_GROUPS, LANES), F32), pltpu.VMEM((T, D), BF16)],
        sem=("arbitrary",), name=name, xfers=xfers)


def _s5_prep(lam_re, lam_im, log_dt, b_re, b_im, c_re, c_im):
    dt = jnp.exp(log_dt)[:, None]
    mag = jnp.exp(lam_re * dt)
    lb_re = mag * jnp.cos(lam_im * dt)
    lb_im = mag * jnp.sin(lam_im * dt)
    den = lam_re * lam_re + lam_im * lam_im
    nr = lb_re - 1.0
    fr = ((nr * lam_re + lb_im * lam_im) / den)[..., None]
    fi = ((lb_im * lam_re - nr * lam_im) / den)[..., None]
    bb_re = fr * b_re - fi * b_im
    bb_im = fr * b_im + fi * b_re
    pair = lambda a: a.reshape(HALF, 2 * SSM_STATE)
    a1 = jnp.concatenate([pair(lb_re), pair(lb_re)], axis=0)
    a2 = jnp.concatenate([-pair(lb_im), pair(lb_im)], axis=0)
    sel = jax.nn.one_hot(jnp.arange(HALF) % 4, 4, dtype=F32)
    eye = jnp.eye(2, dtype=F32)

    def w_in(bb):
        return jnp.einsum('jk,ef,jfph->jkehfp', sel, eye, bb.reshape(HALF, 2, SSM_STATE, SSM_GROUP)
                          ).reshape(HALF, LANES, LANES)

    def w_out(c):
        return jnp.einsum('jk,ef,jfhp->jepkfh', sel, eye, c.reshape(HALF, 2, SSM_GROUP, SSM_STATE)
                          ).reshape(HALF, LANES, LANES)

    wb = jnp.concatenate([w_in(bb_re), w_in(bb_im)], axis=0)
    wc = jnp.concatenate([w_out(c_re), w_out(-c_im)], axis=0)
    wb = wb.reshape(HALF, 2, LANES, LANES).transpose(0, 2, 1, 3).reshape(HALF, LANES, 2 * LANES)
    wc = wc.reshape(HALF, 2 * LANES, LANES)
    return a1, a2, wb, wc


def _tri(n, upper):
    r = lax.broadcasted_iota(jnp.int32, (n, n), 0)
    c = lax.broadcasted_iota(jnp.int32, (n, n), 1)
    return ((r <= c) if upper else (r >= c)).astype(F32)


def _fgate_fwd(fl, bf, *, name):
    L, W = fl.shape
    tr = _tile(L, CUM_TILE, SUBLANES)

    def body(f_ref, b_ref, o_ref, carry):
        @pl.when(pl.program_id(0) == 0)
        def _():
            carry[...] = jnp.zeros_like(carry)

        x = f_ref[...] + b_ref[...]
        ls = jnp.minimum(x, 0.0) - jnp.log(1.0 + jnp.exp(-jnp.abs(x)))
        cum = jnp.dot(_tri(tr, False), ls, preferred_element_type=F32, precision=lax.Precision.HIGHEST) + carry[...]
        o_ref[...] = cum
        carry[...] = cum[tr - 1:tr, :]

    return pl.pallas_call(
        body, grid=(L // tr,),
        in_specs=[pl.BlockSpec((tr, W), lambda i: (i, 0)), pl.BlockSpec((1, W), lambda i: (0, 0))],
        out_specs=pl.BlockSpec((tr, W), lambda i: (i, 0)), out_shape=jax.ShapeDtypeStruct((L, W), F32),
        scratch_shapes=[pltpu.VMEM((1, W), F32)], compiler_params=_params("arbitrary"), name=name)(fl, bf)


def _fgate_bwd(fl, bf, dcum, *, name):
    L, W = fl.shape
    tr = _tile(L, CUM_TILE, SUBLANES)
    n = L // tr

    def body(f_ref, b_ref, d_ref, o_ref, db_ref, carry):
        @pl.when(pl.program_id(0) == 0)
        def _():
            carry[...] = jnp.zeros_like(carry)
            db_ref[...] = jnp.zeros_like(db_ref)

        d = d_ref[...]
        rev = jnp.dot(_tri(tr, True), d, preferred_element_type=F32, precision=lax.Precision.HIGHEST) + carry[...]
        carry[...] += jnp.sum(d, axis=0, keepdims=True)
        df = rev * jax.nn.sigmoid(-(f_ref[...] + b_ref[...]))
        o_ref[...] = df
        db_ref[...] += jnp.sum(df, axis=0, keepdims=True)

    rrow = pl.BlockSpec((tr, W), lambda i: (n - 1 - i, 0))
    vec = pl.BlockSpec((1, W), lambda i: (0, 0))
    return pl.pallas_call(
        body, grid=(n,), in_specs=[rrow, vec, rrow], out_specs=[rrow, vec],
        out_shape=[jax.ShapeDtypeStruct((L, W), F32), jax.ShapeDtypeStruct((1, W), F32)],
        scratch_shapes=[pltpu.VMEM((1, W), F32)], compiler_params=_params("arbitrary"), name=name)(fl, bf, dcum)


_NT = (((1,), (1,)), ((), ()))
_TN = (((0,), (0,)), ((), ()))
HEAD_PAIRS = N_HEADS // 2


def _causal_tiles(n, by_row):
    pairs = ([(i, j) for i in range(n) for j in range(i + 1)] if by_row
             else [(i, j) for j in range(n) for i in range(j, n)])
    return (jnp.array([p[0] for p in pairs], jnp.int32), jnp.array([p[1] for p in pairs], jnp.int32))


def _attn_logits(qs, k, ck, masked, t):
    s = lax.dot_general(qs, k, _NT, preferred_element_type=F32) - ck
    if masked:
        r = lax.broadcasted_iota(jnp.int32, (t, t), 0)
        c = lax.broadcasted_iota(jnp.int32, (t, t), 1)
        s = jnp.where(c > r, NEG, s)
    return s


def _attn_fwd(q, kv, ck, *, name, xfers=()):
    L, D = q.shape
    t = _tile(L, ATTN_TILE)
    n = L // t
    dh = HEAD_DIM

    def body(q_ref, k_ref, v_ref, ck_ref, o_ref, o32_ref, lse_ref, m_s, l_s, acc):
        i, j = pl.program_id(1), pl.program_id(2)

        @pl.when(j == 0)
        def _():
            m_s[...] = jnp.full_like(m_s, NEG)
            l_s[...] = jnp.zeros_like(l_s)
            acc[...] = jnp.zeros_like(acc)

        def tile(masked):
            for e in range(2):
                sl = slice(e * dh, (e + 1) * dh)
                v = v_ref[:, sl]
                s = _attn_logits(q_ref[:, sl] * ATTN_SCALE, k_ref[:, sl], ck_ref[e], masked, t)
                m_new = jnp.maximum(m_s[e], jnp.max(s, axis=1, keepdims=True))
                alpha = jnp.exp(m_s[e] - m_new)
                p = jnp.exp(s - m_new)
                l_s[e] = alpha * l_s[e] + jnp.sum(p, axis=1, keepdims=True)
                p_hi = p.astype(BF16)
                p_lo = (p - p_hi.astype(F32)).astype(BF16)
                pv = (jnp.dot(p_hi, v, preferred_element_type=F32) + jnp.dot(p_lo, v, preferred_element_type=F32))
                acc[e] = alpha * acc[e] + pv
                m_s[e] = m_new

        pl.when(j < i)(functools.partial(tile, False))
        pl.when(j == i)(functools.partial(tile, True))

        @pl.when(j == n - 1)
        def _():
            for e in range(2):
                sl = slice(e * dh, (e + 1) * dh)
                o = acc[e] / l_s[e]
                o_ref[:, sl] = o.astype(BF16)
                o32_ref[:, sl] = o
                lse_ref[e] = m_s[e] + jnp.log(l_s[e])

    qs = pl.BlockSpec((t, LANES), lambda h, i, j: (i, h))
    ks = pl.BlockSpec((t, LANES), lambda h, i, j: (jnp.minimum(i, j), h))
    vs = pl.BlockSpec((t, LANES), lambda h, i, j: (jnp.minimum(i, j), HEAD_PAIRS + h))
    cs = pl.BlockSpec((2, 1, t), lambda h, i, j: (h, 0, jnp.minimum(i, j)))
    return _pcall(
        body, (q, kv, kv, ck), grid=(HEAD_PAIRS, n, n), in_specs=[qs, ks, vs, cs],
        out_specs=[qs, qs, pl.BlockSpec((2, t, 1), lambda h, i, j: (h, i, 0))],
        out_shape=[jax.ShapeDtypeStruct((L, D), BF16), jax.ShapeDtypeStruct((L, D), F32),
                   jax.ShapeDtypeStruct((N_HEADS, L, 1), F32)],
        scratch_shapes=[pltpu.VMEM((2, t, 1), F32), pltpu.VMEM((2, t, 1), F32), pltpu.VMEM((2, t, dh), F32)],
        sem=("parallel", "parallel", "arbitrary"), name=name, xfers=xfers)


def _attn_bwd(q, kv, ck, o, do, lse, *, name, xfers=()):
    L, D = q.shape
    t = _tile(L, ATTN_TILE)
    n = L // t
    dh = HEAD_DIM

    def body(i_tab, j_tab, q_ref, k_ref, v_ref, ck_ref, o_ref, do_ref, lse_ref, dq_ref, dk_ref, dv_ref, dck_ref):
        i, j = i_tab[pl.program_id(1)], j_tab[pl.program_id(1)]

        @pl.when(pl.program_id(1) == 0)
        def _():
            dq_ref[...] = jnp.zeros_like(dq_ref)

        @pl.when(i == j)
        def _():
            dk_ref[...] = jnp.zeros_like(dk_ref)
            dv_ref[...] = jnp.zeros_like(dv_ref)
            dck_ref[...] = jnp.zeros_like(dck_ref)

        def tile(masked):
            rows = pl.ds(pl.multiple_of(i * t, t), t)
            for e in range(2):
                sl = slice(e * dh, (e + 1) * dh)
                qs = q_ref[:, sl] * ATTN_SCALE
                k = k_ref[:, sl]
                do = do_ref[:, sl]
                s = _attn_logits(qs, k, ck_ref[e], masked, t)
                p = jnp.exp(s - lse_ref[e])
                dp = lax.dot_general(do, v_ref[:, sl], _NT, preferred_element_type=F32)
                delta = jnp.sum(do.astype(F32) * o_ref[:, sl], axis=1, keepdims=True)
                ds = p * (dp - delta)
                ds16 = ds.astype(BF16)
                dv_ref[:, sl] += lax.dot_general(p.astype(BF16), do, _TN, preferred_element_type=F32)
                dk_ref[:, sl] += lax.dot_general(ds16, qs, _TN, preferred_element_type=F32)
                dck_ref[e] -= jnp.sum(ds, axis=0, keepdims=True)
                dq_ref[rows, sl] += jnp.dot(ds16, k, preferred_element_type=F32) * ATTN_SCALE

        pl.when(i > j)(functools.partial(tile, False))
        pl.when(i == j)(functools.partial(tile, True))

    qs = pl.BlockSpec((t, LANES), lambda h, s, it, jt: (it[s], h))
    ks = pl.BlockSpec((t, LANES), lambda h, s, it, jt: (jt[s], h))
    vs = pl.BlockSpec((t, LANES), lambda h, s, it, jt: (jt[s], HEAD_PAIRS + h))
    cs = pl.BlockSpec((2, 1, t), lambda h, s, it, jt: (h, 0, jt[s]))
    ls = pl.BlockSpec((2, t, 1), lambda h, s, it, jt: (h, it[s], 0))
    full = jax.ShapeDtypeStruct((L, D), F32)
    tabs = _causal_tiles(n, by_row=False)
    return _pcall(
        body, (q, kv, kv, ck, o, do, lse), grid=(HEAD_PAIRS, tabs[0].shape[0]), in_specs=[qs, ks, vs, cs, qs, qs, ls],
        out_specs=[pl.BlockSpec((L, LANES), lambda h, s, it, jt: (0, h)), ks, ks, cs],
        out_shape=[full, full, full, jax.ShapeDtypeStruct((N_HEADS, 1, L), F32)],
        sem=("parallel", "arbitrary"), name=name, xfers=xfers, prefetch=tabs)


SHARD_COLS_FFN = 2 * D_FF // N_DEV
SHARD_ROWS_FFN = D_FF // N_DEV
SHARD_COLS_GLU = 2 * D_MODEL // N_DEV
SHARD_ROWS_QO = D_MODEL // N_DEV
SHARD_COLS_KVF = (2 * D_MODEL + N_HEADS) // N_DEV
S5_NAMES = ("lam_re", "lam_im", "log_dt", "ssm_b_re", "ssm_b_im", "ssm_c_re", "ssm_c_im")
REPL_LATE_ROWS = 288
REPL_EARLY_ROWS = 320


def _leaves(parts):
    out = []
    for p in parts:
        out.extend(_leaves(p) if isinstance(p, (list, tuple)) else [p.reshape(-1)])
    return out


def _pack_rows(parts, rows):
    flat = jnp.concatenate(_leaves(parts))
    return jnp.pad(flat, (0, rows * D_MODEL - flat.shape[0])).reshape(rows, D_MODEL)


def _unpack_rows(flat, like):
    flat, out, off = flat.reshape(-1), [], 0
    for p in _leaves(like):
        out.append(flat[off:off + p.shape[0]])
        off += p.shape[0]
    return out


def _repl_late(d):
    return [d["g_mix"][0], [d[n][0] for n in S5_NAMES]]


def _repl_early(d):
    return [list(d["g_mix"][1:]), list(d["g_ffn"]), [d[n][1] for n in S5_NAMES], d["g_kv"], d["b_f"],
            list(d["ffn_conv_b"]), d["g_final"]]


def _kvf_blocks(full):
    return full.reshape(D_MODEL, N_DEV, SHARD_COLS_KVF).transpose(1, 0, 2)


class _Step:
    def __init__(self, weights, send=None, plan=None):
        self.w = dict(weights)
        self.send = send or {}
        self.grad = {}
        self.bcast = {}
        self.slots = {}
        self.plan = plan or {}

    def xfers(self, host):
        src = {"w": self.send, "g": self.grad, "b": self.bcast}
        return [(src[kind][k], kind == "g") for kind, k in self.plan.get(host, ())]

    def land(self, host, gathered):
        for (kind, k), g in zip(self.plan.get(host, ()), gathered):
            if kind == "w":
                self.arrive(k, g)
            else:
                self.slots[k] = g

    def arrive(self, k, g):
        name = k[0]
        if name == "w_ffn_in" and len(k) == 3:
            self.w[k] = g
            halves = [self.w.get((name, k[1], h)) for h in range(2)]
            if all(h is not None for h in halves):
                self.w[name, k[1]] = jnp.concatenate(halves, axis=1)
        elif name == "w_ffn_out":
            self.w[k] = g.reshape(4, 2 * SHARD_ROWS_FFN, D_MODEL)
        elif name in ("w_q", "w_o"):
            self.w[k] = g.reshape(D_MODEL, D_MODEL)
        elif name == "w_kvf":
            full = g.transpose(1, 0, 2).reshape(D_MODEL, N_DEV * SHARD_COLS_KVF)
            self.w["w_kv",] = full[:, :2 * D_MODEL]
            self.w["w_f",] = jnp.pad(full[:, 2 * D_MODEL:], ((0, 0), (0, LANES - N_HEADS)))
        elif name == "small":
            flat = g.reshape(N_DEV, -1)
            self.w["ssm_d",] = flat[:, :256].reshape(N_DEV, N_A, LANES).transpose(1, 0, 2).reshape(N_A, D_MODEL)
            cw = flat[:, 256:256 + DEPTH * 3 * SHARD_COLS_FFN].reshape(N_DEV, DEPTH, 3, SHARD_COLS_FFN)
            for layer in range(DEPTH):
                self.w["conv_w", layer] = cw[:, layer].reshape(2, 4, 3, SHARD_COLS_FFN)
        else:
            self.w[k] = g

    def run(self, host, fn, *args, **kw):
        xf = self.xfers(host)
        res = fn(*args, name=host, xfers=xf, **kw)
        if not xf:
            return res[0] if isinstance(res, (list, tuple)) and len(res) == 1 else res
        n_own = len(res) - len(xf)
        self.land(host, res[n_own:])
        return res[0] if n_own == 1 else res[:n_own]


def _step(x, target, S):
    L = x.shape[0]
    W = S.w
    vec = lambda a: a.reshape(1, -1)
    CF = SHARD_COLS_FFN

    h = x
    saved = []
    kvs = None
    for layer in range(DEPTH):
        t = str(layer)
        if layer < N_A:
            (a1, a2, wb, wc), prep_vjp = jax.vjp(_s5_prep, *[W[n][layer] for n in S5_NAMES])
            wb16, wc16 = wb.astype(BF16), wc.astype(BF16)
            hn = _rms_fwd(h, vec(W["g_mix"][layer]), name="mix_norm" + t)
            dvec = vec(W["ssm_d",][layer])
            ypre, yg, sb = S.run("s5_fwd" + t, _s5_fwd, hn, wb16, wc16, a1, a2, dvec)
            z = S.run("glu_mm" + t, _mm, yg, W["w_glu", layer], bk="bkn", ok="bmn", tm=4096)
            z = z.reshape(2, 4, L, SHARD_COLS_GLU)
            h1, hn2 = _glu_res_rms(z, h, vec(W["g_ffn"][layer]), name="glu_res" + t)
            mix_saved = (h, hn, ypre, yg, sb, z, a1, a2, wb16, wc16, dvec, prep_vjp)
        else:
            j = layer - N_A
            if layer == N_A:
                hkv = _rms_fwd(h, vec(W["g_kv"]), name="kv_norm")
                kvm = S.run("kv_mm", _mm, hkv, W["w_kv",], out_dtype=BF16)
                fl = S.run("f_mm", _mm, hkv, W["w_f",])
                cum = _fgate_fwd(fl, W["b_f_pad",], name="fgate_fwd")
                ck = cum[:, :N_HEADS].T.reshape(N_HEADS, 1, L)
                kvs = (h, hkv, fl, kvm, ck)
            _, _, _, kvm, ck = kvs
            hn = _rms_fwd(h, vec(W["g_mix"][layer]), name="mix_norm" + t)
            q = S.run("q_mm" + t, _mm, hn, W["w_q", j], out_dtype=BF16)
            o, o32, lse = S.run("attn_fwd" + t, _attn_fwd, q, kvm, ck)
            h1 = S.run("o_mm" + t, _mm, o, W["w_o", j], add=h)
            hn2 = _rms_fwd(h1, vec(W["g_ffn"][layer]), name="ffn_norm" + t)
            mix_saved = (h, hn, q, o32, o, lse)
        u0 = S.run("ffn_in" + t, _mm, hn2, W["w_ffn_in", layer], bk="nbk", ok="bmn", tm=2048).reshape(2, 4, L, CF)
        a = S.run("ffn_act" + t, _conv_act, u0, W["conv_w", layer], W["conv_b", layer])
        h2 = S.run("ffn_out" + t, _mm, a, W["w_ffn_out", layer], ak="bmk", bk="kbn", add=h1, tn=512, kg=4)
        saved.append((mix_saved, h1, hn2, u0))
        h = h2

    loss, dh, dg_final = _loss_head(h, vec(W["g_final"]), target, name="loss_head")
    g = {"g_final": dg_final.reshape(-1)}
    gl = {k: [None] * DEPTH for k in ("g_mix", "g_ffn", "conv_w", "ffn_conv_b")}
    ga = {k: [None] * N_A for k in S5_NAMES + ("ssm_d",)}
    dk = dv = dck = None
    for layer in reversed(range(DEPTH)):
        t = str(layer)
        mix_saved, h1, hn2, u0 = saved[layer]
        cw, cb = W["conv_w", layer], W["conv_b", layer]
        da = S.run("ffn_da" + t, _mm, dh, W["w_ffn_out", layer], bk="nbk", ok="bmn", tm=2048)
        a, du0, dcw, dcb = S.run("ffn_conv_bwd" + t, _conv_ffn_bwd, u0, cw, cb, da)
        dw_out = S.run("ffn_dwout" + t, _mm, a, dh, ak="bkm", ok="mbn", out_dtype=BF16)
        S.grad["w_ffn_out", layer] = dw_out.reshape(N_DEV, SHARD_ROWS_FFN, D_MODEL)
        du0 = du0.reshape(N_DEV, L, CF)
        S.grad["w_ffn_in", layer] = S.run("ffn_dwin" + t, _mm, du0, hn2, ak="bkm", ok="mbn", out_dtype=BF16)
        dhn2 = S.run("ffn_dhn" + t, _mm, du0, W["w_ffn_in", layer], ak="bmk", bk="kbn", kg=4)
        dh1, dg = _rms_bwd(h1, vec(W["g_ffn"][layer]), dhn2, dh, name="ffn_norm_bwd" + t)
        gl["g_ffn"][layer], gl["conv_w"][layer], gl["ffn_conv_b"][layer] = dg.reshape(-1), dcw, dcb.reshape(-1)
        if layer < N_A:
            hin, hn, ypre, yg, sb, z, a1, a2, wb16, wc16, dvec, prep_vjp = mix_saved
            dz = _glu_bwd(z, dh1, name="glu_bwd" + t).reshape(N_DEV, L, SHARD_COLS_GLU)
            S.grad["w_glu", layer] = S.run("glu_dw" + t, _mm, yg, dz, ak="km", bk="bkn", ok="bmn", out_dtype=BF16,
                                           tk=4096)
            dyg = S.run("glu_dy" + t, _mm, dz, W["w_glu", layer], ak="bmk", bk="bnk", out_dtype=BF16, tm=2048, kg=8)
            if layer == 0:
                S.bcast["repl_early",] = _pack_rows(_repl_early({**g, **gl, **ga}), REPL_EARLY_ROWS)
            du, dwb, dwc, da1, da2, dd = S.run("s5_bwd" + t, _s5_bwd, hn, dyg, ypre, sb, wb16, wc16, a1, a2, dvec)
            for nme, val in zip(S5_NAMES, prep_vjp((da1, da2, dwb, dwc))):
                ga[nme][layer] = val
            ga["ssm_d"][layer] = dd.reshape(-1)
            dh, dg = _rms_bwd(hin, vec(W["g_mix"][layer]), du, dh1, name="mix_norm_bwd" + t)
        else:
            j = layer - N_A
            hin, hn, q, o32, o, lse = mix_saved
            _, _, _, kvm, ck = kvs
            S.grad["w_o", j] = S.run("o_dw" + t, _mm, o, dh1, ak="km", out_dtype=BF16
                                     ).reshape(N_DEV, SHARD_ROWS_QO, D_MODEL)
            do = S.run("o_dx" + t, _mm, dh1, W["w_o", j], bk="nk", out_dtype=BF16)
            dq, dk_l, dv_l, dck_l = S.run("attn_bwd" + t, _attn_bwd, q, kvm, ck, o32, do, lse)
            dk = dk_l if dk is None else dk + dk_l
            dv = dv_l if dv is None else dv + dv_l
            dck = dck_l if dck is None else dck + dck_l
            S.grad["w_q", j] = S.run("q_dw" + t, _mm, hn, dq, ak="km", out_dtype=BF16
                                     ).reshape(N_DEV, SHARD_ROWS_QO, D_MODEL)
            dhn = S.run("q_dx" + t, _mm, dq, W["w_q", j], bk="nk")
            dh, dg = _rms_bwd(hin, vec(W["g_mix"][layer]), dhn, dh1, name="mix_norm_bwd" + t)
            if layer == N_A:
                hkv_in, hkv, fl, _, _ = kvs
                dcum = jnp.pad(dck.reshape(N_HEADS, L).T, ((0, 0), (0, LANES - N_HEADS)))
                dfl, dbf = _fgate_bwd(fl, W["b_f_pad",], dcum, name="fgate_bwd")
                dkv = jnp.concatenate([dk, dv], axis=1).astype(BF16)
                dfl16 = dfl.astype(BF16)
                dw_kv = S.run("kv_dw", _mm, hkv, dkv, ak="km")
                dw_f = S.run("f_dw", _mm, hkv, dfl16, ak="km")
                S.grad["w_kvf",] = _kvf_blocks(jnp.concatenate([dw_kv, dw_f[:, :N_HEADS]], axis=1)).astype(BF16)
                dhkv = S.run("kv_dx", _mm, dkv, W["w_kv",], bk="nk")
                dhkv = S.run("f_dx", _mm, dfl16, W["w_f",], bk="nk", add=dhkv)
                g["b_f"] = dbf[0, :N_HEADS]
                dh, dgkv = _rms_bwd(hkv_in, vec(W["g_kv"]), dhkv, dh, name="kv_norm_bwd")
                g["g_kv"] = dgkv.reshape(-1)
        gl["g_mix"][layer] = dg.reshape(-1)

    for d in (gl, ga):
        for k, v in d.items():
            g[k] = jnp.stack(v)
    return loss, dh, g


def _adamw_layers(slots, w, m, v, *, name):
    shape = w.shape
    nl = len(slots)
    w, m, v = (a.reshape((nl,) + a.shape[-2:]) for a in (w, m, v))
    _, R, C = w.shape
    tr = _tile(R, 256, 16)
    c1 = 1.0 / (1.0 - ADAM_B1 ** ADAM_STEP)
    c2 = 1.0 / (1.0 - ADAM_B2 ** ADAM_STEP)

    def body(*refs):
        s_refs, (w_ref, m_ref, v_ref), (g_ref, d_ref, nm_ref, nv_ref) = refs[:nl], refs[nl:nl + 3], refs[nl + 3:]
        for layer in range(nl):
            @pl.when(pl.program_id(0) == layer)
            def _(s_ref=s_refs[layer]):
                g = s_ref[0].astype(F32)
                for d in range(1, N_DEV):
                    g = g + s_ref[d].astype(F32)
                m2 = ADAM_B1 * m_ref[...] + (1.0 - ADAM_B1) * g
                v2 = ADAM_B2 * v_ref[...] + (1.0 - ADAM_B2) * (g * g)
                g_ref[...] = g
                nm_ref[...] = m2
                nv_ref[...] = v2
                d_ref[...] = -ADAM_LR * ((m2 * c1) / (jnp.sqrt(v2 * c2) + ADAM_EPS) + ADAM_WD * w_ref[...])

    def slab_spec(layer):
        return pl.BlockSpec((N_DEV, tr, C), lambda l, i: (0, jnp.where(l == layer, i, 0), 0))

    row = pl.BlockSpec((None, tr, C), lambda l, i: (l, i, 0))
    out = jax.ShapeDtypeStruct((nl, R, C), F32)
    outs = pl.pallas_call(
        body, grid=(nl, R // tr), in_specs=[slab_spec(layer) for layer in range(nl)] + [row, row, row],
        out_specs=[row, row, row, row], out_shape=[out, out, out, out],
        compiler_params=_params("arbitrary", "arbitrary"), name=name)(*slots, w, m, v)
    return [o.reshape(shape) for o in outs]


_SMALL_ROWS = 72
_ORDER = ("g_mix", "g_ffn", "lam_re", "lam_im", "log_dt", "ssm_b_re", "ssm_b_im", "ssm_c_re", "ssm_c_im", "ssm_d",
          "w_glu", "g_kv", "w_kvf", "b_f", "w_q", "w_o", "w_ffn_in", "ffn_conv_w", "ffn_conv_b", "w_ffn_out", "g_final")


def _pack_small(ssm_d, conv_w):
    flat = jnp.concatenate([ssm_d.reshape(-1), conv_w.reshape(-1)])
    return jnp.pad(flat, (0, _SMALL_ROWS * LANES - flat.shape[0])).reshape(_SMALL_ROWS, LANES)


def _unpack_small(flat):
    flat = flat.reshape(-1)
    return flat[:256].reshape(2, 128), flat[256:256 + 8448].reshape(4, 3, 704)


_FWD_PLAN = {
    "start": [("small",)],
    "s5_fwd0": [("w_glu", 0), ("w_ffn_in", 0, 0)],
    "glu_mm0": [("w_ffn_in", 0, 1)],
    "ffn_in0": [("w_ffn_out", 0)],
    "ffn_act0": [("w_glu", 1)],
    "ffn_out0": [("w_ffn_in", 1, 0)],
    "s5_fwd1": [("w_ffn_in", 1, 1)],
    "ffn_in1": [("w_ffn_out", 1)],
    "ffn_act1": [("w_kvf",)],
    "ffn_out1": [("w_q", 0), ("w_o", 0)],
    "attn_fwd2": [("w_ffn_in", 2), ("w_ffn_out", 2), ("w_q", 1), ("w_o", 1)],
    "attn_fwd3": [("w_ffn_in", 3), ("w_ffn_out", 3)],
}
_BWD_PLAN = {
    "ffn_dhn3": [("w_ffn_out", 3)],
    "attn_bwd3": [("w_ffn_in", 3), ("w_o", 1)],
    "ffn_conv_bwd2": [("w_q", 1)],
    "ffn_dhn2": [("w_ffn_out", 2)],
    "attn_bwd2": [("w_ffn_in", 2), ("w_o", 0)],
    "ffn_conv_bwd1": [("w_q", 0), ("w_kvf",)],
    "ffn_dhn1": [("w_ffn_out", 1)],
    "s5_bwd1": [("w_ffn_in", 1), ("w_glu", 1)],
    "ffn_dhn0": [("w_ffn_out", 0)],
    "glu_dy0": [("w_glu", 0)],
    "s5_bwd0": [("w_ffn_in", 0), ("repl_early",)],
    "end": [("small",), ("repl_late",)],
}
_PLAN = {h: [("w", k) for k in ks] for h, ks in _FWD_PLAN.items()}
_PLAN.update({h: [("b" if k[0].startswith("repl") else "g", k) for k in ks] for h, ks in _BWD_PLAN.items()})


def kernel(x, g_mix, g_ffn, lam_re, lam_im, log_dt, ssm_b_re, ssm_b_im, ssm_c_re, ssm_c_im, ssm_d, w_glu, g_kv, w_kvf, b_f, w_q, w_o, w_ffn_in, ffn_conv_w, ffn_conv_b, w_ffn_out, g_final, loss_target, m_g_mix, m_g_ffn, m_lam_re, m_lam_im, m_log_dt, m_ssm_b_re, m_ssm_b_im, m_ssm_c_re, m_ssm_c_im, m_ssm_d, m_w_glu, m_g_kv, m_w_kvf, m_b_f, m_w_q, m_w_o, m_w_ffn_in, m_ffn_conv_w, m_ffn_conv_b, m_w_ffn_out, m_g_final, v_g_mix, v_g_ffn, v_lam_re, v_lam_im, v_log_dt, v_ssm_b_re, v_ssm_b_im, v_ssm_c_re, v_ssm_c_im, v_ssm_d, v_w_glu, v_g_kv, v_w_kvf, v_b_f, v_w_q, v_w_o, v_w_ffn_in, v_ffn_conv_w, v_ffn_conv_b, v_w_ffn_out, v_g_final):
    wts = dict(g_mix=g_mix, g_ffn=g_ffn, lam_re=lam_re, lam_im=lam_im, log_dt=log_dt, ssm_b_re=ssm_b_re,
               ssm_b_im=ssm_b_im, ssm_c_re=ssm_c_re, ssm_c_im=ssm_c_im, ssm_d=ssm_d, w_glu=w_glu, g_kv=g_kv,
               w_kvf=w_kvf, b_f=b_f, w_q=w_q, w_o=w_o, w_ffn_in=w_ffn_in, ffn_conv_w=ffn_conv_w,
               ffn_conv_b=ffn_conv_b, w_ffn_out=w_ffn_out, g_final=g_final)
    mom = dict(g_mix=m_g_mix, g_ffn=m_g_ffn, lam_re=m_lam_re, lam_im=m_lam_im, log_dt=m_log_dt, ssm_b_re=m_ssm_b_re,
               ssm_b_im=m_ssm_b_im, ssm_c_re=m_ssm_c_re, ssm_c_im=m_ssm_c_im, ssm_d=m_ssm_d, w_glu=m_w_glu,
               g_kv=m_g_kv, w_kvf=m_w_kvf, b_f=m_b_f, w_q=m_w_q, w_o=m_w_o, w_ffn_in=m_w_ffn_in,
               ffn_conv_w=m_ffn_conv_w, ffn_conv_b=m_ffn_conv_b, w_ffn_out=m_w_ffn_out, g_final=m_g_final)
    var = dict(g_mix=v_g_mix, g_ffn=v_g_ffn, lam_re=v_lam_re, lam_im=v_lam_im, log_dt=v_log_dt, ssm_b_re=v_ssm_b_re,
               ssm_b_im=v_ssm_b_im, ssm_c_re=v_ssm_c_re, ssm_c_im=v_ssm_c_im, ssm_d=v_ssm_d, w_glu=v_w_glu,
               g_kv=v_g_kv, w_kvf=v_w_kvf, b_f=v_b_f, w_q=v_w_q, w_o=v_w_o, w_ffn_in=v_w_ffn_in,
               ffn_conv_w=v_ffn_conv_w, ffn_conv_b=v_ffn_conv_b, w_ffn_out=v_w_ffn_out, g_final=v_g_final)
    kinds = ("grad", "delta", "m", "v")

    ready = {n: wts[n] for n in ("g_mix", "g_ffn", "g_kv", "g_final") + S5_NAMES}
    ready["b_f_pad",] = jnp.pad(b_f, (0, LANES - N_HEADS)).reshape(1, LANES)
    send = {("small",): _pack_small(ssm_d, ffn_conv_w), ("w_kvf",): w_kvf.astype(BF16)}
    for layer in range(DEPTH):
        ready["conv_b", layer] = ffn_conv_b[layer].reshape(2, 4, 1, SHARD_COLS_FFN)
        w_in_t = jnp.swapaxes(w_ffn_in[layer], 0, 1).astype(BF16)
        if ("w_ffn_in", layer) in [k for ks in _FWD_PLAN.values() for k in ks]:
            send["w_ffn_in", layer] = w_in_t
        else:
            half = SHARD_COLS_FFN // 2
            send["w_ffn_in", layer, 0] = w_in_t[:half]
            send["w_ffn_in", layer, 1] = w_in_t[half:]
        send["w_ffn_out", layer] = w_ffn_out[layer].astype(BF16)
    for layer in range(N_A):
        send["w_glu", layer] = w_glu[layer].astype(BF16)
        send["w_q", layer] = w_q[layer].astype(BF16)
        send["w_o", layer] = w_o[layer].astype(BF16)

    S = _Step(ready, send, _PLAN)
    S.land("start", _exchange(S.xfers("start"), name="start"))
    loss, dx, g = _step(x[0], loss_target[0], S)
    loss = lax.psum(loss[0, 0], MESH_AXES)

    g_d = g["ssm_d"].reshape(N_A, N_DEV, LANES).transpose(1, 0, 2).reshape(N_DEV, N_A * LANES)
    g_cw = jnp.stack([g["conv_w"][layer].reshape(N_DEV, 3, SHARD_COLS_FFN) for layer in range(DEPTH)], axis=1)
    g_small = jnp.concatenate([g_d, g_cw.reshape(N_DEV, -1)], axis=1)
    g_small = jnp.pad(g_small, ((0, 0), (0, _SMALL_ROWS * LANES - g_small.shape[1])))
    S.grad["small",] = g_small.reshape(N_DEV, _SMALL_ROWS, LANES)
    S.bcast["repl_late",] = _pack_rows(_repl_late(g), REPL_LATE_ROWS).astype(BF16)
    S.land("end", _exchange(S.xfers("end"), name="end"))

    res = {}
    for name, nl in (("w_glu", N_A), ("w_q", DEPTH - N_A), ("w_o", DEPTH - N_A), ("w_ffn_in", DEPTH),
                     ("w_ffn_out", DEPTH)):
        view = (lambda a: jnp.swapaxes(a, 1, 2)) if name == "w_ffn_in" else (lambda a: a)
        outs = _adamw_layers([S.slots[name, layer] for layer in range(nl)], view(wts[name]), view(mom[name]),
                             view(var[name]), name="adamw_" + name)
        res.update({(kind, name): view(a) for kind, a in zip(kinds, outs)})
    outs = _adamw_layers([S.slots["w_kvf",]], w_kvf, m_w_kvf, v_w_kvf, name="adamw_w_kvf")
    res.update({(kind, "w_kvf"): a for kind, a in zip(kinds, outs)})
    outs = _adamw_layers([S.slots["small",]], _pack_small(ssm_d, ffn_conv_w), _pack_small(m_ssm_d, m_ffn_conv_w),
                         _pack_small(v_ssm_d, v_ffn_conv_w), name="adamw_small")
    for kind, flat in zip(kinds, outs):
        res[kind, "ssm_d"], res[kind, "ffn_conv_w"] = _unpack_small(flat)

    pieces = {}
    for key, rows, sel in ((("repl_early",), REPL_EARLY_ROWS, _repl_early), (("repl_late",), REPL_LATE_ROWS, _repl_late)):
        outs = _adamw_layers([S.slots[key]], *[_pack_rows(sel(d), rows) for d in (wts, mom, var)],
                             name="adamw_" + key[0])
        for kind, flat in zip(kinds, outs):
            pieces[kind, key[0]] = _unpack_rows(flat, sel(wts))
    for kind in kinds:
        early, late = iter(pieces[kind, "repl_early"]), iter(pieces[kind, "repl_late"])
        take = lambda it, n: [next(it) for _ in range(n)]
        res[kind, "g_mix"] = jnp.stack(take(late, 1) + take(early, DEPTH - 1))
        res[kind, "g_ffn"] = jnp.stack(take(early, DEPTH))
        for n in S5_NAMES:
            res[kind, n] = jnp.stack([next(late), next(early)]).reshape(wts[n].shape)
        res[kind, "g_kv"], res[kind, "b_f"] = next(early), next(early)
        res[kind, "ffn_conv_b"] = jnp.stack(take(early, DEPTH))
        res[kind, "g_final"] = next(early)

    return (loss, dx[None], *[res[kind, n] for kind in kinds for n in _ORDER])
```

```python
import functools
import math

import jax
import jax.numpy as jnp
from jax import lax
from jax.experimental import pallas as pl
from jax.experimental.pallas import tpu as pltpu

F32 = jnp.float32
BF16 = jnp.bfloat16

D_MODEL = 1024
DEPTH = 4
N_A = 2
N_GROUPS = 64
SSM_GROUP = 16
SSM_STATE = 64
N_HEADS = 16
HEAD_DIM = 64
ATTN_SCALE = HEAD_DIM ** -0.5
D_FF = 2816
EPS = 1e-6
N_DEV = 8
LANES = 128
SUBLANES = 8

ADAM_LR = 0.001
ADAM_B1 = 0.9
ADAM_B2 = 0.999
ADAM_EPS = 1e-08
ADAM_WD = 0.01
ADAM_STEP = 10

ROW_TILE = 512
S5_CHUNK = 256
ATTN_TILE = 512
CUM_TILE = 256
NEG = -1e30

MESH_AXES = ("x", "y", "c")


def _tile(n, target, align=LANES):
    t = (min(target, n) // align) * align
    while t >= align:
        if n % t == 0:
            return t
        t -= align
    return n


def _params(*sem):
    return pltpu.CompilerParams(dimension_semantics=sem, vmem_limit_bytes=56 * 1024 * 1024)


_ANY = pl.BlockSpec(memory_space=pl.ANY)
_XFER_SEMS = (pltpu.SemaphoreType.DMA((N_DEV - 1,)), pltpu.SemaphoreType.DMA((N_DEV - 1,)), pltpu.SemaphoreType.DMA)


def _xfer_copies(x_ref, o_ref, send_sems, recv_sems, local_sem, scatter):
    xi, yi, ci = lax.axis_index("x"), lax.axis_index("y"), lax.axis_index("c")
    me = 4 * xi + 2 * yi + ci

    def src(p):
        return x_ref.at[p] if scatter else x_ref

    own = pltpu.make_async_copy(src(me), o_ref.at[me], local_sem)
    sends, recvs = [], []
    for k in range(1, N_DEV):
        px, py, pc = xi ^ (k >> 2), yi ^ ((k >> 1) & 1), ci ^ (k & 1)
        p = 4 * px + 2 * py + pc
        sends.append(pltpu.make_async_remote_copy(
            src_ref=src(p), dst_ref=o_ref.at[me], send_sem=send_sems.at[k - 1], recv_sem=recv_sems.at[k - 1],
            device_id=(px, py, pc), device_id_type=pl.DeviceIdType.MESH))
        recvs.append(pltpu.make_async_remote_copy(
            src_ref=src(p), dst_ref=o_ref.at[p], send_sem=send_sems.at[k - 1], recv_sem=recv_sems.at[k - 1],
            device_id=(px, py, pc), device_id_type=pl.DeviceIdType.MESH))
    return own, sends, recvs


def _xfer_start(*refs, scatter):
    own, sends, _ = _xfer_copies(*refs, scatter)
    own.start()
    for cp in sends:
        cp.start()


def _xfer_wait(*refs, scatter):
    own, sends, recvs = _xfer_copies(*refs, scatter)
    for cp in recvs:
        cp.wait_recv()
    for cp in sends:
        cp.wait_send()
    own.wait()


def _xfer_out(x, scatter):
    return jax.ShapeDtypeStruct((N_DEV,) + (x.shape[1:] if scatter else x.shape), x.dtype)


def _pcall(body, args, *, grid, in_specs, out_specs, out_shape, scratch_shapes=(), sem, name, xfers=(), prefetch=()):
    out_specs, out_shape = list(out_specs), list(out_shape)
    n_pre, n_in, n_out, n_x, n_scr = len(prefetch), len(in_specs), len(out_specs), len(xfers), len(scratch_shapes)
    flags = [s for _, s in xfers]

    def wrapped(*refs):
        pre, refs = refs[:n_pre], refs[n_pre:]
        ins, xin = refs[:n_in], refs[n_in:n_in + n_x]
        outs = refs[n_in + n_x:n_in + n_x + n_out]
        xout = refs[n_in + n_x + n_out:n_in + 2 * n_x + n_out]
        scr = refs[n_in + 2 * n_x + n_out:]
        own, sems = scr[:n_scr], scr[n_scr:]
        ids = [pl.program_id(d) for d in range(len(grid))]
        first = functools.reduce(jnp.logical_and, [i == 0 for i in ids])
        last = functools.reduce(jnp.logical_and, [i == g - 1 for i, g in zip(ids, grid)])

        @pl.when(first)
        def _():
            for t in range(n_x):
                _xfer_start(xin[t], xout[t], *sems[3 * t:3 * t + 3], scatter=flags[t])

        body(*pre, *ins, *outs, *own)

        @pl.when(last)
        def _():
            for t in range(n_x):
                _xfer_wait(xin[t], xout[t], *sems[3 * t:3 * t + 3], scatter=flags[t])

    grid_spec = pltpu.PrefetchScalarGridSpec(
        num_scalar_prefetch=n_pre, grid=grid, in_specs=list(in_specs) + [_ANY] * n_x,
        out_specs=out_specs + [_ANY] * n_x, scratch_shapes=list(scratch_shapes) + list(_XFER_SEMS) * n_x)
    return pl.pallas_call(
        wrapped if xfers else body, grid_spec=grid_spec, out_shape=out_shape + [_xfer_out(x, s) for x, s in xfers],
        compiler_params=_params(*(["arbitrary"] * len(grid) if xfers else sem)), name=name,
    )(*prefetch, *args, *[x for x, _ in xfers])


def _exchange(xfers, *, name):
    def body():
        pass

    return _pcall(body, (), grid=(1,), in_specs=[], out_specs=[], out_shape=[], sem=("arbitrary",), name=name,
                  xfers=xfers)


def _mm(a, b, *, ak="mk", bk="kn", ok="mn", add=None, out_dtype=F32, tm=1024, tn=1024, tk=1024, kg=1, name,
        xfers=()):
    sa, sb = a.shape, b.shape
    fm = fn = fk = None
    if ak == "mk":
        M, K, a_c = sa[0], sa[1], 1
    elif ak == "km":
        K, M, a_c = sa[0], sa[1], 0
    elif ak == "bmk":
        M, K, a_c, fk = sa[1], sa[0] * sa[2], 1, sa[2]
    else:
        K, M, a_c, fm = sa[1], sa[0] * sa[2], 0, sa[2]
    if bk == "kn":
        N, b_c = sb[1], 0
    elif bk == "nk":
        N, b_c = sb[0], 1
    elif bk == "bkn":
        N, b_c, fn = sb[0] * sb[2], 0, sb[2]
    elif bk == "bnk":
        N, b_c, fk = sb[1], 1, sb[2]
    elif bk == "kbn":
        N, b_c, fk = sb[2], 0, sb[1]
    else:
        N, b_c, fn = sb[0] * sb[1], 1, sb[1]
    tm, tn, tk = fm or _tile(M, tm), fn or _tile(N, tn), fk or _tile(K, tk)
    kblk = None if kg == 1 else kg
    nm, nn, nk = M // tm, N // tn, K // (tk * kg)

    a_spec = {"mk": pl.BlockSpec((tm, tk), lambda i, j, k: (i, k)),
              "km": pl.BlockSpec((tk, tm), lambda i, j, k: (k, i)),
              "bmk": pl.BlockSpec((kblk, tm, tk), lambda i, j, k: (k, i, 0)),
              "bkm": pl.BlockSpec((None, tk, tm), lambda i, j, k: (i, k, 0))}[ak]
    b_spec = {"kn": pl.BlockSpec((tk, tn), lambda i, j, k: (k, j)),
              "nk": pl.BlockSpec((tn, tk), lambda i, j, k: (j, k)),
              "bkn": pl.BlockSpec((None, tk, tn), lambda i, j, k: (j, k, 0)),
              "bnk": pl.BlockSpec((kblk, tn, tk), lambda i, j, k: (k, j, 0)),
              "kbn": pl.BlockSpec((kblk, tk, tn), lambda i, j, k: (k, 0, j)),
              "nbk": pl.BlockSpec((None, tn, tk), lambda i, j, k: (j, 0, k))}[bk]
    if ok == "mn":
        o_spec = pl.BlockSpec((tm, tn), lambda i, j, k: (i, j))
        out_shape = jax.ShapeDtypeStruct((M, N), out_dtype)
    elif ok == "bmn":
        o_spec = pl.BlockSpec((None, tm, tn), lambda i, j, k: (j, i, 0))
        out_shape = jax.ShapeDtypeStruct((nn, M, tn), out_dtype)
    else:
        o_spec = pl.BlockSpec((None, tm, tn), lambda i, j, k: (i, 0, j))
        out_shape = jax.ShapeDtypeStruct((nm, tm, N), out_dtype)
    dims = (((a_c,), (b_c,)), ((), ()))
    has_add = add is not None

    def body(*refs):
        a_ref, b_ref = refs[0], refs[1]
        add_ref = refs[2] if has_add else None
        o_ref = refs[3] if has_add else refs[2]
        if kg == 1:
            part = lax.dot_general(a_ref[...].astype(BF16), b_ref[...].astype(BF16), dims, preferred_element_type=F32)
        else:
            part = sum(lax.dot_general(a_ref[g].astype(BF16), b_ref[g].astype(BF16), dims,
                                       preferred_element_type=F32) for g in range(kg))

        def finish(r):
            if has_add:
                r = r + add_ref[...]
            o_ref[...] = r.astype(out_dtype)

        if nk == 1:
            finish(part)
            return
        acc = refs[-1]
        k = pl.program_id(2)

        @pl.when(k == 0)
        def _():
            acc[...] = part

        @pl.when(k > 0)
        def _():
            acc[...] += part

        @pl.when(k == nk - 1)
        def _():
            finish(acc[...])

    in_specs = [a_spec, b_spec]
    args = [a, b]
    if has_add:
        in_specs.append(pl.BlockSpec((tm, tn), lambda i, j, k: (i, j)))
        args.append(add)
    res = _pcall(body, args, grid=(nm, nn, nk), in_specs=in_specs, out_specs=[o_spec], out_shape=[out_shape],
                 scratch_shapes=[pltpu.VMEM((tm, tn), F32)] if nk > 1 else [],
                 sem=("parallel", "parallel", "arbitrary"), name=name, xfers=xfers)
    return res if xfers else res[0]


def _rms_fwd(h, g, *, name):
    L, D = h.shape
    tr = _tile(L, ROW_TILE, SUBLANES)

    def body(h_ref, g_ref, o_ref):
        x = h_ref[...]
        r = lax.rsqrt(jnp.mean(x * x, axis=1, keepdims=True) + EPS)
        o_ref[...] = (x * r * g_ref[...]).astype(BF16)

    return pl.pallas_call(
        body, grid=(L // tr,),
        in_specs=[pl.BlockSpec((tr, D), lambda i: (i, 0)), pl.BlockSpec((1, D), lambda i: (0, 0))],
        out_specs=pl.BlockSpec((tr, D), lambda i: (i, 0)), out_shape=jax.ShapeDtypeStruct((L, D), BF16),
        compiler_params=_params("parallel"), name=name)(h, g)


def _rms_bwd(h, g, dy, dres, *, name):
    L, D = h.shape
    tr = _tile(L, ROW_TILE, SUBLANES)

    def body(h_ref, g_ref, dy_ref, dres_ref, dh_ref, dg_ref):
        @pl.when(pl.program_id(0) == 0)
        def _():
            dg_ref[...] = jnp.zeros_like(dg_ref)

        x = h_ref[...]
        r = lax.rsqrt(jnp.mean(x * x, axis=1, keepdims=True) + EPS)
        xn = x * r
        dy = dy_ref[...].astype(F32)
        gdy = dy * g_ref[...]
        dx = r * (gdy - xn * jnp.mean(gdy * xn, axis=1, keepdims=True))
        dh_ref[...] = dres_ref[...] + dx
        dg_ref[...] += jnp.sum(dy * xn, axis=0, keepdims=True)

    row = pl.BlockSpec((tr, D), lambda i: (i, 0))
    vec = pl.BlockSpec((1, D), lambda i: (0, 0))
    return pl.pallas_call(
        body, grid=(L // tr,), in_specs=[row, vec, row, row], out_specs=[row, vec],
        out_shape=[jax.ShapeDtypeStruct((L, D), F32), jax.ShapeDtypeStruct((1, D), F32)],
        compiler_params=_params("arbitrary"), name=name)(h, g, dy, dres)


def _glu_res_rms(z, h, g, *, name):
    L, D = h.shape
    nb, cb = z.shape[1], z.shape[3]
    tr = _tile(L, ROW_TILE, SUBLANES)

    def body(z_ref, h_ref, g_ref, h1_ref, hn_ref):
        za = jnp.concatenate([z_ref[0, d] for d in range(nb)], axis=1)
        zg = jnp.concatenate([z_ref[1, d] for d in range(nb)], axis=1)
        x = h_ref[...] + za * jax.nn.sigmoid(zg)
        h1_ref[...] = x
        r = lax.rsqrt(jnp.mean(x * x, axis=1, keepdims=True) + EPS)
        hn_ref[...] = (x * r * g_ref[...]).astype(BF16)

    row = pl.BlockSpec((tr, D), lambda i: (i, 0))
    return pl.pallas_call(
        body, grid=(L // tr,),
        in_specs=[pl.BlockSpec((2, nb, tr, cb), lambda i: (0, 0, i, 0)), row, pl.BlockSpec((1, D), lambda i: (0, 0))],
        out_specs=[row, row],
        out_shape=[jax.ShapeDtypeStruct((L, D), F32), jax.ShapeDtypeStruct((L, D), BF16)],
        compiler_params=_params("parallel"), name=name)(z, h, g)


def _glu_bwd(z, dout, *, name):
    L, D = dout.shape
    nb, cb = z.shape[1], z.shape[3]
    tr = _tile(L, ROW_TILE, SUBLANES)

    def body(z_ref, d_ref, o_ref):
        for d in range(nb):
            dd = d_ref[:, d * cb:(d + 1) * cb]
            sg = jax.nn.sigmoid(z_ref[1, d])
            o_ref[0, d] = (dd * sg).astype(BF16)
            o_ref[1, d] = (dd * z_ref[0, d] * sg * (1.0 - sg)).astype(BF16)

    zs = pl.BlockSpec((2, nb, tr, cb), lambda i: (0, 0, i, 0))
    return pl.pallas_call(
        body, grid=(L // tr,), in_specs=[zs, pl.BlockSpec((tr, D), lambda i: (i, 0))], out_specs=zs,
        out_shape=jax.ShapeDtypeStruct(z.shape, BF16),
        compiler_params=_params("parallel"), name=name)(z, dout)


def _loss_head(h, g, target, *, name):
    L, D = h.shape
    tr = _tile(L, ROW_TILE, SUBLANES)

    def body(h_ref, g_ref, t_ref, loss_ref, dh_ref, dg_ref):
        @pl.when(pl.program_id(0) == 0)
        def _():
            dg_ref[...] = jnp.zeros_like(dg_ref)
            loss_ref[...] = jnp.zeros_like(loss_ref)

        x = h_ref[...]
        gg = g_ref[...]
        r = lax.rsqrt(jnp.mean(x * x, axis=1, keepdims=True) + EPS)
        xn = x * r
        err = xn * gg - t_ref[...]
        loss_ref[...] += 0.5 * jnp.sum(jnp.mean(err * err, axis=1, keepdims=True), axis=0, keepdims=True)
        dy = err * (1.0 / D)
        gdy = dy * gg
        dh_ref[...] = r * (gdy - xn * jnp.mean(gdy * xn, axis=1, keepdims=True))
        dg_ref[...] += jnp.sum(dy * xn, axis=0, keepdims=True)

    row = pl.BlockSpec((tr, D), lambda i: (i, 0))
    vec = pl.BlockSpec((1, D), lambda i: (0, 0))
    return pl.pallas_call(
        body, grid=(L // tr,), in_specs=[row, vec, row],
        out_specs=[pl.BlockSpec((1, 1), lambda i: (0, 0)), row, vec],
        out_shape=[jax.ShapeDtypeStruct((1, 1), F32), jax.ShapeDtypeStruct((L, D), F32),
                   jax.ShapeDtypeStruct((1, D), F32)],
        compiler_params=_params("arbitrary"), name=name)(h, g, target)


CONV_ROW_TILE = 256


def _sigmoid(x):
    return pl.reciprocal(1.0 + jnp.exp(-x), approx=True)


def _conv_specs(L, tr, tc):
    nrb = tr // SUBLANES
    before = lambda i: jnp.maximum(i * nrb - 1, 0)
    after = lambda i: jnp.minimum((i + 1) * nrb, L // SUBLANES - 1)
    main = pl.BlockSpec((2, None, tr, tc), lambda j, i: (0, j, i, 0))
    prev = pl.BlockSpec((2, None, SUBLANES, tc), lambda j, i: (0, j, before(i), 0))
    nxt = pl.BlockSpec((2, None, SUBLANES, tc), lambda j, i: (0, j, after(i), 0))
    cw = pl.BlockSpec((2, None, 3, tc), lambda j, i: (0, j, 0, 0))
    cb = pl.BlockSpec((2, None, 1, tc), lambda j, i: (0, j, 0, 0))
    half = pl.BlockSpec((None, tr, tc), lambda j, i: (j, i, 0))
    half_nxt = pl.BlockSpec((None, SUBLANES, tc), lambda j, i: (j, after(i), 0))
    return main, prev, nxt, cw, cb, half, half_nxt


def _conv_rows(xe, w, b):
    x1 = pltpu.roll(xe, 1, 0)
    x2 = pltpu.roll(xe, 2, 0)
    return b + x2 * w[0:1] + x1 * w[1:2] + xe * w[2:3], x1, x2


def _shift_down(x, halo, k, row):
    y = pltpu.roll(x, k, 0)
    for r in range(k):
        y = jnp.where(row == r, halo[SUBLANES - k + r:SUBLANES - k + r + 1, :], y)
    return y


def _conv_act(u0, cw, cb, *, name, xfers=()):
    _, nb, L, tc = u0.shape
    tr = _tile(L, ROW_TILE, SUBLANES)
    main, prev, _, cws, cbs, half, _ = _conv_specs(L, tr, tc)

    def body(u_ref, p_ref, w_ref, b_ref, a_ref):
        first = pl.program_id(1) == 0
        row = lax.broadcasted_iota(jnp.int32, (tr, tc), 0)
        y = []
        for s in range(2):
            x, w = u_ref[s], w_ref[s]
            halo = jnp.where(first, 0.0, p_ref[s])
            x1 = _shift_down(x, halo, 1, row)
            x2 = _shift_down(x, halo, 2, row)
            y.append(b_ref[s] + x2 * w[0:1] + x1 * w[1:2] + x * w[2:3])
        a_ref[...] = (y[0] * _sigmoid(y[0]) * y[1]).astype(BF16)

    return _pcall(body, (u0, u0, cw, cb), grid=(nb, L // tr), in_specs=[main, prev, cws, cbs], out_specs=[half],
                  out_shape=[jax.ShapeDtypeStruct((nb, L, tc), BF16)], sem=("parallel", "parallel"), name=name,
                  xfers=xfers)


def _conv_ffn_bwd(u0, cw, cb, da, *, name, xfers=()):
    _, nb, L, tc = u0.shape
    tr = _tile(L, CONV_ROW_TILE, SUBLANES)
    main, prev, nxt, cws, cbs, half, half_nxt = _conv_specs(L, tr, tc)
    nr = L // tr
    H = SUBLANES

    def body(u_ref, p_ref, n_ref, w_ref, b_ref, da_ref, dan_ref, a_ref, du0_ref, dcw_ref, dcb_ref):
        i = pl.program_id(1)

        @pl.when(i == 0)
        def _():
            dcw_ref[...] = jnp.zeros_like(dcw_ref)
            dcb_ref[...] = jnp.zeros_like(dcb_ref)

        y, x1, x2 = [], [], []
        for s in range(2):
            xe = jnp.concatenate([jnp.where(i == 0, 0.0, p_ref[s]), u_ref[s], n_ref[s]], axis=0)
            ys, x1s, x2s = _conv_rows(xe, w_ref[s], b_ref[s])
            y.append(ys[H:])
            x1.append(x1s[H:H + tr])
            x2.append(x2s[H:H + tr])
        gate, up = y
        da = jnp.concatenate([da_ref[...], dan_ref[...]], axis=0)
        row = lax.broadcasted_iota(jnp.int32, (tr + H, tc), 0)
        da = jnp.where(jnp.logical_and(i == nr - 1, row >= tr), 0.0, da)
        sg = _sigmoid(gate)
        silu = gate * sg
        a_ref[...] = (silu * up)[:tr].astype(BF16)
        d = (da * up * (sg * (1.0 + gate * (1.0 - sg))), da * silu)
        for s in range(2):
            w = w_ref[s]
            d0 = d[s][:tr]
            d1 = pltpu.roll(d[s], tr + H - 1, 0)[:tr]
            d2 = pltpu.roll(d[s], tr + H - 2, 0)[:tr]
            du0_ref[s] = (d0 * w[2:3] + d1 * w[1:2] + d2 * w[0:1]).astype(BF16)
            dcw_ref[s, 0:1, :] += jnp.sum(d0 * x2[s], axis=0, keepdims=True)
            dcw_ref[s, 1:2, :] += jnp.sum(d0 * x1[s], axis=0, keepdims=True)
            dcw_ref[s, 2:3, :] += jnp.sum(d0 * u_ref[s], axis=0, keepdims=True)
            dcb_ref[s] += jnp.sum(d0, axis=0, keepdims=True)

    return _pcall(body, (u0, u0, u0, cw, cb, da, da), grid=(nb, nr),
                  in_specs=[main, prev, nxt, cws, cbs, half, half_nxt], out_specs=[half, main, cws, cbs],
                  out_shape=[jax.ShapeDtypeStruct((nb, L, tc), BF16), jax.ShapeDtypeStruct((2, nb, L, tc), BF16),
                             jax.ShapeDtypeStruct((2, nb, 3, tc), F32), jax.ShapeDtypeStruct((2, nb, 1, tc), F32)],
                  sem=("parallel", "arbitrary"), name=name, xfers=xfers)


N_TILES = 64
HALF = N_TILES // 2


def _swap(s):
    return jnp.concatenate([s[HALF:], s[:HALF]], axis=0)


def _chan_block(j):
    return ((j % HALF) // 4) * LANES


def _pairs_of(jb):
    return [2 * jb, 2 * jb + 1, HALF // 2 + 2 * jb, HALF // 2 + 2 * jb + 1]


_WB_SPEC = pl.BlockSpec((HALF, LANES, 2 * LANES), lambda c: (0, 0, 0))
_WC_SPEC = pl.BlockSpec((HALF, 2 * LANES, LANES), lambda c: (0, 0, 0))
GELU_C = math.sqrt(2.0 / math.pi)
GELU_A = 0.044715


def _gelu(x):
    return 0.5 * x * (1.0 + jnp.tanh(GELU_C * (x + GELU_A * x * x * x)))


def _gelu_grad(x):
    th = jnp.tanh(GELU_C * (x + GELU_A * x * x * x))
    return 0.5 * (1.0 + th) + 0.5 * x * (1.0 - th * th) * GELU_C * (1.0 + 3.0 * GELU_A * x * x)


def _tile_rows(j, T, TP):
    return pl.ds(j * TP + SUBLANES, T)


def _pair(ref, jp, T, TP):
    return jnp.concatenate([ref[_tile_rows(2 * jp, T, TP), :], ref[_tile_rows(2 * jp + 1, T, TP), :]],
                           axis=1).astype(BF16)


def _unpair(ref, jp, val, T, TP):
    ref[_tile_rows(2 * jp, T, TP), :] = val[:, :LANES]
    ref[_tile_rows(2 * jp + 1, T, TP), :] = val[:, LANES:]


def _s5_project_in(u_ref, wb_ref, s3, T, TP):
    for jp in range(HALF):
        blk = _chan_block(2 * jp)
        _unpair(s3, jp, jnp.dot(u_ref[:, blk:blk + LANES], wb_ref[jp], preferred_element_type=F32), T, TP)


def _s5_scan_fwd(s3, a1, a2, s0, T, TP):
    span = (N_GROUPS - 1) * TP + 2 * SUBLANES

    def blk(i, s):
        view = s3.at[pl.ds(pl.multiple_of(i * SUBLANES, SUBLANES), span)]
        for k in range(SUBLANES):
            rows = pl.ds(SUBLANES + k, N_GROUPS, stride=TP)
            s = a1 * s + a2 * _swap(s) + view[rows, :]
            view[rows, :] = s
        return s

    return lax.fori_loop(0, T // SUBLANES, blk, s0)


def _s5_fwd(hn, wb, wc, a1, a2, dvec, *, name, xfers=()):
    L, D = hn.shape
    T = min(S5_CHUNK, L)
    TP = T + SUBLANES
    nC = L // T

    def body(u_ref, wb_ref, wc_ref, a1_ref, a2_ref, d_ref, y_ref, yg_ref, sb_ref, s3, st):
        @pl.when(pl.program_id(0) == 0)
        def _():
            st[...] = jnp.zeros_like(st)

        sb_ref[0] = st[...]
        _s5_project_in(u_ref, wb_ref, s3, T, TP)
        st[...] = _s5_scan_fwd(s3, a1_ref[...], a2_ref[...], st[...], T, TP)
        for jb in range(D // LANES):
            acc = jnp.zeros((T, LANES), F32)
            for jp in _pairs_of(jb):
                acc += jnp.dot(_pair(s3, jp, T, TP), wc_ref[jp], preferred_element_type=F32)
            cols = slice(jb * LANES, (jb + 1) * LANES)
            y = acc + d_ref[:, cols] * u_ref[:, cols].astype(F32)
            y_ref[:, cols] = y
            yg_ref[:, cols] = _gelu(y).astype(BF16)

    row = pl.BlockSpec((T, D), lambda c: (c, 0))
    aspec = pl.BlockSpec((N_GROUPS, LANES), lambda c: (0, 0))
    return _pcall(
        body, (hn, wb, wc, a1, a2, dvec), grid=(nC,),
        in_specs=[row, _WB_SPEC, _WC_SPEC, aspec, aspec, pl.BlockSpec((1, D), lambda c: (0, 0))],
        out_specs=[row, row, pl.BlockSpec((1, N_GROUPS, LANES), lambda c: (c, 0, 0))],
        out_shape=[jax.ShapeDtypeStruct((L, D), F32), jax.ShapeDtypeStruct((L, D), BF16),
                   jax.ShapeDtypeStruct((nC, N_GROUPS, LANES), F32)],
        scratch_shapes=[pltpu.VMEM((N_GROUPS * TP, LANES), F32), pltpu.VMEM((N_GROUPS, LANES), F32)],
        sem=("arbitrary",), name=name, xfers=xfers)


def _s5_bwd(hn, dyg, ypre, sbound, wb, wc, a1, a2, dvec, *, name, xfers=()):
    L, D = hn.shape
    T = min(S5_CHUNK, L)
    TP = T + SUBLANES
    nC = L // T
    span = (N_GROUPS - 1) * TP + 2 * SUBLANES
    NT = (((1,), (1,)), ((), ()))
    TN = (((0,), (0,)), ((), ()))

    def body(u_ref, dyg_ref, yp_ref, sb_ref, wb_ref, wc_ref, a1_ref, a2_ref, d_ref,
             du_ref, dwb_ref, dwc_ref, da1_ref, da2_ref, dd_ref, s3, g3, gst, dy_s):
        @pl.when(pl.program_id(0) == 0)
        def _():
            gst[...] = jnp.zeros_like(gst)
            dwb_ref[...] = jnp.zeros_like(dwb_ref)
            dwc_ref[...] = jnp.zeros_like(dwc_ref)
            da1_ref[...] = jnp.zeros_like(da1_ref)
            da2_ref[...] = jnp.zeros_like(da2_ref)
            dd_ref[...] = jnp.zeros_like(dd_ref)

        a1 = a1_ref[...]
        a2 = a2_ref[...]
        dy = dyg_ref[...].astype(F32) * _gelu_grad(yp_ref[...])
        dy_s[...] = dy.astype(BF16)
        dd_ref[...] += jnp.sum(dy * u_ref[...].astype(F32), axis=0, keepdims=True)
        du_ref[...] = d_ref[...] * dy

        s3[pl.ds(SUBLANES - 1, N_GROUPS, stride=TP), :] = sb_ref[0]
        _s5_project_in(u_ref, wb_ref, s3, T, TP)
        _s5_scan_fwd(s3, a1, a2, sb_ref[0], T, TP)

        for jp in range(HALF):
            blk = _chan_block(2 * jp)
            _unpair(g3, jp, lax.dot_general(dy_s[:, blk:blk + LANES], wc_ref[jp], NT, preferred_element_type=F32),
                    T, TP)
        a2c = -a2

        def rblk(ii, carry):
            g, acc1, acc2 = carry
            t0 = pl.multiple_of((T // SUBLANES - 1 - ii) * SUBLANES, SUBLANES)
            gv = g3.at[pl.ds(t0, span)]
            sv = s3.at[pl.ds(t0, span)]
            for k in reversed(range(SUBLANES)):
                rows = pl.ds(SUBLANES + k, N_GROUPS, stride=TP)
                g = a1 * g + a2c * _swap(g) + gv[rows, :]
                gv[rows, :] = g
                sp = sv[pl.ds(SUBLANES - 1 + k, N_GROUPS, stride=TP), :]
                acc1 = acc1 + g * sp
                acc2 = acc2 + g * _swap(sp)
            return g, acc1, acc2

        zero = jnp.zeros((N_GROUPS, LANES), F32)
        g, acc1, acc2 = lax.fori_loop(0, T // SUBLANES, rblk, (gst[...], zero, zero))
        gst[...] = g
        da1_ref[...] += acc1
        da2_ref[...] += acc2

        for jb in range(D // LANES):
            cols = slice(jb * LANES, (jb + 1) * LANES)
            acc = jnp.zeros((T, LANES), F32)
            for jp in _pairs_of(jb):
                gp = _pair(g3, jp, T, TP)
                dwc_ref[jp] += lax.dot_general(_pair(s3, jp, T, TP), dy_s[:, cols], TN, preferred_element_type=F32)
                dwb_ref[jp] += lax.dot_general(u_ref[:, cols], gp, TN, preferred_element_type=F32)
                acc += lax.dot_general(gp, wb_ref[jp], NT, preferred_element_type=F32)
            du_ref[:, cols] += acc

    rrow = pl.BlockSpec((T, D), lambda c: (nC - 1 - c, 0))
    aspec = pl.BlockSpec((N_GROUPS, LANES), lambda c: (0, 0))
    vec = pl.BlockSpec((1, D), lambda c: (0, 0))
    return _pcall(
        body, (hn, dyg, ypre, sbound, wb, wc, a1, a2, dvec), grid=(nC,),
        in_specs=[rrow, rrow, rrow, pl.BlockSpec((1, N_GROUPS, LANES), lambda c: (nC - 1 - c, 0, 0)),
                  _WB_SPEC, _WC_SPEC, aspec, aspec, vec],
        out_specs=[rrow, _WB_SPEC, _WC_SPEC, aspec, aspec, vec],
        out_shape=[jax.ShapeDtypeStruct((L, D), F32),
                   jax.ShapeDtypeStruct((HALF, LANES, 2 * LANES), F32),
                   jax.ShapeDtypeStruct((HALF, 2 * LANES, LANES), F32),
                   jax.ShapeDtypeStruct((N_GROUPS, LANES), F32), jax.ShapeDtypeStruct((N_GROUPS, LANES), F32),
                   jax.ShapeDtypeStruct((1, D), F32)],
        scratch_shapes=[pltpu.VMEM((N_GROUPS * TP, LANES), F32), pltpu.VMEM((N_GROUPS * TP, LANES), F32),
                        pltpu.VMEM((N_GROUPS, LANES), F32), pltpu.VMEM((T, D), BF16)],
        sem=("arbitrary",), name=name, xfers=xfers)


def _s5_prep(lam_re, lam_im, log_dt, b_re, b_im, c_re, c_im):
    dt = jnp.exp(log_dt)[:, None]
    mag = jnp.exp(lam_re * dt)
    lb_re = mag * jnp.cos(lam_im * dt)
    lb_im = mag * jnp.sin(lam_im * dt)
    den = lam_re * lam_re + lam_im * lam_im
    nr = lb_re - 1.0
    fr = ((nr * lam_re + lb_im * lam_im) / den)[..., None]
    fi = ((lb_im * lam_re - nr * lam_im) / den)[..., None]
    bb_re = fr * b_re - fi * b_im
    bb_im = fr * b_im + fi * b_re
    pair = lambda a: a.reshape(HALF, 2 * SSM_STATE)
    a1 = jnp.concatenate([pair(lb_re), pair(lb_re)], axis=0)
    a2 = jnp.concatenate([-pair(lb_im), pair(lb_im)], axis=0)
    sel = jax.nn.one_hot(jnp.arange(HALF) % 4, 4, dtype=F32)
    eye = jnp.eye(2, dtype=F32)

    def w_in(bb):
        return jnp.einsum('jk,ef,jfph->jkehfp', sel, eye, bb.reshape(HALF, 2, SSM_STATE, SSM_GROUP)
                          ).reshape(HALF, LANES, LANES)

    def w_out(c):
        return jnp.einsum('jk,ef,jfhp->jepkfh', sel, eye, c.reshape(HALF, 2, SSM_GROUP, SSM_STATE)
                          ).reshape(HALF, LANES, LANES)

    wb = jnp.concatenate([w_in(bb_re), w_in(bb_im)], axis=0)
    wc = jnp.concatenate([w_out(c_re), w_out(-c_im)], axis=0)
    wb = wb.reshape(HALF, 2, LANES, LANES).transpose(0, 2, 1, 3).reshape(HALF, LANES, 2 * LANES)
    wc = wc.reshape(HALF, 2 * LANES, LANES)
    return a1, a2, wb, wc


def _tri(n, upper):
    r = lax.broadcasted_iota(jnp.int32, (n, n), 0)
    c = lax.broadcasted_iota(jnp.int32, (n, n), 1)
    return ((r <= c) if upper else (r >= c)).astype(F32)


def _fgate_fwd(fl, bf, *, name):
    L, W = fl.shape
    tr = _tile(L, CUM_TILE, SUBLANES)

    def body(f_ref, b_ref, o_ref, carry):
        @pl.when(pl.program_id(0) == 0)
        def _():
            carry[...] = jnp.zeros_like(carry)

        x = f_ref[...] + b_ref[...]
        ls = jnp.minimum(x, 0.0) - jnp.log(1.0 + jnp.exp(-jnp.abs(x)))
        cum = jnp.dot(_tri(tr, False), ls, preferred_element_type=F32, precision=lax.Precision.HIGHEST) + carry[...]
        o_ref[...] = cum
        carry[...] = cum[tr - 1:tr, :]

    return pl.pallas_call(
        body, grid=(L // tr,),
        in_specs=[pl.BlockSpec((tr, W), lambda i: (i, 0)), pl.BlockSpec((1, W), lambda i: (0, 0))],
        out_specs=pl.BlockSpec((tr, W), lambda i: (i, 0)), out_shape=jax.ShapeDtypeStruct((L, W), F32),
        scratch_shapes=[pltpu.VMEM((1, W), F32)], compiler_params=_params("arbitrary"), name=name)(fl, bf)


def _fgate_bwd(fl, bf, dcum, *, name):
    L, W = fl.shape
    tr = _tile(L, CUM_TILE, SUBLANES)
    n = L // tr

    def body(f_ref, b_ref, d_ref, o_ref, db_ref, carry):
        @pl.when(pl.program_id(0) == 0)
        def _():
            carry[...] = jnp.zeros_like(carry)
            db_ref[...] = jnp.zeros_like(db_ref)

        d = d_ref[...]
        rev = jnp.dot(_tri(tr, True), d, preferred_element_type=F32, precision=lax.Precision.HIGHEST) + carry[...]
        carry[...] += jnp.sum(d, axis=0, keepdims=True)
        df = rev * jax.nn.sigmoid(-(f_ref[...] + b_ref[...]))
        o_ref[...] = df
        db_ref[...] += jnp.sum(df, axis=0, keepdims=True)

    rrow = pl.BlockSpec((tr, W), lambda i: (n - 1 - i, 0))
    vec = pl.BlockSpec((1, W), lambda i: (0, 0))
    return pl.pallas_call(
        body, grid=(n,), in_specs=[rrow, vec, rrow], out_specs=[rrow, vec],
        out_shape=[jax.ShapeDtypeStruct((L, W), F32), jax.ShapeDtypeStruct((1, W), F32)],
        scratch_shapes=[pltpu.VMEM((1, W), F32)], compiler_params=_params("arbitrary"), name=name)(fl, bf, dcum)


_NT = (((1,), (1,)), ((), ()))
_TN = (((0,), (0,)), ((), ()))
HEAD_PAIRS = N_HEADS // 2


def _causal_tiles(n, by_row):
    pairs = ([(i, j) for i in range(n) for j in range(i + 1)] if by_row
             else [(i, j) for j in range(n) for i in range(j, n)])
    return (jnp.array([p[0] for p in pairs], jnp.int32), jnp.array([p[1] for p in pairs], jnp.int32))


def _attn_logits(qs, k, ck, masked, t):
    s = lax.dot_general(qs, k, _NT, preferred_element_type=F32) - ck
    if masked:
        r = lax.broadcasted_iota(jnp.int32, (t, t), 0)
        c = lax.broadcasted_iota(jnp.int32, (t, t), 1)
        s = jnp.where(c > r, NEG, s)
    return s


def _attn_fwd(q, kv, ck, *, name, xfers=()):
    L, D = q.shape
    t = _tile(L, ATTN_TILE)
    n = L // t
    dh = HEAD_DIM

    def body(q_ref, k_ref, v_ref, ck_ref, o_ref, o32_ref, lse_ref, m_s, l_s, acc):
        i, j = pl.program_id(1), pl.program_id(2)

        @pl.when(j == 0)
        def _():
            m_s[...] = jnp.full_like(m_s, NEG)
            l_s[...] = jnp.zeros_like(l_s)
            acc[...] = jnp.zeros_like(acc)

        def tile(masked):
            for e in range(2):
                sl = slice(e * dh, (e + 1) * dh)
                v = v_ref[:, sl]
                s = _attn_logits(q_ref[:, sl] * ATTN_SCALE, k_ref[:, sl], ck_ref[e], masked, t)
                m_new = jnp.maximum(m_s[e], jnp.max(s, axis=1, keepdims=True))
                alpha = jnp.exp(m_s[e] - m_new)
                p = jnp.exp(s - m_new)
                l_s[e] = alpha * l_s[e] + jnp.sum(p, axis=1, keepdims=True)
                p_hi = p.astype(BF16)
                p_lo = (p - p_hi.astype(F32)).astype(BF16)
                pv = (jnp.dot(p_hi, v, preferred_element_type=F32) + jnp.dot(p_lo, v, preferred_element_type=F32))
                acc[e] = alpha * acc[e] + pv
                m_s[e] = m_new

        pl.when(j < i)(functools.partial(tile, False))
        pl.when(j == i)(functools.partial(tile, True))

        @pl.when(j == n - 1)
        def _():
            for e in range(2):
                sl = slice(e * dh, (e + 1) * dh)
                o = acc[e] / l_s[e]
                o_ref[:, sl] = o.astype(BF16)
                o32_ref[:, sl] = o
                lse_ref[e] = m_s[e] + jnp.log(l_s[e])

    qs = pl.BlockSpec((t, LANES), lambda h, i, j: (i, h))
    ks = pl.BlockSpec((t, LANES), lambda h, i, j: (jnp.minimum(i, j), h))
    vs = pl.BlockSpec((t, LANES), lambda h, i, j: (jnp.minimum(i, j), HEAD_PAIRS + h))
    cs = pl.BlockSpec((2, 1, t), lambda h, i, j: (h, 0, jnp.minimum(i, j)))
    return _pcall(
        body, (q, kv, kv, ck), grid=(HEAD_PAIRS, n, n), in_specs=[qs, ks, vs, cs],
        out_specs=[qs, qs, pl.BlockSpec((2, t, 1), lambda h, i, j: (h, i, 0))],
        out_shape=[jax.ShapeDtypeStruct((L, D), BF16), jax.ShapeDtypeStruct((L, D), F32),
                   jax.ShapeDtypeStruct((N_HEADS, L, 1), F32)],
        scratch_shapes=[pltpu.VMEM((2, t, 1), F32), pltpu.VMEM((2, t, 1), F32), pltpu.VMEM((2, t, dh), F32)],
        sem=("parallel", "parallel", "arbitrary"), name=name, xfers=xfers)


def _attn_delta(do, o, *, name):
    L, D = do.shape
    t = _tile(L, ATTN_TILE)

    def body(do_ref, o_ref, d_ref):
        prod = do_ref[...].astype(F32) * o_ref[...]
        head = lax.broadcasted_iota(jnp.int32, (N_HEADS, D), 0)
        col = lax.broadcasted_iota(jnp.int32, (N_HEADS, D), 1)
        sel = (col // HEAD_DIM == head).astype(F32)
        d_ref[:, 0, :] = lax.dot_general(sel, prod, _NT, preferred_element_type=F32, precision=lax.Precision.HIGHEST)

    row = pl.BlockSpec((t, D), lambda i: (i, 0))
    return pl.pallas_call(
        body, grid=(L // t,), in_specs=[row, row], out_specs=pl.BlockSpec((N_HEADS, 1, t), lambda i: (0, 0, i)),
        out_shape=jax.ShapeDtypeStruct((N_HEADS, 1, L), F32), compiler_params=_params("parallel"), name=name)(do, o)


def _attn_bwd(q, kv, ck_col, do, lse_row, delta_row, *, name, xfers=()):
    L, D = q.shape
    t = _tile(L, ATTN_TILE)
    n = L // t
    dh = HEAD_DIM

    def body(i_tab, j_tab, q_ref, k_ref, v_ref, ck_ref, do_ref, lse_ref, dl_ref, dq_ref, dk_ref, dv_ref, dck_ref):
        i, j = i_tab[pl.program_id(1)], j_tab[pl.program_id(1)]

        @pl.when(pl.program_id(1) == 0)
        def _():
            dq_ref[...] = jnp.zeros_like(dq_ref)

        @pl.when(i == j)
        def _():
            dk_ref[...] = jnp.zeros_like(dk_ref)
            dv_ref[...] = jnp.zeros_like(dv_ref)
            dck_ref[...] = jnp.zeros_like(dck_ref)

        def tile(masked):
            cols = pl.ds(pl.multiple_of(i * t, t), t)
            for e in range(2):
                sl = slice(e * dh, (e + 1) * dh)
                qs = q_ref[:, sl] * ATTN_SCALE
                k = k_ref[:, sl]
                do = do_ref[:, sl]
                st = lax.dot_general(k, qs, _NT, preferred_element_type=F32) - ck_ref[e]
                if masked:
                    kpos = lax.broadcasted_iota(jnp.int32, (t, t), 0)
                    qpos = lax.broadcasted_iota(jnp.int32, (t, t), 1)
                    st = jnp.where(kpos > qpos, NEG, st)
                pt = jnp.exp(st - lse_ref[e])
                dpt = lax.dot_general(v_ref[:, sl], do, _NT, preferred_element_type=F32)
                dst = pt * (dpt - dl_ref[e])
                dst16 = dst.astype(BF16)
                dv_ref[:, sl] += jnp.dot(pt.astype(BF16), do, preferred_element_type=F32)
                dk_ref[:, sl] += jnp.dot(dst16, qs, preferred_element_type=F32)
                dck_ref[e] -= jnp.sum(dst, axis=1, keepdims=True)
                dq_ref[sl, cols] += lax.dot_general(k, dst16, _TN, preferred_element_type=F32) * ATTN_SCALE

        pl.when(i > j)(functools.partial(tile, False))
        pl.when(i == j)(functools.partial(tile, True))

    qs = pl.BlockSpec((t, LANES), lambda h, s, it, jt: (it[s], h))
    ks = pl.BlockSpec((t, LANES), lambda h, s, it, jt: (jt[s], h))
    vs = pl.BlockSpec((t, LANES), lambda h, s, it, jt: (jt[s], HEAD_PAIRS + h))
    cs = pl.BlockSpec((2, t, 1), lambda h, s, it, jt: (h, jt[s], 0))
    ls = pl.BlockSpec((2, 1, t), lambda h, s, it, jt: (h, 0, it[s]))
    full = jax.ShapeDtypeStruct((L, D), F32)
    tabs = _causal_tiles(n, by_row=False)
    return _pcall(
        body, (q, kv, kv, ck_col, do, lse_row, delta_row), grid=(HEAD_PAIRS, tabs[0].shape[0]),
        in_specs=[qs, ks, vs, cs, qs, ls, ls],
        out_specs=[pl.BlockSpec((LANES, L), lambda h, s, it, jt: (h, 0)), ks, ks, cs],
        out_shape=[jax.ShapeDtypeStruct((D, L), F32), full, full, jax.ShapeDtypeStruct((N_HEADS, L, 1), F32)],
        sem=("parallel", "arbitrary"), name=name, xfers=xfers, prefetch=tabs)


SHARD_COLS_FFN = 2 * D_FF // N_DEV
SHARD_ROWS_FFN = D_FF // N_DEV
SHARD_COLS_GLU = 2 * D_MODEL // N_DEV
SHARD_ROWS_QO = D_MODEL // N_DEV
SHARD_COLS_KVF = (2 * D_MODEL + N_HEADS) // N_DEV
S5_NAMES = ("lam_re", "lam_im", "log_dt", "ssm_b_re", "ssm_b_im", "ssm_c_re", "ssm_c_im")
REPL_LATE_ROWS = 288
REPL_EARLY_ROWS = 320


def _leaves(parts):
    out = []
    for p in parts:
        out.extend(_leaves(p) if isinstance(p, (list, tuple)) else [p.reshape(-1)])
    return out


def _pack_rows(parts, rows):
    flat = jnp.concatenate(_leaves(parts))
    return jnp.pad(flat, (0, rows * D_MODEL - flat.shape[0])).reshape(rows, D_MODEL)


def _unpack_rows(flat, like):
    flat, out, off = flat.reshape(-1), [], 0
    for p in _leaves(like):
        out.append(flat[off:off + p.shape[0]])
        off += p.shape[0]
    return out


def _repl_late(d):
    return [d["g_mix"][0], [d[n][0] for n in S5_NAMES]]


def _repl_early(d):
    return [list(d["g_mix"][1:]), list(d["g_ffn"]), [d[n][1] for n in S5_NAMES], d["g_kv"], d["b_f"],
            list(d["ffn_conv_b"]), d["g_final"]]


def _kvf_blocks(full):
    return full.reshape(D_MODEL, N_DEV, SHARD_COLS_KVF).transpose(1, 0, 2)


class _Step:
    def __init__(self, weights, send=None, plan=None):
        self.w = dict(weights)
        self.send = send or {}
        self.grad = {}
        self.bcast = {}
        self.slots = {}
        self.plan = plan or {}

    def xfers(self, host):
        src = {"w": self.send, "g": self.grad, "b": self.bcast}
        return [(src[kind][k], kind == "g") for kind, k in self.plan.get(host, ())]

    def land(self, host, gathered):
        for (kind, k), g in zip(self.plan.get(host, ()), gathered):
            if kind == "w":
                self.arrive(k, g)
            else:
                self.slots[k] = g

    def arrive(self, k, g):
        name = k[0]
        if name == "w_ffn_in" and len(k) == 3:
            self.w[k] = g
            parts = [self.w.get((name, k[1], p)) for p in range(len(_W_IN_PARTS[k[1]]))]
            if all(p is not None for p in parts):
                self.w[name, k[1]] = jnp.concatenate(parts, axis=1)
        elif name == "w_ffn_out":
            self.w[k] = g.reshape(4, 2 * SHARD_ROWS_FFN, D_MODEL)
        elif name in ("w_q", "w_o"):
            self.w[k] = g.reshape(D_MODEL, D_MODEL)
        elif name == "w_kvf":
            full = g.transpose(1, 0, 2).reshape(D_MODEL, N_DEV * SHARD_COLS_KVF)
            self.w["w_kv",] = full[:, :2 * D_MODEL]
            self.w["w_f",] = jnp.pad(full[:, 2 * D_MODEL:], ((0, 0), (0, LANES - N_HEADS)))
        elif name == "small":
            flat = g.reshape(N_DEV, -1)
            self.w["ssm_d",] = flat[:, :256].reshape(N_DEV, N_A, LANES).transpose(1, 0, 2).reshape(N_A, D_MODEL)
            cw = flat[:, 256:256 + DEPTH * 3 * SHARD_COLS_FFN].reshape(N_DEV, DEPTH, 3, SHARD_COLS_FFN)
            for layer in range(DEPTH):
                self.w["conv_w", layer] = cw[:, layer].reshape(2, 4, 3, SHARD_COLS_FFN)
        else:
            self.w[k] = g

    def run(self, host, fn, *args, **kw):
        xf = self.xfers(host)
        res = fn(*args, name=host, xfers=xf, **kw)
        if not xf:
            return res[0] if isinstance(res, (list, tuple)) and len(res) == 1 else res
        n_own = len(res) - len(xf)
        self.land(host, res[n_own:])
        return res[0] if n_own == 1 else res[:n_own]


def _step(x, target, S):
    L = x.shape[0]
    W = S.w
    vec = lambda a: a.reshape(1, -1)
    CF = SHARD_COLS_FFN

    h = x
    saved = []
    kvs = None
    for layer in range(DEPTH):
        t = str(layer)
        if layer < N_A:
            (a1, a2, wb, wc), prep_vjp = jax.vjp(_s5_prep, *[W[n][layer] for n in S5_NAMES])
            wb16, wc16 = wb.astype(BF16), wc.astype(BF16)
            hn = _rms_fwd(h, vec(W["g_mix"][layer]), name="mix_norm" + t)
            dvec = vec(W["ssm_d",][layer])
            ypre, yg, sb = S.run("s5_fwd" + t, _s5_fwd, hn, wb16, wc16, a1, a2, dvec)
            z = S.run("glu_mm" + t, _mm, yg, W["w_glu", layer], bk="bkn", ok="bmn", tm=4096)
            z = z.reshape(2, 4, L, SHARD_COLS_GLU)
            h1, hn2 = _glu_res_rms(z, h, vec(W["g_ffn"][layer]), name="glu_res" + t)
            mix_saved = (h, hn, ypre, yg, sb, z, a1, a2, wb16, wc16, dvec, prep_vjp)
        else:
            j = layer - N_A
            if layer == N_A:
                hkv = _rms_fwd(h, vec(W["g_kv"]), name="kv_norm")
                kvm = S.run("kv_mm", _mm, hkv, W["w_kv",], out_dtype=BF16)
                fl = S.run("f_mm", _mm, hkv, W["w_f",])
                cum = _fgate_fwd(fl, W["b_f_pad",], name="fgate_fwd")
                ck = cum[:, :N_HEADS].T.reshape(N_HEADS, 1, L)
                kvs = (h, hkv, fl, kvm, ck)
            _, _, _, kvm, ck = kvs
            hn = _rms_fwd(h, vec(W["g_mix"][layer]), name="mix_norm" + t)
            q = S.run("q_mm" + t, _mm, hn, W["w_q", j], out_dtype=BF16)
            o, o32, lse = S.run("attn_fwd" + t, _attn_fwd, q, kvm, ck)
            h1 = S.run("o_mm" + t, _mm, o, W["w_o", j], add=h)
            hn2 = _rms_fwd(h1, vec(W["g_ffn"][layer]), name="ffn_norm" + t)
            mix_saved = (h, hn, q, o32, o, lse)
        u0 = S.run("ffn_in" + t, _mm, hn2, W["w_ffn_in", layer], bk="nbk", ok="bmn", tm=2048).reshape(2, 4, L, CF)
        a = S.run("ffn_act" + t, _conv_act, u0, W["conv_w", layer], W["conv_b", layer])
        h2 = S.run("ffn_out" + t, _mm, a, W["w_ffn_out", layer], ak="bmk", bk="kbn", add=h1, tn=512, kg=4)
        saved.append((mix_saved, h1, hn2, u0))
        h = h2

    loss, dh, dg_final = _loss_head(h, vec(W["g_final"]), target, name="loss_head")
    g = {"g_final": dg_final.reshape(-1)}
    gl = {k: [None] * DEPTH for k in ("g_mix", "g_ffn", "conv_w", "ffn_conv_b")}
    ga = {k: [None] * N_A for k in S5_NAMES + ("ssm_d",)}
    dk = dv = dck = None
    for layer in reversed(range(DEPTH)):
        t = str(layer)
        mix_saved, h1, hn2, u0 = saved[layer]
        cw, cb = W["conv_w", layer], W["conv_b", layer]
        da = S.run("ffn_da" + t, _mm, dh, W["w_ffn_out", layer], bk="nbk", ok="bmn", tm=2048)
        a, du0, dcw, dcb = S.run("ffn_conv_bwd" + t, _conv_ffn_bwd, u0, cw, cb, da)
        dw_out = S.run("ffn_dwout" + t, _mm, a, dh, ak="bkm", ok="mbn", out_dtype=BF16)
        S.grad["w_ffn_out", layer] = dw_out.reshape(N_DEV, SHARD_ROWS_FFN, D_MODEL)
        du0 = du0.reshape(N_DEV, L, CF)
        S.grad["w_ffn_in", layer] = S.run("ffn_dwin" + t, _mm, du0, hn2, ak="bkm", ok="mbn", out_dtype=BF16)
        dhn2 = S.run("ffn_dhn" + t, _mm, du0, W["w_ffn_in", layer], ak="bmk", bk="kbn", kg=4)
        dh1, dg = _rms_bwd(h1, vec(W["g_ffn"][layer]), dhn2, dh, name="ffn_norm_bwd" + t)
        gl["g_ffn"][layer], gl["conv_w"][layer], gl["ffn_conv_b"][layer] = dg.reshape(-1), dcw, dcb.reshape(-1)
        if layer < N_A:
            hin, hn, ypre, yg, sb, z, a1, a2, wb16, wc16, dvec, prep_vjp = mix_saved
            dz = _glu_bwd(z, dh1, name="glu_bwd" + t).reshape(N_DEV, L, SHARD_COLS_GLU)
            S.grad["w_glu", layer] = S.run("glu_dw" + t, _mm, yg, dz, ak="km", bk="bkn", ok="bmn", out_dtype=BF16,
                                           tk=4096)
            dyg = S.run("glu_dy" + t, _mm, dz, W["w_glu", layer], ak="bmk", bk="bnk", out_dtype=BF16, tm=2048, kg=8)
            if layer == 0:
                S.bcast["repl_early",] = _pack_rows(_repl_early({**g, **gl, **ga}), REPL_EARLY_ROWS).astype(BF16)
            du, dwb, dwc, da1, da2, dd = S.run("s5_bwd" + t, _s5_bwd, hn, dyg, ypre, sb, wb16, wc16, a1, a2, dvec)
            for nme, val in zip(S5_NAMES, prep_vjp((da1, da2, dwb, dwc))):
                ga[nme][layer] = val
            ga["ssm_d"][layer] = dd.reshape(-1)
            dh, dg = _rms_bwd(hin, vec(W["g_mix"][layer]), du, dh1, name="mix_norm_bwd" + t)
        else:
            j = layer - N_A
            hin, hn, q, o32, o, lse = mix_saved
            _, _, _, kvm, ck = kvs
            S.grad["w_o", j] = S.run("o_dw" + t, _mm, o, dh1, ak="km", out_dtype=BF16
                                     ).reshape(N_DEV, SHARD_ROWS_QO, D_MODEL)
            do = S.run("o_dx" + t, _mm, dh1, W["w_o", j], bk="nk", out_dtype=BF16)
            delta = _attn_delta(do, o32, name="attn_delta" + t)
            dq_t, dk_l, dv_l, dck_l = S.run("attn_bwd" + t, _attn_bwd, q, kvm, ck.reshape(N_HEADS, L, 1), do,
                                            lse.reshape(N_HEADS, 1, L), delta)
            dk = dk_l if dk is None else dk + dk_l
            dv = dv_l if dv is None else dv + dv_l
            dck = dck_l if dck is None else dck + dck_l
            S.grad["w_q", j] = S.run("q_dw" + t, _mm, hn, dq_t, ak="km", bk="nk", out_dtype=BF16
                                     ).reshape(N_DEV, SHARD_ROWS_QO, D_MODEL)
            dhn = S.run("q_dx" + t, _mm, dq_t, W["w_q", j], ak="km", bk="nk")
            dh, dg = _rms_bwd(hin, vec(W["g_mix"][layer]), dhn, dh1, name="mix_norm_bwd" + t)
            if layer == N_A:
                hkv_in, hkv, fl, _, _ = kvs
                dcum = jnp.pad(dck.reshape(N_HEADS, L).T, ((0, 0), (0, LANES - N_HEADS)))
                dfl, dbf = _fgate_bwd(fl, W["b_f_pad",], dcum, name="fgate_bwd")
                dkv = jnp.concatenate([dk, dv], axis=1).astype(BF16)
                dfl16 = dfl.astype(BF16)
                dw_kv = S.run("kv_dw", _mm, hkv, dkv, ak="km")
                dw_f = S.run("f_dw", _mm, hkv, dfl16, ak="km")
                S.grad["w_kvf",] = _kvf_blocks(jnp.concatenate([dw_kv, dw_f[:, :N_HEADS]], axis=1)).astype(BF16)
                dhkv = S.run("kv_dx", _mm, dkv, W["w_kv",], bk="nk")
                dhkv = S.run("f_dx", _mm, dfl16, W["w_f",], bk="nk", add=dhkv)
                g["b_f"] = dbf[0, :N_HEADS]
                dh, dgkv = _rms_bwd(hkv_in, vec(W["g_kv"]), dhkv, dh, name="kv_norm_bwd")
                g["g_kv"] = dgkv.reshape(-1)
        gl["g_mix"][layer] = dg.reshape(-1)

    for d in (gl, ga):
        for k, v in d.items():
            g[k] = jnp.stack(v)
    return loss, dh, g


def _adamw_layers(slots, w, m, v, *, name):
    shape = w.shape
    nl = len(slots)
    w, m, v = (a.reshape((nl,) + a.shape[-2:]) for a in (w, m, v))
    _, R, C = w.shape
    tr = _tile(R, 256, 16)
    c1 = 1.0 / (1.0 - ADAM_B1 ** ADAM_STEP)
    c2 = 1.0 / (1.0 - ADAM_B2 ** ADAM_STEP)

    def body(*refs):
        s_refs, (w_ref, m_ref, v_ref), (g_ref, d_ref, nm_ref, nv_ref) = refs[:nl], refs[nl:nl + 3], refs[nl + 3:]
        for layer in range(nl):
            @pl.when(pl.program_id(0) == layer)
            def _(s_ref=s_refs[layer]):
                g = s_ref[0].astype(F32)
                for d in range(1, N_DEV):
                    g = g + s_ref[d].astype(F32)
                m2 = ADAM_B1 * m_ref[...] + (1.0 - ADAM_B1) * g
                v2 = ADAM_B2 * v_ref[...] + (1.0 - ADAM_B2) * (g * g)
                g_ref[...] = g
                nm_ref[...] = m2
                nv_ref[...] = v2
                d_ref[...] = -ADAM_LR * ((m2 * c1) / (jnp.sqrt(v2 * c2) + ADAM_EPS) + ADAM_WD * w_ref[...])

    def slab_spec(layer):
        return pl.BlockSpec((N_DEV, tr, C), lambda l, i: (0, jnp.where(l == layer, i, 0), 0))

    row = pl.BlockSpec((None, tr, C), lambda l, i: (l, i, 0))
    out = jax.ShapeDtypeStruct((nl, R, C), F32)
    outs = pl.pallas_call(
        body, grid=(nl, R // tr), in_specs=[slab_spec(layer) for layer in range(nl)] + [row, row, row],
        out_specs=[row, row, row, row], out_shape=[out, out, out, out],
        compiler_params=_params("arbitrary", "arbitrary"), name=name)(*slots, w, m, v)
    return [o.reshape(shape) for o in outs]


_SMALL_ROWS = 72
_ORDER = ("g_mix", "g_ffn", "lam_re", "lam_im", "log_dt", "ssm_b_re", "ssm_b_im", "ssm_c_re", "ssm_c_im", "ssm_d",
          "w_glu", "g_kv", "w_kvf", "b_f", "w_q", "w_o", "w_ffn_in", "ffn_conv_w", "ffn_conv_b", "w_ffn_out", "g_final")


def _pack_small(ssm_d, conv_w):
    flat = jnp.concatenate([ssm_d.reshape(-1), conv_w.reshape(-1)])
    return jnp.pad(flat, (0, _SMALL_ROWS * LANES - flat.shape[0])).reshape(_SMALL_ROWS, LANES)


def _unpack_small(flat):
    flat = flat.reshape(-1)
    return flat[:256].reshape(2, 128), flat[256:256 + 8448].reshape(4, 3, 704)


_W_IN_PARTS = {0: (352, 352), 1: (176, 528)}
_FWD_PLAN = {
    "start": [("small",)],
    "s5_fwd0": [("w_glu", 0), ("w_ffn_in", 0, 0)],
    "glu_mm0": [("w_ffn_in", 0, 1)],
    "ffn_in0": [("w_ffn_out", 0)],
    "ffn_act0": [("w_glu", 1)],
    "ffn_out0": [("w_ffn_in", 1, 0)],
    "s5_fwd1": [("w_ffn_in", 1, 1)],
    "ffn_in1": [("w_ffn_out", 1)],
    "ffn_act1": [("w_kvf",), ("w_o", 0)],
    "ffn_out1": [("w_q", 0)],
    "attn_fwd2": [("w_ffn_in", 2), ("w_ffn_out", 2), ("w_q", 1), ("w_o", 1)],
    "attn_fwd3": [("w_ffn_in", 3), ("w_ffn_out", 3)],
}
_BWD_PLAN = {
    "ffn_dhn3": [("w_ffn_out", 3)],
    "attn_bwd3": [("w_ffn_in", 3), ("w_o", 1)],
    "ffn_conv_bwd2": [("w_q", 1)],
    "ffn_dhn2": [("w_ffn_out", 2)],
    "attn_bwd2": [("w_ffn_in", 2), ("w_o", 0)],
    "ffn_conv_bwd1": [("w_q", 0), ("w_kvf",)],
    "ffn_dhn1": [("w_ffn_out", 1)],
    "s5_bwd1": [("w_ffn_in", 1), ("w_glu", 1)],
    "ffn_dhn0": [("w_ffn_out", 0)],
    "s5_bwd0": [("w_ffn_in", 0), ("w_glu", 0), ("repl_early",)],
    "end": [("small",), ("repl_late",)],
}
_PLAN = {h: [("w", k) for k in ks] for h, ks in _FWD_PLAN.items()}
_PLAN.update({h: [("b" if k[0].startswith("repl") else "g", k) for k in ks] for h, ks in _BWD_PLAN.items()})


def kernel(x, g_mix, g_ffn, lam_re, lam_im, log_dt, ssm_b_re, ssm_b_im, ssm_c_re, ssm_c_im, ssm_d, w_glu, g_kv, w_kvf, b_f, w_q, w_o, w_ffn_in, ffn_conv_w, ffn_conv_b, w_ffn_out, g_final, loss_target, m_g_mix, m_g_ffn, m_lam_re, m_lam_im, m_log_dt, m_ssm_b_re, m_ssm_b_im, m_ssm_c_re, m_ssm_c_im, m_ssm_d, m_w_glu, m_g_kv, m_w_kvf, m_b_f, m_w_q, m_w_o, m_w_ffn_in, m_ffn_conv_w, m_ffn_conv_b, m_w_ffn_out, m_g_final, v_g_mix, v_g_ffn, v_lam_re, v_lam_im, v_log_dt, v_ssm_b_re, v_ssm_b_im, v_ssm_c_re, v_ssm_c_im, v_ssm_d, v_w_glu, v_g_kv, v_w_kvf, v_b_f, v_w_q, v_w_o, v_w_ffn_in, v_ffn_conv_w, v_ffn_conv_b, v_w_ffn_out, v_g_final):
    wts = dict(g_mix=g_mix, g_ffn=g_ffn, lam_re=lam_re, lam_im=lam_im, log_dt=log_dt, ssm_b_re=ssm_b_re,
               ssm_b_im=ssm_b_im, ssm_c_re=ssm_c_re, ssm_c_im=ssm_c_im, ssm_d=ssm_d, w_glu=w_glu, g_kv=g_kv,
               w_kvf=w_kvf, b_f=b_f, w_q=w_q, w_o=w_o, w_ffn_in=w_ffn_in, ffn_conv_w=ffn_conv_w,
               ffn_conv_b=ffn_conv_b, w_ffn_out=w_ffn_out, g_final=g_final)
    mom = dict(g_mix=m_g_mix, g_ffn=m_g_ffn, lam_re=m_lam_re, lam_im=m_lam_im, log_dt=m_log_dt, ssm_b_re=m_ssm_b_re,
               ssm_b_im=m_ssm_b_im, ssm_c_re=m_ssm_c_re, ssm_c_im=m_ssm_c_im, ssm_d=m_ssm_d, w_glu=m_w_glu,
               g_kv=m_g_kv, w_kvf=m_w_kvf, b_f=m_b_f, w_q=m_w_q, w_o=m_w_o, w_ffn_in=m_w_ffn_in,
               ffn_conv_w=m_ffn_conv_w, ffn_conv_b=m_ffn_conv_b, w_ffn_out=m_w_ffn_out, g_final=m_g_final)
    var = dict(g_mix=v_g_mix, g_ffn=v_g_ffn, lam_re=v_lam_re, lam_im=v_lam_im, log_dt=v_log_dt, ssm_b_re=v_ssm_b_re,
               ssm_b_im=v_ssm_b_im, ssm_c_re=v_ssm_c_re, ssm_c_im=v_ssm_c_im, ssm_d=v_ssm_d, w_glu=v_w_glu,
               g_kv=v_g_kv, w_kvf=v_w_kvf, b_f=v_b_f, w_q=v_w_q, w_o=v_w_o, w_ffn_in=v_w_ffn_in,
               ffn_conv_w=v_ffn_conv_w, ffn_conv_b=v_ffn_conv_b, w_ffn_out=v_w_ffn_out, g_final=v_g_final)
    kinds = ("grad", "delta", "m", "v")

    ready = {n: wts[n] for n in ("g_mix", "g_ffn", "g_kv", "g_final") + S5_NAMES}
    ready["b_f_pad",] = jnp.pad(b_f, (0, LANES - N_HEADS)).reshape(1, LANES)
    send = {("small",): _pack_small(ssm_d, ffn_conv_w), ("w_kvf",): w_kvf.astype(BF16)}
    for layer in range(DEPTH):
        ready["conv_b", layer] = ffn_conv_b[layer].reshape(2, 4, 1, SHARD_COLS_FFN)
        w_in_t = jnp.swapaxes(w_ffn_in[layer], 0, 1).astype(BF16)
        if layer in _W_IN_PARTS:
            row = 0
            for p, rows in enumerate(_W_IN_PARTS[layer]):
                send["w_ffn_in", layer, p] = w_in_t[row:row + rows]
                row += rows
        else:
            send["w_ffn_in", layer] = w_in_t
        send["w_ffn_out", layer] = w_ffn_out[layer].astype(BF16)
    for layer in range(N_A):
        send["w_glu", layer] = w_glu[layer].astype(BF16)
        send["w_q", layer] = w_q[layer].astype(BF16)
        send["w_o", layer] = w_o[layer].astype(BF16)

    S = _Step(ready, send, _PLAN)
    S.land("start", _exchange(S.xfers("start"), name="start"))
    loss, dx, g = _step(x[0], loss_target[0], S)
    loss = lax.psum(loss[0, 0], MESH_AXES)

    g_d = g["ssm_d"].reshape(N_A, N_DEV, LANES).transpose(1, 0, 2).reshape(N_DEV, N_A * LANES)
    g_cw = jnp.stack([g["conv_w"][layer].reshape(N_DEV, 3, SHARD_COLS_FFN) for layer in range(DEPTH)], axis=1)
    g_small = jnp.concatenate([g_d, g_cw.reshape(N_DEV, -1)], axis=1)
    g_small = jnp.pad(g_small, ((0, 0), (0, _SMALL_ROWS * LANES - g_small.shape[1])))
    S.grad["small",] = g_small.reshape(N_DEV, _SMALL_ROWS, LANES)
    S.bcast["repl_late",] = _pack_rows(_repl_late(g), REPL_LATE_ROWS).astype(BF16)
    S.land("end", _exchange(S.xfers("end"), name="end"))

    res = {}
    for name, nl in (("w_glu", N_A), ("w_q", DEPTH - N_A), ("w_o", DEPTH - N_A), ("w_ffn_in", DEPTH),
                     ("w_ffn_out", DEPTH)):
        view = (lambda a: jnp.swapaxes(a, 1, 2)) if name == "w_ffn_in" else (lambda a: a)
        outs = _adamw_layers([S.slots[name, layer] for layer in range(nl)], view(wts[name]), view(mom[name]),
                             view(var[name]), name="adamw_" + name)
        res.update({(kind, name): view(a) for kind, a in zip(kinds, outs)})
    outs = _adamw_layers([S.slots["w_kvf",]], w_kvf, m_w_kvf, v_w_kvf, name="adamw_w_kvf")
    res.update({(kind, "w_kvf"): a for kind, a in zip(kinds, outs)})
    outs = _adamw_layers([S.slots["small",]], _pack_small(ssm_d, ffn_conv_w), _pack_small(m_ssm_d, m_ffn_conv_w),
                         _pack_small(v_ssm_d, v_ffn_conv_w), name="adamw_small")
    for kind, flat in zip(kinds, outs):
        res[kind, "ssm_d"], res[kind, "ffn_conv_w"] = _unpack_small(flat)

    pieces = {}
    for key, rows, sel in ((("repl_early",), REPL_EARLY_ROWS, _repl_early), (("repl_late",), REPL_LATE_ROWS, _repl_late)):
        outs = _adamw_layers([S.slots[key]], *[_pack_rows(sel(d), rows) for d in (wts, mom, var)],
                             name="adamw_" + key[0])
        for kind, flat in zip(kinds, outs):
            pieces[kind, key[0]] = _unpack_rows(flat, sel(wts))
    for kind in kinds:
        early, late = iter(pieces[kind, "repl_early"]), iter(pieces[kind, "repl_late"])
        take = lambda it, n: [next(it) for _ in range(n)]
        res[kind, "g_mix"] = jnp.stack(take(late, 1) + take(early, DEPTH - 1))
        res[kind, "g_ffn"] = jnp.stack(take(early, DEPTH))
        for n in S5_NAMES:
            res[kind, n] = jnp.stack([next(late), next(early)]).reshape(wts[n].shape)
        res[kind, "g_kv"], res[kind, "b_f"] = next(early), next(early)
        res[kind, "ffn_conv_b"] = jnp.stack(take(early, DEPTH))
        res[kind, "g_final"] = next(early)

    return (loss, dx[None], *[res[kind, n] for kind in kinds for n in _ORDER])
```

```python
import functools
import math

import jax
import jax.numpy as jnp
from jax import lax
from jax.experimental import pallas as pl
from jax.experimental.pallas import tpu as pltpu

F32 = jnp.float32
BF16 = jnp.bfloat16

D_MODEL = 1024
DEPTH = 4
N_A = 2
N_GROUPS = 64
SSM_GROUP = 16
SSM_STATE = 64
N_HEADS = 16
HEAD_DIM = 64
ATTN_SCALE = HEAD_DIM ** -0.5
D_FF = 2816
EPS = 1e-6
N_DEV = 8
LANES = 128
SUBLANES = 8

ADAM_LR = 0.001
ADAM_B1 = 0.9
ADAM_B2 = 0.999
ADAM_EPS = 1e-08
ADAM_WD = 0.01
ADAM_STEP = 10

ROW_TILE = 512
S5_CHUNK = 256
ATTN_TILE = 512
CUM_TILE = 256
NEG = -1e30

MESH_AXES = ("x", "y", "c")


def _tile(n, target, align=LANES):
    t = (min(target, n) // align) * align
    while t >= align:
        if n % t == 0:
            return t
        t -= align
    return n


def _params(*sem):
    return pltpu.CompilerParams(dimension_semantics=sem, vmem_limit_bytes=56 * 1024 * 1024)


_ANY = pl.BlockSpec(memory_space=pl.ANY)
_XFER_SEMS = (pltpu.SemaphoreType.DMA((N_DEV - 1,)), pltpu.SemaphoreType.DMA((N_DEV - 1,)), pltpu.SemaphoreType.DMA)


def _xfer_copies(x_ref, o_ref, send_sems, recv_sems, local_sem, scatter, row_off):
    xi, yi, ci = lax.axis_index("x"), lax.axis_index("y"), lax.axis_index("c")
    me = 4 * xi + 2 * yi + ci

    def src(p):
        return x_ref.at[p] if scatter else x_ref

    def dst(p):
        return o_ref.at[p] if row_off is None else o_ref.at[p, pl.ds(row_off, x_ref.shape[0])]

    own = pltpu.make_async_copy(src(me), dst(me), local_sem)
    sends, recvs = [], []
    for k in range(1, N_DEV):
        px, py, pc = xi ^ (k >> 2), yi ^ ((k >> 1) & 1), ci ^ (k & 1)
        p = 4 * px + 2 * py + pc
        sends.append(pltpu.make_async_remote_copy(
            src_ref=src(p), dst_ref=dst(me), send_sem=send_sems.at[k - 1], recv_sem=recv_sems.at[k - 1],
            device_id=(px, py, pc), device_id_type=pl.DeviceIdType.MESH))
        recvs.append(pltpu.make_async_remote_copy(
            src_ref=src(p), dst_ref=dst(p), send_sem=send_sems.at[k - 1], recv_sem=recv_sems.at[k - 1],
            device_id=(px, py, pc), device_id_type=pl.DeviceIdType.MESH))
    return own, sends, recvs


def _xfer_start(*refs, scatter, row_off):
    own, sends, _ = _xfer_copies(*refs, scatter, row_off)
    own.start()
    for cp in sends:
        cp.start()


def _xfer_wait(*refs, scatter, row_off):
    own, sends, recvs = _xfer_copies(*refs, scatter, row_off)
    for cp in recvs:
        cp.wait_recv()
    for cp in sends:
        cp.wait_send()
    own.wait()


def _pcall(body, args, *, grid, in_specs, out_specs, out_shape, scratch_shapes=(), sem, name, xfers=(), prefetch=()):
    out_specs, out_shape = list(out_specs), list(out_shape)
    xfers = [tuple(x) + (None,) * (3 - len(x)) for x in xfers]
    n_pre, n_in, n_out, n_x, n_scr = len(prefetch), len(in_specs), len(out_specs), len(xfers), len(scratch_shapes)
    flags = [(s, None if w is None else w[1]) for _, s, w in xfers]
    prevs = [(t, w[2]) for t, (_, _, w) in enumerate(xfers) if w is not None and w[2] is not None]
    n_b = len(prevs)
    assert not (prevs and prefetch)

    def xfer_shape(x, scatter, w):
        if w is not None:
            return jax.ShapeDtypeStruct((N_DEV, w[0]) + x.shape[1:], x.dtype)
        return jax.ShapeDtypeStruct((N_DEV,) + (x.shape[1:] if scatter else x.shape), x.dtype)

    def wrapped(*refs):
        pre, refs = refs[:n_pre], refs[n_pre:]
        ins, xin = refs[:n_in], refs[n_in:n_in + n_x]
        refs = refs[n_in + n_x + n_b:]
        outs, xout, scr = refs[:n_out], refs[n_out:n_out + n_x], refs[n_out + n_x:]
        own, sems = scr[:n_scr], scr[n_scr:]
        ids = [pl.program_id(d) for d in range(len(grid))]
        first = functools.reduce(jnp.logical_and, [i == 0 for i in ids])
        last = functools.reduce(jnp.logical_and, [i == g - 1 for i, g in zip(ids, grid)])

        @pl.when(first)
        def _():
            for t in range(n_x):
                _xfer_start(xin[t], xout[t], *sems[3 * t:3 * t + 3], scatter=flags[t][0], row_off=flags[t][1])

        body(*pre, *ins, *outs, *own)

        @pl.when(last)
        def _():
            for t in range(n_x):
                _xfer_wait(xin[t], xout[t], *sems[3 * t:3 * t + 3], scatter=flags[t][0], row_off=flags[t][1])

    grid_spec = pltpu.PrefetchScalarGridSpec(
        num_scalar_prefetch=n_pre, grid=grid, in_specs=list(in_specs) + [_ANY] * (n_x + n_b),
        out_specs=out_specs + [_ANY] * n_x, scratch_shapes=list(scratch_shapes) + list(_XFER_SEMS) * n_x)
    return pl.pallas_call(
        wrapped if xfers else body, grid_spec=grid_spec, out_shape=out_shape + [xfer_shape(*x) for x in xfers],
        input_output_aliases={n_in + n_x + b: n_out + t for b, (t, _) in enumerate(prevs)},
        compiler_params=_params(*(["arbitrary"] * len(grid) if xfers else sem)), name=name,
    )(*prefetch, *args, *[x[0] for x in xfers], *[p for _, p in prevs])


def _exchange(xfers, *, name):
    def body():
        pass

    return _pcall(body, (), grid=(1,), in_specs=[], out_specs=[], out_shape=[], sem=("arbitrary",), name=name,
                  xfers=xfers)


def _mm(a, b, *, ak="mk", bk="kn", ok="mn", add=None, out_dtype=F32, tm=1024, tn=1024, tk=1024, kg=1, name,
        xfers=()):
    sa, sb = a.shape, b.shape
    fm = fn = fk = None
    if ak == "mk":
        M, K, a_c = sa[0], sa[1], 1
    elif ak == "km":
        K, M, a_c = sa[0], sa[1], 0
    elif ak == "bmk":
        M, K, a_c, fk = sa[1], sa[0] * sa[2], 1, sa[2]
    else:
        K, M, a_c, fm = sa[1], sa[0] * sa[2], 0, sa[2]
    if bk == "kn":
        N, b_c = sb[1], 0
    elif bk == "nk":
        N, b_c = sb[0], 1
    elif bk == "bkn":
        N, b_c, fn = sb[0] * sb[2], 0, sb[2]
    elif bk == "bnk":
        N, b_c, fk = sb[1], 1, sb[2]
    elif bk == "kbn":
        N, b_c, fk = sb[2], 0, sb[1]
    else:
        N, b_c, fn = sb[0] * sb[1], 1, sb[1]
    tm, tn, tk = fm or _tile(M, tm), fn or _tile(N, tn), fk or _tile(K, tk)
    kblk = None if kg == 1 else kg
    nm, nn, nk = M // tm, N // tn, K // (tk * kg)

    a_spec = {"mk": pl.BlockSpec((tm, tk), lambda i, j, k: (i, k)),
              "km": pl.BlockSpec((tk, tm), lambda i, j, k: (k, i)),
              "bmk": pl.BlockSpec((kblk, tm, tk), lambda i, j, k: (k, i, 0)),
              "bkm": pl.BlockSpec((None, tk, tm), lambda i, j, k: (i, k, 0))}[ak]
    b_spec = {"kn": pl.BlockSpec((tk, tn), lambda i, j, k: (k, j)),
              "nk": pl.BlockSpec((tn, tk), lambda i, j, k: (j, k)),
              "bkn": pl.BlockSpec((None, tk, tn), lambda i, j, k: (j, k, 0)),
              "bnk": pl.BlockSpec((kblk, tn, tk), lambda i, j, k: (k, j, 0)),
              "kbn": pl.BlockSpec((kblk, tk, tn), lambda i, j, k: (k, 0, j)),
              "nbk": pl.BlockSpec((None, tn, tk), lambda i, j, k: (j, 0, k))}[bk]
    if ok == "mn":
        o_spec = pl.BlockSpec((tm, tn), lambda i, j, k: (i, j))
        out_shape = jax.ShapeDtypeStruct((M, N), out_dtype)
    elif ok == "bmn":
        o_spec = pl.BlockSpec((None, tm, tn), lambda i, j, k: (j, i, 0))
        out_shape = jax.ShapeDtypeStruct((nn, M, tn), out_dtype)
    else:
        o_spec = pl.BlockSpec((None, tm, tn), lambda i, j, k: (i, 0, j))
        out_shape = jax.ShapeDtypeStruct((nm, tm, N), out_dtype)
    dims = (((a_c,), (b_c,)), ((), ()))
    has_add = add is not None

    def body(*refs):
        a_ref, b_ref = refs[0], refs[1]
        add_ref = refs[2] if has_add else None
        o_ref = refs[3] if has_add else refs[2]
        if kg == 1:
            part = lax.dot_general(a_ref[...].astype(BF16), b_ref[...].astype(BF16), dims, preferred_element_type=F32)
        else:
            part = sum(lax.dot_general(a_ref[g].astype(BF16), b_ref[g].astype(BF16), dims,
                                       preferred_element_type=F32) for g in range(kg))

        def finish(r):
            if has_add:
                r = r + add_ref[...]
            o_ref[...] = r.astype(out_dtype)

        if nk == 1:
            finish(part)
            return
        acc = refs[-1]
        k = pl.program_id(2)

        @pl.when(k == 0)
        def _():
            acc[...] = part

        @pl.when(k > 0)
        def _():
            acc[...] += part

        @pl.when(k == nk - 1)
        def _():
            finish(acc[...])

    in_specs = [a_spec, b_spec]
    args = [a, b]
    if has_add:
        in_specs.append(pl.BlockSpec((tm, tn), lambda i, j, k: (i, j)))
        args.append(add)
    res = _pcall(body, args, grid=(nm, nn, nk), in_specs=in_specs, out_specs=[o_spec], out_shape=[out_shape],
                 scratch_shapes=[pltpu.VMEM((tm, tn), F32)] if nk > 1 else [],
                 sem=("parallel", "parallel", "arbitrary"), name=name, xfers=xfers)
    return res if xfers else res[0]


def _rms_fwd(h, g, *, name):
    L, D = h.shape
    tr = _tile(L, ROW_TILE, SUBLANES)

    def body(h_ref, g_ref, o_ref):
        x = h_ref[...]
        r = lax.rsqrt(jnp.mean(x * x, axis=1, keepdims=True) + EPS)
        o_ref[...] = (x * r * g_ref[...]).astype(BF16)

    return pl.pallas_call(
        body, grid=(L // tr,),
        in_specs=[pl.BlockSpec((tr, D), lambda i: (i, 0)), pl.BlockSpec((1, D), lambda i: (0, 0))],
        out_specs=pl.BlockSpec((tr, D), lambda i: (i, 0)), out_shape=jax.ShapeDtypeStruct((L, D), BF16),
        compiler_params=_params("parallel"), name=name)(h, g)


def _rms_bwd(h, g, dy, dres, *, name):
    L, D = h.shape
    tr = _tile(L, ROW_TILE, SUBLANES)

    def body(h_ref, g_ref, dy_ref, dres_ref, dh_ref, dg_ref):
        @pl.when(pl.program_id(0) == 0)
        def _():
            dg_ref[...] = jnp.zeros_like(dg_ref)

        x = h_ref[...]
        r = lax.rsqrt(jnp.mean(x * x, axis=1, keepdims=True) + EPS)
        xn = x * r
        dy = dy_ref[...].astype(F32)
        gdy = dy * g_ref[...]
        dx = r * (gdy - xn * jnp.mean(gdy * xn, axis=1, keepdims=True))
        dh_ref[...] = dres_ref[...] + dx
        dg_ref[...] += jnp.sum(dy * xn, axis=0, keepdims=True)

    row = pl.BlockSpec((tr, D), lambda i: (i, 0))
    vec = pl.BlockSpec((1, D), lambda i: (0, 0))
    return pl.pallas_call(
        body, grid=(L // tr,), in_specs=[row, vec, row, row], out_specs=[row, vec],
        out_shape=[jax.ShapeDtypeStruct((L, D), F32), jax.ShapeDtypeStruct((1, D), F32)],
        compiler_params=_params("arbitrary"), name=name)(h, g, dy, dres)


def _glu_res_rms(z, h, g, *, name):
    L, D = h.shape
    nb, cb = z.shape[1], z.shape[3]
    tr = _tile(L, ROW_TILE, SUBLANES)

    def body(z_ref, h_ref, g_ref, h1_ref, hn_ref):
        za = jnp.concatenate([z_ref[0, d] for d in range(nb)], axis=1)
        zg = jnp.concatenate([z_ref[1, d] for d in range(nb)], axis=1)
        x = h_ref[...] + za * jax.nn.sigmoid(zg)
        h1_ref[...] = x
        r = lax.rsqrt(jnp.mean(x * x, axis=1, keepdims=True) + EPS)
        hn_ref[...] = (x * r * g_ref[...]).astype(BF16)

    row = pl.BlockSpec((tr, D), lambda i: (i, 0))
    return pl.pallas_call(
        body, grid=(L // tr,),
        in_specs=[pl.BlockSpec((2, nb, tr, cb), lambda i: (0, 0, i, 0)), row, pl.BlockSpec((1, D), lambda i: (0, 0))],
        out_specs=[row, row],
        out_shape=[jax.ShapeDtypeStruct((L, D), F32), jax.ShapeDtypeStruct((L, D), BF16)],
        compiler_params=_params("parallel"), name=name)(z, h, g)


def _glu_bwd(z, dout, *, name):
    L, D = dout.shape
    nb, cb = z.shape[1], z.shape[3]
    tr = _tile(L, ROW_TILE, SUBLANES)

    def body(z_ref, d_ref, o_ref):
        for d in range(nb):
            dd = d_ref[:, d * cb:(d + 1) * cb]
            sg = jax.nn.sigmoid(z_ref[1, d])
            o_ref[0, d] = (dd * sg).astype(BF16)
            o_ref[1, d] = (dd * z_ref[0, d] * sg * (1.0 - sg)).astype(BF16)

    zs = pl.BlockSpec((2, nb, tr, cb), lambda i: (0, 0, i, 0))
    return pl.pallas_call(
        body, grid=(L // tr,), in_specs=[zs, pl.BlockSpec((tr, D), lambda i: (i, 0))], out_specs=zs,
        out_shape=jax.ShapeDtypeStruct(z.shape, BF16),
        compiler_params=_params("parallel"), name=name)(z, dout)


def _loss_head(h, g, target, *, name):
    L, D = h.shape
    tr = _tile(L, ROW_TILE, SUBLANES)

    def body(h_ref, g_ref, t_ref, loss_ref, dh_ref, dg_ref):
        @pl.when(pl.program_id(0) == 0)
        def _():
            dg_ref[...] = jnp.zeros_like(dg_ref)
            loss_ref[...] = jnp.zeros_like(loss_ref)

        x = h_ref[...]
        gg = g_ref[...]
        r = lax.rsqrt(jnp.mean(x * x, axis=1, keepdims=True) + EPS)
        xn = x * r
        err = xn * gg - t_ref[...]
        loss_ref[...] += 0.5 * jnp.sum(jnp.mean(err * err, axis=1, keepdims=True), axis=0, keepdims=True)
        dy = err * (1.0 / D)
        gdy = dy * gg
        dh_ref[...] = r * (gdy - xn * jnp.mean(gdy * xn, axis=1, keepdims=True))
        dg_ref[...] += jnp.sum(dy * xn, axis=0, keepdims=True)

    row = pl.BlockSpec((tr, D), lambda i: (i, 0))
    vec = pl.BlockSpec((1, D), lambda i: (0, 0))
    return pl.pallas_call(
        body, grid=(L // tr,), in_specs=[row, vec, row],
        out_specs=[pl.BlockSpec((1, 1), lambda i: (0, 0)), row, vec],
        out_shape=[jax.ShapeDtypeStruct((1, 1), F32), jax.ShapeDtypeStruct((L, D), F32),
                   jax.ShapeDtypeStruct((1, D), F32)],
        compiler_params=_params("arbitrary"), name=name)(h, g, target)


CONV_ROW_TILE = 256


def _sigmoid(x):
    return pl.reciprocal(1.0 + jnp.exp(-x), approx=True)


def _conv_specs(L, tr, tc):
    nrb = tr // SUBLANES
    before = lambda i: jnp.maximum(i * nrb - 1, 0)
    after = lambda i: jnp.minimum((i + 1) * nrb, L // SUBLANES - 1)
    main = pl.BlockSpec((2, None, tr, tc), lambda j, i: (0, j, i, 0))
    prev = pl.BlockSpec((2, None, SUBLANES, tc), lambda j, i: (0, j, before(i), 0))
    nxt = pl.BlockSpec((2, None, SUBLANES, tc), lambda j, i: (0, j, after(i), 0))
    cw = pl.BlockSpec((2, None, 3, tc), lambda j, i: (0, j, 0, 0))
    cb = pl.BlockSpec((2, None, 1, tc), lambda j, i: (0, j, 0, 0))
    half = pl.BlockSpec((None, tr, tc), lambda j, i: (j, i, 0))
    half_nxt = pl.BlockSpec((None, SUBLANES, tc), lambda j, i: (j, after(i), 0))
    return main, prev, nxt, cw, cb, half, half_nxt


def _conv_rows(xe, w, b):
    x1 = pltpu.roll(xe, 1, 0)
    x2 = pltpu.roll(xe, 2, 0)
    return b + x2 * w[0:1] + x1 * w[1:2] + xe * w[2:3], x1, x2


def _shift_down(x, halo, k, row):
    y = pltpu.roll(x, k, 0)
    for r in range(k):
        y = jnp.where(row == r, halo[SUBLANES - k + r:SUBLANES - k + r + 1, :], y)
    return y


def _conv_act(u0, cw, cb, *, name, xfers=()):
    _, nb, L, tc = u0.shape
    tr = _tile(L, ROW_TILE, SUBLANES)
    main, prev, _, cws, cbs, half, _ = _conv_specs(L, tr, tc)

    def body(u_ref, p_ref, w_ref, b_ref, a_ref):
        first = pl.program_id(1) == 0
        row = lax.broadcasted_iota(jnp.int32, (tr, tc), 0)
        y = []
        for s in range(2):
            x, w = u_ref[s], w_ref[s]
            halo = jnp.where(first, 0.0, p_ref[s])
            x1 = _shift_down(x, halo, 1, row)
            x2 = _shift_down(x, halo, 2, row)
            y.append(b_ref[s] + x2 * w[0:1] + x1 * w[1:2] + x * w[2:3])
        a_ref[...] = (y[0] * _sigmoid(y[0]) * y[1]).astype(BF16)

    return _pcall(body, (u0, u0, cw, cb), grid=(nb, L // tr), in_specs=[main, prev, cws, cbs], out_specs=[half],
                  out_shape=[jax.ShapeDtypeStruct((nb, L, tc), BF16)], sem=("parallel", "parallel"), name=name,
                  xfers=xfers)


def _conv_ffn_bwd(u0, cw, cb, da, *, name, xfers=()):
    _, nb, L, tc = u0.shape
    tr = _tile(L, CONV_ROW_TILE, SUBLANES)
    main, prev, nxt, cws, cbs, half, half_nxt = _conv_specs(L, tr, tc)
    nr = L // tr
    H = SUBLANES

    def body(u_ref, p_ref, n_ref, w_ref, b_ref, da_ref, dan_ref, a_ref, du0_ref, dcw_ref, dcb_ref):
        i = pl.program_id(1)

        @pl.when(i == 0)
        def _():
            dcw_ref[...] = jnp.zeros_like(dcw_ref)
            dcb_ref[...] = jnp.zeros_like(dcb_ref)

        y, x1, x2 = [], [], []
        for s in range(2):
            xe = jnp.concatenate([jnp.where(i == 0, 0.0, p_ref[s]), u_ref[s], n_ref[s]], axis=0)
            ys, x1s, x2s = _conv_rows(xe, w_ref[s], b_ref[s])
            y.append(ys[H:])
            x1.append(x1s[H:H + tr])
            x2.append(x2s[H:H + tr])
        gate, up = y
        da = jnp.concatenate([da_ref[...], dan_ref[...]], axis=0)
        row = lax.broadcasted_iota(jnp.int32, (tr + H, tc), 0)
        da = jnp.where(jnp.logical_and(i == nr - 1, row >= tr), 0.0, da)
        sg = _sigmoid(gate)
        silu = gate * sg
        a_ref[...] = (silu * up)[:tr].astype(BF16)
        d = (da * up * (sg * (1.0 + gate * (1.0 - sg))), da * silu)
        for s in range(2):
            w = w_ref[s]
            d0 = d[s][:tr]
            d1 = pltpu.roll(d[s], tr + H - 1, 0)[:tr]
            d2 = pltpu.roll(d[s], tr + H - 2, 0)[:tr]
            du0_ref[s] = (d0 * w[2:3] + d1 * w[1:2] + d2 * w[0:1]).astype(BF16)
            dcw_ref[s, 0:1, :] += jnp.sum(d0 * x2[s], axis=0, keepdims=True)
            dcw_ref[s, 1:2, :] += jnp.sum(d0 * x1[s], axis=0, keepdims=True)
            dcw_ref[s, 2:3, :] += jnp.sum(d0 * u_ref[s], axis=0, keepdims=True)
            dcb_ref[s] += jnp.sum(d0, axis=0, keepdims=True)

    return _pcall(body, (u0, u0, u0, cw, cb, da, da), grid=(nb, nr),
                  in_specs=[main, prev, nxt, cws, cbs, half, half_nxt], out_specs=[half, main, cws, cbs],
                  out_shape=[jax.ShapeDtypeStruct((nb, L, tc), BF16), jax.ShapeDtypeStruct((2, nb, L, tc), BF16),
                             jax.ShapeDtypeStruct((2, nb, 3, tc), F32), jax.ShapeDtypeStruct((2, nb, 1, tc), F32)],
                  sem=("parallel", "arbitrary"), name=name, xfers=xfers)


N_TILES = 64
HALF = N_TILES // 2


def _swap(s):
    return jnp.concatenate([s[HALF:], s[:HALF]], axis=0)


def _chan_block(j):
    return ((j % HALF) // 4) * LANES


def _pairs_of(jb):
    return [2 * jb, 2 * jb + 1, HALF // 2 + 2 * jb, HALF // 2 + 2 * jb + 1]


_WB_SPEC = pl.BlockSpec((HALF, LANES, 2 * LANES), lambda c: (0, 0, 0))
_WC_SPEC = _WB_SPEC
GELU_C = math.sqrt(2.0 / math.pi)
GELU_A = 0.044715


def _gelu(x):
    return 0.5 * x * (1.0 + jnp.tanh(GELU_C * (x + GELU_A * x * x * x)))


def _gelu_grad(x):
    th = jnp.tanh(GELU_C * (x + GELU_A * x * x * x))
    return 0.5 * (1.0 + th) + 0.5 * x * (1.0 - th * th) * GELU_C * (1.0 + 3.0 * GELU_A * x * x)


def _tile_rows(j, T, TP):
    return pl.ds(j * TP + SUBLANES, T)


def _pair(ref, jp, T, TP):
    return jnp.concatenate([ref[_tile_rows(2 * jp, T, TP), :], ref[_tile_rows(2 * jp + 1, T, TP), :]],
                           axis=1).astype(BF16)


def _unpair(ref, jp, val, T, TP):
    ref[_tile_rows(2 * jp, T, TP), :] = val[:, :LANES]
    ref[_tile_rows(2 * jp + 1, T, TP), :] = val[:, LANES:]


def _s5_project_in(u_ref, wb_ref, s3, T, TP):
    for jp in range(HALF):
        blk = _chan_block(2 * jp)
        _unpair(s3, jp, jnp.dot(u_ref[:, blk:blk + LANES], wb_ref[jp], preferred_element_type=F32), T, TP)


def _s5_scan_fwd(s3, a1, a2, s0, T, TP):
    span = (N_GROUPS - 1) * TP + 2 * SUBLANES

    def blk(i, s):
        view = s3.at[pl.ds(pl.multiple_of(i * SUBLANES, SUBLANES), span)]
        for k in range(SUBLANES):
            rows = pl.ds(SUBLANES + k, N_GROUPS, stride=TP)
            s = a1 * s + a2 * _swap(s) + view[rows, :]
            view[rows, :] = s
        return s

    return lax.fori_loop(0, T // SUBLANES, blk, s0)


def _s5_fwd(hn, wb, wc, a1, a2, dvec, *, name, xfers=()):
    L, D = hn.shape
    T = min(S5_CHUNK, L)
    TP = T + SUBLANES
    nC = L // T

    def body(u_ref, wb_ref, wc_ref, a1_ref, a2_ref, d_ref, y_ref, yg_ref, sb_ref, s3, st):
        @pl.when(pl.program_id(0) == 0)
        def _():
            st[...] = jnp.zeros_like(st)

        sb_ref[0] = st[...]
        _s5_project_in(u_ref, wb_ref, s3, T, TP)
        st[...] = _s5_scan_fwd(s3, a1_ref[...], a2_ref[...], st[...], T, TP)
        for jb in range(D // LANES):
            acc = jnp.zeros((T, LANES), F32)
            for jp in _pairs_of(jb):
                acc += lax.dot_general(_pair(s3, jp, T, TP), wc_ref[jp], _NT, preferred_element_type=F32)
            cols = slice(jb * LANES, (jb + 1) * LANES)
            y = acc + d_ref[:, cols] * u_ref[:, cols].astype(F32)
            y_ref[:, cols] = y
            yg_ref[:, cols] = _gelu(y).astype(BF16)

    row = pl.BlockSpec((T, D), lambda c: (c, 0))
    aspec = pl.BlockSpec((N_GROUPS, LANES), lambda c: (0, 0))
    return _pcall(
        body, (hn, wb, wc, a1, a2, dvec), grid=(nC,),
        in_specs=[row, _WB_SPEC, _WC_SPEC, aspec, aspec, pl.BlockSpec((1, D), lambda c: (0, 0))],
        out_specs=[row, row, pl.BlockSpec((1, N_GROUPS, LANES), lambda c: (c, 0, 0))],
        out_shape=[jax.ShapeDtypeStruct((L, D), F32), jax.ShapeDtypeStruct((L, D), BF16),
                   jax.ShapeDtypeStruct((nC, N_GROUPS, LANES), F32)],
        scratch_shapes=[pltpu.VMEM((N_GROUPS * TP, LANES), F32), pltpu.VMEM((N_GROUPS, LANES), F32)],
        sem=("arbitrary",), name=name, xfers=xfers)


def _s5_bwd(hn, dyg, ypre, sbound, wb, wc, a1, a2, dvec, *, name, xfers=()):
    L, D = hn.shape
    T = min(S5_CHUNK, L)
    TP = T + SUBLANES
    nC = L // T
    span = (N_GROUPS - 1) * TP + 2 * SUBLANES
    NT = (((1,), (1,)), ((), ()))
    TN = (((0,), (0,)), ((), ()))

    def body(u_ref, dyg_ref, yp_ref, sb_ref, wb_ref, wc_ref, a1_ref, a2_ref, d_ref,
             du_ref, dwb_ref, dwc_ref, da1_ref, da2_ref, dd_ref, s3, g3, gst, dy_s):
        @pl.when(pl.program_id(0) == 0)
        def _():
            gst[...] = jnp.zeros_like(gst)
            dwb_ref[...] = jnp.zeros_like(dwb_ref)
            dwc_ref[...] = jnp.zeros_like(dwc_ref)
            da1_ref[...] = jnp.zeros_like(da1_ref)
            da2_ref[...] = jnp.zeros_like(da2_ref)
            dd_ref[...] = jnp.zeros_like(dd_ref)

        a1 = a1_ref[...]
        a2 = a2_ref[...]
        dy = dyg_ref[...].astype(F32) * _gelu_grad(yp_ref[...])
        dy_s[...] = dy.astype(BF16)
        dd_ref[...] += jnp.sum(dy * u_ref[...].astype(F32), axis=0, keepdims=True)
        du_ref[...] = d_ref[...] * dy

        s3[pl.ds(SUBLANES - 1, N_GROUPS, stride=TP), :] = sb_ref[0]
        _s5_project_in(u_ref, wb_ref, s3, T, TP)
        _s5_scan_fwd(s3, a1, a2, sb_ref[0], T, TP)

        for jp in range(HALF):
            blk = _chan_block(2 * jp)
            _unpair(g3, jp, jnp.dot(dy_s[:, blk:blk + LANES], wc_ref[jp], preferred_element_type=F32), T, TP)
        a2c = -a2

        def rblk(ii, carry):
            g, acc1, acc2 = carry
            t0 = pl.multiple_of((T // SUBLANES - 1 - ii) * SUBLANES, SUBLANES)
            gv = g3.at[pl.ds(t0, span)]
            sv = s3.at[pl.ds(t0, span)]
            for k in reversed(range(SUBLANES)):
                rows = pl.ds(SUBLANES + k, N_GROUPS, stride=TP)
                g = a1 * g + a2c * _swap(g) + gv[rows, :]
                gv[rows, :] = g
                sp = sv[pl.ds(SUBLANES - 1 + k, N_GROUPS, stride=TP), :]
                acc1 = acc1 + g * sp
                acc2 = acc2 + g * _swap(sp)
            return g, acc1, acc2

        zero = jnp.zeros((N_GROUPS, LANES), F32)
        g, acc1, acc2 = lax.fori_loop(0, T // SUBLANES, rblk, (gst[...], zero, zero))
        gst[...] = g
        da1_ref[...] += acc1
        da2_ref[...] += acc2

        for jb in range(D // LANES):
            cols = slice(jb * LANES, (jb + 1) * LANES)
            acc = jnp.zeros((T, LANES), F32)
            for jp in _pairs_of(jb):
                gp = _pair(g3, jp, T, TP)
                dwc_ref[jp] += lax.dot_general(dy_s[:, cols], _pair(s3, jp, T, TP), TN, preferred_element_type=F32)
                dwb_ref[jp] += lax.dot_general(u_ref[:, cols], gp, TN, preferred_element_type=F32)
                acc += lax.dot_general(gp, wb_ref[jp], NT, preferred_element_type=F32)
            du_ref[:, cols] += acc

    rrow = pl.BlockSpec((T, D), lambda c: (nC - 1 - c, 0))
    aspec = pl.BlockSpec((N_GROUPS, LANES), lambda c: (0, 0))
    vec = pl.BlockSpec((1, D), lambda c: (0, 0))
    return _pcall(
        body, (hn, dyg, ypre, sbound, wb, wc, a1, a2, dvec), grid=(nC,),
        in_specs=[rrow, rrow, rrow, pl.BlockSpec((1, N_GROUPS, LANES), lambda c: (nC - 1 - c, 0, 0)),
                  _WB_SPEC, _WC_SPEC, aspec, aspec, vec],
        out_specs=[rrow, _WB_SPEC, _WC_SPEC, aspec, aspec, vec],
        out_shape=[jax.ShapeDtypeStruct((L, D), F32),
                   jax.ShapeDtypeStruct((HALF, LANES, 2 * LANES), F32),
                   jax.ShapeDtypeStruct((HALF, LANES, 2 * LANES), F32),
                   jax.ShapeDtypeStruct((N_GROUPS, LANES), F32), jax.ShapeDtypeStruct((N_GROUPS, LANES), F32),
                   jax.ShapeDtypeStruct((1, D), F32)],
        scratch_shapes=[pltpu.VMEM((N_GROUPS * TP, LANES), F32), pltpu.VMEM((N_GROUPS * TP, LANES), F32),
                        pltpu.VMEM((N_GROUPS, LANES), F32), pltpu.VMEM((T, D), BF16)],
        sem=("arbitrary",), name=name, xfers=xfers)


def _s5_prep(lam_re, lam_im, log_dt, b_re, b_im, c_re, c_im):
    dt = jnp.exp(log_dt)[:, None]
    mag = jnp.exp(lam_re * dt)
    lb_re = mag * jnp.cos(lam_im * dt)
    lb_im = mag * jnp.sin(lam_im * dt)
    den = lam_re * lam_re + lam_im * lam_im
    nr = lb_re - 1.0
    fr = ((nr * lam_re + lb_im * lam_im) / den)[..., None]
    fi = ((lb_im * lam_re - nr * lam_im) / den)[..., None]
    bb_re = fr * b_re - fi * b_im
    bb_im = fr * b_im + fi * b_re
    pair = lambda a: a.reshape(HALF, 2 * SSM_STATE)
    a1 = jnp.concatenate([pair(lb_re), pair(lb_re)], axis=0)
    a2 = jnp.concatenate([-pair(lb_im), pair(lb_im)], axis=0)

    quads = N_GROUPS // 4
    rows, cols = 4 * SSM_GROUP, 4 * SSM_STATE
    diag = (jnp.arange(rows)[:, None] // SSM_GROUP == jnp.arange(cols)[None, :] // SSM_STATE).astype(F32)
    place = jnp.eye(2, dtype=F32)

    def expand(w):
        blocks = jnp.tile(w.reshape(quads, rows, SSM_STATE), (1, 1, 4)) * diag
        return jnp.einsum('kq,gkrl->gkqrl', place, blocks.reshape(quads // 2, 2, rows, cols)
                          ).reshape(quads, LANES, cols)

    wb = jnp.concatenate([expand(bb_re.transpose(0, 2, 1)), expand(bb_im.transpose(0, 2, 1))], axis=0)
    wc = jnp.concatenate([expand(c_re), expand(-c_im)], axis=0)
    return a1, a2, wb, wc


def _tri(n, upper):
    r = lax.broadcasted_iota(jnp.int32, (n, n), 0)
    c = lax.broadcasted_iota(jnp.int32, (n, n), 1)
    return ((r <= c) if upper else (r >= c)).astype(F32)


def _fgate_fwd(fl, bf, *, name):
    L, W = fl.shape
    tr = _tile(L, CUM_TILE, SUBLANES)

    def body(f_ref, b_ref, o_ref, carry):
        @pl.when(pl.program_id(0) == 0)
        def _():
            carry[...] = jnp.zeros_like(carry)

        x = f_ref[...] + b_ref[...]
        ls = jnp.minimum(x, 0.0) - jnp.log(1.0 + jnp.exp(-jnp.abs(x)))
        cum = jnp.dot(_tri(tr, False), ls, preferred_element_type=F32, precision=lax.Precision.HIGHEST) + carry[...]
        o_ref[...] = cum
        carry[...] = cum[tr - 1:tr, :]

    return pl.pallas_call(
        body, grid=(L // tr,),
        in_specs=[pl.BlockSpec((tr, W), lambda i: (i, 0)), pl.BlockSpec((1, W), lambda i: (0, 0))],
        out_specs=pl.BlockSpec((tr, W), lambda i: (i, 0)), out_shape=jax.ShapeDtypeStruct((L, W), F32),
        scratch_shapes=[pltpu.VMEM((1, W), F32)], compiler_params=_params("arbitrary"), name=name)(fl, bf)


def _fgate_bwd(fl, bf, dcum, *, name):
    L, W = fl.shape
    tr = _tile(L, CUM_TILE, SUBLANES)
    n = L // tr

    def body(f_ref, b_ref, d_ref, o_ref, db_ref, carry):
        @pl.when(pl.program_id(0) == 0)
        def _():
            carry[...] = jnp.zeros_like(carry)
            db_ref[...] = jnp.zeros_like(db_ref)

        d = d_ref[...]
        rev = jnp.dot(_tri(tr, True), d, preferred_element_type=F32, precision=lax.Precision.HIGHEST) + carry[...]
        carry[...] += jnp.sum(d, axis=0, keepdims=True)
        df = rev * jax.nn.sigmoid(-(f_ref[...] + b_ref[...]))
        o_ref[...] = df
        db_ref[...] += jnp.sum(df, axis=0, keepdims=True)

    rrow = pl.BlockSpec((tr, W), lambda i: (n - 1 - i, 0))
    vec = pl.BlockSpec((1, W), lambda i: (0, 0))
    return pl.pallas_call(
        body, grid=(n,), in_specs=[rrow, vec, rrow], out_specs=[rrow, vec],
        out_shape=[jax.ShapeDtypeStruct((L, W), F32), jax.ShapeDtypeStruct((1, W), F32)],
        scratch_shapes=[pltpu.VMEM((1, W), F32)], compiler_params=_params("arbitrary"), name=name)(fl, bf, dcum)


_NT = (((1,), (1,)), ((), ()))
_TN = (((0,), (0,)), ((), ()))
HEAD_PAIRS = N_HEADS // 2


def _causal_tiles(n, by_row):
    pairs = ([(i, j) for i in range(n) for j in range(i + 1)] if by_row
             else [(i, j) for j in range(n) for i in range(j, n)])
    return (jnp.array([p[0] for p in pairs], jnp.int32), jnp.array([p[1] for p in pairs], jnp.int32))


def _attn_logits(qs, k, ck, masked, t):
    s = lax.dot_general(qs, k, _NT, preferred_element_type=F32) - ck
    if masked:
        r = lax.broadcasted_iota(jnp.int32, (t, t), 0)
        c = lax.broadcasted_iota(jnp.int32, (t, t), 1)
        s = jnp.where(c > r, NEG, s)
    return s


def _attn_fwd(q, kv, ck, *, name, xfers=()):
    L, D = q.shape
    t = _tile(L, ATTN_TILE)
    n = L // t
    dh = HEAD_DIM

    def body(q_ref, k_ref, v_ref, ck_ref, o_ref, o32_ref, lse_ref, m_s, l_s, acc):
        i, j = pl.program_id(1), pl.program_id(2)

        @pl.when(j == 0)
        def _():
            m_s[...] = jnp.full_like(m_s, NEG)
            l_s[...] = jnp.zeros_like(l_s)
            acc[...] = jnp.zeros_like(acc)

        def tile(masked):
            for e in range(2):
                sl = slice(e * dh, (e + 1) * dh)
                v = v_ref[:, sl]
                s = _attn_logits(q_ref[:, sl] * ATTN_SCALE, k_ref[:, sl], ck_ref[e], masked, t)
                m_new = jnp.maximum(m_s[e], jnp.max(s, axis=1, keepdims=True))
                alpha = jnp.exp(m_s[e] - m_new)
                p = jnp.exp(s - m_new)
                l_s[e] = alpha * l_s[e] + jnp.sum(p, axis=1, keepdims=True)
                p_hi = p.astype(BF16)
                p_lo = (p - p_hi.astype(F32)).astype(BF16)
                pv = (jnp.dot(p_hi, v, preferred_element_type=F32) + jnp.dot(p_lo, v, preferred_element_type=F32))
                acc[e] = alpha * acc[e] + pv
                m_s[e] = m_new

        pl.when(j < i)(functools.partial(tile, False))
        pl.when(j == i)(functools.partial(tile, True))

        @pl.when(j == n - 1)
        def _():
            for e in range(2):
                sl = slice(e * dh, (e + 1) * dh)
                o = acc[e] / l_s[e]
                o_ref[:, sl] = o.astype(BF16)
                o32_ref[:, sl] = o
                lse_ref[e] = m_s[e] + jnp.log(l_s[e])

    qs = pl.BlockSpec((t, LANES), lambda h, i, j: (i, h))
    ks = pl.BlockSpec((t, LANES), lambda h, i, j: (jnp.minimum(i, j), h))
    vs = pl.BlockSpec((t, LANES), lambda h, i, j: (jnp.minimum(i, j), HEAD_PAIRS + h))
    cs = pl.BlockSpec((2, 1, t), lambda h, i, j: (h, 0, jnp.minimum(i, j)))
    return _pcall(
        body, (q, kv, kv, ck), grid=(HEAD_PAIRS, n, n), in_specs=[qs, ks, vs, cs],
        out_specs=[qs, qs, pl.BlockSpec((2, t, 1), lambda h, i, j: (h, i, 0))],
        out_shape=[jax.ShapeDtypeStruct((L, D), BF16), jax.ShapeDtypeStruct((L, D), F32),
                   jax.ShapeDtypeStruct((N_HEADS, L, 1), F32)],
        scratch_shapes=[pltpu.VMEM((2, t, 1), F32), pltpu.VMEM((2, t, 1), F32), pltpu.VMEM((2, t, dh), F32)],
        sem=("parallel", "parallel", "arbitrary"), name=name, xfers=xfers)


def _attn_delta(do, o, *, name):
    L, D = do.shape
    t = _tile(L, ATTN_TILE)

    def body(do_ref, o_ref, d_ref):
        prod = do_ref[...].astype(F32) * o_ref[...]
        head = lax.broadcasted_iota(jnp.int32, (N_HEADS, D), 0)
        col = lax.broadcasted_iota(jnp.int32, (N_HEADS, D), 1)
        sel = (col // HEAD_DIM == head).astype(F32)
        d_ref[:, 0, :] = lax.dot_general(sel, prod, _NT, preferred_element_type=F32, precision=lax.Precision.HIGHEST)

    row = pl.BlockSpec((t, D), lambda i: (i, 0))
    return pl.pallas_call(
        body, grid=(L // t,), in_specs=[row, row], out_specs=pl.BlockSpec((N_HEADS, 1, t), lambda i: (0, 0, i)),
        out_shape=jax.ShapeDtypeStruct((N_HEADS, 1, L), F32), compiler_params=_params("parallel"), name=name)(do, o)


def _attn_bwd(q, kv, ck_col, do, lse_row, delta_row, *, name, xfers=()):
    L, D = q.shape
    t = _tile(L, ATTN_TILE)
    n = L // t
    dh = HEAD_DIM

    def body(i_tab, j_tab, q_ref, k_ref, v_ref, ck_ref, do_ref, lse_ref, dl_ref, dq_ref, dk_ref, dv_ref, dck_ref):
        i, j = i_tab[pl.program_id(1)], j_tab[pl.program_id(1)]

        @pl.when(pl.program_id(1) == 0)
        def _():
            dq_ref[...] = jnp.zeros_like(dq_ref)

        @pl.when(i == j)
        def _():
            dk_ref[...] = jnp.zeros_like(dk_ref)
            dv_ref[...] = jnp.zeros_like(dv_ref)
            dck_ref[...] = jnp.zeros_like(dck_ref)

        def tile(masked):
            cols = pl.ds(pl.multiple_of(i * t, t), t)
            for e in range(2):
                sl = slice(e * dh, (e + 1) * dh)
                qs = q_ref[:, sl] * ATTN_SCALE
                k = k_ref[:, sl]
                do = do_ref[:, sl]
                st = lax.dot_general(k, qs, _NT, preferred_element_type=F32) - ck_ref[e]
                if masked:
                    kpos = lax.broadcasted_iota(jnp.int32, (t, t), 0)
                    qpos = lax.broadcasted_iota(jnp.int32, (t, t), 1)
                    st = jnp.where(kpos > qpos, NEG, st)
                pt = jnp.exp(st - lse_ref[e])
                dpt = lax.dot_general(v_ref[:, sl], do, _NT, preferred_element_type=F32)
                dst = pt * (dpt - dl_ref[e])
                dst16 = dst.astype(BF16)
                dv_ref[:, sl] += jnp.dot(pt.astype(BF16), do, preferred_element_type=F32)
                dk_ref[:, sl] += jnp.dot(dst16, qs, preferred_element_type=F32)
                dck_ref[e] -= jnp.sum(dst, axis=1, keepdims=True)
                dq_ref[sl, cols] += lax.dot_general(k, dst16, _TN, preferred_element_type=F32) * ATTN_SCALE

        pl.when(i > j)(functools.partial(tile, False))
        pl.when(i == j)(functools.partial(tile, True))

    qs = pl.BlockSpec((t, LANES), lambda h, s, it, jt: (it[s], h))
    ks = pl.BlockSpec((t, LANES), lambda h, s, it, jt: (jt[s], h))
    vs = pl.BlockSpec((t, LANES), lambda h, s, it, jt: (jt[s], HEAD_PAIRS + h))
    cs = pl.BlockSpec((2, t, 1), lambda h, s, it, jt: (h, jt[s], 0))
    ls = pl.BlockSpec((2, 1, t), lambda h, s, it, jt: (h, 0, it[s]))
    full = jax.ShapeDtypeStruct((L, D), F32)
    tabs = _causal_tiles(n, by_row=False)
    return _pcall(
        body, (q, kv, kv, ck_col, do, lse_row, delta_row), grid=(HEAD_PAIRS, tabs[0].shape[0]),
        in_specs=[qs, ks, vs, cs, qs, ls, ls],
        out_specs=[pl.BlockSpec((LANES, L), lambda h, s, it, jt: (h, 0)), ks, ks, cs],
        out_shape=[jax.ShapeDtypeStruct((D, L), F32), full, full, jax.ShapeDtypeStruct((N_HEADS, L, 1), F32)],
        sem=("parallel", "arbitrary"), name=name, xfers=xfers, prefetch=tabs)


SHARD_COLS_FFN = 2 * D_FF // N_DEV
SHARD_ROWS_FFN = D_FF // N_DEV
SHARD_COLS_GLU = 2 * D_MODEL // N_DEV
SHARD_ROWS_QO = D_MODEL // N_DEV
SHARD_COLS_KVF = (2 * D_MODEL + N_HEADS) // N_DEV
S5_NAMES = ("lam_re", "lam_im", "log_dt", "ssm_b_re", "ssm_b_im", "ssm_c_re", "ssm_c_im")
REPL_LATE_ROWS = 288
REPL_EARLY_ROWS = 320


def _leaves(parts):
    out = []
    for p in parts:
        out.extend(_leaves(p) if isinstance(p, (list, tuple)) else [p.reshape(-1)])
    return out


def _pack_rows(parts, rows):
    flat = jnp.concatenate(_leaves(parts))
    return jnp.pad(flat, (0, rows * D_MODEL - flat.shape[0])).reshape(rows, D_MODEL)


def _unpack_rows(flat, like):
    flat, out, off = flat.reshape(-1), [], 0
    for p in _leaves(like):
        out.append(flat[off:off + p.shape[0]])
        off += p.shape[0]
    return out


def _repl_late(d):
    return [d["g_mix"][0], [d[n][0] for n in S5_NAMES]]


def _repl_early(d):
    return [list(d["g_mix"][1:]), list(d["g_ffn"]), [d[n][1] for n in S5_NAMES], d["g_kv"], d["b_f"],
            list(d["ffn_conv_b"]), d["g_final"]]


def _kvf_blocks(full):
    return full.reshape(D_MODEL, N_DEV, SHARD_COLS_KVF).transpose(1, 0, 2)


class _Step:
    def __init__(self, weights, send=None, plan=None):
        self.w = dict(weights)
        self.send = send or {}
        self.grad = {}
        self.bcast = {}
        self.slots = {}
        self.plan = plan or {}

    def xfers(self, host):
        src = {"w": self.send, "g": self.grad, "b": self.bcast}
        out = []
        for kind, k in self.plan.get(host, ()):
            if kind == "w" and k[0] == "w_ffn_in" and len(k) == 3:
                parts = _W_IN_PARTS[k[1]]
                out.append((self.send[k], False, (sum(parts), sum(parts[:k[2]]), self.w.get(("w_ffn_in_parts", k[1])))))
            else:
                out.append((src[kind][k], kind == "g"))
        return out

    def land(self, host, gathered):
        for (kind, k), g in zip(self.plan.get(host, ()), gathered):
            if kind == "w":
                self.arrive(k, g)
            else:
                self.slots[k] = g

    def arrive(self, k, g):
        name = k[0]
        if name == "w_ffn_in" and len(k) == 3:
            self.w["w_ffn_in_parts", k[1]] = g
            if k[2] == len(_W_IN_PARTS[k[1]]) - 1:
                self.w[name, k[1]] = g
        elif name == "w_ffn_out":
            self.w[k] = g.reshape(4, 2 * SHARD_ROWS_FFN, D_MODEL)
        elif name in ("w_q", "w_o"):
            self.w[k] = g.reshape(D_MODEL, D_MODEL)
        elif name == "w_kvf":
            full = g.transpose(1, 0, 2).reshape(D_MODEL, N_DEV * SHARD_COLS_KVF)
            self.w["w_kv",] = full[:, :2 * D_MODEL]
            self.w["w_f",] = jnp.pad(full[:, 2 * D_MODEL:], ((0, 0), (0, LANES - N_HEADS)))
        elif name == "small":
            flat = g.reshape(N_DEV, -1)
            self.w["ssm_d",] = flat[:, :256].reshape(N_DEV, N_A, LANES).transpose(1, 0, 2).reshape(N_A, D_MODEL)
            cw = flat[:, 256:256 + DEPTH * 3 * SHARD_COLS_FFN].reshape(N_DEV, DEPTH, 3, SHARD_COLS_FFN)
            for layer in range(DEPTH):
                self.w["conv_w", layer] = cw[:, layer].reshape(2, 4, 3, SHARD_COLS_FFN)
        else:
            self.w[k] = g

    def run(self, host, fn, *args, **kw):
        xf = self.xfers(host)
        res = fn(*args, name=host, xfers=xf, **kw)
        if not xf:
            return res[0] if isinstance(res, (list, tuple)) and len(res) == 1 else res
        n_own = len(res) - len(xf)
        self.land(host, res[n_own:])
        return res[0] if n_own == 1 else res[:n_own]


def _step(x, target, S):
    L = x.shape[0]
    W = S.w
    vec = lambda a: a.reshape(1, -1)
    CF = SHARD_COLS_FFN

    h = x
    saved = []
    kvs = None
    for layer in range(DEPTH):
        t = str(layer)
        if layer < N_A:
            (a1, a2, wb, wc), prep_vjp = jax.vjp(_s5_prep, *[W[n][layer] for n in S5_NAMES])
            wb16, wc16 = wb.astype(BF16), wc.astype(BF16)
            hn = _rms_fwd(h, vec(W["g_mix"][layer]), name="mix_norm" + t)
            dvec = vec(W["ssm_d",][layer])
            ypre, yg, sb = S.run("s5_fwd" + t, _s5_fwd, hn, wb16, wc16, a1, a2, dvec)
            z = S.run("glu_mm" + t, _mm, yg, W["w_glu", layer], bk="bkn", ok="bmn", tm=4096)
            z = z.reshape(2, 4, L, SHARD_COLS_GLU)
            h1, hn2 = _glu_res_rms(z, h, vec(W["g_ffn"][layer]), name="glu_res" + t)
            mix_saved = (h, hn, ypre, yg, sb, z, a1, a2, wb16, wc16, dvec, prep_vjp)
        else:
            j = layer - N_A
            if layer == N_A:
                hkv = _rms_fwd(h, vec(W["g_kv"]), name="kv_norm")
                kvm = S.run("kv_mm", _mm, hkv, W["w_kv",], out_dtype=BF16)
                fl = S.run("f_mm", _mm, hkv, W["w_f",])
                cum = _fgate_fwd(fl, W["b_f_pad",], name="fgate_fwd")
                ck = cum[:, :N_HEADS].T.reshape(N_HEADS, 1, L)
                kvs = (h, hkv, fl, kvm, ck)
            _, _, _, kvm, ck = kvs
            hn = _rms_fwd(h, vec(W["g_mix"][layer]), name="mix_norm" + t)
            q = S.run("q_mm" + t, _mm, hn, W["w_q", j], out_dtype=BF16)
            o, o32, lse = S.run("attn_fwd" + t, _attn_fwd, q, kvm, ck)
            h1 = S.run("o_mm" + t, _mm, o, W["w_o", j], add=h)
            hn2 = _rms_fwd(h1, vec(W["g_ffn"][layer]), name="ffn_norm" + t)
            mix_saved = (h, hn, q, o32, o, lse)
        u0 = S.run("ffn_in" + t, _mm, hn2, W["w_ffn_in", layer], bk="nbk", ok="bmn", tm=2048).reshape(2, 4, L, CF)
        a = S.run("ffn_act" + t, _conv_act, u0, W["conv_w", layer], W["conv_b", layer])
        h2 = S.run("ffn_out" + t, _mm, a, W["w_ffn_out", layer], ak="bmk", bk="kbn", add=h1, tn=512, kg=4)
        saved.append((mix_saved, h1, hn2, u0))
        h = h2

    loss, dh, dg_final = _loss_head(h, vec(W["g_final"]), target, name="loss_head")
    g = {"g_final": dg_final.reshape(-1)}
    gl = {k: [None] * DEPTH for k in ("g_mix", "g_ffn", "conv_w", "ffn_conv_b")}
    ga = {k: [None] * N_A for k in S5_NAMES + ("ssm_d",)}
    dk = dv = dck = None
    for layer in reversed(range(DEPTH)):
        t = str(layer)
        mix_saved, h1, hn2, u0 = saved[layer]
        cw, cb = W["conv_w", layer], W["conv_b", layer]
        da = S.run("ffn_da" + t, _mm, dh, W["w_ffn_out", layer], bk="nbk", ok="bmn", tm=2048)
        a, du0, dcw, dcb = S.run("ffn_conv_bwd" + t, _conv_ffn_bwd, u0, cw, cb, da)
        dw_out = S.run("ffn_dwout" + t, _mm, a, dh, ak="bkm", ok="mbn", out_dtype=BF16)
        S.grad["w_ffn_out", layer] = dw_out.reshape(N_DEV, SHARD_ROWS_FFN, D_MODEL)
        du0 = du0.reshape(N_DEV, L, CF)
        S.grad["w_ffn_in", layer] = S.run("ffn_dwin" + t, _mm, du0, hn2, ak="bkm", ok="mbn", out_dtype=BF16)
        dhn2 = S.run("ffn_dhn" + t, _mm, du0, W["w_ffn_in", layer], ak="bmk", bk="kbn", kg=4)
        dh1, dg = _rms_bwd(h1, vec(W["g_ffn"][layer]), dhn2, dh, name="ffn_norm_bwd" + t)
        gl["g_ffn"][layer], gl["conv_w"][layer], gl["ffn_conv_b"][layer] = dg.reshape(-1), dcw, dcb.reshape(-1)
        if layer < N_A:
            hin, hn, ypre, yg, sb, z, a1, a2, wb16, wc16, dvec, prep_vjp = mix_saved
            dz = _glu_bwd(z, dh1, name="glu_bwd" + t).reshape(N_DEV, L, SHARD_COLS_GLU)
            S.grad["w_glu", layer] = S.run("glu_dw" + t, _mm, yg, dz, ak="km", bk="bkn", ok="bmn", out_dtype=BF16,
                                           tk=4096)
            dyg = S.run("glu_dy" + t, _mm, dz, W["w_glu", layer], ak="bmk", bk="bnk", out_dtype=BF16, tm=2048, kg=8)
            if layer == 0:
                S.bcast["repl_early",] = _pack_rows(_repl_early({**g, **gl, **ga}), REPL_EARLY_ROWS).astype(BF16)
            du, dwb, dwc, da1, da2, dd = S.run("s5_bwd" + t, _s5_bwd, hn, dyg, ypre, sb, wb16, wc16, a1, a2, dvec)
            for nme, val in zip(S5_NAMES, prep_vjp((da1, da2, dwb, dwc))):
                ga[nme][layer] = val
            ga["ssm_d"][layer] = dd.reshape(-1)
            dh, dg = _rms_bwd(hin, vec(W["g_mix"][layer]), du, dh1, name="mix_norm_bwd" + t)
        else:
            j = layer - N_A
            hin, hn, q, o32, o, lse = mix_saved
            _, _, _, kvm, ck = kvs
            S.grad["w_o", j] = S.run("o_dw" + t, _mm, o, dh1, ak="km", out_dtype=BF16
                                     ).reshape(N_DEV, SHARD_ROWS_QO, D_MODEL)
            do = S.run("o_dx" + t, _mm, dh1, W["w_o", j], bk="nk", out_dtype=BF16)
            delta = _attn_delta(do, o32, name="attn_delta" + t)
            dq_t, dk_l, dv_l, dck_l = S.run("attn_bwd" + t, _attn_bwd, q, kvm, ck.reshape(N_HEADS, L, 1), do,
                                            lse.reshape(N_HEADS, 1, L), delta)
            dk = dk_l if dk is None else dk + dk_l
            dv = dv_l if dv is None else dv + dv_l
            dck = dck_l if dck is None else dck + dck_l
            S.grad["w_q", j] = S.run("q_dw" + t, _mm, hn, dq_t, ak="km", bk="nk", out_dtype=BF16
                                     ).reshape(N_DEV, SHARD_ROWS_QO, D_MODEL)
            dhn = S.run("q_dx" + t, _mm, dq_t, W["w_q", j], ak="km", bk="nk")
            dh, dg = _rms_bwd(hin, vec(W["g_mix"][layer]), dhn, dh1, name="mix_norm_bwd" + t)
            if layer == N_A:
                hkv_in, hkv, fl, _, _ = kvs
                dcum = jnp.pad(dck.reshape(N_HEADS, L).T, ((0, 0), (0, LANES - N_HEADS)))
                dfl, dbf = _fgate_bwd(fl, W["b_f_pad",], dcum, name="fgate_bwd")
                dkv = jnp.concatenate([dk, dv], axis=1).astype(BF16)
                dfl16 = dfl.astype(BF16)
                dw_kv = S.run("kv_dw", _mm, hkv, dkv, ak="km")
                dw_f = S.run("f_dw", _mm, hkv, dfl16, ak="km")
                S.grad["w_kvf",] = _kvf_blocks(jnp.concatenate([dw_kv, dw_f[:, :N_HEADS]], axis=1)).astype(BF16)
                dhkv = S.run("kv_dx", _mm, dkv, W["w_kv",], bk="nk")
                dhkv = S.run("f_dx", _mm, dfl16, W["w_f",], bk="nk", add=dhkv)
                g["b_f"] = dbf[0, :N_HEADS]
                dh, dgkv = _rms_bwd(hkv_in, vec(W["g_kv"]), dhkv, dh, name="kv_norm_bwd")
                g["g_kv"] = dgkv.reshape(-1)
        gl["g_mix"][layer] = dg.reshape(-1)

    for d in (gl, ga):
        for k, v in d.items():
            g[k] = jnp.stack(v)
    return loss, dh, g


def _adamw_layers(slots, w, m, v, *, name):
    shape = w.shape
    nl = len(slots)
    w, m, v = (a.reshape((nl,) + a.shape[-2:]) for a in (w, m, v))
    _, R, C = w.shape
    tr = _tile(R, 256, 16)
    c1 = 1.0 / (1.0 - ADAM_B1 ** ADAM_STEP)
    c2 = 1.0 / (1.0 - ADAM_B2 ** ADAM_STEP)

    def body(*refs):
        s_refs, (w_ref, m_ref, v_ref), (g_ref, d_ref, nm_ref, nv_ref) = refs[:nl], refs[nl:nl + 3], refs[nl + 3:]
        for layer in range(nl):
            @pl.when(pl.program_id(0) == layer)
            def _(s_ref=s_refs[layer]):
                g = s_ref[0].astype(F32)
                for d in range(1, N_DEV):
                    g = g + s_ref[d].astype(F32)
                m2 = ADAM_B1 * m_ref[...] + (1.0 - ADAM_B1) * g
                v2 = ADAM_B2 * v_ref[...] + (1.0 - ADAM_B2) * (g * g)
                g_ref[...] = g
                nm_ref[...] = m2
                nv_ref[...] = v2
                d_ref[...] = -ADAM_LR * ((m2 * c1) / (jnp.sqrt(v2 * c2) + ADAM_EPS) + ADAM_WD * w_ref[...])

    def slab_spec(layer):
        return pl.BlockSpec((N_DEV, tr, C), lambda l, i: (0, jnp.where(l == layer, i, 0), 0))

    row = pl.BlockSpec((None, tr, C), lambda l, i: (l, i, 0))
    out = jax.ShapeDtypeStruct((nl, R, C), F32)
    outs = pl.pallas_call(
        body, grid=(nl, R // tr), in_specs=[slab_spec(layer) for layer in range(nl)] + [row, row, row],
        out_specs=[row, row, row, row], out_shape=[out, out, out, out],
        compiler_params=_params("arbitrary", "arbitrary"), name=name)(*slots, w, m, v)
    return [o.reshape(shape) for o in outs]


_SMALL_ROWS = 72
_ORDER = ("g_mix", "g_ffn", "lam_re", "lam_im", "log_dt", "ssm_b_re", "ssm_b_im", "ssm_c_re", "ssm_c_im", "ssm_d",
          "w_glu", "g_kv", "w_kvf", "b_f", "w_q", "w_o", "w_ffn_in", "ffn_conv_w", "ffn_conv_b", "w_ffn_out", "g_final")


def _pack_small(ssm_d, conv_w):
    flat = jnp.concatenate([ssm_d.reshape(-1), conv_w.reshape(-1)])
    return jnp.pad(flat, (0, _SMALL_ROWS * LANES - flat.shape[0])).reshape(_SMALL_ROWS, LANES)


def _unpack_small(flat):
    flat = flat.reshape(-1)
    return flat[:256].reshape(2, 128), flat[256:256 + 8448].reshape(4, 3, 704)


_W_IN_PARTS = {0: (352, 352), 1: (176, 528)}
_FWD_PLAN = {
    "start": [("small",)],
    "s5_fwd0": [("w_glu", 0), ("w_ffn_in", 0, 0)],
    "glu_mm0": [("w_ffn_in", 0, 1)],
    "ffn_in0": [("w_ffn_out", 0)],
    "ffn_act0": [("w_glu", 1)],
    "ffn_out0": [("w_ffn_in", 1, 0)],
    "s5_fwd1": [("w_ffn_in", 1, 1)],
    "ffn_in1": [("w_ffn_out", 1)],
    "ffn_act1": [("w_kvf",), ("w_o", 0)],
    "ffn_out1": [("w_q", 0)],
    "attn_fwd2": [("w_ffn_in", 2), ("w_ffn_out", 2), ("w_q", 1), ("w_o", 1)],
    "attn_fwd3": [("w_ffn_in", 3), ("w_ffn_out", 3)],
}
_BWD_PLAN = {
    "ffn_dhn3": [("w_ffn_out", 3)],
    "attn_bwd3": [("w_ffn_in", 3), ("w_o", 1)],
    "ffn_conv_bwd2": [("w_q", 1)],
    "ffn_dhn2": [("w_ffn_out", 2)],
    "attn_bwd2": [("w_ffn_in", 2), ("w_o", 0)],
    "ffn_conv_bwd1": [("w_q", 0), ("w_kvf",)],
    "ffn_dhn1": [("w_ffn_out", 1)],
    "s5_bwd1": [("w_ffn_in", 1), ("w_glu", 1)],
    "ffn_dhn0": [("w_ffn_out", 0)],
    "s5_bwd0": [("w_ffn_in", 0), ("w_glu", 0), ("repl_early",)],
    "end": [("small",), ("repl_late",)],
}
_PLAN = {h: [("w", k) for k in ks] for h, ks in _FWD_PLAN.items()}
_PLAN.update({h: [("b" if k[0].startswith("repl") else "g", k) for k in ks] for h, ks in _BWD_PLAN.items()})


def kernel(x, g_mix, g_ffn, lam_re, lam_im, log_dt, ssm_b_re, ssm_b_im, ssm_c_re, ssm_c_im, ssm_d, w_glu, g_kv, w_kvf, b_f, w_q, w_o, w_ffn_in, ffn_conv_w, ffn_conv_b, w_ffn_out, g_final, loss_target, m_g_mix, m_g_ffn, m_lam_re, m_lam_im, m_log_dt, m_ssm_b_re, m_ssm_b_im, m_ssm_c_re, m_ssm_c_im, m_ssm_d, m_w_glu, m_g_kv, m_w_kvf, m_b_f, m_w_q, m_w_o, m_w_ffn_in, m_ffn_conv_w, m_ffn_conv_b, m_w_ffn_out, m_g_final, v_g_mix, v_g_ffn, v_lam_re, v_lam_im, v_log_dt, v_ssm_b_re, v_ssm_b_im, v_ssm_c_re, v_ssm_c_im, v_ssm_d, v_w_glu, v_g_kv, v_w_kvf, v_b_f, v_w_q, v_w_o, v_w_ffn_in, v_ffn_conv_w, v_ffn_conv_b, v_w_ffn_out, v_g_final):
    wts = dict(g_mix=g_mix, g_ffn=g_ffn, lam_re=lam_re, lam_im=lam_im, log_dt=log_dt, ssm_b_re=ssm_b_re,
               ssm_b_im=ssm_b_im, ssm_c_re=ssm_c_re, ssm_c_im=ssm_c_im, ssm_d=ssm_d, w_glu=w_glu, g_kv=g_kv,
               w_kvf=w_kvf, b_f=b_f, w_q=w_q, w_o=w_o, w_ffn_in=w_ffn_in, ffn_conv_w=ffn_conv_w,
               ffn_conv_b=ffn_conv_b, w_ffn_out=w_ffn_out, g_final=g_final)
    mom = dict(g_mix=m_g_mix, g_ffn=m_g_ffn, lam_re=m_lam_re, lam_im=m_lam_im, log_dt=m_log_dt, ssm_b_re=m_ssm_b_re,
               ssm_b_im=m_ssm_b_im, ssm_c_re=m_ssm_c_re, ssm_c_im=m_ssm_c_im, ssm_d=m_ssm_d, w_glu=m_w_glu,
               g_kv=m_g_kv, w_kvf=m_w_kvf, b_f=m_b_f, w_q=m_w_q, w_o=m_w_o, w_ffn_in=m_w_ffn_in,
               ffn_conv_w=m_ffn_conv_w, ffn_conv_b=m_ffn_conv_b, w_ffn_out=m_w_ffn_out, g_final=m_g_final)
    var = dict(g_mix=v_g_mix, g_ffn=v_g_ffn, lam_re=v_lam_re, lam_im=v_lam_im, log_dt=v_log_dt, ssm_b_re=v_ssm_b_re,
               ssm_b_im=v_ssm_b_im, ssm_c_re=v_ssm_c_re, ssm_c_im=v_ssm_c_im, ssm_d=v_ssm_d, w_glu=v_w_glu,
               g_kv=v_g_kv, w_kvf=v_w_kvf, b_f=v_b_f, w_q=v_w_q, w_o=v_w_o, w_ffn_in=v_w_ffn_in,
               ffn_conv_w=v_ffn_conv_w, ffn_conv_b=v_ffn_conv_b, w_ffn_out=v_w_ffn_out, g_final=v_g_final)
    kinds = ("grad", "delta", "m", "v")

    ready = {n: wts[n] for n in ("g_mix", "g_ffn", "g_kv", "g_final") + S5_NAMES}
    ready["b_f_pad",] = jnp.pad(b_f, (0, LANES - N_HEADS)).reshape(1, LANES)
    send = {("small",): _pack_small(ssm_d, ffn_conv_w), ("w_kvf",): w_kvf.astype(BF16)}
    for layer in range(DEPTH):
        ready["conv_b", layer] = ffn_conv_b[layer].reshape(2, 4, 1, SHARD_COLS_FFN)
        w_in_t = jnp.swapaxes(w_ffn_in[layer], 0, 1).astype(BF16)
        if layer in _W_IN_PARTS:
            row = 0
            for p, rows in enumerate(_W_IN_PARTS[layer]):
                send["w_ffn_in", layer, p] = w_in_t[row:row + rows]
                row += rows
        else:
            send["w_ffn_in", layer] = w_in_t
        send["w_ffn_out", layer] = w_ffn_out[layer].astype(BF16)
    for layer in range(N_A):
        send["w_glu", layer] = w_glu[layer].astype(BF16)
        send["w_q", layer] = w_q[layer].astype(BF16)
        send["w_o", layer] = w_o[layer].astype(BF16)

    S = _Step(ready, send, _PLAN)
    S.land("start", _exchange(S.xfers("start"), name="start"))
    loss, dx, g = _step(x[0], loss_target[0], S)
    loss = lax.psum(loss[0, 0], MESH_AXES)

    g_d = g["ssm_d"].reshape(N_A, N_DEV, LANES).transpose(1, 0, 2).reshape(N_DEV, N_A * LANES)
    g_cw = jnp.stack([g["conv_w"][layer].reshape(N_DEV, 3, SHARD_COLS_FFN) for layer in range(DEPTH)], axis=1)
    g_small = jnp.concatenate([g_d, g_cw.reshape(N_DEV, -1)], axis=1)
    g_small = jnp.pad(g_small, ((0, 0), (0, _SMALL_ROWS * LANES - g_small.shape[1])))
    S.grad["small",] = g_small.reshape(N_DEV, _SMALL_ROWS, LANES)
    S.bcast["repl_late",] = _pack_rows(_repl_late(g), REPL_LATE_ROWS).astype(BF16)
    S.land("end", _exchange(S.xfers("end"), name="end"))

    res = {}
    for name, nl in (("w_glu", N_A), ("w_q", DEPTH - N_A), ("w_o", DEPTH - N_A), ("w_ffn_in", DEPTH),
                     ("w_ffn_out", DEPTH)):
        view = (lambda a: jnp.swapaxes(a, 1, 2)) if name == "w_ffn_in" else (lambda a: a)
        outs = _adamw_layers([S.slots[name, layer] for layer in range(nl)], view(wts[name]), view(mom[name]),
                             view(var[name]), name="adamw_" + name)
        res.update({(kind, name): view(a) for kind, a in zip(kinds, outs)})
    outs = _adamw_layers([S.slots["w_kvf",]], w_kvf, m_w_kvf, v_w_kvf, name="adamw_w_kvf")
    res.update({(kind, "w_kvf"): a for kind, a in zip(kinds, outs)})
    outs = _adamw_layers([S.slots["small",]], _pack_small(ssm_d, ffn_conv_w), _pack_small(m_ssm_d, m_ffn_conv_w),
                         _pack_small(v_ssm_d, v_ffn_conv_w), name="adamw_small")
    for kind, flat in zip(kinds, outs):
        res[kind, "ssm_d"], res[kind, "ffn_conv_w"] = _unpack_small(flat)

    pieces = {}
    for key, rows, sel in ((("repl_early",), REPL_EARLY_ROWS, _repl_early), (("repl_late",), REPL_LATE_ROWS, _repl_late)):
        outs = _adamw_layers([S.slots[key]], *[_pack_rows(sel(d), rows) for d in (wts, mom, var)],
                             name="adamw_" + key[0])
        for kind, flat in zip(kinds, outs):
            pieces[kind, key[0]] = _unpack_rows(flat, sel(wts))
    for kind in kinds:
        early, late = iter(pieces[kind, "repl_early"]), iter(pieces[kind, "repl_late"])
        take = lambda it, n: [next(it) for _ in range(n)]
        res[kind, "g_mix"] = jnp.stack(take(late, 1) + take(early, DEPTH - 1))
        res[kind, "g_ffn"] = jnp.stack(take(early, DEPTH))
        for n in S5_NAMES:
            res[kind, n] = jnp.stack([next(late), next(early)]).reshape(wts[n].shape)
        res[kind, "g_kv"], res[kind, "b_f"] = next(early), next(early)
        res[kind, "ffn_conv_b"] = jnp.stack(take(early, DEPTH))
        res[kind, "g_final"] = next(early)

    return (loss, dx[None], *[res[kind, n] for kind in kinds for n in _ORDER])
```

```python
import functools
import math

import jax
import jax.numpy as jnp
from jax import lax
from jax.experimental import pallas as pl
from jax.experimental.pallas import tpu as pltpu

F32 = jnp.float32
BF16 = jnp.bfloat16

D_MODEL = 1024
DEPTH = 4
N_A = 2
N_GROUPS = 64
SSM_GROUP = 16
SSM_STATE = 64
N_HEADS = 16
HEAD_DIM = 64
ATTN_SCALE = HEAD_DIM ** -0.5
D_FF = 2816
EPS = 1e-6
N_DEV = 8
LANES = 128
SUBLANES = 8

ADAM_LR = 0.001
ADAM_B1 = 0.9
ADAM_B2 = 0.999
ADAM_EPS = 1e-08
ADAM_WD = 0.01
ADAM_STEP = 10

ROW_TILE = 512
S5_CHUNK = 256
ATTN_TILE = 512
CUM_TILE = 256
NEG = -1e30

MESH_AXES = ("x", "y", "c")


def _tile(n, target, align=LANES):
    t = (min(target, n) // align) * align
    while t >= align:
        if n % t == 0:
            return t
        t -= align
    return n


def _params(*sem):
    return pltpu.CompilerParams(dimension_semantics=sem, vmem_limit_bytes=56 * 1024 * 1024)


_ANY = pl.BlockSpec(memory_space=pl.ANY)
_XFER_SEMS = (pltpu.SemaphoreType.DMA((N_DEV - 1,)), pltpu.SemaphoreType.DMA((N_DEV - 1,)), pltpu.SemaphoreType.DMA)


def _xfer_copies(x_ref, o_ref, send_sems, recv_sems, local_sem, scatter, row_off):
    xi, yi, ci = lax.axis_index("x"), lax.axis_index("y"), lax.axis_index("c")
    me = 4 * xi + 2 * yi + ci

    def src(p):
        return x_ref.at[p] if scatter else x_ref

    def dst(p):
        return o_ref.at[p] if row_off is None else o_ref.at[p, pl.ds(row_off, x_ref.shape[0])]

    own = pltpu.make_async_copy(src(me), dst(me), local_sem)
    sends, recvs = [], []
    for k in range(1, N_DEV):
        px, py, pc = xi ^ (k >> 2), yi ^ ((k >> 1) & 1), ci ^ (k & 1)
        p = 4 * px + 2 * py + pc
        sends.append(pltpu.make_async_remote_copy(
            src_ref=src(p), dst_ref=dst(me), send_sem=send_sems.at[k - 1], recv_sem=recv_sems.at[k - 1],
            device_id=(px, py, pc), device_id_type=pl.DeviceIdType.MESH))
        recvs.append(pltpu.make_async_remote_copy(
            src_ref=src(p), dst_ref=dst(p), send_sem=send_sems.at[k - 1], recv_sem=recv_sems.at[k - 1],
            device_id=(px, py, pc), device_id_type=pl.DeviceIdType.MESH))
    return own, sends, recvs


def _xfer_start(*refs, scatter, row_off):
    own, sends, _ = _xfer_copies(*refs, scatter, row_off)
    own.start()
    for cp in sends:
        cp.start()


def _xfer_wait(*refs, scatter, row_off):
    own, sends, recvs = _xfer_copies(*refs, scatter, row_off)
    for cp in recvs:
        cp.wait_recv()
    for cp in sends:
        cp.wait_send()
    own.wait()


def _pcall(body, args, *, grid, in_specs, out_specs, out_shape, scratch_shapes=(), sem, name, xfers=(), prefetch=()):
    out_specs, out_shape = list(out_specs), list(out_shape)
    xfers = [tuple(x) + (None,) * (3 - len(x)) for x in xfers]
    n_pre, n_in, n_out, n_x, n_scr = len(prefetch), len(in_specs), len(out_specs), len(xfers), len(scratch_shapes)
    flags = [(s, None if w is None else w[1]) for _, s, w in xfers]
    prevs = [(t, w[2]) for t, (_, _, w) in enumerate(xfers) if w is not None and w[2] is not None]
    n_b = len(prevs)
    assert not (prevs and prefetch)

    def xfer_shape(x, scatter, w):
        if w is not None:
            return jax.ShapeDtypeStruct((N_DEV, w[0]) + x.shape[1:], x.dtype)
        return jax.ShapeDtypeStruct((N_DEV,) + (x.shape[1:] if scatter else x.shape), x.dtype)

    def wrapped(*refs):
        pre, refs = refs[:n_pre], refs[n_pre:]
        ins, xin = refs[:n_in], refs[n_in:n_in + n_x]
        refs = refs[n_in + n_x + n_b:]
        outs, xout, scr = refs[:n_out], refs[n_out:n_out + n_x], refs[n_out + n_x:]
        own, sems = scr[:n_scr], scr[n_scr:]
        ids = [pl.program_id(d) for d in range(len(grid))]
        first = functools.reduce(jnp.logical_and, [i == 0 for i in ids])
        last = functools.reduce(jnp.logical_and, [i == g - 1 for i, g in zip(ids, grid)])

        @pl.when(first)
        def _():
            for t in range(n_x):
                _xfer_start(xin[t], xout[t], *sems[3 * t:3 * t + 3], scatter=flags[t][0], row_off=flags[t][1])

        body(*pre, *ins, *outs, *own)

        @pl.when(last)
        def _():
            for t in range(n_x):
                _xfer_wait(xin[t], xout[t], *sems[3 * t:3 * t + 3], scatter=flags[t][0], row_off=flags[t][1])

    grid_spec = pltpu.PrefetchScalarGridSpec(
        num_scalar_prefetch=n_pre, grid=grid, in_specs=list(in_specs) + [_ANY] * (n_x + n_b),
        out_specs=out_specs + [_ANY] * n_x, scratch_shapes=list(scratch_shapes) + list(_XFER_SEMS) * n_x)
    return pl.pallas_call(
        wrapped if xfers else body, grid_spec=grid_spec, out_shape=out_shape + [xfer_shape(*x) for x in xfers],
        input_output_aliases={n_in + n_x + b: n_out + t for b, (t, _) in enumerate(prevs)},
        compiler_params=_params(*(["arbitrary"] * len(grid) if xfers else sem)), name=name,
    )(*prefetch, *args, *[x[0] for x in xfers], *[p for _, p in prevs])


def _exchange(xfers, *, name):
    def body():
        pass

    return _pcall(body, (), grid=(1,), in_specs=[], out_specs=[], out_shape=[], sem=("arbitrary",), name=name,
                  xfers=xfers)


def _mm(a, b, *, ak="mk", bk="kn", ok="mn", add=None, out_dtype=F32, tm=1024, tn=1024, tk=1024, kg=1, name,
        xfers=()):
    sa, sb = a.shape, b.shape
    fm = fn = fk = None
    if ak == "mk":
        M, K, a_c = sa[0], sa[1], 1
    elif ak == "km":
        K, M, a_c = sa[0], sa[1], 0
    elif ak == "bmk":
        M, K, a_c, fk = sa[1], sa[0] * sa[2], 1, sa[2]
    else:
        K, M, a_c, fm = sa[1], sa[0] * sa[2], 0, sa[2]
    if bk == "kn":
        N, b_c = sb[1], 0
    elif bk == "nk":
        N, b_c = sb[0], 1
    elif bk == "bkn":
        N, b_c, fn = sb[0] * sb[2], 0, sb[2]
    elif bk == "bnk":
        N, b_c, fk = sb[1], 1, sb[2]
    elif bk == "kbn":
        N, b_c, fk = sb[2], 0, sb[1]
    else:
        N, b_c, fn = sb[0] * sb[1], 1, sb[1]
    tm, tn, tk = fm or _tile(M, tm), fn or _tile(N, tn), fk or _tile(K, tk)
    kblk = None if kg == 1 else kg
    nm, nn, nk = M // tm, N // tn, K // (tk * kg)

    a_spec = {"mk": pl.BlockSpec((tm, tk), lambda i, j, k: (i, k)),
              "km": pl.BlockSpec((tk, tm), lambda i, j, k: (k, i)),
              "bmk": pl.BlockSpec((kblk, tm, tk), lambda i, j, k: (k, i, 0)),
              "bkm": pl.BlockSpec((None, tk, tm), lambda i, j, k: (i, k, 0))}[ak]
    b_spec = {"kn": pl.BlockSpec((tk, tn), lambda i, j, k: (k, j)),
              "nk": pl.BlockSpec((tn, tk), lambda i, j, k: (j, k)),
              "bkn": pl.BlockSpec((None, tk, tn), lambda i, j, k: (j, k, 0)),
              "bnk": pl.BlockSpec((kblk, tn, tk), lambda i, j, k: (k, j, 0)),
              "kbn": pl.BlockSpec((kblk, tk, tn), lambda i, j, k: (k, 0, j)),
              "nbk": pl.BlockSpec((None, tn, tk), lambda i, j, k: (j, 0, k))}[bk]
    if ok == "mn":
        o_spec = pl.BlockSpec((tm, tn), lambda i, j, k: (i, j))
        out_shape = jax.ShapeDtypeStruct((M, N), out_dtype)
    elif ok == "bmn":
        o_spec = pl.BlockSpec((None, tm, tn), lambda i, j, k: (j, i, 0))
        out_shape = jax.ShapeDtypeStruct((nn, M, tn), out_dtype)
    else:
        o_spec = pl.BlockSpec((None, tm, tn), lambda i, j, k: (i, 0, j))
        out_shape = jax.ShapeDtypeStruct((nm, tm, N), out_dtype)
    dims = (((a_c,), (b_c,)), ((), ()))
    has_add = add is not None

    def body(*refs):
        a_ref, b_ref = refs[0], refs[1]
        add_ref = refs[2] if has_add else None
        o_ref = refs[3] if has_add else refs[2]
        if kg == 1:
            part = lax.dot_general(a_ref[...].astype(BF16), b_ref[...].astype(BF16), dims, preferred_element_type=F32)
        else:
            part = sum(lax.dot_general(a_ref[g].astype(BF16), b_ref[g].astype(BF16), dims,
                                       preferred_element_type=F32) for g in range(kg))

        def finish(r):
            if has_add:
                r = r + add_ref[...]
            o_ref[...] = r.astype(out_dtype)

        if nk == 1:
            finish(part)
            return
        acc = refs[-1]
        k = pl.program_id(2)

        @pl.when(k == 0)
        def _():
            acc[...] = part

        @pl.when(k > 0)
        def _():
            acc[...] += part

        @pl.when(k == nk - 1)
        def _():
            finish(acc[...])

    in_specs = [a_spec, b_spec]
    args = [a, b]
    if has_add:
        in_specs.append(pl.BlockSpec((tm, tn), lambda i, j, k: (i, j)))
        args.append(add)
    res = _pcall(body, args, grid=(nm, nn, nk), in_specs=in_specs, out_specs=[o_spec], out_shape=[out_shape],
                 scratch_shapes=[pltpu.VMEM((tm, tn), F32)] if nk > 1 else [],
                 sem=("parallel", "parallel", "arbitrary"), name=name, xfers=xfers)
    return res if xfers else res[0]


def _rms_fwd(h, g, *, name):
    L, D = h.shape
    tr = _tile(L, ROW_TILE, SUBLANES)

    def body(h_ref, g_ref, o_ref):
        x = h_ref[...]
        r = lax.rsqrt(jnp.mean(x * x, axis=1, keepdims=True) + EPS)
        o_ref[...] = (x * r * g_ref[...]).astype(BF16)

    return pl.pallas_call(
        body, grid=(L // tr,),
        in_specs=[pl.BlockSpec((tr, D), lambda i: (i, 0)), pl.BlockSpec((1, D), lambda i: (0, 0))],
        out_specs=pl.BlockSpec((tr, D), lambda i: (i, 0)), out_shape=jax.ShapeDtypeStruct((L, D), BF16),
        compiler_params=_params("parallel"), name=name)(h, g)


def _rms_bwd(h, g, dy, dres, *, name):
    L, D = h.shape
    tr = _tile(L, ROW_TILE, SUBLANES)

    def body(h_ref, g_ref, dy_ref, dres_ref, dh_ref, dg_ref):
        @pl.when(pl.program_id(0) == 0)
        def _():
            dg_ref[...] = jnp.zeros_like(dg_ref)

        x = h_ref[...]
        r = lax.rsqrt(jnp.mean(x * x, axis=1, keepdims=True) + EPS)
        xn = x * r
        dy = dy_ref[...].astype(F32)
        gdy = dy * g_ref[...]
        dx = r * (gdy - xn * jnp.mean(gdy * xn, axis=1, keepdims=True))
        dh_ref[...] = dres_ref[...] + dx
        dg_ref[...] += jnp.sum(dy * xn, axis=0, keepdims=True)

    row = pl.BlockSpec((tr, D), lambda i: (i, 0))
    vec = pl.BlockSpec((1, D), lambda i: (0, 0))
    return pl.pallas_call(
        body, grid=(L // tr,), in_specs=[row, vec, row, row], out_specs=[row, vec],
        out_shape=[jax.ShapeDtypeStruct((L, D), F32), jax.ShapeDtypeStruct((1, D), F32)],
        compiler_params=_params("arbitrary"), name=name)(h, g, dy, dres)


def _glu_res_rms(z, h, g, *, name):
    L, D = h.shape
    nb, cb = z.shape[1], z.shape[3]
    tr = _tile(L, ROW_TILE, SUBLANES)

    def body(z_ref, h_ref, g_ref, h1_ref, hn_ref):
        za = jnp.concatenate([z_ref[0, d] for d in range(nb)], axis=1)
        zg = jnp.concatenate([z_ref[1, d] for d in range(nb)], axis=1)
        x = h_ref[...] + za * jax.nn.sigmoid(zg)
        h1_ref[...] = x
        r = lax.rsqrt(jnp.mean(x * x, axis=1, keepdims=True) + EPS)
        hn_ref[...] = (x * r * g_ref[...]).astype(BF16)

    row = pl.BlockSpec((tr, D), lambda i: (i, 0))
    return pl.pallas_call(
        body, grid=(L // tr,),
        in_specs=[pl.BlockSpec((2, nb, tr, cb), lambda i: (0, 0, i, 0)), row, pl.BlockSpec((1, D), lambda i: (0, 0))],
        out_specs=[row, row],
        out_shape=[jax.ShapeDtypeStruct((L, D), F32), jax.ShapeDtypeStruct((L, D), BF16)],
        compiler_params=_params("parallel"), name=name)(z, h, g)


def _glu_bwd(z, dout, *, name):
    L, D = dout.shape
    nb, cb = z.shape[1], z.shape[3]
    tr = _tile(L, ROW_TILE, SUBLANES)

    def body(z_ref, d_ref, o_ref):
        for d in range(nb):
            dd = d_ref[:, d * cb:(d + 1) * cb]
            sg = jax.nn.sigmoid(z_ref[1, d])
            o_ref[0, d] = (dd * sg).astype(BF16)
            o_ref[1, d] = (dd * z_ref[0, d] * sg * (1.0 - sg)).astype(BF16)

    zs = pl.BlockSpec((2, nb, tr, cb), lambda i: (0, 0, i, 0))
    return pl.pallas_call(
        body, grid=(L // tr,), in_specs=[zs, pl.BlockSpec((tr, D), lambda i: (i, 0))], out_specs=zs,
        out_shape=jax.ShapeDtypeStruct(z.shape, BF16),
        compiler_params=_params("parallel"), name=name)(z, dout)


def _loss_head(h, g, target, *, name):
    L, D = h.shape
    tr = _tile(L, ROW_TILE, SUBLANES)

    def body(h_ref, g_ref, t_ref, loss_ref, dh_ref, dg_ref):
        @pl.when(pl.program_id(0) == 0)
        def _():
            dg_ref[...] = jnp.zeros_like(dg_ref)
            loss_ref[...] = jnp.zeros_like(loss_ref)

        x = h_ref[...]
        gg = g_ref[...]
        r = lax.rsqrt(jnp.mean(x * x, axis=1, keepdims=True) + EPS)
        xn = x * r
        err = xn * gg - t_ref[...]
        loss_ref[...] += 0.5 * jnp.sum(jnp.mean(err * err, axis=1, keepdims=True), axis=0, keepdims=True)
        dy = err * (1.0 / D)
        gdy = dy * gg
        dh_ref[...] = r * (gdy - xn * jnp.mean(gdy * xn, axis=1, keepdims=True))
        dg_ref[...] += jnp.sum(dy * xn, axis=0, keepdims=True)

    row = pl.BlockSpec((tr, D), lambda i: (i, 0))
    vec = pl.BlockSpec((1, D), lambda i: (0, 0))
    return pl.pallas_call(
        body, grid=(L // tr,), in_specs=[row, vec, row],
        out_specs=[pl.BlockSpec((1, 1), lambda i: (0, 0)), row, vec],
        out_shape=[jax.ShapeDtypeStruct((1, 1), F32), jax.ShapeDtypeStruct((L, D), F32),
                   jax.ShapeDtypeStruct((1, D), F32)],
        compiler_params=_params("arbitrary"), name=name)(h, g, target)


CONV_ROW_TILE = 256


def _sigmoid(x):
    return pl.reciprocal(1.0 + jnp.exp(-x), approx=True)


def _conv_specs(L, tr, tc):
    nrb = tr // SUBLANES
    before = lambda i: jnp.maximum(i * nrb - 1, 0)
    after = lambda i: jnp.minimum((i + 1) * nrb, L // SUBLANES - 1)
    main = pl.BlockSpec((2, None, tr, tc), lambda j, i: (0, j, i, 0))
    prev = pl.BlockSpec((2, None, SUBLANES, tc), lambda j, i: (0, j, before(i), 0))
    nxt = pl.BlockSpec((2, None, SUBLANES, tc), lambda j, i: (0, j, after(i), 0))
    cw = pl.BlockSpec((2, None, 3, tc), lambda j, i: (0, j, 0, 0))
    cb = pl.BlockSpec((2, None, 1, tc), lambda j, i: (0, j, 0, 0))
    half = pl.BlockSpec((None, tr, tc), lambda j, i: (j, i, 0))
    half_nxt = pl.BlockSpec((None, SUBLANES, tc), lambda j, i: (j, after(i), 0))
    return main, prev, nxt, cw, cb, half, half_nxt


def _conv_rows(xe, w, b):
    x1 = pltpu.roll(xe, 1, 0)
    x2 = pltpu.roll(xe, 2, 0)
    return b + x2 * w[0:1] + x1 * w[1:2] + xe * w[2:3], x1, x2


def _shift_down(x, halo, k, row):
    y = pltpu.roll(x, k, 0)
    for r in range(k):
        y = jnp.where(row == r, halo[SUBLANES - k + r:SUBLANES - k + r + 1, :], y)
    return y


def _conv_act(u0, cw, cb, *, name, xfers=()):
    _, nb, L, tc = u0.shape
    tr = _tile(L, ROW_TILE, SUBLANES)
    main, prev, _, cws, cbs, half, _ = _conv_specs(L, tr, tc)

    def body(u_ref, p_ref, w_ref, b_ref, a_ref):
        first = pl.program_id(1) == 0
        row = lax.broadcasted_iota(jnp.int32, (tr, tc), 0)
        y = []
        for s in range(2):
            x, w = u_ref[s], w_ref[s]
            halo = jnp.where(first, 0.0, p_ref[s])
            x1 = _shift_down(x, halo, 1, row)
            x2 = _shift_down(x, halo, 2, row)
            y.append(b_ref[s] + x2 * w[0:1] + x1 * w[1:2] + x * w[2:3])
        a_ref[...] = (y[0] * _sigmoid(y[0]) * y[1]).astype(BF16)

    return _pcall(body, (u0, u0, cw, cb), grid=(nb, L // tr), in_specs=[main, prev, cws, cbs], out_specs=[half],
                  out_shape=[jax.ShapeDtypeStruct((nb, L, tc), BF16)], sem=("parallel", "parallel"), name=name,
                  xfers=xfers)


def _conv_ffn_bwd(u0, cw, cb, da, *, name, xfers=()):
    _, nb, L, tc = u0.shape
    tr = _tile(L, CONV_ROW_TILE, SUBLANES)
    main, prev, nxt, cws, cbs, half, half_nxt = _conv_specs(L, tr, tc)
    nr = L // tr
    H = SUBLANES

    def body(u_ref, p_ref, n_ref, w_ref, b_ref, da_ref, dan_ref, a_ref, du0_ref, dcw_ref, dcb_ref):
        i = pl.program_id(1)

        @pl.when(i == 0)
        def _():
            dcw_ref[...] = jnp.zeros_like(dcw_ref)
            dcb_ref[...] = jnp.zeros_like(dcb_ref)

        y, x1, x2 = [], [], []
        for s in range(2):
            xe = jnp.concatenate([jnp.where(i == 0, 0.0, p_ref[s]), u_ref[s], n_ref[s]], axis=0)
            ys, x1s, x2s = _conv_rows(xe, w_ref[s], b_ref[s])
            y.append(ys[H:])
            x1.append(x1s[H:H + tr])
            x2.append(x2s[H:H + tr])
        gate, up = y
        da = jnp.concatenate([da_ref[...], dan_ref[...]], axis=0)
        row = lax.broadcasted_iota(jnp.int32, (tr + H, tc), 0)
        da = jnp.where(jnp.logical_and(i == nr - 1, row >= tr), 0.0, da)
        sg = _sigmoid(gate)
        silu = gate * sg
        a_ref[...] = (silu * up)[:tr].astype(BF16)
        d = (da * up * (sg * (1.0 + gate * (1.0 - sg))), da * silu)
        for s in range(2):
            w = w_ref[s]
            d0 = d[s][:tr]
            d1 = pltpu.roll(d[s], tr + H - 1, 0)[:tr]
            d2 = pltpu.roll(d[s], tr + H - 2, 0)[:tr]
            du0_ref[s] = (d0 * w[2:3] + d1 * w[1:2] + d2 * w[0:1]).astype(BF16)
            dcw_ref[s, 0:1, :] += jnp.sum(d0 * x2[s], axis=0, keepdims=True)
            dcw_ref[s, 1:2, :] += jnp.sum(d0 * x1[s], axis=0, keepdims=True)
            dcw_ref[s, 2:3, :] += jnp.sum(d0 * u_ref[s], axis=0, keepdims=True)
            dcb_ref[s] += jnp.sum(d0, axis=0, keepdims=True)

    return _pcall(body, (u0, u0, u0, cw, cb, da, da), grid=(nb, nr),
                  in_specs=[main, prev, nxt, cws, cbs, half, half_nxt], out_specs=[half, main, cws, cbs],
                  out_shape=[jax.ShapeDtypeStruct((nb, L, tc), BF16), jax.ShapeDtypeStruct((2, nb, L, tc), BF16),
                             jax.ShapeDtypeStruct((2, nb, 3, tc), F32), jax.ShapeDtypeStruct((2, nb, 1, tc), F32)],
                  sem=("parallel", "arbitrary"), name=name, xfers=xfers)


N_TILES = 64
HALF = N_TILES // 2


def _swap(s):
    return jnp.concatenate([s[HALF:], s[:HALF]], axis=0)


def _chan_block(j):
    return ((j % HALF) // 4) * LANES


def _pairs_of(jb):
    return [2 * jb, 2 * jb + 1, HALF // 2 + 2 * jb, HALF // 2 + 2 * jb + 1]


_WB_SPEC = pl.BlockSpec((HALF, LANES, 2 * LANES), lambda c: (0, 0, 0))
_WC_SPEC = _WB_SPEC
GELU_C = math.sqrt(2.0 / math.pi)
GELU_A = 0.044715


def _gelu(x):
    return 0.5 * x * (1.0 + jnp.tanh(GELU_C * (x + GELU_A * x * x * x)))


def _gelu_grad(x):
    th = jnp.tanh(GELU_C * (x + GELU_A * x * x * x))
    return 0.5 * (1.0 + th) + 0.5 * x * (1.0 - th * th) * GELU_C * (1.0 + 3.0 * GELU_A * x * x)


def _tile_rows(j, T, TP):
    return pl.ds(j * TP + SUBLANES, T)


def _pair(ref, jp, T, TP):
    return jnp.concatenate([ref[_tile_rows(2 * jp, T, TP), :], ref[_tile_rows(2 * jp + 1, T, TP), :]],
                           axis=1).astype(BF16)


def _unpair(ref, jp, val, T, TP):
    ref[_tile_rows(2 * jp, T, TP), :] = val[:, :LANES]
    ref[_tile_rows(2 * jp + 1, T, TP), :] = val[:, LANES:]


def _s5_project_in(u_ref, wb_ref, s3, T, TP):
    for jp in range(HALF):
        blk = _chan_block(2 * jp)
        _unpair(s3, jp, jnp.dot(u_ref[:, blk:blk + LANES], wb_ref[jp], preferred_element_type=F32), T, TP)


def _s5_scan_fwd(s3, a1, a2, s0, T, TP):
    span = (N_GROUPS - 1) * TP + 2 * SUBLANES

    def blk(i, s):
        view = s3.at[pl.ds(pl.multiple_of(i * SUBLANES, SUBLANES), span)]
        for k in range(SUBLANES):
            rows = pl.ds(SUBLANES + k, N_GROUPS, stride=TP)
            s = a1 * s + a2 * _swap(s) + view[rows, :]
            view[rows, :] = s
        return s

    return lax.fori_loop(0, T // SUBLANES, blk, s0)


def _s5_fwd(hn, wb, wc, a1, a2, dvec, *, name, xfers=()):
    L, D = hn.shape
    T = min(S5_CHUNK, L)
    TP = T + SUBLANES
    nC = L // T

    def body(u_ref, wb_ref, wc_ref, a1_ref, a2_ref, d_ref, y_ref, yg_ref, sb_ref, s3, st):
        @pl.when(pl.program_id(0) == 0)
        def _():
            st[...] = jnp.zeros_like(st)

        sb_ref[0] = st[...]
        _s5_project_in(u_ref, wb_ref, s3, T, TP)
        st[...] = _s5_scan_fwd(s3, a1_ref[...], a2_ref[...], st[...], T, TP)
        for jb in range(D // LANES):
            acc = jnp.zeros((T, LANES), F32)
            for jp in _pairs_of(jb):
                acc += lax.dot_general(_pair(s3, jp, T, TP), wc_ref[jp], _NT, preferred_element_type=F32)
            cols = slice(jb * LANES, (jb + 1) * LANES)
            y = acc + d_ref[:, cols] * u_ref[:, cols].astype(F32)
            y_ref[:, cols] = y
            yg_ref[:, cols] = _gelu(y).astype(BF16)

    row = pl.BlockSpec((T, D), lambda c: (c, 0))
    aspec = pl.BlockSpec((N_GROUPS, LANES), lambda c: (0, 0))
    return _pcall(
        body, (hn, wb, wc, a1, a2, dvec), grid=(nC,),
        in_specs=[row, _WB_SPEC, _WC_SPEC, aspec, aspec, pl.BlockSpec((1, D), lambda c: (0, 0))],
        out_specs=[row, row, pl.BlockSpec((1, N_GROUPS, LANES), lambda c: (c, 0, 0))],
        out_shape=[jax.ShapeDtypeStruct((L, D), F32), jax.ShapeDtypeStruct((L, D), BF16),
                   jax.ShapeDtypeStruct((nC, N_GROUPS, LANES), F32)],
        scratch_shapes=[pltpu.VMEM((N_GROUPS * TP, LANES), F32), pltpu.VMEM((N_GROUPS, LANES), F32)],
        sem=("arbitrary",), name=name, xfers=xfers)


def _s5_bwd(hn, dyg, ypre, sbound, wb, wc, a1, a2, dvec, *, name, xfers=()):
    L, D = hn.shape
    T = min(S5_CHUNK, L)
    TP = T + SUBLANES
    nC = L // T
    span = (N_GROUPS - 1) * TP + 2 * SUBLANES
    NT = (((1,), (1,)), ((), ()))
    TN = (((0,), (0,)), ((), ()))

    def body(u_ref, dyg_ref, yp_ref, sb_ref, wb_ref, wc_ref, a1_ref, a2_ref, d_ref,
             du_ref, dwb_ref, dwc_ref, da1_ref, da2_ref, dd_ref, s3, g3, gst, dy_s):
        @pl.when(pl.program_id(0) == 0)
        def _():
            gst[...] = jnp.zeros_like(gst)
            dwb_ref[...] = jnp.zeros_like(dwb_ref)
            dwc_ref[...] = jnp.zeros_like(dwc_ref)
            da1_ref[...] = jnp.zeros_like(da1_ref)
            da2_ref[...] = jnp.zeros_like(da2_ref)
            dd_ref[...] = jnp.zeros_like(dd_ref)

        a1 = a1_ref[...]
        a2 = a2_ref[...]
        dy = dyg_ref[...].astype(F32) * _gelu_grad(yp_ref[...])
        dy_s[...] = dy.astype(BF16)
        dd_ref[...] += jnp.sum(dy * u_ref[...].astype(F32), axis=0, keepdims=True)
        du_ref[...] = d_ref[...] * dy

        s3[pl.ds(SUBLANES - 1, N_GROUPS, stride=TP), :] = sb_ref[0]
        _s5_project_in(u_ref, wb_ref, s3, T, TP)
        _s5_scan_fwd(s3, a1, a2, sb_ref[0], T, TP)

        for jp in range(HALF):
            blk = _chan_block(2 * jp)
            _unpair(g3, jp, jnp.dot(dy_s[:, blk:blk + LANES], wc_ref[jp], preferred_element_type=F32), T, TP)
        a2c = -a2

        def rblk(ii, carry):
            g, acc1, acc2 = carry
            t0 = pl.multiple_of((T // SUBLANES - 1 - ii) * SUBLANES, SUBLANES)
            gv = g3.at[pl.ds(t0, span)]
            sv = s3.at[pl.ds(t0, span)]
            for k in reversed(range(SUBLANES)):
                rows = pl.ds(SUBLANES + k, N_GROUPS, stride=TP)
                g = a1 * g + a2c * _swap(g) + gv[rows, :]
                gv[rows, :] = g
                sp = sv[pl.ds(SUBLANES - 1 + k, N_GROUPS, stride=TP), :]
                acc1 = acc1 + g * sp
                acc2 = acc2 + g * _swap(sp)
            return g, acc1, acc2

        zero = jnp.zeros((N_GROUPS, LANES), F32)
        g, acc1, acc2 = lax.fori_loop(0, T // SUBLANES, rblk, (gst[...], zero, zero))
        gst[...] = g
        da1_ref[...] += acc1
        da2_ref[...] += acc2

        for jb in range(D // LANES):
            cols = slice(jb * LANES, (jb + 1) * LANES)
            acc = jnp.zeros((T, LANES), F32)
            for jp in _pairs_of(jb):
                gp = _pair(g3, jp, T, TP)
                dwc_ref[jp] += lax.dot_general(dy_s[:, cols], _pair(s3, jp, T, TP), TN, preferred_element_type=F32)
                dwb_ref[jp] += lax.dot_general(u_ref[:, cols], gp, TN, preferred_element_type=F32)
                acc += lax.dot_general(gp, wb_ref[jp], NT, preferred_element_type=F32)
            du_ref[:, cols] += acc

    rrow = pl.BlockSpec((T, D), lambda c: (nC - 1 - c, 0))
    aspec = pl.BlockSpec((N_GROUPS, LANES), lambda c: (0, 0))
    vec = pl.BlockSpec((1, D), lambda c: (0, 0))
    return _pcall(
        body, (hn, dyg, ypre, sbound, wb, wc, a1, a2, dvec), grid=(nC,),
        in_specs=[rrow, rrow, rrow, pl.BlockSpec((1, N_GROUPS, LANES), lambda c: (nC - 1 - c, 0, 0)),
                  _WB_SPEC, _WC_SPEC, aspec, aspec, vec],
        out_specs=[rrow, _WB_SPEC, _WC_SPEC, aspec, aspec, vec],
        out_shape=[jax.ShapeDtypeStruct((L, D), F32),
                   jax.ShapeDtypeStruct((HALF, LANES, 2 * LANES), F32),
                   jax.ShapeDtypeStruct((HALF, LANES, 2 * LANES), F32),
                   jax.ShapeDtypeStruct((N_GROUPS, LANES), F32), jax.ShapeDtypeStruct((N_GROUPS, LANES), F32),
                   jax.ShapeDtypeStruct((1, D), F32)],
        scratch_shapes=[pltpu.VMEM((N_GROUPS * TP, LANES), F32), pltpu.VMEM((N_GROUPS * TP, LANES), F32),
                        pltpu.VMEM((N_GROUPS, LANES), F32), pltpu.VMEM((T, D), BF16)],
        sem=("arbitrary",), name=name, xfers=xfers)


def _s5_prep(lam_re, lam_im, log_dt, b_re, b_im, c_re, c_im):
    dt = jnp.exp(log_dt)[:, None]
    mag = jnp.exp(lam_re * dt)
    lb_re = mag * jnp.cos(lam_im * dt)
    lb_im = mag * jnp.sin(lam_im * dt)
    den = lam_re * lam_re + lam_im * lam_im
    nr = lb_re - 1.0
    fr = ((nr * lam_re + lb_im * lam_im) / den)[..., None]
    fi = ((lb_im * lam_re - nr * lam_im) / den)[..., None]
    bb_re = fr * b_re - fi * b_im
    bb_im = fr * b_im + fi * b_re
    pair = lambda a: a.reshape(HALF, 2 * SSM_STATE)
    a1 = jnp.concatenate([pair(lb_re), pair(lb_re)], axis=0)
    a2 = jnp.concatenate([-pair(lb_im), pair(lb_im)], axis=0)

    quads = N_GROUPS // 4
    rows, cols = 4 * SSM_GROUP, 4 * SSM_STATE
    diag = (jnp.arange(rows)[:, None] // SSM_GROUP == jnp.arange(cols)[None, :] // SSM_STATE).astype(F32)
    place = jnp.eye(2, dtype=F32)

    def expand(w):
        blocks = jnp.tile(w.reshape(quads, rows, SSM_STATE), (1, 1, 4)) * diag
        return jnp.einsum('kq,gkrl->gkqrl', place, blocks.reshape(quads // 2, 2, rows, cols)
                          ).reshape(quads, LANES, cols)

    wb = jnp.concatenate([expand(bb_re.transpose(0, 2, 1)), expand(bb_im.transpose(0, 2, 1))], axis=0)
    wc = jnp.concatenate([expand(c_re), expand(-c_im)], axis=0)
    return a1, a2, wb, wc


def _tri(n, upper):
    r = lax.broadcasted_iota(jnp.int32, (n, n), 0)
    c = lax.broadcasted_iota(jnp.int32, (n, n), 1)
    return ((r <= c) if upper else (r >= c)).astype(F32)


def _fgate_fwd(fl, bf, *, name):
    L, W = fl.shape
    tr = _tile(L, CUM_TILE, SUBLANES)

    def body(f_ref, b_ref, o_ref, carry):
        @pl.when(pl.program_id(0) == 0)
        def _():
            carry[...] = jnp.zeros_like(carry)

        x = f_ref[...] + b_ref[...]
        ls = jnp.minimum(x, 0.0) - jnp.log(1.0 + jnp.exp(-jnp.abs(x)))
        cum = jnp.dot(_tri(tr, False), ls, preferred_element_type=F32, precision=lax.Precision.HIGHEST) + carry[...]
        o_ref[...] = cum
        carry[...] = cum[tr - 1:tr, :]

    return pl.pallas_call(
        body, grid=(L // tr,),
        in_specs=[pl.BlockSpec((tr, W), lambda i: (i, 0)), pl.BlockSpec((1, W), lambda i: (0, 0))],
        out_specs=pl.BlockSpec((tr, W), lambda i: (i, 0)), out_shape=jax.ShapeDtypeStruct((L, W), F32),
        scratch_shapes=[pltpu.VMEM((1, W), F32)], compiler_params=_params("arbitrary"), name=name)(fl, bf)


def _fgate_bwd(fl, bf, dcum, *, name):
    L, W = fl.shape
    tr = _tile(L, CUM_TILE, SUBLANES)
    n = L // tr

    def body(f_ref, b_ref, d_ref, o_ref, db_ref, carry):
        @pl.when(pl.program_id(0) == 0)
        def _():
            carry[...] = jnp.zeros_like(carry)
            db_ref[...] = jnp.zeros_like(db_ref)

        d = d_ref[...]
        rev = jnp.dot(_tri(tr, True), d, preferred_element_type=F32, precision=lax.Precision.HIGHEST) + carry[...]
        carry[...] += jnp.sum(d, axis=0, keepdims=True)
        df = rev * jax.nn.sigmoid(-(f_ref[...] + b_ref[...]))
        o_ref[...] = df
        db_ref[...] += jnp.sum(df, axis=0, keepdims=True)

    rrow = pl.BlockSpec((tr, W), lambda i: (n - 1 - i, 0))
    vec = pl.BlockSpec((1, W), lambda i: (0, 0))
    return pl.pallas_call(
        body, grid=(n,), in_specs=[rrow, vec, rrow], out_specs=[rrow, vec],
        out_shape=[jax.ShapeDtypeStruct((L, W), F32), jax.ShapeDtypeStruct((1, W), F32)],
        scratch_shapes=[pltpu.VMEM((1, W), F32)], compiler_params=_params("arbitrary"), name=name)(fl, bf, dcum)


_NT = (((1,), (1,)), ((), ()))
_TN = (((0,), (0,)), ((), ()))
HEAD_PAIRS = N_HEADS // 2


def _causal_tiles(n, by_row):
    pairs = ([(i, j) for i in range(n) for j in range(i + 1)] if by_row
             else [(i, j) for j in range(n) for i in range(j, n)])
    return (jnp.array([p[0] for p in pairs], jnp.int32), jnp.array([p[1] for p in pairs], jnp.int32))


def _as_row(col):
    return jnp.transpose(jnp.broadcast_to(col, (col.shape[0], LANES)))[0:1, :]


def _attn_logits(qs, k, ck, masked, t):
    s = lax.dot_general(qs, k, _NT, preferred_element_type=F32) - ck
    if masked:
        r = lax.broadcasted_iota(jnp.int32, (t, t), 0)
        c = lax.broadcasted_iota(jnp.int32, (t, t), 1)
        s = jnp.where(c > r, NEG, s)
    return s


def _attn_fwd(q, kv, ck, *, name, xfers=()):
    L, D = q.shape
    t = _tile(L, ATTN_TILE)
    n = L // t
    dh = HEAD_DIM

    def body(q_ref, k_ref, v_ref, ck_ref, o_ref, o32_ref, lse_ref, m_s, l_s, acc):
        i, j = pl.program_id(1), pl.program_id(2)

        @pl.when(j == 0)
        def _():
            m_s[...] = jnp.full_like(m_s, NEG)
            l_s[...] = jnp.zeros_like(l_s)
            acc[...] = jnp.zeros_like(acc)

        def tile(masked):
            for e in range(2):
                sl = slice(e * dh, (e + 1) * dh)
                v = v_ref[:, sl]
                s = _attn_logits(q_ref[:, sl] * ATTN_SCALE, k_ref[:, sl], ck_ref[e], masked, t)
                m_new = jnp.maximum(m_s[e], jnp.max(s, axis=1, keepdims=True))
                alpha = jnp.exp(m_s[e] - m_new)
                p = jnp.exp(s - m_new)
                l_s[e] = alpha * l_s[e] + jnp.sum(p, axis=1, keepdims=True)
                p_hi = p.astype(BF16)
                p_lo = (p - p_hi.astype(F32)).astype(BF16)
                pv = (jnp.dot(p_hi, v, preferred_element_type=F32) + jnp.dot(p_lo, v, preferred_element_type=F32))
                acc[e] = alpha * acc[e] + pv
                m_s[e] = m_new

        pl.when(j < i)(functools.partial(tile, False))
        pl.when(j == i)(functools.partial(tile, True))

        @pl.when(j == n - 1)
        def _():
            for e in range(2):
                sl = slice(e * dh, (e + 1) * dh)
                o = acc[e] / l_s[e]
                o_ref[:, sl] = o.astype(BF16)
                o32_ref[:, sl] = o
                lse_ref[e] = _as_row(m_s[e] + jnp.log(l_s[e]))

    qs = pl.BlockSpec((t, LANES), lambda h, i, j: (i, h))
    ks = pl.BlockSpec((t, LANES), lambda h, i, j: (jnp.minimum(i, j), h))
    vs = pl.BlockSpec((t, LANES), lambda h, i, j: (jnp.minimum(i, j), HEAD_PAIRS + h))
    cs = pl.BlockSpec((2, 1, t), lambda h, i, j: (h, 0, jnp.minimum(i, j)))
    return _pcall(
        body, (q, kv, kv, ck), grid=(HEAD_PAIRS, n, n), in_specs=[qs, ks, vs, cs],
        out_specs=[qs, qs, pl.BlockSpec((2, 1, t), lambda h, i, j: (h, 0, i))],
        out_shape=[jax.ShapeDtypeStruct((L, D), BF16), jax.ShapeDtypeStruct((L, D), F32),
                   jax.ShapeDtypeStruct((N_HEADS, 1, L), F32)],
        scratch_shapes=[pltpu.VMEM((2, t, 1), F32), pltpu.VMEM((2, t, 1), F32), pltpu.VMEM((2, t, dh), F32)],
        sem=("parallel", "parallel", "arbitrary"), name=name, xfers=xfers)


def _attn_delta(do, o, *, name):
    L, D = do.shape
    t = _tile(L, ATTN_TILE)

    def body(do_ref, o_ref, d_ref):
        prod = do_ref[...].astype(F32) * o_ref[...]
        head = lax.broadcasted_iota(jnp.int32, (N_HEADS, D), 0)
        col = lax.broadcasted_iota(jnp.int32, (N_HEADS, D), 1)
        sel = (col // HEAD_DIM == head).astype(F32)
        d_ref[:, 0, :] = lax.dot_general(sel, prod, _NT, preferred_element_type=F32, precision=lax.Precision.HIGHEST)

    row = pl.BlockSpec((t, D), lambda i: (i, 0))
    return pl.pallas_call(
        body, grid=(L // t,), in_specs=[row, row], out_specs=pl.BlockSpec((N_HEADS, 1, t), lambda i: (0, 0, i)),
        out_shape=jax.ShapeDtypeStruct((N_HEADS, 1, L), F32), compiler_params=_params("parallel"), name=name)(do, o)


def _attn_bwd(q, kv, ck_col, do, lse_row, delta_row, *, name, xfers=()):
    L, D = q.shape
    t = _tile(L, ATTN_TILE)
    n = L // t
    dh = HEAD_DIM

    def body(i_tab, j_tab, q_ref, k_ref, v_ref, ck_ref, do_ref, lse_ref, dl_ref, dq_ref, dk_ref, dv_ref, dck_ref,
             dck_s):
        i, j = i_tab[pl.program_id(1)], j_tab[pl.program_id(1)]

        @pl.when(pl.program_id(1) == 0)
        def _():
            dq_ref[...] = jnp.zeros_like(dq_ref)

        @pl.when(i == j)
        def _():
            dk_ref[...] = jnp.zeros_like(dk_ref)
            dv_ref[...] = jnp.zeros_like(dv_ref)
            dck_s[...] = jnp.zeros_like(dck_s)

        def tile(masked):
            cols = pl.ds(pl.multiple_of(i * t, t), t)
            for e in range(2):
                sl = slice(e * dh, (e + 1) * dh)
                qs = q_ref[:, sl] * ATTN_SCALE
                k = k_ref[:, sl]
                do = do_ref[:, sl]
                st = lax.dot_general(k, qs, _NT, preferred_element_type=F32) - ck_ref[e]
                if masked:
                    kpos = lax.broadcasted_iota(jnp.int32, (t, t), 0)
                    qpos = lax.broadcasted_iota(jnp.int32, (t, t), 1)
                    st = jnp.where(kpos > qpos, NEG, st)
                pt = jnp.exp(st - lse_ref[e])
                dpt = lax.dot_general(v_ref[:, sl], do, _NT, preferred_element_type=F32)
                dst = pt * (dpt - dl_ref[e])
                dst16 = dst.astype(BF16)
                dv_ref[:, sl] += jnp.dot(pt.astype(BF16), do, preferred_element_type=F32)
                dk_ref[:, sl] += jnp.dot(dst16, qs, preferred_element_type=F32)
                dck_s[e] -= jnp.sum(dst, axis=1, keepdims=True)
                dq_ref[sl, cols] += lax.dot_general(k, dst16, _TN, preferred_element_type=F32) * ATTN_SCALE

        pl.when(i > j)(functools.partial(tile, False))
        pl.when(i == j)(functools.partial(tile, True))

        @pl.when(i == n - 1)
        def _():
            for e in range(2):
                dck_ref[e] = _as_row(dck_s[e])

    qs = pl.BlockSpec((t, LANES), lambda h, s, it, jt: (it[s], h))
    ks = pl.BlockSpec((t, LANES), lambda h, s, it, jt: (jt[s], h))
    vs = pl.BlockSpec((t, LANES), lambda h, s, it, jt: (jt[s], HEAD_PAIRS + h))
    cs = pl.BlockSpec((2, t, 1), lambda h, s, it, jt: (h, jt[s], 0))
    ls = pl.BlockSpec((2, 1, t), lambda h, s, it, jt: (h, 0, it[s]))
    full = jax.ShapeDtypeStruct((L, D), F32)
    tabs = _causal_tiles(n, by_row=False)
    return _pcall(
        body, (q, kv, kv, ck_col, do, lse_row, delta_row), grid=(HEAD_PAIRS, tabs[0].shape[0]),
        in_specs=[qs, ks, vs, cs, qs, ls, ls],
        out_specs=[pl.BlockSpec((LANES, L), lambda h, s, it, jt: (h, 0)), ks, ks,
                   pl.BlockSpec((2, 1, t), lambda h, s, it, jt: (h, 0, jt[s]))],
        out_shape=[jax.ShapeDtypeStruct((D, L), F32), full, full, jax.ShapeDtypeStruct((N_HEADS, 1, L), F32)],
        scratch_shapes=[pltpu.VMEM((2, t, 1), F32)],
        sem=("parallel", "arbitrary"), name=name, xfers=xfers, prefetch=tabs)


SHARD_COLS_FFN = 2 * D_FF // N_DEV
SHARD_ROWS_FFN = D_FF // N_DEV
SHARD_COLS_GLU = 2 * D_MODEL // N_DEV
SHARD_ROWS_QO = D_MODEL // N_DEV
SHARD_COLS_KVF = (2 * D_MODEL + N_HEADS) // N_DEV
S5_NAMES = ("lam_re", "lam_im", "log_dt", "ssm_b_re", "ssm_b_im", "ssm_c_re", "ssm_c_im")
REPL_LATE_ROWS = 272
REPL_EARLY_ROWS = 304


def _leaves(parts):
    out = []
    for p in parts:
        out.extend(_leaves(p) if isinstance(p, (list, tuple)) else [p.reshape(-1)])
    return out


def _whole_rows(n):
    return -(-n // D_MODEL)


def _pack_rows(parts, rows):
    blocks = [jnp.pad(p, (0, _whole_rows(p.shape[0]) * D_MODEL - p.shape[0])).reshape(-1, D_MODEL)
              for p in _leaves(parts)]
    used = sum(b.shape[0] for b in blocks)
    return jnp.concatenate(blocks + [jnp.zeros((rows - used, D_MODEL), blocks[0].dtype)], axis=0)


def _unpack_rows(packed, like):
    out, row = [], 0
    for p in _leaves(like):
        n = _whole_rows(p.shape[0])
        out.append(packed[row:row + n].reshape(-1)[:p.shape[0]])
        row += n
    return out


def _repl_late(d):
    return [d["g_mix"][0], [d[n][0] for n in S5_NAMES]]


def _repl_early(d):
    return [list(d["g_mix"][1:]), list(d["g_ffn"]), [d[n][1] for n in S5_NAMES], d["g_kv"], d["b_f"],
            list(d["ffn_conv_b"]), d["g_final"]]


def _kvf_blocks(full):
    return full.reshape(D_MODEL, N_DEV, SHARD_COLS_KVF).transpose(1, 0, 2)


class _Step:
    def __init__(self, weights, send=None, plan=None):
        self.w = dict(weights)
        self.send = send or {}
        self.grad = {}
        self.bcast = {}
        self.slots = {}
        self.plan = plan or {}

    def xfers(self, host):
        src = {"w": self.send, "g": self.grad, "b": self.bcast}
        out = []
        for kind, k in self.plan.get(host, ()):
            if kind == "w" and k[0] == "w_ffn_in" and len(k) == 3:
                parts = _W_IN_PARTS[k[1]]
                out.append((self.send[k], False, (sum(parts), sum(parts[:k[2]]), self.w.get(("w_ffn_in_parts", k[1])))))
            else:
                out.append((src[kind][k], kind == "g"))
        return out

    def land(self, host, gathered):
        for (kind, k), g in zip(self.plan.get(host, ()), gathered):
            if kind == "w":
                self.arrive(k, g)
            else:
                self.slots[k] = g

    def arrive(self, k, g):
        name = k[0]
        if name == "w_ffn_in" and len(k) == 3:
            self.w["w_ffn_in_parts", k[1]] = g
            if k[2] == len(_W_IN_PARTS[k[1]]) - 1:
                self.w[name, k[1]] = g
        elif name == "w_ffn_out":
            self.w[k] = g.reshape(4, 2 * SHARD_ROWS_FFN, D_MODEL)
        elif name in ("w_q", "w_o"):
            self.w[k] = g.reshape(D_MODEL, D_MODEL)
        elif name == "w_kvf":
            full = g.transpose(1, 0, 2).reshape(D_MODEL, N_DEV * SHARD_COLS_KVF)
            self.w["w_kv",] = full[:, :2 * D_MODEL]
            self.w["w_f",] = jnp.pad(full[:, 2 * D_MODEL:], ((0, 0), (0, LANES - N_HEADS)))
        elif name == "small":
            flat = g.reshape(N_DEV, -1)
            self.w["ssm_d",] = flat[:, :256].reshape(N_DEV, N_A, LANES).transpose(1, 0, 2).reshape(N_A, D_MODEL)
            cw = flat[:, 256:256 + DEPTH * 3 * SHARD_COLS_FFN].reshape(N_DEV, DEPTH, 3, SHARD_COLS_FFN)
            for layer in range(DEPTH):
                self.w["conv_w", layer] = cw[:, layer].reshape(2, 4, 3, SHARD_COLS_FFN)
        else:
            self.w[k] = g

    def run(self, host, fn, *args, **kw):
        xf = self.xfers(host)
        res = fn(*args, name=host, xfers=xf, **kw)
        if not xf:
            return res[0] if isinstance(res, (list, tuple)) and len(res) == 1 else res
        n_own = len(res) - len(xf)
        self.land(host, res[n_own:])
        return res[0] if n_own == 1 else res[:n_own]


def _step(x, target, S):
    L = x.shape[0]
    W = S.w
    vec = lambda a: a.reshape(1, -1)
    CF = SHARD_COLS_FFN

    h = x
    saved = []
    kvs = None
    for layer in range(DEPTH):
        t = str(layer)
        if layer < N_A:
            (a1, a2, wb, wc), prep_vjp = jax.vjp(_s5_prep, *[W[n][layer] for n in S5_NAMES])
            wb16, wc16 = wb.astype(BF16), wc.astype(BF16)
            hn = _rms_fwd(h, vec(W["g_mix"][layer]), name="mix_norm" + t)
            dvec = vec(W["ssm_d",][layer])
            ypre, yg, sb = S.run("s5_fwd" + t, _s5_fwd, hn, wb16, wc16, a1, a2, dvec)
            z = S.run("glu_mm" + t, _mm, yg, W["w_glu", layer], bk="bkn", ok="bmn", tm=4096)
            z = z.reshape(2, 4, L, SHARD_COLS_GLU)
            h1, hn2 = _glu_res_rms(z, h, vec(W["g_ffn"][layer]), name="glu_res" + t)
            mix_saved = (h, hn, ypre, yg, sb, z, a1, a2, wb16, wc16, dvec, prep_vjp)
        else:
            j = layer - N_A
            if layer == N_A:
                hkv = _rms_fwd(h, vec(W["g_kv"]), name="kv_norm")
                kvm = S.run("kv_mm", _mm, hkv, W["w_kv",], out_dtype=BF16)
                fl = S.run("f_mm", _mm, hkv, W["w_f",])
                cum = _fgate_fwd(fl, W["b_f_pad",], name="fgate_fwd")
                ck = cum[:, :N_HEADS].T.reshape(N_HEADS, 1, L)
                kvs = (h, hkv, fl, kvm, ck)
            _, _, _, kvm, ck = kvs
            hn = _rms_fwd(h, vec(W["g_mix"][layer]), name="mix_norm" + t)
            q = S.run("q_mm" + t, _mm, hn, W["w_q", j], out_dtype=BF16)
            o, o32, lse = S.run("attn_fwd" + t, _attn_fwd, q, kvm, ck)
            h1 = S.run("o_mm" + t, _mm, o, W["w_o", j], add=h)
            hn2 = _rms_fwd(h1, vec(W["g_ffn"][layer]), name="ffn_norm" + t)
            mix_saved = (h, hn, q, o32, o, lse)
        u0 = S.run("ffn_in" + t, _mm, hn2, W["w_ffn_in", layer], bk="nbk", ok="bmn", tm=2048).reshape(2, 4, L, CF)
        a = S.run("ffn_act" + t, _conv_act, u0, W["conv_w", layer], W["conv_b", layer])
        h2 = S.run("ffn_out" + t, _mm, a, W["w_ffn_out", layer], ak="bmk", bk="kbn", add=h1, tn=512, kg=4)
        saved.append((mix_saved, h1, hn2, u0))
        h = h2

    loss, dh, dg_final = _loss_head(h, vec(W["g_final"]), target, name="loss_head")
    g = {"g_final": dg_final.reshape(-1)}
    gl = {k: [None] * DEPTH for k in ("g_mix", "g_ffn", "conv_w", "ffn_conv_b")}
    ga = {k: [None] * N_A for k in S5_NAMES + ("ssm_d",)}
    dk = dv = dck = None
    for layer in reversed(range(DEPTH)):
        t = str(layer)
        mix_saved, h1, hn2, u0 = saved[layer]
        cw, cb = W["conv_w", layer], W["conv_b", layer]
        da = S.run("ffn_da" + t, _mm, dh, W["w_ffn_out", layer], bk="nbk", ok="bmn", tm=2048)
        a, du0, dcw, dcb = S.run("ffn_conv_bwd" + t, _conv_ffn_bwd, u0, cw, cb, da)
        dw_out = S.run("ffn_dwout" + t, _mm, a, dh, ak="bkm", ok="mbn", out_dtype=BF16)
        S.grad["w_ffn_out", layer] = dw_out.reshape(N_DEV, SHARD_ROWS_FFN, D_MODEL)
        du0 = du0.reshape(N_DEV, L, CF)
        S.grad["w_ffn_in", layer] = S.run("ffn_dwin" + t, _mm, du0, hn2, ak="bkm", ok="mbn", out_dtype=BF16)
        dhn2 = S.run("ffn_dhn" + t, _mm, du0, W["w_ffn_in", layer], ak="bmk", bk="kbn", kg=4)
        dh1, dg = _rms_bwd(h1, vec(W["g_ffn"][layer]), dhn2, dh, name="ffn_norm_bwd" + t)
        gl["g_ffn"][layer], gl["conv_w"][layer], gl["ffn_conv_b"][layer] = dg.reshape(-1), dcw, dcb.reshape(-1)
        if layer < N_A:
            hin, hn, ypre, yg, sb, z, a1, a2, wb16, wc16, dvec, prep_vjp = mix_saved
            dz = _glu_bwd(z, dh1, name="glu_bwd" + t).reshape(N_DEV, L, SHARD_COLS_GLU)
            S.grad["w_glu", layer] = S.run("glu_dw" + t, _mm, yg, dz, ak="km", bk="bkn", ok="bmn", out_dtype=BF16,
                                           tk=4096)
            dyg = S.run("glu_dy" + t, _mm, dz, W["w_glu", layer], ak="bmk", bk="bnk", out_dtype=BF16, tm=2048, kg=8)
            if layer == 0:
                S.bcast["repl_early",] = _pack_rows(_repl_early({**g, **gl, **ga}), REPL_EARLY_ROWS).astype(BF16)
            du, dwb, dwc, da1, da2, dd = S.run("s5_bwd" + t, _s5_bwd, hn, dyg, ypre, sb, wb16, wc16, a1, a2, dvec)
            for nme, val in zip(S5_NAMES, prep_vjp((da1, da2, dwb, dwc))):
                ga[nme][layer] = val
            ga["ssm_d"][layer] = dd.reshape(-1)
            dh, dg = _rms_bwd(hin, vec(W["g_mix"][layer]), du, dh1, name="mix_norm_bwd" + t)
        else:
            j = layer - N_A
            hin, hn, q, o32, o, lse = mix_saved
            _, _, _, kvm, ck = kvs
            S.grad["w_o", j] = S.run("o_dw" + t, _mm, o, dh1, ak="km", out_dtype=BF16
                                     ).reshape(N_DEV, SHARD_ROWS_QO, D_MODEL)
            do = S.run("o_dx" + t, _mm, dh1, W["w_o", j], bk="nk", out_dtype=BF16)
            delta = _attn_delta(do, o32, name="attn_delta" + t)
            dq_t, dk_l, dv_l, dck_l = S.run("attn_bwd" + t, _attn_bwd, q, kvm, ck.reshape(N_HEADS, L, 1), do,
                                            lse, delta)
            dk = dk_l if dk is None else dk + dk_l
            dv = dv_l if dv is None else dv + dv_l
            dck = dck_l if dck is None else dck + dck_l
            S.grad["w_q", j] = S.run("q_dw" + t, _mm, hn, dq_t, ak="km", bk="nk", out_dtype=BF16
                                     ).reshape(N_DEV, SHARD_ROWS_QO, D_MODEL)
            dhn = S.run("q_dx" + t, _mm, dq_t, W["w_q", j], ak="km", bk="nk")
            dh, dg = _rms_bwd(hin, vec(W["g_mix"][layer]), dhn, dh1, name="mix_norm_bwd" + t)
            if layer == N_A:
                hkv_in, hkv, fl, _, _ = kvs
                dcum = jnp.pad(dck.reshape(N_HEADS, L).T, ((0, 0), (0, LANES - N_HEADS)))
                dfl, dbf = _fgate_bwd(fl, W["b_f_pad",], dcum, name="fgate_bwd")
                dkv = jnp.concatenate([dk, dv], axis=1).astype(BF16)
                dfl16 = dfl.astype(BF16)
                dw_kv = S.run("kv_dw", _mm, hkv, dkv, ak="km")
                dw_f = S.run("f_dw", _mm, hkv, dfl16, ak="km")
                S.grad["w_kvf",] = _kvf_blocks(jnp.concatenate([dw_kv, dw_f[:, :N_HEADS]], axis=1)).astype(BF16)
                dhkv = S.run("kv_dx", _mm, dkv, W["w_kv",], bk="nk")
                dhkv = S.run("f_dx", _mm, dfl16, W["w_f",], bk="nk", add=dhkv)
                g["b_f"] = dbf[0, :N_HEADS]
                dh, dgkv = _rms_bwd(hkv_in, vec(W["g_kv"]), dhkv, dh, name="kv_norm_bwd")
                g["g_kv"] = dgkv.reshape(-1)
        gl["g_mix"][layer] = dg.reshape(-1)

    for d in (gl, ga):
        for k, v in d.items():
            g[k] = jnp.stack(v)
    return loss, dh, g


def _adamw_layers(slots, w, m, v, *, name):
    shape = w.shape
    nl = len(slots)
    w, m, v = (a.reshape((nl,) + a.shape[-2:]) for a in (w, m, v))
    _, R, C = w.shape
    tr = _tile(R, 256, 16)
    c1 = 1.0 / (1.0 - ADAM_B1 ** ADAM_STEP)
    c2 = 1.0 / (1.0 - ADAM_B2 ** ADAM_STEP)

    def body(*refs):
        s_refs, (w_ref, m_ref, v_ref), (g_ref, d_ref, nm_ref, nv_ref) = refs[:nl], refs[nl:nl + 3], refs[nl + 3:]
        for layer in range(nl):
            @pl.when(pl.program_id(0) == layer)
            def _(s_ref=s_refs[layer]):
                g = s_ref[0].astype(F32)
                for d in range(1, N_DEV):
                    g = g + s_ref[d].astype(F32)
                m2 = ADAM_B1 * m_ref[...] + (1.0 - ADAM_B1) * g
                v2 = ADAM_B2 * v_ref[...] + (1.0 - ADAM_B2) * (g * g)
                g_ref[...] = g
                nm_ref[...] = m2
                nv_ref[...] = v2
                d_ref[...] = -ADAM_LR * ((m2 * c1) / (jnp.sqrt(v2 * c2) + ADAM_EPS) + ADAM_WD * w_ref[...])

    def slab_spec(layer):
        return pl.BlockSpec((N_DEV, tr, C), lambda l, i: (0, jnp.where(l == layer, i, 0), 0))

    row = pl.BlockSpec((None, tr, C), lambda l, i: (l, i, 0))
    out = jax.ShapeDtypeStruct((nl, R, C), F32)
    outs = pl.pallas_call(
        body, grid=(nl, R // tr), in_specs=[slab_spec(layer) for layer in range(nl)] + [row, row, row],
        out_specs=[row, row, row, row], out_shape=[out, out, out, out],
        compiler_params=_params("arbitrary", "arbitrary"), name=name)(*slots, w, m, v)
    return [o.reshape(shape) for o in outs]


_SMALL_ROWS = 72
_ORDER = ("g_mix", "g_ffn", "lam_re", "lam_im", "log_dt", "ssm_b_re", "ssm_b_im", "ssm_c_re", "ssm_c_im", "ssm_d",
          "w_glu", "g_kv", "w_kvf", "b_f", "w_q", "w_o", "w_ffn_in", "ffn_conv_w", "ffn_conv_b", "w_ffn_out", "g_final")


def _pack_small(ssm_d, conv_w):
    flat = jnp.concatenate([ssm_d.reshape(-1), conv_w.reshape(-1)])
    return jnp.pad(flat, (0, _SMALL_ROWS * LANES - flat.shape[0])).reshape(_SMALL_ROWS, LANES)


def _unpack_small(flat):
    flat = flat.reshape(-1)
    return flat[:256].reshape(2, 128), flat[256:256 + 8448].reshape(4, 3, 704)


_W_IN_PARTS = {0: (352, 352), 1: (176, 528)}
_FWD_PLAN = {
    "start": [("small",)],
    "s5_fwd0": [("w_glu", 0), ("w_ffn_in", 0, 0)],
    "glu_mm0": [("w_ffn_in", 0, 1)],
    "ffn_in0": [("w_ffn_out", 0)],
    "ffn_act0": [("w_glu", 1)],
    "ffn_out0": [("w_ffn_in", 1, 0)],
    "s5_fwd1": [("w_ffn_in", 1, 1)],
    "ffn_in1": [("w_ffn_out", 1)],
    "ffn_act1": [("w_kvf",), ("w_o", 0)],
    "ffn_out1": [("w_q", 0)],
    "attn_fwd2": [("w_ffn_in", 2), ("w_ffn_out", 2), ("w_q", 1), ("w_o", 1)],
    "attn_fwd3": [("w_ffn_in", 3), ("w_ffn_out", 3)],
}
_BWD_PLAN = {
    "ffn_dhn3": [("w_ffn_out", 3)],
    "attn_bwd3": [("w_ffn_in", 3), ("w_o", 1)],
    "ffn_conv_bwd2": [("w_q", 1)],
    "ffn_dhn2": [("w_ffn_out", 2)],
    "attn_bwd2": [("w_ffn_in", 2), ("w_o", 0)],
    "ffn_conv_bwd1": [("w_q", 0), ("w_kvf",)],
    "ffn_dhn1": [("w_ffn_out", 1)],
    "s5_bwd1": [("w_ffn_in", 1), ("w_glu", 1)],
    "ffn_dhn0": [("w_ffn_out", 0)],
    "s5_bwd0": [("w_ffn_in", 0), ("w_glu", 0), ("repl_early",)],
    "end": [("small",), ("repl_late",)],
}
_PLAN = {h: [("w", k) for k in ks] for h, ks in _FWD_PLAN.items()}
_PLAN.update({h: [("b" if k[0].startswith("repl") else "g", k) for k in ks] for h, ks in _BWD_PLAN.items()})


def kernel(x, g_mix, g_ffn, lam_re, lam_im, log_dt, ssm_b_re, ssm_b_im, ssm_c_re, ssm_c_im, ssm_d, w_glu, g_kv, w_kvf, b_f, w_q, w_o, w_ffn_in, ffn_conv_w, ffn_conv_b, w_ffn_out, g_final, loss_target, m_g_mix, m_g_ffn, m_lam_re, m_lam_im, m_log_dt, m_ssm_b_re, m_ssm_b_im, m_ssm_c_re, m_ssm_c_im, m_ssm_d, m_w_glu, m_g_kv, m_w_kvf, m_b_f, m_w_q, m_w_o, m_w_ffn_in, m_ffn_conv_w, m_ffn_conv_b, m_w_ffn_out, m_g_final, v_g_mix, v_g_ffn, v_lam_re, v_lam_im, v_log_dt, v_ssm_b_re, v_ssm_b_im, v_ssm_c_re, v_ssm_c_im, v_ssm_d, v_w_glu, v_g_kv, v_w_kvf, v_b_f, v_w_q, v_w_o, v_w_ffn_in, v_ffn_conv_w, v_ffn_conv_b, v_w_ffn_out, v_g_final):
    wts = dict(g_mix=g_mix, g_ffn=g_ffn, lam_re=lam_re, lam_im=lam_im, log_dt=log_dt, ssm_b_re=ssm_b_re,
               ssm_b_im=ssm_b_im, ssm_c_re=ssm_c_re, ssm_c_im=ssm_c_im, ssm_d=ssm_d, w_glu=w_glu, g_kv=g_kv,
               w_kvf=w_kvf, b_f=b_f, w_q=w_q, w_o=w_o, w_ffn_in=w_ffn_in, ffn_conv_w=ffn_conv_w,
               ffn_conv_b=ffn_conv_b, w_ffn_out=w_ffn_out, g_final=g_final)
    mom = dict(g_mix=m_g_mix, g_ffn=m_g_ffn, lam_re=m_lam_re, lam_im=m_lam_im, log_dt=m_log_dt, ssm_b_re=m_ssm_b_re,
               ssm_b_im=m_ssm_b_im, ssm_c_re=m_ssm_c_re, ssm_c_im=m_ssm_c_im, ssm_d=m_ssm_d, w_glu=m_w_glu,
               g_kv=m_g_kv, w_kvf=m_w_kvf, b_f=m_b_f, w_q=m_w_q, w_o=m_w_o, w_ffn_in=m_w_ffn_in,
               ffn_conv_w=m_ffn_conv_w, ffn_conv_b=m_ffn_conv_b, w_ffn_out=m_w_ffn_out, g_final=m_g_final)
    var = dict(g_mix=v_g_mix, g_ffn=v_g_ffn, lam_re=v_lam_re, lam_im=v_lam_im, log_dt=v_log_dt, ssm_b_re=v_ssm_b_re,
               ssm_b_im=v_ssm_b_im, ssm_c_re=v_ssm_c_re, ssm_c_im=v_ssm_c_im, ssm_d=v_ssm_d, w_glu=v_w_glu,
               g_kv=v_g_kv, w_kvf=v_w_kvf, b_f=v_b_f, w_q=v_w_q, w_o=v_w_o, w_ffn_in=v_w_ffn_in,
               ffn_conv_w=v_ffn_conv_w, ffn_conv_b=v_ffn_conv_b, w_ffn_out=v_w_ffn_out, g_final=v_g_final)
    kinds = ("grad", "delta", "m", "v")

    ready = {n: wts[n] for n in ("g_mix", "g_ffn", "g_kv", "g_final") + S5_NAMES}
    ready["b_f_pad",] = jnp.pad(b_f, (0, LANES - N_HEADS)).reshape(1, LANES)
    send = {("small",): _pack_small(ssm_d, ffn_conv_w), ("w_kvf",): w_kvf.astype(BF16)}
    for layer in range(DEPTH):
        ready["conv_b", layer] = ffn_conv_b[layer].reshape(2, 4, 1, SHARD_COLS_FFN)
        w_in_t = jnp.swapaxes(w_ffn_in[layer], 0, 1).astype(BF16)
        if layer in _W_IN_PARTS:
            row = 0
            for p, rows in enumerate(_W_IN_PARTS[layer]):
                send["w_ffn_in", layer, p] = w_in_t[row:row + rows]
                row += rows
        else:
            send["w_ffn_in", layer] = w_in_t
        send["w_ffn_out", layer] = w_ffn_out[layer].astype(BF16)
    for layer in range(N_A):
        send["w_glu", layer] = w_glu[layer].astype(BF16)
        send["w_q", layer] = w_q[layer].astype(BF16)
        send["w_o", layer] = w_o[layer].astype(BF16)

    S = _Step(ready, send, _PLAN)
    S.land("start", _exchange(S.xfers("start"), name="start"))
    loss, dx, g = _step(x[0], loss_target[0], S)
    loss = lax.psum(loss[0, 0], MESH_AXES)

    g_d = g["ssm_d"].reshape(N_A, N_DEV, LANES).transpose(1, 0, 2).reshape(N_DEV, N_A * LANES)
    g_cw = jnp.stack([g["conv_w"][layer].reshape(N_DEV, 3, SHARD_COLS_FFN) for layer in range(DEPTH)], axis=1)
    g_small = jnp.concatenate([g_d, g_cw.reshape(N_DEV, -1)], axis=1)
    g_small = jnp.pad(g_small, ((0, 0), (0, _SMALL_ROWS * LANES - g_small.shape[1])))
    S.grad["small",] = g_small.reshape(N_DEV, _SMALL_ROWS, LANES)
    S.bcast["repl_late",] = _pack_rows(_repl_late(g), REPL_LATE_ROWS).astype(BF16)
    S.land("end", _exchange(S.xfers("end"), name="end"))

    res = {}
    for name, nl in (("w_glu", N_A), ("w_q", DEPTH - N_A), ("w_o", DEPTH - N_A), ("w_ffn_in", DEPTH),
                     ("w_ffn_out", DEPTH)):
        view = (lambda a: jnp.swapaxes(a, 1, 2)) if name == "w_ffn_in" else (lambda a: a)
        outs = _adamw_layers([S.slots[name, layer] for layer in range(nl)], view(wts[name]), view(mom[name]),
                             view(var[name]), name="adamw_" + name)
        res.update({(kind, name): view(a) for kind, a in zip(kinds, outs)})
    outs = _adamw_layers([S.slots["w_kvf",]], w_kvf, m_w_kvf, v_w_kvf, name="adamw_w_kvf")
    res.update({(kind, "w_kvf"): a for kind, a in zip(kinds, outs)})
    outs = _adamw_layers([S.slots["small",]], _pack_small(ssm_d, ffn_conv_w), _pack_small(m_ssm_d, m_ffn_conv_w),
                         _pack_small(v_ssm_d, v_ffn_conv_w), name="adamw_small")
    for kind, flat in zip(kinds, outs):
        res[kind, "ssm_d"], res[kind, "ffn_conv_w"] = _unpack_small(flat)

    pieces = {}
    for key, rows, sel in ((("repl_early",), REPL_EARLY_ROWS, _repl_early), (("repl_late",), REPL_LATE_ROWS, _repl_late)):
        outs = _adamw_layers([S.slots[key]], *[_pack_rows(sel(d), rows) for d in (wts, mom, var)],
                             name="adamw_" + key[0])
        for kind, flat in zip(kinds, outs):
            pieces[kind, key[0]] = _unpack_rows(flat, sel(wts))
    for kind in kinds:
        early, late = iter(pieces[kind, "repl_early"]), iter(pieces[kind, "repl_late"])
        take = lambda it, n: [next(it) for _ in range(n)]
        res[kind, "g_mix"] = jnp.stack(take(late, 1) + take(early, DEPTH - 1))
        res[kind, "g_ffn"] = jnp.stack(take(early, DEPTH))
        for n in S5_NAMES:
            res[kind, n] = jnp.stack([next(late), next(early)]).reshape(wts[n].shape)
        res[kind, "g_kv"], res[kind, "b_f"] = next(early), next(early)
        res[kind, "ffn_conv_b"] = jnp.stack(take(early, DEPTH))
        res[kind, "g_final"] = next(early)

    return (loss, dx[None], *[res[kind, n] for kind in kinds for n in _ORDER])
```

```python
import functools
import math

import jax
import jax.numpy as jnp
from jax import lax
from jax.experimental import pallas as pl
from jax.experimental.pallas import tpu as pltpu

F32 = jnp.float32
BF16 = jnp.bfloat16

D_MODEL = 1024
DEPTH = 4
N_A = 2
N_GROUPS = 64
SSM_GROUP = 16
SSM_STATE = 64
N_HEADS = 16
HEAD_DIM = 64
ATTN_SCALE = HEAD_DIM ** -0.5
D_FF = 2816
EPS = 1e-6
N_DEV = 8
LANES = 128
SUBLANES = 8

ADAM_LR = 0.001
ADAM_B1 = 0.9
ADAM_B2 = 0.999
ADAM_EPS = 1e-08
ADAM_WD = 0.01
ADAM_STEP = 10

ROW_TILE = 512
S5_CHUNK = 256
ATTN_TILE = 512
CUM_TILE = 256
NEG = -1e30

MESH_AXES = ("x", "y", "c")


def _tile(n, target, align=LANES):
    t = (min(target, n) // align) * align
    while t >= align:
        if n % t == 0:
            return t
        t -= align
    return n


def _params(*sem):
    return pltpu.CompilerParams(dimension_semantics=sem, vmem_limit_bytes=56 * 1024 * 1024)


_ANY = pl.BlockSpec(memory_space=pl.ANY)
_XFER_SEMS = (pltpu.SemaphoreType.DMA((N_DEV - 1,)), pltpu.SemaphoreType.DMA((N_DEV - 1,)), pltpu.SemaphoreType.DMA)


def _xfer_copies(x_ref, o_ref, send_sems, recv_sems, local_sem, scatter, row_off):
    xi, yi, ci = lax.axis_index("x"), lax.axis_index("y"), lax.axis_index("c")
    me = 4 * xi + 2 * yi + ci

    def src(p):
        return x_ref.at[p] if scatter else x_ref

    def dst(p):
        return o_ref.at[p] if row_off is None else o_ref.at[p, pl.ds(row_off, x_ref.shape[0])]

    own = pltpu.make_async_copy(src(me), dst(me), local_sem)
    sends, recvs = [], []
    for k in range(1, N_DEV):
        px, py, pc = xi ^ (k >> 2), yi ^ ((k >> 1) & 1), ci ^ (k & 1)
        p = 4 * px + 2 * py + pc
        sends.append(pltpu.make_async_remote_copy(
            src_ref=src(p), dst_ref=dst(me), send_sem=send_sems.at[k - 1], recv_sem=recv_sems.at[k - 1],
            device_id=(px, py, pc), device_id_type=pl.DeviceIdType.MESH))
        recvs.append(pltpu.make_async_remote_copy(
            src_ref=src(p), dst_ref=dst(p), send_sem=send_sems.at[k - 1], recv_sem=recv_sems.at[k - 1],
            device_id=(px, py, pc), device_id_type=pl.DeviceIdType.MESH))
    return own, sends, recvs


def _xfer_start(*refs, scatter, row_off):
    own, sends, _ = _xfer_copies(*refs, scatter, row_off)
    own.start()
    for cp in sends:
        cp.start()


def _xfer_wait(*refs, scatter, row_off):
    own, sends, recvs = _xfer_copies(*refs, scatter, row_off)
    for cp in recvs:
        cp.wait_recv()
    for cp in sends:
        cp.wait_send()
    own.wait()


def _pcall(body, args, *, grid, in_specs, out_specs, out_shape, scratch_shapes=(), sem, name, xfers=(), prefetch=()):
    out_specs, out_shape = list(out_specs), list(out_shape)
    xfers = [tuple(x) + (None,) * (3 - len(x)) for x in xfers]
    n_pre, n_in, n_out, n_x, n_scr = len(prefetch), len(in_specs), len(out_specs), len(xfers), len(scratch_shapes)
    flags = [(s, None if w is None else w[1]) for _, s, w in xfers]
    prevs = [(t, w[2]) for t, (_, _, w) in enumerate(xfers) if w is not None and w[2] is not None]
    n_b = len(prevs)
    assert not (prevs and prefetch)

    def xfer_shape(x, scatter, w):
        if w is not None:
            return jax.ShapeDtypeStruct((N_DEV, w[0]) + x.shape[1:], x.dtype)
        return jax.ShapeDtypeStruct((N_DEV,) + (x.shape[1:] if scatter else x.shape), x.dtype)

    def wrapped(*refs):
        pre, refs = refs[:n_pre], refs[n_pre:]
        ins, xin = refs[:n_in], refs[n_in:n_in + n_x]
        refs = refs[n_in + n_x + n_b:]
        outs, xout, scr = refs[:n_out], refs[n_out:n_out + n_x], refs[n_out + n_x:]
        own, sems = scr[:n_scr], scr[n_scr:]
        ids = [pl.program_id(d) for d in range(len(grid))]
        first = functools.reduce(jnp.logical_and, [i == 0 for i in ids])
        last = functools.reduce(jnp.logical_and, [i == g - 1 for i, g in zip(ids, grid)])

        @pl.when(first)
        def _():
            for t in range(n_x):
                _xfer_start(xin[t], xout[t], *sems[3 * t:3 * t + 3], scatter=flags[t][0], row_off=flags[t][1])

        body(*pre, *ins, *outs, *own)

        @pl.when(last)
        def _():
            for t in range(n_x):
                _xfer_wait(xin[t], xout[t], *sems[3 * t:3 * t + 3], scatter=flags[t][0], row_off=flags[t][1])

    grid_spec = pltpu.PrefetchScalarGridSpec(
        num_scalar_prefetch=n_pre, grid=grid, in_specs=list(in_specs) + [_ANY] * (n_x + n_b),
        out_specs=out_specs + [_ANY] * n_x, scratch_shapes=list(scratch_shapes) + list(_XFER_SEMS) * n_x)
    return pl.pallas_call(
        wrapped if xfers else body, grid_spec=grid_spec, out_shape=out_shape + [xfer_shape(*x) for x in xfers],
        input_output_aliases={n_in + n_x + b: n_out + t for b, (t, _) in enumerate(prevs)},
        compiler_params=_params(*(["arbitrary"] * len(grid) if xfers else sem)), name=name,
    )(*prefetch, *args, *[x[0] for x in xfers], *[p for _, p in prevs])


def _exchange(xfers, *, name):
    def body():
        pass

    return _pcall(body, (), grid=(1,), in_specs=[], out_specs=[], out_shape=[], sem=("arbitrary",), name=name,
                  xfers=xfers)


def _mm(a, b, *, ak="mk", bk="kn", ok="mn", add=None, norm_g=None, rms_bwd=None, out_dtype=F32, tm=1024, tn=1024,
        tk=1024, kg=1, name, xfers=()):
    sa, sb = a.shape, b.shape
    fm = fn = fk = None
    if ak == "mk":
        M, K, a_c = sa[0], sa[1], 1
    elif ak == "km":
        K, M, a_c = sa[0], sa[1], 0
    elif ak == "bmk":
        M, K, a_c, fk = sa[1], sa[0] * sa[2], 1, sa[2]
    else:
        K, M, a_c, fm = sa[1], sa[0] * sa[2], 0, sa[2]
    if bk == "kn":
        N, b_c = sb[1], 0
    elif bk == "nk":
        N, b_c = sb[0], 1
    elif bk == "bkn":
        N, b_c, fn = sb[0] * sb[2], 0, sb[2]
    elif bk == "bnk":
        N, b_c, fk = sb[1], 1, sb[2]
    elif bk == "kbn":
        N, b_c, fk = sb[2], 0, sb[1]
    else:
        N, b_c, fn = sb[0] * sb[1], 1, sb[1]
    tm, tn, tk = fm or _tile(M, tm), fn or _tile(N, tn), fk or _tile(K, tk)
    kblk = None if kg == 1 else kg
    nm, nn, nk = M // tm, N // tn, K // (tk * kg)

    a_spec = {"mk": pl.BlockSpec((tm, tk), lambda i, j, k: (i, k)),
              "km": pl.BlockSpec((tk, tm), lambda i, j, k: (k, i)),
              "bmk": pl.BlockSpec((kblk, tm, tk), lambda i, j, k: (k, i, 0)),
              "bkm": pl.BlockSpec((None, tk, tm), lambda i, j, k: (i, k, 0))}[ak]
    b_spec = {"kn": pl.BlockSpec((tk, tn), lambda i, j, k: (k, j)),
              "nk": pl.BlockSpec((tn, tk), lambda i, j, k: (j, k)),
              "bkn": pl.BlockSpec((None, tk, tn), lambda i, j, k: (j, k, 0)),
              "bnk": pl.BlockSpec((kblk, tn, tk), lambda i, j, k: (k, j, 0)),
              "kbn": pl.BlockSpec((kblk, tk, tn), lambda i, j, k: (k, 0, j)),
              "nbk": pl.BlockSpec((None, tn, tk), lambda i, j, k: (j, 0, k))}[bk]
    if ok == "mn":
        o_spec = pl.BlockSpec((tm, tn), lambda i, j, k: (i, j))
        out_shape = jax.ShapeDtypeStruct((M, N), out_dtype)
    elif ok == "bmn":
        o_spec = pl.BlockSpec((None, tm, tn), lambda i, j, k: (j, i, 0))
        out_shape = jax.ShapeDtypeStruct((nn, M, tn), out_dtype)
    else:
        o_spec = pl.BlockSpec((None, tm, tn), lambda i, j, k: (i, 0, j))
        out_shape = jax.ShapeDtypeStruct((nm, tm, N), out_dtype)
    dims = (((a_c,), (b_c,)), ((), ()))
    has_add = add is not None
    n_extra_in = int(has_add) + int(norm_g is not None) + (3 if rms_bwd is not None else 0)
    if norm_g is not None or rms_bwd is not None:
        assert ok == "mn" and tn == N

    def body(*refs):
        a_ref, b_ref = refs[0], refs[1]
        extra = list(refs[2:2 + n_extra_in])
        add_ref = extra.pop(0) if has_add else None
        ng_ref = extra.pop(0) if norm_g is not None else None
        outs = list(refs[2 + n_extra_in:])
        o_ref = outs.pop(0)
        hn_ref = outs.pop(0) if norm_g is not None else None
        dg_ref = outs.pop(0) if rms_bwd is not None else None
        if kg == 1:
            part = lax.dot_general(a_ref[...].astype(BF16), b_ref[...].astype(BF16), dims, preferred_element_type=F32)
        else:
            part = sum(lax.dot_general(a_ref[g].astype(BF16), b_ref[g].astype(BF16), dims,
                                       preferred_element_type=F32) for g in range(kg))
        if rms_bwd is not None:
            @pl.when(jnp.logical_and(pl.program_id(0) == 0, pl.program_id(2) == 0))
            def _():
                dg_ref[...] = jnp.zeros_like(dg_ref)

        def finish(r):
            if has_add:
                r = r + add_ref[...]
            if rms_bwd is not None:
                h_ref, g_ref, dres_ref = extra
                x = h_ref[...]
                rr = lax.rsqrt(jnp.mean(x * x, axis=1, keepdims=True) + EPS)
                xn = x * rr
                gdy = r * g_ref[...]
                dg_ref[...] += jnp.sum(r * xn, axis=0, keepdims=True)
                r = dres_ref[...] + rr * (gdy - xn * jnp.mean(gdy * xn, axis=1, keepdims=True))
            o_ref[...] = r.astype(out_dtype)
            if norm_g is not None:
                hn_ref[...] = (r * lax.rsqrt(jnp.mean(r * r, axis=1, keepdims=True) + EPS) * ng_ref[...]).astype(BF16)

        if nk == 1:
            finish(part)
            return
        acc = refs[-1]
        k = pl.program_id(2)

        @pl.when(k == 0)
        def _():
            acc[...] = part

        @pl.when(k > 0)
        def _():
            acc[...] += part

        @pl.when(k == nk - 1)
        def _():
            finish(acc[...])

    tile_spec = pl.BlockSpec((tm, tn), lambda i, j, k: (i, j))
    vec_spec = pl.BlockSpec((1, tn), lambda i, j, k: (0, j))
    in_specs, args = [a_spec, b_spec], [a, b]
    out_specs, out_shapes = [o_spec], [out_shape]
    if has_add:
        in_specs.append(tile_spec)
        args.append(add)
    if norm_g is not None:
        in_specs.append(vec_spec)
        args.append(norm_g)
        out_specs.append(tile_spec)
        out_shapes.append(jax.ShapeDtypeStruct((M, N), BF16))
    if rms_bwd is not None:
        in_specs += [tile_spec, vec_spec, tile_spec]
        args += list(rms_bwd)
        out_specs.append(vec_spec)
        out_shapes.append(jax.ShapeDtypeStruct((1, N), F32))
    res = _pcall(body, args, grid=(nm, nn, nk), in_specs=in_specs, out_specs=out_specs, out_shape=out_shapes,
                 scratch_shapes=[pltpu.VMEM((tm, tn), F32)] if nk > 1 else [],
                 sem=("arbitrary" if rms_bwd is not None else "parallel", "parallel", "arbitrary"), name=name,
                 xfers=xfers)
    return res if (xfers or len(out_specs) > 1) else res[0]


def _rms_fwd(h, g, *, name):
    L, D = h.shape
    tr = _tile(L, ROW_TILE, SUBLANES)

    def body(h_ref, g_ref, o_ref):
        x = h_ref[...]
        r = lax.rsqrt(jnp.mean(x * x, axis=1, keepdims=True) + EPS)
        o_ref[...] = (x * r * g_ref[...]).astype(BF16)

    return pl.pallas_call(
        body, grid=(L // tr,),
        in_specs=[pl.BlockSpec((tr, D), lambda i: (i, 0)), pl.BlockSpec((1, D), lambda i: (0, 0))],
        out_specs=pl.BlockSpec((tr, D), lambda i: (i, 0)), out_shape=jax.ShapeDtypeStruct((L, D), BF16),
        compiler_params=_params("parallel"), name=name)(h, g)


def _rms_bwd(h, g, dy, dres, *, name):
    L, D = h.shape
    tr = _tile(L, ROW_TILE, SUBLANES)

    def body(h_ref, g_ref, dy_ref, dres_ref, dh_ref, dg_ref):
        @pl.when(pl.program_id(0) == 0)
        def _():
            dg_ref[...] = jnp.zeros_like(dg_ref)

        x = h_ref[...]
        r = lax.rsqrt(jnp.mean(x * x, axis=1, keepdims=True) + EPS)
        xn = x * r
        dy = dy_ref[...].astype(F32)
        gdy = dy * g_ref[...]
        dx = r * (gdy - xn * jnp.mean(gdy * xn, axis=1, keepdims=True))
        dh_ref[...] = dres_ref[...] + dx
        dg_ref[...] += jnp.sum(dy * xn, axis=0, keepdims=True)

    row = pl.BlockSpec((tr, D), lambda i: (i, 0))
    vec = pl.BlockSpec((1, D), lambda i: (0, 0))
    return pl.pallas_call(
        body, grid=(L // tr,), in_specs=[row, vec, row, row], out_specs=[row, vec],
        out_shape=[jax.ShapeDtypeStruct((L, D), F32), jax.ShapeDtypeStruct((1, D), F32)],
        compiler_params=_params("arbitrary"), name=name)(h, g, dy, dres)


def _glu_res_rms(z, h, g, *, name):
    L, D = h.shape
    nb, cb = z.shape[1], z.shape[3]
    tr = _tile(L, ROW_TILE, SUBLANES)

    def body(z_ref, h_ref, g_ref, h1_ref, hn_ref):
        za = jnp.concatenate([z_ref[0, d] for d in range(nb)], axis=1)
        zg = jnp.concatenate([z_ref[1, d] for d in range(nb)], axis=1)
        x = h_ref[...] + za * jax.nn.sigmoid(zg)
        h1_ref[...] = x
        r = lax.rsqrt(jnp.mean(x * x, axis=1, keepdims=True) + EPS)
        hn_ref[...] = (x * r * g_ref[...]).astype(BF16)

    row = pl.BlockSpec((tr, D), lambda i: (i, 0))
    return pl.pallas_call(
        body, grid=(L // tr,),
        in_specs=[pl.BlockSpec((2, nb, tr, cb), lambda i: (0, 0, i, 0)), row, pl.BlockSpec((1, D), lambda i: (0, 0))],
        out_specs=[row, row],
        out_shape=[jax.ShapeDtypeStruct((L, D), F32), jax.ShapeDtypeStruct((L, D), BF16)],
        compiler_params=_params("parallel"), name=name)(z, h, g)


def _glu_bwd(z, dout, *, name):
    L, D = dout.shape
    nb, cb = z.shape[1], z.shape[3]
    tr = _tile(L, ROW_TILE, SUBLANES)

    def body(z_ref, d_ref, o_ref):
        for d in range(nb):
            dd = d_ref[:, d * cb:(d + 1) * cb]
            sg = jax.nn.sigmoid(z_ref[1, d])
            o_ref[0, d] = (dd * sg).astype(BF16)
            o_ref[1, d] = (dd * z_ref[0, d] * sg * (1.0 - sg)).astype(BF16)

    zs = pl.BlockSpec((2, nb, tr, cb), lambda i: (0, 0, i, 0))
    return pl.pallas_call(
        body, grid=(L // tr,), in_specs=[zs, pl.BlockSpec((tr, D), lambda i: (i, 0))], out_specs=zs,
        out_shape=jax.ShapeDtypeStruct(z.shape, BF16),
        compiler_params=_params("parallel"), name=name)(z, dout)


def _loss_head(h, g, target, *, name):
    L, D = h.shape
    tr = _tile(L, ROW_TILE, SUBLANES)

    def body(h_ref, g_ref, t_ref, loss_ref, dh_ref, dg_ref):
        @pl.when(pl.program_id(0) == 0)
        def _():
            dg_ref[...] = jnp.zeros_like(dg_ref)
            loss_ref[...] = jnp.zeros_like(loss_ref)

        x = h_ref[...]
        gg = g_ref[...]
        r = lax.rsqrt(jnp.mean(x * x, axis=1, keepdims=True) + EPS)
        xn = x * r
        err = xn * gg - t_ref[...]
        loss_ref[...] += 0.5 * jnp.sum(jnp.mean(err * err, axis=1, keepdims=True), axis=0, keepdims=True)
        dy = err * (1.0 / D)
        gdy = dy * gg
        dh_ref[...] = r * (gdy - xn * jnp.mean(gdy * xn, axis=1, keepdims=True))
        dg_ref[...] += jnp.sum(dy * xn, axis=0, keepdims=True)

    row = pl.BlockSpec((tr, D), lambda i: (i, 0))
    vec = pl.BlockSpec((1, D), lambda i: (0, 0))
    return pl.pallas_call(
        body, grid=(L // tr,), in_specs=[row, vec, row],
        out_specs=[pl.BlockSpec((1, 1), lambda i: (0, 0)), row, vec],
        out_shape=[jax.ShapeDtypeStruct((1, 1), F32), jax.ShapeDtypeStruct((L, D), F32),
                   jax.ShapeDtypeStruct((1, D), F32)],
        compiler_params=_params("arbitrary"), name=name)(h, g, target)


CONV_ROW_TILE = 256


def _sigmoid(x):
    return pl.reciprocal(1.0 + jnp.exp(-x), approx=True)


def _conv_specs(L, tr, tc):
    nrb = tr // SUBLANES
    before = lambda i: jnp.maximum(i * nrb - 1, 0)
    after = lambda i: jnp.minimum((i + 1) * nrb, L // SUBLANES - 1)
    main = pl.BlockSpec((2, None, tr, tc), lambda j, i: (0, j, i, 0))
    prev = pl.BlockSpec((2, None, SUBLANES, tc), lambda j, i: (0, j, before(i), 0))
    nxt = pl.BlockSpec((2, None, SUBLANES, tc), lambda j, i: (0, j, after(i), 0))
    cw = pl.BlockSpec((2, None, 3, tc), lambda j, i: (0, j, 0, 0))
    cb = pl.BlockSpec((2, None, 1, tc), lambda j, i: (0, j, 0, 0))
    half = pl.BlockSpec((None, tr, tc), lambda j, i: (j, i, 0))
    half_nxt = pl.BlockSpec((None, SUBLANES, tc), lambda j, i: (j, after(i), 0))
    return main, prev, nxt, cw, cb, half, half_nxt


def _conv_rows(xe, w, b):
    x1 = pltpu.roll(xe, 1, 0)
    x2 = pltpu.roll(xe, 2, 0)
    return b + x2 * w[0:1] + x1 * w[1:2] + xe * w[2:3], x1, x2


def _shift_down(x, halo, k, row):
    y = pltpu.roll(x, k, 0)
    for r in range(k):
        y = jnp.where(row == r, halo[SUBLANES - k + r:SUBLANES - k + r + 1, :], y)
    return y


def _conv_act(u0, cw, cb, *, name, xfers=()):
    _, nb, L, tc = u0.shape
    tr = _tile(L, ROW_TILE, SUBLANES)
    main, prev, _, cws, cbs, half, _ = _conv_specs(L, tr, tc)

    def body(u_ref, p_ref, w_ref, b_ref, a_ref):
        first = pl.program_id(1) == 0
        row = lax.broadcasted_iota(jnp.int32, (tr, tc), 0)
        y = []
        for s in range(2):
            x, w = u_ref[s], w_ref[s]
            halo = jnp.where(first, 0.0, p_ref[s])
            x1 = _shift_down(x, halo, 1, row)
            x2 = _shift_down(x, halo, 2, row)
            y.append(b_ref[s] + x2 * w[0:1] + x1 * w[1:2] + x * w[2:3])
        a_ref[...] = (y[0] * _sigmoid(y[0]) * y[1]).astype(BF16)

    return _pcall(body, (u0, u0, cw, cb), grid=(nb, L // tr), in_specs=[main, prev, cws, cbs], out_specs=[half],
                  out_shape=[jax.ShapeDtypeStruct((nb, L, tc), BF16)], sem=("parallel", "parallel"), name=name,
                  xfers=xfers)


def _conv_ffn_bwd(u0, cw, cb, da, *, name, xfers=()):
    _, nb, L, tc = u0.shape
    tr = _tile(L, CONV_ROW_TILE, SUBLANES)
    main, prev, nxt, cws, cbs, half, half_nxt = _conv_specs(L, tr, tc)
    nr = L // tr
    H = SUBLANES

    def body(u_ref, p_ref, n_ref, w_ref, b_ref, da_ref, dan_ref, a_ref, du0_ref, dcw_ref, dcb_ref):
        i = pl.program_id(1)

        @pl.when(i == 0)
        def _():
            dcw_ref[...] = jnp.zeros_like(dcw_ref)
            dcb_ref[...] = jnp.zeros_like(dcb_ref)

        y, x1, x2 = [], [], []
        for s in range(2):
            xe = jnp.concatenate([jnp.where(i == 0, 0.0, p_ref[s]), u_ref[s], n_ref[s]], axis=0)
            ys, x1s, x2s = _conv_rows(xe, w_ref[s], b_ref[s])
            y.append(ys[H:])
            x1.append(x1s[H:H + tr])
            x2.append(x2s[H:H + tr])
        gate, up = y
        da = jnp.concatenate([da_ref[...], dan_ref[...]], axis=0)
        row = lax.broadcasted_iota(jnp.int32, (tr + H, tc), 0)
        da = jnp.where(jnp.logical_and(i == nr - 1, row >= tr), 0.0, da)
        sg = _sigmoid(gate)
        silu = gate * sg
        a_ref[...] = (silu * up)[:tr].astype(BF16)
        d = (da * up * (sg * (1.0 + gate * (1.0 - sg))), da * silu)
        for s in range(2):
            w = w_ref[s]
            d0 = d[s][:tr]
            d1 = pltpu.roll(d[s], tr + H - 1, 0)[:tr]
            d2 = pltpu.roll(d[s], tr + H - 2, 0)[:tr]
            du0_ref[s] = (d0 * w[2:3] + d1 * w[1:2] + d2 * w[0:1]).astype(BF16)
            dcw_ref[s, 0:1, :] += jnp.sum(d0 * x2[s], axis=0, keepdims=True)
            dcw_ref[s, 1:2, :] += jnp.sum(d0 * x1[s], axis=0, keepdims=True)
            dcw_ref[s, 2:3, :] += jnp.sum(d0 * u_ref[s], axis=0, keepdims=True)
            dcb_ref[s] += jnp.sum(d0, axis=0, keepdims=True)

    return _pcall(body, (u0, u0, u0, cw, cb, da, da), grid=(nb, nr),
                  in_specs=[main, prev, nxt, cws, cbs, half, half_nxt], out_specs=[half, main, cws, cbs],
                  out_shape=[jax.ShapeDtypeStruct((nb, L, tc), BF16), jax.ShapeDtypeStruct((2, nb, L, tc), BF16),
                             jax.ShapeDtypeStruct((2, nb, 3, tc), F32), jax.ShapeDtypeStruct((2, nb, 1, tc), F32)],
                  sem=("parallel", "arbitrary"), name=name, xfers=xfers)


N_TILES = 64
HALF = N_TILES // 2


def _swap(s):
    return jnp.concatenate([s[HALF:], s[:HALF]], axis=0)


def _chan_block(j):
    return ((j % HALF) // 4) * LANES


def _pairs_of(jb):
    return [2 * jb, 2 * jb + 1, HALF // 2 + 2 * jb, HALF // 2 + 2 * jb + 1]


_WB_SPEC = pl.BlockSpec((HALF, LANES, 2 * LANES), lambda c: (0, 0, 0))
_WC_SPEC = _WB_SPEC
GELU_C = math.sqrt(2.0 / math.pi)
GELU_A = 0.044715


def _gelu(x):
    return 0.5 * x * (1.0 + jnp.tanh(GELU_C * (x + GELU_A * x * x * x)))


def _gelu_grad(x):
    th = jnp.tanh(GELU_C * (x + GELU_A * x * x * x))
    return 0.5 * (1.0 + th) + 0.5 * x * (1.0 - th * th) * GELU_C * (1.0 + 3.0 * GELU_A * x * x)


def _tile_rows(j, T, TP):
    return pl.ds(j * TP + SUBLANES, T)


def _pair(ref, jp, T, TP):
    return jnp.concatenate([ref[_tile_rows(2 * jp, T, TP), :], ref[_tile_rows(2 * jp + 1, T, TP), :]],
                           axis=1).astype(BF16)


def _unpair(ref, jp, val, T, TP):
    ref[_tile_rows(2 * jp, T, TP), :] = val[:, :LANES]
    ref[_tile_rows(2 * jp + 1, T, TP), :] = val[:, LANES:]


def _s5_project_in(u_ref, wb_ref, s3, T, TP):
    for jp in range(HALF):
        blk = _chan_block(2 * jp)
        _unpair(s3, jp, jnp.dot(u_ref[:, blk:blk + LANES], wb_ref[jp], preferred_element_type=F32), T, TP)


def _s5_scan_fwd(s3, a1, a2, s0, T, TP):
    span = (N_GROUPS - 1) * TP + 2 * SUBLANES

    def blk(i, s):
        view = s3.at[pl.ds(pl.multiple_of(i * SUBLANES, SUBLANES), span)]
        for k in range(SUBLANES):
            rows = pl.ds(SUBLANES + k, N_GROUPS, stride=TP)
            s = a1 * s + a2 * _swap(s) + view[rows, :]
            view[rows, :] = s
        return s

    return lax.fori_loop(0, T // SUBLANES, blk, s0)


def _s5_fwd(hn, wb, wc, a1, a2, dvec, *, name, xfers=()):
    L, D = hn.shape
    T = min(S5_CHUNK, L)
    TP = T + SUBLANES
    nC = L // T

    def body(u_ref, wb_ref, wc_ref, a1_ref, a2_ref, d_ref, y_ref, yg_ref, sb_ref, s3, st):
        @pl.when(pl.program_id(0) == 0)
        def _():
            st[...] = jnp.zeros_like(st)

        sb_ref[0] = st[...]
        _s5_project_in(u_ref, wb_ref, s3, T, TP)
        st[...] = _s5_scan_fwd(s3, a1_ref[...], a2_ref[...], st[...], T, TP)
        for jb in range(D // LANES):
            acc = jnp.zeros((T, LANES), F32)
            for jp in _pairs_of(jb):
                acc += lax.dot_general(_pair(s3, jp, T, TP), wc_ref[jp], _NT, preferred_element_type=F32)
            cols = slice(jb * LANES, (jb + 1) * LANES)
            y = acc + d_ref[:, cols] * u_ref[:, cols].astype(F32)
            y_ref[:, cols] = y
            yg_ref[:, cols] = _gelu(y).astype(BF16)

    row = pl.BlockSpec((T, D), lambda c: (c, 0))
    aspec = pl.BlockSpec((N_GROUPS, LANES), lambda c: (0, 0))
    return _pcall(
        body, (hn, wb, wc, a1, a2, dvec), grid=(nC,),
        in_specs=[row, _WB_SPEC, _WC_SPEC, aspec, aspec, pl.BlockSpec((1, D), lambda c: (0, 0))],
        out_specs=[row, row, pl.BlockSpec((1, N_GROUPS, LANES), lambda c: (c, 0, 0))],
        out_shape=[jax.ShapeDtypeStruct((L, D), F32), jax.ShapeDtypeStruct((L, D), BF16),
                   jax.ShapeDtypeStruct((nC, N_GROUPS, LANES), F32)],
        scratch_shapes=[pltpu.VMEM((N_GROUPS * TP, LANES), F32), pltpu.VMEM((N_GROUPS, LANES), F32)],
        sem=("arbitrary",), name=name, xfers=xfers)


def _s5_bwd(hn, dyg, ypre, sbound, wb, wc, a1, a2, dvec, *, name, xfers=()):
    L, D = hn.shape
    T = min(S5_CHUNK, L)
    TP = T + SUBLANES
    nC = L // T
    span = (N_GROUPS - 1) * TP + 2 * SUBLANES
    NT = (((1,), (1,)), ((), ()))
    TN = (((0,), (0,)), ((), ()))

    def body(u_ref, dyg_ref, yp_ref, sb_ref, wb_ref, wc_ref, a1_ref, a2_ref, d_ref,
             du_ref, dwb_ref, dwc_ref, da1_ref, da2_ref, dd_ref, s3, g3, gst, dy_s):
        @pl.when(pl.program_id(0) == 0)
        def _():
            gst[...] = jnp.zeros_like(gst)
            dwb_ref[...] = jnp.zeros_like(dwb_ref)
            dwc_ref[...] = jnp.zeros_like(dwc_ref)
            da1_ref[...] = jnp.zeros_like(da1_ref)
            da2_ref[...] = jnp.zeros_like(da2_ref)
            dd_ref[...] = jnp.zeros_like(dd_ref)

        a1 = a1_ref[...]
        a2 = a2_ref[...]
        dy = dyg_ref[...].astype(F32) * _gelu_grad(yp_ref[...])
        dy_s[...] = dy.astype(BF16)
        dd_ref[...] += jnp.sum(dy * u_ref[...].astype(F32), axis=0, keepdims=True)
        du_ref[...] = d_ref[...] * dy

        s3[pl.ds(SUBLANES - 1, N_GROUPS, stride=TP), :] = sb_ref[0]
        _s5_project_in(u_ref, wb_ref, s3, T, TP)
        _s5_scan_fwd(s3, a1, a2, sb_ref[0], T, TP)

        for jp in range(HALF):
            blk = _chan_block(2 * jp)
            _unpair(g3, jp, jnp.dot(dy_s[:, blk:blk + LANES], wc_ref[jp], preferred_element_type=F32), T, TP)
        a2c = -a2

        def rblk(ii, carry):
            g, acc1, acc2 = carry
            t0 = pl.multiple_of((T // SUBLANES - 1 - ii) * SUBLANES, SUBLANES)
            gv = g3.at[pl.ds(t0, span)]
            sv = s3.at[pl.ds(t0, span)]
            for k in reversed(range(SUBLANES)):
                rows = pl.ds(SUBLANES + k, N_GROUPS, stride=TP)
                g = a1 * g + a2c * _swap(g) + gv[rows, :]
                gv[rows, :] = g
                sp = sv[pl.ds(SUBLANES - 1 + k, N_GROUPS, stride=TP), :]
                acc1 = acc1 + g * sp
                acc2 = acc2 + g * _swap(sp)
            return g, acc1, acc2

        zero = jnp.zeros((N_GROUPS, LANES), F32)
        g, acc1, acc2 = lax.fori_loop(0, T // SUBLANES, rblk, (gst[...], zero, zero))
        gst[...] = g
        da1_ref[...] += acc1
        da2_ref[...] += acc2

        for jb in range(D // LANES):
            cols = slice(jb * LANES, (jb + 1) * LANES)
            acc = jnp.zeros((T, LANES), F32)
            for jp in _pairs_of(jb):
                gp = _pair(g3, jp, T, TP)
                dwc_ref[jp] += lax.dot_general(dy_s[:, cols], _pair(s3, jp, T, TP), TN, preferred_element_type=F32)
                dwb_ref[jp] += lax.dot_general(u_ref[:, cols], gp, TN, preferred_element_type=F32)
                acc += lax.dot_general(gp, wb_ref[jp], NT, preferred_element_type=F32)
            du_ref[:, cols] += acc

    rrow = pl.BlockSpec((T, D), lambda c: (nC - 1 - c, 0))
    aspec = pl.BlockSpec((N_GROUPS, LANES), lambda c: (0, 0))
    vec = pl.BlockSpec((1, D), lambda c: (0, 0))
    return _pcall(
        body, (hn, dyg, ypre, sbound, wb, wc, a1, a2, dvec), grid=(nC,),
        in_specs=[rrow, rrow, rrow, pl.BlockSpec((1, N_GROUPS, LANES), lambda c: (nC - 1 - c, 0, 0)),
                  _WB_SPEC, _WC_SPEC, aspec, aspec, vec],
        out_specs=[rrow, _WB_SPEC, _WC_SPEC, aspec, aspec, vec],
        out_shape=[jax.ShapeDtypeStruct((L, D), F32),
                   jax.ShapeDtypeStruct((HALF, LANES, 2 * LANES), F32),
                   jax.ShapeDtypeStruct((HALF, LANES, 2 * LANES), F32),
                   jax.ShapeDtypeStruct((N_GROUPS, LANES), F32), jax.ShapeDtypeStruct((N_GROUPS, LANES), F32),
                   jax.ShapeDtypeStruct((1, D), F32)],
        scratch_shapes=[pltpu.VMEM((N_GROUPS * TP, LANES), F32), pltpu.VMEM((N_GROUPS * TP, LANES), F32),
                        pltpu.VMEM((N_GROUPS, LANES), F32), pltpu.VMEM((T, D), BF16)],
        sem=("arbitrary",), name=name, xfers=xfers)


def _s5_prep(lam_re, lam_im, log_dt, b_re, b_im, c_re, c_im):
    dt = jnp.exp(log_dt)[:, None]
    mag = jnp.exp(lam_re * dt)
    lb_re = mag * jnp.cos(lam_im * dt)
    lb_im = mag * jnp.sin(lam_im * dt)
    den = lam_re * lam_re + lam_im * lam_im
    nr = lb_re - 1.0
    fr = ((nr * lam_re + lb_im * lam_im) / den)[..., None]
    fi = ((lb_im * lam_re - nr * lam_im) / den)[..., None]
    bb_re = fr * b_re - fi * b_im
    bb_im = fr * b_im + fi * b_re
    pair = lambda a: a.reshape(HALF, 2 * SSM_STATE)
    a1 = jnp.concatenate([pair(lb_re), pair(lb_re)], axis=0)
    a2 = jnp.concatenate([-pair(lb_im), pair(lb_im)], axis=0)

    quads = N_GROUPS // 4
    rows, cols = 4 * SSM_GROUP, 4 * SSM_STATE
    diag = (jnp.arange(rows)[:, None] // SSM_GROUP == jnp.arange(cols)[None, :] // SSM_STATE).astype(F32)
    place = jnp.eye(2, dtype=F32)

    def expand(w):
        blocks = jnp.tile(w.reshape(quads, rows, SSM_STATE), (1, 1, 4)) * diag
        return jnp.einsum('kq,gkrl->gkqrl', place, blocks.reshape(quads // 2, 2, rows, cols)
                          ).reshape(quads, LANES, cols)

    wb = jnp.concatenate([expand(bb_re.transpose(0, 2, 1)), expand(bb_im.transpose(0, 2, 1))], axis=0)
    wc = jnp.concatenate([expand(c_re), expand(-c_im)], axis=0)
    return a1, a2, wb, wc


def _tri(n, upper):
    r = lax.broadcasted_iota(jnp.int32, (n, n), 0)
    c = lax.broadcasted_iota(jnp.int32, (n, n), 1)
    return ((r <= c) if upper else (r >= c)).astype(F32)


def _fgate_fwd(fl, bf, *, name):
    L, W = fl.shape
    tr = _tile(L, CUM_TILE, SUBLANES)

    def body(f_ref, b_ref, o_ref, carry):
        @pl.when(pl.program_id(0) == 0)
        def _():
            carry[...] = jnp.zeros_like(carry)

        x = f_ref[...] + b_ref[...]
        ls = jnp.minimum(x, 0.0) - jnp.log(1.0 + jnp.exp(-jnp.abs(x)))
        cum = jnp.dot(_tri(tr, False), ls, preferred_element_type=F32, precision=lax.Precision.HIGHEST) + carry[...]
        o_ref[...] = cum
        carry[...] = cum[tr - 1:tr, :]

    return pl.pallas_call(
        body, grid=(L // tr,),
        in_specs=[pl.BlockSpec((tr, W), lambda i: (i, 0)), pl.BlockSpec((1, W), lambda i: (0, 0))],
        out_specs=pl.BlockSpec((tr, W), lambda i: (i, 0)), out_shape=jax.ShapeDtypeStruct((L, W), F32),
        scratch_shapes=[pltpu.VMEM((1, W), F32)], compiler_params=_params("arbitrary"), name=name)(fl, bf)


def _fgate_bwd(fl, bf, dcum, *, name):
    L, W = fl.shape
    tr = _tile(L, CUM_TILE, SUBLANES)
    n = L // tr

    def body(f_ref, b_ref, d_ref, o_ref, db_ref, carry):
        @pl.when(pl.program_id(0) == 0)
        def _():
            carry[...] = jnp.zeros_like(carry)
            db_ref[...] = jnp.zeros_like(db_ref)

        d = d_ref[...]
        rev = jnp.dot(_tri(tr, True), d, preferred_element_type=F32, precision=lax.Precision.HIGHEST) + carry[...]
        carry[...] += jnp.sum(d, axis=0, keepdims=True)
        df = rev * jax.nn.sigmoid(-(f_ref[...] + b_ref[...]))
        o_ref[...] = df
        db_ref[...] += jnp.sum(df, axis=0, keepdims=True)

    rrow = pl.BlockSpec((tr, W), lambda i: (n - 1 - i, 0))
    vec = pl.BlockSpec((1, W), lambda i: (0, 0))
    return pl.pallas_call(
        body, grid=(n,), in_specs=[rrow, vec, rrow], out_specs=[rrow, vec],
        out_shape=[jax.ShapeDtypeStruct((L, W), F32), jax.ShapeDtypeStruct((1, W), F32)],
        scratch_shapes=[pltpu.VMEM((1, W), F32)], compiler_params=_params("arbitrary"), name=name)(fl, bf, dcum)


_NT = (((1,), (1,)), ((), ()))
_TN = (((0,), (0,)), ((), ()))
HEAD_PAIRS = N_HEADS // 2


def _causal_tiles(n, by_row):
    pairs = ([(i, j) for i in range(n) for j in range(i + 1)] if by_row
             else [(i, j) for j in range(n) for i in range(j, n)])
    return (jnp.array([p[0] for p in pairs], jnp.int32), jnp.array([p[1] for p in pairs], jnp.int32))


def _as_row(col):
    return jnp.transpose(jnp.broadcast_to(col, (col.shape[0], LANES)))[0:1, :]


def _attn_logits(qs, k, ck, masked, t):
    s = lax.dot_general(qs, k, _NT, preferred_element_type=F32) - ck
    if masked:
        r = lax.broadcasted_iota(jnp.int32, (t, t), 0)
        c = lax.broadcasted_iota(jnp.int32, (t, t), 1)
        s = jnp.where(c > r, NEG, s)
    return s


def _attn_fwd(q, kv, ck, *, name, xfers=()):
    L, D = q.shape
    t = _tile(L, ATTN_TILE)
    n = L // t
    dh = HEAD_DIM

    def body(q_ref, k_ref, v_ref, ck_ref, o_ref, o32_ref, lse_ref, m_s, l_s, acc):
        i, j = pl.program_id(1), pl.program_id(2)

        @pl.when(j == 0)
        def _():
            m_s[...] = jnp.full_like(m_s, NEG)
            l_s[...] = jnp.zeros_like(l_s)
            acc[...] = jnp.zeros_like(acc)

        def tile(masked):
            for e in range(2):
                sl = slice(e * dh, (e + 1) * dh)
                v = v_ref[:, sl]
                s = _attn_logits(q_ref[:, sl] * ATTN_SCALE, k_ref[:, sl], ck_ref[e], masked, t)
                m_new = jnp.maximum(m_s[e], jnp.max(s, axis=1, keepdims=True))
                alpha = jnp.exp(m_s[e] - m_new)
                p = jnp.exp(s - m_new)
                l_s[e] = alpha * l_s[e] + jnp.sum(p, axis=1, keepdims=True)
                p_hi = p.astype(BF16)
                p_lo = (p - p_hi.astype(F32)).astype(BF16)
                pv = (jnp.dot(p_hi, v, preferred_element_type=F32) + jnp.dot(p_lo, v, preferred_element_type=F32))
                acc[e] = alpha * acc[e] + pv
                m_s[e] = m_new

        pl.when(j < i)(functools.partial(tile, False))
        pl.when(j == i)(functools.partial(tile, True))

        @pl.when(j == n - 1)
        def _():
            for e in range(2):
                sl = slice(e * dh, (e + 1) * dh)
                o = acc[e] / l_s[e]
                o_ref[:, sl] = o.astype(BF16)
                o32_ref[:, sl] = o
                lse_ref[e] = _as_row(m_s[e] + jnp.log(l_s[e]))

    qs = pl.BlockSpec((t, LANES), lambda h, i, j: (i, h))
    ks = pl.BlockSpec((t, LANES), lambda h, i, j: (jnp.minimum(i, j), h))
    vs = pl.BlockSpec((t, LANES), lambda h, i, j: (jnp.minimum(i, j), HEAD_PAIRS + h))
    cs = pl.BlockSpec((2, 1, t), lambda h, i, j: (h, 0, jnp.minimum(i, j)))
    return _pcall(
        body, (q, kv, kv, ck), grid=(HEAD_PAIRS, n, n), in_specs=[qs, ks, vs, cs],
        out_specs=[qs, qs, pl.BlockSpec((2, 1, t), lambda h, i, j: (h, 0, i))],
        out_shape=[jax.ShapeDtypeStruct((L, D), BF16), jax.ShapeDtypeStruct((L, D), F32),
                   jax.ShapeDtypeStruct((N_HEADS, 1, L), F32)],
        scratch_shapes=[pltpu.VMEM((2, t, 1), F32), pltpu.VMEM((2, t, 1), F32), pltpu.VMEM((2, t, dh), F32)],
        sem=("parallel", "parallel", "arbitrary"), name=name, xfers=xfers)


def _attn_delta(do, o, *, name):
    L, D = do.shape
    t = _tile(L, ATTN_TILE)

    def body(do_ref, o_ref, d_ref):
        prod = do_ref[...].astype(F32) * o_ref[...]
        head = lax.broadcasted_iota(jnp.int32, (N_HEADS, D), 0)
        col = lax.broadcasted_iota(jnp.int32, (N_HEADS, D), 1)
        sel = (col // HEAD_DIM == head).astype(F32)
        d_ref[:, 0, :] = lax.dot_general(sel, prod, _NT, preferred_element_type=F32, precision=lax.Precision.HIGHEST)

    row = pl.BlockSpec((t, D), lambda i: (i, 0))
    return pl.pallas_call(
        body, grid=(L // t,), in_specs=[row, row], out_specs=pl.BlockSpec((N_HEADS, 1, t), lambda i: (0, 0, i)),
        out_shape=jax.ShapeDtypeStruct((N_HEADS, 1, L), F32), compiler_params=_params("parallel"), name=name)(do, o)


def _attn_bwd(q, kv, ck_col, do, lse_row, delta_row, *, name, xfers=()):
    L, D = q.shape
    t = _tile(L, ATTN_TILE)
    n = L // t
    dh = HEAD_DIM

    def body(i_tab, j_tab, q_ref, k_ref, v_ref, ck_ref, do_ref, lse_ref, dl_ref, dq_ref, dk_ref, dv_ref, dck_ref,
             dck_s):
        i, j = i_tab[pl.program_id(1)], j_tab[pl.program_id(1)]

        @pl.when(pl.program_id(1) == 0)
        def _():
            dq_ref[...] = jnp.zeros_like(dq_ref)

        @pl.when(i == j)
        def _():
            dk_ref[...] = jnp.zeros_like(dk_ref)
            dv_ref[...] = jnp.zeros_like(dv_ref)
            dck_s[...] = jnp.zeros_like(dck_s)

        def tile(masked):
            cols = pl.ds(pl.multiple_of(i * t, t), t)
            for e in range(2):
                sl = slice(e * dh, (e + 1) * dh)
                qs = q_ref[:, sl] * ATTN_SCALE
                k = k_ref[:, sl]
                do = do_ref[:, sl]
                st = lax.dot_general(k, qs, _NT, preferred_element_type=F32) - ck_ref[e]
                if masked:
                    kpos = lax.broadcasted_iota(jnp.int32, (t, t), 0)
                    qpos = lax.broadcasted_iota(jnp.int32, (t, t), 1)
                    st = jnp.where(kpos > qpos, NEG, st)
                pt = jnp.exp(st - lse_ref[e])
                dpt = lax.dot_general(v_ref[:, sl], do, _NT, preferred_element_type=F32)
                dst = pt * (dpt - dl_ref[e])
                dst16 = dst.astype(BF16)
                dv_ref[:, sl] += jnp.dot(pt.astype(BF16), do, preferred_element_type=F32)
                dk_ref[:, sl] += jnp.dot(dst16, qs, preferred_element_type=F32)
                dck_s[e] -= jnp.sum(dst, axis=1, keepdims=True)
                dq_ref[sl, cols] += lax.dot_general(k, dst16, _TN, preferred_element_type=F32) * ATTN_SCALE

        pl.when(i > j)(functools.partial(tile, False))
        pl.when(i == j)(functools.partial(tile, True))

        @pl.when(i == n - 1)
        def _():
            for e in range(2):
                dck_ref[e] = _as_row(dck_s[e])

    qs = pl.BlockSpec((t, LANES), lambda h, s, it, jt: (it[s], h))
    ks = pl.BlockSpec((t, LANES), lambda h, s, it, jt: (jt[s], h))
    vs = pl.BlockSpec((t, LANES), lambda h, s, it, jt: (jt[s], HEAD_PAIRS + h))
    cs = pl.BlockSpec((2, t, 1), lambda h, s, it, jt: (h, jt[s], 0))
    ls = pl.BlockSpec((2, 1, t), lambda h, s, it, jt: (h, 0, it[s]))
    full = jax.ShapeDtypeStruct((L, D), F32)
    tabs = _causal_tiles(n, by_row=False)
    return _pcall(
        body, (q, kv, kv, ck_col, do, lse_row, delta_row), grid=(HEAD_PAIRS, tabs[0].shape[0]),
        in_specs=[qs, ks, vs, cs, qs, ls, ls],
        out_specs=[pl.BlockSpec((LANES, L), lambda h, s, it, jt: (h, 0)), ks, ks,
                   pl.BlockSpec((2, 1, t), lambda h, s, it, jt: (h, 0, jt[s]))],
        out_shape=[jax.ShapeDtypeStruct((D, L), F32), full, full, jax.ShapeDtypeStruct((N_HEADS, 1, L), F32)],
        scratch_shapes=[pltpu.VMEM((2, t, 1), F32)],
        sem=("parallel", "arbitrary"), name=name, xfers=xfers, prefetch=tabs)


SHARD_COLS_FFN = 2 * D_FF // N_DEV
SHARD_ROWS_FFN = D_FF // N_DEV
SHARD_COLS_GLU = 2 * D_MODEL // N_DEV
SHARD_ROWS_QO = D_MODEL // N_DEV
SHARD_COLS_KVF = (2 * D_MODEL + N_HEADS) // N_DEV
S5_NAMES = ("lam_re", "lam_im", "log_dt", "ssm_b_re", "ssm_b_im", "ssm_c_re", "ssm_c_im")
REPL_LATE_ROWS = 272
REPL_EARLY_ROWS = 304


def _leaves(parts):
    out = []
    for p in parts:
        out.extend(_leaves(p) if isinstance(p, (list, tuple)) else [p.reshape(-1)])
    return out


def _whole_rows(n):
    return -(-n // D_MODEL)


def _pack_rows(parts, rows):
    blocks = [jnp.pad(p, (0, _whole_rows(p.shape[0]) * D_MODEL - p.shape[0])).reshape(-1, D_MODEL)
              for p in _leaves(parts)]
    used = sum(b.shape[0] for b in blocks)
    return jnp.concatenate(blocks + [jnp.zeros((rows - used, D_MODEL), blocks[0].dtype)], axis=0)


def _unpack_rows(packed, like):
    out, row = [], 0
    for p in _leaves(like):
        n = _whole_rows(p.shape[0])
        out.append(packed[row:row + n].reshape(-1)[:p.shape[0]])
        row += n
    return out


def _repl_late(d):
    return [d["g_mix"][0], [d[n][0] for n in S5_NAMES]]


def _repl_early(d):
    return [list(d["g_mix"][1:]), list(d["g_ffn"]), [d[n][1] for n in S5_NAMES], d["g_kv"], d["b_f"],
            list(d["ffn_conv_b"]), d["g_final"]]


def _kvf_blocks(full):
    return full.reshape(D_MODEL, N_DEV, SHARD_COLS_KVF).transpose(1, 0, 2)


class _Step:
    def __init__(self, weights, send=None, plan=None):
        self.w = dict(weights)
        self.send = send or {}
        self.grad = {}
        self.bcast = {}
        self.slots = {}
        self.plan = plan or {}

    def xfers(self, host):
        src = {"w": self.send, "g": self.grad, "b": self.bcast}
        out = []
        for kind, k in self.plan.get(host, ()):
            if kind == "w" and k[0] == "w_ffn_in" and len(k) == 3:
                parts = _W_IN_PARTS[k[1]]
                out.append((self.send[k], False, (sum(parts), sum(parts[:k[2]]), self.w.get(("w_ffn_in_parts", k[1])))))
            else:
                out.append((src[kind][k], kind == "g"))
        return out

    def land(self, host, gathered):
        for (kind, k), g in zip(self.plan.get(host, ()), gathered):
            if kind == "w":
                self.arrive(k, g)
            else:
                self.slots[k] = g

    def arrive(self, k, g):
        name = k[0]
        if name == "w_ffn_in" and len(k) == 3:
            self.w["w_ffn_in_parts", k[1]] = g
            if k[2] == len(_W_IN_PARTS[k[1]]) - 1:
                self.w[name, k[1]] = g
        elif name == "w_ffn_out":
            self.w[k] = g.reshape(4, 2 * SHARD_ROWS_FFN, D_MODEL)
        elif name in ("w_q", "w_o"):
            self.w[k] = g.reshape(D_MODEL, D_MODEL)
        elif name == "w_kvf":
            full = g.transpose(1, 0, 2).reshape(D_MODEL, N_DEV * SHARD_COLS_KVF)
            self.w["w_kv",] = full[:, :2 * D_MODEL]
            self.w["w_f",] = jnp.pad(full[:, 2 * D_MODEL:], ((0, 0), (0, LANES - N_HEADS)))
        elif name == "small":
            flat = g.reshape(N_DEV, -1)
            self.w["ssm_d",] = flat[:, :256].reshape(N_DEV, N_A, LANES).transpose(1, 0, 2).reshape(N_A, D_MODEL)
            cw = flat[:, 256:256 + DEPTH * 3 * SHARD_COLS_FFN].reshape(N_DEV, DEPTH, 3, SHARD_COLS_FFN)
            for layer in range(DEPTH):
                self.w["conv_w", layer] = cw[:, layer].reshape(2, 4, 3, SHARD_COLS_FFN)
        else:
            self.w[k] = g

    def run(self, host, fn, *args, **kw):
        xf = self.xfers(host)
        res = fn(*args, name=host, xfers=xf, **kw)
        if not xf:
            return res[0] if isinstance(res, (list, tuple)) and len(res) == 1 else res
        n_own = len(res) - len(xf)
        self.land(host, res[n_own:])
        return res[0] if n_own == 1 else res[:n_own]


def _step(x, target, S):
    L = x.shape[0]
    W = S.w
    vec = lambda a: a.reshape(1, -1)
    CF = SHARD_COLS_FFN

    h = x
    saved = []
    kvs = None
    hn_next = None
    for layer in range(DEPTH):
        t = str(layer)
        if layer < N_A:
            (a1, a2, wb, wc), prep_vjp = jax.vjp(_s5_prep, *[W[n][layer] for n in S5_NAMES])
            wb16, wc16 = wb.astype(BF16), wc.astype(BF16)
            hn = hn_next if hn_next is not None else _rms_fwd(h, vec(W["g_mix"][layer]), name="mix_norm" + t)
            dvec = vec(W["ssm_d",][layer])
            ypre, yg, sb = S.run("s5_fwd" + t, _s5_fwd, hn, wb16, wc16, a1, a2, dvec)
            z = S.run("glu_mm" + t, _mm, yg, W["w_glu", layer], bk="bkn", ok="bmn", tm=4096)
            z = z.reshape(2, 4, L, SHARD_COLS_GLU)
            h1, hn2 = _glu_res_rms(z, h, vec(W["g_ffn"][layer]), name="glu_res" + t)
            mix_saved = (h, hn, ypre, yg, sb, z, a1, a2, wb16, wc16, dvec, prep_vjp)
        else:
            j = layer - N_A
            if layer == N_A:
                hkv = _rms_fwd(h, vec(W["g_kv"]), name="kv_norm")
                kvm = S.run("kv_mm", _mm, hkv, W["w_kv",], out_dtype=BF16)
                fl = S.run("f_mm", _mm, hkv, W["w_f",])
                cum = _fgate_fwd(fl, W["b_f_pad",], name="fgate_fwd")
                ck = cum[:, :N_HEADS].T.reshape(N_HEADS, 1, L)
                kvs = (h, hkv, fl, kvm, ck)
            _, _, _, kvm, ck = kvs
            hn = hn_next if hn_next is not None else _rms_fwd(h, vec(W["g_mix"][layer]), name="mix_norm" + t)
            q = S.run("q_mm" + t, _mm, hn, W["w_q", j], out_dtype=BF16)
            o, o32, lse = S.run("attn_fwd" + t, _attn_fwd, q, kvm, ck)
            h1, hn2 = S.run("o_mm" + t, _mm, o, W["w_o", j], add=h, norm_g=vec(W["g_ffn"][layer]))
            mix_saved = (h, hn, q, o32, o, lse)
        u0 = S.run("ffn_in" + t, _mm, hn2, W["w_ffn_in", layer], bk="nbk", ok="bmn", tm=2048).reshape(2, 4, L, CF)
        a = S.run("ffn_act" + t, _conv_act, u0, W["conv_w", layer], W["conv_b", layer])
        if layer + 1 < DEPTH:
            h2, hn_next = S.run("ffn_out" + t, _mm, a, W["w_ffn_out", layer], ak="bmk", bk="kbn", add=h1, kg=4,
                                norm_g=vec(W["g_mix"][layer + 1]))
        else:
            h2 = S.run("ffn_out" + t, _mm, a, W["w_ffn_out", layer], ak="bmk", bk="kbn", add=h1, tn=512, kg=4)
        saved.append((mix_saved, h1, hn2, u0))
        h = h2

    loss, dh, dg_final = _loss_head(h, vec(W["g_final"]), target, name="loss_head")
    g = {"g_final": dg_final.reshape(-1)}
    gl = {k: [None] * DEPTH for k in ("g_mix", "g_ffn", "conv_w", "ffn_conv_b")}
    ga = {k: [None] * N_A for k in S5_NAMES + ("ssm_d",)}
    dk = dv = dck = None
    for layer in reversed(range(DEPTH)):
        t = str(layer)
        mix_saved, h1, hn2, u0 = saved[layer]
        cw, cb = W["conv_w", layer], W["conv_b", layer]
        da = S.run("ffn_da" + t, _mm, dh, W["w_ffn_out", layer], bk="nbk", ok="bmn", tm=2048)
        a, du0, dcw, dcb = S.run("ffn_conv_bwd" + t, _conv_ffn_bwd, u0, cw, cb, da)
        dw_out = S.run("ffn_dwout" + t, _mm, a, dh, ak="bkm", ok="mbn", out_dtype=BF16)
        S.grad["w_ffn_out", layer] = dw_out.reshape(N_DEV, SHARD_ROWS_FFN, D_MODEL)
        du0 = du0.reshape(N_DEV, L, CF)
        S.grad["w_ffn_in", layer] = S.run("ffn_dwin" + t, _mm, du0, hn2, ak="bkm", ok="mbn", out_dtype=BF16)
        dhn2 = S.run("ffn_dhn" + t, _mm, du0, W["w_ffn_in", layer], ak="bmk", bk="kbn", kg=4)
        dh1, dg = _rms_bwd(h1, vec(W["g_ffn"][layer]), dhn2, dh, name="ffn_norm_bwd" + t)
        gl["g_ffn"][layer], gl["conv_w"][layer], gl["ffn_conv_b"][layer] = dg.reshape(-1), dcw, dcb.reshape(-1)
        if layer < N_A:
            hin, hn, ypre, yg, sb, z, a1, a2, wb16, wc16, dvec, prep_vjp = mix_saved
            dz = _glu_bwd(z, dh1, name="glu_bwd" + t).reshape(N_DEV, L, SHARD_COLS_GLU)
            S.grad["w_glu", layer] = S.run("glu_dw" + t, _mm, yg, dz, ak="km", bk="bkn", ok="bmn", out_dtype=BF16,
                                           tk=4096)
            dyg = S.run("glu_dy" + t, _mm, dz, W["w_glu", layer], ak="bmk", bk="bnk", out_dtype=BF16, tm=2048, kg=8)
            if layer == 0:
                S.bcast["repl_early",] = _pack_rows(_repl_early({**g, **gl, **ga}), REPL_EARLY_ROWS).astype(BF16)
            du, dwb, dwc, da1, da2, dd = S.run("s5_bwd" + t, _s5_bwd, hn, dyg, ypre, sb, wb16, wc16, a1, a2, dvec)
            for nme, val in zip(S5_NAMES, prep_vjp((da1, da2, dwb, dwc))):
                ga[nme][layer] = val
            ga["ssm_d"][layer] = dd.reshape(-1)
            dh, dg = _rms_bwd(hin, vec(W["g_mix"][layer]), du, dh1, name="mix_norm_bwd" + t)
        else:
            j = layer - N_A
            hin, hn, q, o32, o, lse = mix_saved
            _, _, _, kvm, ck = kvs
            S.grad["w_o", j] = S.run("o_dw" + t, _mm, o, dh1, ak="km", out_dtype=BF16
                                     ).reshape(N_DEV, SHARD_ROWS_QO, D_MODEL)
            do = S.run("o_dx" + t, _mm, dh1, W["w_o", j], bk="nk", out_dtype=BF16)
            delta = _attn_delta(do, o32, name="attn_delta" + t)
            dq_t, dk_l, dv_l, dck_l = S.run("attn_bwd" + t, _attn_bwd, q, kvm, ck.reshape(N_HEADS, L, 1), do,
                                            lse, delta)
            dk = dk_l if dk is None else dk + dk_l
            dv = dv_l if dv is None else dv + dv_l
            dck = dck_l if dck is None else dck + dck_l
            S.grad["w_q", j] = S.run("q_dw" + t, _mm, hn, dq_t, ak="km", bk="nk", out_dtype=BF16
                                     ).reshape(N_DEV, SHARD_ROWS_QO, D_MODEL)
            dh, dg = S.run("q_dx" + t, _mm, dq_t, W["w_q", j], ak="km", bk="nk",
                           rms_bwd=(hin, vec(W["g_mix"][layer]), dh1))
            if layer == N_A:
                hkv_in, hkv, fl, _, _ = kvs
                dcum = jnp.pad(dck.reshape(N_HEADS, L).T, ((0, 0), (0, LANES - N_HEADS)))
                dfl, dbf = _fgate_bwd(fl, W["b_f_pad",], dcum, name="fgate_bwd")
                dkv = jnp.concatenate([dk, dv], axis=1).astype(BF16)
                dfl16 = dfl.astype(BF16)
                dw_kv = S.run("kv_dw", _mm, hkv, dkv, ak="km")
                dw_f = S.run("f_dw", _mm, hkv, dfl16, ak="km")
                S.grad["w_kvf",] = _kvf_blocks(jnp.concatenate([dw_kv, dw_f[:, :N_HEADS]], axis=1)).astype(BF16)
                dhkv = S.run("kv_dx", _mm, dkv, W["w_kv",], bk="nk")
                dhkv = S.run("f_dx", _mm, dfl16, W["w_f",], bk="nk", add=dhkv)
                g["b_f"] = dbf[0, :N_HEADS]
                dh, dgkv = _rms_bwd(hkv_in, vec(W["g_kv"]), dhkv, dh, name="kv_norm_bwd")
                g["g_kv"] = dgkv.reshape(-1)
        gl["g_mix"][layer] = dg.reshape(-1)

    for d in (gl, ga):
        for k, v in d.items():
            g[k] = jnp.stack(v)
    return loss, dh, g


def _adamw_layers(slots, w, m, v, *, name):
    shape = w.shape
    nl = len(slots)
    w, m, v = (a.reshape((nl,) + a.shape[-2:]) for a in (w, m, v))
    _, R, C = w.shape
    tr = _tile(R, 256, 16)
    c1 = 1.0 / (1.0 - ADAM_B1 ** ADAM_STEP)
    c2 = 1.0 / (1.0 - ADAM_B2 ** ADAM_STEP)

    def body(*refs):
        s_refs, (w_ref, m_ref, v_ref), (g_ref, d_ref, nm_ref, nv_ref) = refs[:nl], refs[nl:nl + 3], refs[nl + 3:]
        for layer in range(nl):
            @pl.when(pl.program_id(0) == layer)
            def _(s_ref=s_refs[layer]):
                g = s_ref[0].astype(F32)
                for d in range(1, N_DEV):
                    g = g + s_ref[d].astype(F32)
                m2 = ADAM_B1 * m_ref[...] + (1.0 - ADAM_B1) * g
                v2 = ADAM_B2 * v_ref[...] + (1.0 - ADAM_B2) * (g * g)
                g_ref[...] = g
                nm_ref[...] = m2
                nv_ref[...] = v2
                d_ref[...] = -ADAM_LR * ((m2 * c1) / (jnp.sqrt(v2 * c2) + ADAM_EPS) + ADAM_WD * w_ref[...])

    def slab_spec(layer):
        return pl.BlockSpec((N_DEV, tr, C), lambda l, i: (0, jnp.where(l == layer, i, 0), 0))

    row = pl.BlockSpec((None, tr, C), lambda l, i: (l, i, 0))
    out = jax.ShapeDtypeStruct((nl, R, C), F32)
    outs = pl.pallas_call(
        body, grid=(nl, R // tr), in_specs=[slab_spec(layer) for layer in range(nl)] + [row, row, row],
        out_specs=[row, row, row, row], out_shape=[out, out, out, out],
        compiler_params=_params("arbitrary", "arbitrary"), name=name)(*slots, w, m, v)
    return [o.reshape(shape) for o in outs]


_SMALL_ROWS = 72
_ORDER = ("g_mix", "g_ffn", "lam_re", "lam_im", "log_dt", "ssm_b_re", "ssm_b_im", "ssm_c_re", "ssm_c_im", "ssm_d",
          "w_glu", "g_kv", "w_kvf", "b_f", "w_q", "w_o", "w_ffn_in", "ffn_conv_w", "ffn_conv_b", "w_ffn_out", "g_final")


def _pack_small(ssm_d, conv_w):
    flat = jnp.concatenate([ssm_d.reshape(-1), conv_w.reshape(-1)])
    return jnp.pad(flat, (0, _SMALL_ROWS * LANES - flat.shape[0])).reshape(_SMALL_ROWS, LANES)


def _unpack_small(flat):
    flat = flat.reshape(-1)
    return flat[:256].reshape(2, 128), flat[256:256 + 8448].reshape(4, 3, 704)


_W_IN_PARTS = {0: (352, 352), 1: (176, 528)}
_FWD_PLAN = {
    "start": [("small",)],
    "s5_fwd0": [("w_glu", 0), ("w_ffn_in", 0, 0)],
    "glu_mm0": [("w_ffn_in", 0, 1)],
    "ffn_in0": [("w_ffn_out", 0)],
    "ffn_act0": [("w_glu", 1)],
    "ffn_out0": [("w_ffn_in", 1, 0)],
    "s5_fwd1": [("w_ffn_in", 1, 1)],
    "ffn_in1": [("w_ffn_out", 1)],
    "ffn_act1": [("w_kvf",), ("w_o", 0)],
    "ffn_out1": [("w_q", 0)],
    "attn_fwd2": [("w_ffn_in", 2), ("w_ffn_out", 2), ("w_q", 1), ("w_o", 1)],
    "attn_fwd3": [("w_ffn_in", 3), ("w_ffn_out", 3)],
}
_BWD_PLAN = {
    "ffn_dhn3": [("w_ffn_out", 3)],
    "attn_bwd3": [("w_ffn_in", 3), ("w_o", 1)],
    "ffn_conv_bwd2": [("w_q", 1)],
    "ffn_dhn2": [("w_ffn_out", 2)],
    "attn_bwd2": [("w_ffn_in", 2), ("w_o", 0)],
    "ffn_conv_bwd1": [("w_q", 0), ("w_kvf",)],
    "ffn_dhn1": [("w_ffn_out", 1)],
    "s5_bwd1": [("w_ffn_in", 1), ("w_glu", 1)],
    "ffn_dhn0": [("w_ffn_out", 0)],
    "s5_bwd0": [("w_ffn_in", 0), ("w_glu", 0), ("repl_early",)],
    "end": [("small",), ("repl_late",)],
}
_PLAN = {h: [("w", k) for k in ks] for h, ks in _FWD_PLAN.items()}
_PLAN.update({h: [("b" if k[0].startswith("repl") else "g", k) for k in ks] for h, ks in _BWD_PLAN.items()})


def kernel(x, g_mix, g_ffn, lam_re, lam_im, log_dt, ssm_b_re, ssm_b_im, ssm_c_re, ssm_c_im, ssm_d, w_glu, g_kv, w_kvf, b_f, w_q, w_o, w_ffn_in, ffn_conv_w, ffn_conv_b, w_ffn_out, g_final, loss_target, m_g_mix, m_g_ffn, m_lam_re, m_lam_im, m_log_dt, m_ssm_b_re, m_ssm_b_im, m_ssm_c_re, m_ssm_c_im, m_ssm_d, m_w_glu, m_g_kv, m_w_kvf, m_b_f, m_w_q, m_w_o, m_w_ffn_in, m_ffn_conv_w, m_ffn_conv_b, m_w_ffn_out, m_g_final, v_g_mix, v_g_ffn, v_lam_re, v_lam_im, v_log_dt, v_ssm_b_re, v_ssm_b_im, v_ssm_c_re, v_ssm_c_im, v_ssm_d, v_w_glu, v_g_kv, v_w_kvf, v_b_f, v_w_q, v_w_o, v_w_ffn_in, v_ffn_conv_w, v_ffn_conv_b, v_w_ffn_out, v_g_final):
    wts = dict(g_mix=g_mix, g_ffn=g_ffn, lam_re=lam_re, lam_im=lam_im, log_dt=log_dt, ssm_b_re=ssm_b_re,
               ssm_b_im=ssm_b_im, ssm_c_re=ssm_c_re, ssm_c_im=ssm_c_im, ssm_d=ssm_d, w_glu=w_glu, g_kv=g_kv,
               w_kvf=w_kvf, b_f=b_f, w_q=w_q, w_o=w_o, w_ffn_in=w_ffn_in, ffn_conv_w=ffn_conv_w,
               ffn_conv_b=ffn_conv_b, w_ffn_out=w_ffn_out, g_final=g_final)
    mom = dict(g_mix=m_g_mix, g_ffn=m_g_ffn, lam_re=m_lam_re, lam_im=m_lam_im, log_dt=m_log_dt, ssm_b_re=m_ssm_b_re,
               ssm_b_im=m_ssm_b_im, ssm_c_re=m_ssm_c_re, ssm_c_im=m_ssm_c_im, ssm_d=m_ssm_d, w_glu=m_w_glu,
               g_kv=m_g_kv, w_kvf=m_w_kvf, b_f=m_b_f, w_q=m_w_q, w_o=m_w_o, w_ffn_in=m_w_ffn_in,
               ffn_conv_w=m_ffn_conv_w, ffn_conv_b=m_ffn_conv_b, w_ffn_out=m_w_ffn_out, g_final=m_g_final)
    var = dict(g_mix=v_g_mix, g_ffn=v_g_ffn, lam_re=v_lam_re, lam_im=v_lam_im, log_dt=v_log_dt, ssm_b_re=v_ssm_b_re,
               ssm_b_im=v_ssm_b_im, ssm_c_re=v_ssm_c_re, ssm_c_im=v_ssm_c_im, ssm_d=v_ssm_d, w_glu=v_w_glu,
               g_kv=v_g_kv, w_kvf=v_w_kvf, b_f=v_b_f, w_q=v_w_q, w_o=v_w_o, w_ffn_in=v_w_ffn_in,
               ffn_conv_w=v_ffn_conv_w, ffn_conv_b=v_ffn_conv_b, w_ffn_out=v_w_ffn_out, g_final=v_g_final)
    kinds = ("grad", "delta", "m", "v")

    ready = {n: wts[n] for n in ("g_mix", "g_ffn", "g_kv", "g_final") + S5_NAMES}
    ready["b_f_pad",] = jnp.pad(b_f, (0, LANES - N_HEADS)).reshape(1, LANES)
    send = {("small",): _pack_small(ssm_d, ffn_conv_w), ("w_kvf",): w_kvf.astype(BF16)}
    for layer in range(DEPTH):
        ready["conv_b", layer] = ffn_conv_b[layer].reshape(2, 4, 1, SHARD_COLS_FFN)
        w_in_t = jnp.swapaxes(w_ffn_in[layer], 0, 1).astype(BF16)
        if layer in _W_IN_PARTS:
            row = 0
            for p, rows in enumerate(_W_IN_PARTS[layer]):
                send["w_ffn_in", layer, p] = w_in_t[row:row + rows]
                row += rows
        else:
            send["w_ffn_in", layer] = w_in_t
        send["w_ffn_out", layer] = w_ffn_out[layer].astype(BF16)
    for layer in range(N_A):
        send["w_glu", layer] = w_glu[layer].astype(BF16)
        send["w_q", layer] = w_q[layer].astype(BF16)
        send["w_o", layer] = w_o[layer].astype(BF16)

    S = _Step(ready, send, _PLAN)
    S.land("start", _exchange(S.xfers("start"), name="start"))
    loss, dx, g = _step(x[0], loss_target[0], S)
    loss = lax.psum(loss[0, 0], MESH_AXES)

    g_d = g["ssm_d"].reshape(N_A, N_DEV, LANES).transpose(1, 0, 2).reshape(N_DEV, N_A * LANES)
    g_cw = jnp.stack([g["conv_w"][layer].reshape(N_DEV, 3, SHARD_COLS_FFN) for layer in range(DEPTH)], axis=1)
    g_small = jnp.concatenate([g_d, g_cw.reshape(N_DEV, -1)], axis=1)
    g_small = jnp.pad(g_small, ((0, 0), (0, _SMALL_ROWS * LANES - g_small.shape[1])))
    S.grad["small",] = g_small.reshape(N_DEV, _SMALL_ROWS, LANES)
    S.bcast["repl_late",] = _pack_rows(_repl_late(g), REPL_LATE_ROWS).astype(BF16)
    S.land("end", _exchange(S.xfers("end"), name="end"))

    res = {}
    for name, nl in (("w_glu", N_A), ("w_q", DEPTH - N_A), ("w_o", DEPTH - N_A), ("w_ffn_in", DEPTH),
                     ("w_ffn_out", DEPTH)):
        view = (lambda a: jnp.swapaxes(a, 1, 2)) if name == "w_ffn_in" else (lambda a: a)
        outs = _adamw_layers([S.slots[name, layer] for layer in range(nl)], view(wts[name]), view(mom[name]),
                             view(var[name]), name="adamw_" + name)
        res.update({(kind, name): view(a) for kind, a in zip(kinds, outs)})
    outs = _adamw_layers([S.slots["w_kvf",]], w_kvf, m_w_kvf, v_w_kvf, name="adamw_w_kvf")
    res.update({(kind, "w_kvf"): a for kind, a in zip(kinds, outs)})
    outs = _adamw_layers([S.slots["small",]], _pack_small(ssm_d, ffn_conv_w), _pack_small(m_ssm_d, m_ffn_conv_w),
                         _pack_small(v_ssm_d, v_ffn_conv_w), name="adamw_small")
    for kind, flat in zip(kinds, outs):
        res[kind, "ssm_d"], res[kind, "ffn_conv_w"] = _unpack_small(flat)

    pieces = {}
    for key, rows, sel in ((("repl_early",), REPL_EARLY_ROWS, _repl_early), (("repl_late",), REPL_LATE_ROWS, _repl_late)):
        outs = _adamw_layers([S.slots[key]], *[_pack_rows(sel(d), rows) for d in (wts, mom, var)],
                             name="adamw_" + key[0])
        for kind, flat in zip(kinds, outs):
            pieces[kind, key[0]] = _unpack_rows(flat, sel(wts))
    for kind in kinds:
        early, late = iter(pieces[kind, "repl_early"]), iter(pieces[kind, "repl_late"])
        take = lambda it, n: [next(it) for _ in range(n)]
        res[kind, "g_mix"] = jnp.stack(take(late, 1) + take(early, DEPTH - 1))
        res[kind, "g_ffn"] = jnp.stack(take(early, DEPTH))
        for n in S5_NAMES:
            res[kind, n] = jnp.stack([next(late), next(early)]).reshape(wts[n].shape)
        res[kind, "g_kv"], res[kind, "b_f"] = next(early), next(early)
        res[kind, "ffn_conv_b"] = jnp.stack(take(early, DEPTH))
        res[kind, "g_final"] = next(early)

    return (loss, dx[None], *[res[kind, n] for kind in kinds for n in _ORDER])
```

```python
import functools
import math

import jax
import jax.numpy as jnp
from jax import lax
from jax.experimental import pallas as pl
from jax.experimental.pallas import tpu as pltpu

F32 = jnp.float32
BF16 = jnp.bfloat16

D_MODEL = 1024
DEPTH = 4
N_A = 2
N_GROUPS = 64
SSM_GROUP = 16
SSM_STATE = 64
N_HEADS = 16
HEAD_DIM = 64
ATTN_SCALE = HEAD_DIM ** -0.5
D_FF = 2816
EPS = 1e-6
N_DEV = 8
LANES = 128
SUBLANES = 8

ADAM_LR = 0.001
ADAM_B1 = 0.9
ADAM_B2 = 0.999
ADAM_EPS = 1e-08
ADAM_WD = 0.01
ADAM_STEP = 10

ROW_TILE = 512
S5_CHUNK = 256
ATTN_TILE = 512
CUM_TILE = 256
NEG = -1e30

MESH_AXES = ("x", "y", "c")


def _tile(n, target, align=LANES):
    t = (min(target, n) // align) * align
    while t >= align:
        if n % t == 0:
            return t
        t -= align
    return n


def _params(*sem):
    return pltpu.CompilerParams(dimension_semantics=sem, vmem_limit_bytes=56 * 1024 * 1024)


_ANY = pl.BlockSpec(memory_space=pl.ANY)
_XFER_SEMS = (pltpu.SemaphoreType.DMA((N_DEV - 1,)), pltpu.SemaphoreType.DMA((N_DEV - 1,)), pltpu.SemaphoreType.DMA)


def _xfer_copies(x_ref, o_ref, send_sems, recv_sems, local_sem, scatter, row_off):
    xi, yi, ci = lax.axis_index("x"), lax.axis_index("y"), lax.axis_index("c")
    me = 4 * xi + 2 * yi + ci

    def src(p):
        return x_ref.at[p] if scatter else x_ref

    def dst(p):
        return o_ref.at[p] if row_off is None else o_ref.at[p, pl.ds(row_off, x_ref.shape[0])]

    own = pltpu.make_async_copy(src(me), dst(me), local_sem)
    sends, recvs = [], []
    for k in range(1, N_DEV):
        px, py, pc = xi ^ (k >> 2), yi ^ ((k >> 1) & 1), ci ^ (k & 1)
        p = 4 * px + 2 * py + pc
        sends.append(pltpu.make_async_remote_copy(
            src_ref=src(p), dst_ref=dst(me), send_sem=send_sems.at[k - 1], recv_sem=recv_sems.at[k - 1],
            device_id=(px, py, pc), device_id_type=pl.DeviceIdType.MESH))
        recvs.append(pltpu.make_async_remote_copy(
            src_ref=src(p), dst_ref=dst(p), send_sem=send_sems.at[k - 1], recv_sem=recv_sems.at[k - 1],
            device_id=(px, py, pc), device_id_type=pl.DeviceIdType.MESH))
    return own, sends, recvs


def _xfer_start(*refs, scatter, row_off):
    own, sends, _ = _xfer_copies(*refs, scatter, row_off)
    own.start()
    for cp in sends:
        cp.start()


def _xfer_wait(*refs, scatter, row_off):
    own, sends, recvs = _xfer_copies(*refs, scatter, row_off)
    for cp in recvs:
        cp.wait_recv()
    for cp in sends:
        cp.wait_send()
    own.wait()


def _pcall(body, args, *, grid, in_specs, out_specs, out_shape, scratch_shapes=(), sem, name, xfers=(), prefetch=()):
    out_specs, out_shape = list(out_specs), list(out_shape)
    xfers = [tuple(x) + (None,) * (3 - len(x)) for x in xfers]
    n_pre, n_in, n_out, n_x, n_scr = len(prefetch), len(in_specs), len(out_specs), len(xfers), len(scratch_shapes)
    flags = [(s, None if w is None else w[1]) for _, s, w in xfers]
    prevs = [(t, w[2]) for t, (_, _, w) in enumerate(xfers) if w is not None and w[2] is not None]
    n_b = len(prevs)
    assert not (prevs and prefetch)

    def xfer_shape(x, scatter, w):
        if w is not None:
            return jax.ShapeDtypeStruct((N_DEV, w[0]) + x.shape[1:], x.dtype)
        return jax.ShapeDtypeStruct((N_DEV,) + (x.shape[1:] if scatter else x.shape), x.dtype)

    def wrapped(*refs):
        pre, refs = refs[:n_pre], refs[n_pre:]
        ins, xin = refs[:n_in], refs[n_in:n_in + n_x]
        refs = refs[n_in + n_x + n_b:]
        outs, xout, scr = refs[:n_out], refs[n_out:n_out + n_x], refs[n_out + n_x:]
        own, sems = scr[:n_scr], scr[n_scr:]
        ids = [pl.program_id(d) for d in range(len(grid))]
        first = functools.reduce(jnp.logical_and, [i == 0 for i in ids])
        last = functools.reduce(jnp.logical_and, [i == g - 1 for i, g in zip(ids, grid)])

        @pl.when(first)
        def _():
            for t in range(n_x):
                _xfer_start(xin[t], xout[t], *sems[3 * t:3 * t + 3], scatter=flags[t][0], row_off=flags[t][1])

        body(*pre, *ins, *outs, *own)

        @pl.when(last)
        def _():
            for t in range(n_x):
                _xfer_wait(xin[t], xout[t], *sems[3 * t:3 * t + 3], scatter=flags[t][0], row_off=flags[t][1])

    grid_spec = pltpu.PrefetchScalarGridSpec(
        num_scalar_prefetch=n_pre, grid=grid, in_specs=list(in_specs) + [_ANY] * (n_x + n_b),
        out_specs=out_specs + [_ANY] * n_x, scratch_shapes=list(scratch_shapes) + list(_XFER_SEMS) * n_x)
    return pl.pallas_call(
        wrapped if xfers else body, grid_spec=grid_spec, out_shape=out_shape + [xfer_shape(*x) for x in xfers],
        input_output_aliases={n_in + n_x + b: n_out + t for b, (t, _) in enumerate(prevs)},
        compiler_params=_params(*(["arbitrary"] * len(grid) if xfers else sem)), name=name,
    )(*prefetch, *args, *[x[0] for x in xfers], *[p for _, p in prevs])


def _exchange(xfers, *, name):
    def body():
        pass

    return _pcall(body, (), grid=(1,), in_specs=[], out_specs=[], out_shape=[], sem=("arbitrary",), name=name,
                  xfers=xfers)


def _mm(a, b, *, ak="mk", bk="kn", ok="mn", add=None, norm_g=None, rms_bwd=None, out_dtype=F32, tm=1024, tn=1024,
        tk=1024, kg=1, name, xfers=()):
    sa, sb = a.shape, b.shape
    fm = fn = fk = None
    if ak == "mk":
        M, K, a_c = sa[0], sa[1], 1
    elif ak == "km":
        K, M, a_c = sa[0], sa[1], 0
    elif ak == "bmk":
        M, K, a_c, fk = sa[1], sa[0] * sa[2], 1, sa[2]
    else:
        K, M, a_c, fm = sa[1], sa[0] * sa[2], 0, sa[2]
    if bk == "kn":
        N, b_c = sb[1], 0
    elif bk == "nk":
        N, b_c = sb[0], 1
    elif bk == "bkn":
        N, b_c, fn = sb[0] * sb[2], 0, sb[2]
    elif bk == "bnk":
        N, b_c, fk = sb[1], 1, sb[2]
    elif bk == "kbn":
        N, b_c, fk = sb[2], 0, sb[1]
    else:
        N, b_c, fn = sb[0] * sb[1], 1, sb[1]
    tm, tn, tk = fm or _tile(M, tm), fn or _tile(N, tn), fk or _tile(K, tk)
    kblk = None if kg == 1 else kg
    nm, nn, nk = M // tm, N // tn, K // (tk * kg)

    a_spec = {"mk": pl.BlockSpec((tm, tk), lambda i, j, k: (i, k)),
              "km": pl.BlockSpec((tk, tm), lambda i, j, k: (k, i)),
              "bmk": pl.BlockSpec((kblk, tm, tk), lambda i, j, k: (k, i, 0)),
              "bkm": pl.BlockSpec((None, tk, tm), lambda i, j, k: (i, k, 0))}[ak]
    b_spec = {"kn": pl.BlockSpec((tk, tn), lambda i, j, k: (k, j)),
              "nk": pl.BlockSpec((tn, tk), lambda i, j, k: (j, k)),
              "bkn": pl.BlockSpec((None, tk, tn), lambda i, j, k: (j, k, 0)),
              "bnk": pl.BlockSpec((kblk, tn, tk), lambda i, j, k: (k, j, 0)),
              "kbn": pl.BlockSpec((kblk, tk, tn), lambda i, j, k: (k, 0, j)),
              "nbk": pl.BlockSpec((None, tn, tk), lambda i, j, k: (j, 0, k))}[bk]
    if ok == "mn":
        o_spec = pl.BlockSpec((tm, tn), lambda i, j, k: (i, j))
        out_shape = jax.ShapeDtypeStruct((M, N), out_dtype)
    elif ok == "bmn":
        o_spec = pl.BlockSpec((None, tm, tn), lambda i, j, k: (j, i, 0))
        out_shape = jax.ShapeDtypeStruct((nn, M, tn), out_dtype)
    else:
        o_spec = pl.BlockSpec((None, tm, tn), lambda i, j, k: (i, 0, j))
        out_shape = jax.ShapeDtypeStruct((nm, tm, N), out_dtype)
    dims = (((a_c,), (b_c,)), ((), ()))
    has_add = add is not None
    n_extra_in = int(has_add) + int(norm_g is not None) + (3 if rms_bwd is not None else 0)
    if norm_g is not None or rms_bwd is not None:
        assert ok == "mn" and tn == N

    def body(*refs):
        a_ref, b_ref = refs[0], refs[1]
        extra = list(refs[2:2 + n_extra_in])
        add_ref = extra.pop(0) if has_add else None
        ng_ref = extra.pop(0) if norm_g is not None else None
        outs = list(refs[2 + n_extra_in:])
        o_ref = outs.pop(0)
        hn_ref = outs.pop(0) if norm_g is not None else None
        dg_ref = outs.pop(0) if rms_bwd is not None else None
        if kg == 1:
            part = lax.dot_general(a_ref[...].astype(BF16), b_ref[...].astype(BF16), dims, preferred_element_type=F32)
        else:
            part = sum(lax.dot_general(a_ref[g].astype(BF16), b_ref[g].astype(BF16), dims,
                                       preferred_element_type=F32) for g in range(kg))
        if rms_bwd is not None:
            @pl.when(jnp.logical_and(pl.program_id(0) == 0, pl.program_id(2) == 0))
            def _():
                dg_ref[...] = jnp.zeros_like(dg_ref)

        def finish(r):
            if has_add:
                r = r + add_ref[...]
            if rms_bwd is not None:
                h_ref, g_ref, dres_ref = extra
                x = h_ref[...]
                rr = lax.rsqrt(jnp.mean(x * x, axis=1, keepdims=True) + EPS)
                xn = x * rr
                gdy = r * g_ref[...]
                dg_ref[...] += jnp.sum(r * xn, axis=0, keepdims=True)
                r = dres_ref[...] + rr * (gdy - xn * jnp.mean(gdy * xn, axis=1, keepdims=True))
            o_ref[...] = r.astype(out_dtype)
            if norm_g is not None:
                hn_ref[...] = (r * lax.rsqrt(jnp.mean(r * r, axis=1, keepdims=True) + EPS) * ng_ref[...]).astype(BF16)

        if nk == 1:
            finish(part)
            return
        acc = refs[-1]
        k = pl.program_id(2)

        @pl.when(k == 0)
        def _():
            acc[...] = part

        @pl.when(k > 0)
        def _():
            acc[...] += part

        @pl.when(k == nk - 1)
        def _():
            finish(acc[...])

    tile_spec = pl.BlockSpec((tm, tn), lambda i, j, k: (i, j))
    vec_spec = pl.BlockSpec((1, tn), lambda i, j, k: (0, j))
    in_specs, args = [a_spec, b_spec], [a, b]
    out_specs, out_shapes = [o_spec], [out_shape]
    if has_add:
        in_specs.append(tile_spec)
        args.append(add)
    if norm_g is not None:
        in_specs.append(vec_spec)
        args.append(norm_g)
        out_specs.append(tile_spec)
        out_shapes.append(jax.ShapeDtypeStruct((M, N), BF16))
    if rms_bwd is not None:
        in_specs += [tile_spec, vec_spec, tile_spec]
        args += list(rms_bwd)
        out_specs.append(vec_spec)
        out_shapes.append(jax.ShapeDtypeStruct((1, N), F32))
    res = _pcall(body, args, grid=(nm, nn, nk), in_specs=in_specs, out_specs=out_specs, out_shape=out_shapes,
                 scratch_shapes=[pltpu.VMEM((tm, tn), F32)] if nk > 1 else [],
                 sem=("arbitrary" if rms_bwd is not None else "parallel", "parallel", "arbitrary"), name=name,
                 xfers=xfers)
    return res if (xfers or len(out_specs) > 1) else res[0]


def _rms_fwd(h, g, *, name):
    L, D = h.shape
    tr = _tile(L, ROW_TILE, SUBLANES)

    def body(h_ref, g_ref, o_ref):
        x = h_ref[...]
        r = lax.rsqrt(jnp.mean(x * x, axis=1, keepdims=True) + EPS)
        o_ref[...] = (x * r * g_ref[...]).astype(BF16)

    return pl.pallas_call(
        body, grid=(L // tr,),
        in_specs=[pl.BlockSpec((tr, D), lambda i: (i, 0)), pl.BlockSpec((1, D), lambda i: (0, 0))],
        out_specs=pl.BlockSpec((tr, D), lambda i: (i, 0)), out_shape=jax.ShapeDtypeStruct((L, D), BF16),
        compiler_params=_params("parallel"), name=name)(h, g)


def _glu_res_rms(z, h, g, *, name):
    L, D = h.shape
    nb, cb = z.shape[1], z.shape[3]
    tr = _tile(L, ROW_TILE, SUBLANES)

    def body(z_ref, h_ref, g_ref, h1_ref, hn_ref):
        za = jnp.concatenate([z_ref[0, d] for d in range(nb)], axis=1)
        zg = jnp.concatenate([z_ref[1, d] for d in range(nb)], axis=1)
        x = h_ref[...] + za * jax.nn.sigmoid(zg)
        h1_ref[...] = x
        r = lax.rsqrt(jnp.mean(x * x, axis=1, keepdims=True) + EPS)
        hn_ref[...] = (x * r * g_ref[...]).astype(BF16)

    row = pl.BlockSpec((tr, D), lambda i: (i, 0))
    return pl.pallas_call(
        body, grid=(L // tr,),
        in_specs=[pl.BlockSpec((2, nb, tr, cb), lambda i: (0, 0, i, 0)), row, pl.BlockSpec((1, D), lambda i: (0, 0))],
        out_specs=[row, row],
        out_shape=[jax.ShapeDtypeStruct((L, D), F32), jax.ShapeDtypeStruct((L, D), BF16)],
        compiler_params=_params("parallel"), name=name)(z, h, g)


def _glu_bwd(z, dout, *, name):
    L, D = dout.shape
    nb, cb = z.shape[1], z.shape[3]
    tr = _tile(L, ROW_TILE, SUBLANES)

    def body(z_ref, d_ref, o_ref):
        for d in range(nb):
            dd = d_ref[:, d * cb:(d + 1) * cb]
            sg = jax.nn.sigmoid(z_ref[1, d])
            o_ref[0, d] = (dd * sg).astype(BF16)
            o_ref[1, d] = (dd * z_ref[0, d] * sg * (1.0 - sg)).astype(BF16)

    zs = pl.BlockSpec((2, nb, tr, cb), lambda i: (0, 0, i, 0))
    return pl.pallas_call(
        body, grid=(L // tr,), in_specs=[zs, pl.BlockSpec((tr, D), lambda i: (i, 0))], out_specs=zs,
        out_shape=jax.ShapeDtypeStruct(z.shape, BF16),
        compiler_params=_params("parallel"), name=name)(z, dout)


def _loss_head(h, g, target, *, name):
    L, D = h.shape
    tr = _tile(L, ROW_TILE, SUBLANES)

    def body(h_ref, g_ref, t_ref, loss_ref, dh_ref, dg_ref):
        @pl.when(pl.program_id(0) == 0)
        def _():
            dg_ref[...] = jnp.zeros_like(dg_ref)
            loss_ref[...] = jnp.zeros_like(loss_ref)

        x = h_ref[...]
        gg = g_ref[...]
        r = lax.rsqrt(jnp.mean(x * x, axis=1, keepdims=True) + EPS)
        xn = x * r
        err = xn * gg - t_ref[...]
        loss_ref[...] += 0.5 * jnp.sum(jnp.mean(err * err, axis=1, keepdims=True), axis=0, keepdims=True)
        dy = err * (1.0 / D)
        gdy = dy * gg
        dh_ref[...] = r * (gdy - xn * jnp.mean(gdy * xn, axis=1, keepdims=True))
        dg_ref[...] += jnp.sum(dy * xn, axis=0, keepdims=True)

    row = pl.BlockSpec((tr, D), lambda i: (i, 0))
    vec = pl.BlockSpec((1, D), lambda i: (0, 0))
    return pl.pallas_call(
        body, grid=(L // tr,), in_specs=[row, vec, row],
        out_specs=[pl.BlockSpec((1, 1), lambda i: (0, 0)), row, vec],
        out_shape=[jax.ShapeDtypeStruct((1, 1), F32), jax.ShapeDtypeStruct((L, D), F32),
                   jax.ShapeDtypeStruct((1, D), F32)],
        compiler_params=_params("arbitrary"), name=name)(h, g, target)


CONV_ROW_TILE = 256


def _sigmoid(x):
    return pl.reciprocal(1.0 + jnp.exp(-x), approx=True)


def _conv_specs(L, tr, tc):
    nrb = tr // SUBLANES
    before = lambda i: jnp.maximum(i * nrb - 1, 0)
    after = lambda i: jnp.minimum((i + 1) * nrb, L // SUBLANES - 1)
    main = pl.BlockSpec((2, None, tr, tc), lambda j, i: (0, j, i, 0))
    prev = pl.BlockSpec((2, None, SUBLANES, tc), lambda j, i: (0, j, before(i), 0))
    nxt = pl.BlockSpec((2, None, SUBLANES, tc), lambda j, i: (0, j, after(i), 0))
    cw = pl.BlockSpec((2, None, 3, tc), lambda j, i: (0, j, 0, 0))
    cb = pl.BlockSpec((2, None, 1, tc), lambda j, i: (0, j, 0, 0))
    half = pl.BlockSpec((None, tr, tc), lambda j, i: (j, i, 0))
    half_nxt = pl.BlockSpec((None, SUBLANES, tc), lambda j, i: (j, after(i), 0))
    return main, prev, nxt, cw, cb, half, half_nxt


def _conv_rows(xe, w, b):
    x1 = pltpu.roll(xe, 1, 0)
    x2 = pltpu.roll(xe, 2, 0)
    return b + x2 * w[0:1] + x1 * w[1:2] + xe * w[2:3], x1, x2


def _shift_down(x, halo, k, row):
    y = pltpu.roll(x, k, 0)
    for r in range(k):
        y = jnp.where(row == r, halo[SUBLANES - k + r:SUBLANES - k + r + 1, :], y)
    return y


def _conv_act(u0, cw, cb, *, name, xfers=()):
    _, nb, L, tc = u0.shape
    tr = _tile(L, ROW_TILE, SUBLANES)
    main, prev, _, cws, cbs, half, _ = _conv_specs(L, tr, tc)

    def body(u_ref, p_ref, w_ref, b_ref, a_ref):
        first = pl.program_id(1) == 0
        row = lax.broadcasted_iota(jnp.int32, (tr, tc), 0)
        y = []
        for s in range(2):
            x, w = u_ref[s], w_ref[s]
            halo = jnp.where(first, 0.0, p_ref[s])
            x1 = _shift_down(x, halo, 1, row)
            x2 = _shift_down(x, halo, 2, row)
            y.append(b_ref[s] + x2 * w[0:1] + x1 * w[1:2] + x * w[2:3])
        a_ref[...] = (y[0] * _sigmoid(y[0]) * y[1]).astype(BF16)

    return _pcall(body, (u0, u0, cw, cb), grid=(nb, L // tr), in_specs=[main, prev, cws, cbs], out_specs=[half],
                  out_shape=[jax.ShapeDtypeStruct((nb, L, tc), BF16)], sem=("parallel", "parallel"), name=name,
                  xfers=xfers)


def _conv_ffn_bwd(u0, cw, cb, da, *, name, xfers=()):
    _, nb, L, tc = u0.shape
    tr = _tile(L, CONV_ROW_TILE, SUBLANES)
    main, prev, nxt, cws, cbs, half, half_nxt = _conv_specs(L, tr, tc)
    nr = L // tr
    H = SUBLANES

    def body(u_ref, p_ref, n_ref, w_ref, b_ref, da_ref, dan_ref, a_ref, du0_ref, dcw_ref, dcb_ref):
        i = pl.program_id(1)

        @pl.when(i == 0)
        def _():
            dcw_ref[...] = jnp.zeros_like(dcw_ref)
            dcb_ref[...] = jnp.zeros_like(dcb_ref)

        y, x1, x2 = [], [], []
        for s in range(2):
            xe = jnp.concatenate([jnp.where(i == 0, 0.0, p_ref[s]), u_ref[s], n_ref[s]], axis=0)
            ys, x1s, x2s = _conv_rows(xe, w_ref[s], b_ref[s])
            y.append(ys[H:])
            x1.append(x1s[H:H + tr])
            x2.append(x2s[H:H + tr])
        gate, up = y
        da = jnp.concatenate([da_ref[...], dan_ref[...]], axis=0)
        row = lax.broadcasted_iota(jnp.int32, (tr + H, tc), 0)
        da = jnp.where(jnp.logical_and(i == nr - 1, row >= tr), 0.0, da)
        sg = _sigmoid(gate)
        silu = gate * sg
        a_ref[...] = (silu * up)[:tr].astype(BF16)
        d = (da * up * (sg * (1.0 + gate * (1.0 - sg))), da * silu)
        for s in range(2):
            w = w_ref[s]
            d0 = d[s][:tr]
            d1 = pltpu.roll(d[s], tr + H - 1, 0)[:tr]
            d2 = pltpu.roll(d[s], tr + H - 2, 0)[:tr]
            du0_ref[s] = (d0 * w[2:3] + d1 * w[1:2] + d2 * w[0:1]).astype(BF16)
            dcw_ref[s, 0:1, :] += jnp.sum(d0 * x2[s], axis=0, keepdims=True)
            dcw_ref[s, 1:2, :] += jnp.sum(d0 * x1[s], axis=0, keepdims=True)
            dcw_ref[s, 2:3, :] += jnp.sum(d0 * u_ref[s], axis=0, keepdims=True)
            dcb_ref[s] += jnp.sum(d0, axis=0, keepdims=True)

    return _pcall(body, (u0, u0, u0, cw, cb, da, da), grid=(nb, nr),
                  in_specs=[main, prev, nxt, cws, cbs, half, half_nxt], out_specs=[half, main, cws, cbs],
                  out_shape=[jax.ShapeDtypeStruct((nb, L, tc), BF16), jax.ShapeDtypeStruct((2, nb, L, tc), BF16),
                             jax.ShapeDtypeStruct((2, nb, 3, tc), F32), jax.ShapeDtypeStruct((2, nb, 1, tc), F32)],
                  sem=("parallel", "arbitrary"), name=name, xfers=xfers)


N_TILES = 64
HALF = N_TILES // 2


def _swap(s):
    return jnp.concatenate([s[HALF:], s[:HALF]], axis=0)


def _chan_block(j):
    return ((j % HALF) // 4) * LANES


def _pairs_of(jb):
    return [2 * jb, 2 * jb + 1, HALF // 2 + 2 * jb, HALF // 2 + 2 * jb + 1]


_WB_SPEC = pl.BlockSpec((HALF, LANES, 2 * LANES), lambda c: (0, 0, 0))
_WC_SPEC = _WB_SPEC
GELU_C = math.sqrt(2.0 / math.pi)
GELU_A = 0.044715


def _gelu(x):
    return 0.5 * x * (1.0 + jnp.tanh(GELU_C * (x + GELU_A * x * x * x)))


def _gelu_grad(x):
    th = jnp.tanh(GELU_C * (x + GELU_A * x * x * x))
    return 0.5 * (1.0 + th) + 0.5 * x * (1.0 - th * th) * GELU_C * (1.0 + 3.0 * GELU_A * x * x)


def _tile_rows(j, T, TP):
    return pl.ds(j * TP + SUBLANES, T)


def _pair(ref, jp, T, TP):
    return jnp.concatenate([ref[_tile_rows(2 * jp, T, TP), :], ref[_tile_rows(2 * jp + 1, T, TP), :]],
                           axis=1).astype(BF16)


def _unpair(ref, jp, val, T, TP):
    ref[_tile_rows(2 * jp, T, TP), :] = val[:, :LANES]
    ref[_tile_rows(2 * jp + 1, T, TP), :] = val[:, LANES:]


def _s5_project_in(u_ref, wb_ref, s3, T, TP):
    for jp in range(HALF):
        blk = _chan_block(2 * jp)
        _unpair(s3, jp, jnp.dot(u_ref[:, blk:blk + LANES], wb_ref[jp], preferred_element_type=F32), T, TP)


def _s5_scan_fwd(s3, a1, a2, s0, T, TP):
    span = (N_GROUPS - 1) * TP + 2 * SUBLANES

    def blk(i, s):
        view = s3.at[pl.ds(pl.multiple_of(i * SUBLANES, SUBLANES), span)]
        for k in range(SUBLANES):
            rows = pl.ds(SUBLANES + k, N_GROUPS, stride=TP)
            s = a1 * s + a2 * _swap(s) + view[rows, :]
            view[rows, :] = s
        return s

    return lax.fori_loop(0, T // SUBLANES, blk, s0)


def _s5_fwd(hn, wb, wc, a1, a2, dvec, *, name, xfers=()):
    L, D = hn.shape
    T = min(S5_CHUNK, L)
    TP = T + SUBLANES
    nC = L // T

    def body(u_ref, wb_ref, wc_ref, a1_ref, a2_ref, d_ref, y_ref, yg_ref, sb_ref, s3, st):
        @pl.when(pl.program_id(0) == 0)
        def _():
            st[...] = jnp.zeros_like(st)

        sb_ref[0] = st[...]
        _s5_project_in(u_ref, wb_ref, s3, T, TP)
        st[...] = _s5_scan_fwd(s3, a1_ref[...], a2_ref[...], st[...], T, TP)
        for jb in range(D // LANES):
            acc = jnp.zeros((T, LANES), F32)
            for jp in _pairs_of(jb):
                acc += lax.dot_general(_pair(s3, jp, T, TP), wc_ref[jp], _NT, preferred_element_type=F32)
            cols = slice(jb * LANES, (jb + 1) * LANES)
            y = acc + d_ref[:, cols] * u_ref[:, cols].astype(F32)
            y_ref[:, cols] = y
            yg_ref[:, cols] = _gelu(y).astype(BF16)

    row = pl.BlockSpec((T, D), lambda c: (c, 0))
    aspec = pl.BlockSpec((N_GROUPS, LANES), lambda c: (0, 0))
    return _pcall(
        body, (hn, wb, wc, a1, a2, dvec), grid=(nC,),
        in_specs=[row, _WB_SPEC, _WC_SPEC, aspec, aspec, pl.BlockSpec((1, D), lambda c: (0, 0))],
        out_specs=[row, row, pl.BlockSpec((1, N_GROUPS, LANES), lambda c: (c, 0, 0))],
        out_shape=[jax.ShapeDtypeStruct((L, D), F32), jax.ShapeDtypeStruct((L, D), BF16),
                   jax.ShapeDtypeStruct((nC, N_GROUPS, LANES), F32)],
        scratch_shapes=[pltpu.VMEM((N_GROUPS * TP, LANES), F32), pltpu.VMEM((N_GROUPS, LANES), F32)],
        sem=("arbitrary",), name=name, xfers=xfers)


def _s5_bwd(hn, dyg, ypre, sbound, wb, wc, a1, a2, dvec, h, g, dres, *, name, xfers=()):
    L, D = hn.shape
    T = min(S5_CHUNK, L)
    TP = T + SUBLANES
    nC = L // T
    span = (N_GROUPS - 1) * TP + 2 * SUBLANES
    NT = (((1,), (1,)), ((), ()))
    TN = (((0,), (0,)), ((), ()))

    def body(u_ref, dyg_ref, yp_ref, sb_ref, wb_ref, wc_ref, a1_ref, a2_ref, d_ref, h_ref, g_ref, dres_ref,
             du_ref, dwb_ref, dwc_ref, da1_ref, da2_ref, dd_ref, dg_ref, s3, g3, gst, dy_s):
        @pl.when(pl.program_id(0) == 0)
        def _():
            gst[...] = jnp.zeros_like(gst)
            dwb_ref[...] = jnp.zeros_like(dwb_ref)
            dwc_ref[...] = jnp.zeros_like(dwc_ref)
            da1_ref[...] = jnp.zeros_like(da1_ref)
            da2_ref[...] = jnp.zeros_like(da2_ref)
            dd_ref[...] = jnp.zeros_like(dd_ref)
            dg_ref[...] = jnp.zeros_like(dg_ref)

        a1 = a1_ref[...]
        a2 = a2_ref[...]
        dy = dyg_ref[...].astype(F32) * _gelu_grad(yp_ref[...])
        dy_s[...] = dy.astype(BF16)
        dd_ref[...] += jnp.sum(dy * u_ref[...].astype(F32), axis=0, keepdims=True)
        du_ref[...] = d_ref[...] * dy

        s3[pl.ds(SUBLANES - 1, N_GROUPS, stride=TP), :] = sb_ref[0]
        _s5_project_in(u_ref, wb_ref, s3, T, TP)
        _s5_scan_fwd(s3, a1, a2, sb_ref[0], T, TP)

        for jp in range(HALF):
            blk = _chan_block(2 * jp)
            _unpair(g3, jp, jnp.dot(dy_s[:, blk:blk + LANES], wc_ref[jp], preferred_element_type=F32), T, TP)
        a2c = -a2

        def rblk(ii, carry):
            g, acc1, acc2 = carry
            t0 = pl.multiple_of((T // SUBLANES - 1 - ii) * SUBLANES, SUBLANES)
            gv = g3.at[pl.ds(t0, span)]
            sv = s3.at[pl.ds(t0, span)]
            for k in reversed(range(SUBLANES)):
                rows = pl.ds(SUBLANES + k, N_GROUPS, stride=TP)
                g = a1 * g + a2c * _swap(g) + gv[rows, :]
                gv[rows, :] = g
                sp = sv[pl.ds(SUBLANES - 1 + k, N_GROUPS, stride=TP), :]
                acc1 = acc1 + g * sp
                acc2 = acc2 + g * _swap(sp)
            return g, acc1, acc2

        zero = jnp.zeros((N_GROUPS, LANES), F32)
        g, acc1, acc2 = lax.fori_loop(0, T // SUBLANES, rblk, (gst[...], zero, zero))
        gst[...] = g
        da1_ref[...] += acc1
        da2_ref[...] += acc2

        for jb in range(D // LANES):
            cols = slice(jb * LANES, (jb + 1) * LANES)
            acc = jnp.zeros((T, LANES), F32)
            for jp in _pairs_of(jb):
                gp = _pair(g3, jp, T, TP)
                dwc_ref[jp] += lax.dot_general(dy_s[:, cols], _pair(s3, jp, T, TP), TN, preferred_element_type=F32)
                dwb_ref[jp] += lax.dot_general(u_ref[:, cols], gp, TN, preferred_element_type=F32)
                acc += lax.dot_general(gp, wb_ref[jp], NT, preferred_element_type=F32)
            du_ref[:, cols] += acc

        du = du_ref[...]
        x = h_ref[...]
        rr = lax.rsqrt(jnp.mean(x * x, axis=1, keepdims=True) + EPS)
        xn = x * rr
        gdy = du * g_ref[...]
        dg_ref[...] += jnp.sum(du * xn, axis=0, keepdims=True)
        du_ref[...] = dres_ref[...] + rr * (gdy - xn * jnp.mean(gdy * xn, axis=1, keepdims=True))

    rrow = pl.BlockSpec((T, D), lambda c: (nC - 1 - c, 0))
    aspec = pl.BlockSpec((N_GROUPS, LANES), lambda c: (0, 0))
    vec = pl.BlockSpec((1, D), lambda c: (0, 0))
    return _pcall(
        body, (hn, dyg, ypre, sbound, wb, wc, a1, a2, dvec, h, g, dres), grid=(nC,),
        in_specs=[rrow, rrow, rrow, pl.BlockSpec((1, N_GROUPS, LANES), lambda c: (nC - 1 - c, 0, 0)),
                  _WB_SPEC, _WC_SPEC, aspec, aspec, vec, rrow, vec, rrow],
        out_specs=[rrow, _WB_SPEC, _WC_SPEC, aspec, aspec, vec, vec],
        out_shape=[jax.ShapeDtypeStruct((L, D), F32),
                   jax.ShapeDtypeStruct((HALF, LANES, 2 * LANES), F32),
                   jax.ShapeDtypeStruct((HALF, LANES, 2 * LANES), F32),
                   jax.ShapeDtypeStruct((N_GROUPS, LANES), F32), jax.ShapeDtypeStruct((N_GROUPS, LANES), F32),
                   jax.ShapeDtypeStruct((1, D), F32), jax.ShapeDtypeStruct((1, D), F32)],
        scratch_shapes=[pltpu.VMEM((N_GROUPS * TP, LANES), F32), pltpu.VMEM((N_GROUPS * TP, LANES), F32),
                        pltpu.VMEM((N_GROUPS, LANES), F32), pltpu.VMEM((T, D), BF16)],
        sem=("arbitrary",), name=name, xfers=xfers)


def _s5_prep(lam_re, lam_im, log_dt, b_re, b_im, c_re, c_im):
    dt = jnp.exp(log_dt)[:, None]
    mag = jnp.exp(lam_re * dt)
    lb_re = mag * jnp.cos(lam_im * dt)
    lb_im = mag * jnp.sin(lam_im * dt)
    den = lam_re * lam_re + lam_im * lam_im
    nr = lb_re - 1.0
    fr = ((nr * lam_re + lb_im * lam_im) / den)[..., None]
    fi = ((lb_im * lam_re - nr * lam_im) / den)[..., None]
    bb_re = fr * b_re - fi * b_im
    bb_im = fr * b_im + fi * b_re
    pair = lambda a: a.reshape(HALF, 2 * SSM_STATE)
    a1 = jnp.concatenate([pair(lb_re), pair(lb_re)], axis=0)
    a2 = jnp.concatenate([-pair(lb_im), pair(lb_im)], axis=0)

    quads = N_GROUPS // 4
    rows, cols = 4 * SSM_GROUP, 4 * SSM_STATE
    diag = (jnp.arange(rows)[:, None] // SSM_GROUP == jnp.arange(cols)[None, :] // SSM_STATE).astype(F32)
    place = jnp.eye(2, dtype=F32)

    def expand(w):
        blocks = jnp.tile(w.reshape(quads, rows, SSM_STATE), (1, 1, 4)) * diag
        return jnp.einsum('kq,gkrl->gkqrl', place, blocks.reshape(quads // 2, 2, rows, cols)
                          ).reshape(quads, LANES, cols)

    wb = jnp.concatenate([expand(bb_re.transpose(0, 2, 1)), expand(bb_im.transpose(0, 2, 1))], axis=0)
    wc = jnp.concatenate([expand(c_re), expand(-c_im)], axis=0)
    return a1, a2, wb, wc


def _tri(n, upper):
    r = lax.broadcasted_iota(jnp.int32, (n, n), 0)
    c = lax.broadcasted_iota(jnp.int32, (n, n), 1)
    return ((r <= c) if upper else (r >= c)).astype(F32)


def _fgate_fwd(fl, bf, *, name):
    L, W = fl.shape
    tr = _tile(L, CUM_TILE, SUBLANES)

    def body(f_ref, b_ref, o_ref, carry):
        @pl.when(pl.program_id(0) == 0)
        def _():
            carry[...] = jnp.zeros_like(carry)

        x = f_ref[...] + b_ref[...]
        ls = jnp.minimum(x, 0.0) - jnp.log(1.0 + jnp.exp(-jnp.abs(x)))
        cum = jnp.dot(_tri(tr, False), ls, preferred_element_type=F32, precision=lax.Precision.HIGHEST) + carry[...]
        o_ref[...] = cum
        carry[...] = cum[tr - 1:tr, :]

    return pl.pallas_call(
        body, grid=(L // tr,),
        in_specs=[pl.BlockSpec((tr, W), lambda i: (i, 0)), pl.BlockSpec((1, W), lambda i: (0, 0))],
        out_specs=pl.BlockSpec((tr, W), lambda i: (i, 0)), out_shape=jax.ShapeDtypeStruct((L, W), F32),
        scratch_shapes=[pltpu.VMEM((1, W), F32)], compiler_params=_params("arbitrary"), name=name)(fl, bf)


def _fgate_bwd(fl, bf, dcum, *, name):
    L, W = fl.shape
    tr = _tile(L, CUM_TILE, SUBLANES)
    n = L // tr

    def body(f_ref, b_ref, d_ref, o_ref, db_ref, carry):
        @pl.when(pl.program_id(0) == 0)
        def _():
            carry[...] = jnp.zeros_like(carry)
            db_ref[...] = jnp.zeros_like(db_ref)

        d = d_ref[...]
        rev = jnp.dot(_tri(tr, True), d, preferred_element_type=F32, precision=lax.Precision.HIGHEST) + carry[...]
        carry[...] += jnp.sum(d, axis=0, keepdims=True)
        df = rev * jax.nn.sigmoid(-(f_ref[...] + b_ref[...]))
        o_ref[...] = df
        db_ref[...] += jnp.sum(df, axis=0, keepdims=True)

    rrow = pl.BlockSpec((tr, W), lambda i: (n - 1 - i, 0))
    vec = pl.BlockSpec((1, W), lambda i: (0, 0))
    return pl.pallas_call(
        body, grid=(n,), in_specs=[rrow, vec, rrow], out_specs=[rrow, vec],
        out_shape=[jax.ShapeDtypeStruct((L, W), F32), jax.ShapeDtypeStruct((1, W), F32)],
        scratch_shapes=[pltpu.VMEM((1, W), F32)], compiler_params=_params("arbitrary"), name=name)(fl, bf, dcum)


_NT = (((1,), (1,)), ((), ()))
_TN = (((0,), (0,)), ((), ()))
HEAD_PAIRS = N_HEADS // 2


def _causal_tiles(n, by_row):
    pairs = ([(i, j) for i in range(n) for j in range(i + 1)] if by_row
             else [(i, j) for j in range(n) for i in range(j, n)])
    return (jnp.array([p[0] for p in pairs], jnp.int32), jnp.array([p[1] for p in pairs], jnp.int32))


def _as_row(col):
    return jnp.transpose(jnp.broadcast_to(col, (col.shape[0], LANES)))[0:1, :]


def _attn_logits(qs, k, ck, masked, t):
    s = lax.dot_general(qs, k, _NT, preferred_element_type=F32) - ck
    if masked:
        r = lax.broadcasted_iota(jnp.int32, (t, t), 0)
        c = lax.broadcasted_iota(jnp.int32, (t, t), 1)
        s = jnp.where(c > r, NEG, s)
    return s


def _attn_fwd(q, kv, ck, *, name, xfers=()):
    L, D = q.shape
    t = _tile(L, ATTN_TILE)
    n = L // t
    dh = HEAD_DIM

    def body(q_ref, k_ref, v_ref, ck_ref, o_ref, o32_ref, lse_ref, m_s, l_s, acc):
        i, j = pl.program_id(1), pl.program_id(2)

        @pl.when(j == 0)
        def _():
            m_s[...] = jnp.full_like(m_s, NEG)
            l_s[...] = jnp.zeros_like(l_s)
            acc[...] = jnp.zeros_like(acc)

        def tile(masked):
            for e in range(2):
                sl = slice(e * dh, (e + 1) * dh)
                v = v_ref[:, sl]
                s = _attn_logits(q_ref[:, sl] * ATTN_SCALE, k_ref[:, sl], ck_ref[e], masked, t)
                m_new = jnp.maximum(m_s[e], jnp.max(s, axis=1, keepdims=True))
                alpha = jnp.exp(m_s[e] - m_new)
                p = jnp.exp(s - m_new)
                l_s[e] = alpha * l_s[e] + jnp.sum(p, axis=1, keepdims=True)
                p_hi = p.astype(BF16)
                p_lo = (p - p_hi.astype(F32)).astype(BF16)
                pv = (jnp.dot(p_hi, v, preferred_element_type=F32) + jnp.dot(p_lo, v, preferred_element_type=F32))
                acc[e] = alpha * acc[e] + pv
                m_s[e] = m_new

        pl.when(j < i)(functools.partial(tile, False))
        pl.when(j == i)(functools.partial(tile, True))

        @pl.when(j == n - 1)
        def _():
            for e in range(2):
                sl = slice(e * dh, (e + 1) * dh)
                o = acc[e] / l_s[e]
                o_ref[:, sl] = o.astype(BF16)
                o32_ref[:, sl] = o
                lse_ref[e] = _as_row(m_s[e] + jnp.log(l_s[e]))

    qs = pl.BlockSpec((t, LANES), lambda h, i, j: (i, h))
    ks = pl.BlockSpec((t, LANES), lambda h, i, j: (jnp.minimum(i, j), h))
    vs = pl.BlockSpec((t, LANES), lambda h, i, j: (jnp.minimum(i, j), HEAD_PAIRS + h))
    cs = pl.BlockSpec((2, 1, t), lambda h, i, j: (h, 0, jnp.minimum(i, j)))
    return _pcall(
        body, (q, kv, kv, ck), grid=(HEAD_PAIRS, n, n), in_specs=[qs, ks, vs, cs],
        out_specs=[qs, qs, pl.BlockSpec((2, 1, t), lambda h, i, j: (h, 0, i))],
        out_shape=[jax.ShapeDtypeStruct((L, D), BF16), jax.ShapeDtypeStruct((L, D), F32),
                   jax.ShapeDtypeStruct((N_HEADS, 1, L), F32)],
        scratch_shapes=[pltpu.VMEM((2, t, 1), F32), pltpu.VMEM((2, t, 1), F32), pltpu.VMEM((2, t, dh), F32)],
        sem=("parallel", "parallel", "arbitrary"), name=name, xfers=xfers)


def _attn_delta(do, o, *, name):
    L, D = do.shape
    t = _tile(L, ATTN_TILE)

    def body(do_ref, o_ref, d_ref):
        prod = do_ref[...].astype(F32) * o_ref[...]
        head = lax.broadcasted_iota(jnp.int32, (N_HEADS, D), 0)
        col = lax.broadcasted_iota(jnp.int32, (N_HEADS, D), 1)
        sel = (col // HEAD_DIM == head).astype(F32)
        d_ref[:, 0, :] = lax.dot_general(sel, prod, _NT, preferred_element_type=F32, precision=lax.Precision.HIGHEST)

    row = pl.BlockSpec((t, D), lambda i: (i, 0))
    return pl.pallas_call(
        body, grid=(L // t,), in_specs=[row, row], out_specs=pl.BlockSpec((N_HEADS, 1, t), lambda i: (0, 0, i)),
        out_shape=jax.ShapeDtypeStruct((N_HEADS, 1, L), F32), compiler_params=_params("parallel"), name=name)(do, o)


def _attn_bwd(q, kv, ck_col, do, lse_row, delta_row, *, name, xfers=()):
    L, D = q.shape
    t = _tile(L, ATTN_TILE)
    n = L // t
    dh = HEAD_DIM

    def body(i_tab, j_tab, q_ref, k_ref, v_ref, ck_ref, do_ref, lse_ref, dl_ref, dq_ref, dk_ref, dv_ref, dck_ref,
             dck_s):
        i, j = i_tab[pl.program_id(1)], j_tab[pl.program_id(1)]

        @pl.when(pl.program_id(1) == 0)
        def _():
            dq_ref[...] = jnp.zeros_like(dq_ref)

        @pl.when(i == j)
        def _():
            dk_ref[...] = jnp.zeros_like(dk_ref)
            dv_ref[...] = jnp.zeros_like(dv_ref)
            dck_s[...] = jnp.zeros_like(dck_s)

        def tile(masked):
            cols = pl.ds(pl.multiple_of(i * t, t), t)
            for e in range(2):
                sl = slice(e * dh, (e + 1) * dh)
                qs = q_ref[:, sl] * ATTN_SCALE
                k = k_ref[:, sl]
                do = do_ref[:, sl]
                st = lax.dot_general(k, qs, _NT, preferred_element_type=F32) - ck_ref[e]
                if masked:
                    kpos = lax.broadcasted_iota(jnp.int32, (t, t), 0)
                    qpos = lax.broadcasted_iota(jnp.int32, (t, t), 1)
                    st = jnp.where(kpos > qpos, NEG, st)
                pt = jnp.exp(st - lse_ref[e])
                dpt = lax.dot_general(v_ref[:, sl], do, _NT, preferred_element_type=F32)
                dst = pt * (dpt - dl_ref[e])
                dst16 = dst.astype(BF16)
                dv_ref[:, sl] += jnp.dot(pt.astype(BF16), do, preferred_element_type=F32)
                dk_ref[:, sl] += jnp.dot(dst16, qs, preferred_element_type=F32)
                dck_s[e] -= jnp.sum(dst, axis=1, keepdims=True)
                dq_ref[sl, cols] += lax.dot_general(k, dst16, _TN, preferred_element_type=F32) * ATTN_SCALE

        pl.when(i > j)(functools.partial(tile, False))
        pl.when(i == j)(functools.partial(tile, True))

        @pl.when(i == n - 1)
        def _():
            for e in range(2):
                dck_ref[e] = _as_row(dck_s[e])

    qs = pl.BlockSpec((t, LANES), lambda h, s, it, jt: (it[s], h))
    ks = pl.BlockSpec((t, LANES), lambda h, s, it, jt: (jt[s], h))
    vs = pl.BlockSpec((t, LANES), lambda h, s, it, jt: (jt[s], HEAD_PAIRS + h))
    cs = pl.BlockSpec((2, t, 1), lambda h, s, it, jt: (h, jt[s], 0))
    ls = pl.BlockSpec((2, 1, t), lambda h, s, it, jt: (h, 0, it[s]))
    full = jax.ShapeDtypeStruct((L, D), F32)
    tabs = _causal_tiles(n, by_row=False)
    return _pcall(
        body, (q, kv, kv, ck_col, do, lse_row, delta_row), grid=(HEAD_PAIRS, tabs[0].shape[0]),
        in_specs=[qs, ks, vs, cs, qs, ls, ls],
        out_specs=[pl.BlockSpec((LANES, L), lambda h, s, it, jt: (h, 0)), ks, ks,
                   pl.BlockSpec((2, 1, t), lambda h, s, it, jt: (h, 0, jt[s]))],
        out_shape=[jax.ShapeDtypeStruct((D, L), F32), full, full, jax.ShapeDtypeStruct((N_HEADS, 1, L), F32)],
        scratch_shapes=[pltpu.VMEM((2, t, 1), F32)],
        sem=("parallel", "arbitrary"), name=name, xfers=xfers, prefetch=tabs)


SHARD_COLS_FFN = 2 * D_FF // N_DEV
SHARD_ROWS_FFN = D_FF // N_DEV
SHARD_COLS_GLU = 2 * D_MODEL // N_DEV
SHARD_ROWS_QO = D_MODEL // N_DEV
SHARD_COLS_KVF = (2 * D_MODEL + N_HEADS) // N_DEV
S5_NAMES = ("lam_re", "lam_im", "log_dt", "ssm_b_re", "ssm_b_im", "ssm_c_re", "ssm_c_im")
REPL_LATE_ROWS = 272
REPL_EARLY_ROWS = 304


def _leaves(parts):
    out = []
    for p in parts:
        out.extend(_leaves(p) if isinstance(p, (list, tuple)) else [p.reshape(-1)])
    return out


def _whole_rows(n):
    return -(-n // D_MODEL)


def _pack_rows(parts, rows):
    blocks = [jnp.pad(p, (0, _whole_rows(p.shape[0]) * D_MODEL - p.shape[0])).reshape(-1, D_MODEL)
              for p in _leaves(parts)]
    used = sum(b.shape[0] for b in blocks)
    return jnp.concatenate(blocks + [jnp.zeros((rows - used, D_MODEL), blocks[0].dtype)], axis=0)


def _unpack_rows(packed, like):
    out, row = [], 0
    for p in _leaves(like):
        n = _whole_rows(p.shape[0])
        out.append(packed[row:row + n].reshape(-1)[:p.shape[0]])
        row += n
    return out


def _repl_late(d):
    return [d["g_mix"][0], [d[n][0] for n in S5_NAMES]]


def _repl_early(d):
    return [list(d["g_mix"][1:]), list(d["g_ffn"]), [d[n][1] for n in S5_NAMES], d["g_kv"], d["b_f"],
            list(d["ffn_conv_b"]), d["g_final"]]


def _kvf_blocks(full):
    return full.reshape(D_MODEL, N_DEV, SHARD_COLS_KVF).transpose(1, 0, 2)


class _Step:
    def __init__(self, weights, send=None, plan=None):
        self.w = dict(weights)
        self.send = send or {}
        self.grad = {}
        self.bcast = {}
        self.slots = {}
        self.plan = plan or {}

    def xfers(self, host):
        src = {"w": self.send, "g": self.grad, "b": self.bcast}
        out = []
        for kind, k in self.plan.get(host, ()):
            if kind == "w" and k[0] == "w_ffn_in" and len(k) == 3:
                parts = _W_IN_PARTS[k[1]]
                out.append((self.send[k], False, (sum(parts), sum(parts[:k[2]]), self.w.get(("w_ffn_in_parts", k[1])))))
            else:
                out.append((src[kind][k], kind == "g"))
        return out

    def land(self, host, gathered):
        for (kind, k), g in zip(self.plan.get(host, ()), gathered):
            if kind == "w":
                self.arrive(k, g)
            else:
                self.slots[k] = g

    def arrive(self, k, g):
        name = k[0]
        if name == "w_ffn_in" and len(k) == 3:
            self.w["w_ffn_in_parts", k[1]] = g
            if k[2] == len(_W_IN_PARTS[k[1]]) - 1:
                self.w[name, k[1]] = g
        elif name == "w_ffn_out":
            self.w[k] = g.reshape(4, 2 * SHARD_ROWS_FFN, D_MODEL)
        elif name in ("w_q", "w_o"):
            self.w[k] = g.reshape(D_MODEL, D_MODEL)
        elif name == "w_kvf":
            full = g.transpose(1, 0, 2).reshape(D_MODEL, N_DEV * SHARD_COLS_KVF)
            self.w["w_kv",] = full[:, :2 * D_MODEL]
            self.w["w_f",] = jnp.pad(full[:, 2 * D_MODEL:], ((0, 0), (0, LANES - N_HEADS)))
        elif name == "small":
            flat = g.reshape(N_DEV, -1)
            self.w["ssm_d",] = flat[:, :256].reshape(N_DEV, N_A, LANES).transpose(1, 0, 2).reshape(N_A, D_MODEL)
            cw = flat[:, 256:256 + DEPTH * 3 * SHARD_COLS_FFN].reshape(N_DEV, DEPTH, 3, SHARD_COLS_FFN)
            for layer in range(DEPTH):
                self.w["conv_w", layer] = cw[:, layer].reshape(2, 4, 3, SHARD_COLS_FFN)
        else:
            self.w[k] = g

    def run(self, host, fn, *args, **kw):
        xf = self.xfers(host)
        res = fn(*args, name=host, xfers=xf, **kw)
        if not xf:
            return res[0] if isinstance(res, (list, tuple)) and len(res) == 1 else res
        n_own = len(res) - len(xf)
        self.land(host, res[n_own:])
        return res[0] if n_own == 1 else res[:n_own]


def _step(x, target, S):
    L = x.shape[0]
    W = S.w
    vec = lambda a: a.reshape(1, -1)
    CF = SHARD_COLS_FFN

    h = x
    saved = []
    kvs = None
    hn_next = None
    for layer in range(DEPTH):
        t = str(layer)
        if layer < N_A:
            (a1, a2, wb, wc), prep_vjp = jax.vjp(_s5_prep, *[W[n][layer] for n in S5_NAMES])
            wb16, wc16 = wb.astype(BF16), wc.astype(BF16)
            hn = hn_next if hn_next is not None else _rms_fwd(h, vec(W["g_mix"][layer]), name="mix_norm" + t)
            dvec = vec(W["ssm_d",][layer])
            ypre, yg, sb = S.run("s5_fwd" + t, _s5_fwd, hn, wb16, wc16, a1, a2, dvec)
            z = S.run("glu_mm" + t, _mm, yg, W["w_glu", layer], bk="bkn", ok="bmn", tm=4096)
            z = z.reshape(2, 4, L, SHARD_COLS_GLU)
            h1, hn2 = _glu_res_rms(z, h, vec(W["g_ffn"][layer]), name="glu_res" + t)
            mix_saved = (h, hn, ypre, yg, sb, z, a1, a2, wb16, wc16, dvec, prep_vjp)
        else:
            j = layer - N_A
            if layer == N_A:
                hkv = _rms_fwd(h, vec(W["g_kv"]), name="kv_norm")
                kvm = S.run("kv_mm", _mm, hkv, W["w_kv",], out_dtype=BF16)
                fl = S.run("f_mm", _mm, hkv, W["w_f",])
                cum = _fgate_fwd(fl, W["b_f_pad",], name="fgate_fwd")
                ck = cum[:, :N_HEADS].T.reshape(N_HEADS, 1, L)
                kvs = (h, hkv, fl, kvm, ck)
            _, _, _, kvm, ck = kvs
            hn = hn_next if hn_next is not None else _rms_fwd(h, vec(W["g_mix"][layer]), name="mix_norm" + t)
            q = S.run("q_mm" + t, _mm, hn, W["w_q", j], out_dtype=BF16)
            o, o32, lse = S.run("attn_fwd" + t, _attn_fwd, q, kvm, ck)
            h1, hn2 = S.run("o_mm" + t, _mm, o, W["w_o", j], add=h, norm_g=vec(W["g_ffn"][layer]))
            mix_saved = (h, hn, q, o32, o, lse)
        u0 = S.run("ffn_in" + t, _mm, hn2, W["w_ffn_in", layer], bk="nbk", ok="bmn", tm=2048).reshape(2, 4, L, CF)
        a = S.run("ffn_act" + t, _conv_act, u0, W["conv_w", layer], W["conv_b", layer])
        if layer + 1 < DEPTH:
            h2, hn_next = S.run("ffn_out" + t, _mm, a, W["w_ffn_out", layer], ak="bmk", bk="kbn", add=h1, kg=4,
                                norm_g=vec(W["g_mix"][layer + 1]))
        else:
            h2 = S.run("ffn_out" + t, _mm, a, W["w_ffn_out", layer], ak="bmk", bk="kbn", add=h1, tn=512, kg=4)
        saved.append((mix_saved, h1, hn2, u0))
        h = h2

    loss, dh, dg_final = _loss_head(h, vec(W["g_final"]), target, name="loss_head")
    g = {"g_final": dg_final.reshape(-1)}
    gl = {k: [None] * DEPTH for k in ("g_mix", "g_ffn", "conv_w", "ffn_conv_b")}
    ga = {k: [None] * N_A for k in S5_NAMES + ("ssm_d",)}
    dk = dv = dck = None
    for layer in reversed(range(DEPTH)):
        t = str(layer)
        mix_saved, h1, hn2, u0 = saved[layer]
        cw, cb = W["conv_w", layer], W["conv_b", layer]
        da = S.run("ffn_da" + t, _mm, dh, W["w_ffn_out", layer], bk="nbk", ok="bmn", tm=2048)
        a, du0, dcw, dcb = S.run("ffn_conv_bwd" + t, _conv_ffn_bwd, u0, cw, cb, da)
        dw_out = S.run("ffn_dwout" + t, _mm, a, dh, ak="bkm", ok="mbn", out_dtype=BF16)
        S.grad["w_ffn_out", layer] = dw_out.reshape(N_DEV, SHARD_ROWS_FFN, D_MODEL)
        du0 = du0.reshape(N_DEV, L, CF)
        S.grad["w_ffn_in", layer] = S.run("ffn_dwin" + t, _mm, du0, hn2, ak="bkm", ok="mbn", out_dtype=BF16)
        dh1, dg = S.run("ffn_dhn" + t, _mm, du0, W["w_ffn_in", layer], ak="bmk", bk="kbn", kg=4, tm=512,
                        rms_bwd=(h1, vec(W["g_ffn"][layer]), dh))
        gl["g_ffn"][layer], gl["conv_w"][layer], gl["ffn_conv_b"][layer] = dg.reshape(-1), dcw, dcb.reshape(-1)
        if layer < N_A:
            hin, hn, ypre, yg, sb, z, a1, a2, wb16, wc16, dvec, prep_vjp = mix_saved
            dz = _glu_bwd(z, dh1, name="glu_bwd" + t).reshape(N_DEV, L, SHARD_COLS_GLU)
            S.grad["w_glu", layer] = S.run("glu_dw" + t, _mm, yg, dz, ak="km", bk="bkn", ok="bmn", out_dtype=BF16,
                                           tk=4096)
            dyg = S.run("glu_dy" + t, _mm, dz, W["w_glu", layer], ak="bmk", bk="bnk", out_dtype=BF16, tm=2048, kg=8)
            if layer == 0:
                S.bcast["repl_early",] = _pack_rows(_repl_early({**g, **gl, **ga}), REPL_EARLY_ROWS).astype(BF16)
            dh, dwb, dwc, da1, da2, dd, dg = S.run("s5_bwd" + t, _s5_bwd, hn, dyg, ypre, sb, wb16, wc16, a1, a2, dvec,
                                                   hin, vec(W["g_mix"][layer]), dh1)
            for nme, val in zip(S5_NAMES, prep_vjp((da1, da2, dwb, dwc))):
                ga[nme][layer] = val
            ga["ssm_d"][layer] = dd.reshape(-1)
        else:
            j = layer - N_A
            hin, hn, q, o32, o, lse = mix_saved
            _, _, _, kvm, ck = kvs
            S.grad["w_o", j] = S.run("o_dw" + t, _mm, o, dh1, ak="km", out_dtype=BF16
                                     ).reshape(N_DEV, SHARD_ROWS_QO, D_MODEL)
            do = S.run("o_dx" + t, _mm, dh1, W["w_o", j], bk="nk", out_dtype=BF16)
            delta = _attn_delta(do, o32, name="attn_delta" + t)
            dq_t, dk_l, dv_l, dck_l = S.run("attn_bwd" + t, _attn_bwd, q, kvm, ck.reshape(N_HEADS, L, 1), do,
                                            lse, delta)
            dk = dk_l if dk is None else dk + dk_l
            dv = dv_l if dv is None else dv + dv_l
            dck = dck_l if dck is None else dck + dck_l
            S.grad["w_q", j] = S.run("q_dw" + t, _mm, hn, dq_t, ak="km", bk="nk", out_dtype=BF16
                                     ).reshape(N_DEV, SHARD_ROWS_QO, D_MODEL)
            dh, dg = S.run("q_dx" + t, _mm, dq_t, W["w_q", j], ak="km", bk="nk",
                           rms_bwd=(hin, vec(W["g_mix"][layer]), dh1))
            if layer == N_A:
                hkv_in, hkv, fl, _, _ = kvs
                dcum = jnp.pad(dck.reshape(N_HEADS, L).T, ((0, 0), (0, LANES - N_HEADS)))
                dfl, dbf = _fgate_bwd(fl, W["b_f_pad",], dcum, name="fgate_bwd")
                dkv = jnp.concatenate([dk, dv], axis=1).astype(BF16)
                dfl16 = dfl.astype(BF16)
                dw_kv = S.run("kv_dw", _mm, hkv, dkv, ak="km")
                dw_f = S.run("f_dw", _mm, hkv, dfl16, ak="km")
                S.grad["w_kvf",] = _kvf_blocks(jnp.concatenate([dw_kv, dw_f[:, :N_HEADS]], axis=1)).astype(BF16)
                dhkv = S.run("kv_dx", _mm, dkv, W["w_kv",], bk="nk")
                dh, dgkv = S.run("f_dx", _mm, dfl16, W["w_f",], bk="nk", add=dhkv,
                                 rms_bwd=(hkv_in, vec(W["g_kv"]), dh))
                g["b_f"] = dbf[0, :N_HEADS]
                g["g_kv"] = dgkv.reshape(-1)
        gl["g_mix"][layer] = dg.reshape(-1)

    for d in (gl, ga):
        for k, v in d.items():
            g[k] = jnp.stack(v)
    return loss, dh, g


def _adamw_layers(slots, w, m, v, *, name):
    shape = w.shape
    nl = len(slots)
    w, m, v = (a.reshape((nl,) + a.shape[-2:]) for a in (w, m, v))
    _, R, C = w.shape
    tr = _tile(R, 256, 16)
    c1 = 1.0 / (1.0 - ADAM_B1 ** ADAM_STEP)
    c2 = 1.0 / (1.0 - ADAM_B2 ** ADAM_STEP)

    def body(*refs):
        s_refs, (w_ref, m_ref, v_ref), (g_ref, d_ref, nm_ref, nv_ref) = refs[:nl], refs[nl:nl + 3], refs[nl + 3:]
        for layer in range(nl):
            @pl.when(pl.program_id(0) == layer)
            def _(s_ref=s_refs[layer]):
                g = s_ref[0].astype(F32)
                for d in range(1, N_DEV):
                    g = g + s_ref[d].astype(F32)
                m2 = ADAM_B1 * m_ref[...] + (1.0 - ADAM_B1) * g
                v2 = ADAM_B2 * v_ref[...] + (1.0 - ADAM_B2) * (g * g)
                g_ref[...] = g
                nm_ref[...] = m2
                nv_ref[...] = v2
                d_ref[...] = -ADAM_LR * ((m2 * c1) / (jnp.sqrt(v2 * c2) + ADAM_EPS) + ADAM_WD * w_ref[...])

    def slab_spec(layer):
        return pl.BlockSpec((N_DEV, tr, C), lambda l, i: (0, jnp.where(l == layer, i, 0), 0))

    row = pl.BlockSpec((None, tr, C), lambda l, i: (l, i, 0))
    out = jax.ShapeDtypeStruct((nl, R, C), F32)
    outs = pl.pallas_call(
        body, grid=(nl, R // tr), in_specs=[slab_spec(layer) for layer in range(nl)] + [row, row, row],
        out_specs=[row, row, row, row], out_shape=[out, out, out, out],
        compiler_params=_params("arbitrary", "arbitrary"), name=name)(*slots, w, m, v)
    return [o.reshape(shape) for o in outs]


_SMALL_ROWS = 72
_ORDER = ("g_mix", "g_ffn", "lam_re", "lam_im", "log_dt", "ssm_b_re", "ssm_b_im", "ssm_c_re", "ssm_c_im", "ssm_d",
          "w_glu", "g_kv", "w_kvf", "b_f", "w_q", "w_o", "w_ffn_in", "ffn_conv_w", "ffn_conv_b", "w_ffn_out", "g_final")


def _pack_small(ssm_d, conv_w):
    flat = jnp.concatenate([ssm_d.reshape(-1), conv_w.reshape(-1)])
    return jnp.pad(flat, (0, _SMALL_ROWS * LANES - flat.shape[0])).reshape(_SMALL_ROWS, LANES)


def _unpack_small(flat):
    flat = flat.reshape(-1)
    return flat[:256].reshape(2, 128), flat[256:256 + 8448].reshape(4, 3, 704)


_W_IN_PARTS = {0: (352, 352), 1: (176, 528)}
_FWD_PLAN = {
    "start": [("small",)],
    "s5_fwd0": [("w_glu", 0), ("w_ffn_in", 0, 0)],
    "glu_mm0": [("w_ffn_in", 0, 1)],
    "ffn_in0": [("w_ffn_out", 0)],
    "ffn_act0": [("w_glu", 1)],
    "ffn_out0": [("w_ffn_in", 1, 0)],
    "s5_fwd1": [("w_ffn_in", 1, 1)],
    "ffn_in1": [("w_ffn_out", 1)],
    "ffn_act1": [("w_kvf",), ("w_o", 0)],
    "ffn_out1": [("w_q", 0)],
    "attn_fwd2": [("w_ffn_in", 2), ("w_ffn_out", 2), ("w_q", 1), ("w_o", 1)],
    "attn_fwd3": [("w_ffn_in", 3), ("w_ffn_out", 3)],
}
_BWD_PLAN = {
    "ffn_dhn3": [("w_ffn_out", 3)],
    "attn_bwd3": [("w_ffn_in", 3), ("w_o", 1)],
    "ffn_conv_bwd2": [("w_q", 1)],
    "ffn_dhn2": [("w_ffn_out", 2)],
    "attn_bwd2": [("w_ffn_in", 2), ("w_o", 0)],
    "ffn_conv_bwd1": [("w_q", 0), ("w_kvf",)],
    "ffn_dhn1": [("w_ffn_out", 1)],
    "s5_bwd1": [("w_ffn_in", 1), ("w_glu", 1)],
    "ffn_dhn0": [("w_ffn_out", 0)],
    "s5_bwd0": [("w_ffn_in", 0), ("w_glu", 0), ("repl_early",)],
    "end": [("small",), ("repl_late",)],
}
_PLAN = {h: [("w", k) for k in ks] for h, ks in _FWD_PLAN.items()}
_PLAN.update({h: [("b" if k[0].startswith("repl") else "g", k) for k in ks] for h, ks in _BWD_PLAN.items()})


def kernel(x, g_mix, g_ffn, lam_re, lam_im, log_dt, ssm_b_re, ssm_b_im, ssm_c_re, ssm_c_im, ssm_d, w_glu, g_kv, w_kvf, b_f, w_q, w_o, w_ffn_in, ffn_conv_w, ffn_conv_b, w_ffn_out, g_final, loss_target, m_g_mix, m_g_ffn, m_lam_re, m_lam_im, m_log_dt, m_ssm_b_re, m_ssm_b_im, m_ssm_c_re, m_ssm_c_im, m_ssm_d, m_w_glu, m_g_kv, m_w_kvf, m_b_f, m_w_q, m_w_o, m_w_ffn_in, m_ffn_conv_w, m_ffn_conv_b, m_w_ffn_out, m_g_final, v_g_mix, v_g_ffn, v_lam_re, v_lam_im, v_log_dt, v_ssm_b_re, v_ssm_b_im, v_ssm_c_re, v_ssm_c_im, v_ssm_d, v_w_glu, v_g_kv, v_w_kvf, v_b_f, v_w_q, v_w_o, v_w_ffn_in, v_ffn_conv_w, v_ffn_conv_b, v_w_ffn_out, v_g_final):
    wts = dict(g_mix=g_mix, g_ffn=g_ffn, lam_re=lam_re, lam_im=lam_im, log_dt=log_dt, ssm_b_re=ssm_b_re,
               ssm_b_im=ssm_b_im, ssm_c_re=ssm_c_re, ssm_c_im=ssm_c_im, ssm_d=ssm_d, w_glu=w_glu, g_kv=g_kv,
               w_kvf=w_kvf, b_f=b_f, w_q=w_q, w_o=w_o, w_ffn_in=w_ffn_in, ffn_conv_w=ffn_conv_w,
               ffn_conv_b=ffn_conv_b, w_ffn_out=w_ffn_out, g_final=g_final)
    mom = dict(g_mix=m_g_mix, g_ffn=m_g_ffn, lam_re=m_lam_re, lam_im=m_lam_im, log_dt=m_log_dt, ssm_b_re=m_ssm_b_re,
               ssm_b_im=m_ssm_b_im, ssm_c_re=m_ssm_c_re, ssm_c_im=m_ssm_c_im, ssm_d=m_ssm_d, w_glu=m_w_glu,
               g_kv=m_g_kv, w_kvf=m_w_kvf, b_f=m_b_f, w_q=m_w_q, w_o=m_w_o, w_ffn_in=m_w_ffn_in,
               ffn_conv_w=m_ffn_conv_w, ffn_conv_b=m_ffn_conv_b, w_ffn_out=m_w_ffn_out, g_final=m_g_final)
    var = dict(g_mix=v_g_mix, g_ffn=v_g_ffn, lam_re=v_lam_re, lam_im=v_lam_im, log_dt=v_log_dt, ssm_b_re=v_ssm_b_re,
               ssm_b_im=v_ssm_b_im, ssm_c_re=v_ssm_c_re, ssm_c_im=v_ssm_c_im, ssm_d=v_ssm_d, w_glu=v_w_glu,
               g_kv=v_g_kv, w_kvf=v_w_kvf, b_f=v_b_f, w_q=v_w_q, w_o=v_w_o, w_ffn_in=v_w_ffn_in,
               ffn_conv_w=v_ffn_conv_w, ffn_conv_b=v_ffn_conv_b, w_ffn_out=v_w_ffn_out, g_final=v_g_final)
    kinds = ("grad", "delta", "m", "v")

    ready = {n: wts[n] for n in ("g_mix", "g_ffn", "g_kv", "g_final") + S5_NAMES}
    ready["b_f_pad",] = jnp.pad(b_f, (0, LANES - N_HEADS)).reshape(1, LANES)
    send = {("small",): _pack_small(ssm_d, ffn_conv_w), ("w_kvf",): w_kvf.astype(BF16)}
    for layer in range(DEPTH):
        ready["conv_b", layer] = ffn_conv_b[layer].reshape(2, 4, 1, SHARD_COLS_FFN)
        w_in_t = jnp.swapaxes(w_ffn_in[layer], 0, 1).astype(BF16)
        if layer in _W_IN_PARTS:
            row = 0
            for p, rows in enumerate(_W_IN_PARTS[layer]):
                send["w_ffn_in", layer, p] = w_in_t[row:row + rows]
                row += rows
        else:
            send["w_ffn_in", layer] = w_in_t
        send["w_ffn_out", layer] = w_ffn_out[layer].astype(BF16)
    for layer in range(N_A):
        send["w_glu", layer] = w_glu[layer].astype(BF16)
        send["w_q", layer] = w_q[layer].astype(BF16)
        send["w_o", layer] = w_o[layer].astype(BF16)

    S = _Step(ready, send, _PLAN)
    S.land("start", _exchange(S.xfers("start"), name="start"))
    loss, dx, g = _step(x[0], loss_target[0], S)
    loss = lax.psum(loss[0, 0], MESH_AXES)

    g_d = g["ssm_d"].reshape(N_A, N_DEV, LANES).transpose(1, 0, 2).reshape(N_DEV, N_A * LANES)
    g_cw = jnp.stack([g["conv_w"][layer].reshape(N_DEV, 3, SHARD_COLS_FFN) for layer in range(DEPTH)], axis=1)
    g_small = jnp.concatenate([g_d, g_cw.reshape(N_DEV, -1)], axis=1)
    g_small = jnp.pad(g_small, ((0, 0), (0, _SMALL_ROWS * LANES - g_small.shape[1])))
    S.grad["small",] = g_small.reshape(N_DEV, _SMALL_ROWS, LANES)
    S.bcast["repl_late",] = _pack_rows(_repl_late(g), REPL_LATE_ROWS).astype(BF16)
    S.land("end", _exchange(S.xfers("end"), name="end"))

    res = {}
    for name, nl in (("w_glu", N_A), ("w_q", DEPTH - N_A), ("w_o", DEPTH - N_A), ("w_ffn_in", DEPTH),
                     ("w_ffn_out", DEPTH)):
        view = (lambda a: jnp.swapaxes(a, 1, 2)) if name == "w_ffn_in" else (lambda a: a)
        outs = _adamw_layers([S.slots[name, layer] for layer in range(nl)], view(wts[name]), view(mom[name]),
                             view(var[name]), name="adamw_" + name)
        res.update({(kind, name): view(a) for kind, a in zip(kinds, outs)})
    outs = _adamw_layers([S.slots["w_kvf",]], w_kvf, m_w_kvf, v_w_kvf, name="adamw_w_kvf")
    res.update({(kind, "w_kvf"): a for kind, a in zip(kinds, outs)})
    outs = _adamw_layers([S.slots["small",]], _pack_small(ssm_d, ffn_conv_w), _pack_small(m_ssm_d, m_ffn_conv_w),
                         _pack_small(v_ssm_d, v_ffn_conv_w), name="adamw_small")
    for kind, flat in zip(kinds, outs):
        res[kind, "ssm_d"], res[kind, "ffn_conv_w"] = _unpack_small(flat)

    pieces = {}
    for key, rows, sel in ((("repl_early",), REPL_EARLY_ROWS, _repl_early), (("repl_late",), REPL_LATE_ROWS, _repl_late)):
        outs = _adamw_layers([S.slots[key]], *[_pack_rows(sel(d), rows) for d in (wts, mom, var)],
                             name="adamw_" + key[0])
        for kind, flat in zip(kinds, outs):
            pieces[kind, key[0]] = _unpack_rows(flat, sel(wts))
    for kind in kinds:
        early, late = iter(pieces[kind, "repl_early"]), iter(pieces[kind, "repl_late"])
        take = lambda it, n: [next(it) for _ in range(n)]
        res[kind, "g_mix"] = jnp.stack(take(late, 1) + take(early, DEPTH - 1))
        res[kind, "g_ffn"] = jnp.stack(take(early, DEPTH))
        for n in S5_NAMES:
            res[kind, n] = jnp.stack([next(late), next(early)]).reshape(wts[n].shape)
        res[kind, "g_kv"], res[kind, "b_f"] = next(early), next(early)
        res[kind, "ffn_conv_b"] = jnp.stack(take(early, DEPTH))
        res[kind, "g_final"] = next(early)

    return (loss, dx[None], *[res[kind, n] for kind in kinds for n in _ORDER])
```

```python
import functools
import math

import jax
import jax.numpy as jnp
from jax import lax
from jax.experimental import pallas as pl
from jax.experimental.pallas import tpu as pltpu

F32 = jnp.float32
BF16 = jnp.bfloat16

D_MODEL = 1024
DEPTH = 4
N_A = 2
N_GROUPS = 64
SSM_GROUP = 16
SSM_STATE = 64
N_HEADS = 16
HEAD_DIM = 64
ATTN_SCALE = HEAD_DIM ** -0.5
D_FF = 2816
EPS = 1e-6
N_DEV = 8
LANES = 128
SUBLANES = 8

ADAM_LR = 0.001
ADAM_B1 = 0.9
ADAM_B2 = 0.999
ADAM_EPS = 1e-08
ADAM_WD = 0.01
ADAM_STEP = 10

ROW_TILE = 512
S5_CHUNK = 256
ATTN_TILE = 512
CUM_TILE = 256
NEG = -1e30

MESH_AXES = ("x", "y", "c")


def _tile(n, target, align=LANES):
    t = (min(target, n) // align) * align
    while t >= align:
        if n % t == 0:
            return t
        t -= align
    return n


def _params(*sem):
    return pltpu.CompilerParams(dimension_semantics=sem, vmem_limit_bytes=56 * 1024 * 1024)


_ANY = pl.BlockSpec(memory_space=pl.ANY)
_XFER_SEMS = (pltpu.SemaphoreType.DMA((N_DEV - 1,)), pltpu.SemaphoreType.DMA((N_DEV - 1,)), pltpu.SemaphoreType.DMA)


def _xfer_copies(x_ref, o_ref, send_sems, recv_sems, local_sem, scatter, row_off):
    xi, yi, ci = lax.axis_index("x"), lax.axis_index("y"), lax.axis_index("c")
    me = 4 * xi + 2 * yi + ci

    def src(p):
        return x_ref.at[p] if scatter else x_ref

    def dst(p):
        return o_ref.at[p] if row_off is None else o_ref.at[p, pl.ds(row_off, x_ref.shape[0])]

    own = pltpu.make_async_copy(src(me), dst(me), local_sem)
    sends, recvs = [], []
    for k in range(1, N_DEV):
        px, py, pc = xi ^ (k >> 2), yi ^ ((k >> 1) & 1), ci ^ (k & 1)
        p = 4 * px + 2 * py + pc
        sends.append(pltpu.make_async_remote_copy(
            src_ref=src(p), dst_ref=dst(me), send_sem=send_sems.at[k - 1], recv_sem=recv_sems.at[k - 1],
            device_id=(px, py, pc), device_id_type=pl.DeviceIdType.MESH))
        recvs.append(pltpu.make_async_remote_copy(
            src_ref=src(p), dst_ref=dst(p), send_sem=send_sems.at[k - 1], recv_sem=recv_sems.at[k - 1],
            device_id=(px, py, pc), device_id_type=pl.DeviceIdType.MESH))
    return own, sends, recvs


def _xfer_start(*refs, scatter, row_off):
    own, sends, _ = _xfer_copies(*refs, scatter, row_off)
    own.start()
    for cp in sends:
        cp.start()


def _xfer_wait(*refs, scatter, row_off):
    own, sends, recvs = _xfer_copies(*refs, scatter, row_off)
    for cp in recvs:
        cp.wait_recv()
    for cp in sends:
        cp.wait_send()
    own.wait()


def _pcall(body, args, *, grid, in_specs, out_specs, out_shape, scratch_shapes=(), sem, name, xfers=(), prefetch=()):
    out_specs, out_shape = list(out_specs), list(out_shape)
    xfers = [tuple(x) + (None,) * (3 - len(x)) for x in xfers]
    n_pre, n_in, n_out, n_x, n_scr = len(prefetch), len(in_specs), len(out_specs), len(xfers), len(scratch_shapes)
    flags = [(s, None if w is None else w[1]) for _, s, w in xfers]
    prevs = [(t, w[2]) for t, (_, _, w) in enumerate(xfers) if w is not None and w[2] is not None]
    n_b = len(prevs)
    assert not (prevs and prefetch)

    def xfer_shape(x, scatter, w):
        if w is not None:
            return jax.ShapeDtypeStruct((N_DEV, w[0]) + x.shape[1:], x.dtype)
        return jax.ShapeDtypeStruct((N_DEV,) + (x.shape[1:] if scatter else x.shape), x.dtype)

    def wrapped(*refs):
        pre, refs = refs[:n_pre], refs[n_pre:]
        ins, xin = refs[:n_in], refs[n_in:n_in + n_x]
        refs = refs[n_in + n_x + n_b:]
        outs, xout, scr = refs[:n_out], refs[n_out:n_out + n_x], refs[n_out + n_x:]
        own, sems = scr[:n_scr], scr[n_scr:]
        ids = [pl.program_id(d) for d in range(len(grid))]
        first = functools.reduce(jnp.logical_and, [i == 0 for i in ids])
        last = functools.reduce(jnp.logical_and, [i == g - 1 for i, g in zip(ids, grid)])

        @pl.when(first)
        def _():
            for t in range(n_x):
                _xfer_start(xin[t], xout[t], *sems[3 * t:3 * t + 3], scatter=flags[t][0], row_off=flags[t][1])

        body(*pre, *ins, *outs, *own)

        @pl.when(last)
        def _():
            for t in range(n_x):
                _xfer_wait(xin[t], xout[t], *sems[3 * t:3 * t + 3], scatter=flags[t][0], row_off=flags[t][1])

    grid_spec = pltpu.PrefetchScalarGridSpec(
        num_scalar_prefetch=n_pre, grid=grid, in_specs=list(in_specs) + [_ANY] * (n_x + n_b),
        out_specs=out_specs + [_ANY] * n_x, scratch_shapes=list(scratch_shapes) + list(_XFER_SEMS) * n_x)
    return pl.pallas_call(
        wrapped if xfers else body, grid_spec=grid_spec, out_shape=out_shape + [xfer_shape(*x) for x in xfers],
        input_output_aliases={n_in + n_x + b: n_out + t for b, (t, _) in enumerate(prevs)},
        compiler_params=_params(*(["arbitrary"] * len(grid) if xfers else sem)), name=name,
    )(*prefetch, *args, *[x[0] for x in xfers], *[p for _, p in prevs])


def _exchange(xfers, *, name):
    def body():
        pass

    return _pcall(body, (), grid=(1,), in_specs=[], out_specs=[], out_shape=[], sem=("arbitrary",), name=name,
                  xfers=xfers)


def _mm(a, b, *, ak="mk", bk="kn", ok="mn", add=None, norm_g=None, rms_bwd=None, out_dtype=F32, tm=1024, tn=1024,
        tk=1024, kg=1, name, xfers=()):
    sa, sb = a.shape, b.shape
    fm = fn = fk = None
    if ak == "mk":
        M, K, a_c = sa[0], sa[1], 1
    elif ak == "km":
        K, M, a_c = sa[0], sa[1], 0
    elif ak == "bmk":
        M, K, a_c, fk = sa[1], sa[0] * sa[2], 1, sa[2]
    else:
        K, M, a_c, fm = sa[1], sa[0] * sa[2], 0, sa[2]
    if bk == "kn":
        N, b_c = sb[1], 0
    elif bk == "nk":
        N, b_c = sb[0], 1
    elif bk == "bkn":
        N, b_c, fn = sb[0] * sb[2], 0, sb[2]
    elif bk == "bnk":
        N, b_c, fk = sb[1], 1, sb[2]
    elif bk == "kbn":
        N, b_c, fk = sb[2], 0, sb[1]
    else:
        N, b_c, fn = sb[0] * sb[1], 1, sb[1]
    tm, tn, tk = fm or _tile(M, tm), fn or _tile(N, tn), fk or _tile(K, tk)
    kblk = None if kg == 1 else kg
    nm, nn, nk = M // tm, N // tn, K // (tk * kg)

    a_spec = {"mk": pl.BlockSpec((tm, tk), lambda i, j, k: (i, k)),
              "km": pl.BlockSpec((tk, tm), lambda i, j, k: (k, i)),
              "bmk": pl.BlockSpec((kblk, tm, tk), lambda i, j, k: (k, i, 0)),
              "bkm": pl.BlockSpec((None, tk, tm), lambda i, j, k: (i, k, 0))}[ak]
    b_spec = {"kn": pl.BlockSpec((tk, tn), lambda i, j, k: (k, j)),
              "nk": pl.BlockSpec((tn, tk), lambda i, j, k: (j, k)),
              "bkn": pl.BlockSpec((None, tk, tn), lambda i, j, k: (j, k, 0)),
              "bnk": pl.BlockSpec((kblk, tn, tk), lambda i, j, k: (k, j, 0)),
              "kbn": pl.BlockSpec((kblk, tk, tn), lambda i, j, k: (k, 0, j)),
              "nbk": pl.BlockSpec((None, tn, tk), lambda i, j, k: (j, 0, k))}[bk]
    if ok == "mn":
        o_spec = pl.BlockSpec((tm, tn), lambda i, j, k: (i, j))
        out_shape = jax.ShapeDtypeStruct((M, N), out_dtype)
    elif ok == "bmn":
        o_spec = pl.BlockSpec((None, tm, tn), lambda i, j, k: (j, i, 0))
        out_shape = jax.ShapeDtypeStruct((nn, M, tn), out_dtype)
    else:
        o_spec = pl.BlockSpec((None, tm, tn), lambda i, j, k: (i, 0, j))
        out_shape = jax.ShapeDtypeStruct((nm, tm, N), out_dtype)
    dims = (((a_c,), (b_c,)), ((), ()))
    has_add = add is not None
    n_extra_in = int(has_add) + int(norm_g is not None) + (3 if rms_bwd is not None else 0)
    if norm_g is not None or rms_bwd is not None:
        assert ok == "mn" and tn == N

    def body(*refs):
        a_ref, b_ref = refs[0], refs[1]
        extra = list(refs[2:2 + n_extra_in])
        add_ref = extra.pop(0) if has_add else None
        ng_ref = extra.pop(0) if norm_g is not None else None
        outs = list(refs[2 + n_extra_in:])
        o_ref = outs.pop(0)
        hn_ref = outs.pop(0) if norm_g is not None else None
        dg_ref = outs.pop(0) if rms_bwd is not None else None
        if kg == 1:
            part = lax.dot_general(a_ref[...].astype(BF16), b_ref[...].astype(BF16), dims, preferred_element_type=F32)
        else:
            part = sum(lax.dot_general(a_ref[g].astype(BF16), b_ref[g].astype(BF16), dims,
                                       preferred_element_type=F32) for g in range(kg))
        if rms_bwd is not None:
            @pl.when(jnp.logical_and(pl.program_id(0) == 0, pl.program_id(2) == 0))
            def _():
                dg_ref[...] = jnp.zeros_like(dg_ref)

        def finish(r):
            if has_add:
                r = r + add_ref[...]
            if rms_bwd is not None:
                h_ref, g_ref, dres_ref = extra
                x = h_ref[...]
                rr = lax.rsqrt(jnp.mean(x * x, axis=1, keepdims=True) + EPS)
                xn = x * rr
                gdy = r * g_ref[...]
                dg_ref[...] += jnp.sum(r * xn, axis=0, keepdims=True)
                r = dres_ref[...] + rr * (gdy - xn * jnp.mean(gdy * xn, axis=1, keepdims=True))
            o_ref[...] = r.astype(out_dtype)
            if norm_g is not None:
                hn_ref[...] = (r * lax.rsqrt(jnp.mean(r * r, axis=1, keepdims=True) + EPS) * ng_ref[...]).astype(BF16)

        if nk == 1:
            finish(part)
            return
        acc = refs[-1]
        k = pl.program_id(2)

        @pl.when(k == 0)
        def _():
            acc[...] = part

        @pl.when(k > 0)
        def _():
            acc[...] += part

        @pl.when(k == nk - 1)
        def _():
            finish(acc[...])

    tile_spec = pl.BlockSpec((tm, tn), lambda i, j, k: (i, j))
    vec_spec = pl.BlockSpec((1, tn), lambda i, j, k: (0, j))
    in_specs, args = [a_spec, b_spec], [a, b]
    out_specs, out_shapes = [o_spec], [out_shape]
    if has_add:
        in_specs.append(tile_spec)
        args.append(add)
    if norm_g is not None:
        in_specs.append(vec_spec)
        args.append(norm_g)
        out_specs.append(tile_spec)
        out_shapes.append(jax.ShapeDtypeStruct((M, N), BF16))
    if rms_bwd is not None:
        in_specs += [tile_spec, vec_spec, tile_spec]
        args += list(rms_bwd)
        out_specs.append(vec_spec)
        out_shapes.append(jax.ShapeDtypeStruct((1, N), F32))
    res = _pcall(body, args, grid=(nm, nn, nk), in_specs=in_specs, out_specs=out_specs, out_shape=out_shapes,
                 scratch_shapes=[pltpu.VMEM((tm, tn), F32)] if nk > 1 else [],
                 sem=("arbitrary" if rms_bwd is not None else "parallel", "parallel", "arbitrary"), name=name,
                 xfers=xfers)
    return res if (xfers or len(out_specs) > 1) else res[0]


def _rms_fwd(h, g, *, name):
    L, D = h.shape
    tr = _tile(L, ROW_TILE, SUBLANES)

    def body(h_ref, g_ref, o_ref):
        x = h_ref[...]
        r = lax.rsqrt(jnp.mean(x * x, axis=1, keepdims=True) + EPS)
        o_ref[...] = (x * r * g_ref[...]).astype(BF16)

    return pl.pallas_call(
        body, grid=(L // tr,),
        in_specs=[pl.BlockSpec((tr, D), lambda i: (i, 0)), pl.BlockSpec((1, D), lambda i: (0, 0))],
        out_specs=pl.BlockSpec((tr, D), lambda i: (i, 0)), out_shape=jax.ShapeDtypeStruct((L, D), BF16),
        compiler_params=_params("parallel"), name=name)(h, g)


def _glu_res_rms(z, h, g, *, name):
    L, D = h.shape
    nb, cb = z.shape[1], z.shape[3]
    tr = _tile(L, ROW_TILE, SUBLANES)

    def body(z_ref, h_ref, g_ref, h1_ref, hn_ref):
        za = jnp.concatenate([z_ref[0, d] for d in range(nb)], axis=1)
        zg = jnp.concatenate([z_ref[1, d] for d in range(nb)], axis=1)
        x = h_ref[...] + za * jax.nn.sigmoid(zg)
        h1_ref[...] = x
        r = lax.rsqrt(jnp.mean(x * x, axis=1, keepdims=True) + EPS)
        hn_ref[...] = (x * r * g_ref[...]).astype(BF16)

    row = pl.BlockSpec((tr, D), lambda i: (i, 0))
    return pl.pallas_call(
        body, grid=(L // tr,),
        in_specs=[pl.BlockSpec((2, nb, tr, cb), lambda i: (0, 0, i, 0)), row, pl.BlockSpec((1, D), lambda i: (0, 0))],
        out_specs=[row, row],
        out_shape=[jax.ShapeDtypeStruct((L, D), F32), jax.ShapeDtypeStruct((L, D), BF16)],
        compiler_params=_params("parallel"), name=name)(z, h, g)


def _glu_bwd(z, dout, *, name):
    L, D = dout.shape
    nb, cb = z.shape[1], z.shape[3]
    tr = _tile(L, ROW_TILE, SUBLANES)

    def body(z_ref, d_ref, o_ref):
        for d in range(nb):
            dd = d_ref[:, d * cb:(d + 1) * cb]
            sg = jax.nn.sigmoid(z_ref[1, d])
            o_ref[0, d] = (dd * sg).astype(BF16)
            o_ref[1, d] = (dd * z_ref[0, d] * sg * (1.0 - sg)).astype(BF16)

    zs = pl.BlockSpec((2, nb, tr, cb), lambda i: (0, 0, i, 0))
    return pl.pallas_call(
        body, grid=(L // tr,), in_specs=[zs, pl.BlockSpec((tr, D), lambda i: (i, 0))], out_specs=zs,
        out_shape=jax.ShapeDtypeStruct(z.shape, BF16),
        compiler_params=_params("parallel"), name=name)(z, dout)


def _loss_head(h, g, target, *, name):
    L, D = h.shape
    tr = _tile(L, ROW_TILE, SUBLANES)

    def body(h_ref, g_ref, t_ref, loss_ref, dh_ref, dg_ref):
        @pl.when(pl.program_id(0) == 0)
        def _():
            dg_ref[...] = jnp.zeros_like(dg_ref)
            loss_ref[...] = jnp.zeros_like(loss_ref)

        x = h_ref[...]
        gg = g_ref[...]
        r = lax.rsqrt(jnp.mean(x * x, axis=1, keepdims=True) + EPS)
        xn = x * r
        err = xn * gg - t_ref[...]
        loss_ref[...] += 0.5 * jnp.sum(jnp.mean(err * err, axis=1, keepdims=True), axis=0, keepdims=True)
        dy = err * (1.0 / D)
        gdy = dy * gg
        dh_ref[...] = r * (gdy - xn * jnp.mean(gdy * xn, axis=1, keepdims=True))
        dg_ref[...] += jnp.sum(dy * xn, axis=0, keepdims=True)

    row = pl.BlockSpec((tr, D), lambda i: (i, 0))
    vec = pl.BlockSpec((1, D), lambda i: (0, 0))
    return pl.pallas_call(
        body, grid=(L // tr,), in_specs=[row, vec, row],
        out_specs=[pl.BlockSpec((1, 1), lambda i: (0, 0)), row, vec],
        out_shape=[jax.ShapeDtypeStruct((1, 1), F32), jax.ShapeDtypeStruct((L, D), F32),
                   jax.ShapeDtypeStruct((1, D), F32)],
        compiler_params=_params("arbitrary"), name=name)(h, g, target)


CONV_ROW_TILE = 256


def _sigmoid(x):
    return pl.reciprocal(1.0 + jnp.exp(-x), approx=True)


def _conv_specs(L, tr, tc):
    nrb = tr // SUBLANES
    before = lambda i: jnp.maximum(i * nrb - 1, 0)
    after = lambda i: jnp.minimum((i + 1) * nrb, L // SUBLANES - 1)
    main = pl.BlockSpec((2, None, tr, tc), lambda j, i: (0, j, i, 0))
    prev = pl.BlockSpec((2, None, SUBLANES, tc), lambda j, i: (0, j, before(i), 0))
    nxt = pl.BlockSpec((2, None, SUBLANES, tc), lambda j, i: (0, j, after(i), 0))
    cw = pl.BlockSpec((2, None, 3, tc), lambda j, i: (0, j, 0, 0))
    cb = pl.BlockSpec((2, None, 1, tc), lambda j, i: (0, j, 0, 0))
    half = pl.BlockSpec((None, tr, tc), lambda j, i: (j, i, 0))
    half_nxt = pl.BlockSpec((None, SUBLANES, tc), lambda j, i: (j, after(i), 0))
    return main, prev, nxt, cw, cb, half, half_nxt


def _conv_rows(xe, w, b):
    x1 = pltpu.roll(xe, 1, 0)
    x2 = pltpu.roll(xe, 2, 0)
    return b + x2 * w[0:1] + x1 * w[1:2] + xe * w[2:3], x1, x2


def _shift_down(x, halo, k, row):
    y = pltpu.roll(x, k, 0)
    for r in range(k):
        y = jnp.where(row == r, halo[SUBLANES - k + r:SUBLANES - k + r + 1, :], y)
    return y


def _conv_act(u0, cw, cb, *, name, xfers=()):
    _, nb, L, tc = u0.shape
    tr = _tile(L, ROW_TILE, SUBLANES)
    main, prev, _, cws, cbs, half, _ = _conv_specs(L, tr, tc)

    def body(u_ref, p_ref, w_ref, b_ref, a_ref):
        first = pl.program_id(1) == 0
        row = lax.broadcasted_iota(jnp.int32, (tr, tc), 0)
        y = []
        for s in range(2):
            x, w = u_ref[s], w_ref[s]
            halo = jnp.where(first, 0.0, p_ref[s])
            x1 = _shift_down(x, halo, 1, row)
            x2 = _shift_down(x, halo, 2, row)
            y.append(b_ref[s] + x2 * w[0:1] + x1 * w[1:2] + x * w[2:3])
        a_ref[...] = (y[0] * _sigmoid(y[0]) * y[1]).astype(BF16)

    return _pcall(body, (u0, u0, cw, cb), grid=(nb, L // tr), in_specs=[main, prev, cws, cbs], out_specs=[half],
                  out_shape=[jax.ShapeDtypeStruct((nb, L, tc), BF16)], sem=("parallel", "parallel"), name=name,
                  xfers=xfers)


def _conv_ffn_bwd(u0, cw, cb, da, *, name, xfers=()):
    _, nb, L, tc = u0.shape
    tr = _tile(L, CONV_ROW_TILE, SUBLANES)
    main, prev, nxt, cws, cbs, half, half_nxt = _conv_specs(L, tr, tc)
    nr = L // tr
    H = SUBLANES

    def body(u_ref, p_ref, n_ref, w_ref, b_ref, da_ref, dan_ref, a_ref, du0_ref, dcw_ref, dcb_ref):
        i = pl.program_id(1)

        @pl.when(i == 0)
        def _():
            dcw_ref[...] = jnp.zeros_like(dcw_ref)
            dcb_ref[...] = jnp.zeros_like(dcb_ref)

        y, x1, x2 = [], [], []
        for s in range(2):
            xe = jnp.concatenate([jnp.where(i == 0, 0.0, p_ref[s]), u_ref[s], n_ref[s]], axis=0)
            ys, x1s, x2s = _conv_rows(xe, w_ref[s], b_ref[s])
            y.append(ys[H:])
            x1.append(x1s[H:H + tr])
            x2.append(x2s[H:H + tr])
        gate, up = y
        da = jnp.concatenate([da_ref[...], dan_ref[...]], axis=0)
        row = lax.broadcasted_iota(jnp.int32, (tr + H, tc), 0)
        da = jnp.where(jnp.logical_and(i == nr - 1, row >= tr), 0.0, da)
        sg = _sigmoid(gate)
        silu = gate * sg
        a_ref[...] = (silu * up)[:tr].astype(BF16)
        d = (da * up * (sg * (1.0 + gate * (1.0 - sg))), da * silu)
        for s in range(2):
            w = w_ref[s]
            d0 = d[s][:tr]
            d1 = pltpu.roll(d[s], tr + H - 1, 0)[:tr]
            d2 = pltpu.roll(d[s], tr + H - 2, 0)[:tr]
            du0_ref[s] = (d0 * w[2:3] + d1 * w[1:2] + d2 * w[0:1]).astype(BF16)
            dcw_ref[s, 0:1, :] += jnp.sum(d0 * x2[s], axis=0, keepdims=True)
            dcw_ref[s, 1:2, :] += jnp.sum(d0 * x1[s], axis=0, keepdims=True)
            dcw_ref[s, 2:3, :] += jnp.sum(d0 * u_ref[s], axis=0, keepdims=True)
            dcb_ref[s] += jnp.sum(d0, axis=0, keepdims=True)

    return _pcall(body, (u0, u0, u0, cw, cb, da, da), grid=(nb, nr),
                  in_specs=[main, prev, nxt, cws, cbs, half, half_nxt], out_specs=[half, main, cws, cbs],
                  out_shape=[jax.ShapeDtypeStruct((nb, L, tc), BF16), jax.ShapeDtypeStruct((2, nb, L, tc), BF16),
                             jax.ShapeDtypeStruct((2, nb, 3, tc), F32), jax.ShapeDtypeStruct((2, nb, 1, tc), F32)],
                  sem=("parallel", "arbitrary"), name=name, xfers=xfers)


HALF = N_GROUPS // 2


def _swap(s):
    return jnp.concatenate([s[HALF:], s[:HALF]], axis=0)


def _chan_block(j):
    return ((j % HALF) // 4) * LANES


def _pairs_of(jb):
    return [2 * jb, 2 * jb + 1, HALF // 2 + 2 * jb, HALF // 2 + 2 * jb + 1]


_WB_SPEC = pl.BlockSpec((HALF, LANES, 2 * LANES), lambda c: (0, 0, 0))
_WC_SPEC = _WB_SPEC
GELU_C = math.sqrt(2.0 / math.pi)
GELU_A = 0.044715


def _gelu(x):
    return 0.5 * x * (1.0 + jnp.tanh(GELU_C * (x + GELU_A * x * x * x)))


def _gelu_grad(x):
    th = jnp.tanh(GELU_C * (x + GELU_A * x * x * x))
    return 0.5 * (1.0 + th) + 0.5 * x * (1.0 - th * th) * GELU_C * (1.0 + 3.0 * GELU_A * x * x)


def _tile_rows(j, T, TP):
    return pl.ds(j * TP + SUBLANES, T)


def _pair(ref, jp, T, TP):
    return jnp.concatenate([ref[_tile_rows(2 * jp, T, TP), :], ref[_tile_rows(2 * jp + 1, T, TP), :]],
                           axis=1).astype(BF16)


def _unpair(ref, jp, val, T, TP):
    ref[_tile_rows(2 * jp, T, TP), :] = val[:, :LANES]
    ref[_tile_rows(2 * jp + 1, T, TP), :] = val[:, LANES:]


def _s5_project_in(u_ref, wb_ref, s3, T, TP):
    for jp in range(HALF):
        blk = _chan_block(2 * jp)
        _unpair(s3, jp, jnp.dot(u_ref[:, blk:blk + LANES], wb_ref[jp], preferred_element_type=F32), T, TP)


def _s5_scan_fwd(s3, a1, a2, s0, T, TP):
    span = (N_GROUPS - 1) * TP + 2 * SUBLANES

    def blk(i, s):
        view = s3.at[pl.ds(pl.multiple_of(i * SUBLANES, SUBLANES), span)]
        for k in range(SUBLANES):
            rows = pl.ds(SUBLANES + k, N_GROUPS, stride=TP)
            s = a1 * s + a2 * _swap(s) + view[rows, :]
            view[rows, :] = s
        return s

    return lax.fori_loop(0, T // SUBLANES, blk, s0)


def _s5_fwd(hn, wb, wc, a1, a2, dvec, *, name, xfers=()):
    L, D = hn.shape
    T = min(S5_CHUNK, L)
    TP = T + SUBLANES
    nC = L // T

    def body(u_ref, wb_ref, wc_ref, a1_ref, a2_ref, d_ref, y_ref, yg_ref, sb_ref, s3, st):
        @pl.when(pl.program_id(0) == 0)
        def _():
            st[...] = jnp.zeros_like(st)

        sb_ref[0] = st[...]
        _s5_project_in(u_ref, wb_ref, s3, T, TP)
        st[...] = _s5_scan_fwd(s3, a1_ref[...], a2_ref[...], st[...], T, TP)
        for jb in range(D // LANES):
            acc = jnp.zeros((T, LANES), F32)
            for jp in _pairs_of(jb):
                acc += lax.dot_general(_pair(s3, jp, T, TP), wc_ref[jp], _NT, preferred_element_type=F32)
            cols = slice(jb * LANES, (jb + 1) * LANES)
            y = acc + d_ref[:, cols] * u_ref[:, cols].astype(F32)
            y_ref[:, cols] = y
            yg_ref[:, cols] = _gelu(y).astype(BF16)

    row = pl.BlockSpec((T, D), lambda c: (c, 0))
    aspec = pl.BlockSpec((N_GROUPS, LANES), lambda c: (0, 0))
    return _pcall(
        body, (hn, wb, wc, a1, a2, dvec), grid=(nC,),
        in_specs=[row, _WB_SPEC, _WC_SPEC, aspec, aspec, pl.BlockSpec((1, D), lambda c: (0, 0))],
        out_specs=[row, row, pl.BlockSpec((1, N_GROUPS, LANES), lambda c: (c, 0, 0))],
        out_shape=[jax.ShapeDtypeStruct((L, D), F32), jax.ShapeDtypeStruct((L, D), BF16),
                   jax.ShapeDtypeStruct((nC, N_GROUPS, LANES), F32)],
        scratch_shapes=[pltpu.VMEM((N_GROUPS * TP, LANES), F32), pltpu.VMEM((N_GROUPS, LANES), F32)],
        sem=("arbitrary",), name=name, xfers=xfers)


def _s5_bwd(hn, dyg, ypre, sbound, wb, wc, a1, a2, dvec, h, g, dres, *, name, xfers=()):
    L, D = hn.shape
    T = min(S5_CHUNK, L)
    TP = T + SUBLANES
    nC = L // T
    span = (N_GROUPS - 1) * TP + 2 * SUBLANES
    NT = (((1,), (1,)), ((), ()))
    TN = (((0,), (0,)), ((), ()))

    def body(u_ref, dyg_ref, yp_ref, sb_ref, wb_ref, wc_ref, a1_ref, a2_ref, d_ref, h_ref, g_ref, dres_ref,
             du_ref, dwb_ref, dwc_ref, da1_ref, da2_ref, dd_ref, dg_ref, s3, g3, gst, dy_s):
        @pl.when(pl.program_id(0) == 0)
        def _():
            gst[...] = jnp.zeros_like(gst)
            dwb_ref[...] = jnp.zeros_like(dwb_ref)
            dwc_ref[...] = jnp.zeros_like(dwc_ref)
            da1_ref[...] = jnp.zeros_like(da1_ref)
            da2_ref[...] = jnp.zeros_like(da2_ref)
            dd_ref[...] = jnp.zeros_like(dd_ref)
            dg_ref[...] = jnp.zeros_like(dg_ref)

        a1 = a1_ref[...]
        a2 = a2_ref[...]
        dy = dyg_ref[...].astype(F32) * _gelu_grad(yp_ref[...])
        dy_s[...] = dy.astype(BF16)
        dd_ref[...] += jnp.sum(dy * u_ref[...].astype(F32), axis=0, keepdims=True)
        du_ref[...] = d_ref[...] * dy

        s3[pl.ds(SUBLANES - 1, N_GROUPS, stride=TP), :] = sb_ref[0]
        _s5_project_in(u_ref, wb_ref, s3, T, TP)
        _s5_scan_fwd(s3, a1, a2, sb_ref[0], T, TP)

        for jp in range(HALF):
            blk = _chan_block(2 * jp)
            _unpair(g3, jp, jnp.dot(dy_s[:, blk:blk + LANES], wc_ref[jp], preferred_element_type=F32), T, TP)
        a2c = -a2

        def rblk(ii, carry):
            g, acc1, acc2 = carry
            t0 = pl.multiple_of((T // SUBLANES - 1 - ii) * SUBLANES, SUBLANES)
            gv = g3.at[pl.ds(t0, span)]
            sv = s3.at[pl.ds(t0, span)]
            for k in reversed(range(SUBLANES)):
                rows = pl.ds(SUBLANES + k, N_GROUPS, stride=TP)
                g = a1 * g + a2c * _swap(g) + gv[rows, :]
                gv[rows, :] = g
                sp = sv[pl.ds(SUBLANES - 1 + k, N_GROUPS, stride=TP), :]
                acc1 = acc1 + g * sp
                acc2 = acc2 + g * _swap(sp)
            return g, acc1, acc2

        zero = jnp.zeros((N_GROUPS, LANES), F32)
        g, acc1, acc2 = lax.fori_loop(0, T // SUBLANES, rblk, (gst[...], zero, zero))
        gst[...] = g
        da1_ref[...] += acc1
        da2_ref[...] += acc2

        for jb in range(D // LANES):
            cols = slice(jb * LANES, (jb + 1) * LANES)
            acc = jnp.zeros((T, LANES), F32)
            for jp in _pairs_of(jb):
                gp = _pair(g3, jp, T, TP)
                dwc_ref[jp] += lax.dot_general(dy_s[:, cols], _pair(s3, jp, T, TP), TN, preferred_element_type=F32)
                dwb_ref[jp] += lax.dot_general(u_ref[:, cols], gp, TN, preferred_element_type=F32)
                acc += lax.dot_general(gp, wb_ref[jp], NT, preferred_element_type=F32)
            du_ref[:, cols] += acc

        du = du_ref[...]
        x = h_ref[...]
        rr = lax.rsqrt(jnp.mean(x * x, axis=1, keepdims=True) + EPS)
        xn = x * rr
        gdy = du * g_ref[...]
        dg_ref[...] += jnp.sum(du * xn, axis=0, keepdims=True)
        du_ref[...] = dres_ref[...] + rr * (gdy - xn * jnp.mean(gdy * xn, axis=1, keepdims=True))

    rrow = pl.BlockSpec((T, D), lambda c: (nC - 1 - c, 0))
    aspec = pl.BlockSpec((N_GROUPS, LANES), lambda c: (0, 0))
    vec = pl.BlockSpec((1, D), lambda c: (0, 0))
    return _pcall(
        body, (hn, dyg, ypre, sbound, wb, wc, a1, a2, dvec, h, g, dres), grid=(nC,),
        in_specs=[rrow, rrow, rrow, pl.BlockSpec((1, N_GROUPS, LANES), lambda c: (nC - 1 - c, 0, 0)),
                  _WB_SPEC, _WC_SPEC, aspec, aspec, vec, rrow, vec, rrow],
        out_specs=[rrow, _WB_SPEC, _WC_SPEC, aspec, aspec, vec, vec],
        out_shape=[jax.ShapeDtypeStruct((L, D), F32),
                   jax.ShapeDtypeStruct((HALF, LANES, 2 * LANES), F32),
                   jax.ShapeDtypeStruct((HALF, LANES, 2 * LANES), F32),
                   jax.ShapeDtypeStruct((N_GROUPS, LANES), F32), jax.ShapeDtypeStruct((N_GROUPS, LANES), F32),
                   jax.ShapeDtypeStruct((1, D), F32), jax.ShapeDtypeStruct((1, D), F32)],
        scratch_shapes=[pltpu.VMEM((N_GROUPS * TP, LANES), F32), pltpu.VMEM((N_GROUPS * TP, LANES), F32),
                        pltpu.VMEM((N_GROUPS, LANES), F32), pltpu.VMEM((T, D), BF16)],
        sem=("arbitrary",), name=name, xfers=xfers)


def _s5_prep(lam_re, lam_im, log_dt, b_re, b_im, c_re, c_im):
    dt = jnp.exp(log_dt)[:, None]
    mag = jnp.exp(lam_re * dt)
    lb_re = mag * jnp.cos(lam_im * dt)
    lb_im = mag * jnp.sin(lam_im * dt)
    den = lam_re * lam_re + lam_im * lam_im
    nr = lb_re - 1.0
    fr = ((nr * lam_re + lb_im * lam_im) / den)[..., None]
    fi = ((lb_im * lam_re - nr * lam_im) / den)[..., None]
    bb_re = fr * b_re - fi * b_im
    bb_im = fr * b_im + fi * b_re
    pair = lambda a: a.reshape(HALF, 2 * SSM_STATE)
    a1 = jnp.concatenate([pair(lb_re), pair(lb_re)], axis=0)
    a2 = jnp.concatenate([-pair(lb_im), pair(lb_im)], axis=0)

    quads = N_GROUPS // 4
    rows, cols = 4 * SSM_GROUP, 4 * SSM_STATE
    diag = (jnp.arange(rows)[:, None] // SSM_GROUP == jnp.arange(cols)[None, :] // SSM_STATE).astype(F32)
    place = jnp.eye(2, dtype=F32)

    def expand(w):
        blocks = jnp.tile(w.reshape(quads, rows, SSM_STATE), (1, 1, 4)) * diag
        return jnp.einsum('kq,gkrl->gkqrl', place, blocks.reshape(quads // 2, 2, rows, cols)
                          ).reshape(quads, LANES, cols)

    wb = jnp.concatenate([expand(bb_re.transpose(0, 2, 1)), expand(bb_im.transpose(0, 2, 1))], axis=0)
    wc = jnp.concatenate([expand(c_re), expand(-c_im)], axis=0)
    return a1, a2, wb, wc


def _tri(n, upper):
    r = lax.broadcasted_iota(jnp.int32, (n, n), 0)
    c = lax.broadcasted_iota(jnp.int32, (n, n), 1)
    return ((r <= c) if upper else (r >= c)).astype(F32)


def _fgate_fwd(fl, bf, *, name):
    L, W = fl.shape
    tr = _tile(L, CUM_TILE, SUBLANES)

    def body(f_ref, b_ref, o_ref, carry):
        @pl.when(pl.program_id(0) == 0)
        def _():
            carry[...] = jnp.zeros_like(carry)

        x = f_ref[...] + b_ref[...]
        ls = jnp.minimum(x, 0.0) - jnp.log(1.0 + jnp.exp(-jnp.abs(x)))
        cum = jnp.dot(_tri(tr, False), ls, preferred_element_type=F32, precision=lax.Precision.HIGHEST) + carry[...]
        o_ref[...] = cum
        carry[...] = cum[tr - 1:tr, :]

    return pl.pallas_call(
        body, grid=(L // tr,),
        in_specs=[pl.BlockSpec((tr, W), lambda i: (i, 0)), pl.BlockSpec((1, W), lambda i: (0, 0))],
        out_specs=pl.BlockSpec((tr, W), lambda i: (i, 0)), out_shape=jax.ShapeDtypeStruct((L, W), F32),
        scratch_shapes=[pltpu.VMEM((1, W), F32)], compiler_params=_params("arbitrary"), name=name)(fl, bf)


def _fgate_bwd(fl, bf, dcum, *, name):
    L, W = fl.shape
    tr = _tile(L, CUM_TILE, SUBLANES)
    n = L // tr

    def body(f_ref, b_ref, d_ref, o_ref, db_ref, carry):
        @pl.when(pl.program_id(0) == 0)
        def _():
            carry[...] = jnp.zeros_like(carry)
            db_ref[...] = jnp.zeros_like(db_ref)

        d = d_ref[...]
        rev = jnp.dot(_tri(tr, True), d, preferred_element_type=F32, precision=lax.Precision.HIGHEST) + carry[...]
        carry[...] += jnp.sum(d, axis=0, keepdims=True)
        df = rev * jax.nn.sigmoid(-(f_ref[...] + b_ref[...]))
        o_ref[...] = df
        db_ref[...] += jnp.sum(df, axis=0, keepdims=True)

    rrow = pl.BlockSpec((tr, W), lambda i: (n - 1 - i, 0))
    vec = pl.BlockSpec((1, W), lambda i: (0, 0))
    return pl.pallas_call(
        body, grid=(n,), in_specs=[rrow, vec, rrow], out_specs=[rrow, vec],
        out_shape=[jax.ShapeDtypeStruct((L, W), F32), jax.ShapeDtypeStruct((1, W), F32)],
        scratch_shapes=[pltpu.VMEM((1, W), F32)], compiler_params=_params("arbitrary"), name=name)(fl, bf, dcum)


_NT = (((1,), (1,)), ((), ()))
_TN = (((0,), (0,)), ((), ()))
HEAD_PAIRS = N_HEADS // 2


def _causal_tiles(n):
    pairs = [(i, j) for j in range(n) for i in range(j, n)]
    return (jnp.array([p[0] for p in pairs], jnp.int32), jnp.array([p[1] for p in pairs], jnp.int32))


def _as_row(col):
    return jnp.transpose(jnp.broadcast_to(col, (col.shape[0], LANES)))[0:1, :]


def _attn_logits(qs, k, ck, masked, t):
    s = lax.dot_general(qs, k, _NT, preferred_element_type=F32) - ck
    if masked:
        r = lax.broadcasted_iota(jnp.int32, (t, t), 0)
        c = lax.broadcasted_iota(jnp.int32, (t, t), 1)
        s = jnp.where(c > r, NEG, s)
    return s


def _attn_fwd(q, kv, ck, *, name, xfers=()):
    L, D = q.shape
    t = _tile(L, ATTN_TILE)
    n = L // t
    dh = HEAD_DIM

    def body(q_ref, k_ref, v_ref, ck_ref, o_ref, o32_ref, lse_ref, m_s, l_s, acc):
        i, j = pl.program_id(1), pl.program_id(2)

        @pl.when(j == 0)
        def _():
            m_s[...] = jnp.full_like(m_s, NEG)
            l_s[...] = jnp.zeros_like(l_s)
            acc[...] = jnp.zeros_like(acc)

        def tile(masked):
            for e in range(2):
                sl = slice(e * dh, (e + 1) * dh)
                v = v_ref[:, sl]
                s = _attn_logits(q_ref[:, sl] * ATTN_SCALE, k_ref[:, sl], ck_ref[e], masked, t)
                m_new = jnp.maximum(m_s[e], jnp.max(s, axis=1, keepdims=True))
                alpha = jnp.exp(m_s[e] - m_new)
                p = jnp.exp(s - m_new)
                l_s[e] = alpha * l_s[e] + jnp.sum(p, axis=1, keepdims=True)
                p_hi = p.astype(BF16)
                p_lo = (p - p_hi.astype(F32)).astype(BF16)
                pv = (jnp.dot(p_hi, v, preferred_element_type=F32) + jnp.dot(p_lo, v, preferred_element_type=F32))
                acc[e] = alpha * acc[e] + pv
                m_s[e] = m_new

        pl.when(j < i)(functools.partial(tile, False))
        pl.when(j == i)(functools.partial(tile, True))

        @pl.when(j == n - 1)
        def _():
            for e in range(2):
                sl = slice(e * dh, (e + 1) * dh)
                o = acc[e] / l_s[e]
                o_ref[:, sl] = o.astype(BF16)
                o32_ref[:, sl] = o
                lse_ref[e] = _as_row(m_s[e] + jnp.log(l_s[e]))

    qs = pl.BlockSpec((t, LANES), lambda h, i, j: (i, h))
    ks = pl.BlockSpec((t, LANES), lambda h, i, j: (jnp.minimum(i, j), h))
    vs = pl.BlockSpec((t, LANES), lambda h, i, j: (jnp.minimum(i, j), HEAD_PAIRS + h))
    cs = pl.BlockSpec((2, 1, t), lambda h, i, j: (h, 0, jnp.minimum(i, j)))
    return _pcall(
        body, (q, kv, kv, ck), grid=(HEAD_PAIRS, n, n), in_specs=[qs, ks, vs, cs],
        out_specs=[qs, qs, pl.BlockSpec((2, 1, t), lambda h, i, j: (h, 0, i))],
        out_shape=[jax.ShapeDtypeStruct((L, D), BF16), jax.ShapeDtypeStruct((L, D), F32),
                   jax.ShapeDtypeStruct((N_HEADS, 1, L), F32)],
        scratch_shapes=[pltpu.VMEM((2, t, 1), F32), pltpu.VMEM((2, t, 1), F32), pltpu.VMEM((2, t, dh), F32)],
        sem=("parallel", "parallel", "arbitrary"), name=name, xfers=xfers)


def _attn_delta(do, o, *, name):
    L, D = do.shape
    t = _tile(L, ATTN_TILE)

    def body(do_ref, o_ref, d_ref):
        prod = do_ref[...].astype(F32) * o_ref[...]
        head = lax.broadcasted_iota(jnp.int32, (N_HEADS, D), 0)
        col = lax.broadcasted_iota(jnp.int32, (N_HEADS, D), 1)
        sel = (col // HEAD_DIM == head).astype(F32)
        d_ref[:, 0, :] = lax.dot_general(sel, prod, _NT, preferred_element_type=F32, precision=lax.Precision.HIGHEST)

    row = pl.BlockSpec((t, D), lambda i: (i, 0))
    return pl.pallas_call(
        body, grid=(L // t,), in_specs=[row, row], out_specs=pl.BlockSpec((N_HEADS, 1, t), lambda i: (0, 0, i)),
        out_shape=jax.ShapeDtypeStruct((N_HEADS, 1, L), F32), compiler_params=_params("parallel"), name=name)(do, o)


def _attn_bwd(q, kv, ck_col, do, lse_row, delta_row, *, name, xfers=()):
    L, D = q.shape
    t = _tile(L, ATTN_TILE)
    n = L // t
    dh = HEAD_DIM

    def body(i_tab, j_tab, q_ref, k_ref, v_ref, ck_ref, do_ref, lse_ref, dl_ref, dq_ref, dk_ref, dv_ref, dck_ref,
             dck_s):
        i, j = i_tab[pl.program_id(1)], j_tab[pl.program_id(1)]

        @pl.when(pl.program_id(1) == 0)
        def _():
            dq_ref[...] = jnp.zeros_like(dq_ref)

        @pl.when(i == j)
        def _():
            dk_ref[...] = jnp.zeros_like(dk_ref)
            dv_ref[...] = jnp.zeros_like(dv_ref)
            dck_s[...] = jnp.zeros_like(dck_s)

        def tile(masked):
            cols = pl.ds(pl.multiple_of(i * t, t), t)
            for e in range(2):
                sl = slice(e * dh, (e + 1) * dh)
                qs = q_ref[:, sl] * ATTN_SCALE
                k = k_ref[:, sl]
                do = do_ref[:, sl]
                st = lax.dot_general(k, qs, _NT, preferred_element_type=F32) - ck_ref[e]
                if masked:
                    kpos = lax.broadcasted_iota(jnp.int32, (t, t), 0)
                    qpos = lax.broadcasted_iota(jnp.int32, (t, t), 1)
                    st = jnp.where(kpos > qpos, NEG, st)
                pt = jnp.exp(st - lse_ref[e])
                dpt = lax.dot_general(v_ref[:, sl], do, _NT, preferred_element_type=F32)
                dst = pt * (dpt - dl_ref[e])
                dst16 = dst.astype(BF16)
                dv_ref[:, sl] += jnp.dot(pt.astype(BF16), do, preferred_element_type=F32)
                dk_ref[:, sl] += jnp.dot(dst16, qs, preferred_element_type=F32)
                dck_s[e] -= jnp.sum(dst, axis=1, keepdims=True)
                dq_ref[sl, cols] += lax.dot_general(k, dst16, _TN, preferred_element_type=F32) * ATTN_SCALE

        pl.when(i > j)(functools.partial(tile, False))
        pl.when(i == j)(functools.partial(tile, True))

        @pl.when(i == n - 1)
        def _():
            for e in range(2):
                dck_ref[e] = _as_row(dck_s[e])

    qs = pl.BlockSpec((t, LANES), lambda h, s, it, jt: (it[s], h))
    ks = pl.BlockSpec((t, LANES), lambda h, s, it, jt: (jt[s], h))
    vs = pl.BlockSpec((t, LANES), lambda h, s, it, jt: (jt[s], HEAD_PAIRS + h))
    cs = pl.BlockSpec((2, t, 1), lambda h, s, it, jt: (h, jt[s], 0))
    ls = pl.BlockSpec((2, 1, t), lambda h, s, it, jt: (h, 0, it[s]))
    full = jax.ShapeDtypeStruct((L, D), F32)
    tabs = _causal_tiles(n)
    return _pcall(
        body, (q, kv, kv, ck_col, do, lse_row, delta_row), grid=(HEAD_PAIRS, tabs[0].shape[0]),
        in_specs=[qs, ks, vs, cs, qs, ls, ls],
        out_specs=[pl.BlockSpec((LANES, L), lambda h, s, it, jt: (h, 0)), ks, ks,
                   pl.BlockSpec((2, 1, t), lambda h, s, it, jt: (h, 0, jt[s]))],
        out_shape=[jax.ShapeDtypeStruct((D, L), F32), full, full, jax.ShapeDtypeStruct((N_HEADS, 1, L), F32)],
        scratch_shapes=[pltpu.VMEM((2, t, 1), F32)],
        sem=("parallel", "arbitrary"), name=name, xfers=xfers, prefetch=tabs)


SHARD_COLS_FFN = 2 * D_FF // N_DEV
SHARD_ROWS_FFN = D_FF // N_DEV
SHARD_COLS_GLU = 2 * D_MODEL // N_DEV
SHARD_ROWS_QO = D_MODEL // N_DEV
SHARD_COLS_KVF = (2 * D_MODEL + N_HEADS) // N_DEV
S5_NAMES = ("lam_re", "lam_im", "log_dt", "ssm_b_re", "ssm_b_im", "ssm_c_re", "ssm_c_im")
REPL_LATE_ROWS = 272
REPL_EARLY_ROWS = 304


def _leaves(parts):
    out = []
    for p in parts:
        out.extend(_leaves(p) if isinstance(p, (list, tuple)) else [p.reshape(-1)])
    return out


def _whole_rows(n):
    return -(-n // D_MODEL)


def _pack_rows(parts, rows):
    blocks = [jnp.pad(p, (0, _whole_rows(p.shape[0]) * D_MODEL - p.shape[0])).reshape(-1, D_MODEL)
              for p in _leaves(parts)]
    used = sum(b.shape[0] for b in blocks)
    return jnp.concatenate(blocks + [jnp.zeros((rows - used, D_MODEL), blocks[0].dtype)], axis=0)


def _unpack_rows(packed, like):
    out, row = [], 0
    for p in _leaves(like):
        n = _whole_rows(p.shape[0])
        out.append(packed[row:row + n].reshape(-1)[:p.shape[0]])
        row += n
    return out


def _repl_late(d):
    return [d["g_mix"][0], [d[n][0] for n in S5_NAMES]]


def _repl_early(d):
    return [list(d["g_mix"][1:]), list(d["g_ffn"]), [d[n][1] for n in S5_NAMES], d["g_kv"], d["b_f"],
            list(d["ffn_conv_b"]), d["g_final"]]


def _kvf_blocks(full):
    return full.reshape(D_MODEL, N_DEV, SHARD_COLS_KVF).transpose(1, 0, 2)


class _Step:
    def __init__(self, weights, send=None, plan=None):
        self.w = dict(weights)
        self.send = send or {}
        self.grad = {}
        self.bcast = {}
        self.slots = {}
        self.plan = plan or {}

    def xfers(self, host):
        src = {"w": self.send, "g": self.grad, "b": self.bcast}
        out = []
        for kind, k in self.plan.get(host, ()):
            if kind == "w" and k[0] == "w_ffn_in" and len(k) == 3:
                parts = _W_IN_PARTS[k[1]]
                out.append((self.send[k], False, (sum(parts), sum(parts[:k[2]]), self.w.get(("w_ffn_in_parts", k[1])))))
            else:
                out.append((src[kind][k], kind == "g"))
        return out

    def land(self, host, gathered):
        for (kind, k), g in zip(self.plan.get(host, ()), gathered):
            if kind == "w":
                self.arrive(k, g)
            else:
                self.slots[k] = g

    def arrive(self, k, g):
        name = k[0]
        if name == "w_ffn_in" and len(k) == 3:
            self.w["w_ffn_in_parts", k[1]] = g
            if k[2] == len(_W_IN_PARTS[k[1]]) - 1:
                self.w[name, k[1]] = g
        elif name == "w_ffn_out":
            self.w[k] = g.reshape(4, 2 * SHARD_ROWS_FFN, D_MODEL)
        elif name in ("w_q", "w_o"):
            self.w[k] = g.reshape(D_MODEL, D_MODEL)
        elif name == "w_kvf":
            full = g.transpose(1, 0, 2).reshape(D_MODEL, N_DEV * SHARD_COLS_KVF)
            self.w["w_kv",] = full[:, :2 * D_MODEL]
            self.w["w_f",] = jnp.pad(full[:, 2 * D_MODEL:], ((0, 0), (0, LANES - N_HEADS)))
        elif name == "small":
            flat = g.reshape(N_DEV, -1)
            self.w["ssm_d",] = flat[:, :256].reshape(N_DEV, N_A, LANES).transpose(1, 0, 2).reshape(N_A, D_MODEL)
            cw = flat[:, 256:256 + DEPTH * 3 * SHARD_COLS_FFN].reshape(N_DEV, DEPTH, 3, SHARD_COLS_FFN)
            for layer in range(DEPTH):
                self.w["conv_w", layer] = cw[:, layer].reshape(2, 4, 3, SHARD_COLS_FFN)
        else:
            self.w[k] = g

    def run(self, host, fn, *args, **kw):
        xf = self.xfers(host)
        res = fn(*args, name=host, xfers=xf, **kw)
        if not xf:
            return res[0] if isinstance(res, (list, tuple)) and len(res) == 1 else res
        n_own = len(res) - len(xf)
        self.land(host, res[n_own:])
        return res[0] if n_own == 1 else res[:n_own]


def _step(x, target, S):
    L = x.shape[0]
    W = S.w
    vec = lambda a: a.reshape(1, -1)
    CF = SHARD_COLS_FFN

    h = x
    saved = []
    kvs = None
    hn_next = None
    for layer in range(DEPTH):
        t = str(layer)
        if layer < N_A:
            (a1, a2, wb, wc), prep_vjp = jax.vjp(_s5_prep, *[W[n][layer] for n in S5_NAMES])
            wb16, wc16 = wb.astype(BF16), wc.astype(BF16)
            hn = hn_next if hn_next is not None else _rms_fwd(h, vec(W["g_mix"][layer]), name="mix_norm" + t)
            dvec = vec(W["ssm_d",][layer])
            ypre, yg, sb = S.run("s5_fwd" + t, _s5_fwd, hn, wb16, wc16, a1, a2, dvec)
            z = S.run("glu_mm" + t, _mm, yg, W["w_glu", layer], bk="bkn", ok="bmn", tm=4096)
            z = z.reshape(2, 4, L, SHARD_COLS_GLU)
            h1, hn2 = _glu_res_rms(z, h, vec(W["g_ffn"][layer]), name="glu_res" + t)
            mix_saved = (h, hn, ypre, yg, sb, z, a1, a2, wb16, wc16, dvec, prep_vjp)
        else:
            j = layer - N_A
            if layer == N_A:
                hkv = _rms_fwd(h, vec(W["g_kv"]), name="kv_norm")
                kvm = S.run("kv_mm", _mm, hkv, W["w_kv",], out_dtype=BF16)
                fl = S.run("f_mm", _mm, hkv, W["w_f",])
                cum = _fgate_fwd(fl, W["b_f_pad",], name="fgate_fwd")
                ck = cum[:, :N_HEADS].T.reshape(N_HEADS, 1, L)
                kvs = (h, hkv, fl, kvm, ck)
            _, _, _, kvm, ck = kvs
            hn = hn_next if hn_next is not None else _rms_fwd(h, vec(W["g_mix"][layer]), name="mix_norm" + t)
            q = S.run("q_mm" + t, _mm, hn, W["w_q", j], out_dtype=BF16)
            o, o32, lse = S.run("attn_fwd" + t, _attn_fwd, q, kvm, ck)
            h1, hn2 = S.run("o_mm" + t, _mm, o, W["w_o", j], add=h, norm_g=vec(W["g_ffn"][layer]))
            mix_saved = (h, hn, q, o32, o, lse)
        u0 = S.run("ffn_in" + t, _mm, hn2, W["w_ffn_in", layer], bk="nbk", ok="bmn", tm=2048).reshape(2, 4, L, CF)
        a = S.run("ffn_act" + t, _conv_act, u0, W["conv_w", layer], W["conv_b", layer])
        if layer + 1 < DEPTH:
            h2, hn_next = S.run("ffn_out" + t, _mm, a, W["w_ffn_out", layer], ak="bmk", bk="kbn", add=h1, kg=4,
                                norm_g=vec(W["g_mix"][layer + 1]))
        else:
            h2 = S.run("ffn_out" + t, _mm, a, W["w_ffn_out", layer], ak="bmk", bk="kbn", add=h1, tn=512, kg=4)
        saved.append((mix_saved, h1, hn2, u0))
        h = h2

    loss, dh, dg_final = _loss_head(h, vec(W["g_final"]), target, name="loss_head")
    g = {"g_final": dg_final.reshape(-1)}
    gl = {k: [None] * DEPTH for k in ("g_mix", "g_ffn", "conv_w", "ffn_conv_b")}
    ga = {k: [None] * N_A for k in S5_NAMES + ("ssm_d",)}
    dk = dv = dck = None
    for layer in reversed(range(DEPTH)):
        t = str(layer)
        mix_saved, h1, hn2, u0 = saved[layer]
        cw, cb = W["conv_w", layer], W["conv_b", layer]
        da = S.run("ffn_da" + t, _mm, dh, W["w_ffn_out", layer], bk="nbk", ok="bmn", tm=2048)
        a, du0, dcw, dcb = S.run("ffn_conv_bwd" + t, _conv_ffn_bwd, u0, cw, cb, da)
        dw_out = S.run("ffn_dwout" + t, _mm, a, dh, ak="bkm", ok="mbn", out_dtype=BF16)
        S.grad["w_ffn_out", layer] = dw_out.reshape(N_DEV, SHARD_ROWS_FFN, D_MODEL)
        du0 = du0.reshape(N_DEV, L, CF)
        S.grad["w_ffn_in", layer] = S.run("ffn_dwin" + t, _mm, du0, hn2, ak="bkm", ok="mbn", out_dtype=BF16)
        dh1, dg = S.run("ffn_dhn" + t, _mm, du0, W["w_ffn_in", layer], ak="bmk", bk="kbn", kg=4, tm=512,
                        rms_bwd=(h1, vec(W["g_ffn"][layer]), dh))
        gl["g_ffn"][layer], gl["conv_w"][layer], gl["ffn_conv_b"][layer] = dg.reshape(-1), dcw, dcb.reshape(-1)
        if layer < N_A:
            hin, hn, ypre, yg, sb, z, a1, a2, wb16, wc16, dvec, prep_vjp = mix_saved
            dz = _glu_bwd(z, dh1, name="glu_bwd" + t).reshape(N_DEV, L, SHARD_COLS_GLU)
            S.grad["w_glu", layer] = S.run("glu_dw" + t, _mm, yg, dz, ak="km", bk="bkn", ok="bmn", out_dtype=BF16,
                                           tk=4096)
            dyg = S.run("glu_dy" + t, _mm, dz, W["w_glu", layer], ak="bmk", bk="bnk", out_dtype=BF16, tm=2048, kg=8)
            if layer == 0:
                S.bcast["repl_early",] = _pack_rows(_repl_early({**g, **gl, **ga}), REPL_EARLY_ROWS).astype(BF16)
            dh, dwb, dwc, da1, da2, dd, dg = S.run("s5_bwd" + t, _s5_bwd, hn, dyg, ypre, sb, wb16, wc16, a1, a2, dvec,
                                                   hin, vec(W["g_mix"][layer]), dh1)
            for nme, val in zip(S5_NAMES, prep_vjp((da1, da2, dwb, dwc))):
                ga[nme][layer] = val
            ga["ssm_d"][layer] = dd.reshape(-1)
        else:
            j = layer - N_A
            hin, hn, q, o32, o, lse = mix_saved
            _, _, _, kvm, ck = kvs
            S.grad["w_o", j] = S.run("o_dw" + t, _mm, o, dh1, ak="km", out_dtype=BF16
                                     ).reshape(N_DEV, SHARD_ROWS_QO, D_MODEL)
            do = S.run("o_dx" + t, _mm, dh1, W["w_o", j], bk="nk", out_dtype=BF16)
            delta = _attn_delta(do, o32, name="attn_delta" + t)
            dq_t, dk_l, dv_l, dck_l = S.run("attn_bwd" + t, _attn_bwd, q, kvm, ck.reshape(N_HEADS, L, 1), do,
                                            lse, delta)
            dk = dk_l if dk is None else dk + dk_l
            dv = dv_l if dv is None else dv + dv_l
            dck = dck_l if dck is None else dck + dck_l
            S.grad["w_q", j] = S.run("q_dw" + t, _mm, hn, dq_t, ak="km", bk="nk", out_dtype=BF16
                                     ).reshape(N_DEV, SHARD_ROWS_QO, D_MODEL)
            dh, dg = S.run("q_dx" + t, _mm, dq_t, W["w_q", j], ak="km", bk="nk",
                           rms_bwd=(hin, vec(W["g_mix"][layer]), dh1))
            if layer == N_A:
                hkv_in, hkv, fl, _, _ = kvs
                dcum = jnp.pad(dck.reshape(N_HEADS, L).T, ((0, 0), (0, LANES - N_HEADS)))
                dfl, dbf = _fgate_bwd(fl, W["b_f_pad",], dcum, name="fgate_bwd")
                dkv = jnp.concatenate([dk, dv], axis=1).astype(BF16)
                dfl16 = dfl.astype(BF16)
                dw_kv = S.run("kv_dw", _mm, hkv, dkv, ak="km")
                dw_f = S.run("f_dw", _mm, hkv, dfl16, ak="km")
                S.grad["w_kvf",] = _kvf_blocks(jnp.concatenate([dw_kv, dw_f[:, :N_HEADS]], axis=1)).astype(BF16)
                dhkv = S.run("kv_dx", _mm, dkv, W["w_kv",], bk="nk")
                dh, dgkv = S.run("f_dx", _mm, dfl16, W["w_f",], bk="nk", add=dhkv,
                                 rms_bwd=(hkv_in, vec(W["g_kv"]), dh))
                g["b_f"] = dbf[0, :N_HEADS]
                g["g_kv"] = dgkv.reshape(-1)
        gl["g_mix"][layer] = dg.reshape(-1)

    for d in (gl, ga):
        for k, v in d.items():
            g[k] = jnp.stack(v)
    return loss, dh, g


def _adamw_layers(slots, w, m, v, *, name):
    shape = w.shape
    nl = len(slots)
    w, m, v = (a.reshape((nl,) + a.shape[-2:]) for a in (w, m, v))
    _, R, C = w.shape
    tr = _tile(R, 256, 16)
    c1 = 1.0 / (1.0 - ADAM_B1 ** ADAM_STEP)
    c2 = 1.0 / (1.0 - ADAM_B2 ** ADAM_STEP)

    def body(*refs):
        s_refs, (w_ref, m_ref, v_ref), (g_ref, d_ref, nm_ref, nv_ref) = refs[:nl], refs[nl:nl + 3], refs[nl + 3:]
        for layer in range(nl):
            @pl.when(pl.program_id(0) == layer)
            def _(s_ref=s_refs[layer]):
                g = s_ref[0].astype(F32)
                for d in range(1, N_DEV):
                    g = g + s_ref[d].astype(F32)
                m2 = ADAM_B1 * m_ref[...] + (1.0 - ADAM_B1) * g
                v2 = ADAM_B2 * v_ref[...] + (1.0 - ADAM_B2) * (g * g)
                g_ref[...] = g
                nm_ref[...] = m2
                nv_ref[...] = v2
                d_ref[...] = -ADAM_LR * ((m2 * c1) / (jnp.sqrt(v2 * c2) + ADAM_EPS) + ADAM_WD * w_ref[...])

    def slab_spec(layer):
        return pl.BlockSpec((N_DEV, tr, C), lambda l, i: (0, jnp.where(l == layer, i, 0), 0))

    row = pl.BlockSpec((None, tr, C), lambda l, i: (l, i, 0))
    out = jax.ShapeDtypeStruct((nl, R, C), F32)
    outs = pl.pallas_call(
        body, grid=(nl, R // tr), in_specs=[slab_spec(layer) for layer in range(nl)] + [row, row, row],
        out_specs=[row, row, row, row], out_shape=[out, out, out, out],
        compiler_params=_params("arbitrary", "arbitrary"), name=name)(*slots, w, m, v)
    return [o.reshape(shape) for o in outs]


_SMALL_ROWS = 72
_ORDER = ("g_mix", "g_ffn", "lam_re", "lam_im", "log_dt", "ssm_b_re", "ssm_b_im", "ssm_c_re", "ssm_c_im", "ssm_d",
          "w_glu", "g_kv", "w_kvf", "b_f", "w_q", "w_o", "w_ffn_in", "ffn_conv_w", "ffn_conv_b", "w_ffn_out", "g_final")


def _pack_small(ssm_d, conv_w):
    flat = jnp.concatenate([ssm_d.reshape(-1), conv_w.reshape(-1)])
    return jnp.pad(flat, (0, _SMALL_ROWS * LANES - flat.shape[0])).reshape(_SMALL_ROWS, LANES)


def _unpack_small(flat):
    flat = flat.reshape(-1)
    return flat[:256].reshape(2, 128), flat[256:256 + 8448].reshape(4, 3, 704)


_W_IN_PARTS = {0: (352, 352), 1: (176, 528)}
_FWD_PLAN = {
    "start": [("small",)],
    "s5_fwd0": [("w_glu", 0), ("w_ffn_in", 0, 0)],
    "glu_mm0": [("w_ffn_in", 0, 1)],
    "ffn_in0": [("w_ffn_out", 0)],
    "ffn_act0": [("w_glu", 1)],
    "ffn_out0": [("w_ffn_in", 1, 0)],
    "s5_fwd1": [("w_ffn_in", 1, 1)],
    "ffn_in1": [("w_ffn_out", 1), ("w_o", 0)],
    "ffn_act1": [("w_kvf",)],
    "ffn_out1": [("w_q", 0)],
    "attn_fwd2": [("w_ffn_in", 2), ("w_ffn_out", 2)],
    "ffn_act2": [("w_q", 1), ("w_o", 1)],
    "attn_fwd3": [("w_ffn_in", 3)],
    "ffn_in3": [("w_ffn_out", 3)],
}
_BWD_PLAN = {
    "ffn_dhn3": [("w_ffn_out", 3)],
    "attn_bwd3": [("w_ffn_in", 3), ("w_o", 1)],
    "ffn_conv_bwd2": [("w_q", 1)],
    "ffn_dhn2": [("w_ffn_out", 2)],
    "attn_bwd2": [("w_ffn_in", 2), ("w_o", 0)],
    "ffn_conv_bwd1": [("w_q", 0), ("w_kvf",)],
    "ffn_dhn1": [("w_ffn_out", 1)],
    "s5_bwd1": [("w_ffn_in", 1), ("w_glu", 1)],
    "ffn_dhn0": [("w_ffn_out", 0)],
    "s5_bwd0": [("w_ffn_in", 0), ("w_glu", 0), ("repl_early",)],
    "end": [("small",), ("repl_late",)],
}
_PLAN = {h: [("w", k) for k in ks] for h, ks in _FWD_PLAN.items()}
_PLAN.update({h: [("b" if k[0].startswith("repl") else "g", k) for k in ks] for h, ks in _BWD_PLAN.items()})


def kernel(x, g_mix, g_ffn, lam_re, lam_im, log_dt, ssm_b_re, ssm_b_im, ssm_c_re, ssm_c_im, ssm_d, w_glu, g_kv, w_kvf, b_f, w_q, w_o, w_ffn_in, ffn_conv_w, ffn_conv_b, w_ffn_out, g_final, loss_target, m_g_mix, m_g_ffn, m_lam_re, m_lam_im, m_log_dt, m_ssm_b_re, m_ssm_b_im, m_ssm_c_re, m_ssm_c_im, m_ssm_d, m_w_glu, m_g_kv, m_w_kvf, m_b_f, m_w_q, m_w_o, m_w_ffn_in, m_ffn_conv_w, m_ffn_conv_b, m_w_ffn_out, m_g_final, v_g_mix, v_g_ffn, v_lam_re, v_lam_im, v_log_dt, v_ssm_b_re, v_ssm_b_im, v_ssm_c_re, v_ssm_c_im, v_ssm_d, v_w_glu, v_g_kv, v_w_kvf, v_b_f, v_w_q, v_w_o, v_w_ffn_in, v_ffn_conv_w, v_ffn_conv_b, v_w_ffn_out, v_g_final):
    wts = dict(g_mix=g_mix, g_ffn=g_ffn, lam_re=lam_re, lam_im=lam_im, log_dt=log_dt, ssm_b_re=ssm_b_re,
               ssm_b_im=ssm_b_im, ssm_c_re=ssm_c_re, ssm_c_im=ssm_c_im, ssm_d=ssm_d, w_glu=w_glu, g_kv=g_kv,
               w_kvf=w_kvf, b_f=b_f, w_q=w_q, w_o=w_o, w_ffn_in=w_ffn_in, ffn_conv_w=ffn_conv_w,
               ffn_conv_b=ffn_conv_b, w_ffn_out=w_ffn_out, g_final=g_final)
    mom = dict(g_mix=m_g_mix, g_ffn=m_g_ffn, lam_re=m_lam_re, lam_im=m_lam_im, log_dt=m_log_dt, ssm_b_re=m_ssm_b_re,
               ssm_b_im=m_ssm_b_im, ssm_c_re=m_ssm_c_re, ssm_c_im=m_ssm_c_im, ssm_d=m_ssm_d, w_glu=m_w_glu,
               g_kv=m_g_kv, w_kvf=m_w_kvf, b_f=m_b_f, w_q=m_w_q, w_o=m_w_o, w_ffn_in=m_w_ffn_in,
               ffn_conv_w=m_ffn_conv_w, ffn_conv_b=m_ffn_conv_b, w_ffn_out=m_w_ffn_out, g_final=m_g_final)
    var = dict(g_mix=v_g_mix, g_ffn=v_g_ffn, lam_re=v_lam_re, lam_im=v_lam_im, log_dt=v_log_dt, ssm_b_re=v_ssm_b_re,
               ssm_b_im=v_ssm_b_im, ssm_c_re=v_ssm_c_re, ssm_c_im=v_ssm_c_im, ssm_d=v_ssm_d, w_glu=v_w_glu,
               g_kv=v_g_kv, w_kvf=v_w_kvf, b_f=v_b_f, w_q=v_w_q, w_o=v_w_o, w_ffn_in=v_w_ffn_in,
               ffn_conv_w=v_ffn_conv_w, ffn_conv_b=v_ffn_conv_b, w_ffn_out=v_w_ffn_out, g_final=v_g_final)
    kinds = ("grad", "delta", "m", "v")

    ready = {n: wts[n] for n in ("g_mix", "g_ffn", "g_kv", "g_final") + S5_NAMES}
    ready["b_f_pad",] = jnp.pad(b_f, (0, LANES - N_HEADS)).reshape(1, LANES)
    send = {("small",): _pack_small(ssm_d, ffn_conv_w), ("w_kvf",): w_kvf.astype(BF16)}
    for layer in range(DEPTH):
        ready["conv_b", layer] = ffn_conv_b[layer].reshape(2, 4, 1, SHARD_COLS_FFN)
        w_in_t = jnp.swapaxes(w_ffn_in[layer], 0, 1).astype(BF16)
        if layer in _W_IN_PARTS:
            row = 0
            for p, rows in enumerate(_W_IN_PARTS[layer]):
                send["w_ffn_in", layer, p] = w_in_t[row:row + rows]
                row += rows
        else:
            send["w_ffn_in", layer] = w_in_t
        send["w_ffn_out", layer] = w_ffn_out[layer].astype(BF16)
    for layer in range(N_A):
        send["w_glu", layer] = w_glu[layer].astype(BF16)
        send["w_q", layer] = w_q[layer].astype(BF16)
        send["w_o", layer] = w_o[layer].astype(BF16)

    S = _Step(ready, send, _PLAN)
    S.land("start", _exchange(S.xfers("start"), name="start"))
    loss, dx, g = _step(x[0], loss_target[0], S)
    loss = lax.psum(loss[0, 0], MESH_AXES)

    g_d = g["ssm_d"].reshape(N_A, N_DEV, LANES).transpose(1, 0, 2).reshape(N_DEV, N_A * LANES)
    g_cw = jnp.stack([g["conv_w"][layer].reshape(N_DEV, 3, SHARD_COLS_FFN) for layer in range(DEPTH)], axis=1)
    g_small = jnp.concatenate([g_d, g_cw.reshape(N_DEV, -1)], axis=1)
    g_small = jnp.pad(g_small, ((0, 0), (0, _SMALL_ROWS * LANES - g_small.shape[1])))
    S.grad["small",] = g_small.reshape(N_DEV, _SMALL_ROWS, LANES)
    S.bcast["repl_late",] = _pack_rows(_repl_late(g), REPL_LATE_ROWS).astype(BF16)
    S.land("end", _exchange(S.xfers("end"), name="end"))

    res = {}
    for name, nl in (("w_glu", N_A), ("w_q", DEPTH - N_A), ("w_o", DEPTH - N_A), ("w_ffn_in", DEPTH),
                     ("w_ffn_out", DEPTH)):
        view = (lambda a: jnp.swapaxes(a, 1, 2)) if name == "w_ffn_in" else (lambda a: a)
        outs = _adamw_layers([S.slots[name, layer] for layer in range(nl)], view(wts[name]), view(mom[name]),
                             view(var[name]), name="adamw_" + name)
        res.update({(kind, name): view(a) for kind, a in zip(kinds, outs)})
    outs = _adamw_layers([S.slots["w_kvf",]], w_kvf, m_w_kvf, v_w_kvf, name="adamw_w_kvf")
    res.update({(kind, "w_kvf"): a for kind, a in zip(kinds, outs)})
    outs = _adamw_layers([S.slots["small",]], _pack_small(ssm_d, ffn_conv_w), _pack_small(m_ssm_d, m_ffn_conv_w),
                         _pack_small(v_ssm_d, v_ffn_conv_w), name="adamw_small")
    for kind, flat in zip(kinds, outs):
        res[kind, "ssm_d"], res[kind, "ffn_conv_w"] = _unpack_small(flat)

    pieces = {}
    for key, rows, sel in ((("repl_early",), REPL_EARLY_ROWS, _repl_early), (("repl_late",), REPL_LATE_ROWS, _repl_late)):
        outs = _adamw_layers([S.slots[key]], *[_pack_rows(sel(d), rows) for d in (wts, mom, var)],
                             name="adamw_" + key[0])
        for kind, flat in zip(kinds, outs):
            pieces[kind, key[0]] = _unpack_rows(flat, sel(wts))
    for kind in kinds:
        early, late = iter(pieces[kind, "repl_early"]), iter(pieces[kind, "repl_late"])
        take = lambda it, n: [next(it) for _ in range(n)]
        res[kind, "g_mix"] = jnp.stack(take(late, 1) + take(early, DEPTH - 1))
        res[kind, "g_ffn"] = jnp.stack(take(early, DEPTH))
        for n in S5_NAMES:
            res[kind, n] = jnp.stack([next(late), next(early)]).reshape(wts[n].shape)
        res[kind, "g_kv"], res[kind, "b_f"] = next(early), next(early)
        res[kind, "ffn_conv_b"] = jnp.stack(take(early, DEPTH))
        res[kind, "g_final"] = next(early)

    return (loss, dx[None], *[res[kind, n] for kind in kinds for n in _ORDER])
```

```python
import functools
import math

import jax
import jax.numpy as jnp
from jax import lax
from jax.experimental import pallas as pl
from jax.experimental.pallas import tpu as pltpu

F32 = jnp.float32
BF16 = jnp.bfloat16

D_MODEL = 1024
DEPTH = 4
N_A = 2
N_GROUPS = 64
SSM_GROUP = 16
SSM_STATE = 64
N_HEADS = 16
HEAD_DIM = 64
ATTN_SCALE = HEAD_DIM ** -0.5
D_FF = 2816
EPS = 1e-6
N_DEV = 8
LANES = 128
SUBLANES = 8

ADAM_LR = 0.001
ADAM_B1 = 0.9
ADAM_B2 = 0.999
ADAM_EPS = 1e-08
ADAM_WD = 0.01
ADAM_STEP = 10

ROW_TILE = 512
S5_CHUNK = 256
ATTN_TILE = 512
CUM_TILE = 256
NEG = -1e30

MESH_AXES = ("x", "y", "c")


def _tile(n, target, align=LANES):
    t = (min(target, n) // align) * align
    while t >= align:
        if n % t == 0:
            return t
        t -= align
    return n


def _params(*sem):
    return pltpu.CompilerParams(dimension_semantics=sem, vmem_limit_bytes=56 * 1024 * 1024)


_ANY = pl.BlockSpec(memory_space=pl.ANY)
_XFER_SEMS = (pltpu.SemaphoreType.DMA((N_DEV - 1,)), pltpu.SemaphoreType.DMA((N_DEV - 1,)), pltpu.SemaphoreType.DMA)


def _xfer_copies(x_ref, o_ref, send_sems, recv_sems, local_sem, scatter, row_off):
    xi, yi, ci = lax.axis_index("x"), lax.axis_index("y"), lax.axis_index("c")
    me = 4 * xi + 2 * yi + ci

    def src(p):
        return x_ref.at[p] if scatter else x_ref

    def dst(p):
        return o_ref.at[p] if row_off is None else o_ref.at[p, pl.ds(row_off, x_ref.shape[0])]

    own = pltpu.make_async_copy(src(me), dst(me), local_sem)
    sends, recvs = [], []
    for k in range(1, N_DEV):
        px, py, pc = xi ^ (k >> 2), yi ^ ((k >> 1) & 1), ci ^ (k & 1)
        p = 4 * px + 2 * py + pc
        sends.append(pltpu.make_async_remote_copy(
            src_ref=src(p), dst_ref=dst(me), send_sem=send_sems.at[k - 1], recv_sem=recv_sems.at[k - 1],
            device_id=(px, py, pc), device_id_type=pl.DeviceIdType.MESH))
        recvs.append(pltpu.make_async_remote_copy(
            src_ref=src(p), dst_ref=dst(p), send_sem=send_sems.at[k - 1], recv_sem=recv_sems.at[k - 1],
            device_id=(px, py, pc), device_id_type=pl.DeviceIdType.MESH))
    return own, sends, recvs


def _xfer_start(*refs, scatter, row_off):
    own, sends, _ = _xfer_copies(*refs, scatter, row_off)
    own.start()
    for cp in sends:
        cp.start()


def _xfer_wait(*refs, scatter, row_off):
    own, sends, recvs = _xfer_copies(*refs, scatter, row_off)
    for cp in recvs:
        cp.wait_recv()
    for cp in sends:
        cp.wait_send()
    own.wait()


def _pcall(body, args, *, grid, in_specs, out_specs, out_shape, scratch_shapes=(), sem, name, xfers=(), prefetch=()):
    out_specs, out_shape = list(out_specs), list(out_shape)
    xfers = [tuple(x) + (None,) * (3 - len(x)) for x in xfers]
    n_pre, n_in, n_out, n_x, n_scr = len(prefetch), len(in_specs), len(out_specs), len(xfers), len(scratch_shapes)
    flags = [(s, None if w is None else w[1]) for _, s, w in xfers]
    prevs = [(t, w[2]) for t, (_, _, w) in enumerate(xfers) if w is not None and w[2] is not None]
    n_b = len(prevs)
    assert not (prevs and prefetch)

    def xfer_shape(x, scatter, w):
        if w is not None:
            return jax.ShapeDtypeStruct((N_DEV, w[0]) + x.shape[1:], x.dtype)
        return jax.ShapeDtypeStruct((N_DEV,) + (x.shape[1:] if scatter else x.shape), x.dtype)

    def wrapped(*refs):
        pre, refs = refs[:n_pre], refs[n_pre:]
        ins, xin = refs[:n_in], refs[n_in:n_in + n_x]
        refs = refs[n_in + n_x + n_b:]
        outs, xout, scr = refs[:n_out], refs[n_out:n_out + n_x], refs[n_out + n_x:]
        own, sems = scr[:n_scr], scr[n_scr:]
        ids = [pl.program_id(d) for d in range(len(grid))]
        first = functools.reduce(jnp.logical_and, [i == 0 for i in ids])
        last = functools.reduce(jnp.logical_and, [i == g - 1 for i, g in zip(ids, grid)])

        @pl.when(first)
        def _():
            for t in range(n_x):
                _xfer_start(xin[t], xout[t], *sems[3 * t:3 * t + 3], scatter=flags[t][0], row_off=flags[t][1])

        body(*pre, *ins, *outs, *own)

        @pl.when(last)
        def _():
            for t in range(n_x):
                _xfer_wait(xin[t], xout[t], *sems[3 * t:3 * t + 3], scatter=flags[t][0], row_off=flags[t][1])

    grid_spec = pltpu.PrefetchScalarGridSpec(
        num_scalar_prefetch=n_pre, grid=grid, in_specs=list(in_specs) + [_ANY] * (n_x + n_b),
        out_specs=out_specs + [_ANY] * n_x, scratch_shapes=list(scratch_shapes) + list(_XFER_SEMS) * n_x)
    return pl.pallas_call(
        wrapped if xfers else body, grid_spec=grid_spec, out_shape=out_shape + [xfer_shape(*x) for x in xfers],
        input_output_aliases={n_in + n_x + b: n_out + t for b, (t, _) in enumerate(prevs)},
        compiler_params=_params(*(["arbitrary"] * len(grid) if xfers else sem)), name=name,
    )(*prefetch, *args, *[x[0] for x in xfers], *[p for _, p in prevs])


def _exchange(xfers, *, name):
    def body():
        pass

    return _pcall(body, (), grid=(1,), in_specs=[], out_specs=[], out_shape=[], sem=("arbitrary",), name=name,
                  xfers=xfers)


def _mm(a, b, *, ak="mk", bk="kn", ok="mn", add=None, norm_g=None, rms_bwd=None, out_dtype=F32, tm=1024, tn=1024,
        tk=1024, kg=1, name, xfers=()):
    sa, sb = a.shape, b.shape
    fm = fn = fk = None
    if ak == "mk":
        M, K, a_c = sa[0], sa[1], 1
    elif ak == "km":
        K, M, a_c = sa[0], sa[1], 0
    elif ak == "bmk":
        M, K, a_c, fk = sa[1], sa[0] * sa[2], 1, sa[2]
    else:
        K, M, a_c, fm = sa[1], sa[0] * sa[2], 0, sa[2]
    if bk == "kn":
        N, b_c = sb[1], 0
    elif bk == "nk":
        N, b_c = sb[0], 1
    elif bk == "bkn":
        N, b_c, fn = sb[0] * sb[2], 0, sb[2]
    elif bk == "bnk":
        N, b_c, fk = sb[1], 1, sb[2]
    elif bk == "kbn":
        N, b_c, fk = sb[2], 0, sb[1]
    else:
        N, b_c, fn = sb[0] * sb[1], 1, sb[1]
    tm, tn, tk = fm or _tile(M, tm), fn or _tile(N, tn), fk or _tile(K, tk)
    kblk = None if kg == 1 else kg
    nm, nn, nk = M // tm, N // tn, K // (tk * kg)

    a_spec = {"mk": pl.BlockSpec((tm, tk), lambda i, j, k: (i, k)),
              "km": pl.BlockSpec((tk, tm), lambda i, j, k: (k, i)),
              "bmk": pl.BlockSpec((kblk, tm, tk), lambda i, j, k: (k, i, 0)),
              "bkm": pl.BlockSpec((None, tk, tm), lambda i, j, k: (i, k, 0))}[ak]
    b_spec = {"kn": pl.BlockSpec((tk, tn), lambda i, j, k: (k, j)),
              "nk": pl.BlockSpec((tn, tk), lambda i, j, k: (j, k)),
              "bkn": pl.BlockSpec((None, tk, tn), lambda i, j, k: (j, k, 0)),
              "bnk": pl.BlockSpec((kblk, tn, tk), lambda i, j, k: (k, j, 0)),
              "kbn": pl.BlockSpec((kblk, tk, tn), lambda i, j, k: (k, 0, j)),
              "nbk": pl.BlockSpec((None, tn, tk), lambda i, j, k: (j, 0, k))}[bk]
    if ok == "mn":
        o_spec = pl.BlockSpec((tm, tn), lambda i, j, k: (i, j))
        out_shape = jax.ShapeDtypeStruct((M, N), out_dtype)
    elif ok == "bmn":
        o_spec = pl.BlockSpec((None, tm, tn), lambda i, j, k: (j, i, 0))
        out_shape = jax.ShapeDtypeStruct((nn, M, tn), out_dtype)
    else:
        o_spec = pl.BlockSpec((None, tm, tn), lambda i, j, k: (i, 0, j))
        out_shape = jax.ShapeDtypeStruct((nm, tm, N), out_dtype)
    dims = (((a_c,), (b_c,)), ((), ()))
    has_add = add is not None
    n_extra_in = int(has_add) + int(norm_g is not None) + (3 if rms_bwd is not None else 0)
    if norm_g is not None or rms_bwd is not None:
        assert ok == "mn" and tn == N

    def body(*refs):
        a_ref, b_ref = refs[0], refs[1]
        extra = list(refs[2:2 + n_extra_in])
        add_ref = extra.pop(0) if has_add else None
        ng_ref = extra.pop(0) if norm_g is not None else None
        outs = list(refs[2 + n_extra_in:])
        o_ref = outs.pop(0)
        hn_ref = outs.pop(0) if norm_g is not None else None
        dg_ref = outs.pop(0) if rms_bwd is not None else None
        if kg == 1:
            part = lax.dot_general(a_ref[...].astype(BF16), b_ref[...].astype(BF16), dims, preferred_element_type=F32)
        else:
            part = sum(lax.dot_general(a_ref[g].astype(BF16), b_ref[g].astype(BF16), dims,
                                       preferred_element_type=F32) for g in range(kg))
        if rms_bwd is not None:
            @pl.when(jnp.logical_and(pl.program_id(0) == 0, pl.program_id(2) == 0))
            def _():
                dg_ref[...] = jnp.zeros_like(dg_ref)

        def finish(r):
            if has_add:
                r = r + add_ref[...]
            if rms_bwd is not None:
                h_ref, g_ref, dres_ref = extra
                x = h_ref[...]
                rr = lax.rsqrt(jnp.mean(x * x, axis=1, keepdims=True) + EPS)
                xn = x * rr
                gdy = r * g_ref[...]
                dg_ref[...] += jnp.sum(r * xn, axis=0, keepdims=True)
                r = dres_ref[...] + rr * (gdy - xn * jnp.mean(gdy * xn, axis=1, keepdims=True))
            o_ref[...] = r.astype(out_dtype)
            if norm_g is not None:
                hn_ref[...] = (r * lax.rsqrt(jnp.mean(r * r, axis=1, keepdims=True) + EPS) * ng_ref[...]).astype(BF16)

        if nk == 1:
            finish(part)
            return
        acc = refs[-1]
        k = pl.program_id(2)

        @pl.when(k == 0)
        def _():
            acc[...] = part

        @pl.when(k > 0)
        def _():
            acc[...] += part

        @pl.when(k == nk - 1)
        def _():
            finish(acc[...])

    tile_spec = pl.BlockSpec((tm, tn), lambda i, j, k: (i, j))
    vec_spec = pl.BlockSpec((1, tn), lambda i, j, k: (0, j))
    in_specs, args = [a_spec, b_spec], [a, b]
    out_specs, out_shapes = [o_spec], [out_shape]
    if has_add:
        in_specs.append(tile_spec)
        args.append(add)
    if norm_g is not None:
        in_specs.append(vec_spec)
        args.append(norm_g)
        out_specs.append(tile_spec)
        out_shapes.append(jax.ShapeDtypeStruct((M, N), BF16))
    if rms_bwd is not None:
        in_specs += [tile_spec, vec_spec, tile_spec]
        args += list(rms_bwd)
        out_specs.append(vec_spec)
        out_shapes.append(jax.ShapeDtypeStruct((1, N), F32))
    res = _pcall(body, args, grid=(nm, nn, nk), in_specs=in_specs, out_specs=out_specs, out_shape=out_shapes,
                 scratch_shapes=[pltpu.VMEM((tm, tn), F32)] if nk > 1 else [],
                 sem=("arbitrary" if rms_bwd is not None else "parallel", "parallel", "arbitrary"), name=name,
                 xfers=xfers)
    return res if (xfers or len(out_specs) > 1) else res[0]


def _rms_fwd(h, g, *, name):
    L, D = h.shape
    tr = _tile(L, ROW_TILE, SUBLANES)

    def body(h_ref, g_ref, o_ref):
        x = h_ref[...]
        r = lax.rsqrt(jnp.mean(x * x, axis=1, keepdims=True) + EPS)
        o_ref[...] = (x * r * g_ref[...]).astype(BF16)

    return pl.pallas_call(
        body, grid=(L // tr,),
        in_specs=[pl.BlockSpec((tr, D), lambda i: (i, 0)), pl.BlockSpec((1, D), lambda i: (0, 0))],
        out_specs=pl.BlockSpec((tr, D), lambda i: (i, 0)), out_shape=jax.ShapeDtypeStruct((L, D), BF16),
        compiler_params=_params("parallel"), name=name)(h, g)


def _glu_res_rms(z, h, g, *, name):
    L, D = h.shape
    nb, cb = z.shape[1], z.shape[3]
    tr = _tile(L, ROW_TILE, SUBLANES)

    def body(z_ref, h_ref, g_ref, h1_ref, hn_ref):
        za = jnp.concatenate([z_ref[0, d] for d in range(nb)], axis=1)
        zg = jnp.concatenate([z_ref[1, d] for d in range(nb)], axis=1)
        x = h_ref[...] + za * jax.nn.sigmoid(zg)
        h1_ref[...] = x
        r = lax.rsqrt(jnp.mean(x * x, axis=1, keepdims=True) + EPS)
        hn_ref[...] = (x * r * g_ref[...]).astype(BF16)

    row = pl.BlockSpec((tr, D), lambda i: (i, 0))
    return pl.pallas_call(
        body, grid=(L // tr,),
        in_specs=[pl.BlockSpec((2, nb, tr, cb), lambda i: (0, 0, i, 0)), row, pl.BlockSpec((1, D), lambda i: (0, 0))],
        out_specs=[row, row],
        out_shape=[jax.ShapeDtypeStruct((L, D), F32), jax.ShapeDtypeStruct((L, D), BF16)],
        compiler_params=_params("parallel"), name=name)(z, h, g)


def _glu_bwd(z, dout, *, name):
    L, D = dout.shape
    nb, cb = z.shape[1], z.shape[3]
    tr = _tile(L, ROW_TILE, SUBLANES)

    def body(z_ref, d_ref, o_ref):
        for d in range(nb):
            dd = d_ref[:, d * cb:(d + 1) * cb]
            sg = jax.nn.sigmoid(z_ref[1, d])
            o_ref[0, d] = (dd * sg).astype(BF16)
            o_ref[1, d] = (dd * z_ref[0, d] * sg * (1.0 - sg)).astype(BF16)

    zs = pl.BlockSpec((2, nb, tr, cb), lambda i: (0, 0, i, 0))
    return pl.pallas_call(
        body, grid=(L // tr,), in_specs=[zs, pl.BlockSpec((tr, D), lambda i: (i, 0))], out_specs=zs,
        out_shape=jax.ShapeDtypeStruct(z.shape, BF16),
        compiler_params=_params("parallel"), name=name)(z, dout)


def _loss_head(h, g, target, *, name):
    L, D = h.shape
    tr = _tile(L, ROW_TILE, SUBLANES)

    def body(h_ref, g_ref, t_ref, loss_ref, dh_ref, dg_ref):
        @pl.when(pl.program_id(0) == 0)
        def _():
            dg_ref[...] = jnp.zeros_like(dg_ref)
            loss_ref[...] = jnp.zeros_like(loss_ref)

        x = h_ref[...]
        gg = g_ref[...]
        r = lax.rsqrt(jnp.mean(x * x, axis=1, keepdims=True) + EPS)
        xn = x * r
        err = xn * gg - t_ref[...]
        loss_ref[...] += 0.5 * jnp.sum(jnp.mean(err * err, axis=1, keepdims=True), axis=0, keepdims=True)
        dy = err * (1.0 / D)
        gdy = dy * gg
        dh_ref[...] = r * (gdy - xn * jnp.mean(gdy * xn, axis=1, keepdims=True))
        dg_ref[...] += jnp.sum(dy * xn, axis=0, keepdims=True)

    row = pl.BlockSpec((tr, D), lambda i: (i, 0))
    vec = pl.BlockSpec((1, D), lambda i: (0, 0))
    return pl.pallas_call(
        body, grid=(L // tr,), in_specs=[row, vec, row],
        out_specs=[pl.BlockSpec((1, 1), lambda i: (0, 0)), row, vec],
        out_shape=[jax.ShapeDtypeStruct((1, 1), F32), jax.ShapeDtypeStruct((L, D), F32),
                   jax.ShapeDtypeStruct((1, D), F32)],
        compiler_params=_params("arbitrary"), name=name)(h, g, target)


CONV_ROW_TILE = 256


def _sigmoid(x):
    return pl.reciprocal(1.0 + jnp.exp(-x), approx=True)


def _conv_specs(L, tr, tc):
    nrb = tr // SUBLANES
    before = lambda i: jnp.maximum(i * nrb - 1, 0)
    after = lambda i: jnp.minimum((i + 1) * nrb, L // SUBLANES - 1)
    main = pl.BlockSpec((2, None, tr, tc), lambda j, i: (0, j, i, 0))
    prev = pl.BlockSpec((2, None, SUBLANES, tc), lambda j, i: (0, j, before(i), 0))
    nxt = pl.BlockSpec((2, None, SUBLANES, tc), lambda j, i: (0, j, after(i), 0))
    cw = pl.BlockSpec((2, None, 3, tc), lambda j, i: (0, j, 0, 0))
    cb = pl.BlockSpec((2, None, 1, tc), lambda j, i: (0, j, 0, 0))
    half = pl.BlockSpec((None, tr, tc), lambda j, i: (j, i, 0))
    half_nxt = pl.BlockSpec((None, SUBLANES, tc), lambda j, i: (j, after(i), 0))
    return main, prev, nxt, cw, cb, half, half_nxt


def _conv_rows(xe, w, b):
    x1 = pltpu.roll(xe, 1, 0)
    x2 = pltpu.roll(xe, 2, 0)
    return b + x2 * w[0:1] + x1 * w[1:2] + xe * w[2:3], x1, x2


def _shift_down(x, halo, k, row):
    y = pltpu.roll(x, k, 0)
    for r in range(k):
        y = jnp.where(row == r, halo[SUBLANES - k + r:SUBLANES - k + r + 1, :], y)
    return y


def _conv_act(u0, cw, cb, *, name, xfers=()):
    _, nb, L, tc = u0.shape
    tr = _tile(L, ROW_TILE, SUBLANES)
    main, prev, _, cws, cbs, half, _ = _conv_specs(L, tr, tc)

    def body(u_ref, p_ref, w_ref, b_ref, a_ref):
        first = pl.program_id(1) == 0
        row = lax.broadcasted_iota(jnp.int32, (tr, tc), 0)
        y = []
        for s in range(2):
            x, w = u_ref[s], w_ref[s]
            halo = jnp.where(first, 0.0, p_ref[s])
            x1 = _shift_down(x, halo, 1, row)
            x2 = _shift_down(x, halo, 2, row)
            y.append(b_ref[s] + x2 * w[0:1] + x1 * w[1:2] + x * w[2:3])
        a_ref[...] = (y[0] * _sigmoid(y[0]) * y[1]).astype(BF16)

    return _pcall(body, (u0, u0, cw, cb), grid=(nb, L // tr), in_specs=[main, prev, cws, cbs], out_specs=[half],
                  out_shape=[jax.ShapeDtypeStruct((nb, L, tc), BF16)], sem=("parallel", "parallel"), name=name,
                  xfers=xfers)


def _conv_ffn_bwd(u0, cw, cb, da, *, name, xfers=()):
    _, nb, L, tc = u0.shape
    tr = _tile(L, CONV_ROW_TILE, SUBLANES)
    main, prev, nxt, cws, cbs, half, half_nxt = _conv_specs(L, tr, tc)
    nr = L // tr
    H = SUBLANES

    def body(u_ref, p_ref, n_ref, w_ref, b_ref, da_ref, dan_ref, a_ref, du0_ref, dcw_ref, dcb_ref):
        i = pl.program_id(1)

        @pl.when(i == 0)
        def _():
            dcw_ref[...] = jnp.zeros_like(dcw_ref)
            dcb_ref[...] = jnp.zeros_like(dcb_ref)

        y, x1, x2 = [], [], []
        for s in range(2):
            xe = jnp.concatenate([jnp.where(i == 0, 0.0, p_ref[s]), u_ref[s], n_ref[s]], axis=0)
            ys, x1s, x2s = _conv_rows(xe, w_ref[s], b_ref[s])
            y.append(ys[H:])
            x1.append(x1s[H:H + tr])
            x2.append(x2s[H:H + tr])
        gate, up = y
        da = jnp.concatenate([da_ref[...], dan_ref[...]], axis=0)
        row = lax.broadcasted_iota(jnp.int32, (tr + H, tc), 0)
        da = jnp.where(jnp.logical_and(i == nr - 1, row >= tr), 0.0, da)
        sg = _sigmoid(gate)
        silu = gate * sg
        a_ref[...] = (silu * up)[:tr].astype(BF16)
        d = (da * up * (sg * (1.0 + gate * (1.0 - sg))), da * silu)
        for s in range(2):
            w = w_ref[s]
            d0 = d[s][:tr]
            d1 = pltpu.roll(d[s], tr + H - 1, 0)[:tr]
            d2 = pltpu.roll(d[s], tr + H - 2, 0)[:tr]
            du0_ref[s] = (d0 * w[2:3] + d1 * w[1:2] + d2 * w[0:1]).astype(BF16)
            dcw_ref[s, 0:1, :] += jnp.sum(d0 * x2[s], axis=0, keepdims=True)
            dcw_ref[s, 1:2, :] += jnp.sum(d0 * x1[s], axis=0, keepdims=True)
            dcw_ref[s, 2:3, :] += jnp.sum(d0 * u_ref[s], axis=0, keepdims=True)
            dcb_ref[s] += jnp.sum(d0, axis=0, keepdims=True)

    return _pcall(body, (u0, u0, u0, cw, cb, da, da), grid=(nb, nr),
                  in_specs=[main, prev, nxt, cws, cbs, half, half_nxt], out_specs=[half, main, cws, cbs],
                  out_shape=[jax.ShapeDtypeStruct((nb, L, tc), BF16), jax.ShapeDtypeStruct((2, nb, L, tc), BF16),
                             jax.ShapeDtypeStruct((2, nb, 3, tc), F32), jax.ShapeDtypeStruct((2, nb, 1, tc), F32)],
                  sem=("parallel", "arbitrary"), name=name, xfers=xfers)


HALF = N_GROUPS // 2


def _swap(s):
    return jnp.concatenate([s[HALF:], s[:HALF]], axis=0)


def _chan_block(j):
    return ((j % HALF) // 4) * LANES


def _pairs_of(jb):
    return [2 * jb, 2 * jb + 1, HALF // 2 + 2 * jb, HALF // 2 + 2 * jb + 1]


_WB_SPEC = pl.BlockSpec((HALF, LANES, 2 * LANES), lambda c: (0, 0, 0))
_WC_SPEC = _WB_SPEC
GELU_C = math.sqrt(2.0 / math.pi)
GELU_A = 0.044715


def _gelu(x):
    return 0.5 * x * (1.0 + jnp.tanh(GELU_C * (x + GELU_A * x * x * x)))


def _gelu_grad(x):
    th = jnp.tanh(GELU_C * (x + GELU_A * x * x * x))
    return 0.5 * (1.0 + th) + 0.5 * x * (1.0 - th * th) * GELU_C * (1.0 + 3.0 * GELU_A * x * x)


def _tile_rows(j, T, TP):
    return pl.ds(j * TP + SUBLANES, T)


def _pair(ref, jp, T, TP):
    return jnp.concatenate([ref[_tile_rows(2 * jp, T, TP), :], ref[_tile_rows(2 * jp + 1, T, TP), :]],
                           axis=1).astype(BF16)


def _unpair(ref, jp, val, T, TP):
    ref[_tile_rows(2 * jp, T, TP), :] = val[:, :LANES]
    ref[_tile_rows(2 * jp + 1, T, TP), :] = val[:, LANES:]


def _s5_project_in(u_ref, wb_ref, s3, T, TP):
    for jp in range(HALF):
        blk = _chan_block(2 * jp)
        _unpair(s3, jp, jnp.dot(u_ref[:, blk:blk + LANES], wb_ref[jp], preferred_element_type=F32), T, TP)


def _s5_scan_fwd(s3, a1, a2, s0, T, TP):
    span = (N_GROUPS - 1) * TP + 2 * SUBLANES

    def blk(i, s):
        view = s3.at[pl.ds(pl.multiple_of(i * SUBLANES, SUBLANES), span)]
        for k in range(SUBLANES):
            rows = pl.ds(SUBLANES + k, N_GROUPS, stride=TP)
            s = a1 * s + a2 * _swap(s) + view[rows, :]
            view[rows, :] = s
        return s

    return lax.fori_loop(0, T // SUBLANES, blk, s0)


def _s5_fwd(hn, wb, wc, a1, a2, dvec, *, name, xfers=()):
    L, D = hn.shape
    T = min(S5_CHUNK, L)
    TP = T + SUBLANES
    nC = L // T

    def body(u_ref, wb_ref, wc_ref, a1_ref, a2_ref, d_ref, y_ref, yg_ref, sb_ref, s3, st):
        @pl.when(pl.program_id(0) == 0)
        def _():
            st[...] = jnp.zeros_like(st)

        sb_ref[0] = st[...]
        _s5_project_in(u_ref, wb_ref, s3, T, TP)
        st[...] = _s5_scan_fwd(s3, a1_ref[...], a2_ref[...], st[...], T, TP)
        for jb in range(D // LANES):
            acc = jnp.zeros((T, LANES), F32)
            for jp in _pairs_of(jb):
                acc += lax.dot_general(_pair(s3, jp, T, TP), wc_ref[jp], _NT, preferred_element_type=F32)
            cols = slice(jb * LANES, (jb + 1) * LANES)
            y = acc + d_ref[:, cols] * u_ref[:, cols].astype(F32)
            y_ref[:, cols] = y
            yg_ref[:, cols] = _gelu(y).astype(BF16)

    row = pl.BlockSpec((T, D), lambda c: (c, 0))
    aspec = pl.BlockSpec((N_GROUPS, LANES), lambda c: (0, 0))
    return _pcall(
        body, (hn, wb, wc, a1, a2, dvec), grid=(nC,),
        in_specs=[row, _WB_SPEC, _WC_SPEC, aspec, aspec, pl.BlockSpec((1, D), lambda c: (0, 0))],
        out_specs=[row, row, pl.BlockSpec((1, N_GROUPS, LANES), lambda c: (c, 0, 0))],
        out_shape=[jax.ShapeDtypeStruct((L, D), F32), jax.ShapeDtypeStruct((L, D), BF16),
                   jax.ShapeDtypeStruct((nC, N_GROUPS, LANES), F32)],
        scratch_shapes=[pltpu.VMEM((N_GROUPS * TP, LANES), F32), pltpu.VMEM((N_GROUPS, LANES), F32)],
        sem=("arbitrary",), name=name, xfers=xfers)


def _s5_bwd(hn, dyg, ypre, sbound, wb, wc, a1, a2, dvec, h, g, dres, *, name, xfers=()):
    L, D = hn.shape
    T = min(S5_CHUNK, L)
    TP = T + SUBLANES
    nC = L // T
    span = (N_GROUPS - 1) * TP + 2 * SUBLANES
    NT = (((1,), (1,)), ((), ()))
    TN = (((0,), (0,)), ((), ()))

    def body(u_ref, dyg_ref, yp_ref, sb_ref, wb_ref, wc_ref, a1_ref, a2_ref, d_ref, h_ref, g_ref, dres_ref,
             du_ref, dwb_ref, dwc_ref, da1_ref, da2_ref, dd_ref, dg_ref, s3, g3, gst, dy_s):
        @pl.when(pl.program_id(0) == 0)
        def _():
            gst[...] = jnp.zeros_like(gst)
            dwb_ref[...] = jnp.zeros_like(dwb_ref)
            dwc_ref[...] = jnp.zeros_like(dwc_ref)
            da1_ref[...] = jnp.zeros_like(da1_ref)
            da2_ref[...] = jnp.zeros_like(da2_ref)
            dd_ref[...] = jnp.zeros_like(dd_ref)
            dg_ref[...] = jnp.zeros_like(dg_ref)

        a1 = a1_ref[...]
        a2 = a2_ref[...]
        dy = dyg_ref[...].astype(F32) * _gelu_grad(yp_ref[...])
        dy_s[...] = dy.astype(BF16)
        dd_ref[...] += jnp.sum(dy * u_ref[...].astype(F32), axis=0, keepdims=True)
        du_ref[...] = d_ref[...] * dy

        s3[pl.ds(SUBLANES - 1, N_GROUPS, stride=TP), :] = sb_ref[0]
        _s5_project_in(u_ref, wb_ref, s3, T, TP)
        _s5_scan_fwd(s3, a1, a2, sb_ref[0], T, TP)

        for jp in range(HALF):
            blk = _chan_block(2 * jp)
            _unpair(g3, jp, jnp.dot(dy_s[:, blk:blk + LANES], wc_ref[jp], preferred_element_type=F32), T, TP)
        a2c = -a2

        def rblk(ii, carry):
            g, acc1, acc2 = carry
            t0 = pl.multiple_of((T // SUBLANES - 1 - ii) * SUBLANES, SUBLANES)
            gv = g3.at[pl.ds(t0, span)]
            sv = s3.at[pl.ds(t0, span)]
            for k in reversed(range(SUBLANES)):
                rows = pl.ds(SUBLANES + k, N_GROUPS, stride=TP)
                g = a1 * g + a2c * _swap(g) + gv[rows, :]
                gv[rows, :] = g
                sp = sv[pl.ds(SUBLANES - 1 + k, N_GROUPS, stride=TP), :]
                acc1 = acc1 + g * sp
                acc2 = acc2 + g * _swap(sp)
            return g, acc1, acc2

        zero = jnp.zeros((N_GROUPS, LANES), F32)
        g, acc1, acc2 = lax.fori_loop(0, T // SUBLANES, rblk, (gst[...], zero, zero))
        gst[...] = g
        da1_ref[...] += acc1
        da2_ref[...] += acc2

        for jb in range(D // LANES):
            cols = slice(jb * LANES, (jb + 1) * LANES)
            acc = jnp.zeros((T, LANES), F32)
            for jp in _pairs_of(jb):
                gp = _pair(g3, jp, T, TP)
                dwc_ref[jp] += lax.dot_general(dy_s[:, cols], _pair(s3, jp, T, TP), TN, preferred_element_type=F32)
                dwb_ref[jp] += lax.dot_general(u_ref[:, cols], gp, TN, preferred_element_type=F32)
                acc += lax.dot_general(gp, wb_ref[jp], NT, preferred_element_type=F32)
            du_ref[:, cols] += acc

        du = du_ref[...]
        x = h_ref[...]
        rr = lax.rsqrt(jnp.mean(x * x, axis=1, keepdims=True) + EPS)
        xn = x * rr
        gdy = du * g_ref[...]
        dg_ref[...] += jnp.sum(du * xn, axis=0, keepdims=True)
        du_ref[...] = dres_ref[...] + rr * (gdy - xn * jnp.mean(gdy * xn, axis=1, keepdims=True))

    rrow = pl.BlockSpec((T, D), lambda c: (nC - 1 - c, 0))
    aspec = pl.BlockSpec((N_GROUPS, LANES), lambda c: (0, 0))
    vec = pl.BlockSpec((1, D), lambda c: (0, 0))
    return _pcall(
        body, (hn, dyg, ypre, sbound, wb, wc, a1, a2, dvec, h, g, dres), grid=(nC,),
        in_specs=[rrow, rrow, rrow, pl.BlockSpec((1, N_GROUPS, LANES), lambda c: (nC - 1 - c, 0, 0)),
                  _WB_SPEC, _WC_SPEC, aspec, aspec, vec, rrow, vec, rrow],
        out_specs=[rrow, _WB_SPEC, _WC_SPEC, aspec, aspec, vec, vec],
        out_shape=[jax.ShapeDtypeStruct((L, D), F32),
                   jax.ShapeDtypeStruct((HALF, LANES, 2 * LANES), F32),
                   jax.ShapeDtypeStruct((HALF, LANES, 2 * LANES), F32),
                   jax.ShapeDtypeStruct((N_GROUPS, LANES), F32), jax.ShapeDtypeStruct((N_GROUPS, LANES), F32),
                   jax.ShapeDtypeStruct((1, D), F32), jax.ShapeDtypeStruct((1, D), F32)],
        scratch_shapes=[pltpu.VMEM((N_GROUPS * TP, LANES), F32), pltpu.VMEM((N_GROUPS * TP, LANES), F32),
                        pltpu.VMEM((N_GROUPS, LANES), F32), pltpu.VMEM((T, D), BF16)],
        sem=("arbitrary",), name=name, xfers=xfers)


def _s5_prep(lam_re, lam_im, log_dt, b_re, b_im, c_re, c_im):
    dt = jnp.exp(log_dt)[:, None]
    mag = jnp.exp(lam_re * dt)
    lb_re = mag * jnp.cos(lam_im * dt)
    lb_im = mag * jnp.sin(lam_im * dt)
    den = lam_re * lam_re + lam_im * lam_im
    nr = lb_re - 1.0
    fr = ((nr * lam_re + lb_im * lam_im) / den)[..., None]
    fi = ((lb_im * lam_re - nr * lam_im) / den)[..., None]
    bb_re = fr * b_re - fi * b_im
    bb_im = fr * b_im + fi * b_re
    pair = lambda a: a.reshape(HALF, 2 * SSM_STATE)
    a1 = jnp.concatenate([pair(lb_re), pair(lb_re)], axis=0)
    a2 = jnp.concatenate([-pair(lb_im), pair(lb_im)], axis=0)

    quads = N_GROUPS // 4
    rows, cols = 4 * SSM_GROUP, 4 * SSM_STATE
    diag = (jnp.arange(rows)[:, None] // SSM_GROUP == jnp.arange(cols)[None, :] // SSM_STATE).astype(F32)
    place = jnp.eye(2, dtype=F32)

    def expand(w):
        blocks = jnp.tile(w.reshape(quads, rows, SSM_STATE), (1, 1, 4)) * diag
        return jnp.einsum('kq,gkrl->gkqrl', place, blocks.reshape(quads // 2, 2, rows, cols)
                          ).reshape(quads, LANES, cols)

    wb = jnp.concatenate([expand(bb_re.transpose(0, 2, 1)), expand(bb_im.transpose(0, 2, 1))], axis=0)
    wc = jnp.concatenate([expand(c_re), expand(-c_im)], axis=0)
    return a1, a2, wb, wc


def _tri(n, upper):
    r = lax.broadcasted_iota(jnp.int32, (n, n), 0)
    c = lax.broadcasted_iota(jnp.int32, (n, n), 1)
    return ((r <= c) if upper else (r >= c)).astype(F32)


def _fgate_fwd(fl, bf, *, name):
    L, W = fl.shape
    tr = _tile(L, CUM_TILE, SUBLANES)

    def body(f_ref, b_ref, o_ref, carry):
        @pl.when(pl.program_id(0) == 0)
        def _():
            carry[...] = jnp.zeros_like(carry)

        x = f_ref[...] + b_ref[...]
        ls = jnp.minimum(x, 0.0) - jnp.log(1.0 + jnp.exp(-jnp.abs(x)))
        cum = jnp.dot(_tri(tr, False), ls, preferred_element_type=F32, precision=lax.Precision.HIGHEST) + carry[...]
        o_ref[...] = cum
        carry[...] = cum[tr - 1:tr, :]

    return pl.pallas_call(
        body, grid=(L // tr,),
        in_specs=[pl.BlockSpec((tr, W), lambda i: (i, 0)), pl.BlockSpec((1, W), lambda i: (0, 0))],
        out_specs=pl.BlockSpec((tr, W), lambda i: (i, 0)), out_shape=jax.ShapeDtypeStruct((L, W), F32),
        scratch_shapes=[pltpu.VMEM((1, W), F32)], compiler_params=_params("arbitrary"), name=name)(fl, bf)


def _fgate_bwd(fl, bf, dcum, *, name):
    L, W = fl.shape
    tr = _tile(L, CUM_TILE, SUBLANES)
    n = L // tr

    def body(f_ref, b_ref, d_ref, o_ref, db_ref, carry):
        @pl.when(pl.program_id(0) == 0)
        def _():
            carry[...] = jnp.zeros_like(carry)
            db_ref[...] = jnp.zeros_like(db_ref)

        d = d_ref[...]
        rev = jnp.dot(_tri(tr, True), d, preferred_element_type=F32, precision=lax.Precision.HIGHEST) + carry[...]
        carry[...] += jnp.sum(d, axis=0, keepdims=True)
        df = rev * jax.nn.sigmoid(-(f_ref[...] + b_ref[...]))
        o_ref[...] = df
        db_ref[...] += jnp.sum(df, axis=0, keepdims=True)

    rrow = pl.BlockSpec((tr, W), lambda i: (n - 1 - i, 0))
    vec = pl.BlockSpec((1, W), lambda i: (0, 0))
    return pl.pallas_call(
        body, grid=(n,), in_specs=[rrow, vec, rrow], out_specs=[rrow, vec],
        out_shape=[jax.ShapeDtypeStruct((L, W), F32), jax.ShapeDtypeStruct((1, W), F32)],
        scratch_shapes=[pltpu.VMEM((1, W), F32)], compiler_params=_params("arbitrary"), name=name)(fl, bf, dcum)


_NT = (((1,), (1,)), ((), ()))
_TN = (((0,), (0,)), ((), ()))
HEAD_PAIRS = N_HEADS // 2


def _causal_tiles(n):
    pairs = [(i, j) for j in range(n) for i in range(j, n)]
    return (jnp.array([p[0] for p in pairs], jnp.int32), jnp.array([p[1] for p in pairs], jnp.int32))


def _as_row(col):
    return jnp.transpose(jnp.broadcast_to(col, (col.shape[0], LANES)))[0:1, :]


def _attn_logits(qs, k, ck, masked, t):
    s = lax.dot_general(qs, k, _NT, preferred_element_type=F32) - ck
    if masked:
        r = lax.broadcasted_iota(jnp.int32, (t, t), 0)
        c = lax.broadcasted_iota(jnp.int32, (t, t), 1)
        s = jnp.where(c > r, NEG, s)
    return s


def _attn_fwd(q, kv, ck, *, name, xfers=()):
    L, D = q.shape
    t = _tile(L, ATTN_TILE)
    n = L // t
    dh = HEAD_DIM

    def body(q_ref, k_ref, v_ref, ck_ref, o_ref, o32_ref, lse_ref, m_s, l_s, acc):
        i, j = pl.program_id(1), pl.program_id(2)

        @pl.when(j == 0)
        def _():
            m_s[...] = jnp.full_like(m_s, NEG)
            l_s[...] = jnp.zeros_like(l_s)
            acc[...] = jnp.zeros_like(acc)

        def tile(masked):
            for e in range(2):
                sl = slice(e * dh, (e + 1) * dh)
                v = v_ref[:, sl]
                s = _attn_logits(q_ref[:, sl] * ATTN_SCALE, k_ref[:, sl], ck_ref[e], masked, t)
                m_new = jnp.maximum(m_s[e], jnp.max(s, axis=1, keepdims=True))
                alpha = jnp.exp(m_s[e] - m_new)
                p = jnp.exp(s - m_new)
                l_s[e] = alpha * l_s[e] + jnp.sum(p, axis=1, keepdims=True)
                p_hi = p.astype(BF16)
                p_lo = (p - p_hi.astype(F32)).astype(BF16)
                pv = (jnp.dot(p_hi, v, preferred_element_type=F32) + jnp.dot(p_lo, v, preferred_element_type=F32))
                acc[e] = alpha * acc[e] + pv
                m_s[e] = m_new

        pl.when(j < i)(functools.partial(tile, False))
        pl.when(j == i)(functools.partial(tile, True))

        @pl.when(j == n - 1)
        def _():
            for e in range(2):
                sl = slice(e * dh, (e + 1) * dh)
                o = acc[e] / l_s[e]
                o_ref[:, sl] = o.astype(BF16)
                o32_ref[:, sl] = o
                lse_ref[e] = _as_row(m_s[e] + jnp.log(l_s[e]))

    qs = pl.BlockSpec((t, LANES), lambda h, i, j: (i, h))
    ks = pl.BlockSpec((t, LANES), lambda h, i, j: (jnp.minimum(i, j), h))
    vs = pl.BlockSpec((t, LANES), lambda h, i, j: (jnp.minimum(i, j), HEAD_PAIRS + h))
    cs = pl.BlockSpec((2, 1, t), lambda h, i, j: (h, 0, jnp.minimum(i, j)))
    return _pcall(
        body, (q, kv, kv, ck), grid=(HEAD_PAIRS, n, n), in_specs=[qs, ks, vs, cs],
        out_specs=[qs, qs, pl.BlockSpec((2, 1, t), lambda h, i, j: (h, 0, i))],
        out_shape=[jax.ShapeDtypeStruct((L, D), BF16), jax.ShapeDtypeStruct((L, D), F32),
                   jax.ShapeDtypeStruct((N_HEADS, 1, L), F32)],
        scratch_shapes=[pltpu.VMEM((2, t, 1), F32), pltpu.VMEM((2, t, 1), F32), pltpu.VMEM((2, t, dh), F32)],
        sem=("parallel", "parallel", "arbitrary"), name=name, xfers=xfers)


def _attn_delta(do, o, *, name):
    L, D = do.shape
    t = _tile(L, ATTN_TILE)

    def body(do_ref, o_ref, d_ref):
        prod = do_ref[...].astype(F32) * o_ref[...]
        head = lax.broadcasted_iota(jnp.int32, (N_HEADS, D), 0)
        col = lax.broadcasted_iota(jnp.int32, (N_HEADS, D), 1)
        sel = (col // HEAD_DIM == head).astype(F32)
        d_ref[:, 0, :] = lax.dot_general(sel, prod, _NT, preferred_element_type=F32, precision=lax.Precision.HIGHEST)

    row = pl.BlockSpec((t, D), lambda i: (i, 0))
    return pl.pallas_call(
        body, grid=(L // t,), in_specs=[row, row], out_specs=pl.BlockSpec((N_HEADS, 1, t), lambda i: (0, 0, i)),
        out_shape=jax.ShapeDtypeStruct((N_HEADS, 1, L), F32), compiler_params=_params("parallel"), name=name)(do, o)


def _attn_bwd(q, kv, ck_col, do, lse_row, delta_row, *, name, xfers=()):
    L, D = q.shape
    t = _tile(L, ATTN_TILE)
    n = L // t
    dh = HEAD_DIM

    def body(i_tab, j_tab, q_ref, k_ref, v_ref, ck_ref, do_ref, lse_ref, dl_ref, dq_ref, dk_ref, dv_ref, dck_ref,
             dck_s):
        i, j = i_tab[pl.program_id(1)], j_tab[pl.program_id(1)]

        @pl.when(pl.program_id(1) == 0)
        def _():
            dq_ref[...] = jnp.zeros_like(dq_ref)

        @pl.when(i == j)
        def _():
            dk_ref[...] = jnp.zeros_like(dk_ref)
            dv_ref[...] = jnp.zeros_like(dv_ref)
            dck_s[...] = jnp.zeros_like(dck_s)

        def tile(masked):
            cols = pl.ds(pl.multiple_of(i * t, t), t)
            for e in range(2):
                sl = slice(e * dh, (e + 1) * dh)
                qs = q_ref[:, sl] * ATTN_SCALE
                k = k_ref[:, sl]
                do = do_ref[:, sl]
                st = lax.dot_general(k, qs, _NT, preferred_element_type=F32) - ck_ref[e]
                if masked:
                    kpos = lax.broadcasted_iota(jnp.int32, (t, t), 0)
                    qpos = lax.broadcasted_iota(jnp.int32, (t, t), 1)
                    st = jnp.where(kpos > qpos, NEG, st)
                pt = jnp.exp(st - lse_ref[e])
                dpt = lax.dot_general(v_ref[:, sl], do, _NT, preferred_element_type=F32)
                dst = pt * (dpt - dl_ref[e])
                dst16 = dst.astype(BF16)
                dv_ref[:, sl] += jnp.dot(pt.astype(BF16), do, preferred_element_type=F32)
                dk_ref[:, sl] += jnp.dot(dst16, qs, preferred_element_type=F32)
                dck_s[e] -= jnp.sum(dst, axis=1, keepdims=True)
                dq_ref[sl, cols] += lax.dot_general(k, dst16, _TN, preferred_element_type=F32) * ATTN_SCALE

        pl.when(i > j)(functools.partial(tile, False))
        pl.when(i == j)(functools.partial(tile, True))

        @pl.when(i == n - 1)
        def _():
            for e in range(2):
                dck_ref[e] = _as_row(dck_s[e])

    qs = pl.BlockSpec((t, LANES), lambda h, s, it, jt: (it[s], h))
    ks = pl.BlockSpec((t, LANES), lambda h, s, it, jt: (jt[s], h))
    vs = pl.BlockSpec((t, LANES), lambda h, s, it, jt: (jt[s], HEAD_PAIRS + h))
    cs = pl.BlockSpec((2, t, 1), lambda h, s, it, jt: (h, jt[s], 0))
    ls = pl.BlockSpec((2, 1, t), lambda h, s, it, jt: (h, 0, it[s]))
    full = jax.ShapeDtypeStruct((L, D), F32)
    tabs = _causal_tiles(n)
    return _pcall(
        body, (q, kv, kv, ck_col, do, lse_row, delta_row), grid=(HEAD_PAIRS, tabs[0].shape[0]),
        in_specs=[qs, ks, vs, cs, qs, ls, ls],
        out_specs=[pl.BlockSpec((LANES, L), lambda h, s, it, jt: (h, 0)), ks, ks,
                   pl.BlockSpec((2, 1, t), lambda h, s, it, jt: (h, 0, jt[s]))],
        out_shape=[jax.ShapeDtypeStruct((D, L), F32), full, full, jax.ShapeDtypeStruct((N_HEADS, 1, L), F32)],
        scratch_shapes=[pltpu.VMEM((2, t, 1), F32)],
        sem=("parallel", "arbitrary"), name=name, xfers=xfers, prefetch=tabs)


SHARD_COLS_FFN = 2 * D_FF // N_DEV
SHARD_ROWS_FFN = D_FF // N_DEV
SHARD_COLS_GLU = 2 * D_MODEL // N_DEV
SHARD_ROWS_QO = D_MODEL // N_DEV
SHARD_COLS_KVF = (2 * D_MODEL + N_HEADS) // N_DEV
S5_NAMES = ("lam_re", "lam_im", "log_dt", "ssm_b_re", "ssm_b_im", "ssm_c_re", "ssm_c_im")
REPL_LATE_ROWS = 272
REPL_EARLY_ROWS = 304


def _leaves(parts):
    out = []
    for p in parts:
        out.extend(_leaves(p) if isinstance(p, (list, tuple)) else [p.reshape(-1)])
    return out


def _whole_rows(n):
    return -(-n // D_MODEL)


def _pack_rows(parts, rows):
    blocks = [jnp.pad(p, (0, _whole_rows(p.shape[0]) * D_MODEL - p.shape[0])).reshape(-1, D_MODEL)
              for p in _leaves(parts)]
    used = sum(b.shape[0] for b in blocks)
    return jnp.concatenate(blocks + [jnp.zeros((rows - used, D_MODEL), blocks[0].dtype)], axis=0)


def _unpack_rows(packed, like):
    out, row = [], 0
    for p in _leaves(like):
        n = _whole_rows(p.shape[0])
        out.append(packed[row:row + n].reshape(-1)[:p.shape[0]])
        row += n
    return out


def _repl_late(d):
    return [d["g_mix"][0], [d[n][0] for n in S5_NAMES]]


def _repl_early(d):
    return [list(d["g_mix"][1:]), list(d["g_ffn"]), [d[n][1] for n in S5_NAMES], d["g_kv"], d["b_f"],
            list(d["ffn_conv_b"]), d["g_final"]]


def _kvf_blocks(full):
    return full.reshape(D_MODEL, N_DEV, SHARD_COLS_KVF).transpose(1, 0, 2)


class _Step:
    def __init__(self, weights, send=None, plan=None):
        self.w = dict(weights)
        self.send = send or {}
        self.grad = {}
        self.bcast = {}
        self.slots = {}
        self.plan = plan or {}

    def xfers(self, host):
        src = {"w": self.send, "g": self.grad, "b": self.bcast}
        out = []
        for kind, k in self.plan.get(host, ()):
            if kind == "w" and k[0] == "w_ffn_in" and len(k) == 3:
                parts = _W_IN_PARTS[k[1]]
                out.append((self.send[k], False, (sum(parts), sum(parts[:k[2]]), self.w.get(("w_ffn_in_parts", k[1])))))
            else:
                out.append((src[kind][k], kind == "g"))
        return out

    def land(self, host, gathered):
        for (kind, k), g in zip(self.plan.get(host, ()), gathered):
            if kind == "w":
                self.arrive(k, g)
            else:
                self.slots[k] = g

    def arrive(self, k, g):
        name = k[0]
        if name == "w_ffn_in" and len(k) == 3:
            self.w["w_ffn_in_parts", k[1]] = g
            if k[2] == len(_W_IN_PARTS[k[1]]) - 1:
                self.w[name, k[1]] = g
        elif name == "w_ffn_out":
            self.w[k] = g.reshape(4, 2 * SHARD_ROWS_FFN, D_MODEL)
        elif name in ("w_q", "w_o"):
            self.w[k] = g.reshape(D_MODEL, D_MODEL)
        elif name == "w_kvf":
            full = g.transpose(1, 0, 2).reshape(D_MODEL, N_DEV * SHARD_COLS_KVF)
            self.w["w_kv",] = full[:, :2 * D_MODEL]
            self.w["w_f",] = jnp.pad(full[:, 2 * D_MODEL:], ((0, 0), (0, LANES - N_HEADS)))
        elif name == "small":
            flat = g.reshape(N_DEV, -1)
            self.w["ssm_d",] = flat[:, :256].reshape(N_DEV, N_A, LANES).transpose(1, 0, 2).reshape(N_A, D_MODEL)
            cw = flat[:, 256:256 + DEPTH * 3 * SHARD_COLS_FFN].reshape(N_DEV, DEPTH, 3, SHARD_COLS_FFN)
            for layer in range(DEPTH):
                self.w["conv_w", layer] = cw[:, layer].reshape(2, 4, 3, SHARD_COLS_FFN)
        else:
            self.w[k] = g

    def run(self, host, fn, *args, **kw):
        xf = self.xfers(host)
        res = fn(*args, name=host, xfers=xf, **kw)
        if not xf:
            return res[0] if isinstance(res, (list, tuple)) and len(res) == 1 else res
        n_own = len(res) - len(xf)
        self.land(host, res[n_own:])
        return res[0] if n_own == 1 else res[:n_own]


def _step(x, target, S):
    L = x.shape[0]
    W = S.w
    vec = lambda a: a.reshape(1, -1)
    CF = SHARD_COLS_FFN

    h = x
    saved = []
    kvs = None
    hn_next = None
    for layer in range(DEPTH):
        t = str(layer)
        if layer < N_A:
            (a1, a2, wb, wc), prep_vjp = jax.vjp(_s5_prep, *[W[n][layer] for n in S5_NAMES])
            wb16, wc16 = wb.astype(BF16), wc.astype(BF16)
            hn = hn_next if hn_next is not None else _rms_fwd(h, vec(W["g_mix"][layer]), name="mix_norm" + t)
            dvec = vec(W["ssm_d",][layer])
            ypre, yg, sb = S.run("s5_fwd" + t, _s5_fwd, hn, wb16, wc16, a1, a2, dvec)
            z = S.run("glu_mm" + t, _mm, yg, W["w_glu", layer], bk="bkn", ok="bmn", tm=4096)
            z = z.reshape(2, 4, L, SHARD_COLS_GLU)
            h1, hn2 = _glu_res_rms(z, h, vec(W["g_ffn"][layer]), name="glu_res" + t)
            mix_saved = (h, hn, ypre, yg, sb, z, a1, a2, wb16, wc16, dvec, prep_vjp)
        else:
            j = layer - N_A
            if layer == N_A:
                hkv = _rms_fwd(h, vec(W["g_kv"]), name="kv_norm")
                kvm = S.run("kv_mm", _mm, hkv, W["w_kv",], out_dtype=BF16)
                fl = S.run("f_mm", _mm, hkv, W["w_f",])
                cum = _fgate_fwd(fl, W["b_f_pad",], name="fgate_fwd")
                ck = cum[:, :N_HEADS].T.reshape(N_HEADS, 1, L)
                kvs = (h, hkv, fl, kvm, ck)
            _, _, _, kvm, ck = kvs
            hn = hn_next if hn_next is not None else _rms_fwd(h, vec(W["g_mix"][layer]), name="mix_norm" + t)
            q = S.run("q_mm" + t, _mm, hn, W["w_q", j], out_dtype=BF16)
            o, o32, lse = S.run("attn_fwd" + t, _attn_fwd, q, kvm, ck)
            h1, hn2 = S.run("o_mm" + t, _mm, o, W["w_o", j], add=h, norm_g=vec(W["g_ffn"][layer]))
            mix_saved = (h, hn, q, o32, o, lse)
        u0 = S.run("ffn_in" + t, _mm, hn2, W["w_ffn_in", layer], bk="nbk", ok="bmn", tm=2048).reshape(2, 4, L, CF)
        a = S.run("ffn_act" + t, _conv_act, u0, W["conv_w", layer], W["conv_b", layer])
        if layer + 1 < DEPTH:
            h2, hn_next = S.run("ffn_out" + t, _mm, a, W["w_ffn_out", layer], ak="bmk", bk="kbn", add=h1, kg=4,
                                norm_g=vec(W["g_mix"][layer + 1]))
        else:
            h2 = S.run("ffn_out" + t, _mm, a, W["w_ffn_out", layer], ak="bmk", bk="kbn", add=h1, tn=512, kg=4)
        saved.append((mix_saved, h1, hn2, u0))
        h = h2

    loss, dh, dg_final = _loss_head(h, vec(W["g_final"]), target, name="loss_head")
    g = {"g_final": dg_final.reshape(-1)}
    gl = {k: [None] * DEPTH for k in ("g_mix", "g_ffn", "conv_w", "ffn_conv_b")}
    ga = {k: [None] * N_A for k in S5_NAMES + ("ssm_d",)}
    dk = dv = dck = None
    for layer in reversed(range(DEPTH)):
        t = str(layer)
        mix_saved, h1, hn2, u0 = saved[layer]
        cw, cb = W["conv_w", layer], W["conv_b", layer]
        da = S.run("ffn_da" + t, _mm, dh, W["w_ffn_out", layer], bk="nbk", ok="bmn", tm=2048)
        a, du0, dcw, dcb = S.run("ffn_conv_bwd" + t, _conv_ffn_bwd, u0, cw, cb, da)
        dw_out = S.run("ffn_dwout" + t, _mm, a, dh, ak="bkm", ok="mbn", out_dtype=BF16)
        S.grad["w_ffn_out", layer] = dw_out.reshape(N_DEV, SHARD_ROWS_FFN, D_MODEL)
        du0 = du0.reshape(N_DEV, L, CF)
        S.grad["w_ffn_in", layer] = S.run("ffn_dwin" + t, _mm, du0, hn2, ak="bkm", ok="mbn", out_dtype=BF16)
        dh1, dg = S.run("ffn_dhn" + t, _mm, du0, W["w_ffn_in", layer], ak="bmk", bk="kbn", kg=4, tm=512,
                        rms_bwd=(h1, vec(W["g_ffn"][layer]), dh))
        gl["g_ffn"][layer], gl["conv_w"][layer], gl["ffn_conv_b"][layer] = dg.reshape(-1), dcw, dcb.reshape(-1)
        if layer < N_A:
            hin, hn, ypre, yg, sb, z, a1, a2, wb16, wc16, dvec, prep_vjp = mix_saved
            dz = _glu_bwd(z, dh1, name="glu_bwd" + t).reshape(N_DEV, L, SHARD_COLS_GLU)
            S.grad["w_glu", layer] = S.run("glu_dw" + t, _mm, yg, dz, ak="km", bk="bkn", ok="bmn", out_dtype=BF16,
                                           tk=4096)
            dyg = S.run("glu_dy" + t, _mm, dz, W["w_glu", layer], ak="bmk", bk="bnk", out_dtype=BF16, tm=2048, kg=8)
            if layer == 0:
                S.bcast["repl_early",] = _pack_rows(_repl_early({**g, **gl, **ga}), REPL_EARLY_ROWS).astype(BF16)
            dh, dwb, dwc, da1, da2, dd, dg = S.run("s5_bwd" + t, _s5_bwd, hn, dyg, ypre, sb, wb16, wc16, a1, a2, dvec,
                                                   hin, vec(W["g_mix"][layer]), dh1)
            for nme, val in zip(S5_NAMES, prep_vjp((da1, da2, dwb, dwc))):
                ga[nme][layer] = val
            ga["ssm_d"][layer] = dd.reshape(-1)
        else:
            j = layer - N_A
            hin, hn, q, o32, o, lse = mix_saved
            _, _, _, kvm, ck = kvs
            S.grad["w_o", j] = S.run("o_dw" + t, _mm, o, dh1, ak="km", out_dtype=BF16
                                     ).reshape(N_DEV, SHARD_ROWS_QO, D_MODEL)
            do = S.run("o_dx" + t, _mm, dh1, W["w_o", j], bk="nk", out_dtype=BF16)
            delta = _attn_delta(do, o32, name="attn_delta" + t)
            dq_t, dk_l, dv_l, dck_l = S.run("attn_bwd" + t, _attn_bwd, q, kvm, ck.reshape(N_HEADS, L, 1), do,
                                            lse, delta)
            dk = dk_l if dk is None else dk + dk_l
            dv = dv_l if dv is None else dv + dv_l
            dck = dck_l if dck is None else dck + dck_l
            S.grad["w_q", j] = S.run("q_dw" + t, _mm, hn, dq_t, ak="km", bk="nk", out_dtype=BF16
                                     ).reshape(N_DEV, SHARD_ROWS_QO, D_MODEL)
            dh, dg = S.run("q_dx" + t, _mm, dq_t, W["w_q", j], ak="km", bk="nk",
                           rms_bwd=(hin, vec(W["g_mix"][layer]), dh1))
            if layer == N_A:
                hkv_in, hkv, fl, _, _ = kvs
                dcum = jnp.pad(dck.reshape(N_HEADS, L).T, ((0, 0), (0, LANES - N_HEADS)))
                dfl, dbf = _fgate_bwd(fl, W["b_f_pad",], dcum, name="fgate_bwd")
                dkv = jnp.concatenate([dk, dv], axis=1).astype(BF16)
                dfl16 = dfl.astype(BF16)
                dw_kv = S.run("kv_dw", _mm, hkv, dkv, ak="km")
                dw_f = S.run("f_dw", _mm, hkv, dfl16, ak="km")
                S.grad["w_kvf",] = _kvf_blocks(jnp.concatenate([dw_kv, dw_f[:, :N_HEADS]], axis=1)).astype(BF16)
                dhkv = S.run("kv_dx", _mm, dkv, W["w_kv",], bk="nk")
                dh, dgkv = S.run("f_dx", _mm, dfl16, W["w_f",], bk="nk", add=dhkv,
                                 rms_bwd=(hkv_in, vec(W["g_kv"]), dh))
                g["b_f"] = dbf[0, :N_HEADS]
                g["g_kv"] = dgkv.reshape(-1)
        gl["g_mix"][layer] = dg.reshape(-1)

    for d in (gl, ga):
        for k, v in d.items():
            g[k] = jnp.stack(v)
    return loss, dh, g


def _adamw_layers(slots, w, m, v, *, name, xfers=()):
    shape = w.shape
    nl = len(slots)
    w, m, v = (a.reshape((nl,) + a.shape[-2:]) for a in (w, m, v))
    _, R, C = w.shape
    tr = _tile(R, 256, 16)
    c1 = 1.0 / (1.0 - ADAM_B1 ** ADAM_STEP)
    c2 = 1.0 / (1.0 - ADAM_B2 ** ADAM_STEP)

    def body(*refs):
        s_refs, (w_ref, m_ref, v_ref), (g_ref, d_ref, nm_ref, nv_ref) = refs[:nl], refs[nl:nl + 3], refs[nl + 3:]
        for layer in range(nl):
            @pl.when(pl.program_id(0) == layer)
            def _(s_ref=s_refs[layer]):
                g = s_ref[0].astype(F32)
                for d in range(1, N_DEV):
                    g = g + s_ref[d].astype(F32)
                m2 = ADAM_B1 * m_ref[...] + (1.0 - ADAM_B1) * g
                v2 = ADAM_B2 * v_ref[...] + (1.0 - ADAM_B2) * (g * g)
                g_ref[...] = g
                nm_ref[...] = m2
                nv_ref[...] = v2
                d_ref[...] = -ADAM_LR * ((m2 * c1) / (jnp.sqrt(v2 * c2) + ADAM_EPS) + ADAM_WD * w_ref[...])

    def slab_spec(layer):
        return pl.BlockSpec((N_DEV, tr, C), lambda l, i: (0, jnp.where(l == layer, i, 0), 0))

    row = pl.BlockSpec((None, tr, C), lambda l, i: (l, i, 0))
    out = jax.ShapeDtypeStruct((nl, R, C), F32)
    res = _pcall(body, (*slots, w, m, v), grid=(nl, R // tr),
                 in_specs=[slab_spec(layer) for layer in range(nl)] + [row, row, row],
                 out_specs=[row, row, row, row], out_shape=[out, out, out, out], sem=("arbitrary", "arbitrary"),
                 name=name, xfers=xfers)
    return [o.reshape(shape) for o in res[:4]] + list(res[4:])


_SMALL_ROWS = 72
_ORDER = ("g_mix", "g_ffn", "lam_re", "lam_im", "log_dt", "ssm_b_re", "ssm_b_im", "ssm_c_re", "ssm_c_im", "ssm_d",
          "w_glu", "g_kv", "w_kvf", "b_f", "w_q", "w_o", "w_ffn_in", "ffn_conv_w", "ffn_conv_b", "w_ffn_out", "g_final")


def _pack_small(ssm_d, conv_w):
    flat = jnp.concatenate([ssm_d.reshape(-1), conv_w.reshape(-1)])
    return jnp.pad(flat, (0, _SMALL_ROWS * LANES - flat.shape[0])).reshape(_SMALL_ROWS, LANES)


def _unpack_small(flat):
    flat = flat.reshape(-1)
    return flat[:256].reshape(2, 128), flat[256:256 + 8448].reshape(4, 3, 704)


_W_IN_PARTS = {0: (352, 352), 1: (176, 528)}
_FWD_PLAN = {
    "start": [("small",)],
    "s5_fwd0": [("w_glu", 0), ("w_ffn_in", 0, 0)],
    "glu_mm0": [("w_ffn_in", 0, 1)],
    "ffn_in0": [("w_ffn_out", 0)],
    "ffn_act0": [("w_glu", 1)],
    "ffn_out0": [("w_ffn_in", 1, 0)],
    "s5_fwd1": [("w_ffn_in", 1, 1)],
    "ffn_in1": [("w_ffn_out", 1), ("w_o", 0)],
    "ffn_act1": [("w_kvf",)],
    "ffn_out1": [("w_q", 0)],
    "attn_fwd2": [("w_ffn_in", 2), ("w_ffn_out", 2)],
    "ffn_act2": [("w_q", 1), ("w_o", 1)],
    "attn_fwd3": [("w_ffn_in", 3)],
    "ffn_in3": [("w_ffn_out", 3)],
}
_BWD_PLAN = {
    "ffn_dhn3": [("w_ffn_out", 3)],
    "attn_bwd3": [("w_ffn_in", 3), ("w_o", 1)],
    "ffn_conv_bwd2": [("w_q", 1)],
    "ffn_dhn2": [("w_ffn_out", 2)],
    "attn_bwd2": [("w_ffn_in", 2), ("w_o", 0)],
    "ffn_conv_bwd1": [("w_q", 0), ("w_kvf",)],
    "ffn_dhn1": [("w_ffn_out", 1)],
    "s5_bwd1": [("w_ffn_in", 1), ("w_glu", 1)],
    "ffn_dhn0": [("w_ffn_out", 0)],
    "s5_bwd0": [("w_ffn_in", 0), ("w_glu", 0), ("repl_early",)],
    "adamw_w_ffn_in": [("small",), ("repl_late",)],
}
_PLAN = {h: [("w", k) for k in ks] for h, ks in _FWD_PLAN.items()}
_PLAN.update({h: [("b" if k[0].startswith("repl") else "g", k) for k in ks] for h, ks in _BWD_PLAN.items()})


def kernel(x, g_mix, g_ffn, lam_re, lam_im, log_dt, ssm_b_re, ssm_b_im, ssm_c_re, ssm_c_im, ssm_d, w_glu, g_kv, w_kvf, b_f, w_q, w_o, w_ffn_in, ffn_conv_w, ffn_conv_b, w_ffn_out, g_final, loss_target, m_g_mix, m_g_ffn, m_lam_re, m_lam_im, m_log_dt, m_ssm_b_re, m_ssm_b_im, m_ssm_c_re, m_ssm_c_im, m_ssm_d, m_w_glu, m_g_kv, m_w_kvf, m_b_f, m_w_q, m_w_o, m_w_ffn_in, m_ffn_conv_w, m_ffn_conv_b, m_w_ffn_out, m_g_final, v_g_mix, v_g_ffn, v_lam_re, v_lam_im, v_log_dt, v_ssm_b_re, v_ssm_b_im, v_ssm_c_re, v_ssm_c_im, v_ssm_d, v_w_glu, v_g_kv, v_w_kvf, v_b_f, v_w_q, v_w_o, v_w_ffn_in, v_ffn_conv_w, v_ffn_conv_b, v_w_ffn_out, v_g_final):
    wts = dict(g_mix=g_mix, g_ffn=g_ffn, lam_re=lam_re, lam_im=lam_im, log_dt=log_dt, ssm_b_re=ssm_b_re,
               ssm_b_im=ssm_b_im, ssm_c_re=ssm_c_re, ssm_c_im=ssm_c_im, ssm_d=ssm_d, w_glu=w_glu, g_kv=g_kv,
               w_kvf=w_kvf, b_f=b_f, w_q=w_q, w_o=w_o, w_ffn_in=w_ffn_in, ffn_conv_w=ffn_conv_w,
               ffn_conv_b=ffn_conv_b, w_ffn_out=w_ffn_out, g_final=g_final)
    mom = dict(g_mix=m_g_mix, g_ffn=m_g_ffn, lam_re=m_lam_re, lam_im=m_lam_im, log_dt=m_log_dt, ssm_b_re=m_ssm_b_re,
               ssm_b_im=m_ssm_b_im, ssm_c_re=m_ssm_c_re, ssm_c_im=m_ssm_c_im, ssm_d=m_ssm_d, w_glu=m_w_glu,
               g_kv=m_g_kv, w_kvf=m_w_kvf, b_f=m_b_f, w_q=m_w_q, w_o=m_w_o, w_ffn_in=m_w_ffn_in,
               ffn_conv_w=m_ffn_conv_w, ffn_conv_b=m_ffn_conv_b, w_ffn_out=m_w_ffn_out, g_final=m_g_final)
    var = dict(g_mix=v_g_mix, g_ffn=v_g_ffn, lam_re=v_lam_re, lam_im=v_lam_im, log_dt=v_log_dt, ssm_b_re=v_ssm_b_re,
               ssm_b_im=v_ssm_b_im, ssm_c_re=v_ssm_c_re, ssm_c_im=v_ssm_c_im, ssm_d=v_ssm_d, w_glu=v_w_glu,
               g_kv=v_g_kv, w_kvf=v_w_kvf, b_f=v_b_f, w_q=v_w_q, w_o=v_w_o, w_ffn_in=v_w_ffn_in,
               ffn_conv_w=v_ffn_conv_w, ffn_conv_b=v_ffn_conv_b, w_ffn_out=v_w_ffn_out, g_final=v_g_final)
    kinds = ("grad", "delta", "m", "v")

    ready = {n: wts[n] for n in ("g_mix", "g_ffn", "g_kv", "g_final") + S5_NAMES}
    ready["b_f_pad",] = jnp.pad(b_f, (0, LANES - N_HEADS)).reshape(1, LANES)
    send = {("small",): _pack_small(ssm_d, ffn_conv_w), ("w_kvf",): w_kvf.astype(BF16)}
    for layer in range(DEPTH):
        ready["conv_b", layer] = ffn_conv_b[layer].reshape(2, 4, 1, SHARD_COLS_FFN)
        w_in_t = jnp.swapaxes(w_ffn_in[layer], 0, 1).astype(BF16)
        if layer in _W_IN_PARTS:
            row = 0
            for p, rows in enumerate(_W_IN_PARTS[layer]):
                send["w_ffn_in", layer, p] = w_in_t[row:row + rows]
                row += rows
        else:
            send["w_ffn_in", layer] = w_in_t
        send["w_ffn_out", layer] = w_ffn_out[layer].astype(BF16)
    for layer in range(N_A):
        send["w_glu", layer] = w_glu[layer].astype(BF16)
        send["w_q", layer] = w_q[layer].astype(BF16)
        send["w_o", layer] = w_o[layer].astype(BF16)

    S = _Step(ready, send, _PLAN)
    S.land("start", _exchange(S.xfers("start"), name="start"))
    loss, dx, g = _step(x[0], loss_target[0], S)
    loss = lax.psum(loss[0, 0], MESH_AXES)

    g_d = g["ssm_d"].reshape(N_A, N_DEV, LANES).transpose(1, 0, 2).reshape(N_DEV, N_A * LANES)
    g_cw = jnp.stack([g["conv_w"][layer].reshape(N_DEV, 3, SHARD_COLS_FFN) for layer in range(DEPTH)], axis=1)
    g_small = jnp.concatenate([g_d, g_cw.reshape(N_DEV, -1)], axis=1)
    g_small = jnp.pad(g_small, ((0, 0), (0, _SMALL_ROWS * LANES - g_small.shape[1])))
    S.grad["small",] = g_small.reshape(N_DEV, _SMALL_ROWS, LANES)
    S.bcast["repl_late",] = _pack_rows(_repl_late(g), REPL_LATE_ROWS).astype(BF16)

    res = {}
    for name, nl in (("w_ffn_in", DEPTH), ("w_glu", N_A), ("w_q", DEPTH - N_A), ("w_o", DEPTH - N_A),
                     ("w_ffn_out", DEPTH)):
        view = (lambda a: jnp.swapaxes(a, 1, 2)) if name == "w_ffn_in" else (lambda a: a)
        outs = S.run("adamw_" + name, _adamw_layers, [S.slots[name, layer] for layer in range(nl)], view(wts[name]),
                     view(mom[name]), view(var[name]))
        res.update({(kind, name): view(a) for kind, a in zip(kinds, outs)})
    outs = _adamw_layers([S.slots["w_kvf",]], w_kvf, m_w_kvf, v_w_kvf, name="adamw_w_kvf")
    res.update({(kind, "w_kvf"): a for kind, a in zip(kinds, outs)})
    outs = _adamw_layers([S.slots["small",]], _pack_small(ssm_d, ffn_conv_w), _pack_small(m_ssm_d, m_ffn_conv_w),
                         _pack_small(v_ssm_d, v_ffn_conv_w), name="adamw_small")
    for kind, flat in zip(kinds, outs):
        res[kind, "ssm_d"], res[kind, "ffn_conv_w"] = _unpack_small(flat)

    pieces = {}
    for key, rows, sel in ((("repl_early",), REPL_EARLY_ROWS, _repl_early), (("repl_late",), REPL_LATE_ROWS, _repl_late)):
        outs = _adamw_layers([S.slots[key]], *[_pack_rows(sel(d), rows) for d in (wts, mom, var)],
                             name="adamw_" + key[0])
        for kind, flat in zip(kinds, outs):
            pieces[kind, key[0]] = _unpack_rows(flat, sel(wts))
    for kind in kinds:
        early, late = iter(pieces[kind, "repl_early"]), iter(pieces[kind, "repl_late"])
        take = lambda it, n: [next(it) for _ in range(n)]
        res[kind, "g_mix"] = jnp.stack(take(late, 1) + take(early, DEPTH - 1))
        res[kind, "g_ffn"] = jnp.stack(take(early, DEPTH))
        for n in S5_NAMES:
            res[kind, n] = jnp.stack([next(late), next(early)]).reshape(wts[n].shape)
        res[kind, "g_kv"], res[kind, "b_f"] = next(early), next(early)
        res[kind, "ffn_conv_b"] = jnp.stack(take(early, DEPTH))
        res[kind, "g_final"] = next(early)

    return (loss, dx[None], *[res[kind, n] for kind in kinds for n in _ORDER])
```
